```python
import jax, jax.numpy as jnp
from jax import lax
import numpy as np

D_MODEL = 1024
BATCH = 8
SEQ = 4096
DEPTH = 4

MEM_LEN = 256
HEAD_DIM = 64
N_SB_HEADS = 8
N_FOX_HEADS = 8
N_MEM_HEADS = 4
MEM_HEAD_DIM = 128
SB_W = N_SB_HEADS * HEAD_DIM
FOX_W = N_FOX_HEADS * HEAD_DIM
MEM_W = N_MEM_HEADS * MEM_HEAD_DIM
N_BRANCH = 3
IN_W = 3 * SB_W + 3 * FOX_W + N_FOX_HEADS + MEM_W
D_FF = ((8 * D_MODEL // 3 + 127) // 128) * 128
Q_BLOCK = 128
RMS_EPS = 1e-6

kernel_name = 'hybrid_sb_fox_mem_macaron'


def _rmsnorm(t, g):
    t32 = t.astype(jnp.float32)
    t32 = t32 * lax.rsqrt(jnp.mean(t32 * t32, axis=-1, keepdims=True) + RMS_EPS)
    return t32.astype(t.dtype) * g


def _swiglu(t, w_gate, w_up, w_down):
    return (jax.nn.silu(t @ w_gate) * (t @ w_up)) @ w_down


def _split_heads(t, n_heads):
    b, s, _ = t.shape
    return t.reshape(b, s, n_heads, -1).transpose(0, 2, 1, 3)


def _merge_heads(t):
    b, h, s, d = t.shape
    return t.transpose(0, 2, 1, 3).reshape(b, s, h * d)


def _query_blocks(t):
    b, h, s = t.shape[:3]
    t = t.reshape((b, h, s // Q_BLOCK, Q_BLOCK) + t.shape[3:])
    return jnp.moveaxis(t, 2, 0)


def _unblock(o):
    nb, b, h, blk, d = o.shape
    return jnp.moveaxis(o, 0, 2).reshape(b, h, nb * blk, d)


def _stick_breaking_attention(q, k, v):
    b, h, s_len, d = q.shape
    scale = d ** -0.5
    key_pos = jnp.arange(s_len)

    def block(args):
        qb, i = args
        z = jnp.einsum('bhqd,bhkd->bhqk', qb, k).astype(jnp.float32) * scale
        q_pos = i * Q_BLOCK + jnp.arange(Q_BLOCK)
        mask = key_pos[None, :] < q_pos[:, None]
        log_beta = jax.nn.log_sigmoid(z)
        log_not = jnp.where(mask, log_beta - z, 0.0)
        log_between = lax.cumsum(log_not, axis=3, reverse=True) - log_not
        w = jnp.where(mask, jnp.exp(log_beta + log_between), 0.0)
        return jnp.einsum('bhqk,bhkd->bhqd', w.astype(v.dtype), v)

    out = lax.map(block, (_query_blocks(q), jnp.arange(s_len // Q_BLOCK)))
    return _unblock(out)


def _forgetting_attention(q, k, v, log_f):
    b, h, s_len, d = q.shape
    scale = d ** -0.5
    key_pos = jnp.arange(s_len)
    c = lax.cumsum(log_f.astype(jnp.float32), axis=2)
    neg = jnp.finfo(jnp.float32).min

    def block(args):
        qb, cb, i = args
        z = jnp.einsum('bhqd,bhkd->bhqk', qb, k).astype(jnp.float32) * scale
        z = z + cb[..., :, None] - c[..., None, :]
        q_pos = i * Q_BLOCK + jnp.arange(Q_BLOCK)
        mask = key_pos[None, :] <= q_pos[:, None]
        p = jax.nn.softmax(jnp.where(mask, z, neg), axis=-1)
        return jnp.einsum('bhqk,bhkd->bhqd', p.astype(v.dtype), v)

    out = lax.map(block, (_query_blocks(q), _query_blocks(c), jnp.arange(s_len // Q_BLOCK)))
    return _unblock(out)


def _memory_attention(q, k, v):
    z = jnp.einsum('bhqd,bhkd->bhqk', q, k).astype(jnp.float32) * (q.shape[-1] ** -0.5)
    p = jax.nn.softmax(z, axis=-1)
    return jnp.einsum('bhqk,bhkd->bhqd', p.astype(v.dtype), v)


def _fwd_setup_inputs(seed: int = 0) -> dict:
    key = jax.random.key(seed)
    ks = jax.random.split(key, 32)
    L = DEPTH

    def w(k, shape, fan_in):
        return jax.random.normal(k, shape, jnp.float32) * (fan_in ** -0.5)

    def gain(k, shape):
        return 1.0 + 0.05 * jax.random.normal(k, shape, jnp.float32)

    return {
        'x': jax.random.normal(ks[0], (BATCH, SEQ, D_MODEL), jnp.float32),
        'mem': jax.random.normal(ks[1], (BATCH, MEM_LEN, D_MODEL), jnp.float32),
        'ffn1_pre_g': gain(ks[2], (L, D_MODEL)),
        'ffn1_post_g': gain(ks[3], (L, D_MODEL)),
        'ffn1_w_gate': w(ks[4], (L, D_MODEL, D_FF), D_MODEL),
        'ffn1_w_up': w(ks[5], (L, D_MODEL, D_FF), D_MODEL),
        'ffn1_w_down': w(ks[6], (L, D_FF, D_MODEL), D_FF),
        'mix_pre_g': gain(ks[7], (L, D_MODEL)),
        'mix_post_g': gain(ks[8], (L, D_MODEL)),
        'w_in': w(ks[9], (L, D_MODEL, IN_W), D_MODEL),
        'b_forget': 2.0 + 0.5 * jax.random.normal(ks[10], (L, N_FOX_HEADS), jnp.float32),
        'mem_norm_g': gain(ks[11], (D_MODEL,)),
        'w_mem_kv': w(ks[12], (L, D_MODEL, 2 * MEM_W), D_MODEL),
        'w_gate': w(ks[13], (L, D_MODEL, N_BRANCH * D_MODEL), D_MODEL),
        'b_gate': 0.02 * jax.random.normal(ks[14], (L, N_BRANCH * D_MODEL), jnp.float32),
        'w_br_sb': w(ks[15], (L, SB_W, D_MODEL), SB_W),
        'w_br_fox': w(ks[16], (L, FOX_W, D_MODEL), FOX_W),
        'w_br_mem': w(ks[17], (L, MEM_W, D_MODEL), MEM_W),
        'w_out': w(ks[18], (L, D_MODEL, D_MODEL), D_MODEL),
        'ffn2_pre_g': gain(ks[19], (L, D_MODEL)),
        'ffn2_post_g': gain(ks[20], (L, D_MODEL)),
        'ffn2_w_gate': w(ks[21], (L, D_MODEL, D_FF), D_MODEL),
        'ffn2_w_up': w(ks[22], (L, D_MODEL, D_FF), D_MODEL),
        'ffn2_w_down': w(ks[23], (L, D_FF, D_MODEL), D_FF),
    }


def _fwd_reference(x, mem, ffn1_pre_g, ffn1_post_g, ffn1_w_gate, ffn1_w_up, ffn1_w_down,
              mix_pre_g, mix_post_g, w_in, b_forget, mem_norm_g, w_mem_kv, w_gate, b_gate,
              w_br_sb, w_br_fox, w_br_mem, w_out,
              ffn2_pre_g, ffn2_post_g, ffn2_w_gate, ffn2_w_up, ffn2_w_down):
    mem_n = _rmsnorm(mem, mem_norm_g)
    split_at = np.cumsum([SB_W, SB_W, SB_W, FOX_W, FOX_W, FOX_W, N_FOX_HEADS])
    h = x
    for l in range(DEPTH):
        f = _swiglu(_rmsnorm(h, ffn1_pre_g[l]), ffn1_w_gate[l], ffn1_w_up[l], ffn1_w_down[l])
        h = h + 0.5 * _rmsnorm(f, ffn1_post_g[l])

        u = _rmsnorm(h, mix_pre_g[l])
        proj = u @ w_in[l]
        q_sb, k_sb, v_sb, q_fx, k_fx, v_fx, f_logit, q_mem = jnp.split(proj, split_at, axis=-1)

        o_sb = _stick_breaking_attention(_split_heads(q_sb, N_SB_HEADS), _split_heads(k_sb, N_SB_HEADS),
                                         _split_heads(v_sb, N_SB_HEADS))
        log_f = jax.nn.log_sigmoid((f_logit + b_forget[l]).astype(jnp.float32)).transpose(0, 2, 1)
        o_fx = _forgetting_attention(_split_heads(q_fx, N_FOX_HEADS), _split_heads(k_fx, N_FOX_HEADS),
                                     _split_heads(v_fx, N_FOX_HEADS), log_f)
        k_mem, v_mem = jnp.split(mem_n @ w_mem_kv[l], 2, axis=-1)
        o_mem = _memory_attention(_split_heads(q_mem, N_MEM_HEADS), _split_heads(k_mem, N_MEM_HEADS),
                                  _split_heads(v_mem, N_MEM_HEADS))

        g_sb, g_fx, g_mem = jnp.split(jax.nn.sigmoid(u @ w_gate[l] + b_gate[l]), N_BRANCH, axis=-1)
        merged = (g_sb * (_merge_heads(o_sb) @ w_br_sb[l])
                  + g_fx * (_merge_heads(o_fx) @ w_br_fox[l])
                  + g_mem * (_merge_heads(o_mem) @ w_br_mem[l]))
        h = h + _rmsnorm(merged @ w_out[l], mix_post_g[l])

        f = _swiglu(_rmsnorm(h, ffn2_pre_g[l]), ffn2_w_gate[l], ffn2_w_up[l], ffn2_w_down[l])
        h = h + 0.5 * _rmsnorm(f, ffn2_post_g[l])
    return h


import jax as _jax
import jax.numpy as _jnp

TWIN_FORMAT = 'train_step'
FWD_PARAMS = ['x', 'mem', 'ffn1_pre_g', 'ffn1_post_g', 'ffn1_w_gate', 'ffn1_w_up', 'ffn1_w_down', 'mix_pre_g', 'mix_post_g', 'w_in', 'b_forget', 'mem_norm_g', 'w_mem_kv', 'w_gate', 'b_gate', 'w_br_sb', 'w_br_fox', 'w_br_mem', 'w_out', 'ffn2_pre_g', 'ffn2_post_g', 'ffn2_w_gate', 'ffn2_w_up', 'ffn2_w_down']
TWIN_WEIGHTS = ['ffn1_pre_g', 'ffn1_post_g', 'ffn1_w_gate', 'ffn1_w_up', 'ffn1_w_down', 'mix_pre_g', 'mix_post_g', 'w_in', 'b_forget', 'mem_norm_g', 'w_mem_kv', 'w_gate', 'b_gate', 'w_br_sb', 'w_br_fox', 'w_br_mem', 'w_out', 'ffn2_pre_g', 'ffn2_post_g', 'ffn2_w_gate', 'ffn2_w_up', 'ffn2_w_down']
TWIN_DIFF_INPUT = 'x'
TWIN_INPUTS = ['x', 'mem', 'ffn1_pre_g', 'ffn1_post_g', 'ffn1_w_gate', 'ffn1_w_up', 'ffn1_w_down', 'mix_pre_g', 'mix_post_g', 'w_in', 'b_forget', 'mem_norm_g', 'w_mem_kv', 'w_gate', 'b_gate', 'w_br_sb', 'w_br_fox', 'w_br_mem', 'w_out', 'ffn2_pre_g', 'ffn2_post_g', 'ffn2_w_gate', 'ffn2_w_up', 'ffn2_w_down', 'loss_target', 'm_ffn1_pre_g', 'm_ffn1_post_g', 'm_ffn1_w_gate', 'm_ffn1_w_up', 'm_ffn1_w_down', 'm_mix_pre_g', 'm_mix_post_g', 'm_w_in', 'm_b_forget', 'm_mem_norm_g', 'm_w_mem_kv', 'm_w_gate', 'm_b_gate', 'm_w_br_sb', 'm_w_br_fox', 'm_w_br_mem', 'm_w_out', 'm_ffn2_pre_g', 'm_ffn2_post_g', 'm_ffn2_w_gate', 'm_ffn2_w_up', 'm_ffn2_w_down', 'v_ffn1_pre_g', 'v_ffn1_post_g', 'v_ffn1_w_gate', 'v_ffn1_w_up', 'v_ffn1_w_down', 'v_mix_pre_g', 'v_mix_post_g', 'v_w_in', 'v_b_forget', 'v_mem_norm_g', 'v_w_mem_kv', 'v_w_gate', 'v_b_gate', 'v_w_br_sb', 'v_w_br_fox', 'v_w_br_mem', 'v_w_out', 'v_ffn2_pre_g', 'v_ffn2_post_g', 'v_ffn2_w_gate', 'v_ffn2_w_up', 'v_ffn2_w_down']
TWIN_OUTPUTS = ['loss', 'grad_x', 'grad_ffn1_pre_g', 'grad_ffn1_post_g', 'grad_ffn1_w_gate', 'grad_ffn1_w_up', 'grad_ffn1_w_down', 'grad_mix_pre_g', 'grad_mix_post_g', 'grad_w_in', 'grad_b_forget', 'grad_mem_norm_g', 'grad_w_mem_kv', 'grad_w_gate', 'grad_b_gate', 'grad_w_br_sb', 'grad_w_br_fox', 'grad_w_br_mem', 'grad_w_out', 'grad_ffn2_pre_g', 'grad_ffn2_post_g', 'grad_ffn2_w_gate', 'grad_ffn2_w_up', 'grad_ffn2_w_down', 'delta_ffn1_pre_g', 'delta_ffn1_post_g', 'delta_ffn1_w_gate', 'delta_ffn1_w_up', 'delta_ffn1_w_down', 'delta_mix_pre_g', 'delta_mix_post_g', 'delta_w_in', 'delta_b_forget', 'delta_mem_norm_g', 'delta_w_mem_kv', 'delta_w_gate', 'delta_b_gate', 'delta_w_br_sb', 'delta_w_br_fox', 'delta_w_br_mem', 'delta_w_out', 'delta_ffn2_pre_g', 'delta_ffn2_post_g', 'delta_ffn2_w_gate', 'delta_ffn2_w_up', 'delta_ffn2_w_down', 'new_m_ffn1_pre_g', 'new_m_ffn1_post_g', 'new_m_ffn1_w_gate', 'new_m_ffn1_w_up', 'new_m_ffn1_w_down', 'new_m_mix_pre_g', 'new_m_mix_post_g', 'new_m_w_in', 'new_m_b_forget', 'new_m_mem_norm_g', 'new_m_w_mem_kv', 'new_m_w_gate', 'new_m_b_gate', 'new_m_w_br_sb', 'new_m_w_br_fox', 'new_m_w_br_mem', 'new_m_w_out', 'new_m_ffn2_pre_g', 'new_m_ffn2_post_g', 'new_m_ffn2_w_gate', 'new_m_ffn2_w_up', 'new_m_ffn2_w_down', 'new_v_ffn1_pre_g', 'new_v_ffn1_post_g', 'new_v_ffn1_w_gate', 'new_v_ffn1_w_up', 'new_v_ffn1_w_down', 'new_v_mix_pre_g', 'new_v_mix_post_g', 'new_v_w_in', 'new_v_b_forget', 'new_v_mem_norm_g', 'new_v_w_mem_kv', 'new_v_w_gate', 'new_v_b_gate', 'new_v_w_br_sb', 'new_v_w_br_fox', 'new_v_w_br_mem', 'new_v_w_out', 'new_v_ffn2_pre_g', 'new_v_ffn2_post_g', 'new_v_ffn2_w_gate', 'new_v_ffn2_w_up', 'new_v_ffn2_w_down']
TWIN_LEAF_KINDS = {'loss': 'loss', 'grad_x': 'grad_x', 'grad_ffn1_pre_g': 'grad_w', 'grad_ffn1_post_g': 'grad_w', 'grad_ffn1_w_gate': 'grad_w', 'grad_ffn1_w_up': 'grad_w', 'grad_ffn1_w_down': 'grad_w', 'grad_mix_pre_g': 'grad_w', 'grad_mix_post_g': 'grad_w', 'grad_w_in': 'grad_w', 'grad_b_forget': 'grad_w', 'grad_mem_norm_g': 'grad_w', 'grad_w_mem_kv': 'grad_w', 'grad_w_gate': 'grad_w', 'grad_b_gate': 'grad_w', 'grad_w_br_sb': 'grad_w', 'grad_w_br_fox': 'grad_w', 'grad_w_br_mem': 'grad_w', 'grad_w_out': 'grad_w', 'grad_ffn2_pre_g': 'grad_w', 'grad_ffn2_post_g': 'grad_w', 'grad_ffn2_w_gate': 'grad_w', 'grad_ffn2_w_up': 'grad_w', 'grad_ffn2_w_down': 'grad_w', 'delta_ffn1_pre_g': 'delta_w', 'delta_ffn1_post_g': 'delta_w', 'delta_ffn1_w_gate': 'delta_w', 'delta_ffn1_w_up': 'delta_w', 'delta_ffn1_w_down': 'delta_w', 'delta_mix_pre_g': 'delta_w', 'delta_mix_post_g': 'delta_w', 'delta_w_in': 'delta_w', 'delta_b_forget': 'delta_w', 'delta_mem_norm_g': 'delta_w', 'delta_w_mem_kv': 'delta_w', 'delta_w_gate': 'delta_w', 'delta_b_gate': 'delta_w', 'delta_w_br_sb': 'delta_w', 'delta_w_br_fox': 'delta_w', 'delta_w_br_mem': 'delta_w', 'delta_w_out': 'delta_w', 'delta_ffn2_pre_g': 'delta_w', 'delta_ffn2_post_g': 'delta_w', 'delta_ffn2_w_gate': 'delta_w', 'delta_ffn2_w_up': 'delta_w', 'delta_ffn2_w_down': 'delta_w', 'new_m_ffn1_pre_g': 'new_m', 'new_m_ffn1_post_g': 'new_m', 'new_m_ffn1_w_gate': 'new_m', 'new_m_ffn1_w_up': 'new_m', 'new_m_ffn1_w_down': 'new_m', 'new_m_mix_pre_g': 'new_m', 'new_m_mix_post_g': 'new_m', 'new_m_w_in': 'new_m', 'new_m_b_forget': 'new_m', 'new_m_mem_norm_g': 'new_m', 'new_m_w_mem_kv': 'new_m', 'new_m_w_gate': 'new_m', 'new_m_b_gate': 'new_m', 'new_m_w_br_sb': 'new_m', 'new_m_w_br_fox': 'new_m', 'new_m_w_br_mem': 'new_m', 'new_m_w_out': 'new_m', 'new_m_ffn2_pre_g': 'new_m', 'new_m_ffn2_post_g': 'new_m', 'new_m_ffn2_w_gate': 'new_m', 'new_m_ffn2_w_up': 'new_m', 'new_m_ffn2_w_down': 'new_m', 'new_v_ffn1_pre_g': 'new_v', 'new_v_ffn1_post_g': 'new_v', 'new_v_ffn1_w_gate': 'new_v', 'new_v_ffn1_w_up': 'new_v', 'new_v_ffn1_w_down': 'new_v', 'new_v_mix_pre_g': 'new_v', 'new_v_mix_post_g': 'new_v', 'new_v_w_in': 'new_v', 'new_v_b_forget': 'new_v', 'new_v_mem_norm_g': 'new_v', 'new_v_w_mem_kv': 'new_v', 'new_v_w_gate': 'new_v', 'new_v_b_gate': 'new_v', 'new_v_w_br_sb': 'new_v', 'new_v_w_br_fox': 'new_v', 'new_v_w_br_mem': 'new_v', 'new_v_w_out': 'new_v', 'new_v_ffn2_pre_g': 'new_v', 'new_v_ffn2_post_g': 'new_v', 'new_v_ffn2_w_gate': 'new_v', 'new_v_ffn2_w_up': 'new_v', 'new_v_ffn2_w_down': 'new_v'}


def _forward(args):
    return _fwd_reference(*[args[k] for k in FWD_PARAMS])


def _output_shape():
    def fwd():
        inp = _fwd_setup_inputs(0)
        return _fwd_reference(*[inp[k] for k in FWD_PARAMS])
    out = _jax.eval_shape(fwd)
    return out.shape, out.dtype

N_MICROBATCH = 1
ADAM_LR = 0.001
ADAM_B1 = 0.9
ADAM_B2 = 0.999
ADAM_EPS = 1e-08
ADAM_WD = 0.01
ADAM_STEP = 10
PER_EXAMPLE_BATCH_AXIS = {'x': 0, 'mem': 0, 'loss_target': 0}
SHARED_INPUTS = []
_WEIGHT_DTYPES = {'ffn1_pre_g': _jnp.float32, 'ffn1_post_g': _jnp.float32, 'ffn1_w_gate': _jnp.float32, 'ffn1_w_up': _jnp.float32, 'ffn1_w_down': _jnp.float32, 'mix_pre_g': _jnp.float32, 'mix_post_g': _jnp.float32, 'w_in': _jnp.float32, 'b_forget': _jnp.float32, 'mem_norm_g': _jnp.float32, 'w_mem_kv': _jnp.float32, 'w_gate': _jnp.float32, 'b_gate': _jnp.float32, 'w_br_sb': _jnp.float32, 'w_br_fox': _jnp.float32, 'w_br_mem': _jnp.float32, 'w_out': _jnp.float32, 'ffn2_pre_g': _jnp.float32, 'ffn2_post_g': _jnp.float32, 'ffn2_w_gate': _jnp.float32, 'ffn2_w_up': _jnp.float32, 'ffn2_w_down': _jnp.float32}
MOMENT_SCALE = {'ffn1_pre_g': 9.789342e-01, 'ffn1_post_g': 7.825879e+00, 'ffn1_w_gate': 4.008728e-01, 'ffn1_w_up': 4.268358e-01, 'ffn1_w_down': 7.104744e-01, 'mix_pre_g': 1.654769e+00, 'mix_post_g': 3.203295e+01, 'w_in': 8.308593e-01, 'b_forget': 3.941380e+00, 'mem_norm_g': 5.970930e-01, 'w_mem_kv': 2.702442e-01, 'w_gate': 1.985371e-01, 'b_gate': 3.812646e-01, 'w_br_sb': 1.197016e+00, 'w_br_fox': 8.818056e-01, 'w_br_mem': 2.416494e-01, 'w_out': 1.575826e+00, 'ffn2_pre_g': 6.668255e-01, 'ffn2_post_g': 7.991819e+00, 'ffn2_w_gate': 2.589494e-01, 'ffn2_w_up': 3.267641e-01, 'ffn2_w_down': 5.427795e-01}


def _to_microbatches(a, axis):
    t = _jnp.moveaxis(a, axis, 0)
    t = t.reshape((N_MICROBATCH, t.shape[0] // N_MICROBATCH) + t.shape[1:])
    return _jnp.moveaxis(t, 1, axis + 1)


def setup_inputs(seed: int = 0) -> dict:
    inp = _fwd_setup_inputs(seed)
    key = _jax.random.fold_in(_jax.random.key(seed), 7919)
    shape, _ = _output_shape()
    out = dict(inp)
    out["loss_target"] = _jax.random.normal(_jax.random.fold_in(key, 0), shape, _jnp.float32)
    for i, name in enumerate(TWIN_WEIGHTS):
        w = inp[name].astype(_jnp.float32)
        if MOMENT_SCALE is None:
            s = _jnp.sqrt(_jnp.mean(_jnp.square(w)) + 1e-30)
        else:
            s = MOMENT_SCALE[name]
        km, kv = _jax.random.split(_jax.random.fold_in(key, i + 1))
        out[name] = w
        out["m_" + name] = s * _jax.random.normal(km, w.shape, _jnp.float32)
        out["v_" + name] = (s * s) * _jax.random.uniform(kv, w.shape, _jnp.float32, 0.5, 1.5)
    if N_MICROBATCH > 1:
        for name, axis in PER_EXAMPLE_BATCH_AXIS.items():
            out[name] = _to_microbatches(out[name], axis)
    return {'x': out['x'], 'mem': out['mem'], 'ffn1_pre_g': out['ffn1_pre_g'], 'ffn1_post_g': out['ffn1_post_g'], 'ffn1_w_gate': out['ffn1_w_gate'], 'ffn1_w_up': out['ffn1_w_up'], 'ffn1_w_down': out['ffn1_w_down'], 'mix_pre_g': out['mix_pre_g'], 'mix_post_g': out['mix_post_g'], 'w_in': out['w_in'], 'b_forget': out['b_forget'], 'mem_norm_g': out['mem_norm_g'], 'w_mem_kv': out['w_mem_kv'], 'w_gate': out['w_gate'], 'b_gate': out['b_gate'], 'w_br_sb': out['w_br_sb'], 'w_br_fox': out['w_br_fox'], 'w_br_mem': out['w_br_mem'], 'w_out': out['w_out'], 'ffn2_pre_g': out['ffn2_pre_g'], 'ffn2_post_g': out['ffn2_post_g'], 'ffn2_w_gate': out['ffn2_w_gate'], 'ffn2_w_up': out['ffn2_w_up'], 'ffn2_w_down': out['ffn2_w_down'], 'loss_target': out['loss_target'], 'm_ffn1_pre_g': out['m_ffn1_pre_g'], 'm_ffn1_post_g': out['m_ffn1_post_g'], 'm_ffn1_w_gate': out['m_ffn1_w_gate'], 'm_ffn1_w_up': out['m_ffn1_w_up'], 'm_ffn1_w_down': out['m_ffn1_w_down'], 'm_mix_pre_g': out['m_mix_pre_g'], 'm_mix_post_g': out['m_mix_post_g'], 'm_w_in': out['m_w_in'], 'm_b_forget': out['m_b_forget'], 'm_mem_norm_g': out['m_mem_norm_g'], 'm_w_mem_kv': out['m_w_mem_kv'], 'm_w_gate': out['m_w_gate'], 'm_b_gate': out['m_b_gate'], 'm_w_br_sb': out['m_w_br_sb'], 'm_w_br_fox': out['m_w_br_fox'], 'm_w_br_mem': out['m_w_br_mem'], 'm_w_out': out['m_w_out'], 'm_ffn2_pre_g': out['m_ffn2_pre_g'], 'm_ffn2_post_g': out['m_ffn2_post_g'], 'm_ffn2_w_gate': out['m_ffn2_w_gate'], 'm_ffn2_w_up': out['m_ffn2_w_up'], 'm_ffn2_w_down': out['m_ffn2_w_down'], 'v_ffn1_pre_g': out['v_ffn1_pre_g'], 'v_ffn1_post_g': out['v_ffn1_post_g'], 'v_ffn1_w_gate': out['v_ffn1_w_gate'], 'v_ffn1_w_up': out['v_ffn1_w_up'], 'v_ffn1_w_down': out['v_ffn1_w_down'], 'v_mix_pre_g': out['v_mix_pre_g'], 'v_mix_post_g': out['v_mix_post_g'], 'v_w_in': out['v_w_in'], 'v_b_forget': out['v_b_forget'], 'v_mem_norm_g': out['v_mem_norm_g'], 'v_w_mem_kv': out['v_w_mem_kv'], 'v_w_gate': out['v_w_gate'], 'v_b_gate': out['v_b_gate'], 'v_w_br_sb': out['v_w_br_sb'], 'v_w_br_fox': out['v_w_br_fox'], 'v_w_br_mem': out['v_w_br_mem'], 'v_w_out': out['v_w_out'], 'v_ffn2_pre_g': out['v_ffn2_pre_g'], 'v_ffn2_post_g': out['v_ffn2_post_g'], 'v_ffn2_w_gate': out['v_ffn2_w_gate'], 'v_ffn2_w_up': out['v_ffn2_w_up'], 'v_ffn2_w_down': out['v_ffn2_w_down']}


def _loss(weights, diff, rest, loss_target):
    with _jax.named_scope("forward"):
        args = {**rest, TWIN_DIFF_INPUT: diff, **{k: w.astype(_WEIGHT_DTYPES[k]) for k, w in weights.items()}}
        y = _forward(args)
    with _jax.named_scope("loss_head"):
        err = _jnp.square(y.astype(_jnp.float32) - loss_target)
        return 0.5 * _jnp.sum(_jnp.mean(err, axis=-1)) if err.ndim else 0.5 * err


def _adamw(w, g, m, v):
    m = ADAM_B1 * m + (1.0 - ADAM_B1) * g
    v = ADAM_B2 * v + (1.0 - ADAM_B2) * _jnp.square(g)
    m_hat = m / (1.0 - ADAM_B1 ** ADAM_STEP)
    v_hat = v / (1.0 - ADAM_B2 ** ADAM_STEP)
    delta = -ADAM_LR * (m_hat / (_jnp.sqrt(v_hat) + ADAM_EPS) + ADAM_WD * w)
    return delta, m, v


def reference(x, mem, ffn1_pre_g, ffn1_post_g, ffn1_w_gate, ffn1_w_up, ffn1_w_down, mix_pre_g, mix_post_g, w_in, b_forget, mem_norm_g, w_mem_kv, w_gate, b_gate, w_br_sb, w_br_fox, w_br_mem, w_out, ffn2_pre_g, ffn2_post_g, ffn2_w_gate, ffn2_w_up, ffn2_w_down, loss_target, m_ffn1_pre_g, m_ffn1_post_g, m_ffn1_w_gate, m_ffn1_w_up, m_ffn1_w_down, m_mix_pre_g, m_mix_post_g, m_w_in, m_b_forget, m_mem_norm_g, m_w_mem_kv, m_w_gate, m_b_gate, m_w_br_sb, m_w_br_fox, m_w_br_mem, m_w_out, m_ffn2_pre_g, m_ffn2_post_g, m_ffn2_w_gate, m_ffn2_w_up, m_ffn2_w_down, v_ffn1_pre_g, v_ffn1_post_g, v_ffn1_w_gate, v_ffn1_w_up, v_ffn1_w_down, v_mix_pre_g, v_mix_post_g, v_w_in, v_b_forget, v_mem_norm_g, v_w_mem_kv, v_w_gate, v_b_gate, v_w_br_sb, v_w_br_fox, v_w_br_mem, v_w_out, v_ffn2_pre_g, v_ffn2_post_g, v_ffn2_w_gate, v_ffn2_w_up, v_ffn2_w_down):
    given = dict(x=x, mem=mem, ffn1_pre_g=ffn1_pre_g, ffn1_post_g=ffn1_post_g, ffn1_w_gate=ffn1_w_gate, ffn1_w_up=ffn1_w_up, ffn1_w_down=ffn1_w_down, mix_pre_g=mix_pre_g, mix_post_g=mix_post_g, w_in=w_in, b_forget=b_forget, mem_norm_g=mem_norm_g, w_mem_kv=w_mem_kv, w_gate=w_gate, b_gate=b_gate, w_br_sb=w_br_sb, w_br_fox=w_br_fox, w_br_mem=w_br_mem, w_out=w_out, ffn2_pre_g=ffn2_pre_g, ffn2_post_g=ffn2_post_g, ffn2_w_gate=ffn2_w_gate, ffn2_w_up=ffn2_w_up, ffn2_w_down=ffn2_w_down, loss_target=loss_target, m_ffn1_pre_g=m_ffn1_pre_g, m_ffn1_post_g=m_ffn1_post_g, m_ffn1_w_gate=m_ffn1_w_gate, m_ffn1_w_up=m_ffn1_w_up, m_ffn1_w_down=m_ffn1_w_down, m_mix_pre_g=m_mix_pre_g, m_mix_post_g=m_mix_post_g, m_w_in=m_w_in, m_b_forget=m_b_forget, m_mem_norm_g=m_mem_norm_g, m_w_mem_kv=m_w_mem_kv, m_w_gate=m_w_gate, m_b_gate=m_b_gate, m_w_br_sb=m_w_br_sb, m_w_br_fox=m_w_br_fox, m_w_br_mem=m_w_br_mem, m_w_out=m_w_out, m_ffn2_pre_g=m_ffn2_pre_g, m_ffn2_post_g=m_ffn2_post_g, m_ffn2_w_gate=m_ffn2_w_gate, m_ffn2_w_up=m_ffn2_w_up, m_ffn2_w_down=m_ffn2_w_down, v_ffn1_pre_g=v_ffn1_pre_g, v_ffn1_post_g=v_ffn1_post_g, v_ffn1_w_gate=v_ffn1_w_gate, v_ffn1_w_up=v_ffn1_w_up, v_ffn1_w_down=v_ffn1_w_down, v_mix_pre_g=v_mix_pre_g, v_mix_post_g=v_mix_post_g, v_w_in=v_w_in, v_b_forget=v_b_forget, v_mem_norm_g=v_mem_norm_g, v_w_mem_kv=v_w_mem_kv, v_w_gate=v_w_gate, v_b_gate=v_b_gate, v_w_br_sb=v_w_br_sb, v_w_br_fox=v_w_br_fox, v_w_br_mem=v_w_br_mem, v_w_out=v_w_out, v_ffn2_pre_g=v_ffn2_pre_g, v_ffn2_post_g=v_ffn2_post_g, v_ffn2_w_gate=v_ffn2_w_gate, v_ffn2_w_up=v_ffn2_w_up, v_ffn2_w_down=v_ffn2_w_down)
    weights = {n: given[n] for n in TWIN_WEIGHTS}
    shared = {n: given[n] for n in SHARED_INPUTS}
    per_example = {n: given[n] for n in ['x', 'mem']}
    grad_fn = _jax.value_and_grad(_loss, argnums=(0, 1))

    def one_microbatch(ex, loss_target):
        ex = dict(ex)
        diff = ex.pop(TWIN_DIFF_INPUT)
        return grad_fn(weights, diff, {**shared, **ex}, loss_target)

    if N_MICROBATCH == 1:
        loss, (grad_w, grad_x) = one_microbatch(per_example, given["loss_target"])
    else:
        def body(carry, xs):
            loss_sum, grad_sum = carry
            l_k, (gw_k, gx_k) = one_microbatch(xs[0], xs[1])
            with _jax.named_scope("update"):
                return (loss_sum + l_k, _jax.tree.map(_jnp.add, grad_sum, gw_k)), gx_k

        init = (_jnp.zeros((), _jnp.float32), _jax.tree.map(_jnp.zeros_like, weights))
        (loss, grad_w), grad_x = _jax.lax.scan(body, init, (per_example, given["loss_target"]))
    with _jax.named_scope("update"):
        delta_w, new_m, new_v = {}, {}, {}
        for n in TWIN_WEIGHTS:
            delta_w[n], new_m[n], new_v[n] = _adamw(weights[n], grad_w[n], given["m_" + n], given["v_" + n])
    return (loss, grad_x, *[grad_w[n] for n in TWIN_WEIGHTS], *[delta_w[n] for n in TWIN_WEIGHTS],
            *[new_m[n] for n in TWIN_WEIGHTS], *[new_v[n] for n in TWIN_WEIGHTS])
```

```python
import functools

import jax
import jax.numpy as jnp
from jax import lax
from jax.experimental import pallas as pl
from jax.experimental.pallas import tpu as pltpu

F32 = jnp.float32
BF16 = jnp.bfloat16

LANE = 128
SUBLANE_BF16 = 16
VMEM_LIMIT = 48 * 1024 * 1024
N_DEV = 8
MESH = pl.DeviceIdType.MESH
ANY = pl.BlockSpec(memory_space=pl.ANY)

RMS_EPS = 1e-6
HEAD_DIM = 64
N_SB_HEADS = 8
N_FOX_HEADS = 8
N_MEM_HEADS = 4
NEG = -1e30
ATT_TQ = 256
ATT_TK = 128

ADAM_LR = 0.001
ADAM_B1 = 0.9
ADAM_B2 = 0.999
ADAM_EPS = 1e-08
ADAM_WD = 0.01
ADAM_STEP = 10


def _tile(n, target, mult=LANE):
    best = None
    for t in range(mult, min(n, target) + 1, mult):
        if n % t == 0:
            best = t
    return best if best is not None else n


def _cparams(sem):
    return pltpu.CompilerParams(dimension_semantics=sem, vmem_limit_bytes=VMEM_LIMIT)


_DIMS = {"nn": (((1,), (0,)), ((), ())), "nt": (((1,), (1,)), ((), ())), "tn": (((0,), (0,)), ((), ()))}


def _dot(a, b, mode="nn"):
    return lax.dot_general(a.astype(BF16), b.astype(BF16), _DIMS[mode], preferred_element_type=F32)


def _mm(name, pairs, mode, out_dtypes, epilogue=None, extras=(), tm=512, tn=512):
    a0, b0 = pairs[0]
    M = a0.shape[1] if mode == "tn" else a0.shape[0]
    N = b0.shape[0] if mode == "nt" else b0.shape[1]
    tm = _tile(M, tm)
    tn = _tile(N, tn)
    np_, ne, no = len(pairs), len(extras), len(out_dtypes)

    def body(*refs):
        a_refs, b_refs = refs[:np_], refs[np_:2 * np_]
        e_refs = refs[2 * np_:2 * np_ + ne]
        o_refs = refs[2 * np_ + ne:]
        accs = [_dot(a[...], b[...], mode) for a, b in zip(a_refs, b_refs)]
        outs = epilogue(accs, [e[...] for e in e_refs]) if epilogue is not None else accs
        for o, val in zip(o_refs, outs):
            o[...] = val.astype(o.dtype)

    in_specs = []
    for a, _ in pairs:
        if mode == "tn":
            in_specs.append(pl.BlockSpec((a.shape[0], tm), lambda j, i: (0, i)))
        else:
            in_specs.append(pl.BlockSpec((tm, a.shape[1]), lambda j, i: (i, 0)))
    for _, b in pairs:
        if mode == "nt":
            in_specs.append(pl.BlockSpec((tn, b.shape[1]), lambda j, i: (j, 0)))
        else:
            in_specs.append(pl.BlockSpec((b.shape[0], tn), lambda j, i: (0, j)))
    for e, off in extras:
        if e.shape[0] == 1:
            in_specs.append(pl.BlockSpec((1, tn), functools.partial(lambda j, i, o: (0, j + o), o=off // tn)))
        else:
            in_specs.append(pl.BlockSpec((tm, tn), functools.partial(lambda j, i, o: (i, j + o), o=off // tn)))
    out_specs = [pl.BlockSpec((tm, tn), lambda j, i: (i, j)) for _ in range(no)]
    outs = pl.pallas_call(
        body, name=name, grid=(N // tn, M // tm),
        in_specs=in_specs, out_specs=out_specs,
        out_shape=[jax.ShapeDtypeStruct((M, N), dt) for dt in out_dtypes],
        compiler_params=_cparams(("parallel", "parallel")),
    )(*[a for a, _ in pairs], *[b for _, b in pairs], *[e for e, _ in extras])
    return outs[0] if no == 1 else outs


def _sum_accs(accs, _):
    total = accs[0]
    for acc in accs[1:]:
        total = total + acc
    return [total]


def _rstd(x):
    return lax.rsqrt(jnp.mean(x * x, axis=-1, keepdims=True) + RMS_EPS)


def _rms_fwd(name, x, g, out_dtype, res=None, scale=1.0, tr=512):
    R, D = x.shape
    tr = _tile(R, tr, 8)
    has_res = res is not None

    def body(*refs):
        x_ref, g_ref = refs[:2]
        o_ref = refs[-1]
        xv = x_ref[...]
        y = (xv * _rstd(xv)) * g_ref[...]
        if has_res:
            y = refs[2][...] + scale * y
        o_ref[...] = y.astype(o_ref.dtype)

    row = pl.BlockSpec((tr, D), lambda i: (i, 0))
    gain = pl.BlockSpec((1, D), lambda i: (0, 0))
    return pl.pallas_call(
        body, name=name, grid=(R // tr,),
        in_specs=[row, gain] + ([row] if has_res else []), out_specs=row,
        out_shape=jax.ShapeDtypeStruct((R, D), out_dtype),
        compiler_params=_cparams(("parallel",)),
    )(x, g.reshape(1, D), *([res] if has_res else []))


def _rms_bwd(name, x, g, dy, out_dtype, scale=1.0, res=None, tr=512):
    R, D = x.shape
    tr = _tile(R, tr, 8)
    has_res = res is not None

    def body(*refs):
        x_ref, g_ref, dy_ref = refs[:3]
        dx_ref, dg_ref = refs[-2:]
        i = pl.program_id(0)
        xv = x_ref[...]
        xhat = xv * _rstd(xv)
        dyv = dy_ref[...].astype(F32) * scale
        gy = dyv * g_ref[...]
        dx = _rstd(xv) * (gy - xhat * jnp.mean(gy * xhat, axis=-1, keepdims=True))
        if has_res:
            dx = refs[3][...] + dx
        dx_ref[...] = dx.astype(dx_ref.dtype)
        part = jnp.sum(dyv * xhat, axis=0, keepdims=True)

        @pl.when(i == 0)
        def _():
            dg_ref[...] = part

        @pl.when(i > 0)
        def _():
            dg_ref[...] += part

    row = pl.BlockSpec((tr, D), lambda i: (i, 0))
    gain = pl.BlockSpec((1, D), lambda i: (0, 0))
    dx, dg = pl.pallas_call(
        body, name=name, grid=(R // tr,),
        in_specs=[row, gain, row] + ([row] if has_res else []), out_specs=[row, gain],
        out_shape=[jax.ShapeDtypeStruct((R, D), out_dtype), jax.ShapeDtypeStruct((1, D), F32)],
        compiler_params=_cparams(("arbitrary",)),
    )(x, g.reshape(1, D), dy, *([res] if has_res else []))
    return dx, dg[0]


def _loss_grad(y, tgt, tr=512):
    R, D = y.shape
    tr = _tile(R, tr, 8)

    def body(y_ref, t_ref, dy_ref, loss_ref):
        i = pl.program_id(0)
        d = y_ref[...] - t_ref[...]
        dy_ref[...] = d / D
        part = 0.5 * jnp.sum(jnp.mean(d * d, axis=-1, keepdims=True), axis=0, keepdims=True)
        tile = jnp.broadcast_to(part, loss_ref.shape)

        @pl.when(i == 0)
        def _():
            loss_ref[...] = tile

        @pl.when(i > 0)
        def _():
            loss_ref[...] += tile

    row = pl.BlockSpec((tr, D), lambda i: (i, 0))
    dy, loss = pl.pallas_call(
        body, name="loss_grad", grid=(R // tr,),
        in_specs=[row, row], out_specs=[row, pl.BlockSpec((8, LANE), lambda i: (0, 0))],
        out_shape=[jax.ShapeDtypeStruct((R, D), F32), jax.ShapeDtypeStruct((8, LANE), F32)],
        compiler_params=_cparams(("arbitrary",)),
    )(y, tgt)
    return loss[0, 0], dy


def _colsum(name, x, tr=512, tn=1024):
    R, N = x.shape
    tr, tn = _tile(R, tr, 8), _tile(N, tn)

    def body(x_ref, o_ref):
        i = pl.program_id(1)
        part = jnp.sum(x_ref[...].astype(F32), axis=0, keepdims=True)

        @pl.when(i == 0)
        def _():
            o_ref[...] = part

        @pl.when(i > 0)
        def _():
            o_ref[...] += part

    out = pl.pallas_call(
        body, name=name, grid=(N // tn, R // tr),
        in_specs=[pl.BlockSpec((tr, tn), lambda j, i: (i, j))], out_specs=pl.BlockSpec((1, tn), lambda j, i: (0, j)),
        out_shape=jax.ShapeDtypeStruct((1, N), F32),
        compiler_params=_cparams(("parallel", "arbitrary")),
    )(x)
    return out[0]


def _tri(tk, rel):
    j = lax.broadcasted_iota(jnp.int32, (tk, tk), 0)
    s = lax.broadcasted_iota(jnp.int32, (tk, tk), 1)
    return rel(j, s).astype(BF16)


def _dot_split(x, m, parts=2):
    total = None
    rem = x
    for _ in range(parts):
        piece = rem.astype(BF16)
        rem = rem - piece.astype(F32)
        term = jnp.dot(piece, m, preferred_element_type=F32)
        total = term if total is None else total + term
    return total


def _log_not_and_beta(z, mask):
    ln = -(jnp.maximum(z, 0.0) + jnp.log(1.0 + jnp.exp(-jnp.abs(z))))
    return jnp.where(mask, ln, 0.0), ln + z


def _sb_fwd(q, k, v, scale):
    H, T, d = q.shape
    tq, tk = min(ATT_TQ, T), ATT_TK

    def body(q_ref, k_ref, v_ref, o_ref, rt_ref, acc_ref, r_ref):
        qi = pl.program_id(1)
        qv = q_ref[0]
        acc_ref[...] = jnp.zeros_like(acc_ref)
        r_ref[...] = jnp.zeros_like(r_ref)
        rows = qi * tq + lax.broadcasted_iota(jnp.int32, (tq, tk), 0)
        cols0 = lax.broadcasted_iota(jnp.int32, (tq, tk), 1)
        after = _tri(tk, lambda j, s: j > s)
        nkb = (qi + 1) * (tq // tk)

        def step(i, carry):
            kb = nkb - 1 - i
            ks = pl.multiple_of(kb * tk, tk)
            kv = k_ref[0, pl.ds(ks, tk), :]
            vv = v_ref[0, pl.ds(ks, tk), :]
            z = _dot(qv, kv, "nt") * scale
            mask = (cols0 + ks) < rows
            ln, lb = _log_not_and_beta(z, mask)
            between = _dot_split(ln, after)
            w = jnp.where(mask, jnp.exp(lb + between + r_ref[...]), 0.0)
            acc_ref[...] += _dot(w, vv)
            r_ref[...] += between[:, 0:1] + ln[:, 0:1]
            return carry

        lax.fori_loop(0, nkb, step, 0)
        o_ref[0] = acc_ref[...]
        rt_ref[0] = r_ref[...]

    blk = pl.BlockSpec((1, tq, d), lambda h, i: (h, i, 0))
    full = pl.BlockSpec((1, T, d), lambda h, i: (h, 0, 0))
    col = pl.BlockSpec((1, tq, 1), lambda h, i: (h, i, 0))
    return pl.pallas_call(
        body, name="sb_fwd", grid=(H, T // tq),
        in_specs=[blk, full, full], out_specs=[blk, col],
        out_shape=[jax.ShapeDtypeStruct((H, T, d), F32), jax.ShapeDtypeStruct((H, T, 1), F32)],
        scratch_shapes=[pltpu.VMEM((tq, d), F32), pltpu.VMEM((tq, 1), F32)],
        compiler_params=_cparams(("parallel", "arbitrary")),
    )(q, k, v)


def _sb_bwd(q, k, v, do, rtot, scale):
    H, T, d = q.shape
    tq, tk = min(ATT_TQ, T), ATT_TK

    def body(q_ref, k_ref, v_ref, do_ref, rt_ref, dq_ref, dk_ref, dv_ref, dq_acc, p_ref, c_ref):
        qi = pl.program_id(1)

        @pl.when(qi == 0)
        def _():
            dk_ref[...] = jnp.zeros_like(dk_ref)
            dv_ref[...] = jnp.zeros_like(dv_ref)

        qv = q_ref[0]
        dov = do_ref[0]
        rt = rt_ref[0]
        dq_acc[...] = jnp.zeros_like(dq_acc)
        p_ref[...] = jnp.zeros_like(p_ref)
        c_ref[...] = jnp.zeros_like(c_ref)
        rows = qi * tq + lax.broadcasted_iota(jnp.int32, (tq, tk), 0)
        cols0 = lax.broadcasted_iota(jnp.int32, (tq, tk), 1)
        upto = _tri(tk, lambda j, s: j <= s)
        before = _tri(tk, lambda j, s: j < s)
        nkb = (qi + 1) * (tq // tk)

        def step(kb, carry):
            ks = pl.multiple_of(kb * tk, tk)
            kv = k_ref[0, pl.ds(ks, tk), :]
            vv = v_ref[0, pl.ds(ks, tk), :]
            z = _dot(qv, kv, "nt") * scale
            mask = (cols0 + ks) < rows
            ln, lb = _log_not_and_beta(z, mask)
            sig = jnp.exp(lb)
            prefix = _dot_split(ln, upto) + p_ref[...]
            w = jnp.where(mask, jnp.exp(lb + (rt - prefix)), 0.0)
            g = _dot(dov, vv, "nt") * w
            gpre = _dot_split(g, before)
            c = gpre + c_ref[...]
            dz = jnp.where(mask, g * (1.0 - sig) - c * sig, 0.0) * scale
            dq_acc[...] += _dot(dz, kv)
            dk_ref[0, pl.ds(ks, tk), :] += _dot(dz, qv, "tn")
            dv_ref[0, pl.ds(ks, tk), :] += _dot(w, dov, "tn")
            p_ref[...] = prefix[:, tk - 1:tk]
            c_ref[...] = c[:, tk - 1:tk] + g[:, tk - 1:tk]
            return carry

        lax.fori_loop(0, nkb, step, 0)
        dq_ref[0] = dq_acc[...].astype(dq_ref.dtype)

    blk = pl.BlockSpec((1, tq, d), lambda h, i: (h, i, 0))
    full = pl.BlockSpec((1, T, d), lambda h, i: (h, 0, 0))
    col = pl.BlockSpec((1, tq, 1), lambda h, i: (h, i, 0))
    return pl.pallas_call(
        body, name="sb_bwd", grid=(H, T // tq),
        in_specs=[blk, full, full, blk, col], out_specs=[blk, full, full],
        out_shape=[jax.ShapeDtypeStruct((H, T, d), BF16), jax.ShapeDtypeStruct((H, T, d), F32),
                   jax.ShapeDtypeStruct((H, T, d), F32)],
        scratch_shapes=[pltpu.VMEM((tq, d), F32), pltpu.VMEM((tq, 1), F32), pltpu.VMEM((tq, 1), F32)],
        compiler_params=_cparams(("parallel", "arbitrary")),
    )(q, k, v, do, rtot)


def _attn_fwd(name, q, k, v, scale, c=None):
    H, T, d = q.shape
    Tk = k.shape[1]
    tq, tk = min(ATT_TQ, T), ATT_TK
    causal = c is not None

    def body(*refs):
        q_ref, k_ref, v_ref = refs[:3]
        cc_ref, cr_ref = refs[3:5] if causal else (None, None)
        o_ref, lse_ref, m_ref, l_ref, acc_ref = refs[-5:]
        qi = pl.program_id(1)
        qv = q_ref[0]
        m_ref[...] = jnp.full_like(m_ref, NEG)
        l_ref[...] = jnp.zeros_like(l_ref)
        acc_ref[...] = jnp.zeros_like(acc_ref)
        rows = qi * tq + lax.broadcasted_iota(jnp.int32, (tq, tk), 0)
        cols0 = lax.broadcasted_iota(jnp.int32, (tq, tk), 1)
        nkb = (qi + 1) * (tq // tk) if causal else Tk // tk

        def step(kb, carry):
            ks = pl.multiple_of(kb * tk, tk)
            z = _dot(qv, k_ref[0, pl.ds(ks, tk), :], "nt") * scale
            if causal:
                z = z + cc_ref[0] - cr_ref[0, kb]
                z = jnp.where((cols0 + ks) <= rows, z, NEG)
            m_prev = m_ref[...]
            m_new = jnp.maximum(m_prev, jnp.max(z, axis=1, keepdims=True))
            p = jnp.exp(z - m_new)
            alpha = jnp.exp(m_prev - m_new)
            l_ref[...] = alpha * l_ref[...] + jnp.sum(p, axis=1, keepdims=True)
            acc_ref[...] = alpha * acc_ref[...] + _dot(p, v_ref[0, pl.ds(ks, tk), :])
            m_ref[...] = m_new
            return carry

        lax.fori_loop(0, nkb, step, 0)
        o_ref[0] = acc_ref[...] / l_ref[...]
        lse_ref[0] = m_ref[...] + jnp.log(l_ref[...])

    blk = pl.BlockSpec((1, tq, d), lambda h, i: (h, i, 0))
    full = pl.BlockSpec((1, Tk, d), lambda h, i: (h, 0, 0))
    col = pl.BlockSpec((1, tq, 1), lambda h, i: (h, i, 0))
    in_specs, args = [blk, full, full], [q, k, v]
    if causal:
        in_specs += [col, pl.BlockSpec((1, T // tk, 1, tk), lambda h, i: (h, 0, 0, 0))]
        args += [c.reshape(H, T, 1), c.reshape(H, T // tk, 1, tk)]
    return pl.pallas_call(
        body, name=name, grid=(H, T // tq),
        in_specs=in_specs, out_specs=[blk, col],
        out_shape=[jax.ShapeDtypeStruct((H, T, d), F32), jax.ShapeDtypeStruct((H, T, 1), F32)],
        scratch_shapes=[pltpu.VMEM((tq, 1), F32), pltpu.VMEM((tq, 1), F32), pltpu.VMEM((tq, d), F32)],
        compiler_params=_cparams(("parallel", "arbitrary")),
    )(*args)


def _attn_bwd(name, q, k, v, o, do, lse, scale, c=None):
    H, T, d = q.shape
    Tk = k.shape[1]
    tq, tk = min(ATT_TQ, T), ATT_TK
    causal = c is not None

    def body(*refs):
        q_ref, k_ref, v_ref, o_ref, do_ref, lse_ref = refs[:6]
        cc_ref, cr_ref = refs[6:8] if causal else (None, None)
        n_out = 5 if causal else 3
        outs = refs[-(n_out + 1):-1]
        dq_ref, dk_ref, dv_ref = outs[:3]
        dc_ref, drow_ref = outs[3:5] if causal else (None, None)
        dq_acc = refs[-1]
        qi = pl.program_id(1)

        @pl.when(qi == 0)
        def _():
            dk_ref[...] = jnp.zeros_like(dk_ref)
            dv_ref[...] = jnp.zeros_like(dv_ref)
            if causal:
                dc_ref[...] = jnp.zeros_like(dc_ref)

        qv = q_ref[0]
        dov = do_ref[0]
        lse_v = lse_ref[0]
        delta = jnp.sum(dov.astype(F32) * o_ref[0], axis=1, keepdims=True)
        dq_acc[...] = jnp.zeros_like(dq_acc)
        if causal:
            drow_ref[...] = jnp.zeros_like(drow_ref)
        rows = qi * tq + lax.broadcasted_iota(jnp.int32, (tq, tk), 0)
        cols0 = lax.broadcasted_iota(jnp.int32, (tq, tk), 1)
        nkb = (qi + 1) * (tq // tk) if causal else Tk // tk

        def step(kb, carry):
            ks = pl.multiple_of(kb * tk, tk)
            kv = k_ref[0, pl.ds(ks, tk), :]
            vv = v_ref[0, pl.ds(ks, tk), :]
            z = _dot(qv, kv, "nt") * scale
            if causal:
                z = z + cc_ref[0] - cr_ref[0, kb]
                z = jnp.where((cols0 + ks) <= rows, z, NEG)
            p = jnp.exp(z - lse_v)
            ds = p * (_dot(dov, vv, "nt") - delta)
            dq_acc[...] += _dot(ds, kv)
            dk_ref[0, pl.ds(ks, tk), :] += _dot(ds, qv, "tn") * scale
            dv_ref[0, pl.ds(ks, tk), :] += _dot(p, dov, "tn")
            if causal:
                dc_ref[0, kb] -= jnp.sum(ds, axis=0, keepdims=True)
                drow_ref[0] += jnp.sum(ds, axis=1, keepdims=True)
            return carry

        lax.fori_loop(0, nkb, step, 0)
        dq_ref[0] = (dq_acc[...] * scale).astype(dq_ref.dtype)

    blk = pl.BlockSpec((1, tq, d), lambda h, i: (h, i, 0))
    full = pl.BlockSpec((1, Tk, d), lambda h, i: (h, 0, 0))
    col = pl.BlockSpec((1, tq, 1), lambda h, i: (h, i, 0))
    crow = pl.BlockSpec((1, T // tk, 1, tk), lambda h, i: (h, 0, 0, 0))
    in_specs, args = [blk, full, full, blk, blk, col], [q, k, v, o, do, lse]
    out_specs = [blk, full, full]
    out_shape = [jax.ShapeDtypeStruct((H, T, d), BF16), jax.ShapeDtypeStruct((H, Tk, d), F32),
                 jax.ShapeDtypeStruct((H, Tk, d), F32)]
    if causal:
        in_specs += [col, crow]
        args += [c.reshape(H, T, 1), c.reshape(H, T // tk, 1, tk)]
        out_specs += [crow, col]
        out_shape += [jax.ShapeDtypeStruct((H, T // tk, 1, tk), F32), jax.ShapeDtypeStruct((H, T, 1), F32)]
    outs = pl.pallas_call(
        body, name=name, grid=(H, T // tq),
        in_specs=in_specs, out_specs=out_specs, out_shape=out_shape,
        scratch_shapes=[pltpu.VMEM((tq, d), F32)],
        compiler_params=_cparams(("parallel", "arbitrary")),
    )(*args)
    if causal:
        return outs[0], outs[1], outs[2], outs[3].reshape(H, T), outs[4].reshape(H, T)
    return outs


def _decay_fwd(fl, b):
    H, T = fl.shape
    tk = ATT_TK

    def body(x_ref, b_ref, c_ref):
        upto = _tri(tk, lambda j, s: j <= s)
        carry = jnp.zeros((H, 1), F32)
        for i in range(T // tk):
            xv = x_ref[:, i * tk:(i + 1) * tk] + b_ref[...]
            lf = jnp.minimum(xv, 0.0) - jnp.log(1.0 + jnp.exp(-jnp.abs(xv)))
            pref = _dot_split(lf, upto, parts=3) + carry
            c_ref[:, i * tk:(i + 1) * tk] = pref
            carry = pref[:, tk - 1:tk]

    vm = pl.BlockSpec(memory_space=pltpu.VMEM)
    return pl.pallas_call(
        body, name="decay_fwd", in_specs=[vm, vm], out_specs=vm,
        out_shape=jax.ShapeDtypeStruct((H, T), F32),
    )(fl, b)


def _decay_bwd(dc_cols, dc_rows, fl, b):
    H, T = fl.shape
    tk = ATT_TK

    def body(dc_ref, dr_ref, x_ref, b_ref, dx_ref, db_ref):
        from_ = _tri(tk, lambda j, s: j >= s)
        carry = jnp.zeros((H, 1), F32)
        total = jnp.zeros((H, 1), F32)
        for i in reversed(range(T // tk)):
            sl = slice(i * tk, (i + 1) * tk)
            suffix = _dot_split(dc_ref[:, sl] + dr_ref[:, sl], from_, parts=3) + carry
            xv = x_ref[:, sl] + b_ref[...]
            dx = suffix / (1.0 + jnp.exp(xv))
            dx_ref[:, sl] = dx
            total = total + jnp.sum(dx, axis=1, keepdims=True)
            carry = suffix[:, 0:1]
        db_ref[...] = jnp.broadcast_to(total, db_ref.shape)

    vm = pl.BlockSpec(memory_space=pltpu.VMEM)
    dx, db = pl.pallas_call(
        body, name="decay_bwd", in_specs=[vm, vm, vm, vm], out_specs=[vm, vm],
        out_shape=[jax.ShapeDtypeStruct((H, T), F32), jax.ShapeDtypeStruct((H, LANE), F32)],
    )(dc_cols, dc_rows, fl, b)
    return dx, db[:, 0]


def _place():
    x, y, c = lax.axis_index("x"), lax.axis_index("y"), lax.axis_index("c")
    return x, y, c, [(1 - x, y), (x, 1 - y), (1 - x, 1 - y)]


def _all_gather(name, block):
    R, C = block.shape

    def body(x_ref, out_ref, send_sems, recv_sems, local_sem):
        x, y, c, chips = _place()
        me, sibling = (x, y, c), (x, y, 1 - c)

        def rows(px, py, pc):
            return out_ref.at[4 * px + 2 * py + pc]

        def copy(k, blk, to, src=None):
            return pltpu.make_async_remote_copy(
                src_ref=rows(*blk) if src is None else src, dst_ref=rows(*blk),
                send_sem=send_sems.at[k], recv_sem=recv_sems.at[k], device_id=to, device_id_type=MESH)

        mine = pltpu.make_async_copy(x_ref, rows(*me), local_sem)
        mine.start()
        first = [copy(0, me, sibling, src=x_ref)]
        first += [copy(1 + j, me, (*chip, c), src=x_ref) for j, chip in enumerate(chips)]
        for cp in first:
            cp.start()
        passed = [copy(4 + j, (*chip, c), sibling) for j, chip in enumerate(chips)]
        for j, chip in enumerate(chips):
            copy(1 + j, (*chip, c), me).wait_recv()
            passed[j].start()
        copy(0, sibling, me).wait_recv()
        for j, chip in enumerate(chips):
            copy(4 + j, (*chip, 1 - c), me).wait_recv()
        for cp in first + passed:
            cp.wait_send()
        mine.wait()

    return pl.pallas_call(
        body, name=name, in_specs=[ANY], out_specs=ANY,
        out_shape=jax.ShapeDtypeStruct((N_DEV, R, C), block.dtype),
        scratch_shapes=[pltpu.SemaphoreType.DMA((7,)), pltpu.SemaphoreType.DMA((7,)), pltpu.SemaphoreType.DMA(())],
    )(block)


def _swap_with_sibling(name, parts):
    _, R, C = parts.shape

    def body(p_ref, out_ref, send_sems, recv_sems):
        x, y, c, _ = _place()
        copies = [pltpu.make_async_remote_copy(
            src_ref=p_ref.at[2 * q + (1 - c)], dst_ref=out_ref.at[q],
            send_sem=send_sems.at[q], recv_sem=recv_sems.at[q], device_id=(x, y, 1 - c), device_id_type=MESH)
            for q in range(4)]
        for cp in copies:
            cp.start()
        for cp in copies:
            cp.wait_recv()
        for cp in copies:
            cp.wait_send()

    return pl.pallas_call(
        body, name=name, in_specs=[ANY], out_specs=ANY,
        out_shape=jax.ShapeDtypeStruct((4, R, C), parts.dtype),
        scratch_shapes=[pltpu.SemaphoreType.DMA((4,)), pltpu.SemaphoreType.DMA((4,))],
    )(parts)


def _add_own(name, parts, got, tr=512):
    _, R, C = parts.shape
    tr = _tile(R, tr, SUBLANE_BF16)

    def body(c_ref, p_ref, g_ref, o_ref):
        o_ref[...] = (p_ref[...].astype(F32) + g_ref[...].astype(F32)).astype(o_ref.dtype)

    return pl.pallas_call(
        body, name=name,
        grid_spec=pltpu.PrefetchScalarGridSpec(
            num_scalar_prefetch=1, grid=(4, R // tr),
            in_specs=[pl.BlockSpec((1, tr, C), lambda q, i, c: (2 * q + c[0], i, 0)),
                      pl.BlockSpec((1, tr, C), lambda q, i, c: (q, i, 0))],
            out_specs=pl.BlockSpec((1, tr, C), lambda q, i, c: (q, i, 0))),
        out_shape=jax.ShapeDtypeStruct((4, R, C), parts.dtype),
        compiler_params=_cparams(("parallel", "parallel")),
    )(lax.axis_index("c").astype(jnp.int32).reshape(1), parts, got)


def _swap_with_chips(name, parts):
    _, R, C = parts.shape

    def body(p_ref, out_ref, send_sems, recv_sems, local_sem):
        x, y, c, chips = _place()
        my_chip = 2 * x + y
        mine = pltpu.make_async_copy(p_ref.at[my_chip], out_ref.at[my_chip], local_sem)
        mine.start()
        sends = [pltpu.make_async_remote_copy(
            src_ref=p_ref.at[2 * cx + cy], dst_ref=out_ref.at[my_chip],
            send_sem=send_sems.at[j], recv_sem=recv_sems.at[j], device_id=(cx, cy, c), device_id_type=MESH)
            for j, (cx, cy) in enumerate(chips)]
        for cp in sends:
            cp.start()
        for j, (cx, cy) in enumerate(chips):
            pltpu.make_async_remote_copy(
                src_ref=p_ref.at[my_chip], dst_ref=out_ref.at[2 * cx + cy],
                send_sem=send_sems.at[j], recv_sem=recv_sems.at[j], device_id=(cx, cy, c), device_id_type=MESH,
            ).wait_recv()
        for cp in sends:
            cp.wait_send()
        mine.wait()

    return pl.pallas_call(
        body, name=name, in_specs=[ANY], out_specs=ANY,
        out_shape=jax.ShapeDtypeStruct((4, R, C), parts.dtype),
        scratch_shapes=[pltpu.SemaphoreType.DMA((3,)), pltpu.SemaphoreType.DMA((3,)), pltpu.SemaphoreType.DMA(())],
    )(parts)


def _sum_parts(name, parts, tr=512):
    P, R, C = parts.shape
    tr = _tile(R, tr, SUBLANE_BF16)

    def body(p_ref, o_ref):
        total = p_ref[0].astype(F32)
        for p in range(1, P):
            total = total + p_ref[p].astype(F32)
        o_ref[...] = total

    return pl.pallas_call(
        body, name=name, grid=(R // tr,),
        in_specs=[pl.BlockSpec((P, tr, C), lambda i: (0, i, 0))], out_specs=pl.BlockSpec((tr, C), lambda i: (i, 0)),
        out_shape=jax.ShapeDtypeStruct((R, C), F32),
        compiler_params=_cparams(("parallel",)),
    )(parts)


def _reduce_scatter(tag, parts):
    got = _swap_with_sibling("rs_pair_" + tag, parts)
    pair = _add_own("rs_add_" + tag, parts, got)
    quad = _swap_with_chips("rs_chips_" + tag, pair)
    return _sum_parts("rs_sum_" + tag, quad)


def _adamw(name, g_parts, w, m, v, tr=512):
    P, R, C = g_parts.shape
    tr = _tile(R, tr, 8)

    def body(g_ref, w_ref, m_ref, v_ref, go_ref, d_ref, mo_ref, vo_ref):
        g = g_ref[0]
        for p in range(1, P):
            g = g + g_ref[p]
        mn = ADAM_B1 * m_ref[...] + (1.0 - ADAM_B1) * g
        vn = ADAM_B2 * v_ref[...] + (1.0 - ADAM_B2) * (g * g)
        m_hat = mn / (1.0 - ADAM_B1 ** ADAM_STEP)
        v_hat = vn / (1.0 - ADAM_B2 ** ADAM_STEP)
        go_ref[...] = g
        d_ref[...] = -ADAM_LR * (m_hat / (jnp.sqrt(v_hat) + ADAM_EPS) + ADAM_WD * w_ref[...])
        mo_ref[...] = mn
        vo_ref[...] = vn

    row = pl.BlockSpec((tr, C), lambda i: (i, 0))
    return pl.pallas_call(
        body, name=name, grid=(R // tr,),
        in_specs=[pl.BlockSpec((P, tr, C), lambda i: (0, i, 0)), row, row, row], out_specs=[row] * 4,
        out_shape=[jax.ShapeDtypeStruct((R, C), F32)] * 4,
        compiler_params=_cparams(("parallel",)),
    )(g_parts, w, m, v)


def _to_heads(t, n_heads):
    rows = t.shape[0]
    return t.reshape(rows, n_heads, -1).transpose(1, 0, 2)


def _from_heads(t):
    return t.transpose(1, 0, 2).reshape(t.shape[1], -1)


def _pad_rows(t, rows):
    return jnp.pad(t, ((0, rows - t.shape[0]), (0, 0)))


class _Layout:
    def __init__(self, D, ff_shard, in_shard, kv_shard, gate_shard, br_in, br_shard, out_shard):
        self.D = D
        self.in_shard = in_shard
        self.in_pad = -(-in_shard // LANE) * LANE
        self.br_in, self.br_shard = br_in, br_shard
        br_rows = br_shard * br_in // D
        sizes = [("g1", ff_shard), ("u1", ff_shard), ("d1", ff_shard), ("win", self.in_pad), ("kv", kv_shard),
                 ("gate", gate_shard), ("br", br_rows), ("out", out_shard),
                 ("g2", ff_shard), ("u2", ff_shard), ("d2", ff_shard)]
        self.seg, off = {}, 0
        for key, n in sizes:
            assert n % SUBLANE_BF16 == 0, (key, n)
            self.seg[key] = (off, n)
            off += n
        self.rows = off

    def pack(self, parts):
        return jnp.concatenate([parts[key] for key in self.seg], axis=0)

    def take(self, gathered, key):
        off, n = self.seg[key]
        return gathered[:, off:off + n, :].reshape(N_DEV * n, self.D)

    def spread(self, full, key):
        _, n = self.seg[key]
        return full.reshape(N_DEV, n, self.D)


def _pack_layer(lay, l, p):
    D = lay.D
    br = jnp.concatenate([p["w_br_sb"][l], p["w_br_fox"][l], p["w_br_mem"][l]], axis=0)
    parts = {
        "g1": p["ffn1_w_gate"][l].T, "u1": p["ffn1_w_up"][l].T, "d1": p["ffn1_w_down"][l],
        "win": _pad_rows(p["w_in"][l].T, lay.in_pad), "kv": p["w_mem_kv"][l], "gate": p["w_gate"][l].T,
        "br": br.T.reshape(-1, D), "out": p["w_out"][l],
        "g2": p["ffn2_w_gate"][l].T, "u2": p["ffn2_w_up"][l].T, "d2": p["ffn2_w_down"][l],
    }
    return lay.pack({k: t.astype(BF16) for k, t in parts.items()})


def _unpack_layer(lay, gathered):
    D = lay.D
    w = {k: lay.take(gathered, k) for k in ("g1", "u1", "d1", "win", "kv", "out", "g2", "u2", "d2")}
    gate = lay.take(gathered, "gate")
    w["gate"] = gate
    w["gate3"] = [gate[i * D:(i + 1) * D] for i in range(3)]
    br = lay.take(gathered, "br").reshape(N_DEV * lay.br_shard, lay.br_in)
    third = lay.br_in // 3
    w["br3"] = [br[:, i * third:(i + 1) * third] for i in range(3)]
    return w


def _silu_mul(accs, _):
    a, b = accs
    return [a, b, a * jax.nn.sigmoid(a) * b]


def _act_bwd(accs, extras):
    ds, (a, b) = accs[0], extras
    sig = jax.nn.sigmoid(a)
    return [ds * b * (sig * (1.0 + a * (1.0 - sig))), ds * (a * sig)]


def _ffn_fwd(tag, h, pre_g, post_g, wg, wu, wd):
    n = _rms_fwd("ffn_norm_" + tag, h, pre_g, BF16)
    a, b, s = _mm("ffn_up_" + tag, [(n, wg), (n, wu)], "nt", [F32, F32, BF16], _silu_mul, tm=256, tn=1408)
    f = _mm("ffn_down_" + tag, [(s, wd)], "nn", [F32], tm=256)
    out = _rms_fwd("ffn_out_" + tag, f, post_g, F32, res=h, scale=0.5)
    return out, (h, n, a, b, s, f)


def _ffn_bwd(tag, dh, saved, pre_g, post_g, wg, wu, wd):
    h, n, a, b, s, f = saved
    df, d_post = _rms_bwd("ffn_dout_" + tag, f, post_g, dh, BF16, scale=0.5)
    da, db = _mm("ffn_dact_" + tag, [(df, wd)], "nt", [BF16, BF16], _act_bwd, [(a, 0), (b, 0)], tm=256, tn=1408)
    d_wd = _mm("ffn_dwd_" + tag, [(s, df)], "tn", [BF16], tm=256)
    dn = _mm("ffn_dn_" + tag, [(da, wg), (db, wu)], "nn", [F32], _sum_accs, tm=256)
    d_wg = _mm("ffn_dwg_" + tag, [(da, n)], "tn", [BF16], tm=256)
    d_wu = _mm("ffn_dwu_" + tag, [(db, n)], "tn", [BF16], tm=256)
    dh_in, d_pre = _rms_bwd("ffn_dnorm_" + tag, h, pre_g, dn, F32, res=dh)
    return dh_in, d_pre, d_post, d_wg, d_wu, d_wd


def _unpad_proj(lay, projp):
    T = projp.shape[0]
    return projp.reshape(T, N_DEV, lay.in_pad)[:, :, :lay.in_shard].reshape(T, N_DEV * lay.in_shard)


def _pad_proj(lay, proj):
    T = proj.shape[0]
    t = proj.reshape(T, N_DEV, lay.in_shard)
    return jnp.pad(t, ((0, 0), (0, 0), (0, lay.in_pad - lay.in_shard))).reshape(T, N_DEV * lay.in_pad)


_SB_W = N_SB_HEADS * HEAD_DIM
_FOX_W = N_FOX_HEADS * HEAD_DIM
_SPLITS = [_SB_W, 2 * _SB_W, 3 * _SB_W, 3 * _SB_W + _FOX_W, 3 * _SB_W + 2 * _FOX_W, 3 * _SB_W + 3 * _FOX_W,
           3 * _SB_W + 3 * _FOX_W + N_FOX_HEADS]


def _gate_act(accs, extras):
    return [jax.nn.sigmoid(accs[0] + extras[0])]


def _merge(accs, extras):
    return [extras[0] * accs[0] + extras[1] * accs[1] + extras[2] * accs[2]]


def _merge_bwd(accs, extras):
    dm = accs[0]
    d_branch = [dm * gi for gi in extras]
    d_gate = [dm * bi * gi * (1.0 - gi) for bi, gi in zip(accs[1:], extras)]
    return d_branch + d_gate


def _mix_fwd(lay, h, w, pre_g, post_g, b_forget, b_gate, mem_n):
    D = lay.D
    u = _rms_fwd("mix_norm", h, pre_g, BF16)
    projp = _mm("mix_in", [(u, w["win"])], "nt", [F32])
    q_sb, k_sb, v_sb, q_fx, k_fx, v_fx, f_logit, q_mem = jnp.split(_unpad_proj(lay, projp), _SPLITS, axis=1)
    sb = [_to_heads(t.astype(BF16), N_SB_HEADS) for t in (q_sb, k_sb, v_sb)]
    fx = [_to_heads(t.astype(BF16), N_FOX_HEADS) for t in (q_fx, k_fx, v_fx)]
    qm = _to_heads(q_mem.astype(BF16), N_MEM_HEADS)
    fl = f_logit.T
    c = _decay_fwd(fl, b_forget.reshape(-1, 1))
    o_sb, rtot = _sb_fwd(*sb, HEAD_DIM ** -0.5)
    o_fx, lse_fx = _attn_fwd("fox_fwd", *fx, HEAD_DIM ** -0.5, c)
    kvm = _mm("mem_kv", [(mem_n, w["kv"])], "nn", [BF16])
    half = kvm.shape[1] // 2
    km, vm = _to_heads(kvm[:, :half], N_MEM_HEADS), _to_heads(kvm[:, half:], N_MEM_HEADS)
    o_mem, lse_mem = _attn_fwd("mem_fwd", qm, km, vm, qm.shape[-1] ** -0.5)
    gates = _mm("mix_gate", [(u, w["gate"])], "nt", [F32], _gate_act, [(b_gate.reshape(1, -1), 0)])
    flat = [_from_heads(o).astype(BF16) for o in (o_sb, o_fx, o_mem)]
    merged = _mm("mix_merge", list(zip(flat, w["br3"])), "nt", [BF16], _merge,
                 [(gates, 0), (gates, D), (gates, 2 * D)])
    z = _mm("mix_out", [(merged, w["out"])], "nn", [F32])
    out = _rms_fwd("mix_res", z, post_g, F32, res=h)
    saved = (h, u, sb, fx, qm, fl, c, o_sb, rtot, o_fx, lse_fx, km, vm, o_mem, lse_mem, gates, flat, merged, z)
    return out, saved


def _mix_bwd(lay, dh, saved, w, pre_g, post_g, b_forget, mem_n, dmem_n):
    D = lay.D
    h, u, sb, fx, qm, fl, c, o_sb, rtot, o_fx, lse_fx, km, vm, o_mem, lse_mem, gates, flat, merged, z = saved
    dz, d_post = _rms_bwd("mix_dres", z, post_g, dh, BF16)
    outs = _mm("mix_dmerge", [(dz, w["out"])] + list(zip(flat, w["br3"])), "nt", [BF16] * 6, _merge_bwd,
               [(gates, 0), (gates, D), (gates, 2 * D)])
    d_branch, d_gate = outs[:3], outs[3:]
    d_wout = _mm("mix_dwout", [(merged, dz)], "tn", [BF16])
    d_o = [_mm("mix_dbr%d" % i, [(d_branch[i], w["br3"][i])], "nn", [BF16]) for i in range(3)]
    d_wbr = [_mm("mix_dwbr%d" % i, [(d_branch[i], flat[i])], "tn", [BF16]) for i in range(3)]
    d_bgate = jnp.concatenate([_colsum("mix_dbgate%d" % i, d_gate[i]) for i in range(3)])
    d_wgate = [_mm("mix_dwgate%d" % i, [(d_gate[i], u)], "tn", [BF16]) for i in range(3)]

    dq_s, dk_s, dv_s = _sb_bwd(*sb, _to_heads(d_o[0], N_SB_HEADS), rtot, HEAD_DIM ** -0.5)
    dq_f, dk_f, dv_f, dc, dc_rows = _attn_bwd("fox_bwd", *fx, o_fx, _to_heads(d_o[1], N_FOX_HEADS), lse_fx,
                                     HEAD_DIM ** -0.5, c)
    dq_m, dk_m, dv_m = _attn_bwd("mem_bwd", qm, km, vm, o_mem, _to_heads(d_o[2], N_MEM_HEADS), lse_mem,
                                 qm.shape[-1] ** -0.5)
    dfl, d_bforget = _decay_bwd(dc, dc_rows, fl, b_forget.reshape(-1, 1))
    dproj = jnp.concatenate(
        [_from_heads(t).astype(BF16) for t in (dq_s, dk_s, dv_s, dq_f, dk_f, dv_f)]
        + [dfl.T.astype(BF16), _from_heads(dq_m).astype(BF16)], axis=1)
    dprojp = _pad_proj(lay, dproj)
    du = _mm("mix_du", list(zip(d_gate, w["gate3"])) + [(dprojp, w["win"])], "nn", [F32], _sum_accs, tm=256)
    d_win = _mm("mix_dwin", [(dprojp, u)], "tn", [BF16])
    dh_in, d_pre = _rms_bwd("mix_dnorm", h, pre_g, du, F32, res=dh)

    dkvm = jnp.concatenate([_from_heads(dk_m), _from_heads(dv_m)], axis=1).astype(BF16)
    d_wkv = _mm("mem_dwkv", [(mem_n, dkvm)], "tn", [BF16])
    dmem_n = _mm("mem_dn", [(dkvm, w["kv"])], "nt", [F32], lambda accs, ex: [accs[0] + ex[0]], [(dmem_n, 0)])
    grads = {"win": d_win, "kv": d_wkv, "gate": jnp.concatenate(d_wgate, axis=0),
             "br": jnp.concatenate(d_wbr, axis=1), "out": d_wout}
    return dh_in, d_pre, d_post, d_bforget, d_bgate, grads, dmem_n


def _layer_fwd(lay, h, w, sp, mem_n):
    h1, s1 = _ffn_fwd("1", h, sp["ffn1_pre_g"], sp["ffn1_post_g"], w["g1"], w["u1"], w["d1"])
    h2, s2 = _mix_fwd(lay, h1, w, sp["mix_pre_g"], sp["mix_post_g"], sp["b_forget"], sp["b_gate"], mem_n)
    h3, s3 = _ffn_fwd("2", h2, sp["ffn2_pre_g"], sp["ffn2_post_g"], w["g2"], w["u2"], w["d2"])
    return h3, (s1, s2, s3)


def _layer_bwd(lay, dh, saved, w, sp, mem_n, dmem_n):
    s1, s2, s3 = saved
    dh, d_pre2, d_post2, d_g2, d_u2, d_d2 = _ffn_bwd("2", dh, s3, sp["ffn2_pre_g"], sp["ffn2_post_g"],
                                                     w["g2"], w["u2"], w["d2"])
    dh, d_mpre, d_mpost, d_bforget, d_bgate, g, dmem_n = _mix_bwd(
        lay, dh, s2, w, sp["mix_pre_g"], sp["mix_post_g"], sp["b_forget"], mem_n, dmem_n)
    dh, d_pre1, d_post1, d_g1, d_u1, d_d1 = _ffn_bwd("1", dh, s1, sp["ffn1_pre_g"], sp["ffn1_post_g"],
                                                     w["g1"], w["u1"], w["d1"])
    g.update({"g1": d_g1, "u1": d_u1, "d1": d_d1, "g2": d_g2, "u2": d_u2, "d2": d_d2})
    g["br"] = g["br"].reshape(N_DEV, lay.br_shard, lay.br_in).reshape(-1, lay.D)
    packed = jnp.concatenate([lay.spread(g[key], key) for key in lay.seg], axis=1)
    small = {"ffn1_pre_g": d_pre1, "ffn1_post_g": d_post1, "mix_pre_g": d_mpre, "mix_post_g": d_mpost,
             "ffn2_pre_g": d_pre2, "ffn2_post_g": d_post2, "b_gate": d_bgate, "b_forget": d_bforget}
    return dh, packed, small, dmem_n


_SHARDED = ["ffn1_w_gate", "ffn1_w_up", "ffn1_w_down", "w_in", "w_mem_kv", "w_gate", "w_br_sb", "w_br_fox",
            "w_br_mem", "w_out", "ffn2_w_gate", "ffn2_w_up", "ffn2_w_down"]
_SMALL_LAYER = ["ffn1_pre_g", "ffn1_post_g", "mix_pre_g", "mix_post_g", "ffn2_pre_g", "ffn2_post_g", "b_gate",
                "b_forget"]
_WEIGHTS = ["ffn1_pre_g", "ffn1_post_g", "ffn1_w_gate", "ffn1_w_up", "ffn1_w_down", "mix_pre_g", "mix_post_g",
            "w_in", "b_forget", "mem_norm_g", "w_mem_kv", "w_gate", "b_gate", "w_br_sb", "w_br_fox", "w_br_mem",
            "w_out", "ffn2_pre_g", "ffn2_post_g", "ffn2_w_gate", "ffn2_w_up", "ffn2_w_down"]


def _pack_small(vals, L, D):
    rows = []
    for l in range(L):
        for name in _SMALL_LAYER:
            t = vals[name][l]
            rows.append(jnp.pad(t, (0, -t.shape[0] % D)).reshape(-1, D))
    rows.append(vals["mem_norm_g"].reshape(1, D))
    packed = jnp.concatenate(rows, axis=0)
    return _pad_rows(packed, -(-packed.shape[0] // 8) * 8)


def _unpack_small(packed, shapes, L, D):
    out = {name: [] for name in _SMALL_LAYER}
    r = 0
    for l in range(L):
        for name in _SMALL_LAYER:
            n = shapes[name][1]
            nr = -(-n // D)
            out[name].append(packed[r:r + nr].reshape(-1)[:n])
            r += nr
    res = {name: jnp.stack(v) for name, v in out.items()}
    res["mem_norm_g"] = packed[r]
    return res


def _unpack_grads(lay, g, l_shapes):
    def seg(key):
        off, n = lay.seg[key]
        return g[off:off + n]
    br = seg("br").reshape(lay.br_shard, lay.br_in).T
    third = lay.br_in // 3
    return {
        "ffn1_w_gate": seg("g1").T, "ffn1_w_up": seg("u1").T, "ffn1_w_down": seg("d1"),
        "w_in": seg("win")[:lay.in_shard].T, "w_mem_kv": seg("kv"), "w_gate": seg("gate").T,
        "w_br_sb": br[:third], "w_br_fox": br[third:2 * third], "w_br_mem": br[2 * third:],
        "w_out": seg("out"), "ffn2_w_gate": seg("g2").T, "ffn2_w_up": seg("u2").T, "ffn2_w_down": seg("d2"),
    }


def _step(p, m, v, x, mem, tgt, gather, reduce_scatter, loss_sum):
    L, D = p["ffn1_pre_g"].shape
    lay = _Layout(D, p["ffn1_w_gate"].shape[2], p["w_in"].shape[2], p["w_mem_kv"].shape[1], p["w_gate"].shape[2],
                  3 * p["w_br_sb"].shape[1], p["w_br_sb"].shape[2], p["w_out"].shape[1])
    ws = [_unpack_layer(lay, gather("ag_weights", _pack_layer(lay, l, p))) for l in range(L)]
    sps = [{name: p[name][l] for name in _SMALL_LAYER} for l in range(L)]

    mem_n = _rms_fwd("mem_norm", mem, p["mem_norm_g"], BF16)
    h, saved = x, []
    for l in range(L):
        h, s = _layer_fwd(lay, h, ws[l], sps[l], mem_n)
        saved.append(s)
    loss_part, dh = _loss_grad(h, tgt)
    loss = loss_sum(loss_part)

    dmem_n = jnp.zeros(mem.shape, F32)
    big, small = [None] * L, {name: [None] * L for name in _SMALL_LAYER}
    for l in reversed(range(L)):
        dh, packed, sm, dmem_n = _layer_bwd(lay, dh, saved[l], ws[l], sps[l], mem_n, dmem_n)
        big[l] = _unpack_grads(lay, reduce_scatter("w", packed), None)
        for name in _SMALL_LAYER:
            small[name][l] = sm[name]
    _, d_memg = _rms_bwd("mem_dnorm", mem, p["mem_norm_g"], dmem_n, F32)

    small_g = {name: jnp.stack(vs) for name, vs in small.items()}
    small_g["mem_norm_g"] = d_memg
    small_names = _SMALL_LAYER + ["mem_norm_g"]
    shapes = {name: p[name].shape for name in small_names}
    g_all = gather("ag_small", _pack_small(small_g, L, D))
    packs = [_pack_small({name: t[name] for name in small_names}, L, D) for t in (p, m, v)]
    res = [_unpack_small(t, shapes, L, D) for t in _adamw("adamw_small", g_all, *packs)]

    out = {kind: {} for kind in ("grad", "delta", "new_m", "new_v")}
    for name in small_names:
        for kind, r in zip(("grad", "delta", "new_m", "new_v"), res):
            out[kind][name] = r[name].reshape(p[name].shape)
    for name in _SHARDED:
        g = jnp.stack([big[l][name] for l in range(L)])
        shp = g.shape
        flat = lambda t: t.reshape(-1, shp[-1])
        r = _adamw("adamw_" + name, flat(g)[None], flat(p[name]), flat(m[name]), flat(v[name]))
        for kind, t in zip(("grad", "delta", "new_m", "new_v"), r):
            out[kind][name] = t.reshape(shp)
    return loss, dh, out


def kernel(x, mem, ffn1_pre_g, ffn1_post_g, ffn1_w_gate, ffn1_w_up, ffn1_w_down, mix_pre_g, mix_post_g, w_in, b_forget, mem_norm_g, w_mem_kv, w_gate, b_gate, w_br_sb, w_br_fox, w_br_mem, w_out, ffn2_pre_g, ffn2_post_g, ffn2_w_gate, ffn2_w_up, ffn2_w_down, loss_target, m_ffn1_pre_g, m_ffn1_post_g, m_ffn1_w_gate, m_ffn1_w_up, m_ffn1_w_down, m_mix_pre_g, m_mix_post_g, m_w_in, m_b_forget, m_mem_norm_g, m_w_mem_kv, m_w_gate, m_b_gate, m_w_br_sb, m_w_br_fox, m_w_br_mem, m_w_out, m_ffn2_pre_g, m_ffn2_post_g, m_ffn2_w_gate, m_ffn2_w_up, m_ffn2_w_down, v_ffn1_pre_g, v_ffn1_post_g, v_ffn1_w_gate, v_ffn1_w_up, v_ffn1_w_down, v_mix_pre_g, v_mix_post_g, v_w_in, v_b_forget, v_mem_norm_g, v_w_mem_kv, v_w_gate, v_b_gate, v_w_br_sb, v_w_br_fox, v_w_br_mem, v_w_out, v_ffn2_pre_g, v_ffn2_post_g, v_ffn2_w_gate, v_ffn2_w_up, v_ffn2_w_down):
    p = dict(zip(_WEIGHTS, (ffn1_pre_g, ffn1_post_g, ffn1_w_gate, ffn1_w_up, ffn1_w_down, mix_pre_g, mix_post_g, w_in, b_forget, mem_norm_g, w_mem_kv, w_gate, b_gate, w_br_sb, w_br_fox, w_br_mem, w_out, ffn2_pre_g, ffn2_post_g, ffn2_w_gate, ffn2_w_up, ffn2_w_down)))
    m = dict(zip(_WEIGHTS, (m_ffn1_pre_g, m_ffn1_post_g, m_ffn1_w_gate, m_ffn1_w_up, m_ffn1_w_down, m_mix_pre_g, m_mix_post_g, m_w_in, m_b_forget, m_mem_norm_g, m_w_mem_kv, m_w_gate, m_b_gate, m_w_br_sb, m_w_br_fox, m_w_br_mem, m_w_out, m_ffn2_pre_g, m_ffn2_post_g, m_ffn2_w_gate, m_ffn2_w_up, m_ffn2_w_down)))
    v = dict(zip(_WEIGHTS, (v_ffn1_pre_g, v_ffn1_post_g, v_ffn1_w_gate, v_ffn1_w_up, v_ffn1_w_down, v_mix_pre_g, v_mix_post_g, v_w_in, v_b_forget, v_mem_norm_g, v_w_mem_kv, v_w_gate, v_b_gate, v_w_br_sb, v_w_br_fox, v_w_br_mem, v_w_out, v_ffn2_pre_g, v_ffn2_post_g, v_ffn2_w_gate, v_ffn2_w_up, v_ffn2_w_down)))
    loss, dx, out = _step(p, m, v, x[0], mem[0], loss_target[0], _all_gather, _reduce_scatter,
                          lambda part: lax.psum(part, ("x", "y", "c")))
    return (loss, dx[None], *[out["grad"][n] for n in _WEIGHTS], *[out["delta"][n] for n in _WEIGHTS],
            *[out["new_m"][n] for n in _WEIGHTS], *[out["new_v"][n] for n in _WEIGHTS])
```

```python
import functools
import math

import jax
import jax.numpy as jnp
from jax import lax
from jax.experimental import pallas as pl
from jax.experimental.pallas import tpu as pltpu

F32 = jnp.float32
BF16 = jnp.bfloat16

LANE = 128
SUBLANE_BF16 = 16
VMEM_LIMIT = 48 * 1024 * 1024
N_DEV = 8
MESH = pl.DeviceIdType.MESH
ANY = pl.BlockSpec(memory_space=pl.ANY)

RMS_EPS = 1e-6
HEAD_DIM = 64
N_SB_HEADS = 8
N_FOX_HEADS = 8
N_MEM_HEADS = 4
NEG = -1e30
ATT_TQ = 512
ATT_TK = 256
DECAY_TK = 128

ADAM_LR = 0.001
ADAM_B1 = 0.9
ADAM_B2 = 0.999
ADAM_EPS = 1e-08
ADAM_WD = 0.01
ADAM_STEP = 10


def _tile(n, target, mult=LANE):
    best = None
    for t in range(mult, min(n, target) + 1, mult):
        if n % t == 0:
            best = t
    return best if best is not None else n


def _cparams(sem):
    return pltpu.CompilerParams(dimension_semantics=sem, vmem_limit_bytes=VMEM_LIMIT)


_DIMS = {"nn": (((1,), (0,)), ((), ())), "nt": (((1,), (1,)), ((), ())), "tn": (((0,), (0,)), ((), ()))}


def _dot(a, b, mode="nn"):
    return lax.dot_general(a.astype(BF16), b.astype(BF16), _DIMS[mode], preferred_element_type=F32)


def _mm(name, pairs, mode, out_dtypes, epilogue=None, extras=(), tm=512, tn=512):
    a0, b0 = pairs[0]
    M = a0.shape[1] if mode == "tn" else a0.shape[0]
    N = b0.shape[0] if mode == "nt" else b0.shape[1]
    tm = _tile(M, tm)
    tn = _tile(N, tn)
    np_, ne, no = len(pairs), len(extras), len(out_dtypes)

    def body(*refs):
        a_refs, b_refs = refs[:np_], refs[np_:2 * np_]
        e_refs = refs[2 * np_:2 * np_ + ne]
        o_refs = refs[2 * np_ + ne:]
        accs = [_dot(a[...], b[...], mode) for a, b in zip(a_refs, b_refs)]
        outs = epilogue(accs, [e[...] for e in e_refs]) if epilogue is not None else accs
        for o, val in zip(o_refs, outs):
            o[...] = val.astype(o.dtype)

    in_specs = []
    for a, _ in pairs:
        if mode == "tn":
            in_specs.append(pl.BlockSpec((a.shape[0], tm), lambda j, i: (0, i)))
        else:
            in_specs.append(pl.BlockSpec((tm, a.shape[1]), lambda j, i: (i, 0)))
    for _, b in pairs:
        if mode == "nt":
            in_specs.append(pl.BlockSpec((tn, b.shape[1]), lambda j, i: (j, 0)))
        else:
            in_specs.append(pl.BlockSpec((b.shape[0], tn), lambda j, i: (0, j)))
    for e, off in extras:
        if e.shape[0] == 1:
            in_specs.append(pl.BlockSpec((1, tn), functools.partial(lambda j, i, o: (0, j + o), o=off // tn)))
        else:
            in_specs.append(pl.BlockSpec((tm, tn), functools.partial(lambda j, i, o: (i, j + o), o=off // tn)))
    out_specs = [pl.BlockSpec((tm, tn), lambda j, i: (i, j)) for _ in range(no)]
    outs = pl.pallas_call(
        body, name=name, grid=(N // tn, M // tm),
        in_specs=in_specs, out_specs=out_specs,
        out_shape=[jax.ShapeDtypeStruct((M, N), dt) for dt in out_dtypes],
        compiler_params=_cparams(("parallel", "parallel")),
    )(*[a for a, _ in pairs], *[b for _, b in pairs], *[e for e, _ in extras])
    return outs[0] if no == 1 else outs


def _sum_accs(accs, _):
    total = accs[0]
    for acc in accs[1:]:
        total = total + acc
    return [total]


def _rstd(x):
    return lax.rsqrt(jnp.mean(x * x, axis=-1, keepdims=True) + RMS_EPS)


def _rms_fwd(name, x, g, out_dtype, res=None, scale=1.0, tr=512):
    R, D = x.shape
    tr = _tile(R, tr, 8)
    has_res = res is not None

    def body(*refs):
        x_ref, g_ref = refs[:2]
        o_ref = refs[-1]
        xv = x_ref[...]
        y = (xv * _rstd(xv)) * g_ref[...]
        if has_res:
            y = refs[2][...] + scale * y
        o_ref[...] = y.astype(o_ref.dtype)

    row = pl.BlockSpec((tr, D), lambda i: (i, 0))
    gain = pl.BlockSpec((1, D), lambda i: (0, 0))
    return pl.pallas_call(
        body, name=name, grid=(R // tr,),
        in_specs=[row, gain] + ([row] if has_res else []), out_specs=row,
        out_shape=jax.ShapeDtypeStruct((R, D), out_dtype),
        compiler_params=_cparams(("parallel",)),
    )(x, g.reshape(1, D), *([res] if has_res else []))


def _rms_bwd(name, x, g, dy, out_dtype, scale=1.0, res=None, tr=512):
    R, D = x.shape
    tr = _tile(R, tr, 8)
    has_res = res is not None

    def body(*refs):
        x_ref, g_ref, dy_ref = refs[:3]
        dx_ref, dg_ref = refs[-2:]
        i = pl.program_id(0)
        xv = x_ref[...]
        xhat = xv * _rstd(xv)
        dyv = dy_ref[...].astype(F32) * scale
        gy = dyv * g_ref[...]
        dx = _rstd(xv) * (gy - xhat * jnp.mean(gy * xhat, axis=-1, keepdims=True))
        if has_res:
            dx = refs[3][...] + dx
        dx_ref[...] = dx.astype(dx_ref.dtype)
        part = jnp.sum(dyv * xhat, axis=0, keepdims=True)

        @pl.when(i == 0)
        def _():
            dg_ref[...] = part

        @pl.when(i > 0)
        def _():
            dg_ref[...] += part

    row = pl.BlockSpec((tr, D), lambda i: (i, 0))
    gain = pl.BlockSpec((1, D), lambda i: (0, 0))
    dx, dg = pl.pallas_call(
        body, name=name, grid=(R // tr,),
        in_specs=[row, gain, row] + ([row] if has_res else []), out_specs=[row, gain],
        out_shape=[jax.ShapeDtypeStruct((R, D), out_dtype), jax.ShapeDtypeStruct((1, D), F32)],
        compiler_params=_cparams(("arbitrary",)),
    )(x, g.reshape(1, D), dy, *([res] if has_res else []))
    return dx, dg[0]


def _loss_grad(y, tgt, tr=512):
    R, D = y.shape
    tr = _tile(R, tr, 8)

    def body(y_ref, t_ref, dy_ref, loss_ref):
        i = pl.program_id(0)
        d = y_ref[...] - t_ref[...]
        dy_ref[...] = d / D
        part = 0.5 * jnp.sum(jnp.mean(d * d, axis=-1, keepdims=True), axis=0, keepdims=True)
        tile = jnp.broadcast_to(part, loss_ref.shape)

        @pl.when(i == 0)
        def _():
            loss_ref[...] = tile

        @pl.when(i > 0)
        def _():
            loss_ref[...] += tile

    row = pl.BlockSpec((tr, D), lambda i: (i, 0))
    dy, loss = pl.pallas_call(
        body, name="loss_grad", grid=(R // tr,),
        in_specs=[row, row], out_specs=[row, pl.BlockSpec((8, LANE), lambda i: (0, 0))],
        out_shape=[jax.ShapeDtypeStruct((R, D), F32), jax.ShapeDtypeStruct((8, LANE), F32)],
        compiler_params=_cparams(("arbitrary",)),
    )(y, tgt)
    return loss[0, 0], dy


def _colsum(name, x, tr=512, tn=1024):
    R, N = x.shape
    tr, tn = _tile(R, tr, 8), _tile(N, tn)

    def body(x_ref, o_ref):
        i = pl.program_id(1)
        part = jnp.sum(x_ref[...].astype(F32), axis=0, keepdims=True)

        @pl.when(i == 0)
        def _():
            o_ref[...] = part

        @pl.when(i > 0)
        def _():
            o_ref[...] += part

    out = pl.pallas_call(
        body, name=name, grid=(N // tn, R // tr),
        in_specs=[pl.BlockSpec((tr, tn), lambda j, i: (i, j))], out_specs=pl.BlockSpec((1, tn), lambda j, i: (0, j)),
        out_shape=jax.ShapeDtypeStruct((1, N), F32),
        compiler_params=_cparams(("parallel", "arbitrary")),
    )(x)
    return out[0]


def _tri(tk, rel):
    j = lax.broadcasted_iota(jnp.int32, (tk, tk), 0)
    s = lax.broadcasted_iota(jnp.int32, (tk, tk), 1)
    return rel(j, s).astype(BF16)


def _dot_split(x, m, parts=2):
    total = None
    rem = x
    for _ in range(parts):
        piece = rem.astype(BF16)
        rem = rem - piece.astype(F32)
        term = jnp.dot(piece, m, preferred_element_type=F32)
        total = term if total is None else total + term
    return total


def _log_not_and_beta(z, mask):
    ln = -(jnp.maximum(z, 0.0) + jnp.log(1.0 + jnp.exp(-jnp.abs(z))))
    return (ln if mask is None else jnp.where(mask, ln, 0.0)), ln + z


def _att_tiles(T, Tk, causal):
    tq = min(ATT_TQ, T)
    tk = min(ATT_TK, tq if causal else Tk)
    return tq, tk, (tq if causal else Tk) // tk


def _key_base(j, tq):
    return j * tq if isinstance(j, int) else pl.multiple_of(j * tq, tq)


def _is_pow2(scale):
    return math.log2(scale).is_integer()


def _sb_fwd(q, k, v, scale):
    H, T, d = q.shape
    tq, tk, nsub = _att_tiles(T, T, True)
    assert _is_pow2(scale)

    def body(q_ref, k_ref, v_ref, o_ref, rt_ref, acc_ref, r_ref):
        qi = pl.program_id(1)
        qv = q_ref[0] * scale
        acc_ref[...] = jnp.zeros_like(acc_ref)
        r_ref[...] = jnp.zeros_like(r_ref)
        row = lax.broadcasted_iota(jnp.int32, (tq, tk), 0)
        col = lax.broadcasted_iota(jnp.int32, (tq, tk), 1)
        after = _tri(tk, lambda j, s: j > s)

        def step(j, diagonal):
            base = _key_base(j, tq)
            parts = []
            for u in reversed(range(nsub)):
                z = _dot(qv, k_ref[0, pl.ds(base + u * tk, tk), :], "nt")
                mask = (col + u * tk) < row if diagonal else None
                ln, lb = _log_not_and_beta(z, mask)
                between = _dot_split(ln, after)
                parts.append((u, lb, between, between[:, 0:1] + ln[:, 0:1], mask))
            r = r_ref[...]
            out = None
            for u, lb, between, total, mask in parts:
                w = jnp.exp(lb + between + r)
                if diagonal:
                    w = jnp.where(mask, w, 0.0)
                term = _dot(w, v_ref[0, pl.ds(base + u * tk, tk), :])
                out = term if out is None else out + term
                r = r + total
            acc_ref[...] += out
            r_ref[...] = r

        def below(i, carry):
            step(qi - 1 - i, False)
            return carry

        step(qi, True)
        lax.fori_loop(0, qi, below, 0)
        o_ref[0] = acc_ref[...]
        rt_ref[0] = r_ref[...]

    blk = pl.BlockSpec((1, tq, d), lambda h, i: (h, i, 0))
    full = pl.BlockSpec((1, T, d), lambda h, i: (h, 0, 0))
    col = pl.BlockSpec((1, tq, 1), lambda h, i: (h, i, 0))
    return pl.pallas_call(
        body, name="sb_fwd", grid=(H, T // tq),
        in_specs=[blk, full, full], out_specs=[blk, col],
        out_shape=[jax.ShapeDtypeStruct((H, T, d), F32), jax.ShapeDtypeStruct((H, T, 1), F32)],
        scratch_shapes=[pltpu.VMEM((tq, d), F32), pltpu.VMEM((tq, 1), F32)],
        compiler_params=_cparams(("parallel", "arbitrary")),
    )(q, k, v)


def _sb_bwd(q, k, v, do, rtot, scale):
    H, T, d = q.shape
    tq, tk, nsub = _att_tiles(T, T, True)
    assert _is_pow2(scale)

    def body(q_ref, k_ref, v_ref, do_ref, rt_ref, dq_ref, dk_ref, dv_ref, dq_acc, p_ref, c_ref):
        qi = pl.program_id(1)

        @pl.when(qi == 0)
        def _():
            dk_ref[...] = jnp.zeros_like(dk_ref)
            dv_ref[...] = jnp.zeros_like(dv_ref)

        qv = q_ref[0] * scale
        dov = do_ref[0]
        rt = rt_ref[0]
        dq_acc[...] = jnp.zeros_like(dq_acc)
        p_ref[...] = jnp.zeros_like(p_ref)
        c_ref[...] = jnp.zeros_like(c_ref)
        row = lax.broadcasted_iota(jnp.int32, (tq, tk), 0)
        col = lax.broadcasted_iota(jnp.int32, (tq, tk), 1)
        upto = _tri(tk, lambda j, s: j <= s)
        before = _tri(tk, lambda j, s: j < s)

        def step(j, diagonal):
            base = _key_base(j, tq)
            first = []
            for u in range(nsub):
                ks = base + u * tk
                kv = k_ref[0, pl.ds(ks, tk), :]
                z = _dot(qv, kv, "nt")
                mask = (col + u * tk) < row if diagonal else None
                ln, lb = _log_not_and_beta(z, mask)
                dw = _dot(dov, v_ref[0, pl.ds(ks, tk), :], "nt")
                first.append((ks, kv, mask, lb, jnp.exp(lb), _dot_split(ln, upto), dw))
            pre, cpre = p_ref[...], c_ref[...]
            dq = None
            for ks, kv, mask, lb, sig, local, dw in first:
                prefix = local + pre
                w = jnp.exp(lb + (rt - prefix))
                if diagonal:
                    w = jnp.where(mask, w, 0.0)
                g = dw * w
                c = _dot_split(g, before) + cpre
                dz = g * (1.0 - sig) - c * sig
                if diagonal:
                    dz = jnp.where(mask, dz, 0.0)
                term = _dot(dz, kv)
                dq = term if dq is None else dq + term
                dk_ref[0, pl.ds(ks, tk), :] += _dot(dz, qv, "tn")
                dv_ref[0, pl.ds(ks, tk), :] += _dot(w, dov, "tn")
                pre = prefix[:, tk - 1:tk]
                cpre = c[:, tk - 1:tk] + g[:, tk - 1:tk]
            dq_acc[...] += dq
            p_ref[...] = pre
            c_ref[...] = cpre

        def below(j, carry):
            step(j, False)
            return carry

        lax.fori_loop(0, qi, below, 0)
        step(qi, True)
        dq_ref[0] = (dq_acc[...] * scale).astype(dq_ref.dtype)

    blk = pl.BlockSpec((1, tq, d), lambda h, i: (h, i, 0))
    full = pl.BlockSpec((1, T, d), lambda h, i: (h, 0, 0))
    col = pl.BlockSpec((1, tq, 1), lambda h, i: (h, i, 0))
    return pl.pallas_call(
        body, name="sb_bwd", grid=(H, T // tq),
        in_specs=[blk, full, full, blk, col], out_specs=[blk, full, full],
        out_shape=[jax.ShapeDtypeStruct((H, T, d), BF16), jax.ShapeDtypeStruct((H, T, d), F32),
                   jax.ShapeDtypeStruct((H, T, d), F32)],
        scratch_shapes=[pltpu.VMEM((tq, d), F32), pltpu.VMEM((tq, 1), F32), pltpu.VMEM((tq, 1), F32)],
        compiler_params=_cparams(("parallel", "arbitrary")),
    )(q, k, v, do, rtot)


def _attn_fwd(name, q, k, v, scale, c=None):
    H, T, d = q.shape
    Tk = k.shape[1]
    causal = c is not None
    tq, tk, nsub = _att_tiles(T, Tk, causal)
    fold = _is_pow2(scale)

    def body(*refs):
        q_ref, k_ref, v_ref = refs[:3]
        cc_ref, cr_ref = refs[3:5] if causal else (None, None)
        o_ref, lse_ref, m_ref, l_ref, acc_ref = refs[-5:]
        qi = pl.program_id(1)
        qv = q_ref[0] * scale if fold else q_ref[0]
        m_ref[...] = jnp.full_like(m_ref, NEG)
        l_ref[...] = jnp.zeros_like(l_ref)
        acc_ref[...] = jnp.zeros_like(acc_ref)
        row = lax.broadcasted_iota(jnp.int32, (tq, tk), 0)
        col = lax.broadcasted_iota(jnp.int32, (tq, tk), 1)

        def step(j, diagonal):
            base = _key_base(j, tq)
            zs = []
            for u in range(nsub):
                z = _dot(qv, k_ref[0, pl.ds(base + u * tk, tk), :], "nt")
                if not fold:
                    z = z * scale
                if causal:
                    z = z + cc_ref[0] - cr_ref[0, j * nsub + u]
                if diagonal:
                    z = jnp.where((col + u * tk) <= row, z, NEG)
                zs.append(z)
            m_prev = m_ref[...]
            m_new = m_prev
            for z in zs:
                m_new = jnp.maximum(m_new, jnp.max(z, axis=1, keepdims=True))
            alpha = jnp.exp(m_prev - m_new)
            l_new = alpha * l_ref[...]
            out = alpha * acc_ref[...]
            for u, z in enumerate(zs):
                p = jnp.exp(z - m_new)
                l_new = l_new + jnp.sum(p, axis=1, keepdims=True)
                out = out + _dot(p, v_ref[0, pl.ds(base + u * tk, tk), :])
            l_ref[...] = l_new
            acc_ref[...] = out
            m_ref[...] = m_new

        def below(j, carry):
            step(j, False)
            return carry

        if causal:
            lax.fori_loop(0, qi, below, 0)
            step(qi, True)
        else:
            step(0, False)
        o_ref[0] = acc_ref[...] / l_ref[...]
        lse_ref[0] = m_ref[...] + jnp.log(l_ref[...])

    blk = pl.BlockSpec((1, tq, d), lambda h, i: (h, i, 0))
    full = pl.BlockSpec((1, Tk, d), lambda h, i: (h, 0, 0))
    col = pl.BlockSpec((1, tq, 1), lambda h, i: (h, i, 0))
    in_specs, args = [blk, full, full], [q, k, v]
    if causal:
        in_specs += [col, pl.BlockSpec((1, T // tk, 1, tk), lambda h, i: (h, 0, 0, 0))]
        args += [c.reshape(H, T, 1), c.reshape(H, T // tk, 1, tk)]
    return pl.pallas_call(
        body, name=name, grid=(H, T // tq),
        in_specs=in_specs, out_specs=[blk, col],
        out_shape=[jax.ShapeDtypeStruct((H, T, d), F32), jax.ShapeDtypeStruct((H, T, 1), F32)],
        scratch_shapes=[pltpu.VMEM((tq, 1), F32), pltpu.VMEM((tq, 1), F32), pltpu.VMEM((tq, d), F32)],
        compiler_params=_cparams(("parallel", "arbitrary")),
    )(*args)


def _attn_bwd(name, q, k, v, o, do, lse, scale, c=None):
    H, T, d = q.shape
    Tk = k.shape[1]
    causal = c is not None
    tq, tk, nsub = _att_tiles(T, Tk, causal)
    fold = _is_pow2(scale)

    def body(*refs):
        q_ref, k_ref, v_ref, o_ref, do_ref, lse_ref = refs[:6]
        cc_ref, cr_ref = refs[6:8] if causal else (None, None)
        n_out = 5 if causal else 3
        outs = refs[-(n_out + 1):-1]
        dq_ref, dk_ref, dv_ref = outs[:3]
        dc_ref, drow_ref = outs[3:5] if causal else (None, None)
        dq_acc = refs[-1]
        qi = pl.program_id(1)

        @pl.when(qi == 0)
        def _():
            dk_ref[...] = jnp.zeros_like(dk_ref)
            dv_ref[...] = jnp.zeros_like(dv_ref)
            if causal:
                dc_ref[...] = jnp.zeros_like(dc_ref)

        qv = q_ref[0] * scale if fold else q_ref[0]
        dov = do_ref[0]
        lse_v = lse_ref[0]
        delta = jnp.sum(dov.astype(F32) * o_ref[0], axis=1, keepdims=True)
        dq_acc[...] = jnp.zeros_like(dq_acc)
        if causal:
            drow_ref[...] = jnp.zeros_like(drow_ref)
        row = lax.broadcasted_iota(jnp.int32, (tq, tk), 0)
        col = lax.broadcasted_iota(jnp.int32, (tq, tk), 1)

        def step(j, diagonal):
            base = _key_base(j, tq)
            dq, drow = None, None
            for u in range(nsub):
                ks = base + u * tk
                kv = k_ref[0, pl.ds(ks, tk), :]
                z = _dot(qv, kv, "nt")
                if not fold:
                    z = z * scale
                if causal:
                    z = z + cc_ref[0] - cr_ref[0, j * nsub + u]
                if diagonal:
                    z = jnp.where((col + u * tk) <= row, z, NEG)
                p = jnp.exp(z - lse_v)
                ds = p * (_dot(dov, v_ref[0, pl.ds(ks, tk), :], "nt") - delta)
                term = _dot(ds, kv)
                dq = term if dq is None else dq + term
                dk = _dot(ds, qv, "tn")
                dk_ref[0, pl.ds(ks, tk), :] += dk if fold else dk * scale
                dv_ref[0, pl.ds(ks, tk), :] += _dot(p, dov, "tn")
                if causal:
                    dc_ref[0, j * nsub + u] -= jnp.sum(ds, axis=0, keepdims=True)
                    rs = jnp.sum(ds, axis=1, keepdims=True)
                    drow = rs if drow is None else drow + rs
            dq_acc[...] += dq
            if causal:
                drow_ref[0] += drow

        def below(j, carry):
            step(j, False)
            return carry

        if causal:
            lax.fori_loop(0, qi, below, 0)
            step(qi, True)
        else:
            step(0, False)
        dq_ref[0] = (dq_acc[...] * scale).astype(dq_ref.dtype)

    blk = pl.BlockSpec((1, tq, d), lambda h, i: (h, i, 0))
    full = pl.BlockSpec((1, Tk, d), lambda h, i: (h, 0, 0))
    col = pl.BlockSpec((1, tq, 1), lambda h, i: (h, i, 0))
    crow = pl.BlockSpec((1, T // tk, 1, tk), lambda h, i: (h, 0, 0, 0))
    in_specs, args = [blk, full, full, blk, blk, col], [q, k, v, o, do, lse]
    out_specs = [blk, full, full]
    out_shape = [jax.ShapeDtypeStruct((H, T, d), BF16), jax.ShapeDtypeStruct((H, Tk, d), F32),
                 jax.ShapeDtypeStruct((H, Tk, d), F32)]
    if causal:
        in_specs += [col, crow]
        args += [c.reshape(H, T, 1), c.reshape(H, T // tk, 1, tk)]
        out_specs += [crow, col]
        out_shape += [jax.ShapeDtypeStruct((H, T // tk, 1, tk), F32), jax.ShapeDtypeStruct((H, T, 1), F32)]
    outs = pl.pallas_call(
        body, name=name, grid=(H, T // tq),
        in_specs=in_specs, out_specs=out_specs, out_shape=out_shape,
        scratch_shapes=[pltpu.VMEM((tq, d), F32)],
        compiler_params=_cparams(("parallel", "arbitrary")),
    )(*args)
    if causal:
        return outs[0], outs[1], outs[2], outs[3].reshape(H, T), outs[4].reshape(H, T)
    return outs


def _decay_fwd(fl, b):
    H, T = fl.shape
    tk = DECAY_TK

    def body(x_ref, b_ref, c_ref):
        upto = _tri(tk, lambda j, s: j <= s)
        carry = jnp.zeros((H, 1), F32)
        for i in range(T // tk):
            xv = x_ref[:, i * tk:(i + 1) * tk] + b_ref[...]
            lf = jnp.minimum(xv, 0.0) - jnp.log(1.0 + jnp.exp(-jnp.abs(xv)))
            pref = _dot_split(lf, upto, parts=3) + carry
            c_ref[:, i * tk:(i + 1) * tk] = pref
            carry = pref[:, tk - 1:tk]

    vm = pl.BlockSpec(memory_space=pltpu.VMEM)
    return pl.pallas_call(
        body, name="decay_fwd", in_specs=[vm, vm], out_specs=vm,
        out_shape=jax.ShapeDtypeStruct((H, T), F32),
    )(fl, b)


def _decay_bwd(dc_cols, dc_rows, fl, b):
    H, T = fl.shape
    tk = DECAY_TK

    def body(dc_ref, dr_ref, x_ref, b_ref, dx_ref, db_ref):
        from_ = _tri(tk, lambda j, s: j >= s)
        carry = jnp.zeros((H, 1), F32)
        total = jnp.zeros((H, 1), F32)
        for i in reversed(range(T // tk)):
            sl = slice(i * tk, (i + 1) * tk)
            suffix = _dot_split(dc_ref[:, sl] + dr_ref[:, sl], from_, parts=3) + carry
            xv = x_ref[:, sl] + b_ref[...]
            dx = suffix / (1.0 + jnp.exp(xv))
            dx_ref[:, sl] = dx
            total = total + jnp.sum(dx, axis=1, keepdims=True)
            carry = suffix[:, 0:1]
        db_ref[...] = jnp.broadcast_to(total, db_ref.shape)

    vm = pl.BlockSpec(memory_space=pltpu.VMEM)
    dx, db = pl.pallas_call(
        body, name="decay_bwd", in_specs=[vm, vm, vm, vm], out_specs=[vm, vm],
        out_shape=[jax.ShapeDtypeStruct((H, T), F32), jax.ShapeDtypeStruct((H, LANE), F32)],
    )(dc_cols, dc_rows, fl, b)
    return dx, db[:, 0]


def _place():
    x, y, c = lax.axis_index("x"), lax.axis_index("y"), lax.axis_index("c")
    return x, y, c, [(1 - x, y), (x, 1 - y), (1 - x, 1 - y)]


def _all_gather(name, block):
    R, C = block.shape

    def body(x_ref, out_ref, send_sems, recv_sems, local_sem):
        x, y, c, chips = _place()
        me, sibling = (x, y, c), (x, y, 1 - c)

        def rows(px, py, pc):
            return out_ref.at[4 * px + 2 * py + pc]

        def copy(k, blk, to, src=None):
            return pltpu.make_async_remote_copy(
                src_ref=rows(*blk) if src is None else src, dst_ref=rows(*blk),
                send_sem=send_sems.at[k], recv_sem=recv_sems.at[k], device_id=to, device_id_type=MESH)

        mine = pltpu.make_async_copy(x_ref, rows(*me), local_sem)
        mine.start()
        first = [copy(0, me, sibling, src=x_ref)]
        first += [copy(1 + j, me, (*chip, c), src=x_ref) for j, chip in enumerate(chips)]
        for cp in first:
            cp.start()
        passed = [copy(4 + j, (*chip, c), sibling) for j, chip in enumerate(chips)]
        for j, chip in enumerate(chips):
            copy(1 + j, (*chip, c), me).wait_recv()
            passed[j].start()
        copy(0, sibling, me).wait_recv()
        for j, chip in enumerate(chips):
            copy(4 + j, (*chip, 1 - c), me).wait_recv()
        for cp in first + passed:
            cp.wait_send()
        mine.wait()

    return pl.pallas_call(
        body, name=name, in_specs=[ANY], out_specs=ANY,
        out_shape=jax.ShapeDtypeStruct((N_DEV, R, C), block.dtype),
        scratch_shapes=[pltpu.SemaphoreType.DMA((7,)), pltpu.SemaphoreType.DMA((7,)), pltpu.SemaphoreType.DMA(())],
    )(block)


def _swap_with_sibling(name, parts):
    _, R, C = parts.shape

    def body(p_ref, out_ref, send_sems, recv_sems):
        x, y, c, _ = _place()
        copies = [pltpu.make_async_remote_copy(
            src_ref=p_ref.at[2 * q + (1 - c)], dst_ref=out_ref.at[q],
            send_sem=send_sems.at[q], recv_sem=recv_sems.at[q], device_id=(x, y, 1 - c), device_id_type=MESH)
            for q in range(4)]
        for cp in copies:
            cp.start()
        for cp in copies:
            cp.wait_recv()
        for cp in copies:
            cp.wait_send()

    return pl.pallas_call(
        body, name=name, in_specs=[ANY], out_specs=ANY,
        out_shape=jax.ShapeDtypeStruct((4, R, C), parts.dtype),
        scratch_shapes=[pltpu.SemaphoreType.DMA((4,)), pltpu.SemaphoreType.DMA((4,))],
    )(parts)


def _add_own(name, parts, got, tr=512):
    _, R, C = parts.shape
    tr = _tile(R, tr, SUBLANE_BF16)

    def body(c_ref, p_ref, g_ref, o_ref):
        o_ref[...] = (p_ref[...].astype(F32) + g_ref[...].astype(F32)).astype(o_ref.dtype)

    return pl.pallas_call(
        body, name=name,
        grid_spec=pltpu.PrefetchScalarGridSpec(
            num_scalar_prefetch=1, grid=(4, R // tr),
            in_specs=[pl.BlockSpec((1, tr, C), lambda q, i, c: (2 * q + c[0], i, 0)),
                      pl.BlockSpec((1, tr, C), lambda q, i, c: (q, i, 0))],
            out_specs=pl.BlockSpec((1, tr, C), lambda q, i, c: (q, i, 0))),
        out_shape=jax.ShapeDtypeStruct((4, R, C), parts.dtype),
        compiler_params=_cparams(("parallel", "parallel")),
    )(lax.axis_index("c").astype(jnp.int32).reshape(1), parts, got)


def _swap_with_chips(name, parts):
    _, R, C = parts.shape

    def body(p_ref, out_ref, send_sems, recv_sems, local_sem):
        x, y, c, chips = _place()
        my_chip = 2 * x + y
        mine = pltpu.make_async_copy(p_ref.at[my_chip], out_ref.at[my_chip], local_sem)
        mine.start()
        sends = [pltpu.make_async_remote_copy(
            src_ref=p_ref.at[2 * cx + cy], dst_ref=out_ref.at[my_chip],
            send_sem=send_sems.at[j], recv_sem=recv_sems.at[j], device_id=(cx, cy, c), device_id_type=MESH)
            for j, (cx, cy) in enumerate(chips)]
        for cp in sends:
            cp.start()
        for j, (cx, cy) in enumerate(chips):
            pltpu.make_async_remote_copy(
                src_ref=p_ref.at[my_chip], dst_ref=out_ref.at[2 * cx + cy],
                send_sem=send_sems.at[j], recv_sem=recv_sems.at[j], device_id=(cx, cy, c), device_id_type=MESH,
            ).wait_recv()
        for cp in sends:
            cp.wait_send()
        mine.wait()

    return pl.pallas_call(
        body, name=name, in_specs=[ANY], out_specs=ANY,
        out_shape=jax.ShapeDtypeStruct((4, R, C), parts.dtype),
        scratch_shapes=[pltpu.SemaphoreType.DMA((3,)), pltpu.SemaphoreType.DMA((3,)), pltpu.SemaphoreType.DMA(())],
    )(parts)


def _sum_parts(name, parts, tr=512):
    P, R, C = parts.shape
    tr = _tile(R, tr, SUBLANE_BF16)

    def body(p_ref, o_ref):
        total = p_ref[0].astype(F32)
        for p in range(1, P):
            total = total + p_ref[p].astype(F32)
        o_ref[...] = total

    return pl.pallas_call(
        body, name=name, grid=(R // tr,),
        in_specs=[pl.BlockSpec((P, tr, C), lambda i: (0, i, 0))], out_specs=pl.BlockSpec((tr, C), lambda i: (i, 0)),
        out_shape=jax.ShapeDtypeStruct((R, C), F32),
        compiler_params=_cparams(("parallel",)),
    )(parts)


def _reduce_scatter(tag, parts):
    got = _swap_with_sibling("rs_pair_" + tag, parts)
    pair = _add_own("rs_add_" + tag, parts, got)
    quad = _swap_with_chips("rs_chips_" + tag, pair)
    return _sum_parts("rs_sum_" + tag, quad)


def _adamw(name, g_parts, w, m, v, tr=512):
    P, R, C = g_parts.shape
    tr = _tile(R, tr, 8)

    def body(g_ref, w_ref, m_ref, v_ref, go_ref, d_ref, mo_ref, vo_ref):
        g = g_ref[0]
        for p in range(1, P):
            g = g + g_ref[p]
        mn = ADAM_B1 * m_ref[...] + (1.0 - ADAM_B1) * g
        vn = ADAM_B2 * v_ref[...] + (1.0 - ADAM_B2) * (g * g)
        m_hat = mn / (1.0 - ADAM_B1 ** ADAM_STEP)
        v_hat = vn / (1.0 - ADAM_B2 ** ADAM_STEP)
        go_ref[...] = g
        d_ref[...] = -ADAM_LR * (m_hat / (jnp.sqrt(v_hat) + ADAM_EPS) + ADAM_WD * w_ref[...])
        mo_ref[...] = mn
        vo_ref[...] = vn

    row = pl.BlockSpec((tr, C), lambda i: (i, 0))
    return pl.pallas_call(
        body, name=name, grid=(R // tr,),
        in_specs=[pl.BlockSpec((P, tr, C), lambda i: (0, i, 0)), row, row, row], out_specs=[row] * 4,
        out_shape=[jax.ShapeDtypeStruct((R, C), F32)] * 4,
        compiler_params=_cparams(("parallel",)),
    )(g_parts, w, m, v)


def _to_heads(t, n_heads):
    rows = t.shape[0]
    return t.reshape(rows, n_heads, -1).transpose(1, 0, 2)


def _from_heads(t):
    return t.transpose(1, 0, 2).reshape(t.shape[1], -1)


def _pad_rows(t, rows):
    return jnp.pad(t, ((0, rows - t.shape[0]), (0, 0)))


class _Layout:
    def __init__(self, D, ff_shard, in_shard, kv_shard, gate_shard, br_in, br_shard, out_shard):
        self.D = D
        self.in_shard = in_shard
        self.in_pad = -(-in_shard // LANE) * LANE
        self.br_in, self.br_shard = br_in, br_shard
        br_rows = br_shard * br_in // D
        sizes = [("g1", ff_shard), ("u1", ff_shard), ("d1", ff_shard), ("win", self.in_pad), ("kv", kv_shard),
                 ("gate", gate_shard), ("br", br_rows), ("out", out_shard),
                 ("g2", ff_shard), ("u2", ff_shard), ("d2", ff_shard)]
        self.seg, off = {}, 0
        for key, n in sizes:
            assert n % SUBLANE_BF16 == 0, (key, n)
            self.seg[key] = (off, n)
            off += n
        self.rows = off

    def pack(self, parts):
        return jnp.concatenate([parts[key] for key in self.seg], axis=0)

    def take(self, gathered, key):
        off, n = self.seg[key]
        return gathered[:, off:off + n, :].reshape(N_DEV * n, self.D)

    def spread(self, full, key):
        _, n = self.seg[key]
        return full.reshape(N_DEV, n, self.D)


def _pack_layer(lay, l, p):
    D = lay.D
    br = jnp.concatenate([p["w_br_sb"][l], p["w_br_fox"][l], p["w_br_mem"][l]], axis=0)
    parts = {
        "g1": p["ffn1_w_gate"][l].T, "u1": p["ffn1_w_up"][l].T, "d1": p["ffn1_w_down"][l],
        "win": _pad_rows(p["w_in"][l].T, lay.in_pad), "kv": p["w_mem_kv"][l], "gate": p["w_gate"][l].T,
        "br": br.T.reshape(-1, D), "out": p["w_out"][l],
        "g2": p["ffn2_w_gate"][l].T, "u2": p["ffn2_w_up"][l].T, "d2": p["ffn2_w_down"][l],
    }
    return lay.pack({k: t.astype(BF16) for k, t in parts.items()})


def _unpack_layer(lay, gathered):
    D = lay.D
    w = {k: lay.take(gathered, k) for k in ("g1", "u1", "d1", "win", "kv", "out", "g2", "u2", "d2")}
    gate = lay.take(gathered, "gate")
    w["gate"] = gate
    w["gate3"] = [gate[i * D:(i + 1) * D] for i in range(3)]
    br = lay.take(gathered, "br").reshape(N_DEV * lay.br_shard, lay.br_in)
    third = lay.br_in // 3
    w["br3"] = [br[:, i * third:(i + 1) * third] for i in range(3)]
    return w


def _silu_mul(accs, _):
    a, b = accs
    return [a, b, a * jax.nn.sigmoid(a) * b]


def _act_bwd(accs, extras):
    ds, (a, b) = accs[0], extras
    sig = jax.nn.sigmoid(a)
    return [ds * b * (sig * (1.0 + a * (1.0 - sig))), ds * (a * sig)]


def _ffn_fwd(tag, h, pre_g, post_g, wg, wu, wd):
    n = _rms_fwd("ffn_norm_" + tag, h, pre_g, BF16)
    a, b, s = _mm("ffn_up_" + tag, [(n, wg), (n, wu)], "nt", [F32, F32, BF16], _silu_mul, tm=256, tn=1408)
    f = _mm("ffn_down_" + tag, [(s, wd)], "nn", [F32], tm=256)
    out = _rms_fwd("ffn_out_" + tag, f, post_g, F32, res=h, scale=0.5)
    return out, (h, n, a, b, s, f)


def _ffn_bwd(tag, dh, saved, pre_g, post_g, wg, wu, wd):
    h, n, a, b, s, f = saved
    df, d_post = _rms_bwd("ffn_dout_" + tag, f, post_g, dh, BF16, scale=0.5)
    da, db = _mm("ffn_dact_" + tag, [(df, wd)], "nt", [BF16, BF16], _act_bwd, [(a, 0), (b, 0)], tm=256, tn=1408)
    d_wd = _mm("ffn_dwd_" + tag, [(s, df)], "tn", [BF16], tm=256)
    dn = _mm("ffn_dn_" + tag, [(da, wg), (db, wu)], "nn", [F32], _sum_accs, tm=256)
    d_wg = _mm("ffn_dwg_" + tag, [(da, n)], "tn", [BF16], tm=256)
    d_wu = _mm("ffn_dwu_" + tag, [(db, n)], "tn", [BF16], tm=256)
    dh_in, d_pre = _rms_bwd("ffn_dnorm_" + tag, h, pre_g, dn, F32, res=dh)
    return dh_in, d_pre, d_post, d_wg, d_wu, d_wd


def _unpad_proj(lay, projp):
    T = projp.shape[0]
    return projp.reshape(T, N_DEV, lay.in_pad)[:, :, :lay.in_shard].reshape(T, N_DEV * lay.in_shard)


def _pad_proj(lay, proj):
    T = proj.shape[0]
    t = proj.reshape(T, N_DEV, lay.in_shard)
    return jnp.pad(t, ((0, 0), (0, 0), (0, lay.in_pad - lay.in_shard))).reshape(T, N_DEV * lay.in_pad)


_SB_W = N_SB_HEADS * HEAD_DIM
_FOX_W = N_FOX_HEADS * HEAD_DIM
_SPLITS = [_SB_W, 2 * _SB_W, 3 * _SB_W, 3 * _SB_W + _FOX_W, 3 * _SB_W + 2 * _FOX_W, 3 * _SB_W + 3 * _FOX_W,
           3 * _SB_W + 3 * _FOX_W + N_FOX_HEADS]


def _gate_act(accs, extras):
    return [jax.nn.sigmoid(accs[0] + extras[0])]


def _merge(accs, extras):
    return [extras[0] * accs[0] + extras[1] * accs[1] + extras[2] * accs[2]]


def _merge_bwd(accs, extras):
    dm = accs[0]
    d_branch = [dm * gi for gi in extras]
    d_gate = [dm * bi * gi * (1.0 - gi) for bi, gi in zip(accs[1:], extras)]
    return d_branch + d_gate


def _mix_fwd(lay, h, w, pre_g, post_g, b_forget, b_gate, mem_n):
    D = lay.D
    u = _rms_fwd("mix_norm", h, pre_g, BF16)
    projp = _mm("mix_in", [(u, w["win"])], "nt", [F32])
    q_sb, k_sb, v_sb, q_fx, k_fx, v_fx, f_logit, q_mem = jnp.split(_unpad_proj(lay, projp), _SPLITS, axis=1)
    sb = [_to_heads(t.astype(BF16), N_SB_HEADS) for t in (q_sb, k_sb, v_sb)]
    fx = [_to_heads(t.astype(BF16), N_FOX_HEADS) for t in (q_fx, k_fx, v_fx)]
    qm = _to_heads(q_mem.astype(BF16), N_MEM_HEADS)
    fl = f_logit.T
    c = _decay_fwd(fl, b_forget.reshape(-1, 1))
    o_sb, rtot = _sb_fwd(*sb, HEAD_DIM ** -0.5)
    o_fx, lse_fx = _attn_fwd("fox_fwd", *fx, HEAD_DIM ** -0.5, c)
    kvm = _mm("mem_kv", [(mem_n, w["kv"])], "nn", [BF16])
    half = kvm.shape[1] // 2
    km, vm = _to_heads(kvm[:, :half], N_MEM_HEADS), _to_heads(kvm[:, half:], N_MEM_HEADS)
    o_mem, lse_mem = _attn_fwd("mem_fwd", qm, km, vm, qm.shape[-1] ** -0.5)
    gates = _mm("mix_gate", [(u, w["gate"])], "nt", [F32], _gate_act, [(b_gate.reshape(1, -1), 0)])
    flat = [_from_heads(o).astype(BF16) for o in (o_sb, o_fx, o_mem)]
    merged = _mm("mix_merge", list(zip(flat, w["br3"])), "nt", [BF16], _merge,
                 [(gates, 0), (gates, D), (gates, 2 * D)])
    z = _mm("mix_out", [(merged, w["out"])], "nn", [F32])
    out = _rms_fwd("mix_res", z, post_g, F32, res=h)
    saved = (h, u, sb, fx, qm, fl, c, o_sb, rtot, o_fx, lse_fx, km, vm, o_mem, lse_mem, gates, flat, merged, z)
    return out, saved


def _mix_bwd(lay, dh, saved, w, pre_g, post_g, b_forget, mem_n, dmem_n):
    D = lay.D
    h, u, sb, fx, qm, fl, c, o_sb, rtot, o_fx, lse_fx, km, vm, o_mem, lse_mem, gates, flat, merged, z = saved
    dz, d_post = _rms_bwd("mix_dres", z, post_g, dh, BF16)
    outs = _mm("mix_dmerge", [(dz, w["out"])] + list(zip(flat, w["br3"])), "nt", [BF16] * 6, _merge_bwd,
               [(gates, 0), (gates, D), (gates, 2 * D)])
    d_branch, d_gate = outs[:3], outs[3:]
    d_wout = _mm("mix_dwout", [(merged, dz)], "tn", [BF16])
    d_o = [_mm("mix_dbr%d" % i, [(d_branch[i], w["br3"][i])], "nn", [BF16]) for i in range(3)]
    d_wbr = [_mm("mix_dwbr%d" % i, [(d_branch[i], flat[i])], "tn", [BF16]) for i in range(3)]
    d_bgate = jnp.concatenate([_colsum("mix_dbgate%d" % i, d_gate[i]) for i in range(3)])
    d_wgate = [_mm("mix_dwgate%d" % i, [(d_gate[i], u)], "tn", [BF16]) for i in range(3)]

    dq_s, dk_s, dv_s = _sb_bwd(*sb, _to_heads(d_o[0], N_SB_HEADS), rtot, HEAD_DIM ** -0.5)
    dq_f, dk_f, dv_f, dc, dc_rows = _attn_bwd("fox_bwd", *fx, o_fx, _to_heads(d_o[1], N_FOX_HEADS), lse_fx,
                                     HEAD_DIM ** -0.5, c)
    dq_m, dk_m, dv_m = _attn_bwd("mem_bwd", qm, km, vm, o_mem, _to_heads(d_o[2], N_MEM_HEADS), lse_mem,
                                 qm.shape[-1] ** -0.5)
    dfl, d_bforget = _decay_bwd(dc, dc_rows, fl, b_forget.reshape(-1, 1))
    dproj = jnp.concatenate(
        [_from_heads(t).astype(BF16) for t in (dq_s, dk_s, dv_s, dq_f, dk_f, dv_f)]
        + [dfl.T.astype(BF16), _from_heads(dq_m).astype(BF16)], axis=1)
    dprojp = _pad_proj(lay, dproj)
    du = _mm("mix_du", list(zip(d_gate, w["gate3"])) + [(dprojp, w["win"])], "nn", [F32], _sum_accs, tm=256)
    d_win = _mm("mix_dwin", [(dprojp, u)], "tn", [BF16])
    dh_in, d_pre = _rms_bwd("mix_dnorm", h, pre_g, du, F32, res=dh)

    dkvm = jnp.concatenate([_from_heads(dk_m), _from_heads(dv_m)], axis=1).astype(BF16)
    d_wkv = _mm("mem_dwkv", [(mem_n, dkvm)], "tn", [BF16])
    dmem_n = _mm("mem_dn", [(dkvm, w["kv"])], "nt", [F32], lambda accs, ex: [accs[0] + ex[0]], [(dmem_n, 0)])
    grads = {"win": d_win, "kv": d_wkv, "gate": jnp.concatenate(d_wgate, axis=0),
             "br": jnp.concatenate(d_wbr, axis=1), "out": d_wout}
    return dh_in, d_pre, d_post, d_bforget, d_bgate, grads, dmem_n


def _layer_fwd(lay, h, w, sp, mem_n):
    h1, s1 = _ffn_fwd("1", h, sp["ffn1_pre_g"], sp["ffn1_post_g"], w["g1"], w["u1"], w["d1"])
    h2, s2 = _mix_fwd(lay, h1, w, sp["mix_pre_g"], sp["mix_post_g"], sp["b_forget"], sp["b_gate"], mem_n)
    h3, s3 = _ffn_fwd("2", h2, sp["ffn2_pre_g"], sp["ffn2_post_g"], w["g2"], w["u2"], w["d2"])
    return h3, (s1, s2, s3)


def _layer_bwd(lay, dh, saved, w, sp, mem_n, dmem_n):
    s1, s2, s3 = saved
    dh, d_pre2, d_post2, d_g2, d_u2, d_d2 = _ffn_bwd("2", dh, s3, sp["ffn2_pre_g"], sp["ffn2_post_g"],
                                                     w["g2"], w["u2"], w["d2"])
    dh, d_mpre, d_mpost, d_bforget, d_bgate, g, dmem_n = _mix_bwd(
        lay, dh, s2, w, sp["mix_pre_g"], sp["mix_post_g"], sp["b_forget"], mem_n, dmem_n)
    dh, d_pre1, d_post1, d_g1, d_u1, d_d1 = _ffn_bwd("1", dh, s1, sp["ffn1_pre_g"], sp["ffn1_post_g"],
                                                     w["g1"], w["u1"], w["d1"])
    g.update({"g1": d_g1, "u1": d_u1, "d1": d_d1, "g2": d_g2, "u2": d_u2, "d2": d_d2})
    g["br"] = g["br"].reshape(N_DEV, lay.br_shard, lay.br_in).reshape(-1, lay.D)
    packed = jnp.concatenate([lay.spread(g[key], key) for key in lay.seg], axis=1)
    small = {"ffn1_pre_g": d_pre1, "ffn1_post_g": d_post1, "mix_pre_g": d_mpre, "mix_post_g": d_mpost,
             "ffn2_pre_g": d_pre2, "ffn2_post_g": d_post2, "b_gate": d_bgate, "b_forget": d_bforget}
    return dh, packed, small, dmem_n


_SHARDED = ["ffn1_w_gate", "ffn1_w_up", "ffn1_w_down", "w_in", "w_mem_kv", "w_gate", "w_br_sb", "w_br_fox",
            "w_br_mem", "w_out", "ffn2_w_gate", "ffn2_w_up", "ffn2_w_down"]
_SMALL_LAYER = ["ffn1_pre_g", "ffn1_post_g", "mix_pre_g", "mix_post_g", "ffn2_pre_g", "ffn2_post_g", "b_gate",
                "b_forget"]
_WEIGHTS = ["ffn1_pre_g", "ffn1_post_g", "ffn1_w_gate", "ffn1_w_up", "ffn1_w_down", "mix_pre_g", "mix_post_g",
            "w_in", "b_forget", "mem_norm_g", "w_mem_kv", "w_gate", "b_gate", "w_br_sb", "w_br_fox", "w_br_mem",
            "w_out", "ffn2_pre_g", "ffn2_post_g", "ffn2_w_gate", "ffn2_w_up", "ffn2_w_down"]


def _pack_small(vals, L, D):
    rows = []
    for l in range(L):
        for name in _SMALL_LAYER:
            t = vals[name][l]
            rows.append(jnp.pad(t, (0, -t.shape[0] % D)).reshape(-1, D))
    rows.append(vals["mem_norm_g"].reshape(1, D))
    packed = jnp.concatenate(rows, axis=0)
    return _pad_rows(packed, -(-packed.shape[0] // 8) * 8)


def _unpack_small(packed, shapes, L, D):
    out = {name: [] for name in _SMALL_LAYER}
    r = 0
    for l in range(L):
        for name in _SMALL_LAYER:
            n = shapes[name][1]
            nr = -(-n // D)
            out[name].append(packed[r:r + nr].reshape(-1)[:n])
            r += nr
    res = {name: jnp.stack(v) for name, v in out.items()}
    res["mem_norm_g"] = packed[r]
    return res


def _unpack_grads(lay, g, l_shapes):
    def seg(key):
        off, n = lay.seg[key]
        return g[off:off + n]
    br = seg("br").reshape(lay.br_shard, lay.br_in).T
    third = lay.br_in // 3
    return {
        "ffn1_w_gate": seg("g1").T, "ffn1_w_up": seg("u1").T, "ffn1_w_down": seg("d1"),
        "w_in": seg("win")[:lay.in_shard].T, "w_mem_kv": seg("kv"), "w_gate": seg("gate").T,
        "w_br_sb": br[:third], "w_br_fox": br[third:2 * third], "w_br_mem": br[2 * third:],
        "w_out": seg("out"), "ffn2_w_gate": seg("g2").T, "ffn2_w_up": seg("u2").T, "ffn2_w_down": seg("d2"),
    }


def _step(p, m, v, x, mem, tgt, gather, reduce_scatter, loss_sum):
    L, D = p["ffn1_pre_g"].shape
    lay = _Layout(D, p["ffn1_w_gate"].shape[2], p["w_in"].shape[2], p["w_mem_kv"].shape[1], p["w_gate"].shape[2],
                  3 * p["w_br_sb"].shape[1], p["w_br_sb"].shape[2], p["w_out"].shape[1])
    ws = [_unpack_layer(lay, gather("ag_weights", _pack_layer(lay, l, p))) for l in range(L)]
    sps = [{name: p[name][l] for name in _SMALL_LAYER} for l in range(L)]

    mem_n = _rms_fwd("mem_norm", mem, p["mem_norm_g"], BF16)
    h, saved = x, []
    for l in range(L):
        h, s = _layer_fwd(lay, h, ws[l], sps[l], mem_n)
        saved.append(s)
    loss_part, dh = _loss_grad(h, tgt)
    loss = loss_sum(loss_part)

    dmem_n = jnp.zeros(mem.shape, F32)
    big, small = [None] * L, {name: [None] * L for name in _SMALL_LAYER}
    for l in reversed(range(L)):
        dh, packed, sm, dmem_n = _layer_bwd(lay, dh, saved[l], ws[l], sps[l], mem_n, dmem_n)
        big[l] = _unpack_grads(lay, reduce_scatter("w", packed), None)
        for name in _SMALL_LAYER:
            small[name][l] = sm[name]
    _, d_memg = _rms_bwd("mem_dnorm", mem, p["mem_norm_g"], dmem_n, F32)

    small_g = {name: jnp.stack(vs) for name, vs in small.items()}
    small_g["mem_norm_g"] = d_memg
    small_names = _SMALL_LAYER + ["mem_norm_g"]
    shapes = {name: p[name].shape for name in small_names}
    g_all = gather("ag_small", _pack_small(small_g, L, D))
    packs = [_pack_small({name: t[name] for name in small_names}, L, D) for t in (p, m, v)]
    res = [_unpack_small(t, shapes, L, D) for t in _adamw("adamw_small", g_all, *packs)]

    out = {kind: {} for kind in ("grad", "delta", "new_m", "new_v")}
    for name in small_names:
        for kind, r in zip(("grad", "delta", "new_m", "new_v"), res):
            out[kind][name] = r[name].reshape(p[name].shape)
    for name in _SHARDED:
        g = jnp.stack([big[l][name] for l in range(L)])
        shp = g.shape
        flat = lambda t: t.reshape(-1, shp[-1])
        r = _adamw("adamw_" + name, flat(g)[None], flat(p[name]), flat(m[name]), flat(v[name]))
        for kind, t in zip(("grad", "delta", "new_m", "new_v"), r):
            out[kind][name] = t.reshape(shp)
    return loss, dh, out


def kernel(x, mem, ffn1_pre_g, ffn1_post_g, ffn1_w_gate, ffn1_w_up, ffn1_w_down, mix_pre_g, mix_post_g, w_in, b_forget, mem_norm_g, w_mem_kv, w_gate, b_gate, w_br_sb, w_br_fox, w_br_mem, w_out, ffn2_pre_g, ffn2_post_g, ffn2_w_gate, ffn2_w_up, ffn2_w_down, loss_target, m_ffn1_pre_g, m_ffn1_post_g, m_ffn1_w_gate, m_ffn1_w_up, m_ffn1_w_down, m_mix_pre_g, m_mix_post_g, m_w_in, m_b_forget, m_mem_norm_g, m_w_mem_kv, m_w_gate, m_b_gate, m_w_br_sb, m_w_br_fox, m_w_br_mem, m_w_out, m_ffn2_pre_g, m_ffn2_post_g, m_ffn2_w_gate, m_ffn2_w_up, m_ffn2_w_down, v_ffn1_pre_g, v_ffn1_post_g, v_ffn1_w_gate, v_ffn1_w_up, v_ffn1_w_down, v_mix_pre_g, v_mix_post_g, v_w_in, v_b_forget, v_mem_norm_g, v_w_mem_kv, v_w_gate, v_b_gate, v_w_br_sb, v_w_br_fox, v_w_br_mem, v_w_out, v_ffn2_pre_g, v_ffn2_post_g, v_ffn2_w_gate, v_ffn2_w_up, v_ffn2_w_down):
    p = dict(zip(_WEIGHTS, (ffn1_pre_g, ffn1_post_g, ffn1_w_gate, ffn1_w_up, ffn1_w_down, mix_pre_g, mix_post_g, w_in, b_forget, mem_norm_g, w_mem_kv, w_gate, b_gate, w_br_sb, w_br_fox, w_br_mem, w_out, ffn2_pre_g, ffn2_post_g, ffn2_w_gate, ffn2_w_up, ffn2_w_down)))
    m = dict(zip(_WEIGHTS, (m_ffn1_pre_g, m_ffn1_post_g, m_ffn1_w_gate, m_ffn1_w_up, m_ffn1_w_down, m_mix_pre_g, m_mix_post_g, m_w_in, m_b_forget, m_mem_norm_g, m_w_mem_kv, m_w_gate, m_b_gate, m_w_br_sb, m_w_br_fox, m_w_br_mem, m_w_out, m_ffn2_pre_g, m_ffn2_post_g, m_ffn2_w_gate, m_ffn2_w_up, m_ffn2_w_down)))
    v = dict(zip(_WEIGHTS, (v_ffn1_pre_g, v_ffn1_post_g, v_ffn1_w_gate, v_ffn1_w_up, v_ffn1_w_down, v_mix_pre_g, v_mix_post_g, v_w_in, v_b_forget, v_mem_norm_g, v_w_mem_kv, v_w_gate, v_b_gate, v_w_br_sb, v_w_br_fox, v_w_br_mem, v_w_out, v_ffn2_pre_g, v_ffn2_post_g, v_ffn2_w_gate, v_ffn2_w_up, v_ffn2_w_down)))
    loss, dx, out = _step(p, m, v, x[0], mem[0], loss_target[0], _all_gather, _reduce_scatter,
                          lambda part: lax.psum(part, ("x", "y", "c")))
    return (loss, dx[None], *[out["grad"][n] for n in _WEIGHTS], *[out["delta"][n] for n in _WEIGHTS],
            *[out["new_m"][n] for n in _WEIGHTS], *[out["new_v"][n] for n in _WEIGHTS])
```

```python
import functools
import math

import jax
import jax.numpy as jnp
from jax import lax
from jax.experimental import pallas as pl
from jax.experimental.pallas import tpu as pltpu

F32 = jnp.float32
BF16 = jnp.bfloat16

LANE = 128
SUBLANE_BF16 = 16
VMEM_LIMIT = 48 * 1024 * 1024
N_DEV = 8
MESH = pl.DeviceIdType.MESH
ANY = pl.BlockSpec(memory_space=pl.ANY)

RMS_EPS = 1e-6
HEAD_DIM = 64
N_SB_HEADS = 8
N_FOX_HEADS = 8
N_MEM_HEADS = 4
NEG = -1e30
ATT_TQ = 512
ATT_TK = 256
DECAY_TK = 128

ADAM_LR = 0.001
ADAM_B1 = 0.9
ADAM_B2 = 0.999
ADAM_EPS = 1e-08
ADAM_WD = 0.01
ADAM_STEP = 10


def _tile(n, target, mult=LANE):
    best = None
    for t in range(mult, min(n, target) + 1, mult):
        if n % t == 0:
            best = t
    return best if best is not None else n


def _cparams(sem):
    return pltpu.CompilerParams(dimension_semantics=sem, vmem_limit_bytes=VMEM_LIMIT)


_DIMS = {"nn": (((1,), (0,)), ((), ())), "nt": (((1,), (1,)), ((), ())), "tn": (((0,), (0,)), ((), ()))}


def _dot(a, b, mode="nn"):
    return lax.dot_general(a.astype(BF16), b.astype(BF16), _DIMS[mode], preferred_element_type=F32)


def _mm(name, pairs, mode, out_dtypes, epilogue=None, extras=(), tm=512, tn=512):
    a0, b0 = pairs[0]
    M = a0.shape[1] if mode == "tn" else a0.shape[0]
    N = b0.shape[0] if mode == "nt" else b0.shape[1]
    tm = _tile(M, tm)
    tn = _tile(N, tn)
    np_, ne, no = len(pairs), len(extras), len(out_dtypes)

    def body(*refs):
        a_refs, b_refs = refs[:np_], refs[np_:2 * np_]
        e_refs = refs[2 * np_:2 * np_ + ne]
        o_refs = refs[2 * np_ + ne:]
        accs = [_dot(a[...], b[...], mode) for a, b in zip(a_refs, b_refs)]
        outs = epilogue(accs, [e[...] for e in e_refs]) if epilogue is not None else accs
        for o, val in zip(o_refs, outs):
            o[...] = val.astype(o.dtype)

    in_specs = []
    for a, _ in pairs:
        if mode == "tn":
            in_specs.append(pl.BlockSpec((a.shape[0], tm), lambda j, i: (0, i)))
        else:
            in_specs.append(pl.BlockSpec((tm, a.shape[1]), lambda j, i: (i, 0)))
    for _, b in pairs:
        if mode == "nt":
            in_specs.append(pl.BlockSpec((tn, b.shape[1]), lambda j, i: (j, 0)))
        else:
            in_specs.append(pl.BlockSpec((b.shape[0], tn), lambda j, i: (0, j)))
    for e, off in extras:
        if e.shape[0] == 1:
            in_specs.append(pl.BlockSpec((1, tn), functools.partial(lambda j, i, o: (0, j + o), o=off // tn)))
        else:
            in_specs.append(pl.BlockSpec((tm, tn), functools.partial(lambda j, i, o: (i, j + o), o=off // tn)))
    out_specs = [pl.BlockSpec((tm, tn), lambda j, i: (i, j)) for _ in range(no)]
    outs = pl.pallas_call(
        body, name=name, grid=(N // tn, M // tm),
        in_specs=in_specs, out_specs=out_specs,
        out_shape=[jax.ShapeDtypeStruct((M, N), dt) for dt in out_dtypes],
        compiler_params=_cparams(("parallel", "parallel")),
    )(*[a for a, _ in pairs], *[b for _, b in pairs], *[e for e, _ in extras])
    return outs[0] if no == 1 else outs


def _sum_accs(accs, _):
    total = accs[0]
    for acc in accs[1:]:
        total = total + acc
    return [total]


def _rstd(x):
    return lax.rsqrt(jnp.mean(x * x, axis=-1, keepdims=True) + RMS_EPS)


def _rms_fwd(name, x, g, out_dtype, res=None, scale=1.0, tr=512):
    R, D = x.shape
    tr = _tile(R, tr, 8)
    has_res = res is not None

    def body(*refs):
        x_ref, g_ref = refs[:2]
        o_ref = refs[-1]
        xv = x_ref[...]
        y = (xv * _rstd(xv)) * g_ref[...]
        if has_res:
            y = refs[2][...] + scale * y
        o_ref[...] = y.astype(o_ref.dtype)

    row = pl.BlockSpec((tr, D), lambda i: (i, 0))
    gain = pl.BlockSpec((1, D), lambda i: (0, 0))
    return pl.pallas_call(
        body, name=name, grid=(R // tr,),
        in_specs=[row, gain] + ([row] if has_res else []), out_specs=row,
        out_shape=jax.ShapeDtypeStruct((R, D), out_dtype),
        compiler_params=_cparams(("parallel",)),
    )(x, g.reshape(1, D), *([res] if has_res else []))


def _rms_bwd(name, x, g, dy, out_dtype, scale=1.0, res=None, tr=512):
    R, D = x.shape
    tr = _tile(R, tr, 8)
    has_res = res is not None

    def body(*refs):
        x_ref, g_ref, dy_ref = refs[:3]
        dx_ref, dg_ref = refs[-2:]
        i = pl.program_id(0)
        xv = x_ref[...]
        xhat = xv * _rstd(xv)
        dyv = dy_ref[...].astype(F32) * scale
        gy = dyv * g_ref[...]
        dx = _rstd(xv) * (gy - xhat * jnp.mean(gy * xhat, axis=-1, keepdims=True))
        if has_res:
            dx = refs[3][...] + dx
        dx_ref[...] = dx.astype(dx_ref.dtype)
        part = jnp.sum(dyv * xhat, axis=0, keepdims=True)

        @pl.when(i == 0)
        def _():
            dg_ref[...] = part

        @pl.when(i > 0)
        def _():
            dg_ref[...] += part

    row = pl.BlockSpec((tr, D), lambda i: (i, 0))
    gain = pl.BlockSpec((1, D), lambda i: (0, 0))
    dx, dg = pl.pallas_call(
        body, name=name, grid=(R // tr,),
        in_specs=[row, gain, row] + ([row] if has_res else []), out_specs=[row, gain],
        out_shape=[jax.ShapeDtypeStruct((R, D), out_dtype), jax.ShapeDtypeStruct((1, D), F32)],
        compiler_params=_cparams(("arbitrary",)),
    )(x, g.reshape(1, D), dy, *([res] if has_res else []))
    return dx, dg[0]


def _loss_grad(y, tgt, tr=512):
    R, D = y.shape
    tr = _tile(R, tr, 8)

    def body(y_ref, t_ref, dy_ref, loss_ref):
        i = pl.program_id(0)
        d = y_ref[...] - t_ref[...]
        dy_ref[...] = d / D
        part = 0.5 * jnp.sum(jnp.mean(d * d, axis=-1, keepdims=True), axis=0, keepdims=True)
        tile = jnp.broadcast_to(part, loss_ref.shape)

        @pl.when(i == 0)
        def _():
            loss_ref[...] = tile

        @pl.when(i > 0)
        def _():
            loss_ref[...] += tile

    row = pl.BlockSpec((tr, D), lambda i: (i, 0))
    dy, loss = pl.pallas_call(
        body, name="loss_grad", grid=(R // tr,),
        in_specs=[row, row], out_specs=[row, pl.BlockSpec((8, LANE), lambda i: (0, 0))],
        out_shape=[jax.ShapeDtypeStruct((R, D), F32), jax.ShapeDtypeStruct((8, LANE), F32)],
        compiler_params=_cparams(("arbitrary",)),
    )(y, tgt)
    return loss[0, 0], dy


def _colsum(name, x, tr=512, tn=1024):
    R, N = x.shape
    tr, tn = _tile(R, tr, 8), _tile(N, tn)

    def body(x_ref, o_ref):
        i = pl.program_id(1)
        part = jnp.sum(x_ref[...].astype(F32), axis=0, keepdims=True)

        @pl.when(i == 0)
        def _():
            o_ref[...] = part

        @pl.when(i > 0)
        def _():
            o_ref[...] += part

    out = pl.pallas_call(
        body, name=name, grid=(N // tn, R // tr),
        in_specs=[pl.BlockSpec((tr, tn), lambda j, i: (i, j))], out_specs=pl.BlockSpec((1, tn), lambda j, i: (0, j)),
        out_shape=jax.ShapeDtypeStruct((1, N), F32),
        compiler_params=_cparams(("parallel", "arbitrary")),
    )(x)
    return out[0]


def _tri(tk, rel):
    j = lax.broadcasted_iota(jnp.int32, (tk, tk), 0)
    s = lax.broadcasted_iota(jnp.int32, (tk, tk), 1)
    return rel(j, s).astype(BF16)


def _dot_split(x, m, parts=2):
    total = None
    rem = x
    for _ in range(parts):
        piece = rem.astype(BF16)
        rem = rem - piece.astype(F32)
        term = jnp.dot(piece, m, preferred_element_type=F32)
        total = term if total is None else total + term
    return total


def _log_not_and_beta(z, mask):
    ln = -(jnp.maximum(z, 0.0) + jnp.log(1.0 + jnp.exp(-jnp.abs(z))))
    return (ln if mask is None else jnp.where(mask, ln, 0.0)), ln + z


def _att_tiles(T, Tk, causal):
    tq = min(ATT_TQ, T)
    tk = min(ATT_TK, tq if causal else Tk)
    return tq, tk, (tq if causal else Tk) // tk


def _key_base(j, tq):
    return j * tq if isinstance(j, int) else pl.multiple_of(j * tq, tq)


def _is_pow2(scale):
    return math.log2(scale).is_integer()


def _sb_fwd(q, k, v, scale):
    H, T, d = q.shape
    tq, tk, nsub = _att_tiles(T, T, True)
    assert _is_pow2(scale)

    def body(q_ref, k_ref, v_ref, o_ref, rt_ref, acc_ref, r_ref):
        qi = pl.program_id(1)
        qv = q_ref[0] * scale
        acc_ref[...] = jnp.zeros_like(acc_ref)
        r_ref[...] = jnp.zeros_like(r_ref)
        row = lax.broadcasted_iota(jnp.int32, (tq, tk), 0)
        col = lax.broadcasted_iota(jnp.int32, (tq, tk), 1)
        after = _tri(tk, lambda j, s: j > s)

        def step(j, diagonal):
            base = _key_base(j, tq)
            parts = []
            for u in reversed(range(nsub)):
                z = _dot(qv, k_ref[0, pl.ds(base + u * tk, tk), :], "nt")
                mask = (col + u * tk) < row if diagonal else None
                ln, lb = _log_not_and_beta(z, mask)
                between = _dot_split(ln, after)
                parts.append((u, lb, between, between[:, 0:1] + ln[:, 0:1], mask))
            r = r_ref[...]
            out = None
            for u, lb, between, total, mask in parts:
                w = jnp.exp(lb + between + r)
                if diagonal:
                    w = jnp.where(mask, w, 0.0)
                term = _dot(w, v_ref[0, pl.ds(base + u * tk, tk), :])
                out = term if out is None else out + term
                r = r + total
            acc_ref[...] += out
            r_ref[...] = r

        def below(i, carry):
            step(qi - 1 - i, False)
            return carry

        step(qi, True)
        lax.fori_loop(0, qi, below, 0)
        o_ref[0] = acc_ref[...]
        rt_ref[0] = r_ref[...]

    blk = pl.BlockSpec((1, tq, d), lambda h, i: (h, i, 0))
    full = pl.BlockSpec((1, T, d), lambda h, i: (h, 0, 0))
    col = pl.BlockSpec((1, tq, 1), lambda h, i: (h, i, 0))
    return pl.pallas_call(
        body, name="sb_fwd", grid=(H, T // tq),
        in_specs=[blk, full, full], out_specs=[blk, col],
        out_shape=[jax.ShapeDtypeStruct((H, T, d), F32), jax.ShapeDtypeStruct((H, T, 1), F32)],
        scratch_shapes=[pltpu.VMEM((tq, d), F32), pltpu.VMEM((tq, 1), F32)],
        compiler_params=_cparams(("parallel", "arbitrary")),
    )(q, k, v)


def _sb_bwd(q, k, v, do, rtot, scale):
    H, T, d = q.shape
    tq, tk, nsub = _att_tiles(T, T, True)
    assert _is_pow2(scale)

    def body(q_ref, k_ref, v_ref, do_ref, rt_ref, dq_ref, dk_ref, dv_ref, dq_acc, p_ref, c_ref):
        qi = pl.program_id(1)

        @pl.when(qi == 0)
        def _():
            dk_ref[...] = jnp.zeros_like(dk_ref)
            dv_ref[...] = jnp.zeros_like(dv_ref)

        qv = q_ref[0] * scale
        dov = do_ref[0]
        rt = rt_ref[0]
        dq_acc[...] = jnp.zeros_like(dq_acc)
        p_ref[...] = jnp.zeros_like(p_ref)
        c_ref[...] = jnp.zeros_like(c_ref)
        row = lax.broadcasted_iota(jnp.int32, (tq, tk), 0)
        col = lax.broadcasted_iota(jnp.int32, (tq, tk), 1)
        upto = _tri(tk, lambda j, s: j <= s)
        before = _tri(tk, lambda j, s: j < s)

        def step(j, diagonal):
            base = _key_base(j, tq)
            first = []
            for u in range(nsub):
                ks = base + u * tk
                kv = k_ref[0, pl.ds(ks, tk), :]
                z = _dot(qv, kv, "nt")
                mask = (col + u * tk) < row if diagonal else None
                ln, lb = _log_not_and_beta(z, mask)
                dw = _dot(dov, v_ref[0, pl.ds(ks, tk), :], "nt")
                first.append((ks, kv, mask, lb, jnp.exp(lb), _dot_split(ln, upto), dw))
            pre, cpre = p_ref[...], c_ref[...]
            dq = None
            for ks, kv, mask, lb, sig, local, dw in first:
                prefix = local + pre
                w = jnp.exp(lb + (rt - prefix))
                if diagonal:
                    w = jnp.where(mask, w, 0.0)
                g = dw * w
                c = _dot_split(g, before) + cpre
                dz = g * (1.0 - sig) - c * sig
                if diagonal:
                    dz = jnp.where(mask, dz, 0.0)
                term = _dot(dz, kv)
                dq = term if dq is None else dq + term
                dk_ref[0, pl.ds(ks, tk), :] += _dot(dz, qv, "tn")
                dv_ref[0, pl.ds(ks, tk), :] += _dot(w, dov, "tn")
                pre = prefix[:, tk - 1:tk]
                cpre = c[:, tk - 1:tk] + g[:, tk - 1:tk]
            dq_acc[...] += dq
            p_ref[...] = pre
            c_ref[...] = cpre

        def below(j, carry):
            step(j, False)
            return carry

        lax.fori_loop(0, qi, below, 0)
        step(qi, True)
        dq_ref[0] = (dq_acc[...] * scale).astype(dq_ref.dtype)

    blk = pl.BlockSpec((1, tq, d), lambda h, i: (h, i, 0))
    full = pl.BlockSpec((1, T, d), lambda h, i: (h, 0, 0))
    col = pl.BlockSpec((1, tq, 1), lambda h, i: (h, i, 0))
    return pl.pallas_call(
        body, name="sb_bwd", grid=(H, T // tq),
        in_specs=[blk, full, full, blk, col], out_specs=[blk, full, full],
        out_shape=[jax.ShapeDtypeStruct((H, T, d), BF16), jax.ShapeDtypeStruct((H, T, d), F32),
                   jax.ShapeDtypeStruct((H, T, d), F32)],
        scratch_shapes=[pltpu.VMEM((tq, d), F32), pltpu.VMEM((tq, 1), F32), pltpu.VMEM((tq, 1), F32)],
        compiler_params=_cparams(("parallel", "arbitrary")),
    )(q, k, v, do, rtot)


def _attn_fwd(name, q, k, v, scale, c=None):
    H, T, d = q.shape
    Tk = k.shape[1]
    causal = c is not None
    tq, tk, nsub = _att_tiles(T, Tk, causal)
    fold = _is_pow2(scale)

    def body(*refs):
        q_ref, k_ref, v_ref = refs[:3]
        cc_ref, cr_ref = refs[3:5] if causal else (None, None)
        o_ref, lse_ref, m_ref, l_ref, acc_ref = refs[-5:]
        qi = pl.program_id(1)
        qv = q_ref[0] * scale if fold else q_ref[0]
        m_ref[...] = jnp.full_like(m_ref, NEG)
        l_ref[...] = jnp.zeros_like(l_ref)
        acc_ref[...] = jnp.zeros_like(acc_ref)
        row = lax.broadcasted_iota(jnp.int32, (tq, tk), 0)
        col = lax.broadcasted_iota(jnp.int32, (tq, tk), 1)

        def step(j, diagonal):
            base = _key_base(j, tq)
            zs = []
            for u in range(nsub):
                z = _dot(qv, k_ref[0, pl.ds(base + u * tk, tk), :], "nt")
                if not fold:
                    z = z * scale
                if causal:
                    z = z + cc_ref[0] - cr_ref[0, j * nsub + u]
                if diagonal:
                    z = jnp.where((col + u * tk) <= row, z, NEG)
                zs.append(z)
            m_prev = m_ref[...]
            m_new = m_prev
            for z in zs:
                m_new = jnp.maximum(m_new, jnp.max(z, axis=1, keepdims=True))
            alpha = jnp.exp(m_prev - m_new)
            l_new = alpha * l_ref[...]
            out = alpha * acc_ref[...]
            for u, z in enumerate(zs):
                p = jnp.exp(z - m_new)
                l_new = l_new + jnp.sum(p, axis=1, keepdims=True)
                out = out + _dot(p, v_ref[0, pl.ds(base + u * tk, tk), :])
            l_ref[...] = l_new
            acc_ref[...] = out
            m_ref[...] = m_new

        def below(j, carry):
            step(j, False)
            return carry

        if causal:
            lax.fori_loop(0, qi, below, 0)
            step(qi, True)
        else:
            step(0, False)
        o_ref[0] = acc_ref[...] / l_ref[...]
        lse_ref[0] = m_ref[...] + jnp.log(l_ref[...])

    blk = pl.BlockSpec((1, tq, d), lambda h, i: (h, i, 0))
    full = pl.BlockSpec((1, Tk, d), lambda h, i: (h, 0, 0))
    col = pl.BlockSpec((1, tq, 1), lambda h, i: (h, i, 0))
    in_specs, args = [blk, full, full], [q, k, v]
    if causal:
        in_specs += [col, pl.BlockSpec((1, T // tk, 1, tk), lambda h, i: (h, 0, 0, 0))]
        args += [c.reshape(H, T, 1), c.reshape(H, T // tk, 1, tk)]
    return pl.pallas_call(
        body, name=name, grid=(H, T // tq),
        in_specs=in_specs, out_specs=[blk, col],
        out_shape=[jax.ShapeDtypeStruct((H, T, d), F32), jax.ShapeDtypeStruct((H, T, 1), F32)],
        scratch_shapes=[pltpu.VMEM((tq, 1), F32), pltpu.VMEM((tq, 1), F32), pltpu.VMEM((tq, d), F32)],
        compiler_params=_cparams(("parallel", "arbitrary")),
    )(*args)


def _attn_bwd(name, q, k, v, o, do, lse, scale, c=None):
    H, T, d = q.shape
    Tk = k.shape[1]
    causal = c is not None
    tq, tk, nsub = _att_tiles(T, Tk, causal)
    fold = _is_pow2(scale)

    def body(*refs):
        q_ref, k_ref, v_ref, o_ref, do_ref, lse_ref = refs[:6]
        cc_ref, cr_ref = refs[6:8] if causal else (None, None)
        n_out = 5 if causal else 3
        outs = refs[-(n_out + 1):-1]
        dq_ref, dk_ref, dv_ref = outs[:3]
        dc_ref, drow_ref = outs[3:5] if causal else (None, None)
        dq_acc = refs[-1]
        qi = pl.program_id(1)

        @pl.when(qi == 0)
        def _():
            dk_ref[...] = jnp.zeros_like(dk_ref)
            dv_ref[...] = jnp.zeros_like(dv_ref)
            if causal:
                dc_ref[...] = jnp.zeros_like(dc_ref)

        qv = q_ref[0] * scale if fold else q_ref[0]
        dov = do_ref[0]
        lse_v = lse_ref[0]
        delta = jnp.sum(dov.astype(F32) * o_ref[0], axis=1, keepdims=True)
        dq_acc[...] = jnp.zeros_like(dq_acc)
        if causal:
            drow_ref[...] = jnp.zeros_like(drow_ref)
        row = lax.broadcasted_iota(jnp.int32, (tq, tk), 0)
        col = lax.broadcasted_iota(jnp.int32, (tq, tk), 1)

        def step(j, diagonal):
            base = _key_base(j, tq)
            dq, drow = None, None
            for u in range(nsub):
                ks = base + u * tk
                kv = k_ref[0, pl.ds(ks, tk), :]
                z = _dot(qv, kv, "nt")
                if not fold:
                    z = z * scale
                if causal:
                    z = z + cc_ref[0] - cr_ref[0, j * nsub + u]
                if diagonal:
                    z = jnp.where((col + u * tk) <= row, z, NEG)
                p = jnp.exp(z - lse_v)
                ds = p * (_dot(dov, v_ref[0, pl.ds(ks, tk), :], "nt") - delta)
                term = _dot(ds, kv)
                dq = term if dq is None else dq + term
                dk = _dot(ds, qv, "tn")
                dk_ref[0, pl.ds(ks, tk), :] += dk if fold else dk * scale
                dv_ref[0, pl.ds(ks, tk), :] += _dot(p, dov, "tn")
                if causal:
                    dc_ref[0, j * nsub + u] -= jnp.sum(ds, axis=0, keepdims=True)
                    rs = jnp.sum(ds, axis=1, keepdims=True)
                    drow = rs if drow is None else drow + rs
            dq_acc[...] += dq
            if causal:
                drow_ref[0] += drow

        def below(j, carry):
            step(j, False)
            return carry

        if causal:
            lax.fori_loop(0, qi, below, 0)
            step(qi, True)
        else:
            step(0, False)
        dq_ref[0] = (dq_acc[...] * scale).astype(dq_ref.dtype)

    blk = pl.BlockSpec((1, tq, d), lambda h, i: (h, i, 0))
    full = pl.BlockSpec((1, Tk, d), lambda h, i: (h, 0, 0))
    col = pl.BlockSpec((1, tq, 1), lambda h, i: (h, i, 0))
    crow = pl.BlockSpec((1, T // tk, 1, tk), lambda h, i: (h, 0, 0, 0))
    in_specs, args = [blk, full, full, blk, blk, col], [q, k, v, o, do, lse]
    out_specs = [blk, full, full]
    out_shape = [jax.ShapeDtypeStruct((H, T, d), BF16), jax.ShapeDtypeStruct((H, Tk, d), F32),
                 jax.ShapeDtypeStruct((H, Tk, d), F32)]
    if causal:
        in_specs += [col, crow]
        args += [c.reshape(H, T, 1), c.reshape(H, T // tk, 1, tk)]
        out_specs += [crow, col]
        out_shape += [jax.ShapeDtypeStruct((H, T // tk, 1, tk), F32), jax.ShapeDtypeStruct((H, T, 1), F32)]
    outs = pl.pallas_call(
        body, name=name, grid=(H, T // tq),
        in_specs=in_specs, out_specs=out_specs, out_shape=out_shape,
        scratch_shapes=[pltpu.VMEM((tq, d), F32)],
        compiler_params=_cparams(("parallel", "arbitrary")),
    )(*args)
    if causal:
        return outs[0], outs[1], outs[2], outs[3].reshape(H, T), outs[4].reshape(H, T)
    return outs


def _decay_fwd(fl, b):
    H, T = fl.shape
    tk = DECAY_TK

    def body(x_ref, b_ref, c_ref):
        upto = _tri(tk, lambda j, s: j <= s)
        carry = jnp.zeros((H, 1), F32)
        for i in range(T // tk):
            xv = x_ref[:, i * tk:(i + 1) * tk] + b_ref[...]
            lf = jnp.minimum(xv, 0.0) - jnp.log(1.0 + jnp.exp(-jnp.abs(xv)))
            pref = _dot_split(lf, upto, parts=3) + carry
            c_ref[:, i * tk:(i + 1) * tk] = pref
            carry = pref[:, tk - 1:tk]

    vm = pl.BlockSpec(memory_space=pltpu.VMEM)
    return pl.pallas_call(
        body, name="decay_fwd", in_specs=[vm, vm], out_specs=vm,
        out_shape=jax.ShapeDtypeStruct((H, T), F32),
    )(fl, b)


def _decay_bwd(dc_cols, dc_rows, fl, b):
    H, T = fl.shape
    tk = DECAY_TK

    def body(dc_ref, dr_ref, x_ref, b_ref, dx_ref, db_ref):
        from_ = _tri(tk, lambda j, s: j >= s)
        carry = jnp.zeros((H, 1), F32)
        total = jnp.zeros((H, 1), F32)
        for i in reversed(range(T // tk)):
            sl = slice(i * tk, (i + 1) * tk)
            suffix = _dot_split(dc_ref[:, sl] + dr_ref[:, sl], from_, parts=3) + carry
            xv = x_ref[:, sl] + b_ref[...]
            dx = suffix / (1.0 + jnp.exp(xv))
            dx_ref[:, sl] = dx
            total = total + jnp.sum(dx, axis=1, keepdims=True)
            carry = suffix[:, 0:1]
        db_ref[...] = jnp.broadcast_to(total, db_ref.shape)

    vm = pl.BlockSpec(memory_space=pltpu.VMEM)
    dx, db = pl.pallas_call(
        body, name="decay_bwd", in_specs=[vm, vm, vm, vm], out_specs=[vm, vm],
        out_shape=[jax.ShapeDtypeStruct((H, T), F32), jax.ShapeDtypeStruct((H, LANE), F32)],
    )(dc_cols, dc_rows, fl, b)
    return dx, db[:, 0]


def _place():
    x, y, c = lax.axis_index("x"), lax.axis_index("y"), lax.axis_index("c")
    return x, y, c, [(1 - x, y), (x, 1 - y), (1 - x, 1 - y)]


def _all_gather(name, block):
    R, C = block.shape

    def body(x_ref, out_ref, send_sems, recv_sems, local_sem):
        x, y, c, chips = _place()
        me, sibling = (x, y, c), (x, y, 1 - c)

        def rows(px, py, pc):
            return out_ref.at[4 * px + 2 * py + pc]

        def copy(k, blk, to, src=None):
            return pltpu.make_async_remote_copy(
                src_ref=rows(*blk) if src is None else src, dst_ref=rows(*blk),
                send_sem=send_sems.at[k], recv_sem=recv_sems.at[k], device_id=to, device_id_type=MESH)

        mine = pltpu.make_async_copy(x_ref, rows(*me), local_sem)
        mine.start()
        first = [copy(0, me, sibling, src=x_ref)]
        first += [copy(1 + j, me, (*chip, c), src=x_ref) for j, chip in enumerate(chips)]
        for cp in first:
            cp.start()
        passed = [copy(4 + j, (*chip, c), sibling) for j, chip in enumerate(chips)]
        for j, chip in enumerate(chips):
            copy(1 + j, (*chip, c), me).wait_recv()
            passed[j].start()
        copy(0, sibling, me).wait_recv()
        for j, chip in enumerate(chips):
            copy(4 + j, (*chip, 1 - c), me).wait_recv()
        for cp in first + passed:
            cp.wait_send()
        mine.wait()

    return pl.pallas_call(
        body, name=name, in_specs=[ANY], out_specs=ANY,
        out_shape=jax.ShapeDtypeStruct((N_DEV, R, C), block.dtype),
        scratch_shapes=[pltpu.SemaphoreType.DMA((7,)), pltpu.SemaphoreType.DMA((7,)), pltpu.SemaphoreType.DMA(())],
    )(block)


def _swap_with_sibling(name, parts):
    _, R, C = parts.shape

    def body(p_ref, out_ref, send_sems, recv_sems):
        x, y, c, _ = _place()
        copies = [pltpu.make_async_remote_copy(
            src_ref=p_ref.at[2 * q + (1 - c)], dst_ref=out_ref.at[q],
            send_sem=send_sems.at[q], recv_sem=recv_sems.at[q], device_id=(x, y, 1 - c), device_id_type=MESH)
            for q in range(4)]
        for cp in copies:
            cp.start()
        for cp in copies:
            cp.wait_recv()
        for cp in copies:
            cp.wait_send()

    return pl.pallas_call(
        body, name=name, in_specs=[ANY], out_specs=ANY,
        out_shape=jax.ShapeDtypeStruct((4, R, C), parts.dtype),
        scratch_shapes=[pltpu.SemaphoreType.DMA((4,)), pltpu.SemaphoreType.DMA((4,))],
    )(parts)


def _add_own(name, parts, got, tr=512):
    _, R, C = parts.shape
    tr = _tile(R, tr, SUBLANE_BF16)

    def body(c_ref, p_ref, g_ref, o_ref):
        o_ref[...] = (p_ref[...].astype(F32) + g_ref[...].astype(F32)).astype(o_ref.dtype)

    return pl.pallas_call(
        body, name=name,
        grid_spec=pltpu.PrefetchScalarGridSpec(
            num_scalar_prefetch=1, grid=(4, R // tr),
            in_specs=[pl.BlockSpec((1, tr, C), lambda q, i, c: (2 * q + c[0], i, 0)),
                      pl.BlockSpec((1, tr, C), lambda q, i, c: (q, i, 0))],
            out_specs=pl.BlockSpec((1, tr, C), lambda q, i, c: (q, i, 0))),
        out_shape=jax.ShapeDtypeStruct((4, R, C), parts.dtype),
        compiler_params=_cparams(("parallel", "parallel")),
    )(lax.axis_index("c").astype(jnp.int32).reshape(1), parts, got)


def _swap_with_chips(name, parts):
    _, R, C = parts.shape

    def body(p_ref, out_ref, send_sems, recv_sems, local_sem):
        x, y, c, chips = _place()
        my_chip = 2 * x + y
        mine = pltpu.make_async_copy(p_ref.at[my_chip], out_ref.at[my_chip], local_sem)
        mine.start()
        sends = [pltpu.make_async_remote_copy(
            src_ref=p_ref.at[2 * cx + cy], dst_ref=out_ref.at[my_chip],
            send_sem=send_sems.at[j], recv_sem=recv_sems.at[j], device_id=(cx, cy, c), device_id_type=MESH)
            for j, (cx, cy) in enumerate(chips)]
        for cp in sends:
            cp.start()
        for j, (cx, cy) in enumerate(chips):
            pltpu.make_async_remote_copy(
                src_ref=p_ref.at[my_chip], dst_ref=out_ref.at[2 * cx + cy],
                send_sem=send_sems.at[j], recv_sem=recv_sems.at[j], device_id=(cx, cy, c), device_id_type=MESH,
            ).wait_recv()
        for cp in sends:
            cp.wait_send()
        mine.wait()

    return pl.pallas_call(
        body, name=name, in_specs=[ANY], out_specs=ANY,
        out_shape=jax.ShapeDtypeStruct((4, R, C), parts.dtype),
        scratch_shapes=[pltpu.SemaphoreType.DMA((3,)), pltpu.SemaphoreType.DMA((3,)), pltpu.SemaphoreType.DMA(())],
    )(parts)


def _sum_parts(name, parts, tr=512):
    P, R, C = parts.shape
    tr = _tile(R, tr, SUBLANE_BF16)

    def body(p_ref, o_ref):
        total = p_ref[0].astype(F32)
        for p in range(1, P):
            total = total + p_ref[p].astype(F32)
        o_ref[...] = total

    return pl.pallas_call(
        body, name=name, grid=(R // tr,),
        in_specs=[pl.BlockSpec((P, tr, C), lambda i: (0, i, 0))], out_specs=pl.BlockSpec((tr, C), lambda i: (i, 0)),
        out_shape=jax.ShapeDtypeStruct((R, C), F32),
        compiler_params=_cparams(("parallel",)),
    )(parts)


_HBM = pl.BlockSpec(memory_space=pltpu.HBM)
_SEM = pl.BlockSpec(memory_space=pltpu.SEMAPHORE)
_EFFECT = pltpu.SideEffectType.DATAFLOW_SIDE_EFFECTING


def _flipped(x, y, c, k):
    px, py, pc = (1 - x if k & 4 else x), (1 - y if k & 2 else y), (1 - c if k & 1 else c)
    return (px, py, pc), 4 * px + 2 * py + pc


def _exchange_start(name, src, per_peer):
    R, C = src.shape[-2:]

    def body(v_ref, land_ref, send_sem, recv_sem, v_thru, land_thru, token):
        x, y, c = lax.axis_index("x"), lax.axis_index("y"), lax.axis_index("c")
        me = 4 * x + 2 * y + c
        for k in range(1, N_DEV):
            peer, idx = _flipped(x, y, c, k)
            pltpu.make_async_remote_copy(
                src_ref=v_ref.at[idx] if per_peer else v_ref, dst_ref=land_ref.at[me],
                send_sem=send_sem, recv_sem=recv_sem, device_id=peer, device_id_type=MESH).start()
        token[...] = jnp.zeros_like(token)

    return pl.pallas_call(
        body, name=name,
        out_shape=(pltpu.SemaphoreType.DMA(()), pltpu.SemaphoreType.DMA(()), pltpu.HBM(src.shape, src.dtype),
                   pltpu.HBM((N_DEV, R, C), src.dtype), jax.ShapeDtypeStruct((8, LANE), F32)),
        in_specs=(_HBM, _HBM), out_specs=(_SEM, _SEM, _HBM, _HBM, pl.BlockSpec(memory_space=pltpu.VMEM)),
        input_output_aliases={0: 2, 1: 3},
        compiler_params=pltpu.CompilerParams(has_side_effects=_EFFECT),
    )(pltpu.with_memory_space_constraint(src, pltpu.HBM),
      pltpu.with_memory_space_constraint(lax.empty((N_DEV, R, C), src.dtype), pltpu.HBM))


def _exchange_wait(name, started, after):
    send_sem, recv_sem, v_thru, land_thru, _ = started

    def body(v_ref, land_ref, send_sem, recv_sem, after_ref, v_dead, got_ref):
        x, y, c = lax.axis_index("x"), lax.axis_index("y"), lax.axis_index("c")
        seven = land_ref.at[pl.ds(0, N_DEV - 1)]
        drain = pltpu.make_async_remote_copy(
            src_ref=seven, dst_ref=seven, send_sem=send_sem, recv_sem=recv_sem,
            device_id=(x, y, c), device_id_type=MESH)
        drain.wait_send()
        drain.wait_recv()

    return pl.pallas_call(
        body, name=name,
        out_shape=(pltpu.HBM(v_thru.shape, v_thru.dtype), pltpu.HBM(land_thru.shape, land_thru.dtype)),
        in_specs=(_HBM, _HBM, _SEM, _SEM, ANY), out_specs=(_HBM, _HBM), input_output_aliases={0: 0, 1: 1},
        compiler_params=pltpu.CompilerParams(has_side_effects=_EFFECT),
    )(v_thru, land_thru, send_sem, recv_sem, after)[1]


def _with_own(landed, own):
    me = 4 * lax.axis_index("x") + 2 * lax.axis_index("y") + lax.axis_index("c")
    return lax.dynamic_update_slice(landed, own[None], (me, 0, 0))


def _after(value, token):
    return lax.optimization_barrier((value, token))[0]


def _reduce_scatter(tag, parts):
    got = _swap_with_sibling("rs_pair_" + tag, parts)
    pair = _add_own("rs_add_" + tag, parts, got)
    quad = _swap_with_chips("rs_chips_" + tag, pair)
    return _sum_parts("rs_sum_" + tag, quad)


def _adamw(name, g_parts, w, m, v, tr=512):
    P, R, C = g_parts.shape
    tr = _tile(R, tr, 8)

    def body(g_ref, w_ref, m_ref, v_ref, go_ref, d_ref, mo_ref, vo_ref):
        g = g_ref[0]
        for p in range(1, P):
            g = g + g_ref[p]
        mn = ADAM_B1 * m_ref[...] + (1.0 - ADAM_B1) * g
        vn = ADAM_B2 * v_ref[...] + (1.0 - ADAM_B2) * (g * g)
        m_hat = mn / (1.0 - ADAM_B1 ** ADAM_STEP)
        v_hat = vn / (1.0 - ADAM_B2 ** ADAM_STEP)
        go_ref[...] = g
        d_ref[...] = -ADAM_LR * (m_hat / (jnp.sqrt(v_hat) + ADAM_EPS) + ADAM_WD * w_ref[...])
        mo_ref[...] = mn
        vo_ref[...] = vn

    row = pl.BlockSpec((tr, C), lambda i: (i, 0))
    return pl.pallas_call(
        body, name=name, grid=(R // tr,),
        in_specs=[pl.BlockSpec((P, tr, C), lambda i: (0, i, 0)), row, row, row], out_specs=[row] * 4,
        out_shape=[jax.ShapeDtypeStruct((R, C), F32)] * 4,
        compiler_params=_cparams(("parallel",)),
    )(g_parts, w, m, v)


def _to_heads(t, n_heads):
    rows = t.shape[0]
    return t.reshape(rows, n_heads, -1).transpose(1, 0, 2)


def _from_heads(t):
    return t.transpose(1, 0, 2).reshape(t.shape[1], -1)


def _pad_rows(t, rows):
    return jnp.pad(t, ((0, rows - t.shape[0]), (0, 0)))


class _Layout:
    def __init__(self, D, ff_shard, in_shard, kv_shard, gate_shard, br_in, br_shard, out_shard):
        self.D = D
        self.in_shard = in_shard
        self.in_pad = -(-in_shard // LANE) * LANE
        self.br_in, self.br_shard = br_in, br_shard
        br_rows = br_shard * br_in // D
        sizes = [("g1", ff_shard), ("u1", ff_shard), ("d1", ff_shard), ("win", self.in_pad), ("kv", kv_shard),
                 ("gate", gate_shard), ("br", br_rows), ("out", out_shard),
                 ("g2", ff_shard), ("u2", ff_shard), ("d2", ff_shard)]
        self.seg, off = {}, 0
        for key, n in sizes:
            assert n % SUBLANE_BF16 == 0, (key, n)
            self.seg[key] = (off, n)
            off += n
        self.rows = off

    def pack(self, parts):
        return jnp.concatenate([parts[key] for key in self.seg], axis=0)

    def take(self, gathered, key):
        off, n = self.seg[key]
        return gathered[:, off:off + n, :].reshape(N_DEV * n, self.D)

    def spread(self, full, key):
        _, n = self.seg[key]
        return full.reshape(N_DEV, n, self.D)


def _pack_layer(lay, l, p):
    D = lay.D
    br = jnp.concatenate([p["w_br_sb"][l], p["w_br_fox"][l], p["w_br_mem"][l]], axis=0)
    parts = {
        "g1": p["ffn1_w_gate"][l].T, "u1": p["ffn1_w_up"][l].T, "d1": p["ffn1_w_down"][l],
        "win": _pad_rows(p["w_in"][l].T, lay.in_pad), "kv": p["w_mem_kv"][l], "gate": p["w_gate"][l].T,
        "br": br.T.reshape(-1, D), "out": p["w_out"][l],
        "g2": p["ffn2_w_gate"][l].T, "u2": p["ffn2_w_up"][l].T, "d2": p["ffn2_w_down"][l],
    }
    return lay.pack({k: t.astype(BF16) for k, t in parts.items()})


def _unpack_layer(lay, gathered):
    D = lay.D
    w = {k: lay.take(gathered, k) for k in ("g1", "u1", "d1", "win", "kv", "out", "g2", "u2", "d2")}
    gate = lay.take(gathered, "gate")
    w["gate"] = gate
    w["gate3"] = [gate[i * D:(i + 1) * D] for i in range(3)]
    br = lay.take(gathered, "br").reshape(N_DEV * lay.br_shard, lay.br_in)
    third = lay.br_in // 3
    w["br3"] = [br[:, i * third:(i + 1) * third] for i in range(3)]
    return w


def _silu_mul(accs, _):
    a, b = accs
    return [a, b, a * jax.nn.sigmoid(a) * b]


def _act_bwd(accs, extras):
    ds, (a, b) = accs[0], extras
    sig = jax.nn.sigmoid(a)
    return [ds * b * (sig * (1.0 + a * (1.0 - sig))), ds * (a * sig)]


def _ffn_fwd(tag, h, pre_g, post_g, wg, wu, wd):
    n = _rms_fwd("ffn_norm_" + tag, h, pre_g, BF16)
    a, b, s = _mm("ffn_up_" + tag, [(n, wg), (n, wu)], "nt", [F32, F32, BF16], _silu_mul, tm=256, tn=1408)
    f = _mm("ffn_down_" + tag, [(s, wd)], "nn", [F32], tm=256)
    out = _rms_fwd("ffn_out_" + tag, f, post_g, F32, res=h, scale=0.5)
    return out, (h, n, a, b, s, f)


def _ffn_bwd(tag, dh, saved, pre_g, post_g, wg, wu, wd):
    h, n, a, b, s, f = saved
    df, d_post = _rms_bwd("ffn_dout_" + tag, f, post_g, dh, BF16, scale=0.5)
    da, db = _mm("ffn_dact_" + tag, [(df, wd)], "nt", [BF16, BF16], _act_bwd, [(a, 0), (b, 0)], tm=256, tn=1408)
    d_wd = _mm("ffn_dwd_" + tag, [(s, df)], "tn", [BF16], tm=256)
    dn = _mm("ffn_dn_" + tag, [(da, wg), (db, wu)], "nn", [F32], _sum_accs, tm=256)
    d_wg = _mm("ffn_dwg_" + tag, [(da, n)], "tn", [BF16], tm=256)
    d_wu = _mm("ffn_dwu_" + tag, [(db, n)], "tn", [BF16], tm=256)
    dh_in, d_pre = _rms_bwd("ffn_dnorm_" + tag, h, pre_g, dn, F32, res=dh)
    return dh_in, d_pre, d_post, d_wg, d_wu, d_wd


def _unpad_proj(lay, projp):
    T = projp.shape[0]
    return projp.reshape(T, N_DEV, lay.in_pad)[:, :, :lay.in_shard].reshape(T, N_DEV * lay.in_shard)


def _pad_proj(lay, proj):
    T = proj.shape[0]
    t = proj.reshape(T, N_DEV, lay.in_shard)
    return jnp.pad(t, ((0, 0), (0, 0), (0, lay.in_pad - lay.in_shard))).reshape(T, N_DEV * lay.in_pad)


_SB_W = N_SB_HEADS * HEAD_DIM
_FOX_W = N_FOX_HEADS * HEAD_DIM
_SPLITS = [_SB_W, 2 * _SB_W, 3 * _SB_W, 3 * _SB_W + _FOX_W, 3 * _SB_W + 2 * _FOX_W, 3 * _SB_W + 3 * _FOX_W,
           3 * _SB_W + 3 * _FOX_W + N_FOX_HEADS]


def _gate_act(accs, extras):
    return [jax.nn.sigmoid(accs[0] + extras[0])]


def _merge(accs, extras):
    return [extras[0] * accs[0] + extras[1] * accs[1] + extras[2] * accs[2]]


def _merge_bwd(accs, extras):
    dm = accs[0]
    d_branch = [dm * gi for gi in extras]
    d_gate = [dm * bi * gi * (1.0 - gi) for bi, gi in zip(accs[1:], extras)]
    return d_branch + d_gate


def _mix_fwd(lay, h, w, pre_g, post_g, b_forget, b_gate, mem_n):
    D = lay.D
    u = _rms_fwd("mix_norm", h, pre_g, BF16)
    projp = _mm("mix_in", [(u, w["win"])], "nt", [F32])
    q_sb, k_sb, v_sb, q_fx, k_fx, v_fx, f_logit, q_mem = jnp.split(_unpad_proj(lay, projp), _SPLITS, axis=1)
    sb = [_to_heads(t.astype(BF16), N_SB_HEADS) for t in (q_sb, k_sb, v_sb)]
    fx = [_to_heads(t.astype(BF16), N_FOX_HEADS) for t in (q_fx, k_fx, v_fx)]
    qm = _to_heads(q_mem.astype(BF16), N_MEM_HEADS)
    fl = f_logit.T
    c = _decay_fwd(fl, b_forget.reshape(-1, 1))
    o_sb, rtot = _sb_fwd(*sb, HEAD_DIM ** -0.5)
    o_fx, lse_fx = _attn_fwd("fox_fwd", *fx, HEAD_DIM ** -0.5, c)
    kvm = _mm("mem_kv", [(mem_n, w["kv"])], "nn", [BF16])
    half = kvm.shape[1] // 2
    km, vm = _to_heads(kvm[:, :half], N_MEM_HEADS), _to_heads(kvm[:, half:], N_MEM_HEADS)
    o_mem, lse_mem = _attn_fwd("mem_fwd", qm, km, vm, qm.shape[-1] ** -0.5)
    gates = _mm("mix_gate", [(u, w["gate"])], "nt", [F32], _gate_act, [(b_gate.reshape(1, -1), 0)])
    flat = [_from_heads(o).astype(BF16) for o in (o_sb, o_fx, o_mem)]
    merged = _mm("mix_merge", list(zip(flat, w["br3"])), "nt", [BF16], _merge,
                 [(gates, 0), (gates, D), (gates, 2 * D)])
    z = _mm("mix_out", [(merged, w["out"])], "nn", [F32])
    out = _rms_fwd("mix_res", z, post_g, F32, res=h)
    saved = (h, u, sb, fx, qm, fl, c, o_sb, rtot, o_fx, lse_fx, km, vm, o_mem, lse_mem, gates, flat, merged, z)
    return out, saved


def _mix_bwd(lay, dh, saved, w, pre_g, post_g, b_forget, mem_n, dmem_n):
    D = lay.D
    h, u, sb, fx, qm, fl, c, o_sb, rtot, o_fx, lse_fx, km, vm, o_mem, lse_mem, gates, flat, merged, z = saved
    dz, d_post = _rms_bwd("mix_dres", z, post_g, dh, BF16)
    outs = _mm("mix_dmerge", [(dz, w["out"])] + list(zip(flat, w["br3"])), "nt", [BF16] * 6, _merge_bwd,
               [(gates, 0), (gates, D), (gates, 2 * D)])
    d_branch, d_gate = outs[:3], outs[3:]
    d_wout = _mm("mix_dwout", [(merged, dz)], "tn", [BF16])
    d_o = [_mm("mix_dbr%d" % i, [(d_branch[i], w["br3"][i])], "nn", [BF16]) for i in range(3)]
    d_wbr = [_mm("mix_dwbr%d" % i, [(d_branch[i], flat[i])], "tn", [BF16]) for i in range(3)]
    d_bgate = jnp.concatenate([_colsum("mix_dbgate%d" % i, d_gate[i]) for i in range(3)])
    d_wgate = [_mm("mix_dwgate%d" % i, [(d_gate[i], u)], "tn", [BF16]) for i in range(3)]

    dq_s, dk_s, dv_s = _sb_bwd(*sb, _to_heads(d_o[0], N_SB_HEADS), rtot, HEAD_DIM ** -0.5)
    dq_f, dk_f, dv_f, dc, dc_rows = _attn_bwd("fox_bwd", *fx, o_fx, _to_heads(d_o[1], N_FOX_HEADS), lse_fx,
                                     HEAD_DIM ** -0.5, c)
    dq_m, dk_m, dv_m = _attn_bwd("mem_bwd", qm, km, vm, o_mem, _to_heads(d_o[2], N_MEM_HEADS), lse_mem,
                                 qm.shape[-1] ** -0.5)
    dfl, d_bforget = _decay_bwd(dc, dc_rows, fl, b_forget.reshape(-1, 1))
    dproj = jnp.concatenate(
        [_from_heads(t).astype(BF16) for t in (dq_s, dk_s, dv_s, dq_f, dk_f, dv_f)]
        + [dfl.T.astype(BF16), _from_heads(dq_m).astype(BF16)], axis=1)
    dprojp = _pad_proj(lay, dproj)
    du = _mm("mix_du", list(zip(d_gate, w["gate3"])) + [(dprojp, w["win"])], "nn", [F32], _sum_accs, tm=256)
    d_win = _mm("mix_dwin", [(dprojp, u)], "tn", [BF16])
    dh_in, d_pre = _rms_bwd("mix_dnorm", h, pre_g, du, F32, res=dh)

    dkvm = jnp.concatenate([_from_heads(dk_m), _from_heads(dv_m)], axis=1).astype(BF16)
    d_wkv = _mm("mem_dwkv", [(mem_n, dkvm)], "tn", [BF16])
    dmem_n = _mm("mem_dn", [(dkvm, w["kv"])], "nt", [F32], lambda accs, ex: [accs[0] + ex[0]], [(dmem_n, 0)])
    grads = {"win": d_win, "kv": d_wkv, "gate": jnp.concatenate(d_wgate, axis=0),
             "br": jnp.concatenate(d_wbr, axis=1), "out": d_wout}
    return dh_in, d_pre, d_post, d_bforget, d_bgate, grads, dmem_n


def _layer_fwd(lay, h, w, sp, mem_n):
    h1, s1 = _ffn_fwd("1", h, sp["ffn1_pre_g"], sp["ffn1_post_g"], w["g1"], w["u1"], w["d1"])
    h2, s2 = _mix_fwd(lay, h1, w, sp["mix_pre_g"], sp["mix_post_g"], sp["b_forget"], sp["b_gate"], mem_n)
    h3, s3 = _ffn_fwd("2", h2, sp["ffn2_pre_g"], sp["ffn2_post_g"], w["g2"], w["u2"], w["d2"])
    return h3, (s1, s2, s3)


def _layer_bwd(lay, dh, saved, w, sp, mem_n, dmem_n):
    s1, s2, s3 = saved
    dh, d_pre2, d_post2, d_g2, d_u2, d_d2 = _ffn_bwd("2", dh, s3, sp["ffn2_pre_g"], sp["ffn2_post_g"],
                                                     w["g2"], w["u2"], w["d2"])
    dh, d_mpre, d_mpost, d_bforget, d_bgate, g, dmem_n = _mix_bwd(
        lay, dh, s2, w, sp["mix_pre_g"], sp["mix_post_g"], sp["b_forget"], mem_n, dmem_n)
    dh, d_pre1, d_post1, d_g1, d_u1, d_d1 = _ffn_bwd("1", dh, s1, sp["ffn1_pre_g"], sp["ffn1_post_g"],
                                                     w["g1"], w["u1"], w["d1"])
    g.update({"g1": d_g1, "u1": d_u1, "d1": d_d1, "g2": d_g2, "u2": d_u2, "d2": d_d2})
    g["br"] = g["br"].reshape(N_DEV, lay.br_shard, lay.br_in).reshape(-1, lay.D)
    packed = jnp.concatenate([lay.spread(g[key], key) for key in lay.seg], axis=1)
    small = {"ffn1_pre_g": d_pre1, "ffn1_post_g": d_post1, "mix_pre_g": d_mpre, "mix_post_g": d_mpost,
             "ffn2_pre_g": d_pre2, "ffn2_post_g": d_post2, "b_gate": d_bgate, "b_forget": d_bforget}
    return dh, packed, small, dmem_n


_SHARDED = ["ffn1_w_gate", "ffn1_w_up", "ffn1_w_down", "w_in", "w_mem_kv", "w_gate", "w_br_sb", "w_br_fox",
            "w_br_mem", "w_out", "ffn2_w_gate", "ffn2_w_up", "ffn2_w_down"]
_SMALL_LAYER = ["ffn1_pre_g", "ffn1_post_g", "mix_pre_g", "mix_post_g", "ffn2_pre_g", "ffn2_post_g", "b_gate",
                "b_forget"]
_WEIGHTS = ["ffn1_pre_g", "ffn1_post_g", "ffn1_w_gate", "ffn1_w_up", "ffn1_w_down", "mix_pre_g", "mix_post_g",
            "w_in", "b_forget", "mem_norm_g", "w_mem_kv", "w_gate", "b_gate", "w_br_sb", "w_br_fox", "w_br_mem",
            "w_out", "ffn2_pre_g", "ffn2_post_g", "ffn2_w_gate", "ffn2_w_up", "ffn2_w_down"]


def _pack_small(vals, L, D):
    rows = []
    for l in range(L):
        for name in _SMALL_LAYER:
            t = vals[name][l]
            rows.append(jnp.pad(t, (0, -t.shape[0] % D)).reshape(-1, D))
    rows.append(vals["mem_norm_g"].reshape(1, D))
    packed = jnp.concatenate(rows, axis=0)
    return _pad_rows(packed, -(-packed.shape[0] // 8) * 8)


def _unpack_small(packed, shapes, L, D):
    out = {name: [] for name in _SMALL_LAYER}
    r = 0
    for l in range(L):
        for name in _SMALL_LAYER:
            n = shapes[name][1]
            nr = -(-n // D)
            out[name].append(packed[r:r + nr].reshape(-1)[:n])
            r += nr
    res = {name: jnp.stack(v) for name, v in out.items()}
    res["mem_norm_g"] = packed[r]
    return res


def _unpack_grads(lay, g, l_shapes):
    def seg(key):
        off, n = lay.seg[key]
        return g[off:off + n]
    br = seg("br").reshape(lay.br_shard, lay.br_in).T
    third = lay.br_in // 3
    return {
        "ffn1_w_gate": seg("g1").T, "ffn1_w_up": seg("u1").T, "ffn1_w_down": seg("d1"),
        "w_in": seg("win")[:lay.in_shard].T, "w_mem_kv": seg("kv"), "w_gate": seg("gate").T,
        "w_br_sb": br[:third], "w_br_fox": br[third:2 * third], "w_br_mem": br[2 * third:],
        "w_out": seg("out"), "ffn2_w_gate": seg("g2").T, "ffn2_w_up": seg("u2").T, "ffn2_w_down": seg("d2"),
    }


class _Exchanges:
    def gather(self, name, block):
        return _all_gather(name, block)

    def gather_start(self, block):
        return _exchange_start("ag_start", block, per_peer=False)

    def gather_wait(self, started, after, block):
        return _with_own(_exchange_wait("ag_wait", started, after), block)

    def scatter(self, parts):
        return _reduce_scatter("w", parts)

    def scatter_start(self, parts):
        return _exchange_start("rs_start", parts, per_peer=True)

    def scatter_wait(self, started, after, parts):
        me = 4 * lax.axis_index("x") + 2 * lax.axis_index("y") + lax.axis_index("c")
        own = lax.dynamic_index_in_dim(parts, me, 0, keepdims=False)
        return _sum_parts("rs_sum8", _with_own(_exchange_wait("rs_wait", started, after), own))

    def token(self, started):
        return started[4]

    def loss_sum(self, part):
        return lax.psum(part, ("x", "y", "c"))


def _step(p, m, v, x, mem, tgt, ex):
    L, D = p["ffn1_pre_g"].shape
    lay = _Layout(D, p["ffn1_w_gate"].shape[2], p["w_in"].shape[2], p["w_mem_kv"].shape[1], p["w_gate"].shape[2],
                  3 * p["w_br_sb"].shape[1], p["w_br_sb"].shape[2], p["w_out"].shape[1])
    blocks = [_pack_layer(lay, l, p) for l in range(L)]
    sps = [{name: p[name][l] for name in _SMALL_LAYER} for l in range(L)]

    mem_n = _rms_fwd("mem_norm", mem, p["mem_norm_g"], BF16)
    gathered = ex.gather("ag_weights", blocks[0])
    h, saved, ws = x, [], []
    for l in range(L):
        if l + 1 < L:
            nxt, gathered = lax.optimization_barrier((blocks[l + 1], gathered))
            started = ex.gather_start(nxt)
            h = _after(h, ex.token(started))
        ws.append(_unpack_layer(lay, gathered))
        h, s = _layer_fwd(lay, h, ws[l], sps[l], mem_n)
        saved.append(s)
        if l + 1 < L:
            gathered = ex.gather_wait(started, h, blocks[l + 1])
    loss_part, dh = _loss_grad(h, tgt)
    loss = ex.loss_sum(loss_part)

    dmem_n = jnp.zeros(mem.shape, F32)
    big, small = [None] * L, {name: [None] * L for name in _SMALL_LAYER}
    flying = {}
    for l in reversed(range(L)):
        dh, packed, sm, dmem_n = _layer_bwd(lay, dh, saved[l], ws[l], sps[l], mem_n, dmem_n)
        if l > 0:
            flying[l] = (ex.scatter_start(packed), packed)
            dh = _after(dh, ex.token(flying[l][0]))
        else:
            big[l] = _unpack_grads(lay, ex.scatter(packed), None)
        for name in _SMALL_LAYER:
            small[name][l] = sm[name]
    for l, (started, packed) in flying.items():
        big[l] = _unpack_grads(lay, ex.scatter_wait(started, dh, packed), None)
    _, d_memg = _rms_bwd("mem_dnorm", mem, p["mem_norm_g"], dmem_n, F32)

    small_g = {name: jnp.stack(vs) for name, vs in small.items()}
    small_g["mem_norm_g"] = d_memg
    small_names = _SMALL_LAYER + ["mem_norm_g"]
    shapes = {name: p[name].shape for name in small_names}
    g_all = ex.gather("ag_small", _pack_small(small_g, L, D))
    packs = [_pack_small({name: t[name] for name in small_names}, L, D) for t in (p, m, v)]
    res = [_unpack_small(t, shapes, L, D) for t in _adamw("adamw_small", g_all, *packs)]

    out = {kind: {} for kind in ("grad", "delta", "new_m", "new_v")}
    for name in small_names:
        for kind, r in zip(("grad", "delta", "new_m", "new_v"), res):
            out[kind][name] = r[name].reshape(p[name].shape)
    for name in _SHARDED:
        g = jnp.stack([big[l][name] for l in range(L)])
        shp = g.shape
        flat = lambda t: t.reshape(-1, shp[-1])
        r = _adamw("adamw_" + name, flat(g)[None], flat(p[name]), flat(m[name]), flat(v[name]))
        for kind, t in zip(("grad", "delta", "new_m", "new_v"), r):
            out[kind][name] = t.reshape(shp)
    return loss, dh, out


def kernel(x, mem, ffn1_pre_g, ffn1_post_g, ffn1_w_gate, ffn1_w_up, ffn1_w_down, mix_pre_g, mix_post_g, w_in, b_forget, mem_norm_g, w_mem_kv, w_gate, b_gate, w_br_sb, w_br_fox, w_br_mem, w_out, ffn2_pre_g, ffn2_post_g, ffn2_w_gate, ffn2_w_up, ffn2_w_down, loss_target, m_ffn1_pre_g, m_ffn1_post_g, m_ffn1_w_gate, m_ffn1_w_up, m_ffn1_w_down, m_mix_pre_g, m_mix_post_g, m_w_in, m_b_forget, m_mem_norm_g, m_w_mem_kv, m_w_gate, m_b_gate, m_w_br_sb, m_w_br_fox, m_w_br_mem, m_w_out, m_ffn2_pre_g, m_ffn2_post_g, m_ffn2_w_gate, m_ffn2_w_up, m_ffn2_w_down, v_ffn1_pre_g, v_ffn1_post_g, v_ffn1_w_gate, v_ffn1_w_up, v_ffn1_w_down, v_mix_pre_g, v_mix_post_g, v_w_in, v_b_forget, v_mem_norm_g, v_w_mem_kv, v_w_gate, v_b_gate, v_w_br_sb, v_w_br_fox, v_w_br_mem, v_w_out, v_ffn2_pre_g, v_ffn2_post_g, v_ffn2_w_gate, v_ffn2_w_up, v_ffn2_w_down):
    p = dict(zip(_WEIGHTS, (ffn1_pre_g, ffn1_post_g, ffn1_w_gate, ffn1_w_up, ffn1_w_down, mix_pre_g, mix_post_g, w_in, b_forget, mem_norm_g, w_mem_kv, w_gate, b_gate, w_br_sb, w_br_fox, w_br_mem, w_out, ffn2_pre_g, ffn2_post_g, ffn2_w_gate, ffn2_w_up, ffn2_w_down)))
    m = dict(zip(_WEIGHTS, (m_ffn1_pre_g, m_ffn1_post_g, m_ffn1_w_gate, m_ffn1_w_up, m_ffn1_w_down, m_mix_pre_g, m_mix_post_g, m_w_in, m_b_forget, m_mem_norm_g, m_w_mem_kv, m_w_gate, m_b_gate, m_w_br_sb, m_w_br_fox, m_w_br_mem, m_w_out, m_ffn2_pre_g, m_ffn2_post_g, m_ffn2_w_gate, m_ffn2_w_up, m_ffn2_w_down)))
    v = dict(zip(_WEIGHTS, (v_ffn1_pre_g, v_ffn1_post_g, v_ffn1_w_gate, v_ffn1_w_up, v_ffn1_w_down, v_mix_pre_g, v_mix_post_g, v_w_in, v_b_forget, v_mem_norm_g, v_w_mem_kv, v_w_gate, v_b_gate, v_w_br_sb, v_w_br_fox, v_w_br_mem, v_w_out, v_ffn2_pre_g, v_ffn2_post_g, v_ffn2_w_gate, v_ffn2_w_up, v_ffn2_w_down)))
    loss, dx, out = _step(p, m, v, x[0], mem[0], loss_target[0], _Exchanges())
    return (loss, dx[None], *[out["grad"][n] for n in _WEIGHTS], *[out["delta"][n] for n in _WEIGHTS],
            *[out["new_m"][n] for n in _WEIGHTS], *[out["new_v"][n] for n in _WEIGHTS])
```

```python
import functools
import math

import jax
import jax.numpy as jnp
from jax import lax
from jax.experimental import pallas as pl
from jax.experimental.pallas import tpu as pltpu

F32 = jnp.float32
BF16 = jnp.bfloat16

LANE = 128
SUBLANE_BF16 = 16
VMEM_LIMIT = 48 * 1024 * 1024
N_DEV = 8
MESH = pl.DeviceIdType.MESH
ANY = pl.BlockSpec(memory_space=pl.ANY)

RMS_EPS = 1e-6
HEAD_DIM = 64
N_SB_HEADS = 8
N_FOX_HEADS = 8
N_MEM_HEADS = 4
NEG = -1e30
ATT_TQ = 512
ATT_TK = 256
DECAY_TK = 128

ADAM_LR = 0.001
ADAM_B1 = 0.9
ADAM_B2 = 0.999
ADAM_EPS = 1e-08
ADAM_WD = 0.01
ADAM_STEP = 10


def _tile(n, target, mult=LANE):
    best = None
    for t in range(mult, min(n, target) + 1, mult):
        if n % t == 0:
            best = t
    return best if best is not None else n


def _cparams(sem):
    return pltpu.CompilerParams(dimension_semantics=sem, vmem_limit_bytes=VMEM_LIMIT)


_DIMS = {"nn": (((1,), (0,)), ((), ())), "nt": (((1,), (1,)), ((), ())), "tn": (((0,), (0,)), ((), ()))}


def _dot(a, b, mode="nn"):
    return lax.dot_general(a.astype(BF16), b.astype(BF16), _DIMS[mode], preferred_element_type=F32)


def _mm(name, pairs, mode, out_dtypes, epilogue=None, extras=(), tm=512, tn=512):
    a0, b0 = pairs[0]
    M = a0.shape[1] if mode == "tn" else a0.shape[0]
    N = b0.shape[0] if mode == "nt" else b0.shape[1]
    tm = _tile(M, tm)
    tn = _tile(N, tn)
    np_, ne, no = len(pairs), len(extras), len(out_dtypes)

    def body(*refs):
        a_refs, b_refs = refs[:np_], refs[np_:2 * np_]
        e_refs = refs[2 * np_:2 * np_ + ne]
        o_refs = refs[2 * np_ + ne:]
        accs = [_dot(a[...], b[...], mode) for a, b in zip(a_refs, b_refs)]
        outs = epilogue(accs, [e[...] for e in e_refs]) if epilogue is not None else accs
        for o, val in zip(o_refs, outs):
            o[...] = val.astype(o.dtype)

    in_specs = []
    for a, _ in pairs:
        if mode == "tn":
            in_specs.append(pl.BlockSpec((a.shape[0], tm), lambda j, i: (0, i)))
        else:
            in_specs.append(pl.BlockSpec((tm, a.shape[1]), lambda j, i: (i, 0)))
    for _, b in pairs:
        if mode == "nt":
            in_specs.append(pl.BlockSpec((tn, b.shape[1]), lambda j, i: (j, 0)))
        else:
            in_specs.append(pl.BlockSpec((b.shape[0], tn), lambda j, i: (0, j)))
    for e, off in extras:
        if e.shape[0] == 1:
            in_specs.append(pl.BlockSpec((1, tn), functools.partial(lambda j, i, o: (0, j + o), o=off // tn)))
        else:
            in_specs.append(pl.BlockSpec((tm, tn), functools.partial(lambda j, i, o: (i, j + o), o=off // tn)))
    out_specs = [pl.BlockSpec((tm, tn), lambda j, i: (i, j)) for _ in range(no)]
    outs = pl.pallas_call(
        body, name=name, grid=(N // tn, M // tm),
        in_specs=in_specs, out_specs=out_specs,
        out_shape=[jax.ShapeDtypeStruct((M, N), dt) for dt in out_dtypes],
        compiler_params=_cparams(("parallel", "parallel")),
    )(*[a for a, _ in pairs], *[b for _, b in pairs], *[e for e, _ in extras])
    return outs[0] if no == 1 else outs


def _sum_accs(accs, _):
    total = accs[0]
    for acc in accs[1:]:
        total = total + acc
    return [total]


def _rstd(x):
    return lax.rsqrt(jnp.mean(x * x, axis=-1, keepdims=True) + RMS_EPS)


def _rms_fwd(name, x, g, out_dtype, res=None, scale=1.0, tr=512):
    R, D = x.shape
    tr = _tile(R, tr, 8)
    has_res = res is not None

    def body(*refs):
        x_ref, g_ref = refs[:2]
        o_ref = refs[-1]
        xv = x_ref[...]
        y = (xv * _rstd(xv)) * g_ref[...]
        if has_res:
            y = refs[2][...] + scale * y
        o_ref[...] = y.astype(o_ref.dtype)

    row = pl.BlockSpec((tr, D), lambda i: (i, 0))
    gain = pl.BlockSpec((1, D), lambda i: (0, 0))
    return pl.pallas_call(
        body, name=name, grid=(R // tr,),
        in_specs=[row, gain] + ([row] if has_res else []), out_specs=row,
        out_shape=jax.ShapeDtypeStruct((R, D), out_dtype),
        compiler_params=_cparams(("parallel",)),
    )(x, g.reshape(1, D), *([res] if has_res else []))


def _rms_bwd(name, x, g, dy, out_dtype, scale=1.0, res=None, tr=512):
    R, D = x.shape
    tr = _tile(R, tr, 8)
    has_res = res is not None

    def body(*refs):
        x_ref, g_ref, dy_ref = refs[:3]
        dx_ref, dg_ref = refs[-2:]
        i = pl.program_id(0)
        xv = x_ref[...]
        xhat = xv * _rstd(xv)
        dyv = dy_ref[...].astype(F32) * scale
        gy = dyv * g_ref[...]
        dx = _rstd(xv) * (gy - xhat * jnp.mean(gy * xhat, axis=-1, keepdims=True))
        if has_res:
            dx = refs[3][...] + dx
        dx_ref[...] = dx.astype(dx_ref.dtype)
        part = jnp.sum(dyv * xhat, axis=0, keepdims=True)

        @pl.when(i == 0)
        def _():
            dg_ref[...] = part

        @pl.when(i > 0)
        def _():
            dg_ref[...] += part

    row = pl.BlockSpec((tr, D), lambda i: (i, 0))
    gain = pl.BlockSpec((1, D), lambda i: (0, 0))
    dx, dg = pl.pallas_call(
        body, name=name, grid=(R // tr,),
        in_specs=[row, gain, row] + ([row] if has_res else []), out_specs=[row, gain],
        out_shape=[jax.ShapeDtypeStruct((R, D), out_dtype), jax.ShapeDtypeStruct((1, D), F32)],
        compiler_params=_cparams(("arbitrary",)),
    )(x, g.reshape(1, D), dy, *([res] if has_res else []))
    return dx, dg[0]


def _loss_grad(y, tgt, tr=512):
    R, D = y.shape
    tr = _tile(R, tr, 8)

    def body(y_ref, t_ref, dy_ref, loss_ref):
        i = pl.program_id(0)
        d = y_ref[...] - t_ref[...]
        dy_ref[...] = d / D
        part = 0.5 * jnp.sum(jnp.mean(d * d, axis=-1, keepdims=True), axis=0, keepdims=True)
        tile = jnp.broadcast_to(part, loss_ref.shape)

        @pl.when(i == 0)
        def _():
            loss_ref[...] = tile

        @pl.when(i > 0)
        def _():
            loss_ref[...] += tile

    row = pl.BlockSpec((tr, D), lambda i: (i, 0))
    dy, loss = pl.pallas_call(
        body, name="loss_grad", grid=(R // tr,),
        in_specs=[row, row], out_specs=[row, pl.BlockSpec((8, LANE), lambda i: (0, 0))],
        out_shape=[jax.ShapeDtypeStruct((R, D), F32), jax.ShapeDtypeStruct((8, LANE), F32)],
        compiler_params=_cparams(("arbitrary",)),
    )(y, tgt)
    return loss[0, 0], dy


def _colsum(name, x, tr=512, tn=1024):
    R, N = x.shape
    tr, tn = _tile(R, tr, 8), _tile(N, tn)

    def body(x_ref, o_ref):
        i = pl.program_id(1)
        part = jnp.sum(x_ref[...].astype(F32), axis=0, keepdims=True)

        @pl.when(i == 0)
        def _():
            o_ref[...] = part

        @pl.when(i > 0)
        def _():
            o_ref[...] += part

    out = pl.pallas_call(
        body, name=name, grid=(N // tn, R // tr),
        in_specs=[pl.BlockSpec((tr, tn), lambda j, i: (i, j))], out_specs=pl.BlockSpec((1, tn), lambda j, i: (0, j)),
        out_shape=jax.ShapeDtypeStruct((1, N), F32),
        compiler_params=_cparams(("parallel", "arbitrary")),
    )(x)
    return out[0]


def _tri(tk, rel):
    j = lax.broadcasted_iota(jnp.int32, (tk, tk), 0)
    s = lax.broadcasted_iota(jnp.int32, (tk, tk), 1)
    return rel(j, s).astype(BF16)


def _dot_split(x, m, parts=2):
    total = None
    rem = x
    for _ in range(parts):
        piece = rem.astype(BF16)
        rem = rem - piece.astype(F32)
        term = jnp.dot(piece, m, preferred_element_type=F32)
        total = term if total is None else total + term
    return total


def _log_not_and_beta(z, mask):
    ln = -(jnp.maximum(z, 0.0) + jnp.log(1.0 + jnp.exp(-jnp.abs(z))))
    return (ln if mask is None else jnp.where(mask, ln, 0.0)), ln + z


def _att_tiles(T, Tk, causal):
    tq = min(ATT_TQ, T)
    tk = min(ATT_TK, tq if causal else Tk)
    return tq, tk, (tq if causal else Tk) // tk


def _key_base(j, tq):
    return j * tq if isinstance(j, int) else pl.multiple_of(j * tq, tq)


def _is_pow2(scale):
    return math.log2(scale).is_integer()


def _sb_fwd(q, k, v, scale):
    H, T, d = q.shape
    tq, tk, nsub = _att_tiles(T, T, True)
    assert _is_pow2(scale)

    def body(q_ref, k_ref, v_ref, o_ref, rt_ref, acc_ref, r_ref):
        qi = pl.program_id(1)
        qv = q_ref[0] * scale
        acc_ref[...] = jnp.zeros_like(acc_ref)
        r_ref[...] = jnp.zeros_like(r_ref)
        row = lax.broadcasted_iota(jnp.int32, (tq, tk), 0)
        col = lax.broadcasted_iota(jnp.int32, (tq, tk), 1)
        after = _tri(tk, lambda j, s: j > s)

        def step(j, diagonal):
            base = _key_base(j, tq)
            parts = []
            for u in reversed(range(nsub)):
                z = _dot(qv, k_ref[0, pl.ds(base + u * tk, tk), :], "nt")
                mask = (col + u * tk) < row if diagonal else None
                ln, lb = _log_not_and_beta(z, mask)
                between = _dot_split(ln, after)
                parts.append((u, lb, between, between[:, 0:1] + ln[:, 0:1], mask))
            r = r_ref[...]
            out = None
            for u, lb, between, total, mask in parts:
                w = jnp.exp(lb + between + r)
                if diagonal:
                    w = jnp.where(mask, w, 0.0)
                term = _dot(w, v_ref[0, pl.ds(base + u * tk, tk), :])
                out = term if out is None else out + term
                r = r + total
            acc_ref[...] += out
            r_ref[...] = r

        def below(i, carry):
            step(qi - 1 - i, False)
            return carry

        step(qi, True)
        lax.fori_loop(0, qi, below, 0)
        o_ref[0] = acc_ref[...]
        rt_ref[0] = r_ref[...]

    blk = pl.BlockSpec((1, tq, d), lambda h, i: (h, i, 0))
    full = pl.BlockSpec((1, T, d), lambda h, i: (h, 0, 0))
    col = pl.BlockSpec((1, tq, 1), lambda h, i: (h, i, 0))
    return pl.pallas_call(
        body, name="sb_fwd", grid=(H, T // tq),
        in_specs=[blk, full, full], out_specs=[blk, col],
        out_shape=[jax.ShapeDtypeStruct((H, T, d), F32), jax.ShapeDtypeStruct((H, T, 1), F32)],
        scratch_shapes=[pltpu.VMEM((tq, d), F32), pltpu.VMEM((tq, 1), F32)],
        compiler_params=_cparams(("parallel", "arbitrary")),
    )(q, k, v)


def _sb_bwd(q, k, v, do, rtot, scale):
    H, T, d = q.shape
    tq, tk, nsub = _att_tiles(T, T, True)
    assert _is_pow2(scale)

    def body(q_ref, k_ref, v_ref, do_ref, rt_ref, dq_ref, dk_ref, dv_ref, dq_acc, p_ref, c_ref):
        qi = pl.program_id(1)

        @pl.when(qi == 0)
        def _():
            dk_ref[...] = jnp.zeros_like(dk_ref)
            dv_ref[...] = jnp.zeros_like(dv_ref)

        qv = q_ref[0] * scale
        dov = do_ref[0]
        rt = rt_ref[0]
        dq_acc[...] = jnp.zeros_like(dq_acc)
        p_ref[...] = jnp.zeros_like(p_ref)
        c_ref[...] = jnp.zeros_like(c_ref)
        row = lax.broadcasted_iota(jnp.int32, (tq, tk), 0)
        col = lax.broadcasted_iota(jnp.int32, (tq, tk), 1)
        upto = _tri(tk, lambda j, s: j <= s)
        before = _tri(tk, lambda j, s: j < s)

        def step(j, diagonal):
            base = _key_base(j, tq)
            first = []
            for u in range(nsub):
                ks = base + u * tk
                kv = k_ref[0, pl.ds(ks, tk), :]
                z = _dot(qv, kv, "nt")
                mask = (col + u * tk) < row if diagonal else None
                ln, lb = _log_not_and_beta(z, mask)
                dw = _dot(dov, v_ref[0, pl.ds(ks, tk), :], "nt")
                first.append((ks, kv, mask, lb, jnp.exp(lb), _dot_split(ln, upto), dw))
            pre, cpre = p_ref[...], c_ref[...]
            dq = None
            for ks, kv, mask, lb, sig, local, dw in first:
                prefix = local + pre
                w = jnp.exp(lb + (rt - prefix))
                if diagonal:
                    w = jnp.where(mask, w, 0.0)
                g = dw * w
                c = _dot_split(g, before) + cpre
                dz = g * (1.0 - sig) - c * sig
                if diagonal:
                    dz = jnp.where(mask, dz, 0.0)
                term = _dot(dz, kv)
                dq = term if dq is None else dq + term
                dk_ref[0, pl.ds(ks, tk), :] += _dot(dz, qv, "tn")
                dv_ref[0, pl.ds(ks, tk), :] += _dot(w, dov, "tn")
                pre = prefix[:, tk - 1:tk]
                cpre = c[:, tk - 1:tk] + g[:, tk - 1:tk]
            dq_acc[...] += dq
            p_ref[...] = pre
            c_ref[...] = cpre

        def below(j, carry):
            step(j, False)
            return carry

        lax.fori_loop(0, qi, below, 0)
        step(qi, True)
        dq_ref[0] = (dq_acc[...] * scale).astype(dq_ref.dtype)

    blk = pl.BlockSpec((1, tq, d), lambda h, i: (h, i, 0))
    full = pl.BlockSpec((1, T, d), lambda h, i: (h, 0, 0))
    col = pl.BlockSpec((1, tq, 1), lambda h, i: (h, i, 0))
    return pl.pallas_call(
        body, name="sb_bwd", grid=(H, T // tq),
        in_specs=[blk, full, full, blk, col], out_specs=[blk, full, full],
        out_shape=[jax.ShapeDtypeStruct((H, T, d), BF16), jax.ShapeDtypeStruct((H, T, d), F32),
                   jax.ShapeDtypeStruct((H, T, d), F32)],
        scratch_shapes=[pltpu.VMEM((tq, d), F32), pltpu.VMEM((tq, 1), F32), pltpu.VMEM((tq, 1), F32)],
        compiler_params=_cparams(("parallel", "arbitrary")),
    )(q, k, v, do, rtot)


def _attn_fwd(name, q, k, v, scale, c=None):
    H, T, d = q.shape
    Tk = k.shape[1]
    causal = c is not None
    tq, tk, nsub = _att_tiles(T, Tk, causal)
    fold = _is_pow2(scale)

    def body(*refs):
        q_ref, k_ref, v_ref = refs[:3]
        cc_ref, cr_ref = refs[3:5] if causal else (None, None)
        o_ref, lse_ref, m_ref, l_ref, acc_ref = refs[-5:]
        qi = pl.program_id(1)
        qv = q_ref[0] * scale if fold else q_ref[0]
        m_ref[...] = jnp.full_like(m_ref, NEG)
        l_ref[...] = jnp.zeros_like(l_ref)
        acc_ref[...] = jnp.zeros_like(acc_ref)
        row = lax.broadcasted_iota(jnp.int32, (tq, tk), 0)
        col = lax.broadcasted_iota(jnp.int32, (tq, tk), 1)

        def step(j, diagonal):
            base = _key_base(j, tq)
            zs = []
            for u in range(nsub):
                z = _dot(qv, k_ref[0, pl.ds(base + u * tk, tk), :], "nt")
                if not fold:
                    z = z * scale
                if causal:
                    z = z + cc_ref[0] - cr_ref[0, j * nsub + u]
                if diagonal:
                    z = jnp.where((col + u * tk) <= row, z, NEG)
                zs.append(z)
            m_prev = m_ref[...]
            m_new = m_prev
            for z in zs:
                m_new = jnp.maximum(m_new, jnp.max(z, axis=1, keepdims=True))
            alpha = jnp.exp(m_prev - m_new)
            l_new = alpha * l_ref[...]
            out = alpha * acc_ref[...]
            for u, z in enumerate(zs):
                p = jnp.exp(z - m_new)
                l_new = l_new + jnp.sum(p, axis=1, keepdims=True)
                out = out + _dot(p, v_ref[0, pl.ds(base + u * tk, tk), :])
            l_ref[...] = l_new
            acc_ref[...] = out
            m_ref[...] = m_new

        def below(j, carry):
            step(j, False)
            return carry

        if causal:
            lax.fori_loop(0, qi, below, 0)
            step(qi, True)
        else:
            step(0, False)
        o_ref[0] = acc_ref[...] / l_ref[...]
        lse_ref[0] = m_ref[...] + jnp.log(l_ref[...])

    blk = pl.BlockSpec((1, tq, d), lambda h, i: (h, i, 0))
    full = pl.BlockSpec((1, Tk, d), lambda h, i: (h, 0, 0))
    col = pl.BlockSpec((1, tq, 1), lambda h, i: (h, i, 0))
    in_specs, args = [blk, full, full], [q, k, v]
    if causal:
        in_specs += [col, pl.BlockSpec((1, T // tk, 1, tk), lambda h, i: (h, 0, 0, 0))]
        args += [c.reshape(H, T, 1), c.reshape(H, T // tk, 1, tk)]
    return pl.pallas_call(
        body, name=name, grid=(H, T // tq),
        in_specs=in_specs, out_specs=[blk, col],
        out_shape=[jax.ShapeDtypeStruct((H, T, d), F32), jax.ShapeDtypeStruct((H, T, 1), F32)],
        scratch_shapes=[pltpu.VMEM((tq, 1), F32), pltpu.VMEM((tq, 1), F32), pltpu.VMEM((tq, d), F32)],
        compiler_params=_cparams(("parallel", "arbitrary")),
    )(*args)


def _attn_bwd(name, q, k, v, o, do, lse, scale, c=None):
    H, T, d = q.shape
    Tk = k.shape[1]
    causal = c is not None
    tq, tk, nsub = _att_tiles(T, Tk, causal)
    fold = _is_pow2(scale)

    def body(*refs):
        q_ref, k_ref, v_ref, o_ref, do_ref, lse_ref = refs[:6]
        cc_ref, cr_ref = refs[6:8] if causal else (None, None)
        n_out = 5 if causal else 3
        outs = refs[-(n_out + 1):-1]
        dq_ref, dk_ref, dv_ref = outs[:3]
        dc_ref, drow_ref = outs[3:5] if causal else (None, None)
        dq_acc = refs[-1]
        qi = pl.program_id(1)

        @pl.when(qi == 0)
        def _():
            dk_ref[...] = jnp.zeros_like(dk_ref)
            dv_ref[...] = jnp.zeros_like(dv_ref)
            if causal:
                dc_ref[...] = jnp.zeros_like(dc_ref)

        qv = q_ref[0] * scale if fold else q_ref[0]
        dov = do_ref[0]
        lse_v = lse_ref[0]
        delta = jnp.sum(dov.astype(F32) * o_ref[0], axis=1, keepdims=True)
        dq_acc[...] = jnp.zeros_like(dq_acc)
        if causal:
            drow_ref[...] = jnp.zeros_like(drow_ref)
        row = lax.broadcasted_iota(jnp.int32, (tq, tk), 0)
        col = lax.broadcasted_iota(jnp.int32, (tq, tk), 1)

        def step(j, diagonal):
            base = _key_base(j, tq)
            dq, drow = None, None
            for u in range(nsub):
                ks = base + u * tk
                kv = k_ref[0, pl.ds(ks, tk), :]
                z = _dot(qv, kv, "nt")
                if not fold:
                    z = z * scale
                if causal:
                    z = z + cc_ref[0] - cr_ref[0, j * nsub + u]
                if diagonal:
                    z = jnp.where((col + u * tk) <= row, z, NEG)
                p = jnp.exp(z - lse_v)
                ds = p * (_dot(dov, v_ref[0, pl.ds(ks, tk), :], "nt") - delta)
                term = _dot(ds, kv)
                dq = term if dq is None else dq + term
                dk = _dot(ds, qv, "tn")
                dk_ref[0, pl.ds(ks, tk), :] += dk if fold else dk * scale
                dv_ref[0, pl.ds(ks, tk), :] += _dot(p, dov, "tn")
                if causal:
                    dc_ref[0, j * nsub + u] -= jnp.sum(ds, axis=0, keepdims=True)
                    rs = jnp.sum(ds, axis=1, keepdims=True)
                    drow = rs if drow is None else drow + rs
            dq_acc[...] += dq
            if causal:
                drow_ref[0] += drow

        def below(j, carry):
            step(j, False)
            return carry

        if causal:
            lax.fori_loop(0, qi, below, 0)
            step(qi, True)
        else:
            step(0, False)
        dq_ref[0] = (dq_acc[...] * scale).astype(dq_ref.dtype)

    blk = pl.BlockSpec((1, tq, d), lambda h, i: (h, i, 0))
    full = pl.BlockSpec((1, Tk, d), lambda h, i: (h, 0, 0))
    col = pl.BlockSpec((1, tq, 1), lambda h, i: (h, i, 0))
    crow = pl.BlockSpec((1, T // tk, 1, tk), lambda h, i: (h, 0, 0, 0))
    in_specs, args = [blk, full, full, blk, blk, col], [q, k, v, o, do, lse]
    out_specs = [blk, full, full]
    out_shape = [jax.ShapeDtypeStruct((H, T, d), BF16), jax.ShapeDtypeStruct((H, Tk, d), F32),
                 jax.ShapeDtypeStruct((H, Tk, d), F32)]
    if causal:
        in_specs += [col, crow]
        args += [c.reshape(H, T, 1), c.reshape(H, T // tk, 1, tk)]
        out_specs += [crow, col]
        out_shape += [jax.ShapeDtypeStruct((H, T // tk, 1, tk), F32), jax.ShapeDtypeStruct((H, T, 1), F32)]
    outs = pl.pallas_call(
        body, name=name, grid=(H, T // tq),
        in_specs=in_specs, out_specs=out_specs, out_shape=out_shape,
        scratch_shapes=[pltpu.VMEM((tq, d), F32)],
        compiler_params=_cparams(("parallel", "arbitrary")),
    )(*args)
    if causal:
        return outs[0], outs[1], outs[2], outs[3].reshape(H, T), outs[4].reshape(H, T)
    return outs


def _decay_fwd(fl, b):
    H, T = fl.shape
    tk = DECAY_TK

    def body(x_ref, b_ref, c_ref):
        upto = _tri(tk, lambda j, s: j <= s)
        carry = jnp.zeros((H, 1), F32)
        for i in range(T // tk):
            xv = x_ref[:, i * tk:(i + 1) * tk] + b_ref[...]
            lf = jnp.minimum(xv, 0.0) - jnp.log(1.0 + jnp.exp(-jnp.abs(xv)))
            pref = _dot_split(lf, upto, parts=3) + carry
            c_ref[:, i * tk:(i + 1) * tk] = pref
            carry = pref[:, tk - 1:tk]

    vm = pl.BlockSpec(memory_space=pltpu.VMEM)
    return pl.pallas_call(
        body, name="decay_fwd", in_specs=[vm, vm], out_specs=vm,
        out_shape=jax.ShapeDtypeStruct((H, T), F32),
    )(fl, b)


def _decay_bwd(dc_cols, dc_rows, fl, b):
    H, T = fl.shape
    tk = DECAY_TK

    def body(dc_ref, dr_ref, x_ref, b_ref, dx_ref, db_ref):
        from_ = _tri(tk, lambda j, s: j >= s)
        carry = jnp.zeros((H, 1), F32)
        total = jnp.zeros((H, 1), F32)
        for i in reversed(range(T // tk)):
            sl = slice(i * tk, (i + 1) * tk)
            suffix = _dot_split(dc_ref[:, sl] + dr_ref[:, sl], from_, parts=3) + carry
            xv = x_ref[:, sl] + b_ref[...]
            dx = suffix / (1.0 + jnp.exp(xv))
            dx_ref[:, sl] = dx
            total = total + jnp.sum(dx, axis=1, keepdims=True)
            carry = suffix[:, 0:1]
        db_ref[...] = jnp.broadcast_to(total, db_ref.shape)

    vm = pl.BlockSpec(memory_space=pltpu.VMEM)
    dx, db = pl.pallas_call(
        body, name="decay_bwd", in_specs=[vm, vm, vm, vm], out_specs=[vm, vm],
        out_shape=[jax.ShapeDtypeStruct((H, T), F32), jax.ShapeDtypeStruct((H, LANE), F32)],
    )(dc_cols, dc_rows, fl, b)
    return dx, db[:, 0]


def _place():
    x, y, c = lax.axis_index("x"), lax.axis_index("y"), lax.axis_index("c")
    return x, y, c, [(1 - x, y), (x, 1 - y), (1 - x, 1 - y)]


def _all_gather(name, block):
    R, C = block.shape

    def body(x_ref, out_ref, send_sems, recv_sems, local_sem):
        x, y, c, chips = _place()
        me, sibling = (x, y, c), (x, y, 1 - c)

        def rows(px, py, pc):
            return out_ref.at[4 * px + 2 * py + pc]

        def copy(k, blk, to, src=None):
            return pltpu.make_async_remote_copy(
                src_ref=rows(*blk) if src is None else src, dst_ref=rows(*blk),
                send_sem=send_sems.at[k], recv_sem=recv_sems.at[k], device_id=to, device_id_type=MESH)

        mine = pltpu.make_async_copy(x_ref, rows(*me), local_sem)
        mine.start()
        first = [copy(0, me, sibling, src=x_ref)]
        first += [copy(1 + j, me, (*chip, c), src=x_ref) for j, chip in enumerate(chips)]
        for cp in first:
            cp.start()
        passed = [copy(4 + j, (*chip, c), sibling) for j, chip in enumerate(chips)]
        for j, chip in enumerate(chips):
            copy(1 + j, (*chip, c), me).wait_recv()
            passed[j].start()
        copy(0, sibling, me).wait_recv()
        for j, chip in enumerate(chips):
            copy(4 + j, (*chip, 1 - c), me).wait_recv()
        for cp in first + passed:
            cp.wait_send()
        mine.wait()

    return pl.pallas_call(
        body, name=name, in_specs=[ANY], out_specs=ANY,
        out_shape=jax.ShapeDtypeStruct((N_DEV, R, C), block.dtype),
        scratch_shapes=[pltpu.SemaphoreType.DMA((7,)), pltpu.SemaphoreType.DMA((7,)), pltpu.SemaphoreType.DMA(())],
    )(block)


def _swap_with_sibling(name, parts):
    _, R, C = parts.shape

    def body(p_ref, out_ref, send_sems, recv_sems):
        x, y, c, _ = _place()
        copies = [pltpu.make_async_remote_copy(
            src_ref=p_ref.at[2 * q + (1 - c)], dst_ref=out_ref.at[q],
            send_sem=send_sems.at[q], recv_sem=recv_sems.at[q], device_id=(x, y, 1 - c), device_id_type=MESH)
            for q in range(4)]
        for cp in copies:
            cp.start()
        for cp in copies:
            cp.wait_recv()
        for cp in copies:
            cp.wait_send()

    return pl.pallas_call(
        body, name=name, in_specs=[ANY], out_specs=ANY,
        out_shape=jax.ShapeDtypeStruct((4, R, C), parts.dtype),
        scratch_shapes=[pltpu.SemaphoreType.DMA((4,)), pltpu.SemaphoreType.DMA((4,))],
    )(parts)


def _add_own(name, parts, got, tr=512):
    _, R, C = parts.shape
    tr = _tile(R, tr, SUBLANE_BF16)

    def body(c_ref, p_ref, g_ref, o_ref):
        o_ref[...] = (p_ref[...].astype(F32) + g_ref[...].astype(F32)).astype(o_ref.dtype)

    return pl.pallas_call(
        body, name=name,
        grid_spec=pltpu.PrefetchScalarGridSpec(
            num_scalar_prefetch=1, grid=(4, R // tr),
            in_specs=[pl.BlockSpec((1, tr, C), lambda q, i, c: (2 * q + c[0], i, 0)),
                      pl.BlockSpec((1, tr, C), lambda q, i, c: (q, i, 0))],
            out_specs=pl.BlockSpec((1, tr, C), lambda q, i, c: (q, i, 0))),
        out_shape=jax.ShapeDtypeStruct((4, R, C), parts.dtype),
        compiler_params=_cparams(("parallel", "parallel")),
    )(lax.axis_index("c").astype(jnp.int32).reshape(1), parts, got)


def _swap_with_chips(name, parts):
    _, R, C = parts.shape

    def body(p_ref, out_ref, send_sems, recv_sems, local_sem):
        x, y, c, chips = _place()
        my_chip = 2 * x + y
        mine = pltpu.make_async_copy(p_ref.at[my_chip], out_ref.at[my_chip], local_sem)
        mine.start()
        sends = [pltpu.make_async_remote_copy(
            src_ref=p_ref.at[2 * cx + cy], dst_ref=out_ref.at[my_chip],
            send_sem=send_sems.at[j], recv_sem=recv_sems.at[j], device_id=(cx, cy, c), device_id_type=MESH)
            for j, (cx, cy) in enumerate(chips)]
        for cp in sends:
            cp.start()
        for j, (cx, cy) in enumerate(chips):
            pltpu.make_async_remote_copy(
                src_ref=p_ref.at[my_chip], dst_ref=out_ref.at[2 * cx + cy],
                send_sem=send_sems.at[j], recv_sem=recv_sems.at[j], device_id=(cx, cy, c), device_id_type=MESH,
            ).wait_recv()
        for cp in sends:
            cp.wait_send()
        mine.wait()

    return pl.pallas_call(
        body, name=name, in_specs=[ANY], out_specs=ANY,
        out_shape=jax.ShapeDtypeStruct((4, R, C), parts.dtype),
        scratch_shapes=[pltpu.SemaphoreType.DMA((3,)), pltpu.SemaphoreType.DMA((3,)), pltpu.SemaphoreType.DMA(())],
    )(parts)


def _sum_parts(name, parts, tr=512):
    P, R, C = parts.shape
    tr = _tile(R, tr, SUBLANE_BF16)

    def body(p_ref, o_ref):
        total = p_ref[0].astype(F32)
        for p in range(1, P):
            total = total + p_ref[p].astype(F32)
        o_ref[...] = total

    return pl.pallas_call(
        body, name=name, grid=(R // tr,),
        in_specs=[pl.BlockSpec((P, tr, C), lambda i: (0, i, 0))], out_specs=pl.BlockSpec((tr, C), lambda i: (i, 0)),
        out_shape=jax.ShapeDtypeStruct((R, C), F32),
        compiler_params=_cparams(("parallel",)),
    )(parts)


_HBM = pl.BlockSpec(memory_space=pltpu.HBM)
_SEM = pl.BlockSpec(memory_space=pltpu.SEMAPHORE)
_EFFECT = pltpu.SideEffectType.DATAFLOW_SIDE_EFFECTING


def _flipped(x, y, c, k):
    px, py, pc = (1 - x if k & 4 else x), (1 - y if k & 2 else y), (1 - c if k & 1 else c)
    return (px, py, pc), 4 * px + 2 * py + pc


def _exchange_start(name, src, per_peer):
    R, C = src.shape[-2:]

    def body(v_ref, land_ref, send_sem, recv_sem, v_thru, land_thru, token):
        x, y, c = lax.axis_index("x"), lax.axis_index("y"), lax.axis_index("c")
        me = 4 * x + 2 * y + c
        for k in range(1, N_DEV):
            peer, idx = _flipped(x, y, c, k)
            pltpu.make_async_remote_copy(
                src_ref=v_ref.at[idx] if per_peer else v_ref, dst_ref=land_ref.at[me],
                send_sem=send_sem, recv_sem=recv_sem, device_id=peer, device_id_type=MESH).start()
        token[...] = jnp.zeros_like(token)

    return pl.pallas_call(
        body, name=name,
        out_shape=(pltpu.SemaphoreType.DMA(()), pltpu.SemaphoreType.DMA(()), pltpu.HBM(src.shape, src.dtype),
                   pltpu.HBM((N_DEV, R, C), src.dtype), jax.ShapeDtypeStruct((8, LANE), F32)),
        in_specs=(_HBM, _HBM), out_specs=(_SEM, _SEM, _HBM, _HBM, pl.BlockSpec(memory_space=pltpu.VMEM)),
        input_output_aliases={0: 2, 1: 3},
        compiler_params=pltpu.CompilerParams(has_side_effects=_EFFECT),
    )(pltpu.with_memory_space_constraint(src, pltpu.HBM),
      pltpu.with_memory_space_constraint(lax.empty((N_DEV, R, C), src.dtype), pltpu.HBM))


def _exchange_wait(name, started, after):
    send_sem, recv_sem, v_thru, land_thru, _ = started

    def body(v_ref, land_ref, send_sem, recv_sem, after_ref, v_dead, got_ref):
        x, y, c = lax.axis_index("x"), lax.axis_index("y"), lax.axis_index("c")
        seven = land_ref.at[pl.ds(0, N_DEV - 1)]
        drain = pltpu.make_async_remote_copy(
            src_ref=seven, dst_ref=seven, send_sem=send_sem, recv_sem=recv_sem,
            device_id=(x, y, c), device_id_type=MESH)
        drain.wait_send()
        drain.wait_recv()

    return pl.pallas_call(
        body, name=name,
        out_shape=(pltpu.HBM(v_thru.shape, v_thru.dtype), pltpu.HBM(land_thru.shape, land_thru.dtype)),
        in_specs=(_HBM, _HBM, _SEM, _SEM, ANY), out_specs=(_HBM, _HBM), input_output_aliases={0: 0, 1: 1},
        compiler_params=pltpu.CompilerParams(has_side_effects=_EFFECT),
    )(v_thru, land_thru, send_sem, recv_sem, after)[1]


def _with_own(landed, own):
    me = 4 * lax.axis_index("x") + 2 * lax.axis_index("y") + lax.axis_index("c")
    return lax.dynamic_update_slice(landed, own[None], (me, 0, 0))


def _after(params, name, token):
    return {**params, name: params[name] + token[0, 0]}


def _reduce_scatter(tag, parts):
    got = _swap_with_sibling("rs_pair_" + tag, parts)
    pair = _add_own("rs_add_" + tag, parts, got)
    quad = _swap_with_chips("rs_chips_" + tag, pair)
    return _sum_parts("rs_sum_" + tag, quad)


def _adamw(name, g_parts, w, m, v, tr=512):
    P, R, C = g_parts.shape
    tr = _tile(R, tr, 8)

    def body(g_ref, w_ref, m_ref, v_ref, go_ref, d_ref, mo_ref, vo_ref):
        g = g_ref[0]
        for p in range(1, P):
            g = g + g_ref[p]
        mn = ADAM_B1 * m_ref[...] + (1.0 - ADAM_B1) * g
        vn = ADAM_B2 * v_ref[...] + (1.0 - ADAM_B2) * (g * g)
        m_hat = mn / (1.0 - ADAM_B1 ** ADAM_STEP)
        v_hat = vn / (1.0 - ADAM_B2 ** ADAM_STEP)
        go_ref[...] = g
        d_ref[...] = -ADAM_LR * (m_hat / (jnp.sqrt(v_hat) + ADAM_EPS) + ADAM_WD * w_ref[...])
        mo_ref[...] = mn
        vo_ref[...] = vn

    row = pl.BlockSpec((tr, C), lambda i: (i, 0))
    return pl.pallas_call(
        body, name=name, grid=(R // tr,),
        in_specs=[pl.BlockSpec((P, tr, C), lambda i: (0, i, 0)), row, row, row], out_specs=[row] * 4,
        out_shape=[jax.ShapeDtypeStruct((R, C), F32)] * 4,
        compiler_params=_cparams(("parallel",)),
    )(g_parts, w, m, v)


def _to_heads(t, n_heads):
    rows = t.shape[0]
    return t.reshape(rows, n_heads, -1).transpose(1, 0, 2)


def _from_heads(t):
    return t.transpose(1, 0, 2).reshape(t.shape[1], -1)


def _pad_rows(t, rows):
    return jnp.pad(t, ((0, rows - t.shape[0]), (0, 0)))


class _Layout:
    def __init__(self, D, ff_shard, in_shard, kv_shard, gate_shard, br_in, br_shard, out_shard):
        self.D = D
        self.in_shard = in_shard
        self.in_pad = -(-in_shard // LANE) * LANE
        self.br_in, self.br_shard = br_in, br_shard
        br_rows = br_shard * br_in // D
        sizes = [("g1", ff_shard), ("u1", ff_shard), ("d1", ff_shard), ("win", self.in_pad), ("kv", kv_shard),
                 ("gate", gate_shard), ("br", br_rows), ("out", out_shard),
                 ("g2", ff_shard), ("u2", ff_shard), ("d2", ff_shard)]
        self.seg, off = {}, 0
        for key, n in sizes:
            assert n % SUBLANE_BF16 == 0, (key, n)
            self.seg[key] = (off, n)
            off += n
        self.rows = off

    def pack(self, parts):
        return jnp.concatenate([parts[key] for key in self.seg], axis=0)

    def take(self, gathered, key):
        off, n = self.seg[key]
        return gathered[:, off:off + n, :].reshape(N_DEV * n, self.D)

    def spread(self, full, key):
        _, n = self.seg[key]
        return full.reshape(N_DEV, n, self.D)


def _pack_layer(lay, l, p):
    D = lay.D
    br = jnp.concatenate([p["w_br_sb"][l], p["w_br_fox"][l], p["w_br_mem"][l]], axis=0)
    parts = {
        "g1": p["ffn1_w_gate"][l].T, "u1": p["ffn1_w_up"][l].T, "d1": p["ffn1_w_down"][l],
        "win": _pad_rows(p["w_in"][l].T, lay.in_pad), "kv": p["w_mem_kv"][l], "gate": p["w_gate"][l].T,
        "br": br.T.reshape(-1, D), "out": p["w_out"][l],
        "g2": p["ffn2_w_gate"][l].T, "u2": p["ffn2_w_up"][l].T, "d2": p["ffn2_w_down"][l],
    }
    return lay.pack({k: t.astype(BF16) for k, t in parts.items()})


def _unpack_layer(lay, gathered):
    D = lay.D
    w = {k: lay.take(gathered, k) for k in ("g1", "u1", "d1", "win", "kv", "out", "g2", "u2", "d2")}
    gate = lay.take(gathered, "gate")
    w["gate"] = gate
    w["gate3"] = [gate[i * D:(i + 1) * D] for i in range(3)]
    br = lay.take(gathered, "br").reshape(N_DEV * lay.br_shard, lay.br_in)
    third = lay.br_in // 3
    w["br3"] = [br[:, i * third:(i + 1) * third] for i in range(3)]
    return w


def _silu_mul(accs, _):
    a, b = accs
    return [a, b, a * jax.nn.sigmoid(a) * b]


def _act_bwd(accs, extras):
    ds, (a, b) = accs[0], extras
    sig = jax.nn.sigmoid(a)
    return [ds * b * (sig * (1.0 + a * (1.0 - sig))), ds * (a * sig)]


def _ffn_fwd(tag, h, pre_g, post_g, wg, wu, wd):
    n = _rms_fwd("ffn_norm_" + tag, h, pre_g, BF16)
    a, b, s = _mm("ffn_up_" + tag, [(n, wg), (n, wu)], "nt", [F32, F32, BF16], _silu_mul, tm=256, tn=1408)
    f = _mm("ffn_down_" + tag, [(s, wd)], "nn", [F32], tm=256)
    out = _rms_fwd("ffn_out_" + tag, f, post_g, F32, res=h, scale=0.5)
    return out, (h, n, a, b, s, f)


def _ffn_bwd(tag, dh, saved, pre_g, post_g, wg, wu, wd):
    h, n, a, b, s, f = saved
    df, d_post = _rms_bwd("ffn_dout_" + tag, f, post_g, dh, BF16, scale=0.5)
    da, db = _mm("ffn_dact_" + tag, [(df, wd)], "nt", [BF16, BF16], _act_bwd, [(a, 0), (b, 0)], tm=256, tn=1408)
    d_wd = _mm("ffn_dwd_" + tag, [(s, df)], "tn", [BF16], tm=256)
    dn = _mm("ffn_dn_" + tag, [(da, wg), (db, wu)], "nn", [F32], _sum_accs, tm=256)
    d_wg = _mm("ffn_dwg_" + tag, [(da, n)], "tn", [BF16], tm=256)
    d_wu = _mm("ffn_dwu_" + tag, [(db, n)], "tn", [BF16], tm=256)
    dh_in, d_pre = _rms_bwd("ffn_dnorm_" + tag, h, pre_g, dn, F32, res=dh)
    return dh_in, d_pre, d_post, d_wg, d_wu, d_wd


def _unpad_proj(lay, projp):
    T = projp.shape[0]
    return projp.reshape(T, N_DEV, lay.in_pad)[:, :, :lay.in_shard].reshape(T, N_DEV * lay.in_shard)


def _pad_proj(lay, proj):
    T = proj.shape[0]
    t = proj.reshape(T, N_DEV, lay.in_shard)
    return jnp.pad(t, ((0, 0), (0, 0), (0, lay.in_pad - lay.in_shard))).reshape(T, N_DEV * lay.in_pad)


_SB_W = N_SB_HEADS * HEAD_DIM
_FOX_W = N_FOX_HEADS * HEAD_DIM
_SPLITS = [_SB_W, 2 * _SB_W, 3 * _SB_W, 3 * _SB_W + _FOX_W, 3 * _SB_W + 2 * _FOX_W, 3 * _SB_W + 3 * _FOX_W,
           3 * _SB_W + 3 * _FOX_W + N_FOX_HEADS]


def _gate_act(accs, extras):
    return [jax.nn.sigmoid(accs[0] + extras[0])]


def _merge(accs, extras):
    return [extras[0] * accs[0] + extras[1] * accs[1] + extras[2] * accs[2]]


def _merge_bwd(accs, extras):
    dm = accs[0]
    d_branch = [dm * gi for gi in extras]
    d_gate = [dm * bi * gi * (1.0 - gi) for bi, gi in zip(accs[1:], extras)]
    return d_branch + d_gate


def _mix_fwd(lay, h, w, pre_g, post_g, b_forget, b_gate, mem_n):
    D = lay.D
    u = _rms_fwd("mix_norm", h, pre_g, BF16)
    projp = _mm("mix_in", [(u, w["win"])], "nt", [F32])
    q_sb, k_sb, v_sb, q_fx, k_fx, v_fx, f_logit, q_mem = jnp.split(_unpad_proj(lay, projp), _SPLITS, axis=1)
    sb = [_to_heads(t.astype(BF16), N_SB_HEADS) for t in (q_sb, k_sb, v_sb)]
    fx = [_to_heads(t.astype(BF16), N_FOX_HEADS) for t in (q_fx, k_fx, v_fx)]
    qm = _to_heads(q_mem.astype(BF16), N_MEM_HEADS)
    fl = f_logit.T
    c = _decay_fwd(fl, b_forget.reshape(-1, 1))
    o_sb, rtot = _sb_fwd(*sb, HEAD_DIM ** -0.5)
    o_fx, lse_fx = _attn_fwd("fox_fwd", *fx, HEAD_DIM ** -0.5, c)
    kvm = _mm("mem_kv", [(mem_n, w["kv"])], "nn", [BF16])
    half = kvm.shape[1] // 2
    km, vm = _to_heads(kvm[:, :half], N_MEM_HEADS), _to_heads(kvm[:, half:], N_MEM_HEADS)
    o_mem, lse_mem = _attn_fwd("mem_fwd", qm, km, vm, qm.shape[-1] ** -0.5)
    gates = _mm("mix_gate", [(u, w["gate"])], "nt", [F32], _gate_act, [(b_gate.reshape(1, -1), 0)])
    flat = [_from_heads(o).astype(BF16) for o in (o_sb, o_fx, o_mem)]
    merged = _mm("mix_merge", list(zip(flat, w["br3"])), "nt", [BF16], _merge,
                 [(gates, 0), (gates, D), (gates, 2 * D)])
    z = _mm("mix_out", [(merged, w["out"])], "nn", [F32])
    out = _rms_fwd("mix_res", z, post_g, F32, res=h)
    saved = (h, u, sb, fx, qm, fl, c, o_sb, rtot, o_fx, lse_fx, km, vm, o_mem, lse_mem, gates, flat, merged, z)
    return out, saved


def _mix_bwd(lay, dh, saved, w, pre_g, post_g, b_forget, mem_n, dmem_n):
    D = lay.D
    h, u, sb, fx, qm, fl, c, o_sb, rtot, o_fx, lse_fx, km, vm, o_mem, lse_mem, gates, flat, merged, z = saved
    dz, d_post = _rms_bwd("mix_dres", z, post_g, dh, BF16)
    outs = _mm("mix_dmerge", [(dz, w["out"])] + list(zip(flat, w["br3"])), "nt", [BF16] * 6, _merge_bwd,
               [(gates, 0), (gates, D), (gates, 2 * D)])
    d_branch, d_gate = outs[:3], outs[3:]
    d_wout = _mm("mix_dwout", [(merged, dz)], "tn", [BF16])
    d_o = [_mm("mix_dbr%d" % i, [(d_branch[i], w["br3"][i])], "nn", [BF16]) for i in range(3)]
    d_wbr = [_mm("mix_dwbr%d" % i, [(d_branch[i], flat[i])], "tn", [BF16]) for i in range(3)]
    d_bgate = jnp.concatenate([_colsum("mix_dbgate%d" % i, d_gate[i]) for i in range(3)])
    d_wgate = [_mm("mix_dwgate%d" % i, [(d_gate[i], u)], "tn", [BF16]) for i in range(3)]

    dq_s, dk_s, dv_s = _sb_bwd(*sb, _to_heads(d_o[0], N_SB_HEADS), rtot, HEAD_DIM ** -0.5)
    dq_f, dk_f, dv_f, dc, dc_rows = _attn_bwd("fox_bwd", *fx, o_fx, _to_heads(d_o[1], N_FOX_HEADS), lse_fx,
                                     HEAD_DIM ** -0.5, c)
    dq_m, dk_m, dv_m = _attn_bwd("mem_bwd", qm, km, vm, o_mem, _to_heads(d_o[2], N_MEM_HEADS), lse_mem,
                                 qm.shape[-1] ** -0.5)
    dfl, d_bforget = _decay_bwd(dc, dc_rows, fl, b_forget.reshape(-1, 1))
    dproj = jnp.concatenate(
        [_from_heads(t).astype(BF16) for t in (dq_s, dk_s, dv_s, dq_f, dk_f, dv_f)]
        + [dfl.T.astype(BF16), _from_heads(dq_m).astype(BF16)], axis=1)
    dprojp = _pad_proj(lay, dproj)
    du = _mm("mix_du", list(zip(d_gate, w["gate3"])) + [(dprojp, w["win"])], "nn", [F32], _sum_accs, tm=256)
    d_win = _mm("mix_dwin", [(dprojp, u)], "tn", [BF16])
    dh_in, d_pre = _rms_bwd("mix_dnorm", h, pre_g, du, F32, res=dh)

    dkvm = jnp.concatenate([_from_heads(dk_m), _from_heads(dv_m)], axis=1).astype(BF16)
    d_wkv = _mm("mem_dwkv", [(mem_n, dkvm)], "tn", [BF16])
    dmem_n = _mm("mem_dn", [(dkvm, w["kv"])], "nt", [F32], lambda accs, ex: [accs[0] + ex[0]], [(dmem_n, 0)])
    grads = {"win": d_win, "kv": d_wkv, "gate": jnp.concatenate(d_wgate, axis=0),
             "br": jnp.concatenate(d_wbr, axis=1), "out": d_wout}
    return dh_in, d_pre, d_post, d_bforget, d_bgate, grads, dmem_n


def _layer_fwd(lay, h, w, sp, mem_n):
    h1, s1 = _ffn_fwd("1", h, sp["ffn1_pre_g"], sp["ffn1_post_g"], w["g1"], w["u1"], w["d1"])
    h2, s2 = _mix_fwd(lay, h1, w, sp["mix_pre_g"], sp["mix_post_g"], sp["b_forget"], sp["b_gate"], mem_n)
    h3, s3 = _ffn_fwd("2", h2, sp["ffn2_pre_g"], sp["ffn2_post_g"], w["g2"], w["u2"], w["d2"])
    return h3, (s1, s2, s3)


def _layer_bwd(lay, dh, saved, w, sp, mem_n, dmem_n):
    s1, s2, s3 = saved
    dh, d_pre2, d_post2, d_g2, d_u2, d_d2 = _ffn_bwd("2", dh, s3, sp["ffn2_pre_g"], sp["ffn2_post_g"],
                                                     w["g2"], w["u2"], w["d2"])
    dh, d_mpre, d_mpost, d_bforget, d_bgate, g, dmem_n = _mix_bwd(
        lay, dh, s2, w, sp["mix_pre_g"], sp["mix_post_g"], sp["b_forget"], mem_n, dmem_n)
    dh, d_pre1, d_post1, d_g1, d_u1, d_d1 = _ffn_bwd("1", dh, s1, sp["ffn1_pre_g"], sp["ffn1_post_g"],
                                                     w["g1"], w["u1"], w["d1"])
    g.update({"g1": d_g1, "u1": d_u1, "d1": d_d1, "g2": d_g2, "u2": d_u2, "d2": d_d2})
    g["br"] = g["br"].reshape(N_DEV, lay.br_shard, lay.br_in).reshape(-1, lay.D)
    packed = jnp.concatenate([lay.spread(g[key], key) for key in lay.seg], axis=1)
    small = {"ffn1_pre_g": d_pre1, "ffn1_post_g": d_post1, "mix_pre_g": d_mpre, "mix_post_g": d_mpost,
             "ffn2_pre_g": d_pre2, "ffn2_post_g": d_post2, "b_gate": d_bgate, "b_forget": d_bforget}
    return dh, packed, small, dmem_n


_SHARDED = ["ffn1_w_gate", "ffn1_w_up", "ffn1_w_down", "w_in", "w_mem_kv", "w_gate", "w_br_sb", "w_br_fox",
            "w_br_mem", "w_out", "ffn2_w_gate", "ffn2_w_up", "ffn2_w_down"]
_SMALL_LAYER = ["ffn1_pre_g", "ffn1_post_g", "mix_pre_g", "mix_post_g", "ffn2_pre_g", "ffn2_post_g", "b_gate",
                "b_forget"]
_WEIGHTS = ["ffn1_pre_g", "ffn1_post_g", "ffn1_w_gate", "ffn1_w_up", "ffn1_w_down", "mix_pre_g", "mix_post_g",
            "w_in", "b_forget", "mem_norm_g", "w_mem_kv", "w_gate", "b_gate", "w_br_sb", "w_br_fox", "w_br_mem",
            "w_out", "ffn2_pre_g", "ffn2_post_g", "ffn2_w_gate", "ffn2_w_up", "ffn2_w_down"]


def _pack_small(vals, L, D):
    rows = []
    for l in range(L):
        for name in _SMALL_LAYER:
            t = vals[name][l]
            rows.append(jnp.pad(t, (0, -t.shape[0] % D)).reshape(-1, D))
    rows.append(vals["mem_norm_g"].reshape(1, D))
    packed = jnp.concatenate(rows, axis=0)
    return _pad_rows(packed, -(-packed.shape[0] // 8) * 8)


def _unpack_small(packed, shapes, L, D):
    out = {name: [] for name in _SMALL_LAYER}
    r = 0
    for l in range(L):
        for name in _SMALL_LAYER:
            n = shapes[name][1]
            nr = -(-n // D)
            out[name].append(packed[r:r + nr].reshape(-1)[:n])
            r += nr
    res = {name: jnp.stack(v) for name, v in out.items()}
    res["mem_norm_g"] = packed[r]
    return res


def _unpack_grads(lay, g, l_shapes):
    def seg(key):
        off, n = lay.seg[key]
        return g[off:off + n]
    br = seg("br").reshape(lay.br_shard, lay.br_in).T
    third = lay.br_in // 3
    return {
        "ffn1_w_gate": seg("g1").T, "ffn1_w_up": seg("u1").T, "ffn1_w_down": seg("d1"),
        "w_in": seg("win")[:lay.in_shard].T, "w_mem_kv": seg("kv"), "w_gate": seg("gate").T,
        "w_br_sb": br[:third], "w_br_fox": br[third:2 * third], "w_br_mem": br[2 * third:],
        "w_out": seg("out"), "ffn2_w_gate": seg("g2").T, "ffn2_w_up": seg("u2").T, "ffn2_w_down": seg("d2"),
    }


class _Exchanges:
    def gather(self, name, block):
        return _all_gather(name, block)

    def gather_start(self, block):
        return _exchange_start("ag_start", block, per_peer=False)

    def gather_wait(self, started, after, block):
        return _with_own(_exchange_wait("ag_wait", started, after), block)

    def scatter(self, parts):
        return _reduce_scatter("w", parts)

    def scatter_start(self, parts):
        return _exchange_start("rs_start", parts, per_peer=True)

    def scatter_wait(self, started, after, parts):
        me = 4 * lax.axis_index("x") + 2 * lax.axis_index("y") + lax.axis_index("c")
        own = lax.dynamic_index_in_dim(parts, me, 0, keepdims=False)
        return _sum_parts("rs_sum8", _with_own(_exchange_wait("rs_wait", started, after), own))

    def token(self, started):
        return started[4]

    def loss_sum(self, part):
        return lax.psum(part, ("x", "y", "c"))


def _step(p, m, v, x, mem, tgt, ex):
    L, D = p["ffn1_pre_g"].shape
    lay = _Layout(D, p["ffn1_w_gate"].shape[2], p["w_in"].shape[2], p["w_mem_kv"].shape[1], p["w_gate"].shape[2],
                  3 * p["w_br_sb"].shape[1], p["w_br_sb"].shape[2], p["w_out"].shape[1])
    blocks = [_pack_layer(lay, l, p) for l in range(L)]
    sps = [{name: p[name][l] for name in _SMALL_LAYER} for l in range(L)]

    mem_n = _rms_fwd("mem_norm", mem, p["mem_norm_g"], BF16)
    gathered = ex.gather("ag_weights", blocks[0])
    h, saved, ws = x, [], []
    for l in range(L):
        if l + 1 < L:
            nxt, gathered = lax.optimization_barrier((blocks[l + 1], gathered))
            started = ex.gather_start(nxt)
            sp = _after(sps[l], "ffn1_pre_g", ex.token(started))
        else:
            sp = sps[l]
        ws.append(_unpack_layer(lay, gathered))
        h, s = _layer_fwd(lay, h, ws[l], sp, mem_n)
        saved.append(s)
        if l + 1 < L:
            gathered = ex.gather_wait(started, h, blocks[l + 1])
    loss_part, dh = _loss_grad(h, tgt)
    loss = ex.loss_sum(loss_part)

    dmem_n = jnp.zeros(mem.shape, F32)
    big, small = [None] * L, {name: [None] * L for name in _SMALL_LAYER}
    flying, token = {}, None
    for l in reversed(range(L)):
        sp = sps[l] if token is None else _after(sps[l], "ffn2_post_g", token)
        dh, packed, sm, dmem_n = _layer_bwd(lay, dh, saved[l], ws[l], sp, mem_n, dmem_n)
        if l > 0:
            flying[l] = (ex.scatter_start(packed), packed)
            token = ex.token(flying[l][0])
        else:
            big[l] = _unpack_grads(lay, ex.scatter(packed), None)
        for name in _SMALL_LAYER:
            small[name][l] = sm[name]
    for l, (started, packed) in flying.items():
        big[l] = _unpack_grads(lay, ex.scatter_wait(started, dh, packed), None)
    _, d_memg = _rms_bwd("mem_dnorm", mem, p["mem_norm_g"], dmem_n, F32)

    small_g = {name: jnp.stack(vs) for name, vs in small.items()}
    small_g["mem_norm_g"] = d_memg
    small_names = _SMALL_LAYER + ["mem_norm_g"]
    shapes = {name: p[name].shape for name in small_names}
    g_all = ex.gather("ag_small", _pack_small(small_g, L, D))
    packs = [_pack_small({name: t[name] for name in small_names}, L, D) for t in (p, m, v)]
    res = [_unpack_small(t, shapes, L, D) for t in _adamw("adamw_small", g_all, *packs)]

    out = {kind: {} for kind in ("grad", "delta", "new_m", "new_v")}
    for name in small_names:
        for kind, r in zip(("grad", "delta", "new_m", "new_v"), res):
            out[kind][name] = r[name].reshape(p[name].shape)
    for name in _SHARDED:
        g = jnp.stack([big[l][name] for l in range(L)])
        shp = g.shape
        flat = lambda t: t.reshape(-1, shp[-1])
        r = _adamw("adamw_" + name, flat(g)[None], flat(p[name]), flat(m[name]), flat(v[name]))
        for kind, t in zip(("grad", "delta", "new_m", "new_v"), r):
            out[kind][name] = t.reshape(shp)
    return loss, dh, out


def kernel(x, mem, ffn1_pre_g, ffn1_post_g, ffn1_w_gate, ffn1_w_up, ffn1_w_down, mix_pre_g, mix_post_g, w_in, b_forget, mem_norm_g, w_mem_kv, w_gate, b_gate, w_br_sb, w_br_fox, w_br_mem, w_out, ffn2_pre_g, ffn2_post_g, ffn2_w_gate, ffn2_w_up, ffn2_w_down, loss_target, m_ffn1_pre_g, m_ffn1_post_g, m_ffn1_w_gate, m_ffn1_w_up, m_ffn1_w_down, m_mix_pre_g, m_mix_post_g, m_w_in, m_b_forget, m_mem_norm_g, m_w_mem_kv, m_w_gate, m_b_gate, m_w_br_sb, m_w_br_fox, m_w_br_mem, m_w_out, m_ffn2_pre_g, m_ffn2_post_g, m_ffn2_w_gate, m_ffn2_w_up, m_ffn2_w_down, v_ffn1_pre_g, v_ffn1_post_g, v_ffn1_w_gate, v_ffn1_w_up, v_ffn1_w_down, v_mix_pre_g, v_mix_post_g, v_w_in, v_b_forget, v_mem_norm_g, v_w_mem_kv, v_w_gate, v_b_gate, v_w_br_sb, v_w_br_fox, v_w_br_mem, v_w_out, v_ffn2_pre_g, v_ffn2_post_g, v_ffn2_w_gate, v_ffn2_w_up, v_ffn2_w_down):
    p = dict(zip(_WEIGHTS, (ffn1_pre_g, ffn1_post_g, ffn1_w_gate, ffn1_w_up, ffn1_w_down, mix_pre_g, mix_post_g, w_in, b_forget, mem_norm_g, w_mem_kv, w_gate, b_gate, w_br_sb, w_br_fox, w_br_mem, w_out, ffn2_pre_g, ffn2_post_g, ffn2_w_gate, ffn2_w_up, ffn2_w_down)))
    m = dict(zip(_WEIGHTS, (m_ffn1_pre_g, m_ffn1_post_g, m_ffn1_w_gate, m_ffn1_w_up, m_ffn1_w_down, m_mix_pre_g, m_mix_post_g, m_w_in, m_b_forget, m_mem_norm_g, m_w_mem_kv, m_w_gate, m_b_gate, m_w_br_sb, m_w_br_fox, m_w_br_mem, m_w_out, m_ffn2_pre_g, m_ffn2_post_g, m_ffn2_w_gate, m_ffn2_w_up, m_ffn2_w_down)))
    v = dict(zip(_WEIGHTS, (v_ffn1_pre_g, v_ffn1_post_g, v_ffn1_w_gate, v_ffn1_w_up, v_ffn1_w_down, v_mix_pre_g, v_mix_post_g, v_w_in, v_b_forget, v_mem_norm_g, v_w_mem_kv, v_w_gate, v_b_gate, v_w_br_sb, v_w_br_fox, v_w_br_mem, v_w_out, v_ffn2_pre_g, v_ffn2_post_g, v_ffn2_w_gate, v_ffn2_w_up, v_ffn2_w_down)))
    loss, dx, out = _step(p, m, v, x[0], mem[0], loss_target[0], _Exchanges())
    return (loss, dx[None], *[out["grad"][n] for n in _WEIGHTS], *[out["delta"][n] for n in _WEIGHTS],
            *[out["new_m"][n] for n in _WEIGHTS], *[out["new_v"][n] for n in _WEIGHTS])
```

```python
import functools
import math

import jax
import jax.numpy as jnp
from jax import lax
from jax.experimental import pallas as pl
from jax.experimental.pallas import tpu as pltpu

F32 = jnp.float32
BF16 = jnp.bfloat16

LANE = 128
SUBLANE_BF16 = 16
VMEM_LIMIT = 48 * 1024 * 1024
N_DEV = 8
MESH = pl.DeviceIdType.MESH
ANY = pl.BlockSpec(memory_space=pl.ANY)

RMS_EPS = 1e-6
HEAD_DIM = 64
N_SB_HEADS = 8
N_FOX_HEADS = 8
N_MEM_HEADS = 4
NEG = -1e30
ATT_TQ = 512
ATT_TK = 256
DECAY_TK = 128
IN_TILE = 768

ADAM_LR = 0.001
ADAM_B1 = 0.9
ADAM_B2 = 0.999
ADAM_EPS = 1e-08
ADAM_WD = 0.01
ADAM_STEP = 10


def _tile(n, target, mult=LANE):
    best = None
    for t in range(mult, min(n, target) + 1, mult):
        if n % t == 0:
            best = t
    return best if best is not None else n


def _cparams(sem):
    return pltpu.CompilerParams(dimension_semantics=sem, vmem_limit_bytes=VMEM_LIMIT)


_DIMS = {"nn": (((1,), (0,)), ((), ())), "nt": (((1,), (1,)), ((), ())), "tn": (((0,), (0,)), ((), ()))}


def _dot(a, b, mode="nn"):
    return lax.dot_general(a.astype(BF16), b.astype(BF16), _DIMS[mode], preferred_element_type=F32)


def _mm(name, pairs, mode, out_dtypes, epilogue=None, extras=(), tm=512, tn=512):
    a0, b0 = pairs[0]
    M = a0.shape[1] if mode == "tn" else a0.shape[0]
    N = b0.shape[0] if mode == "nt" else b0.shape[1]
    tm = _tile(M, tm)
    tn = _tile(N, tn)
    np_, ne, no = len(pairs), len(extras), len(out_dtypes)

    def body(*refs):
        a_refs, b_refs = refs[:np_], refs[np_:2 * np_]
        e_refs = refs[2 * np_:2 * np_ + ne]
        o_refs = refs[2 * np_ + ne:]
        accs = [_dot(a[...], b[...], mode) for a, b in zip(a_refs, b_refs)]
        outs = epilogue(accs, [e[...] for e in e_refs]) if epilogue is not None else accs
        for o, val in zip(o_refs, outs):
            o[...] = val.astype(o.dtype)

    in_specs = []
    for a, _ in pairs:
        if mode == "tn":
            in_specs.append(pl.BlockSpec((a.shape[0], tm), lambda j, i: (0, i)))
        else:
            in_specs.append(pl.BlockSpec((tm, a.shape[1]), lambda j, i: (i, 0)))
    for _, b in pairs:
        if mode == "nt":
            in_specs.append(pl.BlockSpec((tn, b.shape[1]), lambda j, i: (j, 0)))
        else:
            in_specs.append(pl.BlockSpec((b.shape[0], tn), lambda j, i: (0, j)))
    for e, off in extras:
        if e.shape[0] == 1:
            in_specs.append(pl.BlockSpec((1, tn), functools.partial(lambda j, i, o: (0, j + o), o=off // tn)))
        else:
            in_specs.append(pl.BlockSpec((tm, tn), functools.partial(lambda j, i, o: (i, j + o), o=off // tn)))
    out_specs = [pl.BlockSpec((tm, tn), lambda j, i: (i, j)) for _ in range(no)]
    outs = pl.pallas_call(
        body, name=name, grid=(N // tn, M // tm),
        in_specs=in_specs, out_specs=out_specs,
        out_shape=[jax.ShapeDtypeStruct((M, N), dt) for dt in out_dtypes],
        compiler_params=_cparams(("parallel", "parallel")),
    )(*[a for a, _ in pairs], *[b for _, b in pairs], *[e for e, _ in extras])
    return outs[0] if no == 1 else outs


def _sum_accs(accs, _):
    total = accs[0]
    for acc in accs[1:]:
        total = total + acc
    return [total]


def _rstd(x):
    return lax.rsqrt(jnp.mean(x * x, axis=-1, keepdims=True) + RMS_EPS)


def _rms_fwd(name, x, g, out_dtype, res=None, scale=1.0, tr=512):
    R, D = x.shape
    tr = _tile(R, tr, 8)
    has_res = res is not None

    def body(*refs):
        x_ref, g_ref = refs[:2]
        o_ref = refs[-1]
        xv = x_ref[...]
        y = (xv * _rstd(xv)) * g_ref[...]
        if has_res:
            y = refs[2][...] + scale * y
        o_ref[...] = y.astype(o_ref.dtype)

    row = pl.BlockSpec((tr, D), lambda i: (i, 0))
    gain = pl.BlockSpec((1, D), lambda i: (0, 0))
    return pl.pallas_call(
        body, name=name, grid=(R // tr,),
        in_specs=[row, gain] + ([row] if has_res else []), out_specs=row,
        out_shape=jax.ShapeDtypeStruct((R, D), out_dtype),
        compiler_params=_cparams(("parallel",)),
    )(x, g.reshape(1, D), *([res] if has_res else []))


def _rms_bwd(name, x, g, dy, out_dtype, scale=1.0, res=None, tr=512):
    R, D = x.shape
    tr = _tile(R, tr, 8)
    has_res = res is not None

    def body(*refs):
        x_ref, g_ref, dy_ref = refs[:3]
        dx_ref, dg_ref = refs[-2:]
        i = pl.program_id(0)
        xv = x_ref[...]
        xhat = xv * _rstd(xv)
        dyv = dy_ref[...].astype(F32) * scale
        gy = dyv * g_ref[...]
        dx = _rstd(xv) * (gy - xhat * jnp.mean(gy * xhat, axis=-1, keepdims=True))
        if has_res:
            dx = refs[3][...] + dx
        dx_ref[...] = dx.astype(dx_ref.dtype)
        part = jnp.sum(dyv * xhat, axis=0, keepdims=True)

        @pl.when(i == 0)
        def _():
            dg_ref[...] = part

        @pl.when(i > 0)
        def _():
            dg_ref[...] += part

    row = pl.BlockSpec((tr, D), lambda i: (i, 0))
    gain = pl.BlockSpec((1, D), lambda i: (0, 0))
    dx, dg = pl.pallas_call(
        body, name=name, grid=(R // tr,),
        in_specs=[row, gain, row] + ([row] if has_res else []), out_specs=[row, gain],
        out_shape=[jax.ShapeDtypeStruct((R, D), out_dtype), jax.ShapeDtypeStruct((1, D), F32)],
        compiler_params=_cparams(("arbitrary",)),
    )(x, g.reshape(1, D), dy, *([res] if has_res else []))
    return dx, dg[0]


def _loss_grad(y, tgt, tr=512):
    R, D = y.shape
    tr = _tile(R, tr, 8)

    def body(y_ref, t_ref, dy_ref, loss_ref):
        i = pl.program_id(0)
        d = y_ref[...] - t_ref[...]
        dy_ref[...] = d / D
        part = 0.5 * jnp.sum(jnp.mean(d * d, axis=-1, keepdims=True), axis=0, keepdims=True)
        tile = jnp.broadcast_to(part, loss_ref.shape)

        @pl.when(i == 0)
        def _():
            loss_ref[...] = tile

        @pl.when(i > 0)
        def _():
            loss_ref[...] += tile

    row = pl.BlockSpec((tr, D), lambda i: (i, 0))
    dy, loss = pl.pallas_call(
        body, name="loss_grad", grid=(R // tr,),
        in_specs=[row, row], out_specs=[row, pl.BlockSpec((8, LANE), lambda i: (0, 0))],
        out_shape=[jax.ShapeDtypeStruct((R, D), F32), jax.ShapeDtypeStruct((8, LANE), F32)],
        compiler_params=_cparams(("arbitrary",)),
    )(y, tgt)
    return loss[0, 0], dy


def _colsum(name, x, tr=512, tn=1024):
    R, N = x.shape
    tr, tn = _tile(R, tr, 8), _tile(N, tn)

    def body(x_ref, o_ref):
        i = pl.program_id(1)
        part = jnp.sum(x_ref[...].astype(F32), axis=0, keepdims=True)

        @pl.when(i == 0)
        def _():
            o_ref[...] = part

        @pl.when(i > 0)
        def _():
            o_ref[...] += part

    out = pl.pallas_call(
        body, name=name, grid=(N // tn, R // tr),
        in_specs=[pl.BlockSpec((tr, tn), lambda j, i: (i, j))], out_specs=pl.BlockSpec((1, tn), lambda j, i: (0, j)),
        out_shape=jax.ShapeDtypeStruct((1, N), F32),
        compiler_params=_cparams(("parallel", "arbitrary")),
    )(x)
    return out[0]


def _tri(tk, rel):
    j = lax.broadcasted_iota(jnp.int32, (tk, tk), 0)
    s = lax.broadcasted_iota(jnp.int32, (tk, tk), 1)
    return rel(j, s).astype(BF16)


def _dot_split(x, m, parts=2):
    total = None
    rem = x
    for _ in range(parts):
        piece = rem.astype(BF16)
        rem = rem - piece.astype(F32)
        term = jnp.dot(piece, m, preferred_element_type=F32)
        total = term if total is None else total + term
    return total


def _log_not_and_beta(z, mask):
    ln = -(jnp.maximum(z, 0.0) + jnp.log(1.0 + jnp.exp(-jnp.abs(z))))
    return (ln if mask is None else jnp.where(mask, ln, 0.0)), ln + z


def _att_tiles(T, Tk, causal):
    tq = min(ATT_TQ, T)
    tk = min(ATT_TK, tq if causal else Tk)
    return tq, tk, (tq if causal else Tk) // tk


def _key_base(j, tq):
    return j * tq if isinstance(j, int) else pl.multiple_of(j * tq, tq)


def _is_pow2(scale):
    return math.log2(scale).is_integer()


def _sb_fwd(q, k, v, scale):
    H, T, d = q.shape
    tq, tk, nsub = _att_tiles(T, T, True)
    assert _is_pow2(scale)

    def body(q_ref, k_ref, v_ref, o_ref, rt_ref, acc_ref, r_ref):
        qi = pl.program_id(1)
        qv = q_ref[0] * scale
        acc_ref[...] = jnp.zeros_like(acc_ref)
        r_ref[...] = jnp.zeros_like(r_ref)
        row = lax.broadcasted_iota(jnp.int32, (tq, tk), 0)
        col = lax.broadcasted_iota(jnp.int32, (tq, tk), 1)
        after = _tri(tk, lambda j, s: j > s)

        def step(j, diagonal):
            base = _key_base(j, tq)
            parts = []
            for u in reversed(range(nsub)):
                z = _dot(qv, k_ref[0, pl.ds(base + u * tk, tk), :], "nt")
                mask = (col + u * tk) < row if diagonal else None
                ln, lb = _log_not_and_beta(z, mask)
                between = _dot_split(ln, after)
                parts.append((u, lb, between, between[:, 0:1] + ln[:, 0:1], mask))
            r = r_ref[...]
            out = None
            for u, lb, between, total, mask in parts:
                w = jnp.exp(lb + between + r)
                if diagonal:
                    w = jnp.where(mask, w, 0.0)
                term = _dot(w, v_ref[0, pl.ds(base + u * tk, tk), :])
                out = term if out is None else out + term
                r = r + total
            acc_ref[...] += out
            r_ref[...] = r

        def below(i, carry):
            step(qi - 1 - i, False)
            return carry

        step(qi, True)
        lax.fori_loop(0, qi, below, 0)
        o_ref[0] = acc_ref[...]
        rt_ref[0] = r_ref[...]

    blk = pl.BlockSpec((1, tq, d), lambda h, i: (h, i, 0))
    full = pl.BlockSpec((1, T, d), lambda h, i: (h, 0, 0))
    col = pl.BlockSpec((1, tq, 1), lambda h, i: (h, i, 0))
    return pl.pallas_call(
        body, name="sb_fwd", grid=(H, T // tq),
        in_specs=[blk, full, full], out_specs=[blk, col],
        out_shape=[jax.ShapeDtypeStruct((H, T, d), F32), jax.ShapeDtypeStruct((H, T, 1), F32)],
        scratch_shapes=[pltpu.VMEM((tq, d), F32), pltpu.VMEM((tq, 1), F32)],
        compiler_params=_cparams(("parallel", "arbitrary")),
    )(q, k, v)


def _sb_bwd(q, k, v, do, rtot, scale):
    H, T, d = q.shape
    tq, tk, nsub = _att_tiles(T, T, True)
    assert _is_pow2(scale)

    def body(q_ref, k_ref, v_ref, do_ref, rt_ref, dq_ref, dk_ref, dv_ref, dq_acc, p_ref, c_ref):
        qi = pl.program_id(1)

        @pl.when(qi == 0)
        def _():
            dk_ref[...] = jnp.zeros_like(dk_ref)
            dv_ref[...] = jnp.zeros_like(dv_ref)

        qv = q_ref[0] * scale
        dov = do_ref[0]
        rt = rt_ref[0]
        dq_acc[...] = jnp.zeros_like(dq_acc)
        p_ref[...] = jnp.zeros_like(p_ref)
        c_ref[...] = jnp.zeros_like(c_ref)
        row = lax.broadcasted_iota(jnp.int32, (tq, tk), 0)
        col = lax.broadcasted_iota(jnp.int32, (tq, tk), 1)
        upto = _tri(tk, lambda j, s: j <= s)
        before = _tri(tk, lambda j, s: j < s)

        def step(j, diagonal):
            base = _key_base(j, tq)
            first = []
            for u in range(nsub):
                ks = base + u * tk
                kv = k_ref[0, pl.ds(ks, tk), :]
                z = _dot(qv, kv, "nt")
                mask = (col + u * tk) < row if diagonal else None
                ln, lb = _log_not_and_beta(z, mask)
                dw = _dot(dov, v_ref[0, pl.ds(ks, tk), :], "nt")
                first.append((ks, kv, mask, lb, jnp.exp(lb), _dot_split(ln, upto), dw))
            pre, cpre = p_ref[...], c_ref[...]
            dq = None
            for ks, kv, mask, lb, sig, local, dw in first:
                prefix = local + pre
                w = jnp.exp(lb + (rt - prefix))
                if diagonal:
                    w = jnp.where(mask, w, 0.0)
                g = dw * w
                c = _dot_split(g, before) + cpre
                dz = g * (1.0 - sig) - c * sig
                if diagonal:
                    dz = jnp.where(mask, dz, 0.0)
                term = _dot(dz, kv)
                dq = term if dq is None else dq + term
                dk_ref[0, pl.ds(ks, tk), :] += _dot(dz, qv, "tn")
                dv_ref[0, pl.ds(ks, tk), :] += _dot(w, dov, "tn")
                pre = prefix[:, tk - 1:tk]
                cpre = c[:, tk - 1:tk] + g[:, tk - 1:tk]
            dq_acc[...] += dq
            p_ref[...] = pre
            c_ref[...] = cpre

        def below(j, carry):
            step(j, False)
            return carry

        lax.fori_loop(0, qi, below, 0)
        step(qi, True)
        dq_ref[0] = (dq_acc[...] * scale).astype(dq_ref.dtype)

    blk = pl.BlockSpec((1, tq, d), lambda h, i: (h, i, 0))
    full = pl.BlockSpec((1, T, d), lambda h, i: (h, 0, 0))
    col = pl.BlockSpec((1, tq, 1), lambda h, i: (h, i, 0))
    return pl.pallas_call(
        body, name="sb_bwd", grid=(H, T // tq),
        in_specs=[blk, full, full, blk, col], out_specs=[blk, full, full],
        out_shape=[jax.ShapeDtypeStruct((H, T, d), BF16), jax.ShapeDtypeStruct((H, T, d), F32),
                   jax.ShapeDtypeStruct((H, T, d), F32)],
        scratch_shapes=[pltpu.VMEM((tq, d), F32), pltpu.VMEM((tq, 1), F32), pltpu.VMEM((tq, 1), F32)],
        compiler_params=_cparams(("parallel", "arbitrary")),
    )(q, k, v, do, rtot)


def _attn_fwd(name, q, k, v, scale, c=None):
    H, T, d = q.shape
    Tk = k.shape[1]
    causal = c is not None
    tq, tk, nsub = _att_tiles(T, Tk, causal)
    fold = _is_pow2(scale)

    def body(*refs):
        q_ref, k_ref, v_ref = refs[:3]
        cc_ref, cr_ref = refs[3:5] if causal else (None, None)
        o_ref, lse_ref, m_ref, l_ref, acc_ref = refs[-5:]
        qi = pl.program_id(1)
        qv = q_ref[0] * scale if fold else q_ref[0]
        m_ref[...] = jnp.full_like(m_ref, NEG)
        l_ref[...] = jnp.zeros_like(l_ref)
        acc_ref[...] = jnp.zeros_like(acc_ref)
        row = lax.broadcasted_iota(jnp.int32, (tq, tk), 0)
        col = lax.broadcasted_iota(jnp.int32, (tq, tk), 1)

        def step(j, diagonal):
            base = _key_base(j, tq)
            zs = []
            for u in range(nsub):
                z = _dot(qv, k_ref[0, pl.ds(base + u * tk, tk), :], "nt")
                if not fold:
                    z = z * scale
                if causal:
                    z = z + cc_ref[0] - cr_ref[0, j * nsub + u]
                if diagonal:
                    z = jnp.where((col + u * tk) <= row, z, NEG)
                zs.append(z)
            m_prev = m_ref[...]
            m_new = m_prev
            for z in zs:
                m_new = jnp.maximum(m_new, jnp.max(z, axis=1, keepdims=True))
            alpha = jnp.exp(m_prev - m_new)
            l_new = alpha * l_ref[...]
            out = alpha * acc_ref[...]
            for u, z in enumerate(zs):
                p = jnp.exp(z - m_new)
                l_new = l_new + jnp.sum(p, axis=1, keepdims=True)
                out = out + _dot(p, v_ref[0, pl.ds(base + u * tk, tk), :])
            l_ref[...] = l_new
            acc_ref[...] = out
            m_ref[...] = m_new

        def below(j, carry):
            step(j, False)
            return carry

        if causal:
            lax.fori_loop(0, qi, below, 0)
            step(qi, True)
        else:
            step(0, False)
        o_ref[0] = acc_ref[...] / l_ref[...]
        lse_ref[0] = m_ref[...] + jnp.log(l_ref[...])

    blk = pl.BlockSpec((1, tq, d), lambda h, i: (h, i, 0))
    full = pl.BlockSpec((1, Tk, d), lambda h, i: (h, 0, 0))
    col = pl.BlockSpec((1, tq, 1), lambda h, i: (h, i, 0))
    in_specs, args = [blk, full, full], [q, k, v]
    if causal:
        in_specs += [col, pl.BlockSpec((1, T // tk, 1, tk), lambda h, i: (h, 0, 0, 0))]
        args += [c.reshape(H, T, 1), c.reshape(H, T // tk, 1, tk)]
    return pl.pallas_call(
        body, name=name, grid=(H, T // tq),
        in_specs=in_specs, out_specs=[blk, col],
        out_shape=[jax.ShapeDtypeStruct((H, T, d), F32), jax.ShapeDtypeStruct((H, T, 1), F32)],
        scratch_shapes=[pltpu.VMEM((tq, 1), F32), pltpu.VMEM((tq, 1), F32), pltpu.VMEM((tq, d), F32)],
        compiler_params=_cparams(("parallel", "arbitrary")),
    )(*args)


def _attn_bwd(name, q, k, v, o, do, lse, scale, c=None):
    H, T, d = q.shape
    Tk = k.shape[1]
    causal = c is not None
    tq, tk, nsub = _att_tiles(T, Tk, causal)
    fold = _is_pow2(scale)

    def body(*refs):
        q_ref, k_ref, v_ref, o_ref, do_ref, lse_ref = refs[:6]
        cc_ref, cr_ref = refs[6:8] if causal else (None, None)
        n_out = 5 if causal else 3
        outs = refs[-(n_out + 1):-1]
        dq_ref, dk_ref, dv_ref = outs[:3]
        dc_ref, drow_ref = outs[3:5] if causal else (None, None)
        dq_acc = refs[-1]
        qi = pl.program_id(1)

        @pl.when(qi == 0)
        def _():
            dk_ref[...] = jnp.zeros_like(dk_ref)
            dv_ref[...] = jnp.zeros_like(dv_ref)
            if causal:
                dc_ref[...] = jnp.zeros_like(dc_ref)

        qv = q_ref[0] * scale if fold else q_ref[0]
        dov = do_ref[0]
        lse_v = lse_ref[0]
        delta = jnp.sum(dov.astype(F32) * o_ref[0], axis=1, keepdims=True)
        dq_acc[...] = jnp.zeros_like(dq_acc)
        if causal:
            drow_ref[...] = jnp.zeros_like(drow_ref)
        row = lax.broadcasted_iota(jnp.int32, (tq, tk), 0)
        col = lax.broadcasted_iota(jnp.int32, (tq, tk), 1)

        def step(j, diagonal):
            base = _key_base(j, tq)
            dq, drow = None, None
            for u in range(nsub):
                ks = base + u * tk
                kv = k_ref[0, pl.ds(ks, tk), :]
                z = _dot(qv, kv, "nt")
                if not fold:
                    z = z * scale
                if causal:
                    z = z + cc_ref[0] - cr_ref[0, j * nsub + u]
                if diagonal:
                    z = jnp.where((col + u * tk) <= row, z, NEG)
                p = jnp.exp(z - lse_v)
                ds = p * (_dot(dov, v_ref[0, pl.ds(ks, tk), :], "nt") - delta)
                term = _dot(ds, kv)
                dq = term if dq is None else dq + term
                dk = _dot(ds, qv, "tn")
                dk_ref[0, pl.ds(ks, tk), :] += dk if fold else dk * scale
                dv_ref[0, pl.ds(ks, tk), :] += _dot(p, dov, "tn")
                if causal:
                    dc_ref[0, j * nsub + u] -= jnp.sum(ds, axis=0, keepdims=True)
                    rs = jnp.sum(ds, axis=1, keepdims=True)
                    drow = rs if drow is None else drow + rs
            dq_acc[...] += dq
            if causal:
                drow_ref[0] += drow

        def below(j, carry):
            step(j, False)
            return carry

        if causal:
            lax.fori_loop(0, qi, below, 0)
            step(qi, True)
        else:
            step(0, False)
        dq_ref[0] = (dq_acc[...] * scale).astype(dq_ref.dtype)

    blk = pl.BlockSpec((1, tq, d), lambda h, i: (h, i, 0))
    full = pl.BlockSpec((1, Tk, d), lambda h, i: (h, 0, 0))
    col = pl.BlockSpec((1, tq, 1), lambda h, i: (h, i, 0))
    crow = pl.BlockSpec((1, T // tk, 1, tk), lambda h, i: (h, 0, 0, 0))
    in_specs, args = [blk, full, full, blk, blk, col], [q, k, v, o, do, lse]
    out_specs = [blk, full, full]
    out_shape = [jax.ShapeDtypeStruct((H, T, d), BF16), jax.ShapeDtypeStruct((H, Tk, d), F32),
                 jax.ShapeDtypeStruct((H, Tk, d), F32)]
    if causal:
        in_specs += [col, crow]
        args += [c.reshape(H, T, 1), c.reshape(H, T // tk, 1, tk)]
        out_specs += [crow, col]
        out_shape += [jax.ShapeDtypeStruct((H, T // tk, 1, tk), F32), jax.ShapeDtypeStruct((H, T, 1), F32)]
    outs = pl.pallas_call(
        body, name=name, grid=(H, T // tq),
        in_specs=in_specs, out_specs=out_specs, out_shape=out_shape,
        scratch_shapes=[pltpu.VMEM((tq, d), F32)],
        compiler_params=_cparams(("parallel", "arbitrary")),
    )(*args)
    if causal:
        return outs[0], outs[1], outs[2], outs[3].reshape(H, T), outs[4].reshape(H, T)
    return outs


def _decay_fwd(fl, b):
    H, T = fl.shape
    tk = DECAY_TK

    def body(x_ref, b_ref, c_ref):
        upto = _tri(tk, lambda j, s: j <= s)
        carry = jnp.zeros((H, 1), F32)
        for i in range(T // tk):
            xv = x_ref[:, i * tk:(i + 1) * tk] + b_ref[...]
            lf = jnp.minimum(xv, 0.0) - jnp.log(1.0 + jnp.exp(-jnp.abs(xv)))
            pref = _dot_split(lf, upto, parts=3) + carry
            c_ref[:, i * tk:(i + 1) * tk] = pref
            carry = pref[:, tk - 1:tk]

    vm = pl.BlockSpec(memory_space=pltpu.VMEM)
    return pl.pallas_call(
        body, name="decay_fwd", in_specs=[vm, vm], out_specs=vm,
        out_shape=jax.ShapeDtypeStruct((H, T), F32),
    )(fl, b)


def _decay_bwd(dc_cols, dc_rows, fl, b):
    H, T = fl.shape
    tk = DECAY_TK

    def body(dc_ref, dr_ref, x_ref, b_ref, dx_ref, db_ref):
        from_ = _tri(tk, lambda j, s: j >= s)
        carry = jnp.zeros((H, 1), F32)
        total = jnp.zeros((H, 1), F32)
        for i in reversed(range(T // tk)):
            sl = slice(i * tk, (i + 1) * tk)
            suffix = _dot_split(dc_ref[:, sl] + dr_ref[:, sl], from_, parts=3) + carry
            xv = x_ref[:, sl] + b_ref[...]
            dx = suffix / (1.0 + jnp.exp(xv))
            dx_ref[:, sl] = dx
            total = total + jnp.sum(dx, axis=1, keepdims=True)
            carry = suffix[:, 0:1]
        db_ref[...] = jnp.broadcast_to(total, db_ref.shape)

    vm = pl.BlockSpec(memory_space=pltpu.VMEM)
    dx, db = pl.pallas_call(
        body, name="decay_bwd", in_specs=[vm, vm, vm, vm], out_specs=[vm, vm],
        out_shape=[jax.ShapeDtypeStruct((H, T), F32), jax.ShapeDtypeStruct((H, LANE), F32)],
    )(dc_cols, dc_rows, fl, b)
    return dx, db[:, 0]


def _place():
    x, y, c = lax.axis_index("x"), lax.axis_index("y"), lax.axis_index("c")
    return x, y, c, [(1 - x, y), (x, 1 - y), (1 - x, 1 - y)]


def _all_gather(name, block):
    R, C = block.shape

    def body(x_ref, out_ref, send_sems, recv_sems, local_sem):
        x, y, c, chips = _place()
        me, sibling = (x, y, c), (x, y, 1 - c)

        def rows(px, py, pc):
            return out_ref.at[4 * px + 2 * py + pc]

        def copy(k, blk, to, src=None):
            return pltpu.make_async_remote_copy(
                src_ref=rows(*blk) if src is None else src, dst_ref=rows(*blk),
                send_sem=send_sems.at[k], recv_sem=recv_sems.at[k], device_id=to, device_id_type=MESH)

        mine = pltpu.make_async_copy(x_ref, rows(*me), local_sem)
        mine.start()
        first = [copy(0, me, sibling, src=x_ref)]
        first += [copy(1 + j, me, (*chip, c), src=x_ref) for j, chip in enumerate(chips)]
        for cp in first:
            cp.start()
        passed = [copy(4 + j, (*chip, c), sibling) for j, chip in enumerate(chips)]
        for j, chip in enumerate(chips):
            copy(1 + j, (*chip, c), me).wait_recv()
            passed[j].start()
        copy(0, sibling, me).wait_recv()
        for j, chip in enumerate(chips):
            copy(4 + j, (*chip, 1 - c), me).wait_recv()
        for cp in first + passed:
            cp.wait_send()
        mine.wait()

    return pl.pallas_call(
        body, name=name, in_specs=[ANY], out_specs=ANY,
        out_shape=jax.ShapeDtypeStruct((N_DEV, R, C), block.dtype),
        scratch_shapes=[pltpu.SemaphoreType.DMA((7,)), pltpu.SemaphoreType.DMA((7,)), pltpu.SemaphoreType.DMA(())],
    )(block)


def _swap_with_sibling(name, parts):
    _, R, C = parts.shape

    def body(p_ref, out_ref, send_sems, recv_sems):
        x, y, c, _ = _place()
        copies = [pltpu.make_async_remote_copy(
            src_ref=p_ref.at[2 * q + (1 - c)], dst_ref=out_ref.at[q],
            send_sem=send_sems.at[q], recv_sem=recv_sems.at[q], device_id=(x, y, 1 - c), device_id_type=MESH)
            for q in range(4)]
        for cp in copies:
            cp.start()
        for cp in copies:
            cp.wait_recv()
        for cp in copies:
            cp.wait_send()

    return pl.pallas_call(
        body, name=name, in_specs=[ANY], out_specs=ANY,
        out_shape=jax.ShapeDtypeStruct((4, R, C), parts.dtype),
        scratch_shapes=[pltpu.SemaphoreType.DMA((4,)), pltpu.SemaphoreType.DMA((4,))],
    )(parts)


def _add_own(name, parts, got, tr=512):
    _, R, C = parts.shape
    tr = _tile(R, tr, SUBLANE_BF16)

    def body(c_ref, p_ref, g_ref, o_ref):
        o_ref[...] = (p_ref[...].astype(F32) + g_ref[...].astype(F32)).astype(o_ref.dtype)

    return pl.pallas_call(
        body, name=name,
        grid_spec=pltpu.PrefetchScalarGridSpec(
            num_scalar_prefetch=1, grid=(4, R // tr),
            in_specs=[pl.BlockSpec((1, tr, C), lambda q, i, c: (2 * q + c[0], i, 0)),
                      pl.BlockSpec((1, tr, C), lambda q, i, c: (q, i, 0))],
            out_specs=pl.BlockSpec((1, tr, C), lambda q, i, c: (q, i, 0))),
        out_shape=jax.ShapeDtypeStruct((4, R, C), parts.dtype),
        compiler_params=_cparams(("parallel", "parallel")),
    )(lax.axis_index("c").astype(jnp.int32).reshape(1), parts, got)


def _swap_with_chips(name, parts):
    _, R, C = parts.shape

    def body(p_ref, out_ref, send_sems, recv_sems, local_sem):
        x, y, c, chips = _place()
        my_chip = 2 * x + y
        mine = pltpu.make_async_copy(p_ref.at[my_chip], out_ref.at[my_chip], local_sem)
        mine.start()
        sends = [pltpu.make_async_remote_copy(
            src_ref=p_ref.at[2 * cx + cy], dst_ref=out_ref.at[my_chip],
            send_sem=send_sems.at[j], recv_sem=recv_sems.at[j], device_id=(cx, cy, c), device_id_type=MESH)
            for j, (cx, cy) in enumerate(chips)]
        for cp in sends:
            cp.start()
        for j, (cx, cy) in enumerate(chips):
            pltpu.make_async_remote_copy(
                src_ref=p_ref.at[my_chip], dst_ref=out_ref.at[2 * cx + cy],
                send_sem=send_sems.at[j], recv_sem=recv_sems.at[j], device_id=(cx, cy, c), device_id_type=MESH,
            ).wait_recv()
        for cp in sends:
            cp.wait_send()
        mine.wait()

    return pl.pallas_call(
        body, name=name, in_specs=[ANY], out_specs=ANY,
        out_shape=jax.ShapeDtypeStruct((4, R, C), parts.dtype),
        scratch_shapes=[pltpu.SemaphoreType.DMA((3,)), pltpu.SemaphoreType.DMA((3,)), pltpu.SemaphoreType.DMA(())],
    )(parts)


def _sum_parts(name, parts, tr=512):
    P, R, C = parts.shape
    tr = _tile(R, tr, SUBLANE_BF16)

    def body(p_ref, o_ref):
        total = p_ref[0].astype(F32)
        for p in range(1, P):
            total = total + p_ref[p].astype(F32)
        o_ref[...] = total

    return pl.pallas_call(
        body, name=name, grid=(R // tr,),
        in_specs=[pl.BlockSpec((P, tr, C), lambda i: (0, i, 0))], out_specs=pl.BlockSpec((tr, C), lambda i: (i, 0)),
        out_shape=jax.ShapeDtypeStruct((R, C), F32),
        compiler_params=_cparams(("parallel",)),
    )(parts)


_HBM = pl.BlockSpec(memory_space=pltpu.HBM)
_SEM = pl.BlockSpec(memory_space=pltpu.SEMAPHORE)
_EFFECT = pltpu.SideEffectType.DATAFLOW_SIDE_EFFECTING


def _flipped(x, y, c, k):
    px, py, pc = (1 - x if k & 4 else x), (1 - y if k & 2 else y), (1 - c if k & 1 else c)
    return (px, py, pc), 4 * px + 2 * py + pc


def _exchange_start(name, src, per_peer):
    R, C = src.shape[-2:]

    def body(v_ref, land_ref, send_sem, recv_sem, v_thru, land_thru, token):
        x, y, c = lax.axis_index("x"), lax.axis_index("y"), lax.axis_index("c")
        me = 4 * x + 2 * y + c
        for k in range(1, N_DEV):
            peer, idx = _flipped(x, y, c, k)
            pltpu.make_async_remote_copy(
                src_ref=v_ref.at[idx] if per_peer else v_ref, dst_ref=land_ref.at[me],
                send_sem=send_sem, recv_sem=recv_sem, device_id=peer, device_id_type=MESH).start()
        token[...] = jnp.zeros_like(token)

    return pl.pallas_call(
        body, name=name,
        out_shape=(pltpu.SemaphoreType.DMA(()), pltpu.SemaphoreType.DMA(()), pltpu.HBM(src.shape, src.dtype),
                   pltpu.HBM((N_DEV, R, C), src.dtype), jax.ShapeDtypeStruct((8, LANE), F32)),
        in_specs=(_HBM, _HBM), out_specs=(_SEM, _SEM, _HBM, _HBM, pl.BlockSpec(memory_space=pltpu.VMEM)),
        input_output_aliases={0: 2, 1: 3},
        compiler_params=pltpu.CompilerParams(has_side_effects=_EFFECT),
    )(pltpu.with_memory_space_constraint(src, pltpu.HBM),
      pltpu.with_memory_space_constraint(lax.empty((N_DEV, R, C), src.dtype), pltpu.HBM))


def _exchange_wait(name, started, after):
    send_sem, recv_sem, v_thru, land_thru, _ = started

    def body(v_ref, land_ref, send_sem, recv_sem, after_ref, v_dead, got_ref):
        x, y, c = lax.axis_index("x"), lax.axis_index("y"), lax.axis_index("c")
        seven = land_ref.at[pl.ds(0, N_DEV - 1)]
        drain = pltpu.make_async_remote_copy(
            src_ref=seven, dst_ref=seven, send_sem=send_sem, recv_sem=recv_sem,
            device_id=(x, y, c), device_id_type=MESH)
        drain.wait_send()
        drain.wait_recv()

    return pl.pallas_call(
        body, name=name,
        out_shape=(pltpu.HBM(v_thru.shape, v_thru.dtype), pltpu.HBM(land_thru.shape, land_thru.dtype)),
        in_specs=(_HBM, _HBM, _SEM, _SEM, ANY), out_specs=(_HBM, _HBM), input_output_aliases={0: 0, 1: 1},
        compiler_params=pltpu.CompilerParams(has_side_effects=_EFFECT),
    )(v_thru, land_thru, send_sem, recv_sem, after)[1]


def _my_index():
    return 4 * lax.axis_index("x") + 2 * lax.axis_index("y") + lax.axis_index("c")


def _sum_landed(name, landed, parts, tr=512):
    P, R, C = landed.shape
    tr = _tile(R, tr, SUBLANE_BF16)

    def body(me_ref, l_ref, own_ref, o_ref):
        total = None
        for s in range(P):
            part = jnp.where(me_ref[0] == s, own_ref[0], l_ref[s]).astype(F32)
            total = part if total is None else total + part
        o_ref[...] = total

    return pl.pallas_call(
        body, name=name,
        grid_spec=pltpu.PrefetchScalarGridSpec(
            num_scalar_prefetch=1, grid=(R // tr,),
            in_specs=[pl.BlockSpec((P, tr, C), lambda i, me: (0, i, 0)),
                      pl.BlockSpec((1, tr, C), lambda i, me: (me[0], i, 0))],
            out_specs=pl.BlockSpec((tr, C), lambda i, me: (i, 0))),
        out_shape=jax.ShapeDtypeStruct((R, C), F32),
        compiler_params=_cparams(("parallel",)),
    )(_my_index().astype(jnp.int32).reshape(1), landed, parts)


def _after(params, name, token):
    return {**params, name: params[name] + token[0, 0]}


def _reduce_scatter(tag, parts):
    got = _swap_with_sibling("rs_pair_" + tag, parts)
    pair = _add_own("rs_add_" + tag, parts, got)
    quad = _swap_with_chips("rs_chips_" + tag, pair)
    return _sum_parts("rs_sum_" + tag, quad)


def _adamw(name, g_parts, w, m, v, tr=512):
    P, R, C = g_parts.shape
    tr = _tile(R, tr, 8)

    def body(g_ref, w_ref, m_ref, v_ref, go_ref, d_ref, mo_ref, vo_ref):
        g = g_ref[0]
        for p in range(1, P):
            g = g + g_ref[p]
        mn = ADAM_B1 * m_ref[...] + (1.0 - ADAM_B1) * g
        vn = ADAM_B2 * v_ref[...] + (1.0 - ADAM_B2) * (g * g)
        m_hat = mn / (1.0 - ADAM_B1 ** ADAM_STEP)
        v_hat = vn / (1.0 - ADAM_B2 ** ADAM_STEP)
        go_ref[...] = g
        d_ref[...] = -ADAM_LR * (m_hat / (jnp.sqrt(v_hat) + ADAM_EPS) + ADAM_WD * w_ref[...])
        mo_ref[...] = mn
        vo_ref[...] = vn

    row = pl.BlockSpec((tr, C), lambda i: (i, 0))
    return pl.pallas_call(
        body, name=name, grid=(R // tr,),
        in_specs=[pl.BlockSpec((P, tr, C), lambda i: (0, i, 0)), row, row, row], out_specs=[row] * 4,
        out_shape=[jax.ShapeDtypeStruct((R, C), F32)] * 4,
        compiler_params=_cparams(("parallel",)),
    )(g_parts, w, m, v)


def _to_heads(t, n_heads):
    rows = t.shape[0]
    return t.reshape(rows, n_heads, -1).transpose(1, 0, 2)


def _from_heads(t):
    return t.transpose(1, 0, 2).reshape(t.shape[1], -1)


def _pad_rows(t, rows):
    return jnp.pad(t, ((0, rows - t.shape[0]), (0, 0)))


class _Layout:
    def __init__(self, D, ff_shard, in_shard, kv_shard, gate_shard, br_in, br_shard, out_shard):
        self.D = D
        self.in_shard = in_shard
        self.in_pad = -(-in_shard // LANE) * LANE
        self.in_cols = -(-N_DEV * in_shard // IN_TILE) * IN_TILE
        self.br_in, self.br_shard = br_in, br_shard
        br_rows = br_shard * br_in // D
        sizes = [("g1", ff_shard), ("u1", ff_shard), ("d1", ff_shard), ("win", self.in_pad), ("kv", kv_shard),
                 ("gate", gate_shard), ("br", br_rows), ("out", out_shard),
                 ("g2", ff_shard), ("u2", ff_shard), ("d2", ff_shard)]
        self.seg, off = {}, 0
        for key, n in sizes:
            assert n % SUBLANE_BF16 == 0, (key, n)
            self.seg[key] = (off, n)
            off += n
        self.rows = off

    def pack(self, parts):
        return jnp.concatenate([parts[key] for key in self.seg], axis=0)

    def take(self, gathered, key, own=None):
        off, n = self.seg[key]
        seg = gathered[:, off:off + n, :]
        if own is not None:
            seg = lax.dynamic_update_slice(seg, own[0][off:off + n][None], (own[1], 0, 0))
        return seg.reshape(N_DEV * n, self.D)

    def spread(self, full, key):
        _, n = self.seg[key]
        return full.reshape(N_DEV, n, self.D)


def _pack_layer(lay, l, p):
    D = lay.D
    br = jnp.concatenate([p["w_br_sb"][l], p["w_br_fox"][l], p["w_br_mem"][l]], axis=0)
    parts = {
        "g1": p["ffn1_w_gate"][l].T, "u1": p["ffn1_w_up"][l].T, "d1": p["ffn1_w_down"][l],
        "win": _pad_rows(p["w_in"][l].T, lay.in_pad), "kv": p["w_mem_kv"][l], "gate": p["w_gate"][l].T,
        "br": br.T.reshape(-1, D), "out": p["w_out"][l],
        "g2": p["ffn2_w_gate"][l].T, "u2": p["ffn2_w_up"][l].T, "d2": p["ffn2_w_down"][l],
    }
    return lay.pack({k: t.astype(BF16) for k, t in parts.items()})


def _align_win(lay, packed):
    D = lay.D
    real = packed.reshape(N_DEV, lay.in_pad, D)[:, :lay.in_shard].reshape(N_DEV * lay.in_shard, D)
    rows = jnp.concatenate([real[:_QKV_W], real[_QKV_W + N_FOX_HEADS:], real[_QKV_W:_QKV_W + N_FOX_HEADS]], axis=0)
    return _pad_rows(rows, lay.in_cols)


def _unalign_win(lay, aligned):
    D = lay.D
    n_real = N_DEV * lay.in_shard
    mem_w = n_real - _QKV_W - N_FOX_HEADS
    real = jnp.concatenate([aligned[:_QKV_W], aligned[_QKV_W + mem_w:n_real], aligned[_QKV_W:_QKV_W + mem_w]], axis=0)
    real = real.reshape(N_DEV, lay.in_shard, D)
    return jnp.pad(real, ((0, 0), (0, lay.in_pad - lay.in_shard), (0, 0))).reshape(N_DEV * lay.in_pad, D)


def _unpack_layer(lay, gathered, own=None):
    D = lay.D
    w = {k: lay.take(gathered, k, own) for k in ("g1", "u1", "d1", "kv", "out", "g2", "u2", "d2")}
    w["win"] = _align_win(lay, lay.take(gathered, "win", own))
    fl0 = N_DEV * lay.in_shard - N_FOX_HEADS
    w["wfl"] = w["win"][fl0:fl0 + LANE]
    gate = lay.take(gathered, "gate", own)
    w["gate"] = gate
    w["gate3"] = [gate[i * D:(i + 1) * D] for i in range(3)]
    br = lay.take(gathered, "br", own).reshape(N_DEV * lay.br_shard, lay.br_in)
    third = lay.br_in // 3
    w["br3"] = [br[:, i * third:(i + 1) * third] for i in range(3)]
    return w


def _silu_mul(accs, _):
    a, b = accs
    return [a, b, a * jax.nn.sigmoid(a) * b]


def _act_bwd(accs, extras):
    ds, (a, b) = accs[0], extras
    sig = jax.nn.sigmoid(a)
    return [ds * b * (sig * (1.0 + a * (1.0 - sig))), ds * (a * sig)]


def _ffn_fwd(tag, h, pre_g, post_g, wg, wu, wd):
    n = _rms_fwd("ffn_norm_" + tag, h, pre_g, BF16)
    a, b, s = _mm("ffn_up_" + tag, [(n, wg), (n, wu)], "nt", [F32, F32, BF16], _silu_mul, tm=256, tn=1408)
    f = _mm("ffn_down_" + tag, [(s, wd)], "nn", [F32], tm=256)
    out = _rms_fwd("ffn_out_" + tag, f, post_g, F32, res=h, scale=0.5)
    return out, (h, n, a, b, s, f)


def _ffn_bwd(tag, dh, saved, pre_g, post_g, wg, wu, wd):
    h, n, a, b, s, f = saved
    df, d_post = _rms_bwd("ffn_dout_" + tag, f, post_g, dh, BF16, scale=0.5)
    da, db = _mm("ffn_dact_" + tag, [(df, wd)], "nt", [BF16, BF16], _act_bwd, [(a, 0), (b, 0)], tm=256, tn=1408)
    d_wd = _mm("ffn_dwd_" + tag, [(s, df)], "tn", [BF16], tm=256)
    dn = _mm("ffn_dn_" + tag, [(da, wg), (db, wu)], "nn", [F32], _sum_accs, tm=256)
    d_wg = _mm("ffn_dwg_" + tag, [(da, n)], "tn", [BF16], tm=256)
    d_wu = _mm("ffn_dwu_" + tag, [(db, n)], "tn", [BF16], tm=256)
    dh_in, d_pre = _rms_bwd("ffn_dnorm_" + tag, h, pre_g, dn, F32, res=dh)
    return dh_in, d_pre, d_post, d_wg, d_wu, d_wd


_SB_W = N_SB_HEADS * HEAD_DIM
_FOX_W = N_FOX_HEADS * HEAD_DIM
_QKV_W = 3 * _SB_W + 3 * _FOX_W
_SPLITS = [_SB_W, 2 * _SB_W, 3 * _SB_W, 3 * _SB_W + _FOX_W, 3 * _SB_W + 2 * _FOX_W, _QKV_W]


def _gate_act(accs, extras):
    return [jax.nn.sigmoid(accs[0] + extras[0])]


def _merge(accs, extras):
    return [extras[0] * accs[0] + extras[1] * accs[1] + extras[2] * accs[2]]


def _merge_bwd(accs, extras):
    dm = accs[0]
    d_branch = [dm * gi for gi in extras]
    d_gate = [dm * bi * gi * (1.0 - gi) for bi, gi in zip(accs[1:], extras)]
    return d_branch + d_gate


def _mix_fwd(lay, h, w, pre_g, post_g, b_forget, b_gate, mem_n):
    D = lay.D
    u = _rms_fwd("mix_norm", h, pre_g, BF16)
    proj = _mm("mix_in", [(u, w["win"])], "nt", [BF16], tn=IN_TILE)
    mem_w = N_DEV * lay.in_shard - _QKV_W - N_FOX_HEADS
    q_sb, k_sb, v_sb, q_fx, k_fx, v_fx, rest = jnp.split(proj, _SPLITS, axis=1)
    sb = [_to_heads(t, N_SB_HEADS) for t in (q_sb, k_sb, v_sb)]
    fx = [_to_heads(t, N_FOX_HEADS) for t in (q_fx, k_fx, v_fx)]
    qm = _to_heads(rest[:, :mem_w], N_MEM_HEADS)
    fl = _mm("mix_fl", [(u, w["wfl"])], "nt", [F32])[:, :N_FOX_HEADS].T
    c = _decay_fwd(fl, b_forget.reshape(-1, 1))
    o_sb, rtot = _sb_fwd(*sb, HEAD_DIM ** -0.5)
    o_fx, lse_fx = _attn_fwd("fox_fwd", *fx, HEAD_DIM ** -0.5, c)
    kvm = _mm("mem_kv", [(mem_n, w["kv"])], "nn", [BF16])
    half = kvm.shape[1] // 2
    km, vm = _to_heads(kvm[:, :half], N_MEM_HEADS), _to_heads(kvm[:, half:], N_MEM_HEADS)
    o_mem, lse_mem = _attn_fwd("mem_fwd", qm, km, vm, qm.shape[-1] ** -0.5)
    gates = _mm("mix_gate", [(u, w["gate"])], "nt", [F32], _gate_act, [(b_gate.reshape(1, -1), 0)])
    flat = [_from_heads(o).astype(BF16) for o in (o_sb, o_fx, o_mem)]
    merged = _mm("mix_merge", list(zip(flat, w["br3"])), "nt", [BF16], _merge,
                 [(gates, 0), (gates, D), (gates, 2 * D)])
    z = _mm("mix_out", [(merged, w["out"])], "nn", [F32])
    out = _rms_fwd("mix_res", z, post_g, F32, res=h)
    saved = (h, u, sb, fx, qm, fl, c, o_sb, rtot, o_fx, lse_fx, km, vm, o_mem, lse_mem, gates, flat, merged, z)
    return out, saved


def _mix_bwd(lay, dh, saved, w, pre_g, post_g, b_forget, mem_n, dmem_n):
    D = lay.D
    h, u, sb, fx, qm, fl, c, o_sb, rtot, o_fx, lse_fx, km, vm, o_mem, lse_mem, gates, flat, merged, z = saved
    dz, d_post = _rms_bwd("mix_dres", z, post_g, dh, BF16)
    outs = _mm("mix_dmerge", [(dz, w["out"])] + list(zip(flat, w["br3"])), "nt", [BF16] * 6, _merge_bwd,
               [(gates, 0), (gates, D), (gates, 2 * D)])
    d_branch, d_gate = outs[:3], outs[3:]
    d_wout = _mm("mix_dwout", [(merged, dz)], "tn", [BF16])
    d_o = [_mm("mix_dbr%d" % i, [(d_branch[i], w["br3"][i])], "nn", [BF16]) for i in range(3)]
    d_wbr = [_mm("mix_dwbr%d" % i, [(d_branch[i], flat[i])], "tn", [BF16]) for i in range(3)]
    d_bgate = jnp.concatenate([_colsum("mix_dbgate%d" % i, d_gate[i]) for i in range(3)])
    d_wgate = [_mm("mix_dwgate%d" % i, [(d_gate[i], u)], "tn", [BF16]) for i in range(3)]

    dq_s, dk_s, dv_s = _sb_bwd(*sb, _to_heads(d_o[0], N_SB_HEADS), rtot, HEAD_DIM ** -0.5)
    dq_f, dk_f, dv_f, dc, dc_rows = _attn_bwd("fox_bwd", *fx, o_fx, _to_heads(d_o[1], N_FOX_HEADS), lse_fx,
                                     HEAD_DIM ** -0.5, c)
    dq_m, dk_m, dv_m = _attn_bwd("mem_bwd", qm, km, vm, o_mem, _to_heads(d_o[2], N_MEM_HEADS), lse_mem,
                                 qm.shape[-1] ** -0.5)
    dfl, d_bforget = _decay_bwd(dc, dc_rows, fl, b_forget.reshape(-1, 1))
    dproj = jnp.concatenate(
        [_from_heads(t).astype(BF16) for t in (dq_s, dk_s, dv_s, dq_f, dk_f, dv_f, dq_m)] + [dfl.T.astype(BF16)],
        axis=1)
    dproj = jnp.pad(dproj, ((0, 0), (0, lay.in_cols - dproj.shape[1])))
    du = _mm("mix_du", list(zip(d_gate, w["gate3"])) + [(dproj, w["win"])], "nn", [F32], _sum_accs, tm=256)
    d_win = _unalign_win(lay, _mm("mix_dwin", [(dproj, u)], "tn", [BF16]))
    dh_in, d_pre = _rms_bwd("mix_dnorm", h, pre_g, du, F32, res=dh)

    dkvm = jnp.concatenate([_from_heads(dk_m), _from_heads(dv_m)], axis=1).astype(BF16)
    d_wkv = _mm("mem_dwkv", [(mem_n, dkvm)], "tn", [BF16])
    dmem_n = _mm("mem_dn", [(dkvm, w["kv"])], "nt", [F32], lambda accs, ex: [accs[0] + ex[0]], [(dmem_n, 0)])
    grads = {"win": d_win, "kv": d_wkv, "gate": jnp.concatenate(d_wgate, axis=0),
             "br": jnp.concatenate(d_wbr, axis=1), "out": d_wout}
    return dh_in, d_pre, d_post, d_bforget, d_bgate, grads, dmem_n


def _layer_fwd(lay, h, w, sp, mem_n):
    h1, s1 = _ffn_fwd("1", h, sp["ffn1_pre_g"], sp["ffn1_post_g"], w["g1"], w["u1"], w["d1"])
    h2, s2 = _mix_fwd(lay, h1, w, sp["mix_pre_g"], sp["mix_post_g"], sp["b_forget"], sp["b_gate"], mem_n)
    h3, s3 = _ffn_fwd("2", h2, sp["ffn2_pre_g"], sp["ffn2_post_g"], w["g2"], w["u2"], w["d2"])
    return h3, (s1, s2, s3)


def _layer_bwd(lay, dh, saved, w, sp, mem_n, dmem_n):
    s1, s2, s3 = saved
    dh, d_pre2, d_post2, d_g2, d_u2, d_d2 = _ffn_bwd("2", dh, s3, sp["ffn2_pre_g"], sp["ffn2_post_g"],
                                                     w["g2"], w["u2"], w["d2"])
    dh, d_mpre, d_mpost, d_bforget, d_bgate, g, dmem_n = _mix_bwd(
        lay, dh, s2, w, sp["mix_pre_g"], sp["mix_post_g"], sp["b_forget"], mem_n, dmem_n)
    dh, d_pre1, d_post1, d_g1, d_u1, d_d1 = _ffn_bwd("1", dh, s1, sp["ffn1_pre_g"], sp["ffn1_post_g"],
                                                     w["g1"], w["u1"], w["d1"])
    g.update({"g1": d_g1, "u1": d_u1, "d1": d_d1, "g2": d_g2, "u2": d_u2, "d2": d_d2})
    g["br"] = g["br"].reshape(N_DEV, lay.br_shard, lay.br_in).reshape(-1, lay.D)
    packed = jnp.concatenate([lay.spread(g[key], key) for key in lay.seg], axis=1)
    small = {"ffn1_pre_g": d_pre1, "ffn1_post_g": d_post1, "mix_pre_g": d_mpre, "mix_post_g": d_mpost,
             "ffn2_pre_g": d_pre2, "ffn2_post_g": d_post2, "b_gate": d_bgate, "b_forget": d_bforget}
    return dh, packed, small, dmem_n


_SHARDED = ["ffn1_w_gate", "ffn1_w_up", "ffn1_w_down", "w_in", "w_mem_kv", "w_gate", "w_br_sb", "w_br_fox",
            "w_br_mem", "w_out", "ffn2_w_gate", "ffn2_w_up", "ffn2_w_down"]
_SMALL_LAYER = ["ffn1_pre_g", "ffn1_post_g", "mix_pre_g", "mix_post_g", "ffn2_pre_g", "ffn2_post_g", "b_gate",
                "b_forget"]
_WEIGHTS = ["ffn1_pre_g", "ffn1_post_g", "ffn1_w_gate", "ffn1_w_up", "ffn1_w_down", "mix_pre_g", "mix_post_g",
            "w_in", "b_forget", "mem_norm_g", "w_mem_kv", "w_gate", "b_gate", "w_br_sb", "w_br_fox", "w_br_mem",
            "w_out", "ffn2_pre_g", "ffn2_post_g", "ffn2_w_gate", "ffn2_w_up", "ffn2_w_down"]


def _pack_small(vals, L, D):
    rows = []
    for l in range(L):
        for name in _SMALL_LAYER:
            t = vals[name][l]
            rows.append(jnp.pad(t, (0, -t.shape[0] % D)).reshape(-1, D))
    rows.append(vals["mem_norm_g"].reshape(1, D))
    packed = jnp.concatenate(rows, axis=0)
    return _pad_rows(packed, -(-packed.shape[0] // 8) * 8)


def _unpack_small(packed, shapes, L, D):
    out = {name: [] for name in _SMALL_LAYER}
    r = 0
    for l in range(L):
        for name in _SMALL_LAYER:
            n = shapes[name][1]
            nr = -(-n // D)
            out[name].append(packed[r:r + nr].reshape(-1)[:n])
            r += nr
    res = {name: jnp.stack(v) for name, v in out.items()}
    res["mem_norm_g"] = packed[r]
    return res


def _unpack_grads(lay, g, l_shapes):
    def seg(key):
        off, n = lay.seg[key]
        return g[off:off + n]
    br = seg("br").reshape(lay.br_shard, lay.br_in).T
    third = lay.br_in // 3
    return {
        "ffn1_w_gate": seg("g1").T, "ffn1_w_up": seg("u1").T, "ffn1_w_down": seg("d1"),
        "w_in": seg("win")[:lay.in_shard].T, "w_mem_kv": seg("kv"), "w_gate": seg("gate").T,
        "w_br_sb": br[:third], "w_br_fox": br[third:2 * third], "w_br_mem": br[2 * third:],
        "w_out": seg("out"), "ffn2_w_gate": seg("g2").T, "ffn2_w_up": seg("u2").T, "ffn2_w_down": seg("d2"),
    }


class _Exchanges:
    def gather(self, name, block):
        return _all_gather(name, block)

    def gather_start(self, block):
        return _exchange_start("ag_start", block, per_peer=False)

    def gather_wait(self, started, after, block):
        return _exchange_wait("ag_wait", started, after), (block, _my_index())

    def scatter(self, parts):
        return _reduce_scatter("w", parts)

    def scatter_start(self, parts):
        return _exchange_start("rs_start", parts, per_peer=True)

    def scatter_wait(self, started, after, parts):
        return _sum_landed("rs_sum8", _exchange_wait("rs_wait", started, after), parts)

    def token(self, started):
        return started[4]

    def loss_sum(self, part):
        return lax.psum(part, ("x", "y", "c"))


def _step(p, m, v, x, mem, tgt, ex):
    L, D = p["ffn1_pre_g"].shape
    lay = _Layout(D, p["ffn1_w_gate"].shape[2], p["w_in"].shape[2], p["w_mem_kv"].shape[1], p["w_gate"].shape[2],
                  3 * p["w_br_sb"].shape[1], p["w_br_sb"].shape[2], p["w_out"].shape[1])
    blocks = [_pack_layer(lay, l, p) for l in range(L)]
    sps = [{name: p[name][l] for name in _SMALL_LAYER} for l in range(L)]

    mem_n = _rms_fwd("mem_norm", mem, p["mem_norm_g"], BF16)
    gathered, own = ex.gather("ag_weights", blocks[0]), None
    h, saved, ws = x, [], []
    for l in range(L):
        if l + 1 < L:
            nxt, gathered = lax.optimization_barrier((blocks[l + 1], gathered))
            started = ex.gather_start(nxt)
            sp = _after(sps[l], "ffn1_pre_g", ex.token(started))
        else:
            sp = sps[l]
        ws.append(_unpack_layer(lay, gathered, own))
        h, s = _layer_fwd(lay, h, ws[l], sp, mem_n)
        saved.append(s)
        if l + 1 < L:
            gathered, own = ex.gather_wait(started, h, blocks[l + 1])
    loss_part, dh = _loss_grad(h, tgt)
    loss = ex.loss_sum(loss_part)

    dmem_n = jnp.zeros(mem.shape, F32)
    big, small = [None] * L, {name: [None] * L for name in _SMALL_LAYER}
    flying, token = {}, None
    for l in reversed(range(L)):
        sp = sps[l] if token is None else _after(sps[l], "ffn2_post_g", token)
        dh, packed, sm, dmem_n = _layer_bwd(lay, dh, saved[l], ws[l], sp, mem_n, dmem_n)
        if l > 0:
            flying[l] = (ex.scatter_start(packed), packed)
            token = ex.token(flying[l][0])
        else:
            big[l] = _unpack_grads(lay, ex.scatter(packed), None)
        for name in _SMALL_LAYER:
            small[name][l] = sm[name]
    for l, (started, packed) in flying.items():
        big[l] = _unpack_grads(lay, ex.scatter_wait(started, dh, packed), None)
    _, d_memg = _rms_bwd("mem_dnorm", mem, p["mem_norm_g"], dmem_n, F32)

    small_g = {name: jnp.stack(vs) for name, vs in small.items()}
    small_g["mem_norm_g"] = d_memg
    small_names = _SMALL_LAYER + ["mem_norm_g"]
    shapes = {name: p[name].shape for name in small_names}
    g_all = ex.gather("ag_small", _pack_small(small_g, L, D))
    packs = [_pack_small({name: t[name] for name in small_names}, L, D) for t in (p, m, v)]
    res = [_unpack_small(t, shapes, L, D) for t in _adamw("adamw_small", g_all, *packs)]

    out = {kind: {} for kind in ("grad", "delta", "new_m", "new_v")}
    for name in small_names:
        for kind, r in zip(("grad", "delta", "new_m", "new_v"), res):
            out[kind][name] = r[name].reshape(p[name].shape)
    for name in _SHARDED:
        g = jnp.stack([big[l][name] for l in range(L)])
        shp = g.shape
        flat = lambda t: t.reshape(-1, shp[-1])
        r = _adamw("adamw_" + name, flat(g)[None], flat(p[name]), flat(m[name]), flat(v[name]))
        for kind, t in zip(("grad", "delta", "new_m", "new_v"), r):
            out[kind][name] = t.reshape(shp)
    return loss, dh, out


def kernel(x, mem, ffn1_pre_g, ffn1_post_g, ffn1_w_gate, ffn1_w_up, ffn1_w_down, mix_pre_g, mix_post_g, w_in, b_forget, mem_norm_g, w_mem_kv, w_gate, b_gate, w_br_sb, w_br_fox, w_br_mem, w_out, ffn2_pre_g, ffn2_post_g, ffn2_w_gate, ffn2_w_up, ffn2_w_down, loss_target, m_ffn1_pre_g, m_ffn1_post_g, m_ffn1_w_gate, m_ffn1_w_up, m_ffn1_w_down, m_mix_pre_g, m_mix_post_g, m_w_in, m_b_forget, m_mem_norm_g, m_w_mem_kv, m_w_gate, m_b_gate, m_w_br_sb, m_w_br_fox, m_w_br_mem, m_w_out, m_ffn2_pre_g, m_ffn2_post_g, m_ffn2_w_gate, m_ffn2_w_up, m_ffn2_w_down, v_ffn1_pre_g, v_ffn1_post_g, v_ffn1_w_gate, v_ffn1_w_up, v_ffn1_w_down, v_mix_pre_g, v_mix_post_g, v_w_in, v_b_forget, v_mem_norm_g, v_w_mem_kv, v_w_gate, v_b_gate, v_w_br_sb, v_w_br_fox, v_w_br_mem, v_w_out, v_ffn2_pre_g, v_ffn2_post_g, v_ffn2_w_gate, v_ffn2_w_up, v_ffn2_w_down):
    p = dict(zip(_WEIGHTS, (ffn1_pre_g, ffn1_post_g, ffn1_w_gate, ffn1_w_up, ffn1_w_down, mix_pre_g, mix_post_g, w_in, b_forget, mem_norm_g, w_mem_kv, w_gate, b_gate, w_br_sb, w_br_fox, w_br_mem, w_out, ffn2_pre_g, ffn2_post_g, ffn2_w_gate, ffn2_w_up, ffn2_w_down)))
    m = dict(zip(_WEIGHTS, (m_ffn1_pre_g, m_ffn1_post_g, m_ffn1_w_gate, m_ffn1_w_up, m_ffn1_w_down, m_mix_pre_g, m_mix_post_g, m_w_in, m_b_forget, m_mem_norm_g, m_w_mem_kv, m_w_gate, m_b_gate, m_w_br_sb, m_w_br_fox, m_w_br_mem, m_w_out, m_ffn2_pre_g, m_ffn2_post_g, m_ffn2_w_gate, m_ffn2_w_up, m_ffn2_w_down)))
    v = dict(zip(_WEIGHTS, (v_ffn1_pre_g, v_ffn1_post_g, v_ffn1_w_gate, v_ffn1_w_up, v_ffn1_w_down, v_mix_pre_g, v_mix_post_g, v_w_in, v_b_forget, v_mem_norm_g, v_w_mem_kv, v_w_gate, v_b_gate, v_w_br_sb, v_w_br_fox, v_w_br_mem, v_w_out, v_ffn2_pre_g, v_ffn2_post_g, v_ffn2_w_gate, v_ffn2_w_up, v_ffn2_w_down)))
    loss, dx, out = _step(p, m, v, x[0], mem[0], loss_target[0], _Exchanges())
    return (loss, dx[None], *[out["grad"][n] for n in _WEIGHTS], *[out["delta"][n] for n in _WEIGHTS],
            *[out["new_m"][n] for n in _WEIGHTS], *[out["new_v"][n] for n in _WEIGHTS])
```

```python
import functools
import math

import jax
import jax.numpy as jnp
from jax import lax
from jax.experimental import pallas as pl
from jax.experimental.pallas import tpu as pltpu

F32 = jnp.float32
BF16 = jnp.bfloat16

LANE = 128
SUBLANE_BF16 = 16
VMEM_LIMIT = 48 * 1024 * 1024
N_DEV = 8
MESH = pl.DeviceIdType.MESH
ANY = pl.BlockSpec(memory_space=pl.ANY)

RMS_EPS = 1e-6
HEAD_DIM = 64
N_SB_HEADS = 8
N_FOX_HEADS = 8
N_MEM_HEADS = 4
NEG = -1e30
ATT_TQ = 512
ATT_TK = 256
DECAY_TK = 128
IN_TILE = 768

ADAM_LR = 0.001
ADAM_B1 = 0.9
ADAM_B2 = 0.999
ADAM_EPS = 1e-08
ADAM_WD = 0.01
ADAM_STEP = 10


def _tile(n, target, mult=LANE):
    best = None
    for t in range(mult, min(n, target) + 1, mult):
        if n % t == 0:
            best = t
    return best if best is not None else n


def _cparams(sem):
    return pltpu.CompilerParams(dimension_semantics=sem, vmem_limit_bytes=VMEM_LIMIT)


_DIMS = {"nn": (((1,), (0,)), ((), ())), "nt": (((1,), (1,)), ((), ())), "tn": (((0,), (0,)), ((), ()))}


def _dot(a, b, mode="nn"):
    return lax.dot_general(a.astype(BF16), b.astype(BF16), _DIMS[mode], preferred_element_type=F32)


def _mm(name, pairs, mode, out_dtypes, epilogue=None, extras=(), tm=512, tn=512):
    a0, b0 = pairs[0]
    M = a0.shape[1] if mode == "tn" else a0.shape[0]
    N = b0.shape[0] if mode == "nt" else b0.shape[1]
    tm = _tile(M, tm)
    tn = _tile(N, tn)
    np_, ne, no = len(pairs), len(extras), len(out_dtypes)

    def body(*refs):
        a_refs, b_refs = refs[:np_], refs[np_:2 * np_]
        e_refs = refs[2 * np_:2 * np_ + ne]
        o_refs = refs[2 * np_ + ne:]
        accs = [_dot(a[...], b[...], mode) for a, b in zip(a_refs, b_refs)]
        outs = epilogue(accs, [e[...] for e in e_refs]) if epilogue is not None else accs
        for o, val in zip(o_refs, outs):
            o[...] = val.astype(o.dtype)

    in_specs = []
    for a, _ in pairs:
        if mode == "tn":
            in_specs.append(pl.BlockSpec((a.shape[0], tm), lambda j, i: (0, i)))
        else:
            in_specs.append(pl.BlockSpec((tm, a.shape[1]), lambda j, i: (i, 0)))
    for _, b in pairs:
        if mode == "nt":
            in_specs.append(pl.BlockSpec((tn, b.shape[1]), lambda j, i: (j, 0)))
        else:
            in_specs.append(pl.BlockSpec((b.shape[0], tn), lambda j, i: (0, j)))
    for e, off in extras:
        if e.shape[0] == 1:
            in_specs.append(pl.BlockSpec((1, tn), functools.partial(lambda j, i, o: (0, j + o), o=off // tn)))
        else:
            in_specs.append(pl.BlockSpec((tm, tn), functools.partial(lambda j, i, o: (i, j + o), o=off // tn)))
    out_specs = [pl.BlockSpec((tm, tn), lambda j, i: (i, j)) for _ in range(no)]
    outs = pl.pallas_call(
        body, name=name, grid=(N // tn, M // tm),
        in_specs=in_specs, out_specs=out_specs,
        out_shape=[jax.ShapeDtypeStruct((M, N), dt) for dt in out_dtypes],
        compiler_params=_cparams(("parallel", "parallel")),
    )(*[a for a, _ in pairs], *[b for _, b in pairs], *[e for e, _ in extras])
    return outs[0] if no == 1 else outs


def _sum_accs(accs, _):
    total = accs[0]
    for acc in accs[1:]:
        total = total + acc
    return [total]


def _rstd(x):
    return lax.rsqrt(jnp.mean(x * x, axis=-1, keepdims=True) + RMS_EPS)


def _rms_fwd(name, x, g, out_dtype, res=None, scale=1.0, tr=512):
    R, D = x.shape
    tr = _tile(R, tr, 8)
    has_res = res is not None

    def body(*refs):
        x_ref, g_ref = refs[:2]
        o_ref = refs[-1]
        xv = x_ref[...]
        y = (xv * _rstd(xv)) * g_ref[...]
        if has_res:
            y = refs[2][...] + scale * y
        o_ref[...] = y.astype(o_ref.dtype)

    row = pl.BlockSpec((tr, D), lambda i: (i, 0))
    gain = pl.BlockSpec((1, D), lambda i: (0, 0))
    return pl.pallas_call(
        body, name=name, grid=(R // tr,),
        in_specs=[row, gain] + ([row] if has_res else []), out_specs=row,
        out_shape=jax.ShapeDtypeStruct((R, D), out_dtype),
        compiler_params=_cparams(("parallel",)),
    )(x, g.reshape(1, D), *([res] if has_res else []))


def _rms_bwd(name, x, g, dy, out_dtype, scale=1.0, res=None, tr=512):
    R, D = x.shape
    tr = _tile(R, tr, 8)
    has_res = res is not None

    def body(*refs):
        x_ref, g_ref, dy_ref = refs[:3]
        dx_ref, dg_ref = refs[-2:]
        i = pl.program_id(0)
        xv = x_ref[...]
        xhat = xv * _rstd(xv)
        dyv = dy_ref[...].astype(F32) * scale
        gy = dyv * g_ref[...]
        dx = _rstd(xv) * (gy - xhat * jnp.mean(gy * xhat, axis=-1, keepdims=True))
        if has_res:
            dx = refs[3][...] + dx
        dx_ref[...] = dx.astype(dx_ref.dtype)
        part = jnp.sum(dyv * xhat, axis=0, keepdims=True)

        @pl.when(i == 0)
        def _():
            dg_ref[...] = part

        @pl.when(i > 0)
        def _():
            dg_ref[...] += part

    row = pl.BlockSpec((tr, D), lambda i: (i, 0))
    gain = pl.BlockSpec((1, D), lambda i: (0, 0))
    dx, dg = pl.pallas_call(
        body, name=name, grid=(R // tr,),
        in_specs=[row, gain, row] + ([row] if has_res else []), out_specs=[row, gain],
        out_shape=[jax.ShapeDtypeStruct((R, D), out_dtype), jax.ShapeDtypeStruct((1, D), F32)],
        compiler_params=_cparams(("arbitrary",)),
    )(x, g.reshape(1, D), dy, *([res] if has_res else []))
    return dx, dg[0]


def _loss_grad(y, tgt, tr=512):
    R, D = y.shape
    tr = _tile(R, tr, 8)

    def body(y_ref, t_ref, dy_ref, loss_ref):
        i = pl.program_id(0)
        d = y_ref[...] - t_ref[...]
        dy_ref[...] = d / D
        part = 0.5 * jnp.sum(jnp.mean(d * d, axis=-1, keepdims=True), axis=0, keepdims=True)
        tile = jnp.broadcast_to(part, loss_ref.shape)

        @pl.when(i == 0)
        def _():
            loss_ref[...] = tile

        @pl.when(i > 0)
        def _():
            loss_ref[...] += tile

    row = pl.BlockSpec((tr, D), lambda i: (i, 0))
    dy, loss = pl.pallas_call(
        body, name="loss_grad", grid=(R // tr,),
        in_specs=[row, row], out_specs=[row, pl.BlockSpec((8, LANE), lambda i: (0, 0))],
        out_shape=[jax.ShapeDtypeStruct((R, D), F32), jax.ShapeDtypeStruct((8, LANE), F32)],
        compiler_params=_cparams(("arbitrary",)),
    )(y, tgt)
    return loss[0, 0], dy


def _colsum(name, x, tr=512, tn=1024):
    R, N = x.shape
    tr, tn = _tile(R, tr, 8), _tile(N, tn)

    def body(x_ref, o_ref):
        i = pl.program_id(1)
        part = jnp.sum(x_ref[...].astype(F32), axis=0, keepdims=True)

        @pl.when(i == 0)
        def _():
            o_ref[...] = part

        @pl.when(i > 0)
        def _():
            o_ref[...] += part

    out = pl.pallas_call(
        body, name=name, grid=(N // tn, R // tr),
        in_specs=[pl.BlockSpec((tr, tn), lambda j, i: (i, j))], out_specs=pl.BlockSpec((1, tn), lambda j, i: (0, j)),
        out_shape=jax.ShapeDtypeStruct((1, N), F32),
        compiler_params=_cparams(("parallel", "arbitrary")),
    )(x)
    return out[0]


def _tri(tk, rel):
    j = lax.broadcasted_iota(jnp.int32, (tk, tk), 0)
    s = lax.broadcasted_iota(jnp.int32, (tk, tk), 1)
    return rel(j, s).astype(BF16)


def _dot_split(x, m, parts=2):
    total = None
    rem = x
    for _ in range(parts):
        piece = rem.astype(BF16)
        rem = rem - piece.astype(F32)
        term = jnp.dot(piece, m, preferred_element_type=F32)
        total = term if total is None else total + term
    return total


def _log_not_and_beta(z, mask):
    ln = -(jnp.maximum(z, 0.0) + jnp.log(1.0 + jnp.exp(-jnp.abs(z))))
    return (ln if mask is None else jnp.where(mask, ln, 0.0)), ln + z


def _att_tiles(T, Tk, causal):
    tq = min(ATT_TQ, T)
    tk = min(ATT_TK, tq if causal else Tk)
    return tq, tk, (tq if causal else Tk) // tk


def _key_base(j, tq):
    return j * tq if isinstance(j, int) else pl.multiple_of(j * tq, tq)


def _is_pow2(scale):
    return math.log2(scale).is_integer()


def _per_head(x, hpb, d):
    if hpb == 1:
        return [x]
    lane = lax.broadcasted_iota(jnp.int32, x.shape, 1)
    return [jnp.where((lane >= h * d) & (lane < (h + 1) * d), x, jnp.zeros_like(x)) for h in range(hpb)]


def _join_heads(xs, d):
    out = xs[-1]
    if len(xs) > 1:
        lane = lax.broadcasted_iota(jnp.int32, out.shape, 1)
        for h in reversed(range(len(xs) - 1)):
            out = jnp.where(lane < (h + 1) * d, xs[h], out)
    return out


def _lane_tile(rows, off, whole):
    if whole:
        return pl.BlockSpec((rows, LANE), lambda g, i: (0, off + g))
    return pl.BlockSpec((rows, LANE), lambda g, i: (i, off + g))


def _sb_fwd(q, k, v, n_tiles, d, scale):
    T = q[0].shape[0]
    hpb = LANE // d
    tq, tk, nsub = _att_tiles(T, T, True)
    assert _is_pow2(scale)

    def body(q_ref, k_ref, v_ref, o_ref, ob_ref, rt_ref, acc_ref, r_ref):
        qi = pl.program_id(1)
        qh = _per_head(q_ref[...] * scale, hpb, d)
        acc_ref[...] = jnp.zeros_like(acc_ref)
        r_ref[...] = jnp.zeros_like(r_ref)
        row = lax.broadcasted_iota(jnp.int32, (tq, tk), 0)
        col = lax.broadcasted_iota(jnp.int32, (tq, tk), 1)
        after = _tri(tk, lambda j, s: j > s)

        def step(j, diagonal):
            base = _key_base(j, tq)
            for h in range(hpb):
                parts = []
                for u in reversed(range(nsub)):
                    z = _dot(qh[h], k_ref[pl.ds(base + u * tk, tk), :], "nt")
                    mask = (col + u * tk) < row if diagonal else None
                    ln, lb = _log_not_and_beta(z, mask)
                    between = _dot_split(ln, after)
                    parts.append((u, lb, between, between[:, 0:1] + ln[:, 0:1], mask))
                r = r_ref[h]
                out = None
                for u, lb, between, total, mask in parts:
                    w = jnp.exp(lb + between + r)
                    if diagonal:
                        w = jnp.where(mask, w, 0.0)
                    term = _dot(w, v_ref[pl.ds(base + u * tk, tk), :])
                    out = term if out is None else out + term
                    r = r + total
                acc_ref[h] += out
                r_ref[h] = r

        def below(i, carry):
            step(qi - 1 - i, False)
            return carry

        step(qi, True)
        lax.fori_loop(0, qi, below, 0)
        o = _join_heads([acc_ref[h] for h in range(hpb)], d)
        o_ref[...] = o
        ob_ref[...] = o.astype(ob_ref.dtype)
        rt_ref[...] = r_ref[...]

    out = pl.BlockSpec((tq, LANE), lambda g, i: (i, g))
    col = pl.BlockSpec((hpb, tq, 1), lambda g, i: (g, i, 0))
    return pl.pallas_call(
        body, name="sb_fwd", grid=(n_tiles, T // tq),
        in_specs=[_lane_tile(tq, q[1], False), _lane_tile(T, k[1], True), _lane_tile(T, v[1], True)],
        out_specs=[out, out, col],
        out_shape=[jax.ShapeDtypeStruct((T, n_tiles * LANE), F32), jax.ShapeDtypeStruct((T, n_tiles * LANE), BF16),
                   jax.ShapeDtypeStruct((n_tiles * hpb, T, 1), F32)],
        scratch_shapes=[pltpu.VMEM((hpb, tq, LANE), F32), pltpu.VMEM((hpb, tq, 1), F32)],
        compiler_params=_cparams(("parallel", "arbitrary")),
    )(q[0], k[0], v[0])


def _sb_bwd(q, k, v, do, rtot, n_tiles, d, scale):
    T = q[0].shape[0]
    hpb = LANE // d
    tq, tk, nsub = _att_tiles(T, T, True)
    assert _is_pow2(scale)

    def body(q_ref, k_ref, v_ref, do_ref, rt_ref, dq_ref, dk_ref, dv_ref, dk_acc, dv_acc, dq_acc, p_ref, c_ref):
        qi = pl.program_id(1)

        @pl.when(qi == 0)
        def _():
            dk_acc[...] = jnp.zeros_like(dk_acc)
            dv_acc[...] = jnp.zeros_like(dv_acc)

        qh = _per_head(q_ref[...] * scale, hpb, d)
        doh = _per_head(do_ref[...], hpb, d)
        dq_acc[...] = jnp.zeros_like(dq_acc)
        p_ref[...] = jnp.zeros_like(p_ref)
        c_ref[...] = jnp.zeros_like(c_ref)
        row = lax.broadcasted_iota(jnp.int32, (tq, tk), 0)
        col = lax.broadcasted_iota(jnp.int32, (tq, tk), 1)
        upto = _tri(tk, lambda j, s: j <= s)
        before = _tri(tk, lambda j, s: j < s)

        def step(j, diagonal):
            base = _key_base(j, tq)
            for h in range(hpb):
                first = []
                for u in range(nsub):
                    ks = base + u * tk
                    kv = k_ref[pl.ds(ks, tk), :]
                    z = _dot(qh[h], kv, "nt")
                    mask = (col + u * tk) < row if diagonal else None
                    ln, lb = _log_not_and_beta(z, mask)
                    dw = _dot(doh[h], v_ref[pl.ds(ks, tk), :], "nt")
                    first.append((ks, kv, mask, lb, jnp.exp(lb), _dot_split(ln, upto), dw))
                rt, pre, cpre = rt_ref[h], p_ref[h], c_ref[h]
                dq = None
                for ks, kv, mask, lb, sig, local, dw in first:
                    prefix = local + pre
                    w = jnp.exp(lb + (rt - prefix))
                    if diagonal:
                        w = jnp.where(mask, w, 0.0)
                    g = dw * w
                    c = _dot_split(g, before) + cpre
                    dz = g * (1.0 - sig) - c * sig
                    if diagonal:
                        dz = jnp.where(mask, dz, 0.0)
                    term = _dot(dz, kv)
                    dq = term if dq is None else dq + term
                    dk_acc[pl.ds(ks, tk), :] += _dot(dz, qh[h], "tn")
                    dv_acc[pl.ds(ks, tk), :] += _dot(w, doh[h], "tn")
                    pre = prefix[:, tk - 1:tk]
                    cpre = c[:, tk - 1:tk] + g[:, tk - 1:tk]
                dq_acc[h] += dq
                p_ref[h] = pre
                c_ref[h] = cpre

        def below(j, carry):
            step(j, False)
            return carry

        lax.fori_loop(0, qi, below, 0)
        step(qi, True)
        dq_ref[...] = (_join_heads([dq_acc[h] for h in range(hpb)], d) * scale).astype(dq_ref.dtype)

        @pl.when(qi == pl.num_programs(1) - 1)
        def _():
            dk_ref[...] = dk_acc[...].astype(dk_ref.dtype)
            dv_ref[...] = dv_acc[...].astype(dv_ref.dtype)

    blk = pl.BlockSpec((tq, LANE), lambda g, i: (i, g))
    full = pl.BlockSpec((T, LANE), lambda g, i: (0, g))
    col = pl.BlockSpec((hpb, tq, 1), lambda g, i: (g, i, 0))
    wide = jax.ShapeDtypeStruct((T, n_tiles * LANE), BF16)
    return pl.pallas_call(
        body, name="sb_bwd", grid=(n_tiles, T // tq),
        in_specs=[_lane_tile(tq, q[1], False), _lane_tile(T, k[1], True), _lane_tile(T, v[1], True), blk, col],
        out_specs=[blk, full, full], out_shape=[wide, wide, wide],
        scratch_shapes=[pltpu.VMEM((T, LANE), F32), pltpu.VMEM((T, LANE), F32), pltpu.VMEM((hpb, tq, LANE), F32),
                        pltpu.VMEM((hpb, tq, 1), F32), pltpu.VMEM((hpb, tq, 1), F32)],
        compiler_params=_cparams(("parallel", "arbitrary")),
    )(q[0], k[0], v[0], do, rtot)


def _attn_fwd(name, q, k, v, n_tiles, d, scale, c=None):
    T, Tk = q[0].shape[0], k[0].shape[0]
    hpb = LANE // d
    H = n_tiles * hpb
    causal = c is not None
    tq, tk, nsub = _att_tiles(T, Tk, causal)
    fold = _is_pow2(scale)

    def body(*refs):
        q_ref, k_ref, v_ref = refs[:3]
        cc_ref, cr_ref = refs[3:5] if causal else (None, None)
        o_ref, ob_ref, lse_ref, m_ref, l_ref, acc_ref = refs[-6:]
        qi = pl.program_id(1)
        qh = _per_head(q_ref[...] * scale if fold else q_ref[...], hpb, d)
        m_ref[...] = jnp.full_like(m_ref, NEG)
        l_ref[...] = jnp.zeros_like(l_ref)
        acc_ref[...] = jnp.zeros_like(acc_ref)
        row = lax.broadcasted_iota(jnp.int32, (tq, tk), 0)
        col = lax.broadcasted_iota(jnp.int32, (tq, tk), 1)

        def step(j, diagonal):
            base = _key_base(j, tq)
            for h in range(hpb):
                zs = []
                for u in range(nsub):
                    z = _dot(qh[h], k_ref[pl.ds(base + u * tk, tk), :], "nt")
                    if not fold:
                        z = z * scale
                    if causal:
                        z = z + cc_ref[h] - cr_ref[h, j * nsub + u]
                    if diagonal:
                        z = jnp.where((col + u * tk) <= row, z, NEG)
                    zs.append(z)
                m_prev = m_ref[h]
                m_new = m_prev
                for z in zs:
                    m_new = jnp.maximum(m_new, jnp.max(z, axis=1, keepdims=True))
                alpha = jnp.exp(m_prev - m_new)
                l_new = alpha * l_ref[h]
                out = alpha * acc_ref[h]
                for u, z in enumerate(zs):
                    p = jnp.exp(z - m_new)
                    l_new = l_new + jnp.sum(p, axis=1, keepdims=True)
                    out = out + _dot(p, v_ref[pl.ds(base + u * tk, tk), :])
                l_ref[h] = l_new
                acc_ref[h] = out
                m_ref[h] = m_new

        def below(j, carry):
            step(j, False)
            return carry

        if causal:
            lax.fori_loop(0, qi, below, 0)
            step(qi, True)
        else:
            step(0, False)
        o = _join_heads([acc_ref[h] / l_ref[h] for h in range(hpb)], d)
        o_ref[...] = o
        ob_ref[...] = o.astype(ob_ref.dtype)
        lse_ref[...] = m_ref[...] + jnp.log(l_ref[...])

    out = pl.BlockSpec((tq, LANE), lambda g, i: (i, g))
    col = pl.BlockSpec((hpb, tq, 1), lambda g, i: (g, i, 0))
    in_specs = [_lane_tile(tq, q[1], False), _lane_tile(Tk, k[1], True), _lane_tile(Tk, v[1], True)]
    args = [q[0], k[0], v[0]]
    if causal:
        in_specs += [col, pl.BlockSpec((hpb, T // tk, 1, tk), lambda g, i: (g, 0, 0, 0))]
        args += [c.reshape(H, T, 1), c.reshape(H, T // tk, 1, tk)]
    return pl.pallas_call(
        body, name=name, grid=(n_tiles, T // tq),
        in_specs=in_specs, out_specs=[out, out, col],
        out_shape=[jax.ShapeDtypeStruct((T, n_tiles * LANE), F32), jax.ShapeDtypeStruct((T, n_tiles * LANE), BF16),
                   jax.ShapeDtypeStruct((H, T, 1), F32)],
        scratch_shapes=[pltpu.VMEM((hpb, tq, 1), F32), pltpu.VMEM((hpb, tq, 1), F32),
                        pltpu.VMEM((hpb, tq, LANE), F32)],
        compiler_params=_cparams(("parallel", "arbitrary")),
    )(*args)


def _attn_bwd(name, q, k, v, o, do, lse, n_tiles, d, scale, c=None):
    T, Tk = q[0].shape[0], k[0].shape[0]
    hpb = LANE // d
    H = n_tiles * hpb
    causal = c is not None
    tq, tk, nsub = _att_tiles(T, Tk, causal)
    fold = _is_pow2(scale)

    def body(*refs):
        q_ref, k_ref, v_ref, o_ref, do_ref, lse_ref = refs[:6]
        cc_ref, cr_ref = refs[6:8] if causal else (None, None)
        n_out = 5 if causal else 3
        outs = refs[-(n_out + 3):-3]
        dq_ref, dk_ref, dv_ref = outs[:3]
        dc_ref, drow_ref = outs[3:5] if causal else (None, None)
        dk_acc, dv_acc, dq_acc = refs[-3:]
        qi = pl.program_id(1)

        @pl.when(qi == 0)
        def _():
            dk_acc[...] = jnp.zeros_like(dk_acc)
            dv_acc[...] = jnp.zeros_like(dv_acc)
            if causal:
                dc_ref[...] = jnp.zeros_like(dc_ref)

        qh = _per_head(q_ref[...] * scale if fold else q_ref[...], hpb, d)
        doh = _per_head(do_ref[...], hpb, d)
        deltas = [jnp.sum(t.astype(F32) * o_ref[...], axis=1, keepdims=True) for t in doh]
        dq_acc[...] = jnp.zeros_like(dq_acc)
        if causal:
            drow_ref[...] = jnp.zeros_like(drow_ref)
        row = lax.broadcasted_iota(jnp.int32, (tq, tk), 0)
        col = lax.broadcasted_iota(jnp.int32, (tq, tk), 1)

        def step(j, diagonal):
            base = _key_base(j, tq)
            for h in range(hpb):
                lse_v, delta = lse_ref[h], deltas[h]
                dq, drow = None, None
                for u in range(nsub):
                    ks = base + u * tk
                    kv = k_ref[pl.ds(ks, tk), :]
                    z = _dot(qh[h], kv, "nt")
                    if not fold:
                        z = z * scale
                    if causal:
                        z = z + cc_ref[h] - cr_ref[h, j * nsub + u]
                    if diagonal:
                        z = jnp.where((col + u * tk) <= row, z, NEG)
                    p = jnp.exp(z - lse_v)
                    ds = p * (_dot(doh[h], v_ref[pl.ds(ks, tk), :], "nt") - delta)
                    term = _dot(ds, kv)
                    dq = term if dq is None else dq + term
                    dk = _dot(ds, qh[h], "tn")
                    dk_acc[pl.ds(ks, tk), :] += dk if fold else dk * scale
                    dv_acc[pl.ds(ks, tk), :] += _dot(p, doh[h], "tn")
                    if causal:
                        dc_ref[h, j * nsub + u] -= jnp.sum(ds, axis=0, keepdims=True)
                        rs = jnp.sum(ds, axis=1, keepdims=True)
                        drow = rs if drow is None else drow + rs
                dq_acc[h] += dq
                if causal:
                    drow_ref[h] += drow

        def below(j, carry):
            step(j, False)
            return carry

        if causal:
            lax.fori_loop(0, qi, below, 0)
            step(qi, True)
        else:
            step(0, False)
        dq_ref[...] = (_join_heads([dq_acc[h] for h in range(hpb)], d) * scale).astype(dq_ref.dtype)

        @pl.when(qi == pl.num_programs(1) - 1)
        def _():
            dk_ref[...] = dk_acc[...].astype(dk_ref.dtype)
            dv_ref[...] = dv_acc[...].astype(dv_ref.dtype)

    blk = pl.BlockSpec((tq, LANE), lambda g, i: (i, g))
    full = pl.BlockSpec((Tk, LANE), lambda g, i: (0, g))
    col = pl.BlockSpec((hpb, tq, 1), lambda g, i: (g, i, 0))
    crow = pl.BlockSpec((hpb, T // tk, 1, tk), lambda g, i: (g, 0, 0, 0))
    in_specs = [_lane_tile(tq, q[1], False), _lane_tile(Tk, k[1], True), _lane_tile(Tk, v[1], True), blk, blk, col]
    args = [q[0], k[0], v[0], o, do, lse]
    out_specs = [blk, full, full]
    out_shape = [jax.ShapeDtypeStruct((T, n_tiles * LANE), BF16), jax.ShapeDtypeStruct((Tk, n_tiles * LANE), BF16),
                 jax.ShapeDtypeStruct((Tk, n_tiles * LANE), BF16)]
    if causal:
        in_specs += [col, crow]
        args += [c.reshape(H, T, 1), c.reshape(H, T // tk, 1, tk)]
        out_specs += [crow, col]
        out_shape += [jax.ShapeDtypeStruct((H, T // tk, 1, tk), F32), jax.ShapeDtypeStruct((H, T, 1), F32)]
    outs = pl.pallas_call(
        body, name=name, grid=(n_tiles, T // tq),
        in_specs=in_specs, out_specs=out_specs, out_shape=out_shape,
        scratch_shapes=[pltpu.VMEM((Tk, LANE), F32), pltpu.VMEM((Tk, LANE), F32), pltpu.VMEM((hpb, tq, LANE), F32)],
        compiler_params=_cparams(("parallel", "arbitrary")),
    )(*args)
    if causal:
        return outs[0], outs[1], outs[2], outs[3].reshape(H, T), outs[4].reshape(H, T)
    return outs


def _decay_fwd(fl, b):
    H, T = fl.shape
    tk = DECAY_TK

    def body(x_ref, b_ref, c_ref):
        upto = _tri(tk, lambda j, s: j <= s)
        carry = jnp.zeros((H, 1), F32)
        for i in range(T // tk):
            xv = x_ref[:, i * tk:(i + 1) * tk] + b_ref[...]
            lf = jnp.minimum(xv, 0.0) - jnp.log(1.0 + jnp.exp(-jnp.abs(xv)))
            pref = _dot_split(lf, upto, parts=3) + carry
            c_ref[:, i * tk:(i + 1) * tk] = pref
            carry = pref[:, tk - 1:tk]

    vm = pl.BlockSpec(memory_space=pltpu.VMEM)
    return pl.pallas_call(
        body, name="decay_fwd", in_specs=[vm, vm], out_specs=vm,
        out_shape=jax.ShapeDtypeStruct((H, T), F32),
    )(fl, b)


def _decay_bwd(dc_cols, dc_rows, fl, b):
    H, T = fl.shape
    tk = DECAY_TK

    def body(dc_ref, dr_ref, x_ref, b_ref, dx_ref, db_ref):
        from_ = _tri(tk, lambda j, s: j >= s)
        carry = jnp.zeros((H, 1), F32)
        total = jnp.zeros((H, 1), F32)
        for i in reversed(range(T // tk)):
            sl = slice(i * tk, (i + 1) * tk)
            suffix = _dot_split(dc_ref[:, sl] + dr_ref[:, sl], from_, parts=3) + carry
            xv = x_ref[:, sl] + b_ref[...]
            dx = suffix / (1.0 + jnp.exp(xv))
            dx_ref[:, sl] = dx
            total = total + jnp.sum(dx, axis=1, keepdims=True)
            carry = suffix[:, 0:1]
        db_ref[...] = jnp.broadcast_to(total, db_ref.shape)

    vm = pl.BlockSpec(memory_space=pltpu.VMEM)
    dx, db = pl.pallas_call(
        body, name="decay_bwd", in_specs=[vm, vm, vm, vm], out_specs=[vm, vm],
        out_shape=[jax.ShapeDtypeStruct((H, T), F32), jax.ShapeDtypeStruct((H, LANE), F32)],
    )(dc_cols, dc_rows, fl, b)
    return dx, db[:, 0]


def _place():
    x, y, c = lax.axis_index("x"), lax.axis_index("y"), lax.axis_index("c")
    return x, y, c, [(1 - x, y), (x, 1 - y), (1 - x, 1 - y)]


def _all_gather(name, block):
    R, C = block.shape

    def body(x_ref, out_ref, send_sems, recv_sems, local_sem):
        x, y, c, chips = _place()
        me, sibling = (x, y, c), (x, y, 1 - c)

        def rows(px, py, pc):
            return out_ref.at[4 * px + 2 * py + pc]

        def copy(k, blk, to, src=None):
            return pltpu.make_async_remote_copy(
                src_ref=rows(*blk) if src is None else src, dst_ref=rows(*blk),
                send_sem=send_sems.at[k], recv_sem=recv_sems.at[k], device_id=to, device_id_type=MESH)

        mine = pltpu.make_async_copy(x_ref, rows(*me), local_sem)
        mine.start()
        first = [copy(0, me, sibling, src=x_ref)]
        first += [copy(1 + j, me, (*chip, c), src=x_ref) for j, chip in enumerate(chips)]
        for cp in first:
            cp.start()
        passed = [copy(4 + j, (*chip, c), sibling) for j, chip in enumerate(chips)]
        for j, chip in enumerate(chips):
            copy(1 + j, (*chip, c), me).wait_recv()
            passed[j].start()
        copy(0, sibling, me).wait_recv()
        for j, chip in enumerate(chips):
            copy(4 + j, (*chip, 1 - c), me).wait_recv()
        for cp in first + passed:
            cp.wait_send()
        mine.wait()

    return pl.pallas_call(
        body, name=name, in_specs=[ANY], out_specs=ANY,
        out_shape=jax.ShapeDtypeStruct((N_DEV, R, C), block.dtype),
        scratch_shapes=[pltpu.SemaphoreType.DMA((7,)), pltpu.SemaphoreType.DMA((7,)), pltpu.SemaphoreType.DMA(())],
    )(block)


def _swap_with_sibling(name, parts):
    _, R, C = parts.shape

    def body(p_ref, out_ref, send_sems, recv_sems):
        x, y, c, _ = _place()
        copies = [pltpu.make_async_remote_copy(
            src_ref=p_ref.at[2 * q + (1 - c)], dst_ref=out_ref.at[q],
            send_sem=send_sems.at[q], recv_sem=recv_sems.at[q], device_id=(x, y, 1 - c), device_id_type=MESH)
            for q in range(4)]
        for cp in copies:
            cp.start()
        for cp in copies:
            cp.wait_recv()
        for cp in copies:
            cp.wait_send()

    return pl.pallas_call(
        body, name=name, in_specs=[ANY], out_specs=ANY,
        out_shape=jax.ShapeDtypeStruct((4, R, C), parts.dtype),
        scratch_shapes=[pltpu.SemaphoreType.DMA((4,)), pltpu.SemaphoreType.DMA((4,))],
    )(parts)


def _add_own(name, parts, got, tr=512):
    _, R, C = parts.shape
    tr = _tile(R, tr, SUBLANE_BF16)

    def body(c_ref, p_ref, g_ref, o_ref):
        o_ref[...] = (p_ref[...].astype(F32) + g_ref[...].astype(F32)).astype(o_ref.dtype)

    return pl.pallas_call(
        body, name=name,
        grid_spec=pltpu.PrefetchScalarGridSpec(
            num_scalar_prefetch=1, grid=(4, R // tr),
            in_specs=[pl.BlockSpec((1, tr, C), lambda q, i, c: (2 * q + c[0], i, 0)),
                      pl.BlockSpec((1, tr, C), lambda q, i, c: (q, i, 0))],
            out_specs=pl.BlockSpec((1, tr, C), lambda q, i, c: (q, i, 0))),
        out_shape=jax.ShapeDtypeStruct((4, R, C), parts.dtype),
        compiler_params=_cparams(("parallel", "parallel")),
    )(lax.axis_index("c").astype(jnp.int32).reshape(1), parts, got)


def _swap_with_chips(name, parts):
    _, R, C = parts.shape

    def body(p_ref, out_ref, send_sems, recv_sems, local_sem):
        x, y, c, chips = _place()
        my_chip = 2 * x + y
        mine = pltpu.make_async_copy(p_ref.at[my_chip], out_ref.at[my_chip], local_sem)
        mine.start()
        sends = [pltpu.make_async_remote_copy(
            src_ref=p_ref.at[2 * cx + cy], dst_ref=out_ref.at[my_chip],
            send_sem=send_sems.at[j], recv_sem=recv_sems.at[j], device_id=(cx, cy, c), device_id_type=MESH)
            for j, (cx, cy) in enumerate(chips)]
        for cp in sends:
            cp.start()
        for j, (cx, cy) in enumerate(chips):
            pltpu.make_async_remote_copy(
                src_ref=p_ref.at[my_chip], dst_ref=out_ref.at[2 * cx + cy],
                send_sem=send_sems.at[j], recv_sem=recv_sems.at[j], device_id=(cx, cy, c), device_id_type=MESH,
            ).wait_recv()
        for cp in sends:
            cp.wait_send()
        mine.wait()

    return pl.pallas_call(
        body, name=name, in_specs=[ANY], out_specs=ANY,
        out_shape=jax.ShapeDtypeStruct((4, R, C), parts.dtype),
        scratch_shapes=[pltpu.SemaphoreType.DMA((3,)), pltpu.SemaphoreType.DMA((3,)), pltpu.SemaphoreType.DMA(())],
    )(parts)


def _sum_parts(name, parts, tr=512):
    P, R, C = parts.shape
    tr = _tile(R, tr, SUBLANE_BF16)

    def body(p_ref, o_ref):
        total = p_ref[0].astype(F32)
        for p in range(1, P):
            total = total + p_ref[p].astype(F32)
        o_ref[...] = total

    return pl.pallas_call(
        body, name=name, grid=(R // tr,),
        in_specs=[pl.BlockSpec((P, tr, C), lambda i: (0, i, 0))], out_specs=pl.BlockSpec((tr, C), lambda i: (i, 0)),
        out_shape=jax.ShapeDtypeStruct((R, C), F32),
        compiler_params=_cparams(("parallel",)),
    )(parts)


_HBM = pl.BlockSpec(memory_space=pltpu.HBM)
_SEM = pl.BlockSpec(memory_space=pltpu.SEMAPHORE)
_EFFECT = pltpu.SideEffectType.DATAFLOW_SIDE_EFFECTING


def _flipped(x, y, c, k):
    px, py, pc = (1 - x if k & 4 else x), (1 - y if k & 2 else y), (1 - c if k & 1 else c)
    return (px, py, pc), 4 * px + 2 * py + pc


def _exchange_start(name, src, per_peer):
    R, C = src.shape[-2:]

    def body(v_ref, land_ref, send_sem, recv_sem, v_thru, land_thru, token):
        x, y, c = lax.axis_index("x"), lax.axis_index("y"), lax.axis_index("c")
        me = 4 * x + 2 * y + c
        for k in range(1, N_DEV):
            peer, idx = _flipped(x, y, c, k)
            pltpu.make_async_remote_copy(
                src_ref=v_ref.at[idx] if per_peer else v_ref, dst_ref=land_ref.at[me],
                send_sem=send_sem, recv_sem=recv_sem, device_id=peer, device_id_type=MESH).start()
        token[...] = jnp.zeros_like(token)

    return pl.pallas_call(
        body, name=name,
        out_shape=(pltpu.SemaphoreType.DMA(()), pltpu.SemaphoreType.DMA(()), pltpu.HBM(src.shape, src.dtype),
                   pltpu.HBM((N_DEV, R, C), src.dtype), jax.ShapeDtypeStruct((8, LANE), F32)),
        in_specs=(_HBM, _HBM), out_specs=(_SEM, _SEM, _HBM, _HBM, pl.BlockSpec(memory_space=pltpu.VMEM)),
        input_output_aliases={0: 2, 1: 3},
        compiler_params=pltpu.CompilerParams(has_side_effects=_EFFECT),
    )(pltpu.with_memory_space_constraint(src, pltpu.HBM),
      pltpu.with_memory_space_constraint(lax.empty((N_DEV, R, C), src.dtype), pltpu.HBM))


def _exchange_wait(name, started, after):
    send_sem, recv_sem, v_thru, land_thru, _ = started

    def body(v_ref, land_ref, send_sem, recv_sem, after_ref, v_dead, got_ref):
        x, y, c = lax.axis_index("x"), lax.axis_index("y"), lax.axis_index("c")
        seven = land_ref.at[pl.ds(0, N_DEV - 1)]
        drain = pltpu.make_async_remote_copy(
            src_ref=seven, dst_ref=seven, send_sem=send_sem, recv_sem=recv_sem,
            device_id=(x, y, c), device_id_type=MESH)
        drain.wait_send()
        drain.wait_recv()

    return pl.pallas_call(
        body, name=name,
        out_shape=(pltpu.HBM(v_thru.shape, v_thru.dtype), pltpu.HBM(land_thru.shape, land_thru.dtype)),
        in_specs=(_HBM, _HBM, _SEM, _SEM, ANY), out_specs=(_HBM, _HBM), input_output_aliases={0: 0, 1: 1},
        compiler_params=pltpu.CompilerParams(has_side_effects=_EFFECT),
    )(v_thru, land_thru, send_sem, recv_sem, after)[1]


def _my_index():
    return 4 * lax.axis_index("x") + 2 * lax.axis_index("y") + lax.axis_index("c")


def _sum_landed(name, landed, parts, tr=512):
    P, R, C = landed.shape
    tr = _tile(R, tr, SUBLANE_BF16)

    def body(me_ref, l_ref, own_ref, o_ref):
        total = None
        for s in range(P):
            part = jnp.where(me_ref[0] == s, own_ref[0], l_ref[s]).astype(F32)
            total = part if total is None else total + part
        o_ref[...] = total

    return pl.pallas_call(
        body, name=name,
        grid_spec=pltpu.PrefetchScalarGridSpec(
            num_scalar_prefetch=1, grid=(R // tr,),
            in_specs=[pl.BlockSpec((P, tr, C), lambda i, me: (0, i, 0)),
                      pl.BlockSpec((1, tr, C), lambda i, me: (me[0], i, 0))],
            out_specs=pl.BlockSpec((tr, C), lambda i, me: (i, 0))),
        out_shape=jax.ShapeDtypeStruct((R, C), F32),
        compiler_params=_cparams(("parallel",)),
    )(_my_index().astype(jnp.int32).reshape(1), landed, parts)


def _after(params, name, token):
    return {**params, name: params[name] + token[0, 0]}


def _reduce_scatter(tag, parts):
    got = _swap_with_sibling("rs_pair_" + tag, parts)
    pair = _add_own("rs_add_" + tag, parts, got)
    quad = _swap_with_chips("rs_chips_" + tag, pair)
    return _sum_parts("rs_sum_" + tag, quad)


def _adamw(name, g_parts, w, m, v, tr=512):
    P, R, C = g_parts.shape
    tr = _tile(R, tr, 8)

    def body(g_ref, w_ref, m_ref, v_ref, go_ref, d_ref, mo_ref, vo_ref):
        g = g_ref[0]
        for p in range(1, P):
            g = g + g_ref[p]
        mn = ADAM_B1 * m_ref[...] + (1.0 - ADAM_B1) * g
        vn = ADAM_B2 * v_ref[...] + (1.0 - ADAM_B2) * (g * g)
        m_hat = mn / (1.0 - ADAM_B1 ** ADAM_STEP)
        v_hat = vn / (1.0 - ADAM_B2 ** ADAM_STEP)
        go_ref[...] = g
        d_ref[...] = -ADAM_LR * (m_hat / (jnp.sqrt(v_hat) + ADAM_EPS) + ADAM_WD * w_ref[...])
        mo_ref[...] = mn
        vo_ref[...] = vn

    row = pl.BlockSpec((tr, C), lambda i: (i, 0))
    return pl.pallas_call(
        body, name=name, grid=(R // tr,),
        in_specs=[pl.BlockSpec((P, tr, C), lambda i: (0, i, 0)), row, row, row], out_specs=[row] * 4,
        out_shape=[jax.ShapeDtypeStruct((R, C), F32)] * 4,
        compiler_params=_cparams(("parallel",)),
    )(g_parts, w, m, v)


def _pad_rows(t, rows):
    return jnp.pad(t, ((0, rows - t.shape[0]), (0, 0)))


class _Layout:
    def __init__(self, D, ff_shard, in_shard, kv_shard, gate_shard, br_in, br_shard, out_shard):
        self.D = D
        self.in_shard = in_shard
        self.in_pad = -(-in_shard // LANE) * LANE
        self.in_cols = -(-N_DEV * in_shard // IN_TILE) * IN_TILE
        self.br_in, self.br_shard = br_in, br_shard
        br_rows = br_shard * br_in // D
        sizes = [("g1", ff_shard), ("u1", ff_shard), ("d1", ff_shard), ("win", self.in_pad), ("kv", kv_shard),
                 ("gate", gate_shard), ("br", br_rows), ("out", out_shard),
                 ("g2", ff_shard), ("u2", ff_shard), ("d2", ff_shard)]
        self.seg, off = {}, 0
        for key, n in sizes:
            assert n % SUBLANE_BF16 == 0, (key, n)
            self.seg[key] = (off, n)
            off += n
        self.rows = off

    def pack(self, parts):
        return jnp.concatenate([parts[key] for key in self.seg], axis=0)

    def take(self, gathered, key, own=None):
        off, n = self.seg[key]
        seg = gathered[:, off:off + n, :]
        if own is not None:
            seg = lax.dynamic_update_slice(seg, own[0][off:off + n][None], (own[1], 0, 0))
        return seg.reshape(N_DEV * n, self.D)

    def spread(self, full, key):
        _, n = self.seg[key]
        return full.reshape(N_DEV, n, self.D)


def _pack_layer(lay, l, p):
    D = lay.D
    br = jnp.concatenate([p["w_br_sb"][l], p["w_br_fox"][l], p["w_br_mem"][l]], axis=0)
    parts = {
        "g1": p["ffn1_w_gate"][l].T, "u1": p["ffn1_w_up"][l].T, "d1": p["ffn1_w_down"][l],
        "win": _pad_rows(p["w_in"][l].T, lay.in_pad), "kv": p["w_mem_kv"][l], "gate": p["w_gate"][l].T,
        "br": br.T.reshape(-1, D), "out": p["w_out"][l],
        "g2": p["ffn2_w_gate"][l].T, "u2": p["ffn2_w_up"][l].T, "d2": p["ffn2_w_down"][l],
    }
    return lay.pack({k: t.astype(BF16) for k, t in parts.items()})


def _align_win(lay, packed):
    D = lay.D
    real = packed.reshape(N_DEV, lay.in_pad, D)[:, :lay.in_shard].reshape(N_DEV * lay.in_shard, D)
    rows = jnp.concatenate([real[:_QKV_W], real[_QKV_W + N_FOX_HEADS:], real[_QKV_W:_QKV_W + N_FOX_HEADS]], axis=0)
    return _pad_rows(rows, lay.in_cols)


def _unalign_win(lay, aligned):
    D = lay.D
    n_real = N_DEV * lay.in_shard
    mem_w = n_real - _QKV_W - N_FOX_HEADS
    real = jnp.concatenate([aligned[:_QKV_W], aligned[_QKV_W + mem_w:n_real], aligned[_QKV_W:_QKV_W + mem_w]], axis=0)
    real = real.reshape(N_DEV, lay.in_shard, D)
    return jnp.pad(real, ((0, 0), (0, lay.in_pad - lay.in_shard), (0, 0))).reshape(N_DEV * lay.in_pad, D)


def _unpack_layer(lay, gathered, own=None):
    D = lay.D
    w = {k: lay.take(gathered, k, own) for k in ("g1", "u1", "d1", "kv", "out", "g2", "u2", "d2")}
    w["win"] = _align_win(lay, lay.take(gathered, "win", own))
    fl0 = N_DEV * lay.in_shard - N_FOX_HEADS
    w["wfl"] = w["win"][fl0:fl0 + LANE]
    gate = lay.take(gathered, "gate", own)
    w["gate"] = gate
    w["gate3"] = [gate[i * D:(i + 1) * D] for i in range(3)]
    br = lay.take(gathered, "br", own).reshape(N_DEV * lay.br_shard, lay.br_in)
    third = lay.br_in // 3
    w["br3"] = [br[:, i * third:(i + 1) * third] for i in range(3)]
    return w


def _silu_mul(accs, _):
    a, b = accs
    return [a, b, a * jax.nn.sigmoid(a) * b]


def _act_bwd(accs, extras):
    ds, (a, b) = accs[0], extras
    sig = jax.nn.sigmoid(a)
    return [ds * b * (sig * (1.0 + a * (1.0 - sig))), ds * (a * sig)]


def _ffn_fwd(tag, h, pre_g, post_g, wg, wu, wd):
    n = _rms_fwd("ffn_norm_" + tag, h, pre_g, BF16)
    a, b, s = _mm("ffn_up_" + tag, [(n, wg), (n, wu)], "nt", [F32, F32, BF16], _silu_mul, tm=256, tn=1408)
    f = _mm("ffn_down_" + tag, [(s, wd)], "nn", [F32], tm=256)
    out = _rms_fwd("ffn_out_" + tag, f, post_g, F32, res=h, scale=0.5)
    return out, (h, n, a, b, s, f)


def _ffn_bwd(tag, dh, saved, pre_g, post_g, wg, wu, wd):
    h, n, a, b, s, f = saved
    df, d_post = _rms_bwd("ffn_dout_" + tag, f, post_g, dh, BF16, scale=0.5)
    da, db = _mm("ffn_dact_" + tag, [(df, wd)], "nt", [BF16, BF16], _act_bwd, [(a, 0), (b, 0)], tm=256, tn=1408)
    d_wd = _mm("ffn_dwd_" + tag, [(s, df)], "tn", [BF16], tm=256)
    dn = _mm("ffn_dn_" + tag, [(da, wg), (db, wu)], "nn", [F32], _sum_accs, tm=256)
    d_wg = _mm("ffn_dwg_" + tag, [(da, n)], "tn", [BF16], tm=256)
    d_wu = _mm("ffn_dwu_" + tag, [(db, n)], "tn", [BF16], tm=256)
    dh_in, d_pre = _rms_bwd("ffn_dnorm_" + tag, h, pre_g, dn, F32, res=dh)
    return dh_in, d_pre, d_post, d_wg, d_wu, d_wd


_SB_W = N_SB_HEADS * HEAD_DIM
_FOX_W = N_FOX_HEADS * HEAD_DIM
_QKV_W = 3 * _SB_W + 3 * _FOX_W


def _gate_act(accs, extras):
    return [jax.nn.sigmoid(accs[0] + extras[0])]


def _merge(accs, extras):
    return [extras[0] * accs[0] + extras[1] * accs[1] + extras[2] * accs[2]]


def _merge_bwd(accs, extras):
    dm = accs[0]
    d_branch = [dm * gi for gi in extras]
    d_gate = [dm * bi * gi * (1.0 - gi) for bi, gi in zip(accs[1:], extras)]
    return d_branch + d_gate


def _mix_tiles(lay):
    sb, fx = _SB_W // LANE, _FOX_W // LANE
    mem_w = N_DEV * lay.in_shard - _QKV_W - N_FOX_HEADS
    return (0, sb, 2 * sb, sb), (3 * sb, 3 * sb + fx, 3 * sb + 2 * fx, fx), (_QKV_W // LANE, mem_w // LANE)


def _mix_fwd(lay, h, w, pre_g, post_g, b_forget, b_gate, mem_n):
    D = lay.D
    (sq, sk, sv, sn), (fq, fk, fv, fn), (mq, mn) = _mix_tiles(lay)
    mem_d = mn * LANE // N_MEM_HEADS
    u = _rms_fwd("mix_norm", h, pre_g, BF16)
    proj = _mm("mix_in", [(u, w["win"])], "nt", [BF16], tn=IN_TILE)
    fl = _mm("mix_fl", [(u, w["wfl"])], "nt", [F32])[:, :N_FOX_HEADS].T
    c = _decay_fwd(fl, b_forget.reshape(-1, 1))
    _, o_sb, rtot = _sb_fwd((proj, sq), (proj, sk), (proj, sv), sn, HEAD_DIM, HEAD_DIM ** -0.5)
    o_fx32, o_fx, lse_fx = _attn_fwd("fox_fwd", (proj, fq), (proj, fk), (proj, fv), fn, HEAD_DIM,
                                     HEAD_DIM ** -0.5, c)
    kvm = _mm("mem_kv", [(mem_n, w["kv"])], "nn", [BF16])
    o_mem32, o_mem, lse_mem = _attn_fwd("mem_fwd", (proj, mq), (kvm, 0), (kvm, mn), mn, mem_d, mem_d ** -0.5)
    gates = _mm("mix_gate", [(u, w["gate"])], "nt", [F32], _gate_act, [(b_gate.reshape(1, -1), 0)])
    flat = [o_sb, o_fx, o_mem]
    merged = _mm("mix_merge", list(zip(flat, w["br3"])), "nt", [BF16], _merge,
                 [(gates, 0), (gates, D), (gates, 2 * D)])
    z = _mm("mix_out", [(merged, w["out"])], "nn", [F32])
    out = _rms_fwd("mix_res", z, post_g, F32, res=h)
    saved = (h, u, proj, fl, c, rtot, o_fx32, lse_fx, kvm, o_mem32, lse_mem, gates, flat, merged, z)
    return out, saved


def _mix_bwd(lay, dh, saved, w, pre_g, post_g, b_forget, mem_n, dmem_n):
    D = lay.D
    (sq, sk, sv, sn), (fq, fk, fv, fn), (mq, mn) = _mix_tiles(lay)
    mem_d = mn * LANE // N_MEM_HEADS
    h, u, proj, fl, c, rtot, o_fx32, lse_fx, kvm, o_mem32, lse_mem, gates, flat, merged, z = saved
    dz, d_post = _rms_bwd("mix_dres", z, post_g, dh, BF16)
    outs = _mm("mix_dmerge", [(dz, w["out"])] + list(zip(flat, w["br3"])), "nt", [BF16] * 6, _merge_bwd,
               [(gates, 0), (gates, D), (gates, 2 * D)])
    d_branch, d_gate = outs[:3], outs[3:]
    d_wout = _mm("mix_dwout", [(merged, dz)], "tn", [BF16])
    d_o = [_mm("mix_dbr%d" % i, [(d_branch[i], w["br3"][i])], "nn", [BF16]) for i in range(3)]
    d_wbr = [_mm("mix_dwbr%d" % i, [(d_branch[i], flat[i])], "tn", [BF16]) for i in range(3)]
    d_bgate = jnp.concatenate([_colsum("mix_dbgate%d" % i, d_gate[i]) for i in range(3)])
    d_wgate = [_mm("mix_dwgate%d" % i, [(d_gate[i], u)], "tn", [BF16]) for i in range(3)]

    d_sb = _sb_bwd((proj, sq), (proj, sk), (proj, sv), d_o[0], rtot, sn, HEAD_DIM, HEAD_DIM ** -0.5)
    *d_fx, dc, dc_rows = _attn_bwd("fox_bwd", (proj, fq), (proj, fk), (proj, fv), o_fx32, d_o[1], lse_fx, fn,
                                   HEAD_DIM, HEAD_DIM ** -0.5, c)
    dq_m, dk_m, dv_m = _attn_bwd("mem_bwd", (proj, mq), (kvm, 0), (kvm, mn), o_mem32, d_o[2], lse_mem, mn,
                                 mem_d, mem_d ** -0.5)
    dfl, d_bforget = _decay_bwd(dc, dc_rows, fl, b_forget.reshape(-1, 1))
    pieces = list(d_sb) + list(d_fx) + [dq_m]
    dflp = jnp.pad(dfl.T.astype(BF16), ((0, 0), (0, LANE - dfl.shape[0])))
    offs = [sum(t.shape[1] for t in pieces[:i]) for i in range(len(pieces) + 1)]
    win_rows = [w["win"][offs[i]:offs[i + 1]] for i in range(len(pieces))]
    du = _mm("mix_du", list(zip(d_gate, w["gate3"])) + list(zip(pieces, win_rows)) + [(dflp, w["wfl"])], "nn",
             [F32], _sum_accs, tm=256)
    d_rows = [_mm("mix_dwin%d" % i, [(t, u)], "tn", [BF16]) for i, t in enumerate(pieces)]
    d_wfl = _mm("mix_dwfl", [(dflp, u)], "tn", [BF16])
    d_win = _unalign_win(lay, _pad_rows(jnp.concatenate(list(d_rows) + [d_wfl], axis=0), lay.in_cols))
    dh_in, d_pre = _rms_bwd("mix_dnorm", h, pre_g, du, F32, res=dh)

    dkvm = jnp.concatenate([dk_m, dv_m], axis=1)
    d_wkv = _mm("mem_dwkv", [(mem_n, dkvm)], "tn", [BF16])
    dmem_n = _mm("mem_dn", [(dkvm, w["kv"])], "nt", [F32], lambda accs, ex: [accs[0] + ex[0]], [(dmem_n, 0)])
    grads = {"win": d_win, "kv": d_wkv, "gate": jnp.concatenate(d_wgate, axis=0),
             "br": jnp.concatenate(d_wbr, axis=1), "out": d_wout}
    return dh_in, d_pre, d_post, d_bforget, d_bgate, grads, dmem_n


def _layer_fwd(lay, h, w, sp, mem_n):
    h1, s1 = _ffn_fwd("1", h, sp["ffn1_pre_g"], sp["ffn1_post_g"], w["g1"], w["u1"], w["d1"])
    h2, s2 = _mix_fwd(lay, h1, w, sp["mix_pre_g"], sp["mix_post_g"], sp["b_forget"], sp["b_gate"], mem_n)
    h3, s3 = _ffn_fwd("2", h2, sp["ffn2_pre_g"], sp["ffn2_post_g"], w["g2"], w["u2"], w["d2"])
    return h3, (s1, s2, s3)


def _layer_bwd(lay, dh, saved, w, sp, mem_n, dmem_n):
    s1, s2, s3 = saved
    dh, d_pre2, d_post2, d_g2, d_u2, d_d2 = _ffn_bwd("2", dh, s3, sp["ffn2_pre_g"], sp["ffn2_post_g"],
                                                     w["g2"], w["u2"], w["d2"])
    dh, d_mpre, d_mpost, d_bforget, d_bgate, g, dmem_n = _mix_bwd(
        lay, dh, s2, w, sp["mix_pre_g"], sp["mix_post_g"], sp["b_forget"], mem_n, dmem_n)
    dh, d_pre1, d_post1, d_g1, d_u1, d_d1 = _ffn_bwd("1", dh, s1, sp["ffn1_pre_g"], sp["ffn1_post_g"],
                                                     w["g1"], w["u1"], w["d1"])
    g.update({"g1": d_g1, "u1": d_u1, "d1": d_d1, "g2": d_g2, "u2": d_u2, "d2": d_d2})
    g["br"] = g["br"].reshape(N_DEV, lay.br_shard, lay.br_in).reshape(-1, lay.D)
    packed = jnp.concatenate([lay.spread(g[key], key) for key in lay.seg], axis=1)
    small = {"ffn1_pre_g": d_pre1, "ffn1_post_g": d_post1, "mix_pre_g": d_mpre, "mix_post_g": d_mpost,
             "ffn2_pre_g": d_pre2, "ffn2_post_g": d_post2, "b_gate": d_bgate, "b_forget": d_bforget}
    return dh, packed, small, dmem_n


_SHARDED = ["ffn1_w_gate", "ffn1_w_up", "ffn1_w_down", "w_in", "w_mem_kv", "w_gate", "w_br_sb", "w_br_fox",
            "w_br_mem", "w_out", "ffn2_w_gate", "ffn2_w_up", "ffn2_w_down"]
_SMALL_LAYER = ["ffn1_pre_g", "ffn1_post_g", "mix_pre_g", "mix_post_g", "ffn2_pre_g", "ffn2_post_g", "b_gate",
                "b_forget"]
_WEIGHTS = ["ffn1_pre_g", "ffn1_post_g", "ffn1_w_gate", "ffn1_w_up", "ffn1_w_down", "mix_pre_g", "mix_post_g",
            "w_in", "b_forget", "mem_norm_g", "w_mem_kv", "w_gate", "b_gate", "w_br_sb", "w_br_fox", "w_br_mem",
            "w_out", "ffn2_pre_g", "ffn2_post_g", "ffn2_w_gate", "ffn2_w_up", "ffn2_w_down"]


def _pack_small(vals, L, D):
    rows = []
    for l in range(L):
        for name in _SMALL_LAYER:
            t = vals[name][l]
            rows.append(jnp.pad(t, (0, -t.shape[0] % D)).reshape(-1, D))
    rows.append(vals["mem_norm_g"].reshape(1, D))
    packed = jnp.concatenate(rows, axis=0)
    return _pad_rows(packed, -(-packed.shape[0] // 8) * 8)


def _unpack_small(packed, shapes, L, D):
    out = {name: [] for name in _SMALL_LAYER}
    r = 0
    for l in range(L):
        for name in _SMALL_LAYER:
            n = shapes[name][1]
            nr = -(-n // D)
            out[name].append(packed[r:r + nr].reshape(-1)[:n])
            r += nr
    res = {name: jnp.stack(v) for name, v in out.items()}
    res["mem_norm_g"] = packed[r]
    return res


def _unpack_grads(lay, g, l_shapes):
    def seg(key):
        off, n = lay.seg[key]
        return g[off:off + n]
    br = seg("br").reshape(lay.br_shard, lay.br_in).T
    third = lay.br_in // 3
    return {
        "ffn1_w_gate": seg("g1").T, "ffn1_w_up": seg("u1").T, "ffn1_w_down": seg("d1"),
        "w_in": seg("win")[:lay.in_shard].T, "w_mem_kv": seg("kv"), "w_gate": seg("gate").T,
        "w_br_sb": br[:third], "w_br_fox": br[third:2 * third], "w_br_mem": br[2 * third:],
        "w_out": seg("out"), "ffn2_w_gate": seg("g2").T, "ffn2_w_up": seg("u2").T, "ffn2_w_down": seg("d2"),
    }


class _Exchanges:
    def gather(self, name, block):
        return _all_gather(name, block)

    def gather_start(self, block):
        return _exchange_start("ag_start", block, per_peer=False)

    def gather_wait(self, started, after, block):
        return _exchange_wait("ag_wait", started, after), (block, _my_index())

    def scatter(self, parts):
        return _reduce_scatter("w", parts)

    def scatter_start(self, parts):
        return _exchange_start("rs_start", parts, per_peer=True)

    def scatter_wait(self, started, after, parts):
        return _sum_landed("rs_sum8", _exchange_wait("rs_wait", started, after), parts)

    def token(self, started):
        return started[4]

    def loss_sum(self, part):
        return lax.psum(part, ("x", "y", "c"))


def _step(p, m, v, x, mem, tgt, ex):
    L, D = p["ffn1_pre_g"].shape
    lay = _Layout(D, p["ffn1_w_gate"].shape[2], p["w_in"].shape[2], p["w_mem_kv"].shape[1], p["w_gate"].shape[2],
                  3 * p["w_br_sb"].shape[1], p["w_br_sb"].shape[2], p["w_out"].shape[1])
    blocks = [_pack_layer(lay, l, p) for l in range(L)]
    sps = [{name: p[name][l] for name in _SMALL_LAYER} for l in range(L)]

    mem_n = _rms_fwd("mem_norm", mem, p["mem_norm_g"], BF16)
    gathered, own = ex.gather("ag_weights", blocks[0]), None
    h, saved, ws = x, [], []
    for l in range(L):
        if l + 1 < L:
            nxt, gathered = lax.optimization_barrier((blocks[l + 1], gathered))
            started = ex.gather_start(nxt)
            sp = _after(sps[l], "ffn1_pre_g", ex.token(started))
        else:
            sp = sps[l]
        ws.append(_unpack_layer(lay, gathered, own))
        h, s = _layer_fwd(lay, h, ws[l], sp, mem_n)
        saved.append(s)
        if l + 1 < L:
            gathered, own = ex.gather_wait(started, h, blocks[l + 1])
    loss_part, dh = _loss_grad(h, tgt)
    loss = ex.loss_sum(loss_part)

    dmem_n = jnp.zeros(mem.shape, F32)
    big, small = [None] * L, {name: [None] * L for name in _SMALL_LAYER}
    flying, token = {}, None
    for l in reversed(range(L)):
        sp = sps[l] if token is None else _after(sps[l], "ffn2_post_g", token)
        dh, packed, sm, dmem_n = _layer_bwd(lay, dh, saved[l], ws[l], sp, mem_n, dmem_n)
        if l > 0:
            flying[l] = (ex.scatter_start(packed), packed)
            token = ex.token(flying[l][0])
        else:
            big[l] = _unpack_grads(lay, ex.scatter(packed), None)
        for name in _SMALL_LAYER:
            small[name][l] = sm[name]
    for l, (started, packed) in flying.items():
        big[l] = _unpack_grads(lay, ex.scatter_wait(started, dh, packed), None)
    _, d_memg = _rms_bwd("mem_dnorm", mem, p["mem_norm_g"], dmem_n, F32)

    small_g = {name: jnp.stack(vs) for name, vs in small.items()}
    small_g["mem_norm_g"] = d_memg
    small_names = _SMALL_LAYER + ["mem_norm_g"]
    shapes = {name: p[name].shape for name in small_names}
    g_all = ex.gather("ag_small", _pack_small(small_g, L, D))
    packs = [_pack_small({name: t[name] for name in small_names}, L, D) for t in (p, m, v)]
    res = [_unpack_small(t, shapes, L, D) for t in _adamw("adamw_small", g_all, *packs)]

    out = {kind: {} for kind in ("grad", "delta", "new_m", "new_v")}
    for name in small_names:
        for kind, r in zip(("grad", "delta", "new_m", "new_v"), res):
            out[kind][name] = r[name].reshape(p[name].shape)
    for name in _SHARDED:
        g = jnp.stack([big[l][name] for l in range(L)])
        shp = g.shape
        flat = lambda t: t.reshape(-1, shp[-1])
        r = _adamw("adamw_" + name, flat(g)[None], flat(p[name]), flat(m[name]), flat(v[name]))
        for kind, t in zip(("grad", "delta", "new_m", "new_v"), r):
            out[kind][name] = t.reshape(shp)
    return loss, dh, out


def kernel(x, mem, ffn1_pre_g, ffn1_post_g, ffn1_w_gate, ffn1_w_up, ffn1_w_down, mix_pre_g, mix_post_g, w_in, b_forget, mem_norm_g, w_mem_kv, w_gate, b_gate, w_br_sb, w_br_fox, w_br_mem, w_out, ffn2_pre_g, ffn2_post_g, ffn2_w_gate, ffn2_w_up, ffn2_w_down, loss_target, m_ffn1_pre_g, m_ffn1_post_g, m_ffn1_w_gate, m_ffn1_w_up, m_ffn1_w_down, m_mix_pre_g, m_mix_post_g, m_w_in, m_b_forget, m_mem_norm_g, m_w_mem_kv, m_w_gate, m_b_gate, m_w_br_sb, m_w_br_fox, m_w_br_mem, m_w_out, m_ffn2_pre_g, m_ffn2_post_g, m_ffn2_w_gate, m_ffn2_w_up, m_ffn2_w_down, v_ffn1_pre_g, v_ffn1_post_g, v_ffn1_w_gate, v_ffn1_w_up, v_ffn1_w_down, v_mix_pre_g, v_mix_post_g, v_w_in, v_b_forget, v_mem_norm_g, v_w_mem_kv, v_w_gate, v_b_gate, v_w_br_sb, v_w_br_fox, v_w_br_mem, v_w_out, v_ffn2_pre_g, v_ffn2_post_g, v_ffn2_w_gate, v_ffn2_w_up, v_ffn2_w_down):
    p = dict(zip(_WEIGHTS, (ffn1_pre_g, ffn1_post_g, ffn1_w_gate, ffn1_w_up, ffn1_w_down, mix_pre_g, mix_post_g, w_in, b_forget, mem_norm_g, w_mem_kv, w_gate, b_gate, w_br_sb, w_br_fox, w_br_mem, w_out, ffn2_pre_g, ffn2_post_g, ffn2_w_gate, ffn2_w_up, ffn2_w_down)))
    m = dict(zip(_WEIGHTS, (m_ffn1_pre_g, m_ffn1_post_g, m_ffn1_w_gate, m_ffn1_w_up, m_ffn1_w_down, m_mix_pre_g, m_mix_post_g, m_w_in, m_b_forget, m_mem_norm_g, m_w_mem_kv, m_w_gate, m_b_gate, m_w_br_sb, m_w_br_fox, m_w_br_mem, m_w_out, m_ffn2_pre_g, m_ffn2_post_g, m_ffn2_w_gate, m_ffn2_w_up, m_ffn2_w_down)))
    v = dict(zip(_WEIGHTS, (v_ffn1_pre_g, v_ffn1_post_g, v_ffn1_w_gate, v_ffn1_w_up, v_ffn1_w_down, v_mix_pre_g, v_mix_post_g, v_w_in, v_b_forget, v_mem_norm_g, v_w_mem_kv, v_w_gate, v_b_gate, v_w_br_sb, v_w_br_fox, v_w_br_mem, v_w_out, v_ffn2_pre_g, v_ffn2_post_g, v_ffn2_w_gate, v_ffn2_w_up, v_ffn2_w_down)))
    loss, dx, out = _step(p, m, v, x[0], mem[0], loss_target[0], _Exchanges())
    return (loss, dx[None], *[out["grad"][n] for n in _WEIGHTS], *[out["delta"][n] for n in _WEIGHTS],
            *[out["new_m"][n] for n in _WEIGHTS], *[out["new_v"][n] for n in _WEIGHTS])
```

```python
import functools
import math

import jax
import jax.numpy as jnp
from jax import lax
from jax.experimental import pallas as pl
from jax.experimental.pallas import tpu as pltpu

F32 = jnp.float32
BF16 = jnp.bfloat16

LANE = 128
SUBLANE_BF16 = 16
VMEM_LIMIT = 56 * 1024 * 1024
N_DEV = 8
MESH = pl.DeviceIdType.MESH
ANY = pl.BlockSpec(memory_space=pl.ANY)

RMS_EPS = 1e-6
HEAD_DIM = 64
N_SB_HEADS = 8
N_FOX_HEADS = 8
N_MEM_HEADS = 4
NEG = -1e30
ATT_TQ = 512
ATT_TK = 256
DECAY_TK = 128
IN_TILE = 1280

ADAM_LR = 0.001
ADAM_B1 = 0.9
ADAM_B2 = 0.999
ADAM_EPS = 1e-08
ADAM_WD = 0.01
ADAM_STEP = 10


def _tile(n, target, mult=LANE):
    best = None
    for t in range(mult, min(n, target) + 1, mult):
        if n % t == 0:
            best = t
    return best if best is not None else n


def _cparams(sem):
    return pltpu.CompilerParams(dimension_semantics=sem, vmem_limit_bytes=VMEM_LIMIT)


_DIMS = {"nn": (((1,), (0,)), ((), ())), "nt": (((1,), (1,)), ((), ())), "tn": (((0,), (0,)), ((), ()))}


def _dot(a, b, mode="nn"):
    return lax.dot_general(a.astype(BF16), b.astype(BF16), _DIMS[mode], preferred_element_type=F32)


def _mm(name, pairs, mode, out_dtypes, epilogue=None, extras=(), tm=512, tn=1024):
    a0, b0 = pairs[0]
    M = a0.shape[1] if mode == "tn" else a0.shape[0]
    N = b0.shape[0] if mode == "nt" else b0.shape[1]
    tm = _tile(M, tm)
    tn = _tile(N, tn)
    np_, ne, no = len(pairs), len(extras), len(out_dtypes)

    def body(*refs):
        a_refs, b_refs = refs[:np_], refs[np_:2 * np_]
        e_refs = refs[2 * np_:2 * np_ + ne]
        o_refs = refs[2 * np_ + ne:]
        accs = [_dot(a[...], b[...], mode) for a, b in zip(a_refs, b_refs)]
        outs = epilogue(accs, [e[...] for e in e_refs]) if epilogue is not None else accs
        for o, val in zip(o_refs, outs):
            o[...] = val.astype(o.dtype)

    in_specs = []
    for a, _ in pairs:
        if mode == "tn":
            in_specs.append(pl.BlockSpec((a.shape[0], tm), lambda j, i: (0, i)))
        else:
            in_specs.append(pl.BlockSpec((tm, a.shape[1]), lambda j, i: (i, 0)))
    for _, b in pairs:
        if mode == "nt":
            in_specs.append(pl.BlockSpec((tn, b.shape[1]), lambda j, i: (j, 0)))
        else:
            in_specs.append(pl.BlockSpec((b.shape[0], tn), lambda j, i: (0, j)))
    for e, off in extras:
        if e.shape[0] == 1:
            in_specs.append(pl.BlockSpec((1, tn), functools.partial(lambda j, i, o: (0, j + o), o=off // tn)))
        else:
            in_specs.append(pl.BlockSpec((tm, tn), functools.partial(lambda j, i, o: (i, j + o), o=off // tn)))
    out_specs = [pl.BlockSpec((tm, tn), lambda j, i: (i, j)) for _ in range(no)]
    outs = pl.pallas_call(
        body, name=name, grid=(N // tn, M // tm),
        in_specs=in_specs, out_specs=out_specs,
        out_shape=[jax.ShapeDtypeStruct((M, N), dt) for dt in out_dtypes],
        compiler_params=_cparams(("parallel", "parallel")),
    )(*[a for a, _ in pairs], *[b for _, b in pairs], *[e for e, _ in extras])
    return outs[0] if no == 1 else outs


def _sum_accs(accs, _):
    total = accs[0]
    for acc in accs[1:]:
        total = total + acc
    return [total]


def _rstd(x):
    return lax.rsqrt(jnp.mean(x * x, axis=-1, keepdims=True) + RMS_EPS)


def _rms_fwd(name, x, g, out_dtype, res=None, scale=1.0, tr=512):
    R, D = x.shape
    tr = _tile(R, tr, 8)
    has_res = res is not None

    def body(*refs):
        x_ref, g_ref = refs[:2]
        o_ref = refs[-1]
        xv = x_ref[...]
        y = (xv * _rstd(xv)) * g_ref[...]
        if has_res:
            y = refs[2][...] + scale * y
        o_ref[...] = y.astype(o_ref.dtype)

    row = pl.BlockSpec((tr, D), lambda i: (i, 0))
    gain = pl.BlockSpec((1, D), lambda i: (0, 0))
    return pl.pallas_call(
        body, name=name, grid=(R // tr,),
        in_specs=[row, gain] + ([row] if has_res else []), out_specs=row,
        out_shape=jax.ShapeDtypeStruct((R, D), out_dtype),
        compiler_params=_cparams(("parallel",)),
    )(x, g.reshape(1, D), *([res] if has_res else []))


def _rms_bwd(name, x, g, dy, out_dtype, scale=1.0, res=None, tr=512):
    R, D = x.shape
    tr = _tile(R, tr, 8)
    has_res = res is not None

    def body(*refs):
        x_ref, g_ref, dy_ref = refs[:3]
        dx_ref, dg_ref = refs[-2:]
        i = pl.program_id(0)
        xv = x_ref[...]
        xhat = xv * _rstd(xv)
        dyv = dy_ref[...].astype(F32) * scale
        gy = dyv * g_ref[...]
        dx = _rstd(xv) * (gy - xhat * jnp.mean(gy * xhat, axis=-1, keepdims=True))
        if has_res:
            dx = refs[3][...] + dx
        dx_ref[...] = dx.astype(dx_ref.dtype)
        part = jnp.sum(dyv * xhat, axis=0, keepdims=True)

        @pl.when(i == 0)
        def _():
            dg_ref[...] = part

        @pl.when(i > 0)
        def _():
            dg_ref[...] += part

    row = pl.BlockSpec((tr, D), lambda i: (i, 0))
    gain = pl.BlockSpec((1, D), lambda i: (0, 0))
    dx, dg = pl.pallas_call(
        body, name=name, grid=(R // tr,),
        in_specs=[row, gain, row] + ([row] if has_res else []), out_specs=[row, gain],
        out_shape=[jax.ShapeDtypeStruct((R, D), out_dtype), jax.ShapeDtypeStruct((1, D), F32)],
        compiler_params=_cparams(("arbitrary",)),
    )(x, g.reshape(1, D), dy, *([res] if has_res else []))
    return dx, dg[0]


def _loss_grad(y, tgt, tr=512):
    R, D = y.shape
    tr = _tile(R, tr, 8)

    def body(y_ref, t_ref, dy_ref, loss_ref):
        i = pl.program_id(0)
        d = y_ref[...] - t_ref[...]
        dy_ref[...] = d / D
        part = 0.5 * jnp.sum(jnp.mean(d * d, axis=-1, keepdims=True), axis=0, keepdims=True)
        tile = jnp.broadcast_to(part, loss_ref.shape)

        @pl.when(i == 0)
        def _():
            loss_ref[...] = tile

        @pl.when(i > 0)
        def _():
            loss_ref[...] += tile

    row = pl.BlockSpec((tr, D), lambda i: (i, 0))
    dy, loss = pl.pallas_call(
        body, name="loss_grad", grid=(R // tr,),
        in_specs=[row, row], out_specs=[row, pl.BlockSpec((8, LANE), lambda i: (0, 0))],
        out_shape=[jax.ShapeDtypeStruct((R, D), F32), jax.ShapeDtypeStruct((8, LANE), F32)],
        compiler_params=_cparams(("arbitrary",)),
    )(y, tgt)
    return loss[0, 0], dy


def _colsum(name, x, tr=512, tn=1024):
    R, N = x.shape
    tr, tn = _tile(R, tr, 8), _tile(N, tn)

    def body(x_ref, o_ref):
        i = pl.program_id(1)
        part = jnp.sum(x_ref[...].astype(F32), axis=0, keepdims=True)

        @pl.when(i == 0)
        def _():
            o_ref[...] = part

        @pl.when(i > 0)
        def _():
            o_ref[...] += part

    out = pl.pallas_call(
        body, name=name, grid=(N // tn, R // tr),
        in_specs=[pl.BlockSpec((tr, tn), lambda j, i: (i, j))], out_specs=pl.BlockSpec((1, tn), lambda j, i: (0, j)),
        out_shape=jax.ShapeDtypeStruct((1, N), F32),
        compiler_params=_cparams(("parallel", "arbitrary")),
    )(x)
    return out[0]


def _tri(tk, rel):
    j = lax.broadcasted_iota(jnp.int32, (tk, tk), 0)
    s = lax.broadcasted_iota(jnp.int32, (tk, tk), 1)
    return rel(j, s).astype(BF16)


def _dot_split(x, m, parts=2):
    total = None
    rem = x
    for _ in range(parts):
        piece = rem.astype(BF16)
        rem = rem - piece.astype(F32)
        term = jnp.dot(piece, m, preferred_element_type=F32)
        total = term if total is None else total + term
    return total


def _log_not_and_beta(z, mask):
    ln = -(jnp.maximum(z, 0.0) + jnp.log(1.0 + jnp.exp(-jnp.abs(z))))
    return (ln if mask is None else jnp.where(mask, ln, 0.0)), ln + z


def _att_tiles(T, Tk, causal):
    tq = min(ATT_TQ, T)
    tk = min(ATT_TK, tq if causal else Tk)
    return tq, tk, (tq if causal else Tk) // tk


def _key_base(j, tq):
    return j * tq if isinstance(j, int) else pl.multiple_of(j * tq, tq)


def _is_pow2(scale):
    return math.log2(scale).is_integer()


def _per_head(x, hpb, d):
    if hpb == 1:
        return [x]
    lane = lax.broadcasted_iota(jnp.int32, x.shape, 1)
    return [jnp.where((lane >= h * d) & (lane < (h + 1) * d), x, jnp.zeros_like(x)) for h in range(hpb)]


def _join_heads(xs, d):
    out = xs[-1]
    if len(xs) > 1:
        lane = lax.broadcasted_iota(jnp.int32, out.shape, 1)
        for h in reversed(range(len(xs) - 1)):
            out = jnp.where(lane < (h + 1) * d, xs[h], out)
    return out


def _lane_tile(rows, off, whole):
    if whole:
        return pl.BlockSpec((rows, LANE), lambda g, i: (0, off + g))
    return pl.BlockSpec((rows, LANE), lambda g, i: (i, off + g))


def _sb_fwd(q, k, v, n_tiles, d, scale):
    T = q[0].shape[0]
    hpb = LANE // d
    tq, tk, nsub = _att_tiles(T, T, True)
    assert _is_pow2(scale)

    def body(q_ref, k_ref, v_ref, o_ref, ob_ref, rt_ref, acc_ref, r_ref):
        qi = pl.program_id(1)
        qh = _per_head(q_ref[...] * scale, hpb, d)
        acc_ref[...] = jnp.zeros_like(acc_ref)
        r_ref[...] = jnp.zeros_like(r_ref)
        row = lax.broadcasted_iota(jnp.int32, (tq, tk), 0)
        col = lax.broadcasted_iota(jnp.int32, (tq, tk), 1)
        after = _tri(tk, lambda j, s: j > s)

        def step(j, diagonal):
            base = _key_base(j, tq)
            for h in range(hpb):
                parts = []
                for u in reversed(range(nsub)):
                    z = _dot(qh[h], k_ref[pl.ds(base + u * tk, tk), :], "nt")
                    mask = (col + u * tk) < row if diagonal else None
                    ln, lb = _log_not_and_beta(z, mask)
                    between = _dot_split(ln, after, parts=1)
                    first = ln[:, 0:1].astype(BF16).astype(F32)
                    parts.append((u, lb, between, between[:, 0:1] + first, mask))
                r = r_ref[h]
                out = None
                for u, lb, between, total, mask in parts:
                    w = jnp.exp(lb + between + r)
                    if diagonal:
                        w = jnp.where(mask, w, 0.0)
                    term = _dot(w, v_ref[pl.ds(base + u * tk, tk), :])
                    out = term if out is None else out + term
                    r = r + total
                acc_ref[h] += out
                r_ref[h] = r

        def below(i, carry):
            step(qi - 1 - i, False)
            return carry

        step(qi, True)
        lax.fori_loop(0, qi, below, 0)
        o = _join_heads([acc_ref[h] for h in range(hpb)], d)
        o_ref[...] = o
        ob_ref[...] = o.astype(ob_ref.dtype)
        rt_ref[...] = r_ref[...]

    out = pl.BlockSpec((tq, LANE), lambda g, i: (i, g))
    col = pl.BlockSpec((hpb, tq, 1), lambda g, i: (g, i, 0))
    return pl.pallas_call(
        body, name="sb_fwd", grid=(n_tiles, T // tq),
        in_specs=[_lane_tile(tq, q[1], False), _lane_tile(T, k[1], True), _lane_tile(T, v[1], True)],
        out_specs=[out, out, col],
        out_shape=[jax.ShapeDtypeStruct((T, n_tiles * LANE), F32), jax.ShapeDtypeStruct((T, n_tiles * LANE), BF16),
                   jax.ShapeDtypeStruct((n_tiles * hpb, T, 1), F32)],
        scratch_shapes=[pltpu.VMEM((hpb, tq, LANE), F32), pltpu.VMEM((hpb, tq, 1), F32)],
        compiler_params=_cparams(("parallel", "arbitrary")),
    )(q[0], k[0], v[0])


def _sb_bwd(q, k, v, do, rtot, n_tiles, d, scale):
    T = q[0].shape[0]
    hpb = LANE // d
    tq, tk, nsub = _att_tiles(T, T, True)
    assert _is_pow2(scale)

    def body(q_ref, k_ref, v_ref, do_ref, rt_ref, dq_ref, dk_ref, dv_ref, dk_acc, dv_acc, dq_acc, p_ref, c_ref):
        qi = pl.program_id(1)

        @pl.when(qi == 0)
        def _():
            dk_acc[...] = jnp.zeros_like(dk_acc)
            dv_acc[...] = jnp.zeros_like(dv_acc)

        qh = _per_head(q_ref[...] * scale, hpb, d)
        doh = _per_head(do_ref[...], hpb, d)
        dq_acc[...] = jnp.zeros_like(dq_acc)
        p_ref[...] = jnp.zeros_like(p_ref)
        c_ref[...] = jnp.zeros_like(c_ref)
        row = lax.broadcasted_iota(jnp.int32, (tq, tk), 0)
        col = lax.broadcasted_iota(jnp.int32, (tq, tk), 1)
        upto = _tri(tk, lambda j, s: j <= s)
        before = _tri(tk, lambda j, s: j < s)

        def step(j, diagonal):
            base = _key_base(j, tq)
            for h in range(hpb):
                first = []
                for u in range(nsub):
                    ks = base + u * tk
                    kv = k_ref[pl.ds(ks, tk), :]
                    z = _dot(qh[h], kv, "nt")
                    mask = (col + u * tk) < row if diagonal else None
                    ln, lb = _log_not_and_beta(z, mask)
                    dw = _dot(doh[h], v_ref[pl.ds(ks, tk), :], "nt")
                    first.append((ks, kv, mask, lb, jnp.exp(lb), _dot_split(ln, upto, parts=1), dw))
                rt, pre, cpre = rt_ref[h], p_ref[h], c_ref[h]
                dq = None
                for ks, kv, mask, lb, sig, local, dw in first:
                    prefix = local + pre
                    w = jnp.exp(lb + (rt - prefix))
                    if diagonal:
                        w = jnp.where(mask, w, 0.0)
                    g = dw * w
                    c = _dot_split(g, before, parts=1) + cpre
                    dz = g * (1.0 - sig) - c * sig
                    if diagonal:
                        dz = jnp.where(mask, dz, 0.0)
                    term = _dot(dz, kv)
                    dq = term if dq is None else dq + term
                    dk_acc[pl.ds(ks, tk), :] += _dot(dz, qh[h], "tn")
                    dv_acc[pl.ds(ks, tk), :] += _dot(w, doh[h], "tn")
                    pre = prefix[:, tk - 1:tk]
                    cpre = c[:, tk - 1:tk] + g[:, tk - 1:tk]
                dq_acc[h] += dq
                p_ref[h] = pre
                c_ref[h] = cpre

        def below(j, carry):
            step(j, False)
            return carry

        lax.fori_loop(0, qi, below, 0)
        step(qi, True)
        dq_ref[...] = (_join_heads([dq_acc[h] for h in range(hpb)], d) * scale).astype(dq_ref.dtype)

        @pl.when(qi == pl.num_programs(1) - 1)
        def _():
            dk_ref[...] = dk_acc[...].astype(dk_ref.dtype)
            dv_ref[...] = dv_acc[...].astype(dv_ref.dtype)

    blk = pl.BlockSpec((tq, LANE), lambda g, i: (i, g))
    full = pl.BlockSpec((T, LANE), lambda g, i: (0, g))
    col = pl.BlockSpec((hpb, tq, 1), lambda g, i: (g, i, 0))
    wide = jax.ShapeDtypeStruct((T, n_tiles * LANE), BF16)
    return pl.pallas_call(
        body, name="sb_bwd", grid=(n_tiles, T // tq),
        in_specs=[_lane_tile(tq, q[1], False), _lane_tile(T, k[1], True), _lane_tile(T, v[1], True), blk, col],
        out_specs=[blk, full, full], out_shape=[wide, wide, wide],
        scratch_shapes=[pltpu.VMEM((T, LANE), F32), pltpu.VMEM((T, LANE), F32), pltpu.VMEM((hpb, tq, LANE), F32),
                        pltpu.VMEM((hpb, tq, 1), F32), pltpu.VMEM((hpb, tq, 1), F32)],
        compiler_params=_cparams(("parallel", "arbitrary")),
    )(q[0], k[0], v[0], do, rtot)


def _attn_fwd(name, q, k, v, n_tiles, d, scale, c=None):
    T, Tk = q[0].shape[0], k[0].shape[0]
    hpb = LANE // d
    H = n_tiles * hpb
    causal = c is not None
    tq, tk, nsub = _att_tiles(T, Tk, causal)
    fold = _is_pow2(scale)

    def body(*refs):
        q_ref, k_ref, v_ref = refs[:3]
        cc_ref, cr_ref = refs[3:5] if causal else (None, None)
        o_ref, ob_ref, lse_ref, m_ref, l_ref, acc_ref = refs[-6:]
        qi = pl.program_id(1)
        qh = _per_head(q_ref[...] * scale if fold else q_ref[...], hpb, d)
        m_ref[...] = jnp.full_like(m_ref, NEG)
        l_ref[...] = jnp.zeros_like(l_ref)
        acc_ref[...] = jnp.zeros_like(acc_ref)
        row = lax.broadcasted_iota(jnp.int32, (tq, tk), 0)
        col = lax.broadcasted_iota(jnp.int32, (tq, tk), 1)

        def step(j, diagonal):
            base = _key_base(j, tq)
            for h in range(hpb):
                zs = []
                for u in range(nsub):
                    z = _dot(qh[h], k_ref[pl.ds(base + u * tk, tk), :], "nt")
                    if not fold:
                        z = z * scale
                    if causal:
                        z = z + cc_ref[h] - cr_ref[h, j * nsub + u]
                    if diagonal:
                        z = jnp.where((col + u * tk) <= row, z, NEG)
                    zs.append(z)
                m_prev = m_ref[h]
                m_new = m_prev
                for z in zs:
                    m_new = jnp.maximum(m_new, jnp.max(z, axis=1, keepdims=True))
                alpha = jnp.exp(m_prev - m_new)
                l_new = alpha * l_ref[h]
                out = alpha * acc_ref[h]
                for u, z in enumerate(zs):
                    p = jnp.exp(z - m_new)
                    l_new = l_new + jnp.sum(p, axis=1, keepdims=True)
                    out = out + _dot(p, v_ref[pl.ds(base + u * tk, tk), :])
                l_ref[h] = l_new
                acc_ref[h] = out
                m_ref[h] = m_new

        def below(j, carry):
            step(j, False)
            return carry

        if causal:
            lax.fori_loop(0, qi, below, 0)
            step(qi, True)
        else:
            step(0, False)
        o = _join_heads([acc_ref[h] / l_ref[h] for h in range(hpb)], d)
        o_ref[...] = o
        ob_ref[...] = o.astype(ob_ref.dtype)
        lse_ref[...] = m_ref[...] + jnp.log(l_ref[...])

    out = pl.BlockSpec((tq, LANE), lambda g, i: (i, g))
    col = pl.BlockSpec((hpb, tq, 1), lambda g, i: (g, i, 0))
    in_specs = [_lane_tile(tq, q[1], False), _lane_tile(Tk, k[1], True), _lane_tile(Tk, v[1], True)]
    args = [q[0], k[0], v[0]]
    if causal:
        in_specs += [col, pl.BlockSpec((hpb, T // tk, 1, tk), lambda g, i: (g, 0, 0, 0))]
        args += [c.reshape(H, T, 1), c.reshape(H, T // tk, 1, tk)]
    return pl.pallas_call(
        body, name=name, grid=(n_tiles, T // tq),
        in_specs=in_specs, out_specs=[out, out, col],
        out_shape=[jax.ShapeDtypeStruct((T, n_tiles * LANE), F32), jax.ShapeDtypeStruct((T, n_tiles * LANE), BF16),
                   jax.ShapeDtypeStruct((H, T, 1), F32)],
        scratch_shapes=[pltpu.VMEM((hpb, tq, 1), F32), pltpu.VMEM((hpb, tq, 1), F32),
                        pltpu.VMEM((hpb, tq, LANE), F32)],
        compiler_params=_cparams(("parallel", "arbitrary")),
    )(*args)


def _attn_bwd(name, q, k, v, o, do, lse, n_tiles, d, scale, c=None):
    T, Tk = q[0].shape[0], k[0].shape[0]
    hpb = LANE // d
    H = n_tiles * hpb
    causal = c is not None
    tq, tk, nsub = _att_tiles(T, Tk, causal)
    fold = _is_pow2(scale)

    def body(*refs):
        q_ref, k_ref, v_ref, o_ref, do_ref, lse_ref = refs[:6]
        cc_ref, cr_ref = refs[6:8] if causal else (None, None)
        n_out = 5 if causal else 3
        outs = refs[-(n_out + 3):-3]
        dq_ref, dk_ref, dv_ref = outs[:3]
        dc_ref, drow_ref = outs[3:5] if causal else (None, None)
        dk_acc, dv_acc, dq_acc = refs[-3:]
        qi = pl.program_id(1)

        @pl.when(qi == 0)
        def _():
            dk_acc[...] = jnp.zeros_like(dk_acc)
            dv_acc[...] = jnp.zeros_like(dv_acc)
            if causal:
                dc_ref[...] = jnp.zeros_like(dc_ref)

        qh = _per_head(q_ref[...] * scale if fold else q_ref[...], hpb, d)
        doh = _per_head(do_ref[...], hpb, d)
        deltas = [jnp.sum(t.astype(F32) * o_ref[...], axis=1, keepdims=True) for t in doh]
        dq_acc[...] = jnp.zeros_like(dq_acc)
        if causal:
            drow_ref[...] = jnp.zeros_like(drow_ref)
        row = lax.broadcasted_iota(jnp.int32, (tq, tk), 0)
        col = lax.broadcasted_iota(jnp.int32, (tq, tk), 1)

        def step(j, diagonal):
            base = _key_base(j, tq)
            for h in range(hpb):
                lse_v, delta = lse_ref[h], deltas[h]
                dq, drow = None, None
                for u in range(nsub):
                    ks = base + u * tk
                    kv = k_ref[pl.ds(ks, tk), :]
                    z = _dot(qh[h], kv, "nt")
                    if not fold:
                        z = z * scale
                    if causal:
                        z = z + cc_ref[h] - cr_ref[h, j * nsub + u]
                    if diagonal:
                        z = jnp.where((col + u * tk) <= row, z, NEG)
                    p = jnp.exp(z - lse_v)
                    ds = p * (_dot(doh[h], v_ref[pl.ds(ks, tk), :], "nt") - delta)
                    term = _dot(ds, kv)
                    dq = term if dq is None else dq + term
                    dk = _dot(ds, qh[h], "tn")
                    dk_acc[pl.ds(ks, tk), :] += dk if fold else dk * scale
                    dv_acc[pl.ds(ks, tk), :] += _dot(p, doh[h], "tn")
                    if causal:
                        dc_ref[h, j * nsub + u] -= jnp.sum(ds, axis=0, keepdims=True)
                        rs = jnp.sum(ds, axis=1, keepdims=True)
                        drow = rs if drow is None else drow + rs
                dq_acc[h] += dq
                if causal:
                    drow_ref[h] += drow

        def below(j, carry):
            step(j, False)
            return carry

        if causal:
            lax.fori_loop(0, qi, below, 0)
            step(qi, True)
        else:
            step(0, False)
        dq_ref[...] = (_join_heads([dq_acc[h] for h in range(hpb)], d) * scale).astype(dq_ref.dtype)

        @pl.when(qi == pl.num_programs(1) - 1)
        def _():
            dk_ref[...] = dk_acc[...].astype(dk_ref.dtype)
            dv_ref[...] = dv_acc[...].astype(dv_ref.dtype)

    blk = pl.BlockSpec((tq, LANE), lambda g, i: (i, g))
    full = pl.BlockSpec((Tk, LANE), lambda g, i: (0, g))
    col = pl.BlockSpec((hpb, tq, 1), lambda g, i: (g, i, 0))
    crow = pl.BlockSpec((hpb, T // tk, 1, tk), lambda g, i: (g, 0, 0, 0))
    in_specs = [_lane_tile(tq, q[1], False), _lane_tile(Tk, k[1], True), _lane_tile(Tk, v[1], True), blk, blk, col]
    args = [q[0], k[0], v[0], o, do, lse]
    out_specs = [blk, full, full]
    out_shape = [jax.ShapeDtypeStruct((T, n_tiles * LANE), BF16), jax.ShapeDtypeStruct((Tk, n_tiles * LANE), BF16),
                 jax.ShapeDtypeStruct((Tk, n_tiles * LANE), BF16)]
    if causal:
        in_specs += [col, crow]
        args += [c.reshape(H, T, 1), c.reshape(H, T // tk, 1, tk)]
        out_specs += [crow, col]
        out_shape += [jax.ShapeDtypeStruct((H, T // tk, 1, tk), F32), jax.ShapeDtypeStruct((H, T, 1), F32)]
    outs = pl.pallas_call(
        body, name=name, grid=(n_tiles, T // tq),
        in_specs=in_specs, out_specs=out_specs, out_shape=out_shape,
        scratch_shapes=[pltpu.VMEM((Tk, LANE), F32), pltpu.VMEM((Tk, LANE), F32), pltpu.VMEM((hpb, tq, LANE), F32)],
        compiler_params=_cparams(("parallel", "arbitrary")),
    )(*args)
    if causal:
        return outs[0], outs[1], outs[2], outs[3].reshape(H, T), outs[4].reshape(H, T)
    return outs


def _decay_fwd(fl, b):
    H, T = fl.shape
    tk = DECAY_TK

    def body(x_ref, b_ref, c_ref):
        upto = _tri(tk, lambda j, s: j <= s)
        carry = jnp.zeros((H, 1), F32)
        for i in range(T // tk):
            xv = x_ref[:, i * tk:(i + 1) * tk] + b_ref[...]
            lf = jnp.minimum(xv, 0.0) - jnp.log(1.0 + jnp.exp(-jnp.abs(xv)))
            pref = _dot_split(lf, upto, parts=3) + carry
            c_ref[:, i * tk:(i + 1) * tk] = pref
            carry = pref[:, tk - 1:tk]

    vm = pl.BlockSpec(memory_space=pltpu.VMEM)
    return pl.pallas_call(
        body, name="decay_fwd", in_specs=[vm, vm], out_specs=vm,
        out_shape=jax.ShapeDtypeStruct((H, T), F32),
    )(fl, b)


def _decay_bwd(dc_cols, dc_rows, fl, b):
    H, T = fl.shape
    tk = DECAY_TK

    def body(dc_ref, dr_ref, x_ref, b_ref, dx_ref, db_ref):
        from_ = _tri(tk, lambda j, s: j >= s)
        carry = jnp.zeros((H, 1), F32)
        total = jnp.zeros((H, 1), F32)
        for i in reversed(range(T // tk)):
            sl = slice(i * tk, (i + 1) * tk)
            suffix = _dot_split(dc_ref[:, sl] + dr_ref[:, sl], from_, parts=3) + carry
            xv = x_ref[:, sl] + b_ref[...]
            dx = suffix / (1.0 + jnp.exp(xv))
            dx_ref[:, sl] = dx
            total = total + jnp.sum(dx, axis=1, keepdims=True)
            carry = suffix[:, 0:1]
        db_ref[...] = jnp.broadcast_to(total, db_ref.shape)

    vm = pl.BlockSpec(memory_space=pltpu.VMEM)
    dx, db = pl.pallas_call(
        body, name="decay_bwd", in_specs=[vm, vm, vm, vm], out_specs=[vm, vm],
        out_shape=[jax.ShapeDtypeStruct((H, T), F32), jax.ShapeDtypeStruct((H, LANE), F32)],
    )(dc_cols, dc_rows, fl, b)
    return dx, db[:, 0]


def _place():
    x, y, c = lax.axis_index("x"), lax.axis_index("y"), lax.axis_index("c")
    return x, y, c, [(1 - x, y), (x, 1 - y), (1 - x, 1 - y)]


def _all_gather(name, block):
    R, C = block.shape

    def body(x_ref, out_ref, send_sems, recv_sems, local_sem):
        x, y, c, chips = _place()
        me, sibling = (x, y, c), (x, y, 1 - c)

        def rows(px, py, pc):
            return out_ref.at[4 * px + 2 * py + pc]

        def copy(k, blk, to, src=None):
            return pltpu.make_async_remote_copy(
                src_ref=rows(*blk) if src is None else src, dst_ref=rows(*blk),
                send_sem=send_sems.at[k], recv_sem=recv_sems.at[k], device_id=to, device_id_type=MESH)

        mine = pltpu.make_async_copy(x_ref, rows(*me), local_sem)
        mine.start()
        first = [copy(0, me, sibling, src=x_ref)]
        first += [copy(1 + j, me, (*chip, c), src=x_ref) for j, chip in enumerate(chips)]
        for cp in first:
            cp.start()
        passed = [copy(4 + j, (*chip, c), sibling) for j, chip in enumerate(chips)]
        for j, chip in enumerate(chips):
            copy(1 + j, (*chip, c), me).wait_recv()
            passed[j].start()
        copy(0, sibling, me).wait_recv()
        for j, chip in enumerate(chips):
            copy(4 + j, (*chip, 1 - c), me).wait_recv()
        for cp in first + passed:
            cp.wait_send()
        mine.wait()

    return pl.pallas_call(
        body, name=name, in_specs=[ANY], out_specs=ANY,
        out_shape=jax.ShapeDtypeStruct((N_DEV, R, C), block.dtype),
        scratch_shapes=[pltpu.SemaphoreType.DMA((7,)), pltpu.SemaphoreType.DMA((7,)), pltpu.SemaphoreType.DMA(())],
    )(block)


def _swap_with_sibling(name, parts):
    _, R, C = parts.shape

    def body(p_ref, out_ref, send_sems, recv_sems):
        x, y, c, _ = _place()
        copies = [pltpu.make_async_remote_copy(
            src_ref=p_ref.at[2 * q + (1 - c)], dst_ref=out_ref.at[q],
            send_sem=send_sems.at[q], recv_sem=recv_sems.at[q], device_id=(x, y, 1 - c), device_id_type=MESH)
            for q in range(4)]
        for cp in copies:
            cp.start()
        for cp in copies:
            cp.wait_recv()
        for cp in copies:
            cp.wait_send()

    return pl.pallas_call(
        body, name=name, in_specs=[ANY], out_specs=ANY,
        out_shape=jax.ShapeDtypeStruct((4, R, C), parts.dtype),
        scratch_shapes=[pltpu.SemaphoreType.DMA((4,)), pltpu.SemaphoreType.DMA((4,))],
    )(parts)


def _add_own(name, parts, got, tr=512):
    _, R, C = parts.shape
    tr = _tile(R, tr, SUBLANE_BF16)

    def body(c_ref, p_ref, g_ref, o_ref):
        o_ref[...] = (p_ref[...].astype(F32) + g_ref[...].astype(F32)).astype(o_ref.dtype)

    return pl.pallas_call(
        body, name=name,
        grid_spec=pltpu.PrefetchScalarGridSpec(
            num_scalar_prefetch=1, grid=(4, R // tr),
            in_specs=[pl.BlockSpec((1, tr, C), lambda q, i, c: (2 * q + c[0], i, 0)),
                      pl.BlockSpec((1, tr, C), lambda q, i, c: (q, i, 0))],
            out_specs=pl.BlockSpec((1, tr, C), lambda q, i, c: (q, i, 0))),
        out_shape=jax.ShapeDtypeStruct((4, R, C), parts.dtype),
        compiler_params=_cparams(("parallel", "parallel")),
    )(lax.axis_index("c").astype(jnp.int32).reshape(1), parts, got)


def _swap_with_chips(name, parts):
    _, R, C = parts.shape

    def body(p_ref, out_ref, send_sems, recv_sems, local_sem):
        x, y, c, chips = _place()
        my_chip = 2 * x + y
        mine = pltpu.make_async_copy(p_ref.at[my_chip], out_ref.at[my_chip], local_sem)
        mine.start()
        sends = [pltpu.make_async_remote_copy(
            src_ref=p_ref.at[2 * cx + cy], dst_ref=out_ref.at[my_chip],
            send_sem=send_sems.at[j], recv_sem=recv_sems.at[j], device_id=(cx, cy, c), device_id_type=MESH)
            for j, (cx, cy) in enumerate(chips)]
        for cp in sends:
            cp.start()
        for j, (cx, cy) in enumerate(chips):
            pltpu.make_async_remote_copy(
                src_ref=p_ref.at[my_chip], dst_ref=out_ref.at[2 * cx + cy],
                send_sem=send_sems.at[j], recv_sem=recv_sems.at[j], device_id=(cx, cy, c), device_id_type=MESH,
            ).wait_recv()
        for cp in sends:
            cp.wait_send()
        mine.wait()

    return pl.pallas_call(
        body, name=name, in_specs=[ANY], out_specs=ANY,
        out_shape=jax.ShapeDtypeStruct((4, R, C), parts.dtype),
        scratch_shapes=[pltpu.SemaphoreType.DMA((3,)), pltpu.SemaphoreType.DMA((3,)), pltpu.SemaphoreType.DMA(())],
    )(parts)


def _sum_parts(name, parts, tr=512):
    P, R, C = parts.shape
    tr = _tile(R, tr, SUBLANE_BF16)

    def body(p_ref, o_ref):
        total = p_ref[0].astype(F32)
        for p in range(1, P):
            total = total + p_ref[p].astype(F32)
        o_ref[...] = total

    return pl.pallas_call(
        body, name=name, grid=(R // tr,),
        in_specs=[pl.BlockSpec((P, tr, C), lambda i: (0, i, 0))], out_specs=pl.BlockSpec((tr, C), lambda i: (i, 0)),
        out_shape=jax.ShapeDtypeStruct((R, C), F32),
        compiler_params=_cparams(("parallel",)),
    )(parts)


_HBM = pl.BlockSpec(memory_space=pltpu.HBM)
_SEM = pl.BlockSpec(memory_space=pltpu.SEMAPHORE)
_EFFECT = pltpu.SideEffectType.DATAFLOW_SIDE_EFFECTING


def _flipped(x, y, c, k):
    px, py, pc = (1 - x if k & 4 else x), (1 - y if k & 2 else y), (1 - c if k & 1 else c)
    return (px, py, pc), 4 * px + 2 * py + pc


def _exchange_start(name, src, per_peer):
    R, C = src.shape[-2:]

    def body(v_ref, land_ref, send_sem, recv_sem, v_thru, land_thru, token):
        x, y, c = lax.axis_index("x"), lax.axis_index("y"), lax.axis_index("c")
        me = 4 * x + 2 * y + c
        for k in range(1, N_DEV):
            peer, idx = _flipped(x, y, c, k)
            pltpu.make_async_remote_copy(
                src_ref=v_ref.at[idx] if per_peer else v_ref, dst_ref=land_ref.at[me],
                send_sem=send_sem, recv_sem=recv_sem, device_id=peer, device_id_type=MESH).start()
        token[...] = jnp.zeros_like(token)

    return pl.pallas_call(
        body, name=name,
        out_shape=(pltpu.SemaphoreType.DMA(()), pltpu.SemaphoreType.DMA(()), pltpu.HBM(src.shape, src.dtype),
                   pltpu.HBM((N_DEV, R, C), src.dtype), jax.ShapeDtypeStruct((8, LANE), F32)),
        in_specs=(_HBM, _HBM), out_specs=(_SEM, _SEM, _HBM, _HBM, pl.BlockSpec(memory_space=pltpu.VMEM)),
        input_output_aliases={0: 2, 1: 3},
        compiler_params=pltpu.CompilerParams(has_side_effects=_EFFECT),
    )(pltpu.with_memory_space_constraint(src, pltpu.HBM),
      pltpu.with_memory_space_constraint(lax.empty((N_DEV, R, C), src.dtype), pltpu.HBM))


def _exchange_wait(name, started, after):
    send_sem, recv_sem, v_thru, land_thru, _ = started

    def body(v_ref, land_ref, send_sem, recv_sem, after_ref, v_dead, got_ref):
        x, y, c = lax.axis_index("x"), lax.axis_index("y"), lax.axis_index("c")
        seven = land_ref.at[pl.ds(0, N_DEV - 1)]
        drain = pltpu.make_async_remote_copy(
            src_ref=seven, dst_ref=seven, send_sem=send_sem, recv_sem=recv_sem,
            device_id=(x, y, c), device_id_type=MESH)
        drain.wait_send()
        drain.wait_recv()

    return pl.pallas_call(
        body, name=name,
        out_shape=(pltpu.HBM(v_thru.shape, v_thru.dtype), pltpu.HBM(land_thru.shape, land_thru.dtype)),
        in_specs=(_HBM, _HBM, _SEM, _SEM, ANY), out_specs=(_HBM, _HBM), input_output_aliases={0: 0, 1: 1},
        compiler_params=pltpu.CompilerParams(has_side_effects=_EFFECT),
    )(v_thru, land_thru, send_sem, recv_sem, after)[1]


def _my_index():
    return 4 * lax.axis_index("x") + 2 * lax.axis_index("y") + lax.axis_index("c")


def _sum_landed(name, landed, parts, tr=512):
    P, R, C = landed.shape
    tr = _tile(R, tr, SUBLANE_BF16)

    def body(me_ref, l_ref, own_ref, o_ref):
        total = None
        for s in range(P):
            part = jnp.where(me_ref[0] == s, own_ref[0], l_ref[s]).astype(F32)
            total = part if total is None else total + part
        o_ref[...] = total

    return pl.pallas_call(
        body, name=name,
        grid_spec=pltpu.PrefetchScalarGridSpec(
            num_scalar_prefetch=1, grid=(R // tr,),
            in_specs=[pl.BlockSpec((P, tr, C), lambda i, me: (0, i, 0)),
                      pl.BlockSpec((1, tr, C), lambda i, me: (me[0], i, 0))],
            out_specs=pl.BlockSpec((tr, C), lambda i, me: (i, 0))),
        out_shape=jax.ShapeDtypeStruct((R, C), F32),
        compiler_params=_cparams(("parallel",)),
    )(_my_index().astype(jnp.int32).reshape(1), landed, parts)


def _after(params, name, token):
    return {**params, name: params[name] + token[0, 0]}


def _reduce_scatter(tag, parts):
    got = _swap_with_sibling("rs_pair_" + tag, parts)
    pair = _add_own("rs_add_" + tag, parts, got)
    quad = _swap_with_chips("rs_chips_" + tag, pair)
    return _sum_parts("rs_sum_" + tag, quad)


def _adamw(name, g_parts, w, m, v, tr=512):
    P, R, C = g_parts.shape
    tr = _tile(R, tr, 8)

    def body(g_ref, w_ref, m_ref, v_ref, go_ref, d_ref, mo_ref, vo_ref):
        g = g_ref[0]
        for p in range(1, P):
            g = g + g_ref[p]
        mn = ADAM_B1 * m_ref[...] + (1.0 - ADAM_B1) * g
        vn = ADAM_B2 * v_ref[...] + (1.0 - ADAM_B2) * (g * g)
        m_hat = mn / (1.0 - ADAM_B1 ** ADAM_STEP)
        v_hat = vn / (1.0 - ADAM_B2 ** ADAM_STEP)
        go_ref[...] = g
        d_ref[...] = -ADAM_LR * (m_hat / (jnp.sqrt(v_hat) + ADAM_EPS) + ADAM_WD * w_ref[...])
        mo_ref[...] = mn
        vo_ref[...] = vn

    row = pl.BlockSpec((tr, C), lambda i: (i, 0))
    return pl.pallas_call(
        body, name=name, grid=(R // tr,),
        in_specs=[pl.BlockSpec((P, tr, C), lambda i: (0, i, 0)), row, row, row], out_specs=[row] * 4,
        out_shape=[jax.ShapeDtypeStruct((R, C), F32)] * 4,
        compiler_params=_cparams(("parallel",)),
    )(g_parts, w, m, v)


def _pad_rows(t, rows):
    return jnp.pad(t, ((0, rows - t.shape[0]), (0, 0)))


class _Layout:
    def __init__(self, D, ff_shard, in_shard, kv_shard, gate_shard, br_in, br_shard, out_shard):
        self.D = D
        self.in_shard = in_shard
        self.in_pad = -(-in_shard // LANE) * LANE
        self.in_cols = -(-N_DEV * in_shard // IN_TILE) * IN_TILE
        self.br_in, self.br_shard = br_in, br_shard
        br_rows = br_shard * br_in // D
        sizes = [("g1", ff_shard), ("u1", ff_shard), ("d1", ff_shard), ("win", self.in_pad), ("kv", kv_shard),
                 ("gate", gate_shard), ("br", br_rows), ("out", out_shard),
                 ("g2", ff_shard), ("u2", ff_shard), ("d2", ff_shard)]
        self.seg, off = {}, 0
        for key, n in sizes:
            assert n % SUBLANE_BF16 == 0, (key, n)
            self.seg[key] = (off, n)
            off += n
        self.rows = off

    def pack(self, parts):
        return jnp.concatenate([parts[key] for key in self.seg], axis=0)

    def take(self, gathered, key, own=None):
        off, n = self.seg[key]
        seg = gathered[:, off:off + n, :]
        if own is not None:
            seg = lax.dynamic_update_slice(seg, own[0][off:off + n][None], (own[1], 0, 0))
        return seg.reshape(N_DEV * n, self.D)

    def spread(self, full, key):
        _, n = self.seg[key]
        return full.reshape(N_DEV, n, self.D)


def _pack_layer(lay, l, p):
    D = lay.D
    br = jnp.concatenate([p["w_br_sb"][l], p["w_br_fox"][l], p["w_br_mem"][l]], axis=0)
    parts = {
        "g1": p["ffn1_w_gate"][l].T, "u1": p["ffn1_w_up"][l].T, "d1": p["ffn1_w_down"][l],
        "win": _pad_rows(p["w_in"][l].T, lay.in_pad), "kv": p["w_mem_kv"][l], "gate": p["w_gate"][l].T,
        "br": br.T.reshape(-1, D), "out": p["w_out"][l],
        "g2": p["ffn2_w_gate"][l].T, "u2": p["ffn2_w_up"][l].T, "d2": p["ffn2_w_down"][l],
    }
    return lay.pack({k: t.astype(BF16) for k, t in parts.items()})


def _align_win(lay, packed):
    D = lay.D
    real = packed.reshape(N_DEV, lay.in_pad, D)[:, :lay.in_shard].reshape(N_DEV * lay.in_shard, D)
    rows = jnp.concatenate([real[:_QKV_W], real[_QKV_W + N_FOX_HEADS:], real[_QKV_W:_QKV_W + N_FOX_HEADS]], axis=0)
    return _pad_rows(rows, lay.in_cols)


def _unalign_win(lay, aligned):
    D = lay.D
    n_real = N_DEV * lay.in_shard
    mem_w = n_real - _QKV_W - N_FOX_HEADS
    real = jnp.concatenate([aligned[:_QKV_W], aligned[_QKV_W + mem_w:n_real], aligned[_QKV_W:_QKV_W + mem_w]], axis=0)
    real = real.reshape(N_DEV, lay.in_shard, D)
    return jnp.pad(real, ((0, 0), (0, lay.in_pad - lay.in_shard), (0, 0))).reshape(N_DEV * lay.in_pad, D)


def _unpack_layer(lay, gathered, own=None):
    D = lay.D
    w = {k: lay.take(gathered, k, own) for k in ("g1", "u1", "d1", "kv", "out", "g2", "u2", "d2")}
    w["win"] = _align_win(lay, lay.take(gathered, "win", own))
    fl0 = N_DEV * lay.in_shard - N_FOX_HEADS
    w["wfl"] = w["win"][fl0:fl0 + LANE]
    gate = lay.take(gathered, "gate", own)
    w["gate"] = gate
    w["gate3"] = [gate[i * D:(i + 1) * D] for i in range(3)]
    br = lay.take(gathered, "br", own).reshape(N_DEV * lay.br_shard, lay.br_in)
    third = lay.br_in // 3
    w["br3"] = [br[:, i * third:(i + 1) * third] for i in range(3)]
    return w


def _silu_mul(accs, _):
    a, b = accs
    return [a, b, a * jax.nn.sigmoid(a) * b]


def _act_bwd(accs, extras):
    ds, (a, b) = accs[0], extras
    sig = jax.nn.sigmoid(a)
    return [ds * b * (sig * (1.0 + a * (1.0 - sig))), ds * (a * sig)]


def _ffn_fwd(tag, h, pre_g, post_g, wg, wu, wd):
    n = _rms_fwd("ffn_norm_" + tag, h, pre_g, BF16)
    a, b, s = _mm("ffn_up_" + tag, [(n, wg), (n, wu)], "nt", [F32, F32, BF16], _silu_mul, tn=1408)
    f = _mm("ffn_down_" + tag, [(s, wd)], "nn", [F32])
    out = _rms_fwd("ffn_out_" + tag, f, post_g, F32, res=h, scale=0.5)
    return out, (h, n, a, b, s, f)


def _ffn_bwd(tag, dh, saved, pre_g, post_g, wg, wu, wd):
    h, n, a, b, s, f = saved
    df, d_post = _rms_bwd("ffn_dout_" + tag, f, post_g, dh, BF16, scale=0.5)
    da, db = _mm("ffn_dact_" + tag, [(df, wd)], "nt", [BF16, BF16], _act_bwd, [(a, 0), (b, 0)], tn=1408)
    d_wd = _mm("ffn_dwd_" + tag, [(s, df)], "tn", [BF16], tm=256)
    dn = _mm("ffn_dn_" + tag, [(da, wg), (db, wu)], "nn", [F32], _sum_accs)
    d_wg = _mm("ffn_dwg_" + tag, [(da, n)], "tn", [BF16], tm=256)
    d_wu = _mm("ffn_dwu_" + tag, [(db, n)], "tn", [BF16], tm=256)
    dh_in, d_pre = _rms_bwd("ffn_dnorm_" + tag, h, pre_g, dn, F32, res=dh)
    return dh_in, d_pre, d_post, d_wg, d_wu, d_wd


_SB_W = N_SB_HEADS * HEAD_DIM
_FOX_W = N_FOX_HEADS * HEAD_DIM
_QKV_W = 3 * _SB_W + 3 * _FOX_W


def _gate_act(accs, extras):
    return [jax.nn.sigmoid(accs[0] + extras[0])]


def _merge(accs, extras):
    return [extras[0] * accs[0] + extras[1] * accs[1] + extras[2] * accs[2]]


def _merge_bwd(accs, extras):
    dm = accs[0]
    d_branch = [dm * gi for gi in extras]
    d_gate = [dm * bi * gi * (1.0 - gi) for bi, gi in zip(accs[1:], extras)]
    return d_branch + d_gate


def _mix_tiles(lay):
    sb, fx = _SB_W // LANE, _FOX_W // LANE
    mem_w = N_DEV * lay.in_shard - _QKV_W - N_FOX_HEADS
    return (0, sb, 2 * sb, sb), (3 * sb, 3 * sb + fx, 3 * sb + 2 * fx, fx), (_QKV_W // LANE, mem_w // LANE)


def _mix_fwd(lay, h, w, pre_g, post_g, b_forget, b_gate, mem_n):
    D = lay.D
    (sq, sk, sv, sn), (fq, fk, fv, fn), (mq, mn) = _mix_tiles(lay)
    mem_d = mn * LANE // N_MEM_HEADS
    u = _rms_fwd("mix_norm", h, pre_g, BF16)
    proj = _mm("mix_in", [(u, w["win"])], "nt", [BF16], tm=1024, tn=IN_TILE)
    fl = _mm("mix_fl", [(u, w["wfl"])], "nt", [F32])[:, :N_FOX_HEADS].T
    c = _decay_fwd(fl, b_forget.reshape(-1, 1))
    _, o_sb, rtot = _sb_fwd((proj, sq), (proj, sk), (proj, sv), sn, HEAD_DIM, HEAD_DIM ** -0.5)
    o_fx32, o_fx, lse_fx = _attn_fwd("fox_fwd", (proj, fq), (proj, fk), (proj, fv), fn, HEAD_DIM,
                                     HEAD_DIM ** -0.5, c)
    kvm = _mm("mem_kv", [(mem_n, w["kv"])], "nn", [BF16])
    o_mem32, o_mem, lse_mem = _attn_fwd("mem_fwd", (proj, mq), (kvm, 0), (kvm, mn), mn, mem_d, mem_d ** -0.5)
    gates = _mm("mix_gate", [(u, w["gate"])], "nt", [F32], _gate_act, [(b_gate.reshape(1, -1), 0)], tm=1024)
    flat = [o_sb, o_fx, o_mem]
    merged = _mm("mix_merge", list(zip(flat, w["br3"])), "nt", [BF16], _merge,
                 [(gates, 0), (gates, D), (gates, 2 * D)])
    z = _mm("mix_out", [(merged, w["out"])], "nn", [F32])
    out = _rms_fwd("mix_res", z, post_g, F32, res=h)
    saved = (h, u, proj, fl, c, rtot, o_fx32, lse_fx, kvm, o_mem32, lse_mem, gates, flat, merged, z)
    return out, saved


def _mix_bwd(lay, dh, saved, w, pre_g, post_g, b_forget, mem_n, dmem_n):
    D = lay.D
    (sq, sk, sv, sn), (fq, fk, fv, fn), (mq, mn) = _mix_tiles(lay)
    mem_d = mn * LANE // N_MEM_HEADS
    h, u, proj, fl, c, rtot, o_fx32, lse_fx, kvm, o_mem32, lse_mem, gates, flat, merged, z = saved
    dz, d_post = _rms_bwd("mix_dres", z, post_g, dh, BF16)
    outs = _mm("mix_dmerge", [(dz, w["out"])] + list(zip(flat, w["br3"])), "nt", [BF16] * 6, _merge_bwd,
               [(gates, 0), (gates, D), (gates, 2 * D)], tn=512)
    d_branch, d_gate = outs[:3], outs[3:]
    d_wout = _mm("mix_dwout", [(merged, dz)], "tn", [BF16])
    d_o = [_mm("mix_dbr%d" % i, [(d_branch[i], w["br3"][i])], "nn", [BF16]) for i in range(3)]
    d_wbr = [_mm("mix_dwbr%d" % i, [(d_branch[i], flat[i])], "tn", [BF16]) for i in range(3)]
    d_bgate = jnp.concatenate([_colsum("mix_dbgate%d" % i, d_gate[i]) for i in range(3)])
    d_wgate = [_mm("mix_dwgate%d" % i, [(d_gate[i], u)], "tn", [BF16]) for i in range(3)]

    d_sb = _sb_bwd((proj, sq), (proj, sk), (proj, sv), d_o[0], rtot, sn, HEAD_DIM, HEAD_DIM ** -0.5)
    *d_fx, dc, dc_rows = _attn_bwd("fox_bwd", (proj, fq), (proj, fk), (proj, fv), o_fx32, d_o[1], lse_fx, fn,
                                   HEAD_DIM, HEAD_DIM ** -0.5, c)
    dq_m, dk_m, dv_m = _attn_bwd("mem_bwd", (proj, mq), (kvm, 0), (kvm, mn), o_mem32, d_o[2], lse_mem, mn,
                                 mem_d, mem_d ** -0.5)
    dfl, d_bforget = _decay_bwd(dc, dc_rows, fl, b_forget.reshape(-1, 1))
    pieces = list(d_sb) + list(d_fx) + [dq_m]
    dflp = jnp.pad(dfl.T.astype(BF16), ((0, 0), (0, LANE - dfl.shape[0])))
    offs = [sum(t.shape[1] for t in pieces[:i]) for i in range(len(pieces) + 1)]
    win_rows = [w["win"][offs[i]:offs[i + 1]] for i in range(len(pieces))]
    du = _mm("mix_du", list(zip(d_gate, w["gate3"])) + list(zip(pieces, win_rows)) + [(dflp, w["wfl"])], "nn",
             [F32], _sum_accs, tm=256, tn=512)
    d_rows = [_mm("mix_dwin%d" % i, [(t, u)], "tn", [BF16]) for i, t in enumerate(pieces)]
    d_wfl = _mm("mix_dwfl", [(dflp, u)], "tn", [BF16])
    d_win = _unalign_win(lay, _pad_rows(jnp.concatenate(list(d_rows) + [d_wfl], axis=0), lay.in_cols))
    dh_in, d_pre = _rms_bwd("mix_dnorm", h, pre_g, du, F32, res=dh)

    dkvm = jnp.concatenate([dk_m, dv_m], axis=1)
    d_wkv = _mm("mem_dwkv", [(mem_n, dkvm)], "tn", [BF16])
    dmem_n = _mm("mem_dn", [(dkvm, w["kv"])], "nt", [F32], lambda accs, ex: [accs[0] + ex[0]], [(dmem_n, 0)])
    grads = {"win": d_win, "kv": d_wkv, "gate": jnp.concatenate(d_wgate, axis=0),
             "br": jnp.concatenate(d_wbr, axis=1), "out": d_wout}
    return dh_in, d_pre, d_post, d_bforget, d_bgate, grads, dmem_n


def _layer_fwd(lay, h, w, sp, mem_n):
    h1, s1 = _ffn_fwd("1", h, sp["ffn1_pre_g"], sp["ffn1_post_g"], w["g1"], w["u1"], w["d1"])
    h2, s2 = _mix_fwd(lay, h1, w, sp["mix_pre_g"], sp["mix_post_g"], sp["b_forget"], sp["b_gate"], mem_n)
    h3, s3 = _ffn_fwd("2", h2, sp["ffn2_pre_g"], sp["ffn2_post_g"], w["g2"], w["u2"], w["d2"])
    return h3, (s1, s2, s3)


def _layer_bwd(lay, dh, saved, w, sp, mem_n, dmem_n):
    s1, s2, s3 = saved
    dh, d_pre2, d_post2, d_g2, d_u2, d_d2 = _ffn_bwd("2", dh, s3, sp["ffn2_pre_g"], sp["ffn2_post_g"],
                                                     w["g2"], w["u2"], w["d2"])
    dh, d_mpre, d_mpost, d_bforget, d_bgate, g, dmem_n = _mix_bwd(
        lay, dh, s2, w, sp["mix_pre_g"], sp["mix_post_g"], sp["b_forget"], mem_n, dmem_n)
    dh, d_pre1, d_post1, d_g1, d_u1, d_d1 = _ffn_bwd("1", dh, s1, sp["ffn1_pre_g"], sp["ffn1_post_g"],
                                                     w["g1"], w["u1"], w["d1"])
    g.update({"g1": d_g1, "u1": d_u1, "d1": d_d1, "g2": d_g2, "u2": d_u2, "d2": d_d2})
    g["br"] = g["br"].reshape(N_DEV, lay.br_shard, lay.br_in).reshape(-1, lay.D)
    packed = jnp.concatenate([lay.spread(g[key], key) for key in lay.seg], axis=1)
    small = {"ffn1_pre_g": d_pre1, "ffn1_post_g": d_post1, "mix_pre_g": d_mpre, "mix_post_g": d_mpost,
             "ffn2_pre_g": d_pre2, "ffn2_post_g": d_post2, "b_gate": d_bgate, "b_forget": d_bforget}
    return dh, packed, small, dmem_n


_SHARDED = ["ffn1_w_gate", "ffn1_w_up", "ffn1_w_down", "w_in", "w_mem_kv", "w_gate", "w_br_sb", "w_br_fox",
            "w_br_mem", "w_out", "ffn2_w_gate", "ffn2_w_up", "ffn2_w_down"]
_SMALL_LAYER = ["ffn1_pre_g", "ffn1_post_g", "mix_pre_g", "mix_post_g", "ffn2_pre_g", "ffn2_post_g", "b_gate",
                "b_forget"]
_WEIGHTS = ["ffn1_pre_g", "ffn1_post_g", "ffn1_w_gate", "ffn1_w_up", "ffn1_w_down", "mix_pre_g", "mix_post_g",
            "w_in", "b_forget", "mem_norm_g", "w_mem_kv", "w_gate", "b_gate", "w_br_sb", "w_br_fox", "w_br_mem",
            "w_out", "ffn2_pre_g", "ffn2_post_g", "ffn2_w_gate", "ffn2_w_up", "ffn2_w_down"]


def _pack_small(vals, L, D):
    rows = []
    for l in range(L):
        for name in _SMALL_LAYER:
            t = vals[name][l]
            rows.append(jnp.pad(t, (0, -t.shape[0] % D)).reshape(-1, D))
    rows.append(vals["mem_norm_g"].reshape(1, D))
    packed = jnp.concatenate(rows, axis=0)
    return _pad_rows(packed, -(-packed.shape[0] // 8) * 8)


def _unpack_small(packed, shapes, L, D):
    out = {name: [] for name in _SMALL_LAYER}
    r = 0
    for l in range(L):
        for name in _SMALL_LAYER:
            n = shapes[name][1]
            nr = -(-n // D)
            out[name].append(packed[r:r + nr].reshape(-1)[:n])
            r += nr
    res = {name: jnp.stack(v) for name, v in out.items()}
    res["mem_norm_g"] = packed[r]
    return res


def _unpack_grads(lay, g, l_shapes):
    def seg(key):
        off, n = lay.seg[key]
        return g[off:off + n]
    br = seg("br").reshape(lay.br_shard, lay.br_in).T
    third = lay.br_in // 3
    return {
        "ffn1_w_gate": seg("g1").T, "ffn1_w_up": seg("u1").T, "ffn1_w_down": seg("d1"),
        "w_in": seg("win")[:lay.in_shard].T, "w_mem_kv": seg("kv"), "w_gate": seg("gate").T,
        "w_br_sb": br[:third], "w_br_fox": br[third:2 * third], "w_br_mem": br[2 * third:],
        "w_out": seg("out"), "ffn2_w_gate": seg("g2").T, "ffn2_w_up": seg("u2").T, "ffn2_w_down": seg("d2"),
    }


class _Exchanges:
    def gather(self, name, block):
        return _all_gather(name, block)

    def gather_start(self, block):
        return _exchange_start("ag_start", block, per_peer=False)

    def gather_wait(self, started, after, block):
        return _exchange_wait("ag_wait", started, after), (block, _my_index())

    def scatter(self, parts):
        return _reduce_scatter("w", parts)

    def scatter_start(self, parts):
        return _exchange_start("rs_start", parts, per_peer=True)

    def scatter_wait(self, started, after, parts):
        return _sum_landed("rs_sum8", _exchange_wait("rs_wait", started, after), parts)

    def token(self, started):
        return started[4]

    def loss_sum(self, part):
        return lax.psum(part, ("x", "y", "c"))


def _step(p, m, v, x, mem, tgt, ex):
    L, D = p["ffn1_pre_g"].shape
    lay = _Layout(D, p["ffn1_w_gate"].shape[2], p["w_in"].shape[2], p["w_mem_kv"].shape[1], p["w_gate"].shape[2],
                  3 * p["w_br_sb"].shape[1], p["w_br_sb"].shape[2], p["w_out"].shape[1])
    blocks = [_pack_layer(lay, l, p) for l in range(L)]
    sps = [{name: p[name][l] for name in _SMALL_LAYER} for l in range(L)]

    mem_n = _rms_fwd("mem_norm", mem, p["mem_norm_g"], BF16)
    gathered, own = ex.gather("ag_weights", blocks[0]), None
    h, saved, ws = x, [], []
    for l in range(L):
        if l + 1 < L:
            nxt, gathered = lax.optimization_barrier((blocks[l + 1], gathered))
            started = ex.gather_start(nxt)
            sp = _after(sps[l], "ffn1_pre_g", ex.token(started))
        else:
            sp = sps[l]
        ws.append(_unpack_layer(lay, gathered, own))
        h, s = _layer_fwd(lay, h, ws[l], sp, mem_n)
        saved.append(s)
        if l + 1 < L:
            gathered, own = ex.gather_wait(started, h, blocks[l + 1])
    loss_part, dh = _loss_grad(h, tgt)
    loss = ex.loss_sum(loss_part)

    dmem_n = jnp.zeros(mem.shape, F32)
    big, small = [None] * L, {name: [None] * L for name in _SMALL_LAYER}
    flying, token = {}, None
    for l in reversed(range(L)):
        sp = sps[l] if token is None else _after(sps[l], "ffn2_post_g", token)
        dh, packed, sm, dmem_n = _layer_bwd(lay, dh, saved[l], ws[l], sp, mem_n, dmem_n)
        if l > 0:
            flying[l] = (ex.scatter_start(packed), packed)
            token = ex.token(flying[l][0])
        else:
            big[l] = _unpack_grads(lay, ex.scatter(packed), None)
        for name in _SMALL_LAYER:
            small[name][l] = sm[name]
    for l, (started, packed) in flying.items():
        big[l] = _unpack_grads(lay, ex.scatter_wait(started, dh, packed), None)
    _, d_memg = _rms_bwd("mem_dnorm", mem, p["mem_norm_g"], dmem_n, F32)

    small_g = {name: jnp.stack(vs) for name, vs in small.items()}
    small_g["mem_norm_g"] = d_memg
    small_names = _SMALL_LAYER + ["mem_norm_g"]
    shapes = {name: p[name].shape for name in small_names}
    g_all = ex.gather("ag_small", _pack_small(small_g, L, D))
    packs = [_pack_small({name: t[name] for name in small_names}, L, D) for t in (p, m, v)]
    res = [_unpack_small(t, shapes, L, D) for t in _adamw("adamw_small", g_all, *packs)]

    out = {kind: {} for kind in ("grad", "delta", "new_m", "new_v")}
    for name in small_names:
        for kind, r in zip(("grad", "delta", "new_m", "new_v"), res):
            out[kind][name] = r[name].reshape(p[name].shape)
    for name in _SHARDED:
        g = jnp.stack([big[l][name] for l in range(L)])
        shp = g.shape
        flat = lambda t: t.reshape(-1, shp[-1])
        r = _adamw("adamw_" + name, flat(g)[None], flat(p[name]), flat(m[name]), flat(v[name]))
        for kind, t in zip(("grad", "delta", "new_m", "new_v"), r):
            out[kind][name] = t.reshape(shp)
    return loss, dh, out


def kernel(x, mem, ffn1_pre_g, ffn1_post_g, ffn1_w_gate, ffn1_w_up, ffn1_w_down, mix_pre_g, mix_post_g, w_in, b_forget, mem_norm_g, w_mem_kv, w_gate, b_gate, w_br_sb, w_br_fox, w_br_mem, w_out, ffn2_pre_g, ffn2_post_g, ffn2_w_gate, ffn2_w_up, ffn2_w_down, loss_target, m_ffn1_pre_g, m_ffn1_post_g, m_ffn1_w_gate, m_ffn1_w_up, m_ffn1_w_down, m_mix_pre_g, m_mix_post_g, m_w_in, m_b_forget, m_mem_norm_g, m_w_mem_kv, m_w_gate, m_b_gate, m_w_br_sb, m_w_br_fox, m_w_br_mem, m_w_out, m_ffn2_pre_g, m_ffn2_post_g, m_ffn2_w_gate, m_ffn2_w_up, m_ffn2_w_down, v_ffn1_pre_g, v_ffn1_post_g, v_ffn1_w_gate, v_ffn1_w_up, v_ffn1_w_down, v_mix_pre_g, v_mix_post_g, v_w_in, v_b_forget, v_mem_norm_g, v_w_mem_kv, v_w_gate, v_b_gate, v_w_br_sb, v_w_br_fox, v_w_br_mem, v_w_out, v_ffn2_pre_g, v_ffn2_post_g, v_ffn2_w_gate, v_ffn2_w_up, v_ffn2_w_down):
    p = dict(zip(_WEIGHTS, (ffn1_pre_g, ffn1_post_g, ffn1_w_gate, ffn1_w_up, ffn1_w_down, mix_pre_g, mix_post_g, w_in, b_forget, mem_norm_g, w_mem_kv, w_gate, b_gate, w_br_sb, w_br_fox, w_br_mem, w_out, ffn2_pre_g, ffn2_post_g, ffn2_w_gate, ffn2_w_up, ffn2_w_down)))
    m = dict(zip(_WEIGHTS, (m_ffn1_pre_g, m_ffn1_post_g, m_ffn1_w_gate, m_ffn1_w_up, m_ffn1_w_down, m_mix_pre_g, m_mix_post_g, m_w_in, m_b_forget, m_mem_norm_g, m_w_mem_kv, m_w_gate, m_b_gate, m_w_br_sb, m_w_br_fox, m_w_br_mem, m_w_out, m_ffn2_pre_g, m_ffn2_post_g, m_ffn2_w_gate, m_ffn2_w_up, m_ffn2_w_down)))
    v = dict(zip(_WEIGHTS, (v_ffn1_pre_g, v_ffn1_post_g, v_ffn1_w_gate, v_ffn1_w_up, v_ffn1_w_down, v_mix_pre_g, v_mix_post_g, v_w_in, v_b_forget, v_mem_norm_g, v_w_mem_kv, v_w_gate, v_b_gate, v_w_br_sb, v_w_br_fox, v_w_br_mem, v_w_out, v_ffn2_pre_g, v_ffn2_post_g, v_ffn2_w_gate, v_ffn2_w_up, v_ffn2_w_down)))
    loss, dx, out = _step(p, m, v, x[0], mem[0], loss_target[0], _Exchanges())
    return (loss, dx[None], *[out["grad"][n] for n in _WEIGHTS], *[out["delta"][n] for n in _WEIGHTS],
            *[out["new_m"][n] for n in _WEIGHTS], *[out["new_v"][n] for n in _WEIGHTS])
```

```python
import functools
import math

import jax
import jax.numpy as jnp
from jax import lax
from jax.experimental import pallas as pl
from jax.experimental.pallas import tpu as pltpu

F32 = jnp.float32
BF16 = jnp.bfloat16

LANE = 128
SUBLANE_BF16 = 16
VMEM_LIMIT = 56 * 1024 * 1024
N_DEV = 8
MESH = pl.DeviceIdType.MESH
ANY = pl.BlockSpec(memory_space=pl.ANY)

RMS_EPS = 1e-6
HEAD_DIM = 64
N_SB_HEADS = 8
N_FOX_HEADS = 8
N_MEM_HEADS = 4
NEG = -1e30
ATT_TQ = 512
ATT_TK = 256
DECAY_TK = 128
IN_TILE = 1280

ADAM_LR = 0.001
ADAM_B1 = 0.9
ADAM_B2 = 0.999
ADAM_EPS = 1e-08
ADAM_WD = 0.01
ADAM_STEP = 10


def _tile(n, target, mult=LANE):
    best = None
    for t in range(mult, min(n, target) + 1, mult):
        if n % t == 0:
            best = t
    return best if best is not None else n


def _cparams(sem):
    return pltpu.CompilerParams(dimension_semantics=sem, vmem_limit_bytes=VMEM_LIMIT)


_DIMS = {"nn": (((1,), (0,)), ((), ())), "nt": (((1,), (1,)), ((), ())), "tn": (((0,), (0,)), ((), ()))}


def _dot(a, b, mode="nn"):
    return lax.dot_general(a.astype(BF16), b.astype(BF16), _DIMS[mode], preferred_element_type=F32)


def _mm(name, pairs, mode, out_dtypes, epilogue=None, extras=(), tm=512, tn=1024):
    a0, b0 = pairs[0]
    M = a0.shape[1] if mode == "tn" else a0.shape[0]
    N = b0.shape[0] if mode == "nt" else b0.shape[1]
    tm = _tile(M, tm)
    tn = _tile(N, tn)
    np_, ne, no = len(pairs), len(extras), len(out_dtypes)

    def body(*refs):
        a_refs, b_refs = refs[:np_], refs[np_:2 * np_]
        e_refs = refs[2 * np_:2 * np_ + ne]
        o_refs = refs[2 * np_ + ne:]
        accs = [_dot(a[...], b[...], mode) for a, b in zip(a_refs, b_refs)]
        outs = epilogue(accs, [e[...] for e in e_refs]) if epilogue is not None else accs
        for o, val in zip(o_refs, outs):
            o[...] = val.astype(o.dtype)

    in_specs = []
    for a, _ in pairs:
        if mode == "tn":
            in_specs.append(pl.BlockSpec((a.shape[0], tm), lambda j, i: (0, i)))
        else:
            in_specs.append(pl.BlockSpec((tm, a.shape[1]), lambda j, i: (i, 0)))
    for _, b in pairs:
        if mode == "nt":
            in_specs.append(pl.BlockSpec((tn, b.shape[1]), lambda j, i: (j, 0)))
        else:
            in_specs.append(pl.BlockSpec((b.shape[0], tn), lambda j, i: (0, j)))
    for e, off in extras:
        if e.shape[0] == 1:
            in_specs.append(pl.BlockSpec((1, tn), functools.partial(lambda j, i, o: (0, j + o), o=off // tn)))
        else:
            in_specs.append(pl.BlockSpec((tm, tn), functools.partial(lambda j, i, o: (i, j + o), o=off // tn)))
    out_specs = [pl.BlockSpec((tm, tn), lambda j, i: (i, j)) for _ in range(no)]
    outs = pl.pallas_call(
        body, name=name, grid=(N // tn, M // tm),
        in_specs=in_specs, out_specs=out_specs,
        out_shape=[jax.ShapeDtypeStruct((M, N), dt) for dt in out_dtypes],
        compiler_params=_cparams(("parallel", "parallel")),
    )(*[a for a, _ in pairs], *[b for _, b in pairs], *[e for e, _ in extras])
    return outs[0] if no == 1 else outs


def _sum_accs(accs, _):
    total = accs[0]
    for acc in accs[1:]:
        total = total + acc
    return [total]


def _rstd(x):
    return lax.rsqrt(jnp.mean(x * x, axis=-1, keepdims=True) + RMS_EPS)


def _rms_fwd(name, x, g, out_dtype, res=None, scale=1.0, tr=512):
    R, D = x.shape
    tr = _tile(R, tr, 8)
    has_res = res is not None

    def body(*refs):
        x_ref, g_ref = refs[:2]
        o_ref = refs[-1]
        xv = x_ref[...]
        y = (xv * _rstd(xv)) * g_ref[...]
        if has_res:
            y = refs[2][...] + scale * y
        o_ref[...] = y.astype(o_ref.dtype)

    row = pl.BlockSpec((tr, D), lambda i: (i, 0))
    gain = pl.BlockSpec((1, D), lambda i: (0, 0))
    return pl.pallas_call(
        body, name=name, grid=(R // tr,),
        in_specs=[row, gain] + ([row] if has_res else []), out_specs=row,
        out_shape=jax.ShapeDtypeStruct((R, D), out_dtype),
        compiler_params=_cparams(("parallel",)),
    )(x, g.reshape(1, D), *([res] if has_res else []))


def _rms_bwd(name, x, g, dy, out_dtype, scale=1.0, res=None, tr=512):
    R, D = x.shape
    tr = _tile(R, tr, 8)
    has_res = res is not None

    def body(*refs):
        x_ref, g_ref, dy_ref = refs[:3]
        dx_ref, dg_ref = refs[-2:]
        i = pl.program_id(0)
        xv = x_ref[...]
        xhat = xv * _rstd(xv)
        dyv = dy_ref[...].astype(F32) * scale
        gy = dyv * g_ref[...]
        dx = _rstd(xv) * (gy - xhat * jnp.mean(gy * xhat, axis=-1, keepdims=True))
        if has_res:
            dx = refs[3][...] + dx
        dx_ref[...] = dx.astype(dx_ref.dtype)
        part = jnp.sum(dyv * xhat, axis=0, keepdims=True)

        @pl.when(i == 0)
        def _():
            dg_ref[...] = part

        @pl.when(i > 0)
        def _():
            dg_ref[...] += part

    row = pl.BlockSpec((tr, D), lambda i: (i, 0))
    gain = pl.BlockSpec((1, D), lambda i: (0, 0))
    dx, dg = pl.pallas_call(
        body, name=name, grid=(R // tr,),
        in_specs=[row, gain, row] + ([row] if has_res else []), out_specs=[row, gain],
        out_shape=[jax.ShapeDtypeStruct((R, D), out_dtype), jax.ShapeDtypeStruct((1, D), F32)],
        compiler_params=_cparams(("arbitrary",)),
    )(x, g.reshape(1, D), dy, *([res] if has_res else []))
    return dx, dg[0]


def _loss_grad(y, tgt, tr=512):
    R, D = y.shape
    tr = _tile(R, tr, 8)

    def body(y_ref, t_ref, dy_ref, loss_ref):
        i = pl.program_id(0)
        d = y_ref[...] - t_ref[...]
        dy_ref[...] = d / D
        part = 0.5 * jnp.sum(jnp.mean(d * d, axis=-1, keepdims=True), axis=0, keepdims=True)
        tile = jnp.broadcast_to(part, loss_ref.shape)

        @pl.when(i == 0)
        def _():
            loss_ref[...] = tile

        @pl.when(i > 0)
        def _():
            loss_ref[...] += tile

    row = pl.BlockSpec((tr, D), lambda i: (i, 0))
    dy, loss = pl.pallas_call(
        body, name="loss_grad", grid=(R // tr,),
        in_specs=[row, row], out_specs=[row, pl.BlockSpec((8, LANE), lambda i: (0, 0))],
        out_shape=[jax.ShapeDtypeStruct((R, D), F32), jax.ShapeDtypeStruct((8, LANE), F32)],
        compiler_params=_cparams(("arbitrary",)),
    )(y, tgt)
    return loss[0, 0], dy


def _colsum(name, x, tr=512, tn=1024):
    R, N = x.shape
    tr, tn = _tile(R, tr, 8), _tile(N, tn)

    def body(x_ref, o_ref):
        i = pl.program_id(1)
        part = jnp.sum(x_ref[...].astype(F32), axis=0, keepdims=True)

        @pl.when(i == 0)
        def _():
            o_ref[...] = part

        @pl.when(i > 0)
        def _():
            o_ref[...] += part

    out = pl.pallas_call(
        body, name=name, grid=(N // tn, R // tr),
        in_specs=[pl.BlockSpec((tr, tn), lambda j, i: (i, j))], out_specs=pl.BlockSpec((1, tn), lambda j, i: (0, j)),
        out_shape=jax.ShapeDtypeStruct((1, N), F32),
        compiler_params=_cparams(("parallel", "arbitrary")),
    )(x)
    return out[0]


def _tri(tk, rel):
    j = lax.broadcasted_iota(jnp.int32, (tk, tk), 0)
    s = lax.broadcasted_iota(jnp.int32, (tk, tk), 1)
    return rel(j, s).astype(BF16)


def _dot_split(x, m, parts=2):
    total = None
    rem = x
    for _ in range(parts):
        piece = rem.astype(BF16)
        rem = rem - piece.astype(F32)
        term = jnp.dot(piece, m, preferred_element_type=F32)
        total = term if total is None else total + term
    return total


def _log_not_and_beta(z, mask):
    ln = -(jnp.maximum(z, 0.0) + jnp.log(1.0 + jnp.exp(-jnp.abs(z))))
    return (ln if mask is None else jnp.where(mask, ln, 0.0)), ln + z


def _att_tiles(T, Tk, causal):
    tq = min(ATT_TQ, T)
    tk = min(ATT_TK, tq if causal else Tk)
    return tq, tk, (tq if causal else Tk) // tk


def _key_base(j, tq):
    return j * tq if isinstance(j, int) else pl.multiple_of(j * tq, tq)


def _is_pow2(scale):
    return math.log2(scale).is_integer()


def _per_head(x, hpb, d):
    if hpb == 1:
        return [x]
    lane = lax.broadcasted_iota(jnp.int32, x.shape, 1)
    return [jnp.where((lane >= h * d) & (lane < (h + 1) * d), x, jnp.zeros_like(x)) for h in range(hpb)]


def _join_heads(xs, d):
    out = xs[-1]
    if len(xs) > 1:
        lane = lax.broadcasted_iota(jnp.int32, out.shape, 1)
        for h in reversed(range(len(xs) - 1)):
            out = jnp.where(lane < (h + 1) * d, xs[h], out)
    return out


def _lane_tile(rows, off, whole):
    if whole:
        return pl.BlockSpec((rows, LANE), lambda g, i: (0, off + g))
    return pl.BlockSpec((rows, LANE), lambda g, i: (i, off + g))


def _sb_fwd(q, k, v, n_tiles, d, scale):
    T = q[0].shape[0]
    hpb = LANE // d
    tq, tk, nsub = _att_tiles(T, T, True)
    assert _is_pow2(scale)

    def body(q_ref, k_ref, v_ref, ob_ref, rt_ref, acc_ref, r_ref):
        qi = pl.program_id(1)
        qh = _per_head(q_ref[...] * scale, hpb, d)
        acc_ref[...] = jnp.zeros_like(acc_ref)
        r_ref[...] = jnp.zeros_like(r_ref)
        row = lax.broadcasted_iota(jnp.int32, (tq, tk), 0)
        col = lax.broadcasted_iota(jnp.int32, (tq, tk), 1)
        after = _tri(tk, lambda j, s: j > s)

        def step(j, diagonal):
            base = _key_base(j, tq)
            for h in range(hpb):
                parts = []
                for u in reversed(range(nsub)):
                    z = _dot(qh[h], k_ref[pl.ds(base + u * tk, tk), :], "nt")
                    mask = (col + u * tk) < row if diagonal else None
                    ln, lb = _log_not_and_beta(z, mask)
                    between = _dot_split(ln, after, parts=1)
                    first = ln[:, 0:1].astype(BF16).astype(F32)
                    parts.append((u, lb, between, between[:, 0:1] + first, mask))
                r = r_ref[h]
                out = None
                for u, lb, between, total, mask in parts:
                    w = jnp.exp(lb + between + r)
                    if diagonal:
                        w = jnp.where(mask, w, 0.0)
                    term = _dot(w, v_ref[pl.ds(base + u * tk, tk), :])
                    out = term if out is None else out + term
                    r = r + total
                acc_ref[h] += out
                r_ref[h] = r

        def below(i, carry):
            step(qi - 1 - i, False)
            return carry

        step(qi, True)
        lax.fori_loop(0, qi, below, 0)
        ob_ref[...] = _join_heads([acc_ref[h] for h in range(hpb)], d).astype(ob_ref.dtype)
        rt_ref[...] = r_ref[...]

    out = pl.BlockSpec((tq, LANE), lambda g, i: (i, g))
    col = pl.BlockSpec((hpb, tq, 1), lambda g, i: (g, i, 0))
    return pl.pallas_call(
        body, name="sb_fwd", grid=(n_tiles, T // tq),
        in_specs=[_lane_tile(tq, q[1], False), _lane_tile(T, k[1], True), _lane_tile(T, v[1], True)],
        out_specs=[out, col],
        out_shape=[jax.ShapeDtypeStruct((T, n_tiles * LANE), BF16), jax.ShapeDtypeStruct((n_tiles * hpb, T, 1), F32)],
        scratch_shapes=[pltpu.VMEM((hpb, tq, LANE), F32), pltpu.VMEM((hpb, tq, 1), F32)],
        compiler_params=_cparams(("parallel", "arbitrary")),
    )(q[0], k[0], v[0])


def _sb_bwd(q, k, v, do, rtot, n_tiles, d, scale):
    T = q[0].shape[0]
    hpb = LANE // d
    tq, tk, nsub = _att_tiles(T, T, True)
    assert _is_pow2(scale)

    def body(q_ref, k_ref, v_ref, do_ref, rt_ref, dq_ref, dk_ref, dv_ref, dk_acc, dv_acc, dq_acc, p_ref, c_ref):
        qi = pl.program_id(1)

        @pl.when(qi == 0)
        def _():
            dk_acc[...] = jnp.zeros_like(dk_acc)
            dv_acc[...] = jnp.zeros_like(dv_acc)

        qh = _per_head(q_ref[...] * scale, hpb, d)
        doh = _per_head(do_ref[...], hpb, d)
        dq_acc[...] = jnp.zeros_like(dq_acc)
        p_ref[...] = jnp.zeros_like(p_ref)
        c_ref[...] = jnp.zeros_like(c_ref)
        row = lax.broadcasted_iota(jnp.int32, (tq, tk), 0)
        col = lax.broadcasted_iota(jnp.int32, (tq, tk), 1)
        upto = _tri(tk, lambda j, s: j <= s)
        before = _tri(tk, lambda j, s: j < s)
        rt_wide = [jnp.broadcast_to(rt_ref[h], (tq, tk)) for h in range(hpb)]

        def step(j, diagonal):
            base = _key_base(j, tq)
            for h in range(hpb):
                first = []
                for u in range(nsub):
                    ks = base + u * tk
                    kv = k_ref[pl.ds(ks, tk), :]
                    z = _dot(qh[h], kv, "nt")
                    mask = (col + u * tk) < row if diagonal else None
                    ln, lb = _log_not_and_beta(z, mask)
                    dw = _dot(doh[h], v_ref[pl.ds(ks, tk), :], "nt")
                    first.append((ks, kv, mask, lb, jnp.exp(lb), _dot_split(ln, upto, parts=1), dw))
                rt, pre, cpre = rt_wide[h], p_ref[h], c_ref[h]
                dq = None
                for ks, kv, mask, lb, sig, local, dw in first:
                    prefix = local + pre
                    w = jnp.exp(lb + (rt - prefix))
                    if diagonal:
                        w = jnp.where(mask, w, 0.0)
                    g = dw * w
                    c = _dot_split(g, before, parts=1) + cpre
                    dz = g * (1.0 - sig) - c * sig
                    if diagonal:
                        dz = jnp.where(mask, dz, 0.0)
                    term = _dot(dz, kv)
                    dq = term if dq is None else dq + term
                    dk_acc[pl.ds(ks, tk), :] += _dot(dz, qh[h], "tn")
                    dv_acc[pl.ds(ks, tk), :] += _dot(w, doh[h], "tn")
                    pre = prefix[:, tk - 1:tk]
                    cpre = c[:, tk - 1:tk] + g[:, tk - 1:tk]
                dq_acc[h] += dq
                p_ref[h] = pre
                c_ref[h] = cpre

        def below(j, carry):
            step(j, False)
            return carry

        lax.fori_loop(0, qi, below, 0)
        step(qi, True)
        dq_ref[...] = (_join_heads([dq_acc[h] for h in range(hpb)], d) * scale).astype(dq_ref.dtype)

        @pl.when(qi == pl.num_programs(1) - 1)
        def _():
            dk_ref[...] = dk_acc[...].astype(dk_ref.dtype)
            dv_ref[...] = dv_acc[...].astype(dv_ref.dtype)

    blk = pl.BlockSpec((tq, LANE), lambda g, i: (i, g))
    full = pl.BlockSpec((T, LANE), lambda g, i: (0, g))
    col = pl.BlockSpec((hpb, tq, 1), lambda g, i: (g, i, 0))
    wide = jax.ShapeDtypeStruct((T, n_tiles * LANE), BF16)
    return pl.pallas_call(
        body, name="sb_bwd", grid=(n_tiles, T // tq),
        in_specs=[_lane_tile(tq, q[1], False), _lane_tile(T, k[1], True), _lane_tile(T, v[1], True), blk, col],
        out_specs=[blk, full, full], out_shape=[wide, wide, wide],
        scratch_shapes=[pltpu.VMEM((T, LANE), F32), pltpu.VMEM((T, LANE), F32), pltpu.VMEM((hpb, tq, LANE), F32),
                        pltpu.VMEM((hpb, tq, 1), F32), pltpu.VMEM((hpb, tq, 1), F32)],
        compiler_params=_cparams(("parallel", "arbitrary")),
    )(q[0], k[0], v[0], do, rtot)


def _attn_fwd(name, q, k, v, n_tiles, d, scale, c=None):
    T, Tk = q[0].shape[0], k[0].shape[0]
    hpb = LANE // d
    H = n_tiles * hpb
    causal = c is not None
    tq, tk, nsub = _att_tiles(T, Tk, causal)
    fold = _is_pow2(scale)

    def body(*refs):
        q_ref, k_ref, v_ref = refs[:3]
        cc_ref, cr_ref = refs[3:5] if causal else (None, None)
        o_ref, ob_ref, lse_ref, m_ref, l_ref, acc_ref = refs[-6:]
        qi = pl.program_id(1)
        qh = _per_head(q_ref[...] * scale if fold else q_ref[...], hpb, d)
        bias = [jnp.broadcast_to(cc_ref[h], (tq, tk)) for h in range(hpb)] if causal else None
        ones = jnp.ones((tk, LANE), BF16)
        m_ref[...] = jnp.full_like(m_ref, NEG)
        l_ref[...] = jnp.zeros_like(l_ref)
        acc_ref[...] = jnp.zeros_like(acc_ref)
        row = lax.broadcasted_iota(jnp.int32, (tq, tk), 0)
        col = lax.broadcasted_iota(jnp.int32, (tq, tk), 1)

        def step(j, diagonal):
            base = _key_base(j, tq)
            for h in range(hpb):
                zs = []
                for u in range(nsub):
                    z = _dot(qh[h], k_ref[pl.ds(base + u * tk, tk), :], "nt")
                    if not fold:
                        z = z * scale
                    if causal:
                        z = z + bias[h] - cr_ref[h, j * nsub + u]
                    if diagonal:
                        z = jnp.where((col + u * tk) <= row, z, NEG)
                    zs.append(z)
                m_prev = m_ref[h]
                top = zs[0]
                for z in zs[1:]:
                    top = jnp.maximum(top, z)
                m_new = jnp.maximum(m_prev, jnp.max(top, axis=1, keepdims=True))
                alpha = jnp.exp(m_prev - m_new)
                l_new = alpha * l_ref[h]
                out = alpha * acc_ref[h]
                m_wide = jnp.broadcast_to(m_new, (tq, tk))
                for u, z in enumerate(zs):
                    p = jnp.exp(z - m_wide).astype(BF16)
                    l_new = l_new + jnp.dot(p, ones, preferred_element_type=F32)[:, 0:1]
                    out = out + _dot(p, v_ref[pl.ds(base + u * tk, tk), :])
                l_ref[h] = l_new
                acc_ref[h] = out
                m_ref[h] = m_new

        def below(j, carry):
            step(j, False)
            return carry

        if causal:
            lax.fori_loop(0, qi, below, 0)
            step(qi, True)
        else:
            step(0, False)
        o = _join_heads([acc_ref[h] / l_ref[h] for h in range(hpb)], d)
        o_ref[...] = o
        ob_ref[...] = o.astype(ob_ref.dtype)
        lse_ref[...] = m_ref[...] + jnp.log(l_ref[...])

    out = pl.BlockSpec((tq, LANE), lambda g, i: (i, g))
    col = pl.BlockSpec((hpb, tq, 1), lambda g, i: (g, i, 0))
    in_specs = [_lane_tile(tq, q[1], False), _lane_tile(Tk, k[1], True), _lane_tile(Tk, v[1], True)]
    args = [q[0], k[0], v[0]]
    if causal:
        in_specs += [col, pl.BlockSpec((hpb, T // tk, 1, tk), lambda g, i: (g, 0, 0, 0))]
        args += [c.reshape(H, T, 1), c.reshape(H, T // tk, 1, tk)]
    return pl.pallas_call(
        body, name=name, grid=(n_tiles, T // tq),
        in_specs=in_specs, out_specs=[out, out, col],
        out_shape=[jax.ShapeDtypeStruct((T, n_tiles * LANE), F32), jax.ShapeDtypeStruct((T, n_tiles * LANE), BF16),
                   jax.ShapeDtypeStruct((H, T, 1), F32)],
        scratch_shapes=[pltpu.VMEM((hpb, tq, 1), F32), pltpu.VMEM((hpb, tq, 1), F32),
                        pltpu.VMEM((hpb, tq, LANE), F32)],
        compiler_params=_cparams(("parallel", "arbitrary")),
    )(*args)


def _attn_bwd(name, q, k, v, o, do, lse, n_tiles, d, scale, c=None):
    T, Tk = q[0].shape[0], k[0].shape[0]
    hpb = LANE // d
    H = n_tiles * hpb
    causal = c is not None
    tq, tk, nsub = _att_tiles(T, Tk, causal)
    fold = _is_pow2(scale)

    def body(*refs):
        q_ref, k_ref, v_ref, o_ref, do_ref, lse_ref = refs[:6]
        cc_ref, cr_ref = refs[6:8] if causal else (None, None)
        n_out = 5 if causal else 3
        outs = refs[-(n_out + 3):-3]
        dq_ref, dk_ref, dv_ref = outs[:3]
        dc_ref, drow_ref = outs[3:5] if causal else (None, None)
        dk_acc, dv_acc, dq_acc = refs[-3:]
        qi = pl.program_id(1)

        @pl.when(qi == 0)
        def _():
            dk_acc[...] = jnp.zeros_like(dk_acc)
            dv_acc[...] = jnp.zeros_like(dv_acc)
            if causal:
                dc_ref[...] = jnp.zeros_like(dc_ref)

        qh = _per_head(q_ref[...] * scale if fold else q_ref[...], hpb, d)
        doh = _per_head(do_ref[...], hpb, d)
        delta_wide = [jnp.broadcast_to(jnp.sum(t.astype(F32) * o_ref[...], axis=1, keepdims=True), (tq, tk))
                      for t in doh]
        shift = [jnp.broadcast_to((cc_ref[h] - lse_ref[h]) if causal else -lse_ref[h], (tq, tk)) for h in range(hpb)]
        dq_acc[...] = jnp.zeros_like(dq_acc)
        if causal:
            drow_ref[...] = jnp.zeros_like(drow_ref)
        row = lax.broadcasted_iota(jnp.int32, (tq, tk), 0)
        col = lax.broadcasted_iota(jnp.int32, (tq, tk), 1)

        def step(j, diagonal):
            base = _key_base(j, tq)
            for h in range(hpb):
                dq, dsum = None, None
                for u in range(nsub):
                    ks = base + u * tk
                    kv = k_ref[pl.ds(ks, tk), :]
                    z = _dot(qh[h], kv, "nt")
                    if not fold:
                        z = z * scale
                    z = z + shift[h]
                    if causal:
                        z = z - cr_ref[h, j * nsub + u]
                    if diagonal:
                        z = jnp.where((col + u * tk) <= row, z, NEG)
                    p = jnp.exp(z)
                    ds = p * (_dot(doh[h], v_ref[pl.ds(ks, tk), :], "nt") - delta_wide[h])
                    term = _dot(ds, kv)
                    dq = term if dq is None else dq + term
                    dk = _dot(ds, qh[h], "tn")
                    dk_acc[pl.ds(ks, tk), :] += dk if fold else dk * scale
                    dv_acc[pl.ds(ks, tk), :] += _dot(p, doh[h], "tn")
                    if causal:
                        dc_ref[h, j * nsub + u] -= jnp.sum(ds, axis=0, keepdims=True)
                        dsum = ds if dsum is None else dsum + ds
                dq_acc[h] += dq
                if causal:
                    drow_ref[h] += jnp.sum(dsum, axis=1, keepdims=True)

        def below(j, carry):
            step(j, False)
            return carry

        if causal:
            lax.fori_loop(0, qi, below, 0)
            step(qi, True)
        else:
            step(0, False)
        dq_ref[...] = (_join_heads([dq_acc[h] for h in range(hpb)], d) * scale).astype(dq_ref.dtype)

        @pl.when(qi == pl.num_programs(1) - 1)
        def _():
            dk_ref[...] = dk_acc[...].astype(dk_ref.dtype)
            dv_ref[...] = dv_acc[...].astype(dv_ref.dtype)

    blk = pl.BlockSpec((tq, LANE), lambda g, i: (i, g))
    full = pl.BlockSpec((Tk, LANE), lambda g, i: (0, g))
    col = pl.BlockSpec((hpb, tq, 1), lambda g, i: (g, i, 0))
    crow = pl.BlockSpec((hpb, T // tk, 1, tk), lambda g, i: (g, 0, 0, 0))
    in_specs = [_lane_tile(tq, q[1], False), _lane_tile(Tk, k[1], True), _lane_tile(Tk, v[1], True), blk, blk, col]
    args = [q[0], k[0], v[0], o, do, lse]
    out_specs = [blk, full, full]
    out_shape = [jax.ShapeDtypeStruct((T, n_tiles * LANE), BF16), jax.ShapeDtypeStruct((Tk, n_tiles * LANE), BF16),
                 jax.ShapeDtypeStruct((Tk, n_tiles * LANE), BF16)]
    if causal:
        in_specs += [col, crow]
        args += [c.reshape(H, T, 1), c.reshape(H, T // tk, 1, tk)]
        out_specs += [crow, col]
        out_shape += [jax.ShapeDtypeStruct((H, T // tk, 1, tk), F32), jax.ShapeDtypeStruct((H, T, 1), F32)]
    outs = pl.pallas_call(
        body, name=name, grid=(n_tiles, T // tq),
        in_specs=in_specs, out_specs=out_specs, out_shape=out_shape,
        scratch_shapes=[pltpu.VMEM((Tk, LANE), F32), pltpu.VMEM((Tk, LANE), F32), pltpu.VMEM((hpb, tq, LANE), F32)],
        compiler_params=_cparams(("parallel", "arbitrary")),
    )(*args)
    if causal:
        return outs[0], outs[1], outs[2], outs[3].reshape(H, T), outs[4].reshape(H, T)
    return outs


def _decay_fwd(fl, b):
    H, T = fl.shape
    tk = DECAY_TK

    def body(x_ref, b_ref, c_ref):
        upto = _tri(tk, lambda j, s: j <= s)
        carry = jnp.zeros((H, 1), F32)
        for i in range(T // tk):
            xv = x_ref[:, i * tk:(i + 1) * tk] + b_ref[...]
            lf = jnp.minimum(xv, 0.0) - jnp.log(1.0 + jnp.exp(-jnp.abs(xv)))
            pref = _dot_split(lf, upto, parts=3) + carry
            c_ref[:, i * tk:(i + 1) * tk] = pref
            carry = pref[:, tk - 1:tk]

    vm = pl.BlockSpec(memory_space=pltpu.VMEM)
    return pl.pallas_call(
        body, name="decay_fwd", in_specs=[vm, vm], out_specs=vm,
        out_shape=jax.ShapeDtypeStruct((H, T), F32),
    )(fl, b)


def _decay_bwd(dc_cols, dc_rows, fl, b):
    H, T = fl.shape
    tk = DECAY_TK

    def body(dc_ref, dr_ref, x_ref, b_ref, dx_ref, db_ref):
        from_ = _tri(tk, lambda j, s: j >= s)
        carry = jnp.zeros((H, 1), F32)
        total = jnp.zeros((H, 1), F32)
        for i in reversed(range(T // tk)):
            sl = slice(i * tk, (i + 1) * tk)
            suffix = _dot_split(dc_ref[:, sl] + dr_ref[:, sl], from_, parts=3) + carry
            xv = x_ref[:, sl] + b_ref[...]
            dx = suffix / (1.0 + jnp.exp(xv))
            dx_ref[:, sl] = dx
            total = total + jnp.sum(dx, axis=1, keepdims=True)
            carry = suffix[:, 0:1]
        db_ref[...] = jnp.broadcast_to(total, db_ref.shape)

    vm = pl.BlockSpec(memory_space=pltpu.VMEM)
    dx, db = pl.pallas_call(
        body, name="decay_bwd", in_specs=[vm, vm, vm, vm], out_specs=[vm, vm],
        out_shape=[jax.ShapeDtypeStruct((H, T), F32), jax.ShapeDtypeStruct((H, LANE), F32)],
    )(dc_cols, dc_rows, fl, b)
    return dx, db[:, 0]


def _place():
    x, y, c = lax.axis_index("x"), lax.axis_index("y"), lax.axis_index("c")
    return x, y, c, [(1 - x, y), (x, 1 - y), (1 - x, 1 - y)]


def _all_gather(name, block):
    R, C = block.shape

    def body(x_ref, out_ref, send_sems, recv_sems, local_sem):
        x, y, c, chips = _place()
        me, sibling = (x, y, c), (x, y, 1 - c)

        def rows(px, py, pc):
            return out_ref.at[4 * px + 2 * py + pc]

        def copy(k, blk, to, src=None):
            return pltpu.make_async_remote_copy(
                src_ref=rows(*blk) if src is None else src, dst_ref=rows(*blk),
                send_sem=send_sems.at[k], recv_sem=recv_sems.at[k], device_id=to, device_id_type=MESH)

        mine = pltpu.make_async_copy(x_ref, rows(*me), local_sem)
        mine.start()
        first = [copy(0, me, sibling, src=x_ref)]
        first += [copy(1 + j, me, (*chip, c), src=x_ref) for j, chip in enumerate(chips)]
        for cp in first:
            cp.start()
        passed = [copy(4 + j, (*chip, c), sibling) for j, chip in enumerate(chips)]
        for j, chip in enumerate(chips):
            copy(1 + j, (*chip, c), me).wait_recv()
            passed[j].start()
        copy(0, sibling, me).wait_recv()
        for j, chip in enumerate(chips):
            copy(4 + j, (*chip, 1 - c), me).wait_recv()
        for cp in first + passed:
            cp.wait_send()
        mine.wait()

    return pl.pallas_call(
        body, name=name, in_specs=[ANY], out_specs=ANY,
        out_shape=jax.ShapeDtypeStruct((N_DEV, R, C), block.dtype),
        scratch_shapes=[pltpu.SemaphoreType.DMA((7,)), pltpu.SemaphoreType.DMA((7,)), pltpu.SemaphoreType.DMA(())],
    )(block)


def _swap_with_sibling(name, parts):
    _, R, C = parts.shape

    def body(p_ref, out_ref, send_sems, recv_sems):
        x, y, c, _ = _place()
        copies = [pltpu.make_async_remote_copy(
            src_ref=p_ref.at[2 * q + (1 - c)], dst_ref=out_ref.at[q],
            send_sem=send_sems.at[q], recv_sem=recv_sems.at[q], device_id=(x, y, 1 - c), device_id_type=MESH)
            for q in range(4)]
        for cp in copies:
            cp.start()
        for cp in copies:
            cp.wait_recv()
        for cp in copies:
            cp.wait_send()

    return pl.pallas_call(
        body, name=name, in_specs=[ANY], out_specs=ANY,
        out_shape=jax.ShapeDtypeStruct((4, R, C), parts.dtype),
        scratch_shapes=[pltpu.SemaphoreType.DMA((4,)), pltpu.SemaphoreType.DMA((4,))],
    )(parts)


def _add_own(name, parts, got, tr=512):
    _, R, C = parts.shape
    tr = _tile(R, tr, SUBLANE_BF16)

    def body(c_ref, p_ref, g_ref, o_ref):
        o_ref[...] = (p_ref[...].astype(F32) + g_ref[...].astype(F32)).astype(o_ref.dtype)

    return pl.pallas_call(
        body, name=name,
        grid_spec=pltpu.PrefetchScalarGridSpec(
            num_scalar_prefetch=1, grid=(4, R // tr),
            in_specs=[pl.BlockSpec((1, tr, C), lambda q, i, c: (2 * q + c[0], i, 0)),
                      pl.BlockSpec((1, tr, C), lambda q, i, c: (q, i, 0))],
            out_specs=pl.BlockSpec((1, tr, C), lambda q, i, c: (q, i, 0))),
        out_shape=jax.ShapeDtypeStruct((4, R, C), parts.dtype),
        compiler_params=_cparams(("parallel", "parallel")),
    )(lax.axis_index("c").astype(jnp.int32).reshape(1), parts, got)


def _swap_with_chips(name, parts):
    _, R, C = parts.shape

    def body(p_ref, out_ref, send_sems, recv_sems, local_sem):
        x, y, c, chips = _place()
        my_chip = 2 * x + y
        mine = pltpu.make_async_copy(p_ref.at[my_chip], out_ref.at[my_chip], local_sem)
        mine.start()
        sends = [pltpu.make_async_remote_copy(
            src_ref=p_ref.at[2 * cx + cy], dst_ref=out_ref.at[my_chip],
            send_sem=send_sems.at[j], recv_sem=recv_sems.at[j], device_id=(cx, cy, c), device_id_type=MESH)
            for j, (cx, cy) in enumerate(chips)]
        for cp in sends:
            cp.start()
        for j, (cx, cy) in enumerate(chips):
            pltpu.make_async_remote_copy(
                src_ref=p_ref.at[my_chip], dst_ref=out_ref.at[2 * cx + cy],
                send_sem=send_sems.at[j], recv_sem=recv_sems.at[j], device_id=(cx, cy, c), device_id_type=MESH,
            ).wait_recv()
        for cp in sends:
            cp.wait_send()
        mine.wait()

    return pl.pallas_call(
        body, name=name, in_specs=[ANY], out_specs=ANY,
        out_shape=jax.ShapeDtypeStruct((4, R, C), parts.dtype),
        scratch_shapes=[pltpu.SemaphoreType.DMA((3,)), pltpu.SemaphoreType.DMA((3,)), pltpu.SemaphoreType.DMA(())],
    )(parts)


def _sum_parts(name, parts, tr=512):
    P, R, C = parts.shape
    tr = _tile(R, tr, SUBLANE_BF16)

    def body(p_ref, o_ref):
        total = p_ref[0].astype(F32)
        for p in range(1, P):
            total = total + p_ref[p].astype(F32)
        o_ref[...] = total

    return pl.pallas_call(
        body, name=name, grid=(R // tr,),
        in_specs=[pl.BlockSpec((P, tr, C), lambda i: (0, i, 0))], out_specs=pl.BlockSpec((tr, C), lambda i: (i, 0)),
        out_shape=jax.ShapeDtypeStruct((R, C), F32),
        compiler_params=_cparams(("parallel",)),
    )(parts)


_HBM = pl.BlockSpec(memory_space=pltpu.HBM)
_SEM = pl.BlockSpec(memory_space=pltpu.SEMAPHORE)
_EFFECT = pltpu.SideEffectType.DATAFLOW_SIDE_EFFECTING


def _flipped(x, y, c, k):
    px, py, pc = (1 - x if k & 4 else x), (1 - y if k & 2 else y), (1 - c if k & 1 else c)
    return (px, py, pc), 4 * px + 2 * py + pc


def _exchange_start(name, src, per_peer):
    R, C = src.shape[-2:]

    def body(v_ref, land_ref, send_sem, recv_sem, v_thru, land_thru, token):
        x, y, c = lax.axis_index("x"), lax.axis_index("y"), lax.axis_index("c")
        me = 4 * x + 2 * y + c
        for k in range(1, N_DEV):
            peer, idx = _flipped(x, y, c, k)
            pltpu.make_async_remote_copy(
                src_ref=v_ref.at[idx] if per_peer else v_ref, dst_ref=land_ref.at[me],
                send_sem=send_sem, recv_sem=recv_sem, device_id=peer, device_id_type=MESH).start()
        token[...] = jnp.zeros_like(token)

    return pl.pallas_call(
        body, name=name,
        out_shape=(pltpu.SemaphoreType.DMA(()), pltpu.SemaphoreType.DMA(()), pltpu.HBM(src.shape, src.dtype),
                   pltpu.HBM((N_DEV, R, C), src.dtype), jax.ShapeDtypeStruct((8, LANE), F32)),
        in_specs=(_HBM, _HBM), out_specs=(_SEM, _SEM, _HBM, _HBM, pl.BlockSpec(memory_space=pltpu.VMEM)),
        input_output_aliases={0: 2, 1: 3},
        compiler_params=pltpu.CompilerParams(has_side_effects=_EFFECT),
    )(pltpu.with_memory_space_constraint(src, pltpu.HBM),
      pltpu.with_memory_space_constraint(lax.empty((N_DEV, R, C), src.dtype), pltpu.HBM))


def _exchange_wait(name, started, after):
    send_sem, recv_sem, v_thru, land_thru, _ = started

    def body(v_ref, land_ref, send_sem, recv_sem, after_ref, v_dead, got_ref):
        x, y, c = lax.axis_index("x"), lax.axis_index("y"), lax.axis_index("c")
        seven = land_ref.at[pl.ds(0, N_DEV - 1)]
        drain = pltpu.make_async_remote_copy(
            src_ref=seven, dst_ref=seven, send_sem=send_sem, recv_sem=recv_sem,
            device_id=(x, y, c), device_id_type=MESH)
        drain.wait_send()
        drain.wait_recv()

    return pl.pallas_call(
        body, name=name,
        out_shape=(pltpu.HBM(v_thru.shape, v_thru.dtype), pltpu.HBM(land_thru.shape, land_thru.dtype)),
        in_specs=(_HBM, _HBM, _SEM, _SEM, ANY), out_specs=(_HBM, _HBM), input_output_aliases={0: 0, 1: 1},
        compiler_params=pltpu.CompilerParams(has_side_effects=_EFFECT),
    )(v_thru, land_thru, send_sem, recv_sem, after)[1]


def _my_index():
    return 4 * lax.axis_index("x") + 2 * lax.axis_index("y") + lax.axis_index("c")


def _sum_landed(name, landed, parts, tr=512):
    P, R, C = landed.shape
    tr = _tile(R, tr, SUBLANE_BF16)

    def body(me_ref, l_ref, own_ref, o_ref):
        total = None
        for s in range(P):
            part = jnp.where(me_ref[0] == s, own_ref[0], l_ref[s]).astype(F32)
            total = part if total is None else total + part
        o_ref[...] = total

    return pl.pallas_call(
        body, name=name,
        grid_spec=pltpu.PrefetchScalarGridSpec(
            num_scalar_prefetch=1, grid=(R // tr,),
            in_specs=[pl.BlockSpec((P, tr, C), lambda i, me: (0, i, 0)),
                      pl.BlockSpec((1, tr, C), lambda i, me: (me[0], i, 0))],
            out_specs=pl.BlockSpec((tr, C), lambda i, me: (i, 0))),
        out_shape=jax.ShapeDtypeStruct((R, C), F32),
        compiler_params=_cparams(("parallel",)),
    )(_my_index().astype(jnp.int32).reshape(1), landed, parts)


def _after(params, name, token):
    return {**params, name: params[name] + token[0, 0]}


def _reduce_scatter(tag, parts):
    got = _swap_with_sibling("rs_pair_" + tag, parts)
    pair = _add_own("rs_add_" + tag, parts, got)
    quad = _swap_with_chips("rs_chips_" + tag, pair)
    return _sum_parts("rs_sum_" + tag, quad)


def _adamw(name, g_parts, w, m, v, tr=512):
    P, R, C = g_parts.shape
    tr = _tile(R, tr, 8)

    def body(g_ref, w_ref, m_ref, v_ref, go_ref, d_ref, mo_ref, vo_ref):
        g = g_ref[0]
        for p in range(1, P):
            g = g + g_ref[p]
        mn = ADAM_B1 * m_ref[...] + (1.0 - ADAM_B1) * g
        vn = ADAM_B2 * v_ref[...] + (1.0 - ADAM_B2) * (g * g)
        m_hat = mn / (1.0 - ADAM_B1 ** ADAM_STEP)
        v_hat = vn / (1.0 - ADAM_B2 ** ADAM_STEP)
        go_ref[...] = g
        d_ref[...] = -ADAM_LR * (m_hat / (jnp.sqrt(v_hat) + ADAM_EPS) + ADAM_WD * w_ref[...])
        mo_ref[...] = mn
        vo_ref[...] = vn

    row = pl.BlockSpec((tr, C), lambda i: (i, 0))
    return pl.pallas_call(
        body, name=name, grid=(R // tr,),
        in_specs=[pl.BlockSpec((P, tr, C), lambda i: (0, i, 0)), row, row, row], out_specs=[row] * 4,
        out_shape=[jax.ShapeDtypeStruct((R, C), F32)] * 4,
        compiler_params=_cparams(("parallel",)),
    )(g_parts, w, m, v)


def _pad_rows(t, rows):
    return jnp.pad(t, ((0, rows - t.shape[0]), (0, 0)))


class _Layout:
    def __init__(self, D, ff_shard, in_shard, kv_shard, gate_shard, br_in, br_shard, out_shard):
        self.D = D
        self.in_shard = in_shard
        self.in_pad = -(-in_shard // LANE) * LANE
        self.in_cols = -(-N_DEV * in_shard // IN_TILE) * IN_TILE
        self.br_in, self.br_shard = br_in, br_shard
        br_rows = br_shard * br_in // D
        sizes = [("g1", ff_shard), ("u1", ff_shard), ("d1", ff_shard), ("win", self.in_pad), ("kv", kv_shard),
                 ("gate", gate_shard), ("br", br_rows), ("out", out_shard),
                 ("g2", ff_shard), ("u2", ff_shard), ("d2", ff_shard)]
        self.seg, off = {}, 0
        for key, n in sizes:
            assert n % SUBLANE_BF16 == 0, (key, n)
            self.seg[key] = (off, n)
            off += n
        self.rows = off

    def pack(self, parts):
        return jnp.concatenate([parts[key] for key in self.seg], axis=0)

    def take(self, gathered, key, own=None):
        off, n = self.seg[key]
        seg = gathered[:, off:off + n, :]
        if own is not None:
            seg = lax.dynamic_update_slice(seg, own[0][off:off + n][None], (own[1], 0, 0))
        return seg.reshape(N_DEV * n, self.D)

    def spread(self, full, key):
        _, n = self.seg[key]
        return full.reshape(N_DEV, n, self.D)


def _pack_layer(lay, l, p):
    D = lay.D
    br = jnp.concatenate([p["w_br_sb"][l], p["w_br_fox"][l], p["w_br_mem"][l]], axis=0)
    parts = {
        "g1": p["ffn1_w_gate"][l].T, "u1": p["ffn1_w_up"][l].T, "d1": p["ffn1_w_down"][l],
        "win": _pad_rows(p["w_in"][l].T, lay.in_pad), "kv": p["w_mem_kv"][l], "gate": p["w_gate"][l].T,
        "br": br.T.reshape(-1, D), "out": p["w_out"][l],
        "g2": p["ffn2_w_gate"][l].T, "u2": p["ffn2_w_up"][l].T, "d2": p["ffn2_w_down"][l],
    }
    return lay.pack({k: t.astype(BF16) for k, t in parts.items()})


def _align_win(lay, packed):
    D = lay.D
    real = packed.reshape(N_DEV, lay.in_pad, D)[:, :lay.in_shard].reshape(N_DEV * lay.in_shard, D)
    rows = jnp.concatenate([real[:_QKV_W], real[_QKV_W + N_FOX_HEADS:], real[_QKV_W:_QKV_W + N_FOX_HEADS]], axis=0)
    return _pad_rows(rows, lay.in_cols)


def _unalign_win(lay, aligned):
    D = lay.D
    n_real = N_DEV * lay.in_shard
    mem_w = n_real - _QKV_W - N_FOX_HEADS
    real = jnp.concatenate([aligned[:_QKV_W], aligned[_QKV_W + mem_w:n_real], aligned[_QKV_W:_QKV_W + mem_w]], axis=0)
    real = real.reshape(N_DEV, lay.in_shard, D)
    return jnp.pad(real, ((0, 0), (0, lay.in_pad - lay.in_shard), (0, 0))).reshape(N_DEV * lay.in_pad, D)


def _unpack_layer(lay, gathered, own=None):
    D = lay.D
    w = {k: lay.take(gathered, k, own) for k in ("g1", "u1", "d1", "kv", "out", "g2", "u2", "d2")}
    w["win"] = _align_win(lay, lay.take(gathered, "win", own))
    fl0 = N_DEV * lay.in_shard - N_FOX_HEADS
    w["wfl"] = w["win"][fl0:fl0 + LANE]
    gate = lay.take(gathered, "gate", own)
    w["gate"] = gate
    w["gate3"] = [gate[i * D:(i + 1) * D] for i in range(3)]
    br = lay.take(gathered, "br", own).reshape(N_DEV * lay.br_shard, lay.br_in)
    third = lay.br_in // 3
    w["br3"] = [br[:, i * third:(i + 1) * third] for i in range(3)]
    return w


def _silu_mul(accs, _):
    a, b = accs
    return [a, b, a * jax.nn.sigmoid(a) * b]


def _act_bwd(accs, extras):
    ds, (a, b) = accs[0], extras
    sig = jax.nn.sigmoid(a)
    return [ds * b * (sig * (1.0 + a * (1.0 - sig))), ds * (a * sig)]


def _ffn_fwd(tag, h, pre_g, post_g, wg, wu, wd):
    n = _rms_fwd("ffn_norm_" + tag, h, pre_g, BF16)
    a, b, s = _mm("ffn_up_" + tag, [(n, wg), (n, wu)], "nt", [F32, F32, BF16], _silu_mul, tn=1408)
    f = _mm("ffn_down_" + tag, [(s, wd)], "nn", [F32])
    out = _rms_fwd("ffn_out_" + tag, f, post_g, F32, res=h, scale=0.5)
    return out, (h, n, a, b, s, f)


def _ffn_bwd(tag, dh, saved, pre_g, post_g, wg, wu, wd):
    h, n, a, b, s, f = saved
    df, d_post = _rms_bwd("ffn_dout_" + tag, f, post_g, dh, BF16, scale=0.5)
    da, db = _mm("ffn_dact_" + tag, [(df, wd)], "nt", [BF16, BF16], _act_bwd, [(a, 0), (b, 0)], tn=1408)
    d_wd = _mm("ffn_dwd_" + tag, [(s, df)], "tn", [BF16], tm=256)
    dn = _mm("ffn_dn_" + tag, [(da, wg), (db, wu)], "nn", [F32], _sum_accs)
    d_wg = _mm("ffn_dwg_" + tag, [(da, n)], "tn", [BF16], tm=256)
    d_wu = _mm("ffn_dwu_" + tag, [(db, n)], "tn", [BF16], tm=256)
    dh_in, d_pre = _rms_bwd("ffn_dnorm_" + tag, h, pre_g, dn, F32, res=dh)
    return dh_in, d_pre, d_post, d_wg, d_wu, d_wd


_SB_W = N_SB_HEADS * HEAD_DIM
_FOX_W = N_FOX_HEADS * HEAD_DIM
_QKV_W = 3 * _SB_W + 3 * _FOX_W


def _gate_act(accs, extras):
    return [jax.nn.sigmoid(accs[0] + extras[0])]


def _merge(accs, extras):
    return [extras[0] * accs[0] + extras[1] * accs[1] + extras[2] * accs[2]]


def _merge_bwd(accs, extras):
    dm = accs[0]
    d_branch = [dm * gi for gi in extras]
    d_gate = [dm * bi * gi * (1.0 - gi) for bi, gi in zip(accs[1:], extras)]
    return d_branch + d_gate


def _mix_tiles(lay):
    sb, fx = _SB_W // LANE, _FOX_W // LANE
    mem_w = N_DEV * lay.in_shard - _QKV_W - N_FOX_HEADS
    return (0, sb, 2 * sb, sb), (3 * sb, 3 * sb + fx, 3 * sb + 2 * fx, fx), (_QKV_W // LANE, mem_w // LANE)


def _mix_fwd(lay, h, w, pre_g, post_g, b_forget, b_gate, mem_n):
    D = lay.D
    (sq, sk, sv, sn), (fq, fk, fv, fn), (mq, mn) = _mix_tiles(lay)
    mem_d = mn * LANE // N_MEM_HEADS
    u = _rms_fwd("mix_norm", h, pre_g, BF16)
    proj = _mm("mix_in", [(u, w["win"])], "nt", [BF16], tm=1024, tn=IN_TILE)
    fl = _mm("mix_fl", [(u, w["wfl"])], "nt", [F32])[:, :N_FOX_HEADS].T
    c = _decay_fwd(fl, b_forget.reshape(-1, 1))
    o_sb, rtot = _sb_fwd((proj, sq), (proj, sk), (proj, sv), sn, HEAD_DIM, HEAD_DIM ** -0.5)
    o_fx32, o_fx, lse_fx = _attn_fwd("fox_fwd", (proj, fq), (proj, fk), (proj, fv), fn, HEAD_DIM,
                                     HEAD_DIM ** -0.5, c)
    kvm = _mm("mem_kv", [(mem_n, w["kv"])], "nn", [BF16])
    o_mem32, o_mem, lse_mem = _attn_fwd("mem_fwd", (proj, mq), (kvm, 0), (kvm, mn), mn, mem_d, mem_d ** -0.5)
    gates = _mm("mix_gate", [(u, w["gate"])], "nt", [F32], _gate_act, [(b_gate.reshape(1, -1), 0)], tm=1024)
    flat = [o_sb, o_fx, o_mem]
    merged = _mm("mix_merge", list(zip(flat, w["br3"])), "nt", [BF16], _merge,
                 [(gates, 0), (gates, D), (gates, 2 * D)])
    z = _mm("mix_out", [(merged, w["out"])], "nn", [F32])
    out = _rms_fwd("mix_res", z, post_g, F32, res=h)
    saved = (h, u, proj, fl, c, rtot, o_fx32, lse_fx, kvm, o_mem32, lse_mem, gates, flat, merged, z)
    return out, saved


def _mix_bwd(lay, dh, saved, w, pre_g, post_g, b_forget, mem_n, dmem_n):
    D = lay.D
    (sq, sk, sv, sn), (fq, fk, fv, fn), (mq, mn) = _mix_tiles(lay)
    mem_d = mn * LANE // N_MEM_HEADS
    h, u, proj, fl, c, rtot, o_fx32, lse_fx, kvm, o_mem32, lse_mem, gates, flat, merged, z = saved
    dz, d_post = _rms_bwd("mix_dres", z, post_g, dh, BF16)
    outs = _mm("mix_dmerge", [(dz, w["out"])] + list(zip(flat, w["br3"])), "nt", [BF16] * 6, _merge_bwd,
               [(gates, 0), (gates, D), (gates, 2 * D)], tn=512)
    d_branch, d_gate = outs[:3], outs[3:]
    d_wout = _mm("mix_dwout", [(merged, dz)], "tn", [BF16])
    d_o = [_mm("mix_dbr%d" % i, [(d_branch[i], w["br3"][i])], "nn", [BF16]) for i in range(3)]
    d_wbr = [_mm("mix_dwbr%d" % i, [(d_branch[i], flat[i])], "tn", [BF16]) for i in range(3)]
    d_bgate = jnp.concatenate([_colsum("mix_dbgate%d" % i, d_gate[i]) for i in range(3)])
    d_wgate = [_mm("mix_dwgate%d" % i, [(d_gate[i], u)], "tn", [BF16]) for i in range(3)]

    d_sb = _sb_bwd((proj, sq), (proj, sk), (proj, sv), d_o[0], rtot, sn, HEAD_DIM, HEAD_DIM ** -0.5)
    *d_fx, dc, dc_rows = _attn_bwd("fox_bwd", (proj, fq), (proj, fk), (proj, fv), o_fx32, d_o[1], lse_fx, fn,
                                   HEAD_DIM, HEAD_DIM ** -0.5, c)
    dq_m, dk_m, dv_m = _attn_bwd("mem_bwd", (proj, mq), (kvm, 0), (kvm, mn), o_mem32, d_o[2], lse_mem, mn,
                                 mem_d, mem_d ** -0.5)
    dfl, d_bforget = _decay_bwd(dc, dc_rows, fl, b_forget.reshape(-1, 1))
    pieces = list(d_sb) + list(d_fx) + [dq_m]
    dflp = jnp.pad(dfl.T.astype(BF16), ((0, 0), (0, LANE - dfl.shape[0])))
    offs = [sum(t.shape[1] for t in pieces[:i]) for i in range(len(pieces) + 1)]
    win_rows = [w["win"][offs[i]:offs[i + 1]] for i in range(len(pieces))]
    du = _mm("mix_du", list(zip(d_gate, w["gate3"])) + list(zip(pieces, win_rows)) + [(dflp, w["wfl"])], "nn",
             [F32], _sum_accs, tm=256, tn=512)
    d_rows = [_mm("mix_dwin%d" % i, [(t, u)], "tn", [BF16]) for i, t in enumerate(pieces)]
    d_wfl = _mm("mix_dwfl", [(dflp, u)], "tn", [BF16])
    d_win = _unalign_win(lay, _pad_rows(jnp.concatenate(list(d_rows) + [d_wfl], axis=0), lay.in_cols))
    dh_in, d_pre = _rms_bwd("mix_dnorm", h, pre_g, du, F32, res=dh)

    dkvm = jnp.concatenate([dk_m, dv_m], axis=1)
    d_wkv = _mm("mem_dwkv", [(mem_n, dkvm)], "tn", [BF16])
    dmem_n = _mm("mem_dn", [(dkvm, w["kv"])], "nt", [F32], lambda accs, ex: [accs[0] + ex[0]], [(dmem_n, 0)])
    grads = {"win": d_win, "kv": d_wkv, "gate": jnp.concatenate(d_wgate, axis=0),
             "br": jnp.concatenate(d_wbr, axis=1), "out": d_wout}
    return dh_in, d_pre, d_post, d_bforget, d_bgate, grads, dmem_n


def _layer_fwd(lay, h, w, sp, mem_n):
    h1, s1 = _ffn_fwd("1", h, sp["ffn1_pre_g"], sp["ffn1_post_g"], w["g1"], w["u1"], w["d1"])
    h2, s2 = _mix_fwd(lay, h1, w, sp["mix_pre_g"], sp["mix_post_g"], sp["b_forget"], sp["b_gate"], mem_n)
    h3, s3 = _ffn_fwd("2", h2, sp["ffn2_pre_g"], sp["ffn2_post_g"], w["g2"], w["u2"], w["d2"])
    return h3, (s1, s2, s3)


def _layer_bwd(lay, dh, saved, w, sp, mem_n, dmem_n):
    s1, s2, s3 = saved
    dh, d_pre2, d_post2, d_g2, d_u2, d_d2 = _ffn_bwd("2", dh, s3, sp["ffn2_pre_g"], sp["ffn2_post_g"],
                                                     w["g2"], w["u2"], w["d2"])
    dh, d_mpre, d_mpost, d_bforget, d_bgate, g, dmem_n = _mix_bwd(
        lay, dh, s2, w, sp["mix_pre_g"], sp["mix_post_g"], sp["b_forget"], mem_n, dmem_n)
    dh, d_pre1, d_post1, d_g1, d_u1, d_d1 = _ffn_bwd("1", dh, s1, sp["ffn1_pre_g"], sp["ffn1_post_g"],
                                                     w["g1"], w["u1"], w["d1"])
    g.update({"g1": d_g1, "u1": d_u1, "d1": d_d1, "g2": d_g2, "u2": d_u2, "d2": d_d2})
    g["br"] = g["br"].reshape(N_DEV, lay.br_shard, lay.br_in).reshape(-1, lay.D)
    packed = jnp.concatenate([lay.spread(g[key], key) for key in lay.seg], axis=1)
    small = {"ffn1_pre_g": d_pre1, "ffn1_post_g": d_post1, "mix_pre_g": d_mpre, "mix_post_g": d_mpost,
             "ffn2_pre_g": d_pre2, "ffn2_post_g": d_post2, "b_gate": d_bgate, "b_forget": d_bforget}
    return dh, packed, small, dmem_n


_SHARDED = ["ffn1_w_gate", "ffn1_w_up", "ffn1_w_down", "w_in", "w_mem_kv", "w_gate", "w_br_sb", "w_br_fox",
            "w_br_mem", "w_out", "ffn2_w_gate", "ffn2_w_up", "ffn2_w_down"]
_SMALL_LAYER = ["ffn1_pre_g", "ffn1_post_g", "mix_pre_g", "mix_post_g", "ffn2_pre_g", "ffn2_post_g", "b_gate",
                "b_forget"]
_WEIGHTS = ["ffn1_pre_g", "ffn1_post_g", "ffn1_w_gate", "ffn1_w_up", "ffn1_w_down", "mix_pre_g", "mix_post_g",
            "w_in", "b_forget", "mem_norm_g", "w_mem_kv", "w_gate", "b_gate", "w_br_sb", "w_br_fox", "w_br_mem",
            "w_out", "ffn2_pre_g", "ffn2_post_g", "ffn2_w_gate", "ffn2_w_up", "ffn2_w_down"]


def _pack_small(vals, L, D):
    rows = []
    for l in range(L):
        for name in _SMALL_LAYER:
            t = vals[name][l]
            rows.append(jnp.pad(t, (0, -t.shape[0] % D)).reshape(-1, D))
    rows.append(vals["mem_norm_g"].reshape(1, D))
    packed = jnp.concatenate(rows, axis=0)
    return _pad_rows(packed, -(-packed.shape[0] // 8) * 8)


def _unpack_small(packed, shapes, L, D):
    out = {name: [] for name in _SMALL_LAYER}
    r = 0
    for l in range(L):
        for name in _SMALL_LAYER:
            n = shapes[name][1]
            nr = -(-n // D)
            out[name].append(packed[r:r + nr].reshape(-1)[:n])
            r += nr
    res = {name: jnp.stack(v) for name, v in out.items()}
    res["mem_norm_g"] = packed[r]
    return res


def _unpack_grads(lay, g, l_shapes):
    def seg(key):
        off, n = lay.seg[key]
        return g[off:off + n]
    br = seg("br").reshape(lay.br_shard, lay.br_in).T
    third = lay.br_in // 3
    return {
        "ffn1_w_gate": seg("g1").T, "ffn1_w_up": seg("u1").T, "ffn1_w_down": seg("d1"),
        "w_in": seg("win")[:lay.in_shard].T, "w_mem_kv": seg("kv"), "w_gate": seg("gate").T,
        "w_br_sb": br[:third], "w_br_fox": br[third:2 * third], "w_br_mem": br[2 * third:],
        "w_out": seg("out"), "ffn2_w_gate": seg("g2").T, "ffn2_w_up": seg("u2").T, "ffn2_w_down": seg("d2"),
    }


class _Exchanges:
    def gather(self, name, block):
        return _all_gather(name, block)

    def gather_start(self, block):
        return _exchange_start("ag_start", block, per_peer=False)

    def gather_wait(self, started, after, block):
        return _exchange_wait("ag_wait", started, after), (block, _my_index())

    def scatter(self, parts):
        return _reduce_scatter("w", parts)

    def scatter_start(self, parts):
        return _exchange_start("rs_start", parts, per_peer=True)

    def scatter_wait(self, started, after, parts):
        return _sum_landed("rs_sum8", _exchange_wait("rs_wait", started, after), parts)

    def token(self, started):
        return started[4]

    def loss_sum(self, part):
        return lax.psum(part, ("x", "y", "c"))


def _step(p, m, v, x, mem, tgt, ex):
    L, D = p["ffn1_pre_g"].shape
    lay = _Layout(D, p["ffn1_w_gate"].shape[2], p["w_in"].shape[2], p["w_mem_kv"].shape[1], p["w_gate"].shape[2],
                  3 * p["w_br_sb"].shape[1], p["w_br_sb"].shape[2], p["w_out"].shape[1])
    blocks = [_pack_layer(lay, l, p) for l in range(L)]
    sps = [{name: p[name][l] for name in _SMALL_LAYER} for l in range(L)]

    mem_n = _rms_fwd("mem_norm", mem, p["mem_norm_g"], BF16)
    gathered, own = ex.gather("ag_weights", blocks[0]), None
    h, saved, ws = x, [], []
    for l in range(L):
        if l + 1 < L:
            nxt, gathered = lax.optimization_barrier((blocks[l + 1], gathered))
            started = ex.gather_start(nxt)
            sp = _after(sps[l], "ffn1_pre_g", ex.token(started))
        else:
            sp = sps[l]
        ws.append(_unpack_layer(lay, gathered, own))
        h, s = _layer_fwd(lay, h, ws[l], sp, mem_n)
        saved.append(s)
        if l + 1 < L:
            gathered, own = ex.gather_wait(started, h, blocks[l + 1])
    loss_part, dh = _loss_grad(h, tgt)
    loss = ex.loss_sum(loss_part)

    dmem_n = jnp.zeros(mem.shape, F32)
    big, small = [None] * L, {name: [None] * L for name in _SMALL_LAYER}
    flying, token = {}, None
    for l in reversed(range(L)):
        sp = sps[l] if token is None else _after(sps[l], "ffn2_post_g", token)
        dh, packed, sm, dmem_n = _layer_bwd(lay, dh, saved[l], ws[l], sp, mem_n, dmem_n)
        if l > 0:
            flying[l] = (ex.scatter_start(packed), packed)
            token = ex.token(flying[l][0])
        else:
            big[l] = _unpack_grads(lay, ex.scatter(packed), None)
        for name in _SMALL_LAYER:
            small[name][l] = sm[name]
    for l, (started, packed) in flying.items():
        big[l] = _unpack_grads(lay, ex.scatter_wait(started, dh, packed), None)
    _, d_memg = _rms_bwd("mem_dnorm", mem, p["mem_norm_g"], dmem_n, F32)

    small_g = {name: jnp.stack(vs) for name, vs in small.items()}
    small_g["mem_norm_g"] = d_memg
    small_names = _SMALL_LAYER + ["mem_norm_g"]
    shapes = {name: p[name].shape for name in small_names}
    g_all = ex.gather("ag_small", _pack_small(small_g, L, D))
    packs = [_pack_small({name: t[name] for name in small_names}, L, D) for t in (p, m, v)]
    res = [_unpack_small(t, shapes, L, D) for t in _adamw("adamw_small", g_all, *packs)]

    out = {kind: {} for kind in ("grad", "delta", "new_m", "new_v")}
    for name in small_names:
        for kind, r in zip(("grad", "delta", "new_m", "new_v"), res):
            out[kind][name] = r[name].reshape(p[name].shape)
    for name in _SHARDED:
        g = jnp.stack([big[l][name] for l in range(L)])
        shp = g.shape
        flat = lambda t: t.reshape(-1, shp[-1])
        r = _adamw("adamw_" + name, flat(g)[None], flat(p[name]), flat(m[name]), flat(v[name]))
        for kind, t in zip(("grad", "delta", "new_m", "new_v"), r):
            out[kind][name] = t.reshape(shp)
    return loss, dh, out


def kernel(x, mem, ffn1_pre_g, ffn1_post_g, ffn1_w_gate, ffn1_w_up, ffn1_w_down, mix_pre_g, mix_post_g, w_in, b_forget, mem_norm_g, w_mem_kv, w_gate, b_gate, w_br_sb, w_br_fox, w_br_mem, w_out, ffn2_pre_g, ffn2_post_g, ffn2_w_gate, ffn2_w_up, ffn2_w_down, loss_target, m_ffn1_pre_g, m_ffn1_post_g, m_ffn1_w_gate, m_ffn1_w_up, m_ffn1_w_down, m_mix_pre_g, m_mix_post_g, m_w_in, m_b_forget, m_mem_norm_g, m_w_mem_kv, m_w_gate, m_b_gate, m_w_br_sb, m_w_br_fox, m_w_br_mem, m_w_out, m_ffn2_pre_g, m_ffn2_post_g, m_ffn2_w_gate, m_ffn2_w_up, m_ffn2_w_down, v_ffn1_pre_g, v_ffn1_post_g, v_ffn1_w_gate, v_ffn1_w_up, v_ffn1_w_down, v_mix_pre_g, v_mix_post_g, v_w_in, v_b_forget, v_mem_norm_g, v_w_mem_kv, v_w_gate, v_b_gate, v_w_br_sb, v_w_br_fox, v_w_br_mem, v_w_out, v_ffn2_pre_g, v_ffn2_post_g, v_ffn2_w_gate, v_ffn2_w_up, v_ffn2_w_down):
    p = dict(zip(_WEIGHTS, (ffn1_pre_g, ffn1_post_g, ffn1_w_gate, ffn1_w_up, ffn1_w_down, mix_pre_g, mix_post_g, w_in, b_forget, mem_norm_g, w_mem_kv, w_gate, b_gate, w_br_sb, w_br_fox, w_br_mem, w_out, ffn2_pre_g, ffn2_post_g, ffn2_w_gate, ffn2_w_up, ffn2_w_down)))
    m = dict(zip(_WEIGHTS, (m_ffn1_pre_g, m_ffn1_post_g, m_ffn1_w_gate, m_ffn1_w_up, m_ffn1_w_down, m_mix_pre_g, m_mix_post_g, m_w_in, m_b_forget, m_mem_norm_g, m_w_mem_kv, m_w_gate, m_b_gate, m_w_br_sb, m_w_br_fox, m_w_br_mem, m_w_out, m_ffn2_pre_g, m_ffn2_post_g, m_ffn2_w_gate, m_ffn2_w_up, m_ffn2_w_down)))
    v = dict(zip(_WEIGHTS, (v_ffn1_pre_g, v_ffn1_post_g, v_ffn1_w_gate, v_ffn1_w_up, v_ffn1_w_down, v_mix_pre_g, v_mix_post_g, v_w_in, v_b_forget, v_mem_norm_g, v_w_mem_kv, v_w_gate, v_b_gate, v_w_br_sb, v_w_br_fox, v_w_br_mem, v_w_out, v_ffn2_pre_g, v_ffn2_post_g, v_ffn2_w_gate, v_ffn2_w_up, v_ffn2_w_down)))
    loss, dx, out = _step(p, m, v, x[0], mem[0], loss_target[0], _Exchanges())
    return (loss, dx[None], *[out["grad"][n] for n in _WEIGHTS], *[out["delta"][n] for n in _WEIGHTS],
            *[out["new_m"][n] for n in _WEIGHTS], *[out["new_v"][n] for n in _WEIGHTS])
```

```python
import functools
import math

import jax
import jax.numpy as jnp
from jax import lax
from jax.experimental import pallas as pl
from jax.experimental.pallas import tpu as pltpu

F32 = jnp.float32
BF16 = jnp.bfloat16

LANE = 128
SUBLANE_BF16 = 16
VMEM_LIMIT = 56 * 1024 * 1024
N_DEV = 8
MESH = pl.DeviceIdType.MESH
ANY = pl.BlockSpec(memory_space=pl.ANY)

RMS_EPS = 1e-6
HEAD_DIM = 64
N_SB_HEADS = 8
N_FOX_HEADS = 8
N_MEM_HEADS = 4
NEG = -1e30
ATT_TQ = 512
ATT_TK = 256
DECAY_TK = 128
IN_TILE = 1280

ADAM_LR = 0.001
ADAM_B1 = 0.9
ADAM_B2 = 0.999
ADAM_EPS = 1e-08
ADAM_WD = 0.01
ADAM_STEP = 10


def _tile(n, target, mult=LANE):
    best = None
    for t in range(mult, min(n, target) + 1, mult):
        if n % t == 0:
            best = t
    return best if best is not None else n


def _cparams(sem):
    return pltpu.CompilerParams(dimension_semantics=sem, vmem_limit_bytes=VMEM_LIMIT)


_DIMS = {"nn": (((1,), (0,)), ((), ())), "nt": (((1,), (1,)), ((), ())), "tn": (((0,), (0,)), ((), ()))}


def _dot(a, b, mode="nn"):
    return lax.dot_general(a.astype(BF16), b.astype(BF16), _DIMS[mode], preferred_element_type=F32)


def _mm(name, pairs, mode, out_dtypes, epilogue=None, extras=(), tm=512, tn=1024, out_rows=None):
    a0, b0 = pairs[0]
    M = a0.shape[1] if mode == "tn" else a0.shape[0]
    N = b0.shape[0] if mode == "nt" else b0.shape[1]
    tm = _tile(M, tm)
    tn = _tile(N, tn)
    np_, ne, no = len(pairs), len(extras), len(out_dtypes)

    def body(*refs):
        a_refs, b_refs = refs[:np_], refs[np_:2 * np_]
        e_refs = refs[2 * np_:2 * np_ + ne]
        o_refs = refs[2 * np_ + ne:]
        accs = [_dot(a[...], b[...], mode) for a, b in zip(a_refs, b_refs)]
        outs = epilogue(accs, [e[...] for e in e_refs]) if epilogue is not None else accs
        for o, val in zip(o_refs, outs):
            o[...] = val.astype(o.dtype)

    in_specs = []
    for a, _ in pairs:
        if mode == "tn":
            in_specs.append(pl.BlockSpec((a.shape[0], tm), lambda j, i: (0, i)))
        else:
            in_specs.append(pl.BlockSpec((tm, a.shape[1]), lambda j, i: (i, 0)))
    for _, b in pairs:
        if mode == "nt":
            in_specs.append(pl.BlockSpec((tn, b.shape[1]), lambda j, i: (j, 0)))
        else:
            in_specs.append(pl.BlockSpec((b.shape[0], tn), lambda j, i: (0, j)))
    for e, off in extras:
        if e.shape[0] == 1:
            in_specs.append(pl.BlockSpec((1, tn), functools.partial(lambda j, i, o: (0, j + o), o=off // tn)))
        else:
            in_specs.append(pl.BlockSpec((tm, tn), functools.partial(lambda j, i, o: (i, j + o), o=off // tn)))
    rows = [tm if r is None else r for r in (out_rows or [None] * no)]
    out_specs = [pl.BlockSpec((r, tn), lambda j, i: (i, j)) for r in rows]
    outs = pl.pallas_call(
        body, name=name, grid=(N // tn, M // tm),
        in_specs=in_specs, out_specs=out_specs,
        out_shape=[jax.ShapeDtypeStruct((M // tm * r, N), dt) for r, dt in zip(rows, out_dtypes)],
        compiler_params=_cparams(("parallel", "parallel")),
    )(*[a for a, _ in pairs], *[b for _, b in pairs], *[e for e, _ in extras])
    return outs[0] if no == 1 else outs


def _sum_accs(accs, _):
    total = accs[0]
    for acc in accs[1:]:
        total = total + acc
    return [total]


def _rstd(x):
    return lax.rsqrt(jnp.mean(x * x, axis=-1, keepdims=True) + RMS_EPS)


def _rms_fwd(name, x, g, out_dtype, res=None, scale=1.0, tr=512):
    R, D = x.shape
    tr = _tile(R, tr, 8)
    has_res = res is not None

    def body(*refs):
        x_ref, g_ref = refs[:2]
        o_ref = refs[-1]
        xv = x_ref[...]
        y = (xv * _rstd(xv)) * g_ref[...]
        if has_res:
            y = refs[2][...] + scale * y
        o_ref[...] = y.astype(o_ref.dtype)

    row = pl.BlockSpec((tr, D), lambda i: (i, 0))
    gain = pl.BlockSpec((1, D), lambda i: (0, 0))
    return pl.pallas_call(
        body, name=name, grid=(R // tr,),
        in_specs=[row, gain] + ([row] if has_res else []), out_specs=row,
        out_shape=jax.ShapeDtypeStruct((R, D), out_dtype),
        compiler_params=_cparams(("parallel",)),
    )(x, g.reshape(1, D), *([res] if has_res else []))


def _rms_bwd(name, x, g, dy, out_dtype, scale=1.0, res=None, tr=512):
    R, D = x.shape
    tr = _tile(R, tr, 8)
    has_res = res is not None

    def body(*refs):
        x_ref, g_ref, dy_ref = refs[:3]
        dx_ref, dg_ref = refs[-2:]
        i = pl.program_id(0)
        xv = x_ref[...]
        xhat = xv * _rstd(xv)
        dyv = dy_ref[...].astype(F32) * scale
        gy = dyv * g_ref[...]
        dx = _rstd(xv) * (gy - xhat * jnp.mean(gy * xhat, axis=-1, keepdims=True))
        if has_res:
            dx = refs[3][...] + dx
        dx_ref[...] = dx.astype(dx_ref.dtype)
        part = jnp.sum(dyv * xhat, axis=0, keepdims=True)

        @pl.when(i == 0)
        def _():
            dg_ref[...] = part

        @pl.when(i > 0)
        def _():
            dg_ref[...] += part

    row = pl.BlockSpec((tr, D), lambda i: (i, 0))
    gain = pl.BlockSpec((1, D), lambda i: (0, 0))
    dx, dg = pl.pallas_call(
        body, name=name, grid=(R // tr,),
        in_specs=[row, gain, row] + ([row] if has_res else []), out_specs=[row, gain],
        out_shape=[jax.ShapeDtypeStruct((R, D), out_dtype), jax.ShapeDtypeStruct((1, D), F32)],
        compiler_params=_cparams(("arbitrary",)),
    )(x, g.reshape(1, D), dy, *([res] if has_res else []))
    return dx, dg[0]


def _loss_grad(y, tgt, tr=512):
    R, D = y.shape
    tr = _tile(R, tr, 8)

    def body(y_ref, t_ref, dy_ref, loss_ref):
        i = pl.program_id(0)
        d = y_ref[...] - t_ref[...]
        dy_ref[...] = d / D
        part = 0.5 * jnp.sum(jnp.mean(d * d, axis=-1, keepdims=True), axis=0, keepdims=True)
        tile = jnp.broadcast_to(part, loss_ref.shape)

        @pl.when(i == 0)
        def _():
            loss_ref[...] = tile

        @pl.when(i > 0)
        def _():
            loss_ref[...] += tile

    row = pl.BlockSpec((tr, D), lambda i: (i, 0))
    dy, loss = pl.pallas_call(
        body, name="loss_grad", grid=(R // tr,),
        in_specs=[row, row], out_specs=[row, pl.BlockSpec((8, LANE), lambda i: (0, 0))],
        out_shape=[jax.ShapeDtypeStruct((R, D), F32), jax.ShapeDtypeStruct((8, LANE), F32)],
        compiler_params=_cparams(("arbitrary",)),
    )(y, tgt)
    return loss[0, 0], dy


def _colsum(name, x, tr=512, tn=1024):
    R, N = x.shape
    tr, tn = _tile(R, tr, 8), _tile(N, tn)

    def body(x_ref, o_ref):
        i = pl.program_id(1)
        part = jnp.sum(x_ref[...].astype(F32), axis=0, keepdims=True)

        @pl.when(i == 0)
        def _():
            o_ref[...] = part

        @pl.when(i > 0)
        def _():
            o_ref[...] += part

    out = pl.pallas_call(
        body, name=name, grid=(N // tn, R // tr),
        in_specs=[pl.BlockSpec((tr, tn), lambda j, i: (i, j))], out_specs=pl.BlockSpec((1, tn), lambda j, i: (0, j)),
        out_shape=jax.ShapeDtypeStruct((1, N), F32),
        compiler_params=_cparams(("parallel", "arbitrary")),
    )(x)
    return out[0]


def _tri(tk, rel):
    j = lax.broadcasted_iota(jnp.int32, (tk, tk), 0)
    s = lax.broadcasted_iota(jnp.int32, (tk, tk), 1)
    return rel(j, s).astype(BF16)


def _dot_split(x, m, parts=2):
    total = None
    rem = x
    for _ in range(parts):
        piece = rem.astype(BF16)
        rem = rem - piece.astype(F32)
        term = jnp.dot(piece, m, preferred_element_type=F32)
        total = term if total is None else total + term
    return total


def _log_not_and_beta(z, mask):
    ln = -(jnp.maximum(z, 0.0) + jnp.log(1.0 + jnp.exp(-jnp.abs(z))))
    return (ln if mask is None else jnp.where(mask, ln, 0.0)), ln + z


def _att_tiles(T, Tk, causal):
    tq = min(ATT_TQ, T)
    tk = min(ATT_TK, tq if causal else Tk)
    return tq, tk, (tq if causal else Tk) // tk


def _key_base(j, tq):
    return j * tq if isinstance(j, int) else pl.multiple_of(j * tq, tq)


def _is_pow2(scale):
    return math.log2(scale).is_integer()


def _per_head(x, hpb, d):
    if hpb == 1:
        return [x]
    lane = lax.broadcasted_iota(jnp.int32, x.shape, 1)
    return [jnp.where((lane >= h * d) & (lane < (h + 1) * d), x, jnp.zeros_like(x)) for h in range(hpb)]


def _join_heads(xs, d):
    out = xs[-1]
    if len(xs) > 1:
        lane = lax.broadcasted_iota(jnp.int32, out.shape, 1)
        for h in reversed(range(len(xs) - 1)):
            out = jnp.where(lane < (h + 1) * d, xs[h], out)
    return out


def _lane_tile(rows, off, whole):
    if whole:
        return pl.BlockSpec((rows, LANE), lambda g, i: (0, off + g))
    return pl.BlockSpec((rows, LANE), lambda g, i: (i, off + g))


def _sb_fwd(q, k, v, n_tiles, d, scale):
    T = q[0].shape[0]
    hpb = LANE // d
    tq, tk, nsub = _att_tiles(T, T, True)
    assert _is_pow2(scale)

    def body(q_ref, k_ref, v_ref, ob_ref, rt_ref, acc_ref, r_ref):
        qi = pl.program_id(1)
        qh = _per_head(q_ref[...] * scale, hpb, d)
        acc_ref[...] = jnp.zeros_like(acc_ref)
        r_ref[...] = jnp.zeros_like(r_ref)
        row = lax.broadcasted_iota(jnp.int32, (tq, tk), 0)
        col = lax.broadcasted_iota(jnp.int32, (tq, tk), 1)
        after = _tri(tk, lambda j, s: j > s)

        def step(j, diagonal):
            base = _key_base(j, tq)
            for h in range(hpb):
                parts = []
                for u in reversed(range(nsub)):
                    z = _dot(qh[h], k_ref[pl.ds(base + u * tk, tk), :], "nt")
                    mask = (col + u * tk) < row if diagonal else None
                    ln, lb = _log_not_and_beta(z, mask)
                    between = _dot_split(ln, after, parts=1)
                    first = ln[:, 0:1].astype(BF16).astype(F32)
                    parts.append((u, lb, between, between[:, 0:1] + first, mask))
                r = r_ref[h]
                out = None
                for u, lb, between, total, mask in parts:
                    w = jnp.exp(lb + between + r)
                    if diagonal:
                        w = jnp.where(mask, w, 0.0)
                    term = _dot(w, v_ref[pl.ds(base + u * tk, tk), :])
                    out = term if out is None else out + term
                    r = r + total
                acc_ref[h] += out
                r_ref[h] = r

        def below(i, carry):
            step(qi - 1 - i, False)
            return carry

        step(qi, True)
        lax.fori_loop(0, qi, below, 0)
        ob_ref[...] = _join_heads([acc_ref[h] for h in range(hpb)], d).astype(ob_ref.dtype)
        rt_ref[...] = r_ref[...]

    out = pl.BlockSpec((tq, LANE), lambda g, i: (i, g))
    col = pl.BlockSpec((hpb, tq, 1), lambda g, i: (g, i, 0))
    return pl.pallas_call(
        body, name="sb_fwd", grid=(n_tiles, T // tq),
        in_specs=[_lane_tile(tq, q[1], False), _lane_tile(T, k[1], True), _lane_tile(T, v[1], True)],
        out_specs=[out, col],
        out_shape=[jax.ShapeDtypeStruct((T, n_tiles * LANE), BF16), jax.ShapeDtypeStruct((n_tiles * hpb, T, 1), F32)],
        scratch_shapes=[pltpu.VMEM((hpb, tq, LANE), F32), pltpu.VMEM((hpb, tq, 1), F32)],
        compiler_params=_cparams(("parallel", "arbitrary")),
    )(q[0], k[0], v[0])


def _sb_bwd(q, k, v, do, rtot, n_tiles, d, scale):
    T = q[0].shape[0]
    hpb = LANE // d
    tq, tk, nsub = _att_tiles(T, T, True)
    assert _is_pow2(scale)

    def body(q_ref, k_ref, v_ref, do_ref, rt_ref, dq_ref, dk_ref, dv_ref, dk_acc, dv_acc, dq_acc, p_ref, c_ref):
        qi = pl.program_id(1)

        @pl.when(qi == 0)
        def _():
            dk_acc[...] = jnp.zeros_like(dk_acc)
            dv_acc[...] = jnp.zeros_like(dv_acc)

        qh = _per_head(q_ref[...] * scale, hpb, d)
        doh = _per_head(do_ref[...], hpb, d)
        dq_acc[...] = jnp.zeros_like(dq_acc)
        p_ref[...] = jnp.zeros_like(p_ref)
        c_ref[...] = jnp.zeros_like(c_ref)
        row = lax.broadcasted_iota(jnp.int32, (tq, tk), 0)
        col = lax.broadcasted_iota(jnp.int32, (tq, tk), 1)
        upto = _tri(tk, lambda j, s: j <= s)
        before = _tri(tk, lambda j, s: j < s)
        rt_wide = [jnp.broadcast_to(rt_ref[h], (tq, tk)) for h in range(hpb)]

        def step(j, diagonal):
            base = _key_base(j, tq)
            for h in range(hpb):
                first = []
                for u in range(nsub):
                    ks = base + u * tk
                    kv = k_ref[pl.ds(ks, tk), :]
                    z = _dot(qh[h], kv, "nt")
                    mask = (col + u * tk) < row if diagonal else None
                    ln, lb = _log_not_and_beta(z, mask)
                    dw = _dot(doh[h], v_ref[pl.ds(ks, tk), :], "nt")
                    first.append((ks, kv, mask, lb, jnp.exp(lb), _dot_split(ln, upto, parts=1), dw))
                rt, pre, cpre = rt_wide[h], p_ref[h], c_ref[h]
                dq = None
                for ks, kv, mask, lb, sig, local, dw in first:
                    prefix = local + pre
                    w = jnp.exp(lb + (rt - prefix))
                    if diagonal:
                        w = jnp.where(mask, w, 0.0)
                    g = dw * w
                    c = _dot_split(g, before, parts=1) + cpre
                    dz = g * (1.0 - sig) - c * sig
                    if diagonal:
                        dz = jnp.where(mask, dz, 0.0)
                    term = _dot(dz, kv)
                    dq = term if dq is None else dq + term
                    dk_acc[pl.ds(ks, tk), :] += _dot(dz, qh[h], "tn")
                    dv_acc[pl.ds(ks, tk), :] += _dot(w, doh[h], "tn")
                    pre = prefix[:, tk - 1:tk]
                    cpre = c[:, tk - 1:tk] + g[:, tk - 1:tk]
                dq_acc[h] += dq
                p_ref[h] = pre
                c_ref[h] = cpre

        def below(j, carry):
            step(j, False)
            return carry

        lax.fori_loop(0, qi, below, 0)
        step(qi, True)
        dq_ref[...] = (_join_heads([dq_acc[h] for h in range(hpb)], d) * scale).astype(dq_ref.dtype)

        @pl.when(qi == pl.num_programs(1) - 1)
        def _():
            dk_ref[...] = dk_acc[...].astype(dk_ref.dtype)
            dv_ref[...] = dv_acc[...].astype(dv_ref.dtype)

    blk = pl.BlockSpec((tq, LANE), lambda g, i: (i, g))
    full = pl.BlockSpec((T, LANE), lambda g, i: (0, g))
    col = pl.BlockSpec((hpb, tq, 1), lambda g, i: (g, i, 0))
    wide = jax.ShapeDtypeStruct((T, n_tiles * LANE), BF16)
    return pl.pallas_call(
        body, name="sb_bwd", grid=(n_tiles, T // tq),
        in_specs=[_lane_tile(tq, q[1], False), _lane_tile(T, k[1], True), _lane_tile(T, v[1], True), blk, col],
        out_specs=[blk, full, full], out_shape=[wide, wide, wide],
        scratch_shapes=[pltpu.VMEM((T, LANE), F32), pltpu.VMEM((T, LANE), F32), pltpu.VMEM((hpb, tq, LANE), F32),
                        pltpu.VMEM((hpb, tq, 1), F32), pltpu.VMEM((hpb, tq, 1), F32)],
        compiler_params=_cparams(("parallel", "arbitrary")),
    )(q[0], k[0], v[0], do, rtot)


def _attn_fwd(name, q, k, v, n_tiles, d, scale, c=None):
    T, Tk = q[0].shape[0], k[0].shape[0]
    hpb = LANE // d
    H = n_tiles * hpb
    causal = c is not None
    tq, tk, nsub = _att_tiles(T, Tk, causal)
    fold = _is_pow2(scale)

    def body(*refs):
        q_ref, k_ref, v_ref = refs[:3]
        cc_ref, cr_ref = refs[3:5] if causal else (None, None)
        o_ref, ob_ref, lse_ref, m_ref, l_ref, acc_ref = refs[-6:]
        qi = pl.program_id(1)
        qh = _per_head(q_ref[...] * scale if fold else q_ref[...], hpb, d)
        bias = [jnp.broadcast_to(cc_ref[h], (tq, tk)) for h in range(hpb)] if causal else None
        ones = jnp.ones((tk, LANE), BF16)
        m_ref[...] = jnp.full_like(m_ref, NEG)
        l_ref[...] = jnp.zeros_like(l_ref)
        acc_ref[...] = jnp.zeros_like(acc_ref)
        row = lax.broadcasted_iota(jnp.int32, (tq, tk), 0)
        col = lax.broadcasted_iota(jnp.int32, (tq, tk), 1)

        def step(j, diagonal):
            base = _key_base(j, tq)
            for h in range(hpb):
                zs = []
                for u in range(nsub):
                    z = _dot(qh[h], k_ref[pl.ds(base + u * tk, tk), :], "nt")
                    if not fold:
                        z = z * scale
                    if causal:
                        z = z + bias[h] - cr_ref[h, j * nsub + u]
                    if diagonal:
                        z = jnp.where((col + u * tk) <= row, z, NEG)
                    zs.append(z)
                m_prev = m_ref[h]
                top = zs[0]
                for z in zs[1:]:
                    top = jnp.maximum(top, z)
                m_new = jnp.maximum(m_prev, jnp.max(top, axis=1, keepdims=True))
                alpha = jnp.exp(m_prev - m_new)
                l_new = alpha * l_ref[h]
                out = alpha * acc_ref[h]
                m_wide = jnp.broadcast_to(m_new, (tq, tk))
                for u, z in enumerate(zs):
                    p = jnp.exp(z - m_wide).astype(BF16)
                    l_new = l_new + jnp.dot(p, ones, preferred_element_type=F32)[:, 0:1]
                    out = out + _dot(p, v_ref[pl.ds(base + u * tk, tk), :])
                l_ref[h] = l_new
                acc_ref[h] = out
                m_ref[h] = m_new

        def below(j, carry):
            step(j, False)
            return carry

        if causal:
            lax.fori_loop(0, qi, below, 0)
            step(qi, True)
        else:
            step(0, False)
        o = _join_heads([acc_ref[h] / l_ref[h] for h in range(hpb)], d)
        o_ref[...] = o
        ob_ref[...] = o.astype(ob_ref.dtype)
        lse_ref[...] = m_ref[...] + jnp.log(l_ref[...])

    out = pl.BlockSpec((tq, LANE), lambda g, i: (i, g))
    col = pl.BlockSpec((hpb, tq, 1), lambda g, i: (g, i, 0))
    in_specs = [_lane_tile(tq, q[1], False), _lane_tile(Tk, k[1], True), _lane_tile(Tk, v[1], True)]
    args = [q[0], k[0], v[0]]
    if causal:
        in_specs += [col, pl.BlockSpec((hpb, T // tk, 1, tk), lambda g, i: (g, 0, 0, 0))]
        args += [c.reshape(H, T, 1), c.reshape(H, T // tk, 1, tk)]
    return pl.pallas_call(
        body, name=name, grid=(n_tiles, T // tq),
        in_specs=in_specs, out_specs=[out, out, col],
        out_shape=[jax.ShapeDtypeStruct((T, n_tiles * LANE), F32), jax.ShapeDtypeStruct((T, n_tiles * LANE), BF16),
                   jax.ShapeDtypeStruct((H, T, 1), F32)],
        scratch_shapes=[pltpu.VMEM((hpb, tq, 1), F32), pltpu.VMEM((hpb, tq, 1), F32),
                        pltpu.VMEM((hpb, tq, LANE), F32)],
        compiler_params=_cparams(("parallel", "arbitrary")),
    )(*args)


def _attn_bwd(name, q, k, v, o, do, lse, n_tiles, d, scale, c=None):
    T, Tk = q[0].shape[0], k[0].shape[0]
    hpb = LANE // d
    H = n_tiles * hpb
    causal = c is not None
    tq, tk, nsub = _att_tiles(T, Tk, causal)
    fold = _is_pow2(scale)

    def body(*refs):
        q_ref, k_ref, v_ref, o_ref, do_ref, lse_ref = refs[:6]
        cc_ref, cr_ref = refs[6:8] if causal else (None, None)
        n_out = 5 if causal else 3
        outs = refs[-(n_out + 3):-3]
        dq_ref, dk_ref, dv_ref = outs[:3]
        dc_ref, drow_ref = outs[3:5] if causal else (None, None)
        dk_acc, dv_acc, dq_acc = refs[-3:]
        qi = pl.program_id(1)

        @pl.when(qi == 0)
        def _():
            dk_acc[...] = jnp.zeros_like(dk_acc)
            dv_acc[...] = jnp.zeros_like(dv_acc)
            if causal:
                dc_ref[...] = jnp.zeros_like(dc_ref)

        qh = _per_head(q_ref[...] * scale if fold else q_ref[...], hpb, d)
        doh = _per_head(do_ref[...], hpb, d)
        delta_wide = [jnp.broadcast_to(jnp.sum(t.astype(F32) * o_ref[...], axis=1, keepdims=True), (tq, tk))
                      for t in doh]
        shift = [jnp.broadcast_to((cc_ref[h] - lse_ref[h]) if causal else -lse_ref[h], (tq, tk)) for h in range(hpb)]
        dq_acc[...] = jnp.zeros_like(dq_acc)
        if causal:
            drow_ref[...] = jnp.zeros_like(drow_ref)
        row = lax.broadcasted_iota(jnp.int32, (tq, tk), 0)
        col = lax.broadcasted_iota(jnp.int32, (tq, tk), 1)

        def step(j, diagonal):
            base = _key_base(j, tq)
            for h in range(hpb):
                dq, dsum = None, None
                for u in range(nsub):
                    ks = base + u * tk
                    kv = k_ref[pl.ds(ks, tk), :]
                    z = _dot(qh[h], kv, "nt")
                    if not fold:
                        z = z * scale
                    z = z + shift[h]
                    if causal:
                        z = z - cr_ref[h, j * nsub + u]
                    if diagonal:
                        z = jnp.where((col + u * tk) <= row, z, NEG)
                    p = jnp.exp(z)
                    ds = p * (_dot(doh[h], v_ref[pl.ds(ks, tk), :], "nt") - delta_wide[h])
                    term = _dot(ds, kv)
                    dq = term if dq is None else dq + term
                    dk = _dot(ds, qh[h], "tn")
                    dk_acc[pl.ds(ks, tk), :] += dk if fold else dk * scale
                    dv_acc[pl.ds(ks, tk), :] += _dot(p, doh[h], "tn")
                    if causal:
                        dc_ref[h, j * nsub + u] -= jnp.sum(ds, axis=0, keepdims=True)
                        dsum = ds if dsum is None else dsum + ds
                dq_acc[h] += dq
                if causal:
                    drow_ref[h] += jnp.sum(dsum, axis=1, keepdims=True)

        def below(j, carry):
            step(j, False)
            return carry

        if causal:
            lax.fori_loop(0, qi, below, 0)
            step(qi, True)
        else:
            step(0, False)
        dq_ref[...] = (_join_heads([dq_acc[h] for h in range(hpb)], d) * scale).astype(dq_ref.dtype)

        @pl.when(qi == pl.num_programs(1) - 1)
        def _():
            dk_ref[...] = dk_acc[...].astype(dk_ref.dtype)
            dv_ref[...] = dv_acc[...].astype(dv_ref.dtype)

    blk = pl.BlockSpec((tq, LANE), lambda g, i: (i, g))
    full = pl.BlockSpec((Tk, LANE), lambda g, i: (0, g))
    col = pl.BlockSpec((hpb, tq, 1), lambda g, i: (g, i, 0))
    crow = pl.BlockSpec((hpb, T // tk, 1, tk), lambda g, i: (g, 0, 0, 0))
    in_specs = [_lane_tile(tq, q[1], False), _lane_tile(Tk, k[1], True), _lane_tile(Tk, v[1], True), blk, blk, col]
    args = [q[0], k[0], v[0], o, do, lse]
    out_specs = [blk, full, full]
    out_shape = [jax.ShapeDtypeStruct((T, n_tiles * LANE), BF16), jax.ShapeDtypeStruct((Tk, n_tiles * LANE), BF16),
                 jax.ShapeDtypeStruct((Tk, n_tiles * LANE), BF16)]
    if causal:
        in_specs += [col, crow]
        args += [c.reshape(H, T, 1), c.reshape(H, T // tk, 1, tk)]
        out_specs += [crow, col]
        out_shape += [jax.ShapeDtypeStruct((H, T // tk, 1, tk), F32), jax.ShapeDtypeStruct((H, T, 1), F32)]
    outs = pl.pallas_call(
        body, name=name, grid=(n_tiles, T // tq),
        in_specs=in_specs, out_specs=out_specs, out_shape=out_shape,
        scratch_shapes=[pltpu.VMEM((Tk, LANE), F32), pltpu.VMEM((Tk, LANE), F32), pltpu.VMEM((hpb, tq, LANE), F32)],
        compiler_params=_cparams(("parallel", "arbitrary")),
    )(*args)
    if causal:
        return outs[0], outs[1], outs[2], outs[3].reshape(H, T), outs[4].reshape(H, T)
    return outs


def _decay_fwd(fl, b):
    H, T = fl.shape
    tk = DECAY_TK

    def body(x_ref, b_ref, c_ref):
        upto = _tri(tk, lambda j, s: j <= s)
        carry = jnp.zeros((H, 1), F32)
        for i in range(T // tk):
            xv = x_ref[:, i * tk:(i + 1) * tk] + b_ref[...]
            lf = jnp.minimum(xv, 0.0) - jnp.log(1.0 + jnp.exp(-jnp.abs(xv)))
            pref = _dot_split(lf, upto, parts=3) + carry
            c_ref[:, i * tk:(i + 1) * tk] = pref
            carry = pref[:, tk - 1:tk]

    vm = pl.BlockSpec(memory_space=pltpu.VMEM)
    return pl.pallas_call(
        body, name="decay_fwd", in_specs=[vm, vm], out_specs=vm,
        out_shape=jax.ShapeDtypeStruct((H, T), F32),
    )(fl, b)


def _decay_bwd(dc_cols, dc_rows, fl, b):
    H, T = fl.shape
    tk = DECAY_TK

    def body(dc_ref, dr_ref, x_ref, b_ref, dx_ref, db_ref):
        from_ = _tri(tk, lambda j, s: j >= s)
        carry = jnp.zeros((H, 1), F32)
        total = jnp.zeros((H, 1), F32)
        for i in reversed(range(T // tk)):
            sl = slice(i * tk, (i + 1) * tk)
            suffix = _dot_split(dc_ref[:, sl] + dr_ref[:, sl], from_, parts=3) + carry
            xv = x_ref[:, sl] + b_ref[...]
            dx = suffix / (1.0 + jnp.exp(xv))
            dx_ref[:, sl] = dx
            total = total + jnp.sum(dx, axis=1, keepdims=True)
            carry = suffix[:, 0:1]
        db_ref[...] = jnp.broadcast_to(total, db_ref.shape)

    vm = pl.BlockSpec(memory_space=pltpu.VMEM)
    dx, db = pl.pallas_call(
        body, name="decay_bwd", in_specs=[vm, vm, vm, vm], out_specs=[vm, vm],
        out_shape=[jax.ShapeDtypeStruct((H, T), F32), jax.ShapeDtypeStruct((H, LANE), F32)],
    )(dc_cols, dc_rows, fl, b)
    return dx, db[:, 0]


def _place():
    x, y, c = lax.axis_index("x"), lax.axis_index("y"), lax.axis_index("c")
    return x, y, c, [(1 - x, y), (x, 1 - y), (1 - x, 1 - y)]


def _all_gather(name, block):
    R, C = block.shape

    def body(x_ref, out_ref, send_sems, recv_sems, local_sem):
        x, y, c, chips = _place()
        me, sibling = (x, y, c), (x, y, 1 - c)

        def rows(px, py, pc):
            return out_ref.at[4 * px + 2 * py + pc]

        def copy(k, blk, to, src=None):
            return pltpu.make_async_remote_copy(
                src_ref=rows(*blk) if src is None else src, dst_ref=rows(*blk),
                send_sem=send_sems.at[k], recv_sem=recv_sems.at[k], device_id=to, device_id_type=MESH)

        mine = pltpu.make_async_copy(x_ref, rows(*me), local_sem)
        mine.start()
        first = [copy(0, me, sibling, src=x_ref)]
        first += [copy(1 + j, me, (*chip, c), src=x_ref) for j, chip in enumerate(chips)]
        for cp in first:
            cp.start()
        passed = [copy(4 + j, (*chip, c), sibling) for j, chip in enumerate(chips)]
        for j, chip in enumerate(chips):
            copy(1 + j, (*chip, c), me).wait_recv()
            passed[j].start()
        copy(0, sibling, me).wait_recv()
        for j, chip in enumerate(chips):
            copy(4 + j, (*chip, 1 - c), me).wait_recv()
        for cp in first + passed:
            cp.wait_send()
        mine.wait()

    return pl.pallas_call(
        body, name=name, in_specs=[ANY], out_specs=ANY,
        out_shape=jax.ShapeDtypeStruct((N_DEV, R, C), block.dtype),
        scratch_shapes=[pltpu.SemaphoreType.DMA((7,)), pltpu.SemaphoreType.DMA((7,)), pltpu.SemaphoreType.DMA(())],
    )(block)


def _swap_with_sibling(name, parts):
    _, R, C = parts.shape

    def body(p_ref, out_ref, send_sems, recv_sems):
        x, y, c, _ = _place()
        copies = [pltpu.make_async_remote_copy(
            src_ref=p_ref.at[2 * q + (1 - c)], dst_ref=out_ref.at[q],
            send_sem=send_sems.at[q], recv_sem=recv_sems.at[q], device_id=(x, y, 1 - c), device_id_type=MESH)
            for q in range(4)]
        for cp in copies:
            cp.start()
        for cp in copies:
            cp.wait_recv()
        for cp in copies:
            cp.wait_send()

    return pl.pallas_call(
        body, name=name, in_specs=[ANY], out_specs=ANY,
        out_shape=jax.ShapeDtypeStruct((4, R, C), parts.dtype),
        scratch_shapes=[pltpu.SemaphoreType.DMA((4,)), pltpu.SemaphoreType.DMA((4,))],
    )(parts)


def _add_own(name, parts, got, tr=512):
    _, R, C = parts.shape
    tr = _tile(R, tr, SUBLANE_BF16)

    def body(c_ref, p_ref, g_ref, o_ref):
        o_ref[...] = (p_ref[...].astype(F32) + g_ref[...].astype(F32)).astype(o_ref.dtype)

    return pl.pallas_call(
        body, name=name,
        grid_spec=pltpu.PrefetchScalarGridSpec(
            num_scalar_prefetch=1, grid=(4, R // tr),
            in_specs=[pl.BlockSpec((1, tr, C), lambda q, i, c: (2 * q + c[0], i, 0)),
                      pl.BlockSpec((1, tr, C), lambda q, i, c: (q, i, 0))],
            out_specs=pl.BlockSpec((1, tr, C), lambda q, i, c: (q, i, 0))),
        out_shape=jax.ShapeDtypeStruct((4, R, C), parts.dtype),
        compiler_params=_cparams(("parallel", "parallel")),
    )(lax.axis_index("c").astype(jnp.int32).reshape(1), parts, got)


def _swap_with_chips(name, parts):
    _, R, C = parts.shape

    def body(p_ref, out_ref, send_sems, recv_sems, local_sem):
        x, y, c, chips = _place()
        my_chip = 2 * x + y
        mine = pltpu.make_async_copy(p_ref.at[my_chip], out_ref.at[my_chip], local_sem)
        mine.start()
        sends = [pltpu.make_async_remote_copy(
            src_ref=p_ref.at[2 * cx + cy], dst_ref=out_ref.at[my_chip],
            send_sem=send_sems.at[j], recv_sem=recv_sems.at[j], device_id=(cx, cy, c), device_id_type=MESH)
            for j, (cx, cy) in enumerate(chips)]
        for cp in sends:
            cp.start()
        for j, (cx, cy) in enumerate(chips):
            pltpu.make_async_remote_copy(
                src_ref=p_ref.at[my_chip], dst_ref=out_ref.at[2 * cx + cy],
                send_sem=send_sems.at[j], recv_sem=recv_sems.at[j], device_id=(cx, cy, c), device_id_type=MESH,
            ).wait_recv()
        for cp in sends:
            cp.wait_send()
        mine.wait()

    return pl.pallas_call(
        body, name=name, in_specs=[ANY], out_specs=ANY,
        out_shape=jax.ShapeDtypeStruct((4, R, C), parts.dtype),
        scratch_shapes=[pltpu.SemaphoreType.DMA((3,)), pltpu.SemaphoreType.DMA((3,)), pltpu.SemaphoreType.DMA(())],
    )(parts)


def _sum_parts(name, parts, tr=512):
    P, R, C = parts.shape
    tr = _tile(R, tr, SUBLANE_BF16)

    def body(p_ref, o_ref):
        total = p_ref[0].astype(F32)
        for p in range(1, P):
            total = total + p_ref[p].astype(F32)
        o_ref[...] = total

    return pl.pallas_call(
        body, name=name, grid=(R // tr,),
        in_specs=[pl.BlockSpec((P, tr, C), lambda i: (0, i, 0))], out_specs=pl.BlockSpec((tr, C), lambda i: (i, 0)),
        out_shape=jax.ShapeDtypeStruct((R, C), F32),
        compiler_params=_cparams(("parallel",)),
    )(parts)


_HBM = pl.BlockSpec(memory_space=pltpu.HBM)
_SEM = pl.BlockSpec(memory_space=pltpu.SEMAPHORE)
_EFFECT = pltpu.SideEffectType.DATAFLOW_SIDE_EFFECTING


def _flipped(x, y, c, k):
    px, py, pc = (1 - x if k & 4 else x), (1 - y if k & 2 else y), (1 - c if k & 1 else c)
    return (px, py, pc), 4 * px + 2 * py + pc


def _exchange_start(name, src, per_peer):
    R, C = src.shape[-2:]

    def body(v_ref, land_ref, send_sem, recv_sem, v_thru, land_thru, token):
        x, y, c = lax.axis_index("x"), lax.axis_index("y"), lax.axis_index("c")
        me = 4 * x + 2 * y + c
        for k in range(1, N_DEV):
            peer, idx = _flipped(x, y, c, k)
            pltpu.make_async_remote_copy(
                src_ref=v_ref.at[idx] if per_peer else v_ref, dst_ref=land_ref.at[me],
                send_sem=send_sem, recv_sem=recv_sem, device_id=peer, device_id_type=MESH).start()
        token[...] = jnp.zeros_like(token)

    return pl.pallas_call(
        body, name=name,
        out_shape=(pltpu.SemaphoreType.DMA(()), pltpu.SemaphoreType.DMA(()), pltpu.HBM(src.shape, src.dtype),
                   pltpu.HBM((N_DEV, R, C), src.dtype), jax.ShapeDtypeStruct((8, LANE), F32)),
        in_specs=(_HBM, _HBM), out_specs=(_SEM, _SEM, _HBM, _HBM, pl.BlockSpec(memory_space=pltpu.VMEM)),
        input_output_aliases={0: 2, 1: 3},
        compiler_params=pltpu.CompilerParams(has_side_effects=_EFFECT),
    )(pltpu.with_memory_space_constraint(src, pltpu.HBM),
      pltpu.with_memory_space_constraint(lax.empty((N_DEV, R, C), src.dtype), pltpu.HBM))


def _exchange_wait(name, started, after):
    send_sem, recv_sem, v_thru, land_thru, _ = started

    def body(v_ref, land_ref, send_sem, recv_sem, after_ref, v_dead, got_ref):
        x, y, c = lax.axis_index("x"), lax.axis_index("y"), lax.axis_index("c")
        seven = land_ref.at[pl.ds(0, N_DEV - 1)]
        drain = pltpu.make_async_remote_copy(
            src_ref=seven, dst_ref=seven, send_sem=send_sem, recv_sem=recv_sem,
            device_id=(x, y, c), device_id_type=MESH)
        drain.wait_send()
        drain.wait_recv()

    return pl.pallas_call(
        body, name=name,
        out_shape=(pltpu.HBM(v_thru.shape, v_thru.dtype), pltpu.HBM(land_thru.shape, land_thru.dtype)),
        in_specs=(_HBM, _HBM, _SEM, _SEM, ANY), out_specs=(_HBM, _HBM), input_output_aliases={0: 0, 1: 1},
        compiler_params=pltpu.CompilerParams(has_side_effects=_EFFECT),
    )(v_thru, land_thru, send_sem, recv_sem, after)[1]


def _my_index():
    return 4 * lax.axis_index("x") + 2 * lax.axis_index("y") + lax.axis_index("c")


def _sum_landed(name, landed, parts, tr=512):
    P, R, C = landed.shape
    tr = _tile(R, tr, SUBLANE_BF16)

    def body(me_ref, l_ref, own_ref, o_ref):
        total = None
        for s in range(P):
            part = jnp.where(me_ref[0] == s, own_ref[0], l_ref[s]).astype(F32)
            total = part if total is None else total + part
        o_ref[...] = total

    return pl.pallas_call(
        body, name=name,
        grid_spec=pltpu.PrefetchScalarGridSpec(
            num_scalar_prefetch=1, grid=(R // tr,),
            in_specs=[pl.BlockSpec((P, tr, C), lambda i, me: (0, i, 0)),
                      pl.BlockSpec((1, tr, C), lambda i, me: (me[0], i, 0))],
            out_specs=pl.BlockSpec((tr, C), lambda i, me: (i, 0))),
        out_shape=jax.ShapeDtypeStruct((R, C), F32),
        compiler_params=_cparams(("parallel",)),
    )(_my_index().astype(jnp.int32).reshape(1), landed, parts)


def _after(params, name, token):
    return {**params, name: params[name] + token[0, 0]}


def _reduce_scatter(tag, parts):
    got = _swap_with_sibling("rs_pair_" + tag, parts)
    pair = _add_own("rs_add_" + tag, parts, got)
    quad = _swap_with_chips("rs_chips_" + tag, pair)
    return _sum_parts("rs_sum_" + tag, quad)


def _adamw(name, g_parts, w, m, v, tr=512):
    P, R, C = g_parts.shape
    tr = _tile(R, tr, 8)

    def body(g_ref, w_ref, m_ref, v_ref, go_ref, d_ref, mo_ref, vo_ref):
        g = g_ref[0]
        for p in range(1, P):
            g = g + g_ref[p]
        mn = ADAM_B1 * m_ref[...] + (1.0 - ADAM_B1) * g
        vn = ADAM_B2 * v_ref[...] + (1.0 - ADAM_B2) * (g * g)
        m_hat = mn / (1.0 - ADAM_B1 ** ADAM_STEP)
        v_hat = vn / (1.0 - ADAM_B2 ** ADAM_STEP)
        go_ref[...] = g
        d_ref[...] = -ADAM_LR * (m_hat / (jnp.sqrt(v_hat) + ADAM_EPS) + ADAM_WD * w_ref[...])
        mo_ref[...] = mn
        vo_ref[...] = vn

    row = pl.BlockSpec((tr, C), lambda i: (i, 0))
    return pl.pallas_call(
        body, name=name, grid=(R // tr,),
        in_specs=[pl.BlockSpec((P, tr, C), lambda i: (0, i, 0)), row, row, row], out_specs=[row] * 4,
        out_shape=[jax.ShapeDtypeStruct((R, C), F32)] * 4,
        compiler_params=_cparams(("parallel",)),
    )(g_parts, w, m, v)


def _pad_rows(t, rows):
    return jnp.pad(t, ((0, rows - t.shape[0]), (0, 0)))


class _Layout:
    def __init__(self, D, ff_shard, in_shard, kv_shard, gate_shard, br_in, br_shard, out_shard):
        self.D = D
        self.in_shard = in_shard
        self.in_pad = -(-in_shard // LANE) * LANE
        self.in_cols = -(-N_DEV * in_shard // IN_TILE) * IN_TILE
        self.br_in, self.br_shard = br_in, br_shard
        br_rows = br_shard * br_in // D
        sizes = [("g1", ff_shard), ("u1", ff_shard), ("d1", ff_shard), ("win", self.in_pad), ("kv", kv_shard),
                 ("gate", gate_shard), ("br", br_rows), ("out", out_shard),
                 ("g2", ff_shard), ("u2", ff_shard), ("d2", ff_shard)]
        self.seg, off = {}, 0
        for key, n in sizes:
            assert n % SUBLANE_BF16 == 0, (key, n)
            self.seg[key] = (off, n)
            off += n
        self.rows = off

    def pack(self, parts):
        return jnp.concatenate([parts[key] for key in self.seg], axis=0)

    def take(self, gathered, key, own=None):
        off, n = self.seg[key]
        seg = gathered[:, off:off + n, :]
        if own is not None:
            seg = lax.dynamic_update_slice(seg, own[0][off:off + n][None], (own[1], 0, 0))
        return seg.reshape(N_DEV * n, self.D)

    def spread(self, full, key):
        _, n = self.seg[key]
        return full.reshape(N_DEV, n, self.D)


def _pack_layer(lay, l, p):
    D = lay.D
    br = jnp.concatenate([p["w_br_sb"][l], p["w_br_fox"][l], p["w_br_mem"][l]], axis=0)
    parts = {
        "g1": p["ffn1_w_gate"][l].T, "u1": p["ffn1_w_up"][l].T, "d1": p["ffn1_w_down"][l],
        "win": _pad_rows(p["w_in"][l].T, lay.in_pad), "kv": p["w_mem_kv"][l], "gate": p["w_gate"][l].T,
        "br": br.T.reshape(-1, D), "out": p["w_out"][l],
        "g2": p["ffn2_w_gate"][l].T, "u2": p["ffn2_w_up"][l].T, "d2": p["ffn2_w_down"][l],
    }
    return lay.pack({k: t.astype(BF16) for k, t in parts.items()})


def _align_win(lay, packed):
    D = lay.D
    real = packed.reshape(N_DEV, lay.in_pad, D)[:, :lay.in_shard].reshape(N_DEV * lay.in_shard, D)
    rows = jnp.concatenate([real[:_QKV_W], real[_QKV_W + N_FOX_HEADS:], real[_QKV_W:_QKV_W + N_FOX_HEADS]], axis=0)
    return _pad_rows(rows, lay.in_cols)


def _unalign_win(lay, aligned):
    D = lay.D
    n_real = N_DEV * lay.in_shard
    mem_w = n_real - _QKV_W - N_FOX_HEADS
    real = jnp.concatenate([aligned[:_QKV_W], aligned[_QKV_W + mem_w:n_real], aligned[_QKV_W:_QKV_W + mem_w]], axis=0)
    real = real.reshape(N_DEV, lay.in_shard, D)
    return jnp.pad(real, ((0, 0), (0, lay.in_pad - lay.in_shard), (0, 0))).reshape(N_DEV * lay.in_pad, D)


def _unpack_layer(lay, gathered, own=None):
    D = lay.D
    w = {k: lay.take(gathered, k, own) for k in ("g1", "u1", "d1", "kv", "out", "g2", "u2", "d2")}
    w["win"] = _align_win(lay, lay.take(gathered, "win", own))
    fl0 = N_DEV * lay.in_shard - N_FOX_HEADS
    w["wfl"] = w["win"][fl0:fl0 + LANE]
    gate = lay.take(gathered, "gate", own)
    w["gate"] = gate
    w["gate3"] = [gate[i * D:(i + 1) * D] for i in range(3)]
    br = lay.take(gathered, "br", own).reshape(N_DEV * lay.br_shard, lay.br_in)
    third = lay.br_in // 3
    w["br3"] = [br[:, i * third:(i + 1) * third] for i in range(3)]
    return w


def _silu_mul(accs, _):
    a, b = accs
    return [a, b, a * jax.nn.sigmoid(a) * b]


def _act_bwd(accs, extras):
    ds, (a, b) = accs[0], extras
    sig = jax.nn.sigmoid(a)
    return [ds * b * (sig * (1.0 + a * (1.0 - sig))), ds * (a * sig)]


def _res_norm(scale):
    def epilogue(accs, extras):
        f, (res, g) = accs[0], extras
        return [f, res + scale * ((f * _rstd(f)) * g)]
    return epilogue


def _norm_bwd(accs, extras):
    dy, (x, res, g) = _sum_accs(accs, None)[0], extras
    r = _rstd(x)
    xhat = x * r
    gy = dy * g
    dx = res + r * (gy - xhat * jnp.mean(gy * xhat, axis=-1, keepdims=True))
    part = jnp.sum(dy * xhat, axis=0, keepdims=True)
    first = lax.broadcasted_iota(jnp.int32, (8, part.shape[1]), 0) == 0
    return [dx, jnp.where(first, part, 0.0)]


def _ffn_fwd(tag, h, pre_g, post_g, wg, wu, wd):
    D = h.shape[1]
    n = _rms_fwd("ffn_norm_" + tag, h, pre_g, BF16)
    a, b, s = _mm("ffn_up_" + tag, [(n, wg), (n, wu)], "nt", [F32, F32, BF16], _silu_mul, tn=1408)
    f, out = _mm("ffn_down_" + tag, [(s, wd)], "nn", [F32, F32], _res_norm(0.5),
                 [(h, 0), (post_g.reshape(1, D), 0)], tn=D)
    return out, (h, n, a, b, s, f)


def _ffn_bwd(tag, dh, saved, pre_g, post_g, wg, wu, wd):
    h, n, a, b, s, f = saved
    D = h.shape[1]
    df, d_post = _rms_bwd("ffn_dout_" + tag, f, post_g, dh, BF16, scale=0.5)
    da, db = _mm("ffn_dact_" + tag, [(df, wd)], "nt", [BF16, BF16], _act_bwd, [(a, 0), (b, 0)], tn=1408)
    d_wd = _mm("ffn_dwd_" + tag, [(s, df)], "tn", [BF16], tm=256)
    dh_in, d_pre_rows = _mm("ffn_dn_" + tag, [(da, wg), (db, wu)], "nn", [F32, F32], _norm_bwd,
                            [(h, 0), (dh, 0), (pre_g.reshape(1, D), 0)], tm=256, tn=D, out_rows=[None, 8])
    d_pre = _colsum("ffn_dpre_" + tag, d_pre_rows)
    d_wg = _mm("ffn_dwg_" + tag, [(da, n)], "tn", [BF16], tm=256)
    d_wu = _mm("ffn_dwu_" + tag, [(db, n)], "tn", [BF16], tm=256)
    return dh_in, d_pre, d_post, d_wg, d_wu, d_wd


_SB_W = N_SB_HEADS * HEAD_DIM
_FOX_W = N_FOX_HEADS * HEAD_DIM
_QKV_W = 3 * _SB_W + 3 * _FOX_W


def _gate_act(accs, extras):
    return [jax.nn.sigmoid(accs[0] + extras[0])]


def _merge(accs, extras):
    return [extras[0] * accs[0] + extras[1] * accs[1] + extras[2] * accs[2]]


def _merge_bwd(accs, extras):
    dm = accs[0]
    d_branch = [dm * gi for gi in extras]
    d_gate = [dm * bi * gi * (1.0 - gi) for bi, gi in zip(accs[1:], extras)]
    return d_branch + d_gate


def _mix_tiles(lay):
    sb, fx = _SB_W // LANE, _FOX_W // LANE
    mem_w = N_DEV * lay.in_shard - _QKV_W - N_FOX_HEADS
    return (0, sb, 2 * sb, sb), (3 * sb, 3 * sb + fx, 3 * sb + 2 * fx, fx), (_QKV_W // LANE, mem_w // LANE)


def _mix_fwd(lay, h, w, pre_g, post_g, b_forget, b_gate, mem_n):
    D = lay.D
    (sq, sk, sv, sn), (fq, fk, fv, fn), (mq, mn) = _mix_tiles(lay)
    mem_d = mn * LANE // N_MEM_HEADS
    u = _rms_fwd("mix_norm", h, pre_g, BF16)
    proj = _mm("mix_in", [(u, w["win"])], "nt", [BF16], tm=1024, tn=IN_TILE)
    fl = _mm("mix_fl", [(u, w["wfl"])], "nt", [F32])[:, :N_FOX_HEADS].T
    c = _decay_fwd(fl, b_forget.reshape(-1, 1))
    o_sb, rtot = _sb_fwd((proj, sq), (proj, sk), (proj, sv), sn, HEAD_DIM, HEAD_DIM ** -0.5)
    o_fx32, o_fx, lse_fx = _attn_fwd("fox_fwd", (proj, fq), (proj, fk), (proj, fv), fn, HEAD_DIM,
                                     HEAD_DIM ** -0.5, c)
    kvm = _mm("mem_kv", [(mem_n, w["kv"])], "nn", [BF16])
    o_mem32, o_mem, lse_mem = _attn_fwd("mem_fwd", (proj, mq), (kvm, 0), (kvm, mn), mn, mem_d, mem_d ** -0.5)
    gates = _mm("mix_gate", [(u, w["gate"])], "nt", [F32], _gate_act, [(b_gate.reshape(1, -1), 0)], tm=1024)
    flat = [o_sb, o_fx, o_mem]
    merged = _mm("mix_merge", list(zip(flat, w["br3"])), "nt", [BF16], _merge,
                 [(gates, 0), (gates, D), (gates, 2 * D)])
    z, out = _mm("mix_out", [(merged, w["out"])], "nn", [F32, F32], _res_norm(1.0),
                 [(h, 0), (post_g.reshape(1, D), 0)], tn=D)
    saved = (h, u, proj, fl, c, rtot, o_fx32, lse_fx, kvm, o_mem32, lse_mem, gates, flat, merged, z)
    return out, saved


def _mix_bwd(lay, dh, saved, w, pre_g, post_g, b_forget, mem_n, dmem_n):
    D = lay.D
    (sq, sk, sv, sn), (fq, fk, fv, fn), (mq, mn) = _mix_tiles(lay)
    mem_d = mn * LANE // N_MEM_HEADS
    h, u, proj, fl, c, rtot, o_fx32, lse_fx, kvm, o_mem32, lse_mem, gates, flat, merged, z = saved
    dz, d_post = _rms_bwd("mix_dres", z, post_g, dh, BF16)
    outs = _mm("mix_dmerge", [(dz, w["out"])] + list(zip(flat, w["br3"])), "nt", [BF16] * 6, _merge_bwd,
               [(gates, 0), (gates, D), (gates, 2 * D)], tn=512)
    d_branch, d_gate = outs[:3], outs[3:]
    d_wout = _mm("mix_dwout", [(merged, dz)], "tn", [BF16])
    d_o = [_mm("mix_dbr%d" % i, [(d_branch[i], w["br3"][i])], "nn", [BF16]) for i in range(3)]
    d_wbr = [_mm("mix_dwbr%d" % i, [(d_branch[i], flat[i])], "tn", [BF16]) for i in range(3)]
    d_bgate = jnp.concatenate([_colsum("mix_dbgate%d" % i, d_gate[i]) for i in range(3)])
    d_wgate = [_mm("mix_dwgate%d" % i, [(d_gate[i], u)], "tn", [BF16]) for i in range(3)]

    d_sb = _sb_bwd((proj, sq), (proj, sk), (proj, sv), d_o[0], rtot, sn, HEAD_DIM, HEAD_DIM ** -0.5)
    *d_fx, dc, dc_rows = _attn_bwd("fox_bwd", (proj, fq), (proj, fk), (proj, fv), o_fx32, d_o[1], lse_fx, fn,
                                   HEAD_DIM, HEAD_DIM ** -0.5, c)
    dq_m, dk_m, dv_m = _attn_bwd("mem_bwd", (proj, mq), (kvm, 0), (kvm, mn), o_mem32, d_o[2], lse_mem, mn,
                                 mem_d, mem_d ** -0.5)
    dfl, d_bforget = _decay_bwd(dc, dc_rows, fl, b_forget.reshape(-1, 1))
    pieces = list(d_sb) + list(d_fx) + [dq_m]
    dflp = jnp.pad(dfl.T.astype(BF16), ((0, 0), (0, LANE - dfl.shape[0])))
    offs = [sum(t.shape[1] for t in pieces[:i]) for i in range(len(pieces) + 1)]
    win_rows = [w["win"][offs[i]:offs[i + 1]] for i in range(len(pieces))]
    du = _mm("mix_du", list(zip(d_gate, w["gate3"])) + list(zip(pieces, win_rows)) + [(dflp, w["wfl"])], "nn",
             [F32], _sum_accs, tm=256, tn=512)
    d_rows = [_mm("mix_dwin%d" % i, [(t, u)], "tn", [BF16]) for i, t in enumerate(pieces)]
    d_wfl = _mm("mix_dwfl", [(dflp, u)], "tn", [BF16])
    d_win = _unalign_win(lay, _pad_rows(jnp.concatenate(list(d_rows) + [d_wfl], axis=0), lay.in_cols))
    dh_in, d_pre = _rms_bwd("mix_dnorm", h, pre_g, du, F32, res=dh)

    dkvm = jnp.concatenate([dk_m, dv_m], axis=1)
    d_wkv = _mm("mem_dwkv", [(mem_n, dkvm)], "tn", [BF16])
    dmem_n = _mm("mem_dn", [(dkvm, w["kv"])], "nt", [F32], lambda accs, ex: [accs[0] + ex[0]], [(dmem_n, 0)])
    grads = {"win": d_win, "kv": d_wkv, "gate": jnp.concatenate(d_wgate, axis=0),
             "br": jnp.concatenate(d_wbr, axis=1), "out": d_wout}
    return dh_in, d_pre, d_post, d_bforget, d_bgate, grads, dmem_n


def _layer_fwd(lay, h, w, sp, mem_n):
    h1, s1 = _ffn_fwd("1", h, sp["ffn1_pre_g"], sp["ffn1_post_g"], w["g1"], w["u1"], w["d1"])
    h2, s2 = _mix_fwd(lay, h1, w, sp["mix_pre_g"], sp["mix_post_g"], sp["b_forget"], sp["b_gate"], mem_n)
    h3, s3 = _ffn_fwd("2", h2, sp["ffn2_pre_g"], sp["ffn2_post_g"], w["g2"], w["u2"], w["d2"])
    return h3, (s1, s2, s3)


def _layer_bwd(lay, dh, saved, w, sp, mem_n, dmem_n):
    s1, s2, s3 = saved
    dh, d_pre2, d_post2, d_g2, d_u2, d_d2 = _ffn_bwd("2", dh, s3, sp["ffn2_pre_g"], sp["ffn2_post_g"],
                                                     w["g2"], w["u2"], w["d2"])
    dh, d_mpre, d_mpost, d_bforget, d_bgate, g, dmem_n = _mix_bwd(
        lay, dh, s2, w, sp["mix_pre_g"], sp["mix_post_g"], sp["b_forget"], mem_n, dmem_n)
    dh, d_pre1, d_post1, d_g1, d_u1, d_d1 = _ffn_bwd("1", dh, s1, sp["ffn1_pre_g"], sp["ffn1_post_g"],
                                                     w["g1"], w["u1"], w["d1"])
    g.update({"g1": d_g1, "u1": d_u1, "d1": d_d1, "g2": d_g2, "u2": d_u2, "d2": d_d2})
    g["br"] = g["br"].reshape(N_DEV, lay.br_shard, lay.br_in).reshape(-1, lay.D)
    packed = jnp.concatenate([lay.spread(g[key], key) for key in lay.seg], axis=1)
    small = {"ffn1_pre_g": d_pre1, "ffn1_post_g": d_post1, "mix_pre_g": d_mpre, "mix_post_g": d_mpost,
             "ffn2_pre_g": d_pre2, "ffn2_post_g": d_post2, "b_gate": d_bgate, "b_forget": d_bforget}
    return dh, packed, small, dmem_n


_SHARDED = ["ffn1_w_gate", "ffn1_w_up", "ffn1_w_down", "w_in", "w_mem_kv", "w_gate", "w_br_sb", "w_br_fox",
            "w_br_mem", "w_out", "ffn2_w_gate", "ffn2_w_up", "ffn2_w_down"]
_SMALL_LAYER = ["ffn1_pre_g", "ffn1_post_g", "mix_pre_g", "mix_post_g", "ffn2_pre_g", "ffn2_post_g", "b_gate",
                "b_forget"]
_WEIGHTS = ["ffn1_pre_g", "ffn1_post_g", "ffn1_w_gate", "ffn1_w_up", "ffn1_w_down", "mix_pre_g", "mix_post_g",
            "w_in", "b_forget", "mem_norm_g", "w_mem_kv", "w_gate", "b_gate", "w_br_sb", "w_br_fox", "w_br_mem",
            "w_out", "ffn2_pre_g", "ffn2_post_g", "ffn2_w_gate", "ffn2_w_up", "ffn2_w_down"]


def _pack_small(vals, L, D):
    rows = []
    for l in range(L):
        for name in _SMALL_LAYER:
            t = vals[name][l]
            rows.append(jnp.pad(t, (0, -t.shape[0] % D)).reshape(-1, D))
    rows.append(vals["mem_norm_g"].reshape(1, D))
    packed = jnp.concatenate(rows, axis=0)
    return _pad_rows(packed, -(-packed.shape[0] // 8) * 8)


def _unpack_small(packed, shapes, L, D):
    out = {name: [] for name in _SMALL_LAYER}
    r = 0
    for l in range(L):
        for name in _SMALL_LAYER:
            n = shapes[name][1]
            nr = -(-n // D)
            out[name].append(packed[r:r + nr].reshape(-1)[:n])
            r += nr
    res = {name: jnp.stack(v) for name, v in out.items()}
    res["mem_norm_g"] = packed[r]
    return res


def _unpack_grads(lay, g, l_shapes):
    def seg(key):
        off, n = lay.seg[key]
        return g[off:off + n]
    br = seg("br").reshape(lay.br_shard, lay.br_in).T
    third = lay.br_in // 3
    return {
        "ffn1_w_gate": seg("g1").T, "ffn1_w_up": seg("u1").T, "ffn1_w_down": seg("d1"),
        "w_in": seg("win")[:lay.in_shard].T, "w_mem_kv": seg("kv"), "w_gate": seg("gate").T,
        "w_br_sb": br[:third], "w_br_fox": br[third:2 * third], "w_br_mem": br[2 * third:],
        "w_out": seg("out"), "ffn2_w_gate": seg("g2").T, "ffn2_w_up": seg("u2").T, "ffn2_w_down": seg("d2"),
    }


class _Exchanges:
    def gather(self, name, block):
        return _all_gather(name, block)

    def gather_start(self, block):
        return _exchange_start("ag_start", block, per_peer=False)

    def gather_wait(self, started, after, block):
        return _exchange_wait("ag_wait", started, after), (block, _my_index())

    def scatter(self, parts):
        return _reduce_scatter("w", parts)

    def scatter_start(self, parts):
        return _exchange_start("rs_start", parts, per_peer=True)

    def scatter_wait(self, started, after, parts):
        return _sum_landed("rs_sum8", _exchange_wait("rs_wait", started, after), parts)

    def token(self, started):
        return started[4]

    def loss_sum(self, part):
        return lax.psum(part, ("x", "y", "c"))


def _step(p, m, v, x, mem, tgt, ex):
    L, D = p["ffn1_pre_g"].shape
    lay = _Layout(D, p["ffn1_w_gate"].shape[2], p["w_in"].shape[2], p["w_mem_kv"].shape[1], p["w_gate"].shape[2],
                  3 * p["w_br_sb"].shape[1], p["w_br_sb"].shape[2], p["w_out"].shape[1])
    blocks = [_pack_layer(lay, l, p) for l in range(L)]
    sps = [{name: p[name][l] for name in _SMALL_LAYER} for l in range(L)]

    mem_n = _rms_fwd("mem_norm", mem, p["mem_norm_g"], BF16)
    gathered, own = ex.gather("ag_weights", blocks[0]), None
    h, saved, ws = x, [], []
    for l in range(L):
        if l + 1 < L:
            nxt, gathered = lax.optimization_barrier((blocks[l + 1], gathered))
            started = ex.gather_start(nxt)
            sp = _after(sps[l], "ffn1_pre_g", ex.token(started))
        else:
            sp = sps[l]
        ws.append(_unpack_layer(lay, gathered, own))
        h, s = _layer_fwd(lay, h, ws[l], sp, mem_n)
        saved.append(s)
        if l + 1 < L:
            gathered, own = ex.gather_wait(started, h, blocks[l + 1])
    loss_part, dh = _loss_grad(h, tgt)
    loss = ex.loss_sum(loss_part)

    dmem_n = jnp.zeros(mem.shape, F32)
    big, small = [None] * L, {name: [None] * L for name in _SMALL_LAYER}
    flying, token = {}, None
    for l in reversed(range(L)):
        sp = sps[l] if token is None else _after(sps[l], "ffn2_post_g", token)
        dh, packed, sm, dmem_n = _layer_bwd(lay, dh, saved[l], ws[l], sp, mem_n, dmem_n)
        if l > 0:
            flying[l] = (ex.scatter_start(packed), packed)
            token = ex.token(flying[l][0])
        else:
            big[l] = _unpack_grads(lay, ex.scatter(packed), None)
        for name in _SMALL_LAYER:
            small[name][l] = sm[name]
    for l, (started, packed) in flying.items():
        big[l] = _unpack_grads(lay, ex.scatter_wait(started, dh, packed), None)
    _, d_memg = _rms_bwd("mem_dnorm", mem, p["mem_norm_g"], dmem_n, F32)

    small_g = {name: jnp.stack(vs) for name, vs in small.items()}
    small_g["mem_norm_g"] = d_memg
    small_names = _SMALL_LAYER + ["mem_norm_g"]
    shapes = {name: p[name].shape for name in small_names}
    g_all = ex.gather("ag_small", _pack_small(small_g, L, D))
    packs = [_pack_small({name: t[name] for name in small_names}, L, D) for t in (p, m, v)]
    res = [_unpack_small(t, shapes, L, D) for t in _adamw("adamw_small", g_all, *packs)]

    out = {kind: {} for kind in ("grad", "delta", "new_m", "new_v")}
    for name in small_names:
        for kind, r in zip(("grad", "delta", "new_m", "new_v"), res):
            out[kind][name] = r[name].reshape(p[name].shape)
    for name in _SHARDED:
        g = jnp.stack([big[l][name] for l in range(L)])
        shp = g.shape
        flat = lambda t: t.reshape(-1, shp[-1])
        r = _adamw("adamw_" + name, flat(g)[None], flat(p[name]), flat(m[name]), flat(v[name]))
        for kind, t in zip(("grad", "delta", "new_m", "new_v"), r):
            out[kind][name] = t.reshape(shp)
    return loss, dh, out


def kernel(x, mem, ffn1_pre_g, ffn1_post_g, ffn1_w_gate, ffn1_w_up, ffn1_w_down, mix_pre_g, mix_post_g, w_in, b_forget, mem_norm_g, w_mem_kv, w_gate, b_gate, w_br_sb, w_br_fox, w_br_mem, w_out, ffn2_pre_g, ffn2_post_g, ffn2_w_gate, ffn2_w_up, ffn2_w_down, loss_target, m_ffn1_pre_g, m_ffn1_post_g, m_ffn1_w_gate, m_ffn1_w_up, m_ffn1_w_down, m_mix_pre_g, m_mix_post_g, m_w_in, m_b_forget, m_mem_norm_g, m_w_mem_kv, m_w_gate, m_b_gate, m_w_br_sb, m_w_br_fox, m_w_br_mem, m_w_out, m_ffn2_pre_g, m_ffn2_post_g, m_ffn2_w_gate, m_ffn2_w_up, m_ffn2_w_down, v_ffn1_pre_g, v_ffn1_post_g, v_ffn1_w_gate, v_ffn1_w_up, v_ffn1_w_down, v_mix_pre_g, v_mix_post_g, v_w_in, v_b_forget, v_mem_norm_g, v_w_mem_kv, v_w_gate, v_b_gate, v_w_br_sb, v_w_br_fox, v_w_br_mem, v_w_out, v_ffn2_pre_g, v_ffn2_post_g, v_ffn2_w_gate, v_ffn2_w_up, v_ffn2_w_down):
    p = dict(zip(_WEIGHTS, (ffn1_pre_g, ffn1_post_g, ffn1_w_gate, ffn1_w_up, ffn1_w_down, mix_pre_g, mix_post_g, w_in, b_forget, mem_norm_g, w_mem_kv, w_gate, b_gate, w_br_sb, w_br_fox, w_br_mem, w_out, ffn2_pre_g, ffn2_post_g, ffn2_w_gate, ffn2_w_up, ffn2_w_down)))
    m = dict(zip(_WEIGHTS, (m_ffn1_pre_g, m_ffn1_post_g, m_ffn1_w_gate, m_ffn1_w_up, m_ffn1_w_down, m_mix_pre_g, m_mix_post_g, m_w_in, m_b_forget, m_mem_norm_g, m_w_mem_kv, m_w_gate, m_b_gate, m_w_br_sb, m_w_br_fox, m_w_br_mem, m_w_out, m_ffn2_pre_g, m_ffn2_post_g, m_ffn2_w_gate, m_ffn2_w_up, m_ffn2_w_down)))
    v = dict(zip(_WEIGHTS, (v_ffn1_pre_g, v_ffn1_post_g, v_ffn1_w_gate, v_ffn1_w_up, v_ffn1_w_down, v_mix_pre_g, v_mix_post_g, v_w_in, v_b_forget, v_mem_norm_g, v_w_mem_kv, v_w_gate, v_b_gate, v_w_br_sb, v_w_br_fox, v_w_br_mem, v_w_out, v_ffn2_pre_g, v_ffn2_post_g, v_ffn2_w_gate, v_ffn2_w_up, v_ffn2_w_down)))
    loss, dx, out = _step(p, m, v, x[0], mem[0], loss_target[0], _Exchanges())
    return (loss, dx[None], *[out["grad"][n] for n in _WEIGHTS], *[out["delta"][n] for n in _WEIGHTS],
            *[out["new_m"][n] for n in _WEIGHTS], *[out["new_v"][n] for n in _WEIGHTS])
```

```python
import functools
import math

import jax
import jax.numpy as jnp
from jax import lax
from jax.experimental import pallas as pl
from jax.experimental.pallas import tpu as pltpu

F32 = jnp.float32
BF16 = jnp.bfloat16

LANE = 128
SUBLANE_BF16 = 16
VMEM_LIMIT = 56 * 1024 * 1024
N_DEV = 8
MESH = pl.DeviceIdType.MESH
ANY = pl.BlockSpec(memory_space=pl.ANY)

RMS_EPS = 1e-6
HEAD_DIM = 64
N_SB_HEADS = 8
N_FOX_HEADS = 8
N_MEM_HEADS = 4
NEG = -1e30
ATT_TQ = 1024
ATT_TK = 256
DECAY_TK = 128
IN_TILE = 1280

ADAM_LR = 0.001
ADAM_B1 = 0.9
ADAM_B2 = 0.999
ADAM_EPS = 1e-08
ADAM_WD = 0.01
ADAM_STEP = 10


def _tile(n, target, mult=LANE):
    best = None
    for t in range(mult, min(n, target) + 1, mult):
        if n % t == 0:
            best = t
    return best if best is not None else n


def _cparams(sem):
    return pltpu.CompilerParams(dimension_semantics=sem, vmem_limit_bytes=VMEM_LIMIT)


_DIMS = {"nn": (((1,), (0,)), ((), ())), "nt": (((1,), (1,)), ((), ())), "tn": (((0,), (0,)), ((), ()))}


def _dot(a, b, mode="nn"):
    return lax.dot_general(a.astype(BF16), b.astype(BF16), _DIMS[mode], preferred_element_type=F32)


def _mm(name, pairs, mode, out_dtypes, epilogue=None, extras=(), tm=512, tn=1024, out_rows=None):
    a0, b0 = pairs[0]
    M = a0.shape[1] if mode == "tn" else a0.shape[0]
    N = b0.shape[0] if mode == "nt" else b0.shape[1]
    tm = _tile(M, tm)
    tn = _tile(N, tn)
    np_, ne, no = len(pairs), len(extras), len(out_dtypes)

    def body(*refs):
        a_refs, b_refs = refs[:np_], refs[np_:2 * np_]
        e_refs = refs[2 * np_:2 * np_ + ne]
        o_refs = refs[2 * np_ + ne:]
        accs = [_dot(a[...], b[...], mode) for a, b in zip(a_refs, b_refs)]
        outs = epilogue(accs, [e[...] for e in e_refs]) if epilogue is not None else accs
        for o, val in zip(o_refs, outs):
            o[...] = val.astype(o.dtype)

    in_specs = []
    for a, _ in pairs:
        if mode == "tn":
            in_specs.append(pl.BlockSpec((a.shape[0], tm), lambda j, i: (0, i)))
        else:
            in_specs.append(pl.BlockSpec((tm, a.shape[1]), lambda j, i: (i, 0)))
    for _, b in pairs:
        if mode == "nt":
            in_specs.append(pl.BlockSpec((tn, b.shape[1]), lambda j, i: (j, 0)))
        else:
            in_specs.append(pl.BlockSpec((b.shape[0], tn), lambda j, i: (0, j)))
    for e, off in extras:
        if e.shape[0] == 1:
            in_specs.append(pl.BlockSpec((1, tn), functools.partial(lambda j, i, o: (0, j + o), o=off // tn)))
        else:
            in_specs.append(pl.BlockSpec((tm, tn), functools.partial(lambda j, i, o: (i, j + o), o=off // tn)))
    rows = [tm if r is None else r for r in (out_rows or [None] * no)]
    out_specs = [pl.BlockSpec((r, tn), lambda j, i: (i, j)) for r in rows]
    outs = pl.pallas_call(
        body, name=name, grid=(N // tn, M // tm),
        in_specs=in_specs, out_specs=out_specs,
        out_shape=[jax.ShapeDtypeStruct((M // tm * r, N), dt) for r, dt in zip(rows, out_dtypes)],
        compiler_params=_cparams(("parallel", "parallel")),
    )(*[a for a, _ in pairs], *[b for _, b in pairs], *[e for e, _ in extras])
    return outs[0] if no == 1 else outs


def _sum_accs(accs, _):
    total = accs[0]
    for acc in accs[1:]:
        total = total + acc
    return [total]


def _rstd(x):
    return lax.rsqrt(jnp.mean(x * x, axis=-1, keepdims=True) + RMS_EPS)


def _rms_fwd(name, x, g, out_dtype, res=None, scale=1.0, tr=512):
    R, D = x.shape
    tr = _tile(R, tr, 8)
    has_res = res is not None

    def body(*refs):
        x_ref, g_ref = refs[:2]
        o_ref = refs[-1]
        xv = x_ref[...]
        y = (xv * _rstd(xv)) * g_ref[...]
        if has_res:
            y = refs[2][...] + scale * y
        o_ref[...] = y.astype(o_ref.dtype)

    row = pl.BlockSpec((tr, D), lambda i: (i, 0))
    gain = pl.BlockSpec((1, D), lambda i: (0, 0))
    return pl.pallas_call(
        body, name=name, grid=(R // tr,),
        in_specs=[row, gain] + ([row] if has_res else []), out_specs=row,
        out_shape=jax.ShapeDtypeStruct((R, D), out_dtype),
        compiler_params=_cparams(("parallel",)),
    )(x, g.reshape(1, D), *([res] if has_res else []))


def _rms_bwd(name, x, g, dy, out_dtype, scale=1.0, res=None, tr=512):
    R, D = x.shape
    tr = _tile(R, tr, 8)
    has_res = res is not None

    def body(*refs):
        x_ref, g_ref, dy_ref = refs[:3]
        dx_ref, dg_ref = refs[-2:]
        i = pl.program_id(0)
        xv = x_ref[...]
        xhat = xv * _rstd(xv)
        dyv = dy_ref[...].astype(F32) * scale
        gy = dyv * g_ref[...]
        dx = _rstd(xv) * (gy - xhat * jnp.mean(gy * xhat, axis=-1, keepdims=True))
        if has_res:
            dx = refs[3][...] + dx
        dx_ref[...] = dx.astype(dx_ref.dtype)
        part = jnp.sum(dyv * xhat, axis=0, keepdims=True)

        @pl.when(i == 0)
        def _():
            dg_ref[...] = part

        @pl.when(i > 0)
        def _():
            dg_ref[...] += part

    row = pl.BlockSpec((tr, D), lambda i: (i, 0))
    gain = pl.BlockSpec((1, D), lambda i: (0, 0))
    dx, dg = pl.pallas_call(
        body, name=name, grid=(R // tr,),
        in_specs=[row, gain, row] + ([row] if has_res else []), out_specs=[row, gain],
        out_shape=[jax.ShapeDtypeStruct((R, D), out_dtype), jax.ShapeDtypeStruct((1, D), F32)],
        compiler_params=_cparams(("arbitrary",)),
    )(x, g.reshape(1, D), dy, *([res] if has_res else []))
    return dx, dg[0]


def _loss_grad(y, tgt, tr=512):
    R, D = y.shape
    tr = _tile(R, tr, 8)

    def body(y_ref, t_ref, dy_ref, loss_ref):
        i = pl.program_id(0)
        d = y_ref[...] - t_ref[...]
        dy_ref[...] = d / D
        part = 0.5 * jnp.sum(jnp.mean(d * d, axis=-1, keepdims=True), axis=0, keepdims=True)
        tile = jnp.broadcast_to(part, loss_ref.shape)

        @pl.when(i == 0)
        def _():
            loss_ref[...] = tile

        @pl.when(i > 0)
        def _():
            loss_ref[...] += tile

    row = pl.BlockSpec((tr, D), lambda i: (i, 0))
    dy, loss = pl.pallas_call(
        body, name="loss_grad", grid=(R // tr,),
        in_specs=[row, row], out_specs=[row, pl.BlockSpec((8, LANE), lambda i: (0, 0))],
        out_shape=[jax.ShapeDtypeStruct((R, D), F32), jax.ShapeDtypeStruct((8, LANE), F32)],
        compiler_params=_cparams(("arbitrary",)),
    )(y, tgt)
    return loss[0, 0], dy


def _colsum(name, x, tr=512, tn=1024):
    R, N = x.shape
    tr, tn = _tile(R, tr, 8), _tile(N, tn)

    def body(x_ref, o_ref):
        i = pl.program_id(1)
        part = jnp.sum(x_ref[...].astype(F32), axis=0, keepdims=True)

        @pl.when(i == 0)
        def _():
            o_ref[...] = part

        @pl.when(i > 0)
        def _():
            o_ref[...] += part

    out = pl.pallas_call(
        body, name=name, grid=(N // tn, R // tr),
        in_specs=[pl.BlockSpec((tr, tn), lambda j, i: (i, j))], out_specs=pl.BlockSpec((1, tn), lambda j, i: (0, j)),
        out_shape=jax.ShapeDtypeStruct((1, N), F32),
        compiler_params=_cparams(("parallel", "arbitrary")),
    )(x)
    return out[0]


def _tri(tk, rel):
    j = lax.broadcasted_iota(jnp.int32, (tk, tk), 0)
    s = lax.broadcasted_iota(jnp.int32, (tk, tk), 1)
    return rel(j, s).astype(BF16)


def _dot_split(x, m, parts=2):
    total = None
    rem = x
    for _ in range(parts):
        piece = rem.astype(BF16)
        rem = rem - piece.astype(F32)
        term = jnp.dot(piece, m, preferred_element_type=F32)
        total = term if total is None else total + term
    return total


def _log_not_and_beta(z, mask):
    ln = -(jnp.maximum(z, 0.0) + jnp.log(1.0 + jnp.exp(-jnp.abs(z))))
    return (ln if mask is None else jnp.where(mask, ln, 0.0)), ln + z


def _att_tiles(T, Tk, causal):
    tq = min(ATT_TQ, T)
    tk = min(ATT_TK, tq if causal else Tk)
    return tq, tk, (tq if causal else Tk) // tk


def _key_base(j, tq):
    return j * tq if isinstance(j, int) else pl.multiple_of(j * tq, tq)


def _is_pow2(scale):
    return math.log2(scale).is_integer()


def _per_head(x, hpb, d):
    if hpb == 1:
        return [x]
    lane = lax.broadcasted_iota(jnp.int32, x.shape, 1)
    return [jnp.where((lane >= h * d) & (lane < (h + 1) * d), x, jnp.zeros_like(x)) for h in range(hpb)]


def _join_heads(xs, d):
    out = xs[-1]
    if len(xs) > 1:
        lane = lax.broadcasted_iota(jnp.int32, out.shape, 1)
        for h in reversed(range(len(xs) - 1)):
            out = jnp.where(lane < (h + 1) * d, xs[h], out)
    return out


def _lane_tile(rows, off, whole):
    if whole:
        return pl.BlockSpec((rows, LANE), lambda g, i: (0, off + g))
    return pl.BlockSpec((rows, LANE), lambda g, i: (i, off + g))


def _sb_fwd(q, k, v, n_tiles, d, scale):
    T = q[0].shape[0]
    hpb = LANE // d
    tq, tk, nsub = _att_tiles(T, T, True)
    assert _is_pow2(scale)

    def body(q_ref, k_ref, v_ref, ob_ref, rt_ref, acc_ref, r_ref):
        qi = pl.program_id(1)
        qh = _per_head(q_ref[...] * scale, hpb, d)
        acc_ref[...] = jnp.zeros_like(acc_ref)
        r_ref[...] = jnp.zeros_like(r_ref)
        row = lax.broadcasted_iota(jnp.int32, (tq, tk), 0)
        col = lax.broadcasted_iota(jnp.int32, (tq, tk), 1)
        after = _tri(tk, lambda j, s: j > s)

        def step(j, diagonal):
            base = _key_base(j, tq)
            for h in range(hpb):
                parts = []
                for u in reversed(range(nsub)):
                    z = _dot(qh[h], k_ref[pl.ds(base + u * tk, tk), :], "nt")
                    mask = (col + u * tk) < row if diagonal else None
                    ln, lb = _log_not_and_beta(z, mask)
                    between = _dot_split(ln, after, parts=1)
                    first = ln[:, 0:1].astype(BF16).astype(F32)
                    parts.append((u, lb, between, between[:, 0:1] + first, mask))
                r = r_ref[h]
                out = None
                for u, lb, between, total, mask in parts:
                    w = jnp.exp(lb + between + r)
                    if diagonal:
                        w = jnp.where(mask, w, 0.0)
                    term = _dot(w, v_ref[pl.ds(base + u * tk, tk), :])
                    out = term if out is None else out + term
                    r = r + total
                acc_ref[h] += out
                r_ref[h] = r

        def below(i, carry):
            step(qi - 1 - i, False)
            return carry

        step(qi, True)
        lax.fori_loop(0, qi, below, 0)
        ob_ref[...] = _join_heads([acc_ref[h] for h in range(hpb)], d).astype(ob_ref.dtype)
        rt_ref[...] = r_ref[...]

    out = pl.BlockSpec((tq, LANE), lambda g, i: (i, g))
    col = pl.BlockSpec((hpb, tq, 1), lambda g, i: (g, i, 0))
    return pl.pallas_call(
        body, name="sb_fwd", grid=(n_tiles, T // tq),
        in_specs=[_lane_tile(tq, q[1], False), _lane_tile(T, k[1], True), _lane_tile(T, v[1], True)],
        out_specs=[out, col],
        out_shape=[jax.ShapeDtypeStruct((T, n_tiles * LANE), BF16), jax.ShapeDtypeStruct((n_tiles * hpb, T, 1), F32)],
        scratch_shapes=[pltpu.VMEM((hpb, tq, LANE), F32), pltpu.VMEM((hpb, tq, 1), F32)],
        compiler_params=_cparams(("parallel", "arbitrary")),
    )(q[0], k[0], v[0])


def _sb_bwd(q, k, v, do, rtot, n_tiles, d, scale):
    T = q[0].shape[0]
    hpb = LANE // d
    tq, tk, nsub = _att_tiles(T, T, True)
    assert _is_pow2(scale)

    def body(q_ref, k_ref, v_ref, do_ref, rt_ref, dq_ref, dk_ref, dv_ref, dk_acc, dv_acc, dq_acc, p_ref, c_ref):
        qi = pl.program_id(1)

        @pl.when(qi == 0)
        def _():
            dk_acc[...] = jnp.zeros_like(dk_acc)
            dv_acc[...] = jnp.zeros_like(dv_acc)

        qh = _per_head(q_ref[...] * scale, hpb, d)
        doh = _per_head(do_ref[...], hpb, d)
        dq_acc[...] = jnp.zeros_like(dq_acc)
        p_ref[...] = jnp.zeros_like(p_ref)
        c_ref[...] = jnp.zeros_like(c_ref)
        row = lax.broadcasted_iota(jnp.int32, (tq, tk), 0)
        col = lax.broadcasted_iota(jnp.int32, (tq, tk), 1)
        upto = _tri(tk, lambda j, s: j <= s)
        before = _tri(tk, lambda j, s: j < s)
        rt_wide = [jnp.broadcast_to(rt_ref[h], (tq, tk)) for h in range(hpb)]

        def step(j, diagonal):
            base = _key_base(j, tq)
            for h in range(hpb):
                first = []
                for u in range(nsub):
                    ks = base + u * tk
                    kv = k_ref[pl.ds(ks, tk), :]
                    z = _dot(qh[h], kv, "nt")
                    mask = (col + u * tk) < row if diagonal else None
                    ln, lb = _log_not_and_beta(z, mask)
                    dw = _dot(doh[h], v_ref[pl.ds(ks, tk), :], "nt")
                    first.append((ks, kv, mask, lb, jnp.exp(lb), _dot_split(ln, upto, parts=1), dw))
                rt, pre, cpre = rt_wide[h], p_ref[h], c_ref[h]
                dq = None
                for ks, kv, mask, lb, sig, local, dw in first:
                    prefix = local + pre
                    w = jnp.exp(lb + (rt - prefix))
                    if diagonal:
                        w = jnp.where(mask, w, 0.0)
                    g = dw * w
                    c = _dot_split(g, before, parts=1) + cpre
                    dz = g * (1.0 - sig) - c * sig
                    if diagonal:
                        dz = jnp.where(mask, dz, 0.0)
                    term = _dot(dz, kv)
                    dq = term if dq is None else dq + term
                    dk_acc[pl.ds(ks, tk), :] += _dot(dz, qh[h], "tn")
                    dv_acc[pl.ds(ks, tk), :] += _dot(w, doh[h], "tn")
                    pre = prefix[:, tk - 1:tk]
                    cpre = c[:, tk - 1:tk] + g[:, tk - 1:tk]
                dq_acc[h] += dq
                p_ref[h] = pre
                c_ref[h] = cpre

        def below(j, carry):
            step(j, False)
            return carry

        lax.fori_loop(0, qi, below, 0)
        step(qi, True)
        dq_ref[...] = (_join_heads([dq_acc[h] for h in range(hpb)], d) * scale).astype(dq_ref.dtype)

        @pl.when(qi == pl.num_programs(1) - 1)
        def _():
            dk_ref[...] = dk_acc[...].astype(dk_ref.dtype)
            dv_ref[...] = dv_acc[...].astype(dv_ref.dtype)

    blk = pl.BlockSpec((tq, LANE), lambda g, i: (i, g))
    full = pl.BlockSpec((T, LANE), lambda g, i: (0, g))
    col = pl.BlockSpec((hpb, tq, 1), lambda g, i: (g, i, 0))
    wide = jax.ShapeDtypeStruct((T, n_tiles * LANE), BF16)
    return pl.pallas_call(
        body, name="sb_bwd", grid=(n_tiles, T // tq),
        in_specs=[_lane_tile(tq, q[1], False), _lane_tile(T, k[1], True), _lane_tile(T, v[1], True), blk, col],
        out_specs=[blk, full, full], out_shape=[wide, wide, wide],
        scratch_shapes=[pltpu.VMEM((T, LANE), F32), pltpu.VMEM((T, LANE), F32), pltpu.VMEM((hpb, tq, LANE), F32),
                        pltpu.VMEM((hpb, tq, 1), F32), pltpu.VMEM((hpb, tq, 1), F32)],
        compiler_params=_cparams(("parallel", "arbitrary")),
    )(q[0], k[0], v[0], do, rtot)


def _attn_fwd(name, q, k, v, n_tiles, d, scale, c=None):
    T, Tk = q[0].shape[0], k[0].shape[0]
    hpb = LANE // d
    H = n_tiles * hpb
    causal = c is not None
    tq, tk, nsub = _att_tiles(T, Tk, causal)
    fold = _is_pow2(scale)

    def body(*refs):
        q_ref, k_ref, v_ref = refs[:3]
        cc_ref, cr_ref = refs[3:5] if causal else (None, None)
        o_ref, ob_ref, lse_ref, m_ref, l_ref, acc_ref = refs[-6:]
        qi = pl.program_id(1)
        qh = _per_head(q_ref[...] * scale if fold else q_ref[...], hpb, d)
        bias = [jnp.broadcast_to(cc_ref[h], (tq, tk)) for h in range(hpb)] if causal else None
        ones = jnp.ones((tk, LANE), BF16)
        m_ref[...] = jnp.full_like(m_ref, NEG)
        l_ref[...] = jnp.zeros_like(l_ref)
        acc_ref[...] = jnp.zeros_like(acc_ref)
        row = lax.broadcasted_iota(jnp.int32, (tq, tk), 0)
        col = lax.broadcasted_iota(jnp.int32, (tq, tk), 1)

        def step(j, diagonal):
            base = _key_base(j, tq)
            for h in range(hpb):
                zs = []
                for u in range(nsub):
                    z = _dot(qh[h], k_ref[pl.ds(base + u * tk, tk), :], "nt")
                    if not fold:
                        z = z * scale
                    if causal:
                        z = z + bias[h] - cr_ref[h, j * nsub + u]
                    if diagonal:
                        z = jnp.where((col + u * tk) <= row, z, NEG)
                    zs.append(z)
                m_prev = m_ref[h]
                top = zs[0]
                for z in zs[1:]:
                    top = jnp.maximum(top, z)
                m_new = jnp.maximum(m_prev, jnp.max(top, axis=1, keepdims=True))
                alpha = jnp.exp(m_prev - m_new)
                l_new = alpha * l_ref[h]
                out = alpha * acc_ref[h]
                m_wide = jnp.broadcast_to(m_new, (tq, tk))
                for u, z in enumerate(zs):
                    p = jnp.exp(z - m_wide).astype(BF16)
                    l_new = l_new + jnp.dot(p, ones, preferred_element_type=F32)[:, 0:1]
                    out = out + _dot(p, v_ref[pl.ds(base + u * tk, tk), :])
                l_ref[h] = l_new
                acc_ref[h] = out
                m_ref[h] = m_new

        def below(j, carry):
            step(j, False)
            return carry

        if causal:
            lax.fori_loop(0, qi, below, 0)
            step(qi, True)
        else:
            step(0, False)
        o = _join_heads([acc_ref[h] / l_ref[h] for h in range(hpb)], d)
        o_ref[...] = o
        ob_ref[...] = o.astype(ob_ref.dtype)
        lse_ref[...] = m_ref[...] + jnp.log(l_ref[...])

    out = pl.BlockSpec((tq, LANE), lambda g, i: (i, g))
    col = pl.BlockSpec((hpb, tq, 1), lambda g, i: (g, i, 0))
    in_specs = [_lane_tile(tq, q[1], False), _lane_tile(Tk, k[1], True), _lane_tile(Tk, v[1], True)]
    args = [q[0], k[0], v[0]]
    if causal:
        in_specs += [col, pl.BlockSpec((hpb, T // tk, 1, tk), lambda g, i: (g, 0, 0, 0))]
        args += [c.reshape(H, T, 1), c.reshape(H, T // tk, 1, tk)]
    return pl.pallas_call(
        body, name=name, grid=(n_tiles, T // tq),
        in_specs=in_specs, out_specs=[out, out, col],
        out_shape=[jax.ShapeDtypeStruct((T, n_tiles * LANE), F32), jax.ShapeDtypeStruct((T, n_tiles * LANE), BF16),
                   jax.ShapeDtypeStruct((H, T, 1), F32)],
        scratch_shapes=[pltpu.VMEM((hpb, tq, 1), F32), pltpu.VMEM((hpb, tq, 1), F32),
                        pltpu.VMEM((hpb, tq, LANE), F32)],
        compiler_params=_cparams(("parallel", "arbitrary")),
    )(*args)


def _attn_bwd(name, q, k, v, o, do, lse, n_tiles, d, scale, c=None):
    T, Tk = q[0].shape[0], k[0].shape[0]
    hpb = LANE // d
    H = n_tiles * hpb
    causal = c is not None
    tq, tk, nsub = _att_tiles(T, Tk, causal)
    fold = _is_pow2(scale)

    def body(*refs):
        q_ref, k_ref, v_ref, o_ref, do_ref, lse_ref = refs[:6]
        cc_ref, cr_ref = refs[6:8] if causal else (None, None)
        n_out = 5 if causal else 3
        outs = refs[-(n_out + 3):-3]
        dq_ref, dk_ref, dv_ref = outs[:3]
        dc_ref, drow_ref = outs[3:5] if causal else (None, None)
        dk_acc, dv_acc, dq_acc = refs[-3:]
        qi = pl.program_id(1)

        @pl.when(qi == 0)
        def _():
            dk_acc[...] = jnp.zeros_like(dk_acc)
            dv_acc[...] = jnp.zeros_like(dv_acc)
            if causal:
                dc_ref[...] = jnp.zeros_like(dc_ref)

        qh = _per_head(q_ref[...] * scale if fold else q_ref[...], hpb, d)
        doh = _per_head(do_ref[...], hpb, d)
        delta_wide = [jnp.broadcast_to(jnp.sum(t.astype(F32) * o_ref[...], axis=1, keepdims=True), (tq, tk))
                      for t in doh]
        shift = [jnp.broadcast_to((cc_ref[h] - lse_ref[h]) if causal else -lse_ref[h], (tq, tk)) for h in range(hpb)]
        dq_acc[...] = jnp.zeros_like(dq_acc)
        if causal:
            drow_ref[...] = jnp.zeros_like(drow_ref)
        row = lax.broadcasted_iota(jnp.int32, (tq, tk), 0)
        col = lax.broadcasted_iota(jnp.int32, (tq, tk), 1)

        def step(j, diagonal):
            base = _key_base(j, tq)
            for h in range(hpb):
                dq, dsum = None, None
                for u in range(nsub):
                    ks = base + u * tk
                    kv = k_ref[pl.ds(ks, tk), :]
                    z = _dot(qh[h], kv, "nt")
                    if not fold:
                        z = z * scale
                    z = z + shift[h]
                    if causal:
                        z = z - cr_ref[h, j * nsub + u]
                    if diagonal:
                        z = jnp.where((col + u * tk) <= row, z, NEG)
                    p = jnp.exp(z)
                    ds = p * (_dot(doh[h], v_ref[pl.ds(ks, tk), :], "nt") - delta_wide[h])
                    term = _dot(ds, kv)
                    dq = term if dq is None else dq + term
                    dk = _dot(ds, qh[h], "tn")
                    dk_acc[pl.ds(ks, tk), :] += dk if fold else dk * scale
                    dv_acc[pl.ds(ks, tk), :] += _dot(p, doh[h], "tn")
                    if causal:
                        dc_ref[h, j * nsub + u] -= jnp.sum(ds, axis=0, keepdims=True)
                        dsum = ds if dsum is None else dsum + ds
                dq_acc[h] += dq
                if causal:
                    drow_ref[h] += jnp.sum(dsum, axis=1, keepdims=True)

        def below(j, carry):
            step(j, False)
            return carry

        if causal:
            lax.fori_loop(0, qi, below, 0)
            step(qi, True)
        else:
            step(0, False)
        dq_ref[...] = (_join_heads([dq_acc[h] for h in range(hpb)], d) * scale).astype(dq_ref.dtype)

        @pl.when(qi == pl.num_programs(1) - 1)
        def _():
            dk_ref[...] = dk_acc[...].astype(dk_ref.dtype)
            dv_ref[...] = dv_acc[...].astype(dv_ref.dtype)

    blk = pl.BlockSpec((tq, LANE), lambda g, i: (i, g))
    full = pl.BlockSpec((Tk, LANE), lambda g, i: (0, g))
    col = pl.BlockSpec((hpb, tq, 1), lambda g, i: (g, i, 0))
    crow = pl.BlockSpec((hpb, T // tk, 1, tk), lambda g, i: (g, 0, 0, 0))
    in_specs = [_lane_tile(tq, q[1], False), _lane_tile(Tk, k[1], True), _lane_tile(Tk, v[1], True), blk, blk, col]
    args = [q[0], k[0], v[0], o, do, lse]
    out_specs = [blk, full, full]
    out_shape = [jax.ShapeDtypeStruct((T, n_tiles * LANE), BF16), jax.ShapeDtypeStruct((Tk, n_tiles * LANE), BF16),
                 jax.ShapeDtypeStruct((Tk, n_tiles * LANE), BF16)]
    if causal:
        in_specs += [col, crow]
        args += [c.reshape(H, T, 1), c.reshape(H, T // tk, 1, tk)]
        out_specs += [crow, col]
        out_shape += [jax.ShapeDtypeStruct((H, T // tk, 1, tk), F32), jax.ShapeDtypeStruct((H, T, 1), F32)]
    outs = pl.pallas_call(
        body, name=name, grid=(n_tiles, T // tq),
        in_specs=in_specs, out_specs=out_specs, out_shape=out_shape,
        scratch_shapes=[pltpu.VMEM((Tk, LANE), F32), pltpu.VMEM((Tk, LANE), F32), pltpu.VMEM((hpb, tq, LANE), F32)],
        compiler_params=_cparams(("parallel", "arbitrary")),
    )(*args)
    if causal:
        return outs[0], outs[1], outs[2], outs[3].reshape(H, T), outs[4].reshape(H, T)
    return outs


def _decay_fwd(fl, b):
    H, T = fl.shape
    tk = DECAY_TK

    def body(x_ref, b_ref, c_ref):
        upto = _tri(tk, lambda j, s: j <= s)
        carry = jnp.zeros((H, 1), F32)
        for i in range(T // tk):
            xv = x_ref[:, i * tk:(i + 1) * tk] + b_ref[...]
            lf = jnp.minimum(xv, 0.0) - jnp.log(1.0 + jnp.exp(-jnp.abs(xv)))
            pref = _dot_split(lf, upto, parts=3) + carry
            c_ref[:, i * tk:(i + 1) * tk] = pref
            carry = pref[:, tk - 1:tk]

    vm = pl.BlockSpec(memory_space=pltpu.VMEM)
    return pl.pallas_call(
        body, name="decay_fwd", in_specs=[vm, vm], out_specs=vm,
        out_shape=jax.ShapeDtypeStruct((H, T), F32),
    )(fl, b)


def _decay_bwd(dc_cols, dc_rows, fl, b):
    H, T = fl.shape
    tk = DECAY_TK

    def body(dc_ref, dr_ref, x_ref, b_ref, dx_ref, db_ref):
        from_ = _tri(tk, lambda j, s: j >= s)
        carry = jnp.zeros((H, 1), F32)
        total = jnp.zeros((H, 1), F32)
        for i in reversed(range(T // tk)):
            sl = slice(i * tk, (i + 1) * tk)
            suffix = _dot_split(dc_ref[:, sl] + dr_ref[:, sl], from_, parts=3) + carry
            xv = x_ref[:, sl] + b_ref[...]
            dx = suffix / (1.0 + jnp.exp(xv))
            dx_ref[:, sl] = dx
            total = total + jnp.sum(dx, axis=1, keepdims=True)
            carry = suffix[:, 0:1]
        db_ref[...] = jnp.broadcast_to(total, db_ref.shape)

    vm = pl.BlockSpec(memory_space=pltpu.VMEM)
    dx, db = pl.pallas_call(
        body, name="decay_bwd", in_specs=[vm, vm, vm, vm], out_specs=[vm, vm],
        out_shape=[jax.ShapeDtypeStruct((H, T), F32), jax.ShapeDtypeStruct((H, LANE), F32)],
    )(dc_cols, dc_rows, fl, b)
    return dx, db[:, 0]


def _place():
    x, y, c = lax.axis_index("x"), lax.axis_index("y"), lax.axis_index("c")
    return x, y, c, [(1 - x, y), (x, 1 - y), (1 - x, 1 - y)]


def _all_gather(name, block):
    R, C = block.shape

    def body(x_ref, out_ref, send_sems, recv_sems, local_sem):
        x, y, c, chips = _place()
        me, sibling = (x, y, c), (x, y, 1 - c)

        def rows(px, py, pc):
            return out_ref.at[4 * px + 2 * py + pc]

        def copy(k, blk, to, src=None):
            return pltpu.make_async_remote_copy(
                src_ref=rows(*blk) if src is None else src, dst_ref=rows(*blk),
                send_sem=send_sems.at[k], recv_sem=recv_sems.at[k], device_id=to, device_id_type=MESH)

        mine = pltpu.make_async_copy(x_ref, rows(*me), local_sem)
        mine.start()
        first = [copy(0, me, sibling, src=x_ref)]
        first += [copy(1 + j, me, (*chip, c), src=x_ref) for j, chip in enumerate(chips)]
        for cp in first:
            cp.start()
        passed = [copy(4 + j, (*chip, c), sibling) for j, chip in enumerate(chips)]
        for j, chip in enumerate(chips):
            copy(1 + j, (*chip, c), me).wait_recv()
            passed[j].start()
        copy(0, sibling, me).wait_recv()
        for j, chip in enumerate(chips):
            copy(4 + j, (*chip, 1 - c), me).wait_recv()
        for cp in first + passed:
            cp.wait_send()
        mine.wait()

    return pl.pallas_call(
        body, name=name, in_specs=[ANY], out_specs=ANY,
        out_shape=jax.ShapeDtypeStruct((N_DEV, R, C), block.dtype),
        scratch_shapes=[pltpu.SemaphoreType.DMA((7,)), pltpu.SemaphoreType.DMA((7,)), pltpu.SemaphoreType.DMA(())],
    )(block)


def _swap_with_sibling(name, parts):
    _, R, C = parts.shape

    def body(p_ref, out_ref, send_sems, recv_sems):
        x, y, c, _ = _place()
        copies = [pltpu.make_async_remote_copy(
            src_ref=p_ref.at[2 * q + (1 - c)], dst_ref=out_ref.at[q],
            send_sem=send_sems.at[q], recv_sem=recv_sems.at[q], device_id=(x, y, 1 - c), device_id_type=MESH)
            for q in range(4)]
        for cp in copies:
            cp.start()
        for cp in copies:
            cp.wait_recv()
        for cp in copies:
            cp.wait_send()

    return pl.pallas_call(
        body, name=name, in_specs=[ANY], out_specs=ANY,
        out_shape=jax.ShapeDtypeStruct((4, R, C), parts.dtype),
        scratch_shapes=[pltpu.SemaphoreType.DMA((4,)), pltpu.SemaphoreType.DMA((4,))],
    )(parts)


def _add_own(name, parts, got, tr=512):
    _, R, C = parts.shape
    tr = _tile(R, tr, SUBLANE_BF16)

    def body(c_ref, p_ref, g_ref, o_ref):
        o_ref[...] = (p_ref[...].astype(F32) + g_ref[...].astype(F32)).astype(o_ref.dtype)

    return pl.pallas_call(
        body, name=name,
        grid_spec=pltpu.PrefetchScalarGridSpec(
            num_scalar_prefetch=1, grid=(4, R // tr),
            in_specs=[pl.BlockSpec((1, tr, C), lambda q, i, c: (2 * q + c[0], i, 0)),
                      pl.BlockSpec((1, tr, C), lambda q, i, c: (q, i, 0))],
            out_specs=pl.BlockSpec((1, tr, C), lambda q, i, c: (q, i, 0))),
        out_shape=jax.ShapeDtypeStruct((4, R, C), parts.dtype),
        compiler_params=_cparams(("parallel", "parallel")),
    )(lax.axis_index("c").astype(jnp.int32).reshape(1), parts, got)


def _swap_with_chips(name, parts):
    _, R, C = parts.shape

    def body(p_ref, out_ref, send_sems, recv_sems, local_sem):
        x, y, c, chips = _place()
        my_chip = 2 * x + y
        mine = pltpu.make_async_copy(p_ref.at[my_chip], out_ref.at[my_chip], local_sem)
        mine.start()
        sends = [pltpu.make_async_remote_copy(
            src_ref=p_ref.at[2 * cx + cy], dst_ref=out_ref.at[my_chip],
            send_sem=send_sems.at[j], recv_sem=recv_sems.at[j], device_id=(cx, cy, c), device_id_type=MESH)
            for j, (cx, cy) in enumerate(chips)]
        for cp in sends:
            cp.start()
        for j, (cx, cy) in enumerate(chips):
            pltpu.make_async_remote_copy(
                src_ref=p_ref.at[my_chip], dst_ref=out_ref.at[2 * cx + cy],
                send_sem=send_sems.at[j], recv_sem=recv_sems.at[j], device_id=(cx, cy, c), device_id_type=MESH,
            ).wait_recv()
        for cp in sends:
            cp.wait_send()
        mine.wait()

    return pl.pallas_call(
        body, name=name, in_specs=[ANY], out_specs=ANY,
        out_shape=jax.ShapeDtypeStruct((4, R, C), parts.dtype),
        scratch_shapes=[pltpu.SemaphoreType.DMA((3,)), pltpu.SemaphoreType.DMA((3,)), pltpu.SemaphoreType.DMA(())],
    )(parts)


def _sum_parts(name, parts, tr=512):
    P, R, C = parts.shape
    tr = _tile(R, tr, SUBLANE_BF16)

    def body(p_ref, o_ref):
        total = p_ref[0].astype(F32)
        for p in range(1, P):
            total = total + p_ref[p].astype(F32)
        o_ref[...] = total

    return pl.pallas_call(
        body, name=name, grid=(R // tr,),
        in_specs=[pl.BlockSpec((P, tr, C), lambda i: (0, i, 0))], out_specs=pl.BlockSpec((tr, C), lambda i: (i, 0)),
        out_shape=jax.ShapeDtypeStruct((R, C), F32),
        compiler_params=_cparams(("parallel",)),
    )(parts)


_HBM = pl.BlockSpec(memory_space=pltpu.HBM)
_SEM = pl.BlockSpec(memory_space=pltpu.SEMAPHORE)
_EFFECT = pltpu.SideEffectType.DATAFLOW_SIDE_EFFECTING


def _flipped(x, y, c, k):
    px, py, pc = (1 - x if k & 4 else x), (1 - y if k & 2 else y), (1 - c if k & 1 else c)
    return (px, py, pc), 4 * px + 2 * py + pc


def _exchange_start(name, src, per_peer):
    R, C = src.shape[-2:]

    def body(v_ref, land_ref, send_sem, recv_sem, v_thru, land_thru, token):
        x, y, c = lax.axis_index("x"), lax.axis_index("y"), lax.axis_index("c")
        me = 4 * x + 2 * y + c
        for k in range(1, N_DEV):
            peer, idx = _flipped(x, y, c, k)
            pltpu.make_async_remote_copy(
                src_ref=v_ref.at[idx] if per_peer else v_ref, dst_ref=land_ref.at[me],
                send_sem=send_sem, recv_sem=recv_sem, device_id=peer, device_id_type=MESH).start()
        token[...] = jnp.zeros_like(token)

    return pl.pallas_call(
        body, name=name,
        out_shape=(pltpu.SemaphoreType.DMA(()), pltpu.SemaphoreType.DMA(()), pltpu.HBM(src.shape, src.dtype),
                   pltpu.HBM((N_DEV, R, C), src.dtype), jax.ShapeDtypeStruct((8, LANE), F32)),
        in_specs=(_HBM, _HBM), out_specs=(_SEM, _SEM, _HBM, _HBM, pl.BlockSpec(memory_space=pltpu.VMEM)),
        input_output_aliases={0: 2, 1: 3},
        compiler_params=pltpu.CompilerParams(has_side_effects=_EFFECT),
    )(pltpu.with_memory_space_constraint(src, pltpu.HBM),
      pltpu.with_memory_space_constraint(lax.empty((N_DEV, R, C), src.dtype), pltpu.HBM))


def _exchange_wait(name, started, after):
    send_sem, recv_sem, v_thru, land_thru, _ = started

    def body(v_ref, land_ref, send_sem, recv_sem, after_ref, v_dead, got_ref):
        x, y, c = lax.axis_index("x"), lax.axis_index("y"), lax.axis_index("c")
        seven = land_ref.at[pl.ds(0, N_DEV - 1)]
        drain = pltpu.make_async_remote_copy(
            src_ref=seven, dst_ref=seven, send_sem=send_sem, recv_sem=recv_sem,
            device_id=(x, y, c), device_id_type=MESH)
        drain.wait_send()
        drain.wait_recv()

    return pl.pallas_call(
        body, name=name,
        out_shape=(pltpu.HBM(v_thru.shape, v_thru.dtype), pltpu.HBM(land_thru.shape, land_thru.dtype)),
        in_specs=(_HBM, _HBM, _SEM, _SEM, ANY), out_specs=(_HBM, _HBM), input_output_aliases={0: 0, 1: 1},
        compiler_params=pltpu.CompilerParams(has_side_effects=_EFFECT),
    )(v_thru, land_thru, send_sem, recv_sem, after)


def _my_index():
    return 4 * lax.axis_index("x") + 2 * lax.axis_index("y") + lax.axis_index("c")


def _sum_landed(name, landed, parts, tr=512):
    P, R, C = landed.shape
    tr = _tile(R, tr, SUBLANE_BF16)

    def body(me_ref, l_ref, own_ref, o_ref):
        total = None
        for s in range(P):
            part = jnp.where(me_ref[0] == s, own_ref[0], l_ref[s]).astype(F32)
            total = part if total is None else total + part
        o_ref[...] = total

    return pl.pallas_call(
        body, name=name,
        grid_spec=pltpu.PrefetchScalarGridSpec(
            num_scalar_prefetch=1, grid=(R // tr,),
            in_specs=[pl.BlockSpec((P, tr, C), lambda i, me: (0, i, 0)),
                      pl.BlockSpec((1, tr, C), lambda i, me: (me[0], i, 0))],
            out_specs=pl.BlockSpec((tr, C), lambda i, me: (i, 0))),
        out_shape=jax.ShapeDtypeStruct((R, C), F32),
        compiler_params=_cparams(("parallel",)),
    )(_my_index().astype(jnp.int32).reshape(1), landed, parts)


def _after(params, name, token):
    return {**params, name: params[name] + token[0, 0]}


def _reduce_scatter(tag, parts):
    got = _swap_with_sibling("rs_pair_" + tag, parts)
    pair = _add_own("rs_add_" + tag, parts, got)
    quad = _swap_with_chips("rs_chips_" + tag, pair)
    return _sum_parts("rs_sum_" + tag, quad)


def _adamw(name, g_parts, w, m, v, tr=512):
    P, R, C = g_parts.shape
    tr = _tile(R, tr, 8)

    def body(g_ref, w_ref, m_ref, v_ref, go_ref, d_ref, mo_ref, vo_ref):
        g = g_ref[0]
        for p in range(1, P):
            g = g + g_ref[p]
        mn = ADAM_B1 * m_ref[...] + (1.0 - ADAM_B1) * g
        vn = ADAM_B2 * v_ref[...] + (1.0 - ADAM_B2) * (g * g)
        m_hat = mn / (1.0 - ADAM_B1 ** ADAM_STEP)
        v_hat = vn / (1.0 - ADAM_B2 ** ADAM_STEP)
        go_ref[...] = g
        d_ref[...] = -ADAM_LR * (m_hat / (jnp.sqrt(v_hat) + ADAM_EPS) + ADAM_WD * w_ref[...])
        mo_ref[...] = mn
        vo_ref[...] = vn

    row = pl.BlockSpec((tr, C), lambda i: (i, 0))
    return pl.pallas_call(
        body, name=name, grid=(R // tr,),
        in_specs=[pl.BlockSpec((P, tr, C), lambda i: (0, i, 0)), row, row, row], out_specs=[row] * 4,
        out_shape=[jax.ShapeDtypeStruct((R, C), F32)] * 4,
        compiler_params=_cparams(("parallel",)),
    )(g_parts, w, m, v)


def _pad_rows(t, rows):
    return jnp.pad(t, ((0, rows - t.shape[0]), (0, 0)))


class _Layout:
    def __init__(self, D, ff_shard, in_shard, kv_shard, gate_shard, br_in, br_shard, out_shard):
        self.D = D
        self.in_shard = in_shard
        self.in_pad = -(-in_shard // LANE) * LANE
        self.in_cols = -(-N_DEV * in_shard // IN_TILE) * IN_TILE
        self.br_in, self.br_shard = br_in, br_shard
        br_rows = br_shard * br_in // D
        sizes = [("g1", ff_shard), ("u1", ff_shard), ("d1", ff_shard), ("win", self.in_pad), ("kv", kv_shard),
                 ("gate", gate_shard), ("br", br_rows), ("out", out_shard),
                 ("g2", ff_shard), ("u2", ff_shard), ("d2", ff_shard)]
        self.seg, off = {}, 0
        for key, n in sizes:
            assert n % SUBLANE_BF16 == 0, (key, n)
            self.seg[key] = (off, n)
            off += n
        self.rows = off

    def pack(self, parts):
        return jnp.concatenate([parts[key] for key in self.seg], axis=0)

    def take(self, gathered, key, own=None):
        off, n = self.seg[key]
        seg = gathered[:, off:off + n, :]
        if own is not None:
            seg = lax.dynamic_update_slice(seg, own[0][off:off + n][None], (own[1], 0, 0))
        return seg.reshape(N_DEV * n, self.D)

    def spread(self, full, key):
        _, n = self.seg[key]
        return full.reshape(N_DEV, n, self.D)


def _pack_layer(lay, l, p):
    D = lay.D
    br = jnp.concatenate([p["w_br_sb"][l], p["w_br_fox"][l], p["w_br_mem"][l]], axis=0)
    parts = {
        "g1": p["ffn1_w_gate"][l].T, "u1": p["ffn1_w_up"][l].T, "d1": p["ffn1_w_down"][l],
        "win": _pad_rows(p["w_in"][l].T, lay.in_pad), "kv": p["w_mem_kv"][l], "gate": p["w_gate"][l].T,
        "br": br.T.reshape(-1, D), "out": p["w_out"][l],
        "g2": p["ffn2_w_gate"][l].T, "u2": p["ffn2_w_up"][l].T, "d2": p["ffn2_w_down"][l],
    }
    return lay.pack({k: t.astype(BF16) for k, t in parts.items()})


def _align_win(lay, packed):
    D = lay.D
    real = packed.reshape(N_DEV, lay.in_pad, D)[:, :lay.in_shard].reshape(N_DEV * lay.in_shard, D)
    rows = jnp.concatenate([real[:_QKV_W], real[_QKV_W + N_FOX_HEADS:], real[_QKV_W:_QKV_W + N_FOX_HEADS]], axis=0)
    return _pad_rows(rows, lay.in_cols)


def _unalign_win(lay, aligned):
    D = lay.D
    n_real = N_DEV * lay.in_shard
    mem_w = n_real - _QKV_W - N_FOX_HEADS
    real = jnp.concatenate([aligned[:_QKV_W], aligned[_QKV_W + mem_w:n_real], aligned[_QKV_W:_QKV_W + mem_w]], axis=0)
    real = real.reshape(N_DEV, lay.in_shard, D)
    return jnp.pad(real, ((0, 0), (0, lay.in_pad - lay.in_shard), (0, 0))).reshape(N_DEV * lay.in_pad, D)


def _unpack_layer(lay, gathered, own=None):
    D = lay.D
    w = {k: lay.take(gathered, k, own) for k in ("g1", "u1", "d1", "kv", "out", "g2", "u2", "d2")}
    w["win"] = _align_win(lay, lay.take(gathered, "win", own))
    fl0 = N_DEV * lay.in_shard - N_FOX_HEADS
    w["wfl"] = w["win"][fl0:fl0 + LANE]
    gate = lay.take(gathered, "gate", own)
    w["gate"] = gate
    w["gate3"] = [gate[i * D:(i + 1) * D] for i in range(3)]
    br = lay.take(gathered, "br", own).reshape(N_DEV * lay.br_shard, lay.br_in)
    third = lay.br_in // 3
    w["br3"] = [br[:, i * third:(i + 1) * third] for i in range(3)]
    return w


def _silu_mul(accs, _):
    a, b = accs
    return [a, b, a * jax.nn.sigmoid(a) * b]


def _act_bwd(accs, extras):
    ds, (a, b) = accs[0], extras
    sig = jax.nn.sigmoid(a)
    return [ds * b * (sig * (1.0 + a * (1.0 - sig))), ds * (a * sig)]


def _res_norm(scale):
    def epilogue(accs, extras):
        f, (res, g) = accs[0], extras
        return [f, res + scale * ((f * _rstd(f)) * g)]
    return epilogue


def _norm_bwd(accs, extras):
    dy, (x, res, g) = _sum_accs(accs, None)[0], extras
    r = _rstd(x)
    xhat = x * r
    gy = dy * g
    dx = res + r * (gy - xhat * jnp.mean(gy * xhat, axis=-1, keepdims=True))
    part = jnp.sum(dy * xhat, axis=0, keepdims=True)
    first = lax.broadcasted_iota(jnp.int32, (8, part.shape[1]), 0) == 0
    return [dx, jnp.where(first, part, 0.0)]


def _ffn_fwd(tag, h, pre_g, post_g, wg, wu, wd):
    D = h.shape[1]
    n = _rms_fwd("ffn_norm_" + tag, h, pre_g, BF16)
    a, b, s = _mm("ffn_up_" + tag, [(n, wg), (n, wu)], "nt", [F32, F32, BF16], _silu_mul, tn=1408)
    f, out = _mm("ffn_down_" + tag, [(s, wd)], "nn", [F32, F32], _res_norm(0.5),
                 [(h, 0), (post_g.reshape(1, D), 0)], tn=D)
    return out, (h, n, a, b, s, f)


def _ffn_bwd(tag, dh, saved, pre_g, post_g, wg, wu, wd):
    h, n, a, b, s, f = saved
    D = h.shape[1]
    df, d_post = _rms_bwd("ffn_dout_" + tag, f, post_g, dh, BF16, scale=0.5)
    da, db = _mm("ffn_dact_" + tag, [(df, wd)], "nt", [BF16, BF16], _act_bwd, [(a, 0), (b, 0)], tn=1408)
    d_wd = _mm("ffn_dwd_" + tag, [(s, df)], "tn", [BF16], tm=256)
    dh_in, d_pre_rows = _mm("ffn_dn_" + tag, [(da, wg), (db, wu)], "nn", [F32, F32], _norm_bwd,
                            [(h, 0), (dh, 0), (pre_g.reshape(1, D), 0)], tm=256, tn=D, out_rows=[None, 8])
    d_pre = _colsum("ffn_dpre_" + tag, d_pre_rows)
    d_wg = _mm("ffn_dwg_" + tag, [(da, n)], "tn", [BF16], tm=256)
    d_wu = _mm("ffn_dwu_" + tag, [(db, n)], "tn", [BF16], tm=256)
    return dh_in, d_pre, d_post, d_wg, d_wu, d_wd


_SB_W = N_SB_HEADS * HEAD_DIM
_FOX_W = N_FOX_HEADS * HEAD_DIM
_QKV_W = 3 * _SB_W + 3 * _FOX_W


def _gate_act(accs, extras):
    return [jax.nn.sigmoid(accs[0] + extras[0])]


def _merge(accs, extras):
    return [extras[0] * accs[0] + extras[1] * accs[1] + extras[2] * accs[2]]


def _merge_bwd(accs, extras):
    dm = accs[0]
    d_branch = [dm * gi for gi in extras]
    d_gate = [dm * bi * gi * (1.0 - gi) for bi, gi in zip(accs[1:], extras)]
    return d_branch + d_gate


def _mix_tiles(lay):
    sb, fx = _SB_W // LANE, _FOX_W // LANE
    mem_w = N_DEV * lay.in_shard - _QKV_W - N_FOX_HEADS
    return (0, sb, 2 * sb, sb), (3 * sb, 3 * sb + fx, 3 * sb + 2 * fx, fx), (_QKV_W // LANE, mem_w // LANE)


def _mix_fwd(lay, h, w, pre_g, post_g, b_forget, b_gate, mem_n):
    D = lay.D
    (sq, sk, sv, sn), (fq, fk, fv, fn), (mq, mn) = _mix_tiles(lay)
    mem_d = mn * LANE // N_MEM_HEADS
    u = _rms_fwd("mix_norm", h, pre_g, BF16)
    proj = _mm("mix_in", [(u, w["win"])], "nt", [BF16], tm=1024, tn=IN_TILE)
    fl = _mm("mix_fl", [(u, w["wfl"])], "nt", [F32])[:, :N_FOX_HEADS].T
    c = _decay_fwd(fl, b_forget.reshape(-1, 1))
    o_sb, rtot = _sb_fwd((proj, sq), (proj, sk), (proj, sv), sn, HEAD_DIM, HEAD_DIM ** -0.5)
    o_fx32, o_fx, lse_fx = _attn_fwd("fox_fwd", (proj, fq), (proj, fk), (proj, fv), fn, HEAD_DIM,
                                     HEAD_DIM ** -0.5, c)
    kvm = _mm("mem_kv", [(mem_n, w["kv"])], "nn", [BF16])
    o_mem32, o_mem, lse_mem = _attn_fwd("mem_fwd", (proj, mq), (kvm, 0), (kvm, mn), mn, mem_d, mem_d ** -0.5)
    gates = _mm("mix_gate", [(u, w["gate"])], "nt", [F32], _gate_act, [(b_gate.reshape(1, -1), 0)], tm=1024)
    flat = [o_sb, o_fx, o_mem]
    merged = _mm("mix_merge", list(zip(flat, w["br3"])), "nt", [BF16], _merge,
                 [(gates, 0), (gates, D), (gates, 2 * D)])
    z, out = _mm("mix_out", [(merged, w["out"])], "nn", [F32, F32], _res_norm(1.0),
                 [(h, 0), (post_g.reshape(1, D), 0)], tn=D)
    saved = (h, u, proj, fl, c, rtot, o_fx32, lse_fx, kvm, o_mem32, lse_mem, gates, flat, merged, z)
    return out, saved


def _mix_bwd(lay, dh, saved, w, pre_g, post_g, b_forget, mem_n, dmem_n):
    D = lay.D
    (sq, sk, sv, sn), (fq, fk, fv, fn), (mq, mn) = _mix_tiles(lay)
    mem_d = mn * LANE // N_MEM_HEADS
    h, u, proj, fl, c, rtot, o_fx32, lse_fx, kvm, o_mem32, lse_mem, gates, flat, merged, z = saved
    dz, d_post = _rms_bwd("mix_dres", z, post_g, dh, BF16)
    outs = _mm("mix_dmerge", [(dz, w["out"])] + list(zip(flat, w["br3"])), "nt", [BF16] * 6, _merge_bwd,
               [(gates, 0), (gates, D), (gates, 2 * D)], tn=512)
    d_branch, d_gate = outs[:3], outs[3:]
    d_wout = _mm("mix_dwout", [(merged, dz)], "tn", [BF16])
    d_o = [_mm("mix_dbr%d" % i, [(d_branch[i], w["br3"][i])], "nn", [BF16]) for i in range(3)]
    d_wbr = [_mm("mix_dwbr%d" % i, [(d_branch[i], flat[i])], "tn", [BF16]) for i in range(3)]
    d_bgate = jnp.concatenate([_colsum("mix_dbgate%d" % i, d_gate[i]) for i in range(3)])
    d_wgate = [_mm("mix_dwgate%d" % i, [(d_gate[i], u)], "tn", [BF16]) for i in range(3)]

    d_sb = _sb_bwd((proj, sq), (proj, sk), (proj, sv), d_o[0], rtot, sn, HEAD_DIM, HEAD_DIM ** -0.5)
    *d_fx, dc, dc_rows = _attn_bwd("fox_bwd", (proj, fq), (proj, fk), (proj, fv), o_fx32, d_o[1], lse_fx, fn,
                                   HEAD_DIM, HEAD_DIM ** -0.5, c)
    dq_m, dk_m, dv_m = _attn_bwd("mem_bwd", (proj, mq), (kvm, 0), (kvm, mn), o_mem32, d_o[2], lse_mem, mn,
                                 mem_d, mem_d ** -0.5)
    dfl, d_bforget = _decay_bwd(dc, dc_rows, fl, b_forget.reshape(-1, 1))
    pieces = list(d_sb) + list(d_fx) + [dq_m]
    dflp = jnp.pad(dfl.T.astype(BF16), ((0, 0), (0, LANE - dfl.shape[0])))
    offs = [sum(t.shape[1] for t in pieces[:i]) for i in range(len(pieces) + 1)]
    win_rows = [w["win"][offs[i]:offs[i + 1]] for i in range(len(pieces))]
    du = _mm("mix_du", list(zip(d_gate, w["gate3"])) + list(zip(pieces, win_rows)) + [(dflp, w["wfl"])], "nn",
             [F32], _sum_accs, tm=256, tn=512)
    d_rows = [_mm("mix_dwin%d" % i, [(t, u)], "tn", [BF16]) for i, t in enumerate(pieces)]
    d_wfl = _mm("mix_dwfl", [(dflp, u)], "tn", [BF16])
    d_win = _unalign_win(lay, _pad_rows(jnp.concatenate(list(d_rows) + [d_wfl], axis=0), lay.in_cols))
    dh_in, d_pre = _rms_bwd("mix_dnorm", h, pre_g, du, F32, res=dh)

    dkvm = jnp.concatenate([dk_m, dv_m], axis=1)
    d_wkv = _mm("mem_dwkv", [(mem_n, dkvm)], "tn", [BF16])
    dmem_n = _mm("mem_dn", [(dkvm, w["kv"])], "nt", [F32], lambda accs, ex: [accs[0] + ex[0]], [(dmem_n, 0)])
    grads = {"win": d_win, "kv": d_wkv, "gate": jnp.concatenate(d_wgate, axis=0),
             "br": jnp.concatenate(d_wbr, axis=1), "out": d_wout}
    return dh_in, d_pre, d_post, d_bforget, d_bgate, grads, dmem_n


def _layer_fwd(lay, h, w, sp, mem_n):
    h1, s1 = _ffn_fwd("1", h, sp["ffn1_pre_g"], sp["ffn1_post_g"], w["g1"], w["u1"], w["d1"])
    h2, s2 = _mix_fwd(lay, h1, w, sp["mix_pre_g"], sp["mix_post_g"], sp["b_forget"], sp["b_gate"], mem_n)
    h3, s3 = _ffn_fwd("2", h2, sp["ffn2_pre_g"], sp["ffn2_post_g"], w["g2"], w["u2"], w["d2"])
    return h3, (s1, s2, s3)


def _layer_bwd(lay, dh, saved, w, sp, mem_n, dmem_n):
    s1, s2, s3 = saved
    dh, d_pre2, d_post2, d_g2, d_u2, d_d2 = _ffn_bwd("2", dh, s3, sp["ffn2_pre_g"], sp["ffn2_post_g"],
                                                     w["g2"], w["u2"], w["d2"])
    dh, d_mpre, d_mpost, d_bforget, d_bgate, g, dmem_n = _mix_bwd(
        lay, dh, s2, w, sp["mix_pre_g"], sp["mix_post_g"], sp["b_forget"], mem_n, dmem_n)
    dh, d_pre1, d_post1, d_g1, d_u1, d_d1 = _ffn_bwd("1", dh, s1, sp["ffn1_pre_g"], sp["ffn1_post_g"],
                                                     w["g1"], w["u1"], w["d1"])
    g.update({"g1": d_g1, "u1": d_u1, "d1": d_d1, "g2": d_g2, "u2": d_u2, "d2": d_d2})
    g["br"] = g["br"].reshape(N_DEV, lay.br_shard, lay.br_in).reshape(-1, lay.D)
    packed = jnp.concatenate([lay.spread(g[key], key) for key in lay.seg], axis=1)
    small = {"ffn1_pre_g": d_pre1, "ffn1_post_g": d_post1, "mix_pre_g": d_mpre, "mix_post_g": d_mpost,
             "ffn2_pre_g": d_pre2, "ffn2_post_g": d_post2, "b_gate": d_bgate, "b_forget": d_bforget}
    return dh, packed, small, dmem_n


_SHARDED = ["ffn1_w_gate", "ffn1_w_up", "ffn1_w_down", "w_in", "w_mem_kv", "w_gate", "w_br_sb", "w_br_fox",
            "w_br_mem", "w_out", "ffn2_w_gate", "ffn2_w_up", "ffn2_w_down"]
_SMALL_LAYER = ["ffn1_pre_g", "ffn1_post_g", "mix_pre_g", "mix_post_g", "ffn2_pre_g", "ffn2_post_g", "b_gate",
                "b_forget"]
_WEIGHTS = ["ffn1_pre_g", "ffn1_post_g", "ffn1_w_gate", "ffn1_w_up", "ffn1_w_down", "mix_pre_g", "mix_post_g",
            "w_in", "b_forget", "mem_norm_g", "w_mem_kv", "w_gate", "b_gate", "w_br_sb", "w_br_fox", "w_br_mem",
            "w_out", "ffn2_pre_g", "ffn2_post_g", "ffn2_w_gate", "ffn2_w_up", "ffn2_w_down"]


def _pack_small(vals, L, D):
    rows = []
    for l in range(L):
        for name in _SMALL_LAYER:
            t = vals[name][l]
            rows.append(jnp.pad(t, (0, -t.shape[0] % D)).reshape(-1, D))
    rows.append(vals["mem_norm_g"].reshape(1, D))
    packed = jnp.concatenate(rows, axis=0)
    return _pad_rows(packed, -(-packed.shape[0] // 8) * 8)


def _unpack_small(packed, shapes, L, D):
    out = {name: [] for name in _SMALL_LAYER}
    r = 0
    for l in range(L):
        for name in _SMALL_LAYER:
            n = shapes[name][1]
            nr = -(-n // D)
            out[name].append(packed[r:r + nr].reshape(-1)[:n])
            r += nr
    res = {name: jnp.stack(v) for name, v in out.items()}
    res["mem_norm_g"] = packed[r]
    return res


def _unpack_grads(lay, g, l_shapes):
    def seg(key):
        off, n = lay.seg[key]
        return g[off:off + n]
    br = seg("br").reshape(lay.br_shard, lay.br_in).T
    third = lay.br_in // 3
    return {
        "ffn1_w_gate": seg("g1").T, "ffn1_w_up": seg("u1").T, "ffn1_w_down": seg("d1"),
        "w_in": seg("win")[:lay.in_shard].T, "w_mem_kv": seg("kv"), "w_gate": seg("gate").T,
        "w_br_sb": br[:third], "w_br_fox": br[third:2 * third], "w_br_mem": br[2 * third:],
        "w_out": seg("out"), "ffn2_w_gate": seg("g2").T, "ffn2_w_up": seg("u2").T, "ffn2_w_down": seg("d2"),
    }


class _Exchanges:
    def gather(self, name, block):
        return _all_gather(name, block)

    def gather_start(self, block):
        return _exchange_start("ag_start", block, per_peer=False)

    def gather_wait(self, started, after):
        block, landed = _exchange_wait("ag_wait", started, after)
        return landed, (block, _my_index())

    def scatter(self, parts):
        return _reduce_scatter("w", parts)

    def scatter_start(self, parts):
        return _exchange_start("rs_start", parts, per_peer=True)

    def scatter_wait(self, started, after):
        parts, landed = _exchange_wait("rs_wait", started, after)
        return _sum_landed("rs_sum8", landed, parts)

    def token(self, started):
        return started[4]

    def loss_sum(self, part):
        return lax.psum(part, ("x", "y", "c"))


def _step(p, m, v, x, mem, tgt, ex):
    L, D = p["ffn1_pre_g"].shape
    lay = _Layout(D, p["ffn1_w_gate"].shape[2], p["w_in"].shape[2], p["w_mem_kv"].shape[1], p["w_gate"].shape[2],
                  3 * p["w_br_sb"].shape[1], p["w_br_sb"].shape[2], p["w_out"].shape[1])
    blocks = [_pack_layer(lay, l, p) for l in range(L)]
    sps = [{name: p[name][l] for name in _SMALL_LAYER} for l in range(L)]

    mem_n = _rms_fwd("mem_norm", mem, p["mem_norm_g"], BF16)
    gathered, own = ex.gather("ag_weights", blocks[0]), None
    h, saved, ws = x, [], []
    for l in range(L):
        if l + 1 < L:
            nxt, gathered = lax.optimization_barrier((blocks[l + 1], gathered))
            started = ex.gather_start(nxt)
            sp = _after(sps[l], "ffn1_pre_g", ex.token(started))
        else:
            sp = sps[l]
        ws.append(_unpack_layer(lay, gathered, own))
        h, s = _layer_fwd(lay, h, ws[l], sp, mem_n)
        saved.append(s)
        if l + 1 < L:
            gathered, own = ex.gather_wait(started, h)
    loss_part, dh = _loss_grad(h, tgt)
    loss = ex.loss_sum(loss_part)

    dmem_n = jnp.zeros(mem.shape, F32)
    big, small = [None] * L, {name: [None] * L for name in _SMALL_LAYER}
    flying, token = {}, None
    for l in reversed(range(L)):
        sp = sps[l] if token is None else _after(sps[l], "ffn2_post_g", token)
        dh, packed, sm, dmem_n = _layer_bwd(lay, dh, saved[l], ws[l], sp, mem_n, dmem_n)
        if l > 0:
            flying[l] = ex.scatter_start(packed)
            token = ex.token(flying[l])
        else:
            big[l] = _unpack_grads(lay, ex.scatter(packed), None)
        for name in _SMALL_LAYER:
            small[name][l] = sm[name]
    for l, started in flying.items():
        big[l] = _unpack_grads(lay, ex.scatter_wait(started, dh), None)
    _, d_memg = _rms_bwd("mem_dnorm", mem, p["mem_norm_g"], dmem_n, F32)

    small_g = {name: jnp.stack(vs) for name, vs in small.items()}
    small_g["mem_norm_g"] = d_memg
    small_names = _SMALL_LAYER + ["mem_norm_g"]
    shapes = {name: p[name].shape for name in small_names}
    g_all = ex.gather("ag_small", _pack_small(small_g, L, D))
    packs = [_pack_small({name: t[name] for name in small_names}, L, D) for t in (p, m, v)]
    res = [_unpack_small(t, shapes, L, D) for t in _adamw("adamw_small", g_all, *packs)]

    out = {kind: {} for kind in ("grad", "delta", "new_m", "new_v")}
    for name in small_names:
        for kind, r in zip(("grad", "delta", "new_m", "new_v"), res):
            out[kind][name] = r[name].reshape(p[name].shape)
    for name in _SHARDED:
        g = jnp.stack([big[l][name] for l in range(L)])
        shp = g.shape
        flat = lambda t: t.reshape(-1, shp[-1])
        r = _adamw("adamw_" + name, flat(g)[None], flat(p[name]), flat(m[name]), flat(v[name]))
        for kind, t in zip(("grad", "delta", "new_m", "new_v"), r):
            out[kind][name] = t.reshape(shp)
    return loss, dh, out


def kernel(x, mem, ffn1_pre_g, ffn1_post_g, ffn1_w_gate, ffn1_w_up, ffn1_w_down, mix_pre_g, mix_post_g, w_in, b_forget, mem_norm_g, w_mem_kv, w_gate, b_gate, w_br_sb, w_br_fox, w_br_mem, w_out, ffn2_pre_g, ffn2_post_g, ffn2_w_gate, ffn2_w_up, ffn2_w_down, loss_target, m_ffn1_pre_g, m_ffn1_post_g, m_ffn1_w_gate, m_ffn1_w_up, m_ffn1_w_down, m_mix_pre_g, m_mix_post_g, m_w_in, m_b_forget, m_mem_norm_g, m_w_mem_kv, m_w_gate, m_b_gate, m_w_br_sb, m_w_br_fox, m_w_br_mem, m_w_out, m_ffn2_pre_g, m_ffn2_post_g, m_ffn2_w_gate, m_ffn2_w_up, m_ffn2_w_down, v_ffn1_pre_g, v_ffn1_post_g, v_ffn1_w_gate, v_ffn1_w_up, v_ffn1_w_down, v_mix_pre_g, v_mix_post_g, v_w_in, v_b_forget, v_mem_norm_g, v_w_mem_kv, v_w_gate, v_b_gate, v_w_br_sb, v_w_br_fox, v_w_br_mem, v_w_out, v_ffn2_pre_g, v_ffn2_post_g, v_ffn2_w_gate, v_ffn2_w_up, v_ffn2_w_down):
    p = dict(zip(_WEIGHTS, (ffn1_pre_g, ffn1_post_g, ffn1_w_gate, ffn1_w_up, ffn1_w_down, mix_pre_g, mix_post_g, w_in, b_forget, mem_norm_g, w_mem_kv, w_gate, b_gate, w_br_sb, w_br_fox, w_br_mem, w_out, ffn2_pre_g, ffn2_post_g, ffn2_w_gate, ffn2_w_up, ffn2_w_down)))
    m = dict(zip(_WEIGHTS, (m_ffn1_pre_g, m_ffn1_post_g, m_ffn1_w_gate, m_ffn1_w_up, m_ffn1_w_down, m_mix_pre_g, m_mix_post_g, m_w_in, m_b_forget, m_mem_norm_g, m_w_mem_kv, m_w_gate, m_b_gate, m_w_br_sb, m_w_br_fox, m_w_br_mem, m_w_out, m_ffn2_pre_g, m_ffn2_post_g, m_ffn2_w_gate, m_ffn2_w_up, m_ffn2_w_down)))
    v = dict(zip(_WEIGHTS, (v_ffn1_pre_g, v_ffn1_post_g, v_ffn1_w_gate, v_ffn1_w_up, v_ffn1_w_down, v_mix_pre_g, v_mix_post_g, v_w_in, v_b_forget, v_mem_norm_g, v_w_mem_kv, v_w_gate, v_b_gate, v_w_br_sb, v_w_br_fox, v_w_br_mem, v_w_out, v_ffn2_pre_g, v_ffn2_post_g, v_ffn2_w_gate, v_ffn2_w_up, v_ffn2_w_down)))
    loss, dx, out = _step(p, m, v, x[0], mem[0], loss_target[0], _Exchanges())
    return (loss, dx[None], *[out["grad"][n] for n in _WEIGHTS], *[out["delta"][n] for n in _WEIGHTS],
            *[out["new_m"][n] for n in _WEIGHTS], *[out["new_v"][n] for n in _WEIGHTS])
```

```python
import functools
import math

import jax
import jax.numpy as jnp
from jax import lax
from jax.experimental import pallas as pl
from jax.experimental.pallas import tpu as pltpu

F32 = jnp.float32
BF16 = jnp.bfloat16

LANE = 128
SUBLANE_BF16 = 16
VMEM_LIMIT = 56 * 1024 * 1024
N_DEV = 8
MESH = pl.DeviceIdType.MESH
ANY = pl.BlockSpec(memory_space=pl.ANY)

RMS_EPS = 1e-6
HEAD_DIM = 64
N_SB_HEADS = 8
N_FOX_HEADS = 8
N_MEM_HEADS = 4
NEG = -1e30
ATT_TQ = 1024
ATT_TK = 256
DECAY_TK = 128
IN_TILE = 1280

ADAM_LR = 0.001
ADAM_B1 = 0.9
ADAM_B2 = 0.999
ADAM_EPS = 1e-08
ADAM_WD = 0.01
ADAM_STEP = 10


def _tile(n, target, mult=LANE):
    best = None
    for t in range(mult, min(n, target) + 1, mult):
        if n % t == 0:
            best = t
    return best if best is not None else n


def _cparams(sem):
    return pltpu.CompilerParams(dimension_semantics=sem, vmem_limit_bytes=VMEM_LIMIT)


_DIMS = {"nn": (((1,), (0,)), ((), ())), "nt": (((1,), (1,)), ((), ())), "tn": (((0,), (0,)), ((), ()))}


def _dot(a, b, mode="nn"):
    return lax.dot_general(a.astype(BF16), b.astype(BF16), _DIMS[mode], preferred_element_type=F32)


def _mm(name, pairs, mode, out_dtypes, epilogue=None, extras=(), tm=512, tn=1024, out_rows=None):
    a0, b0 = pairs[0]
    M = a0.shape[1] if mode == "tn" else a0.shape[0]
    N = b0.shape[0] if mode == "nt" else b0.shape[1]
    tm = _tile(M, tm)
    tn = _tile(N, tn)
    np_, ne, no = len(pairs), len(extras), len(out_dtypes)

    def body(*refs):
        a_refs, b_refs = refs[:np_], refs[np_:2 * np_]
        e_refs = refs[2 * np_:2 * np_ + ne]
        o_refs = refs[2 * np_ + ne:]
        accs = [_dot(a[...], b[...], mode) for a, b in zip(a_refs, b_refs)]
        outs = epilogue(accs, [e[...] for e in e_refs]) if epilogue is not None else accs
        for o, val in zip(o_refs, outs):
            o[...] = val.astype(o.dtype)

    in_specs = []
    for a, _ in pairs:
        if mode == "tn":
            in_specs.append(pl.BlockSpec((a.shape[0], tm), lambda j, i: (0, i)))
        else:
            in_specs.append(pl.BlockSpec((tm, a.shape[1]), lambda j, i: (i, 0)))
    for _, b in pairs:
        if mode == "nt":
            in_specs.append(pl.BlockSpec((tn, b.shape[1]), lambda j, i: (j, 0)))
        else:
            in_specs.append(pl.BlockSpec((b.shape[0], tn), lambda j, i: (0, j)))
    for e, off in extras:
        if e.shape[0] == 1:
            in_specs.append(pl.BlockSpec((1, tn), functools.partial(lambda j, i, o: (0, j + o), o=off // tn)))
        else:
            in_specs.append(pl.BlockSpec((tm, tn), functools.partial(lambda j, i, o: (i, j + o), o=off // tn)))
    rows = [tm if r is None else r for r in (out_rows or [None] * no)]
    out_specs = [pl.BlockSpec((r, tn), lambda j, i: (i, j)) for r in rows]
    outs = pl.pallas_call(
        body, name=name, grid=(N // tn, M // tm),
        in_specs=in_specs, out_specs=out_specs,
        out_shape=[jax.ShapeDtypeStruct((M // tm * r, N), dt) for r, dt in zip(rows, out_dtypes)],
        compiler_params=_cparams(("parallel", "parallel")),
    )(*[a for a, _ in pairs], *[b for _, b in pairs], *[e for e, _ in extras])
    return outs[0] if no == 1 else outs


def _sum_accs(accs, _):
    total = accs[0]
    for acc in accs[1:]:
        total = total + acc
    return [total]


def _rstd(x):
    return lax.rsqrt(jnp.mean(x * x, axis=-1, keepdims=True) + RMS_EPS)


def _rms_fwd(name, x, g, out_dtype, res=None, scale=1.0, tr=512):
    R, D = x.shape
    tr = _tile(R, tr, 8)
    has_res = res is not None

    def body(*refs):
        x_ref, g_ref = refs[:2]
        o_ref = refs[-1]
        xv = x_ref[...]
        y = (xv * _rstd(xv)) * g_ref[...]
        if has_res:
            y = refs[2][...] + scale * y
        o_ref[...] = y.astype(o_ref.dtype)

    row = pl.BlockSpec((tr, D), lambda i: (i, 0))
    gain = pl.BlockSpec((1, D), lambda i: (0, 0))
    return pl.pallas_call(
        body, name=name, grid=(R // tr,),
        in_specs=[row, gain] + ([row] if has_res else []), out_specs=row,
        out_shape=jax.ShapeDtypeStruct((R, D), out_dtype),
        compiler_params=_cparams(("parallel",)),
    )(x, g.reshape(1, D), *([res] if has_res else []))


def _rms_bwd(name, x, g, dy, out_dtype, scale=1.0, res=None, tr=512):
    R, D = x.shape
    tr = _tile(R, tr, 8)
    has_res = res is not None

    def body(*refs):
        x_ref, g_ref, dy_ref = refs[:3]
        dx_ref, dg_ref = refs[-2:]
        i = pl.program_id(0)
        xv = x_ref[...]
        xhat = xv * _rstd(xv)
        dyv = dy_ref[...].astype(F32) * scale
        gy = dyv * g_ref[...]
        dx = _rstd(xv) * (gy - xhat * jnp.mean(gy * xhat, axis=-1, keepdims=True))
        if has_res:
            dx = refs[3][...] + dx
        dx_ref[...] = dx.astype(dx_ref.dtype)
        part = jnp.sum(dyv * xhat, axis=0, keepdims=True)

        @pl.when(i == 0)
        def _():
            dg_ref[...] = part

        @pl.when(i > 0)
        def _():
            dg_ref[...] += part

    row = pl.BlockSpec((tr, D), lambda i: (i, 0))
    gain = pl.BlockSpec((1, D), lambda i: (0, 0))
    dx, dg = pl.pallas_call(
        body, name=name, grid=(R // tr,),
        in_specs=[row, gain, row] + ([row] if has_res else []), out_specs=[row, gain],
        out_shape=[jax.ShapeDtypeStruct((R, D), out_dtype), jax.ShapeDtypeStruct((1, D), F32)],
        compiler_params=_cparams(("arbitrary",)),
    )(x, g.reshape(1, D), dy, *([res] if has_res else []))
    return dx, dg[0]


def _loss_grad(y, tgt, tr=512):
    R, D = y.shape
    tr = _tile(R, tr, 8)

    def body(y_ref, t_ref, dy_ref, loss_ref):
        i = pl.program_id(0)
        d = y_ref[...] - t_ref[...]
        dy_ref[...] = d / D
        part = 0.5 * jnp.sum(jnp.mean(d * d, axis=-1, keepdims=True), axis=0, keepdims=True)
        tile = jnp.broadcast_to(part, loss_ref.shape)

        @pl.when(i == 0)
        def _():
            loss_ref[...] = tile

        @pl.when(i > 0)
        def _():
            loss_ref[...] += tile

    row = pl.BlockSpec((tr, D), lambda i: (i, 0))
    dy, loss = pl.pallas_call(
        body, name="loss_grad", grid=(R // tr,),
        in_specs=[row, row], out_specs=[row, pl.BlockSpec((8, LANE), lambda i: (0, 0))],
        out_shape=[jax.ShapeDtypeStruct((R, D), F32), jax.ShapeDtypeStruct((8, LANE), F32)],
        compiler_params=_cparams(("arbitrary",)),
    )(y, tgt)
    return loss[0, 0], dy


def _colsum(name, x, tr=512, tn=1024):
    R, N = x.shape
    tr, tn = _tile(R, tr, 8), _tile(N, tn)

    def body(x_ref, o_ref):
        i = pl.program_id(1)
        part = jnp.sum(x_ref[...].astype(F32), axis=0, keepdims=True)

        @pl.when(i == 0)
        def _():
            o_ref[...] = part

        @pl.when(i > 0)
        def _():
            o_ref[...] += part

    out = pl.pallas_call(
        body, name=name, grid=(N // tn, R // tr),
        in_specs=[pl.BlockSpec((tr, tn), lambda j, i: (i, j))], out_specs=pl.BlockSpec((1, tn), lambda j, i: (0, j)),
        out_shape=jax.ShapeDtypeStruct((1, N), F32),
        compiler_params=_cparams(("parallel", "arbitrary")),
    )(x)
    return out[0]


def _tri(tk, rel):
    j = lax.broadcasted_iota(jnp.int32, (tk, tk), 0)
    s = lax.broadcasted_iota(jnp.int32, (tk, tk), 1)
    return rel(j, s).astype(BF16)


def _dot_split(x, m, parts=2):
    total = None
    rem = x
    for _ in range(parts):
        piece = rem.astype(BF16)
        rem = rem - piece.astype(F32)
        term = jnp.dot(piece, m, preferred_element_type=F32)
        total = term if total is None else total + term
    return total


def _log_not_and_beta(z, mask):
    ln = -(jnp.maximum(z, 0.0) + jnp.log(1.0 + jnp.exp(-jnp.abs(z))))
    return (ln if mask is None else jnp.where(mask, ln, 0.0)), ln + z


def _att_tiles(T, Tk, causal):
    tq = min(ATT_TQ, T)
    tk = min(ATT_TK, tq if causal else Tk)
    return tq, tk, (tq if causal else Tk) // tk


def _key_base(j, tq):
    return j * tq if isinstance(j, int) else pl.multiple_of(j * tq, tq)


def _is_pow2(scale):
    return math.log2(scale).is_integer()


def _per_head(x, hpb, d):
    if hpb == 1:
        return [x]
    lane = lax.broadcasted_iota(jnp.int32, x.shape, 1)
    return [jnp.where((lane >= h * d) & (lane < (h + 1) * d), x, jnp.zeros_like(x)) for h in range(hpb)]


def _join_heads(xs, d):
    out = xs[-1]
    if len(xs) > 1:
        lane = lax.broadcasted_iota(jnp.int32, out.shape, 1)
        for h in reversed(range(len(xs) - 1)):
            out = jnp.where(lane < (h + 1) * d, xs[h], out)
    return out


def _lane_tile(rows, off, whole):
    if whole:
        return pl.BlockSpec((rows, LANE), lambda g, i: (0, off + g))
    return pl.BlockSpec((rows, LANE), lambda g, i: (i, off + g))


def _sb_fwd(q, k, v, n_tiles, d, scale):
    T = q[0].shape[0]
    hpb = LANE // d
    tq, tk, nsub = _att_tiles(T, T, True)
    assert _is_pow2(scale)

    def body(q_ref, k_ref, v_ref, ob_ref, rt_ref, acc_ref, r_ref):
        qi = pl.program_id(1)
        qh = _per_head(q_ref[...] * scale, hpb, d)
        acc_ref[...] = jnp.zeros_like(acc_ref)
        r_ref[...] = jnp.zeros_like(r_ref)
        row = lax.broadcasted_iota(jnp.int32, (tq, tk), 0)
        col = lax.broadcasted_iota(jnp.int32, (tq, tk), 1)
        after = _tri(tk, lambda j, s: j > s)

        def step(j, diagonal):
            base = _key_base(j, tq)
            for h in range(hpb):
                parts = []
                for u in reversed(range(nsub)):
                    r0 = u * tk if diagonal else 0
                    z = _dot(qh[h][r0:], k_ref[pl.ds(base + u * tk, tk), :], "nt")
                    mask = (col[r0:] + u * tk) < row[r0:] if diagonal else None
                    ln, lb = _log_not_and_beta(z, mask)
                    between = _dot_split(ln, after, parts=1)
                    first = ln[:, 0:1].astype(BF16).astype(F32)
                    parts.append((u, r0, lb, between, between[:, 0:1] + first, mask))
                out = None
                for u, r0, lb, between, total, mask in parts:
                    w = jnp.exp(lb + between + r_ref[h, r0:, :])
                    if diagonal:
                        w = jnp.where(mask, w, 0.0)
                    term = _dot(w, v_ref[pl.ds(base + u * tk, tk), :])
                    r_ref[h, r0:, :] += total
                    if diagonal:
                        acc_ref[h, r0:, :] += term
                    else:
                        out = term if out is None else out + term
                if not diagonal:
                    acc_ref[h] += out

        def below(i, carry):
            step(qi - 1 - i, False)
            return carry

        step(qi, True)
        lax.fori_loop(0, qi, below, 0)
        ob_ref[...] = _join_heads([acc_ref[h] for h in range(hpb)], d).astype(ob_ref.dtype)
        rt_ref[...] = r_ref[...]

    out = pl.BlockSpec((tq, LANE), lambda g, i: (i, g))
    col = pl.BlockSpec((hpb, tq, 1), lambda g, i: (g, i, 0))
    return pl.pallas_call(
        body, name="sb_fwd", grid=(n_tiles, T // tq),
        in_specs=[_lane_tile(tq, q[1], False), _lane_tile(T, k[1], True), _lane_tile(T, v[1], True)],
        out_specs=[out, col],
        out_shape=[jax.ShapeDtypeStruct((T, n_tiles * LANE), BF16), jax.ShapeDtypeStruct((n_tiles * hpb, T, 1), F32)],
        scratch_shapes=[pltpu.VMEM((hpb, tq, LANE), F32), pltpu.VMEM((hpb, tq, 1), F32)],
        compiler_params=_cparams(("parallel", "arbitrary")),
    )(q[0], k[0], v[0])


def _sb_bwd(q, k, v, do, rtot, n_tiles, d, scale):
    T = q[0].shape[0]
    hpb = LANE // d
    tq, tk, nsub = _att_tiles(T, T, True)
    assert _is_pow2(scale)

    def body(q_ref, k_ref, v_ref, do_ref, rt_ref, dq_ref, dk_ref, dv_ref, dk_acc, dv_acc, dq_acc, p_ref, c_ref):
        qi = pl.program_id(1)

        @pl.when(qi == 0)
        def _():
            dk_acc[...] = jnp.zeros_like(dk_acc)
            dv_acc[...] = jnp.zeros_like(dv_acc)

        qh = _per_head(q_ref[...] * scale, hpb, d)
        doh = _per_head(do_ref[...], hpb, d)
        dq_acc[...] = jnp.zeros_like(dq_acc)
        p_ref[...] = jnp.zeros_like(p_ref)
        c_ref[...] = jnp.zeros_like(c_ref)
        row = lax.broadcasted_iota(jnp.int32, (tq, tk), 0)
        col = lax.broadcasted_iota(jnp.int32, (tq, tk), 1)
        upto = _tri(tk, lambda j, s: j <= s)
        before = _tri(tk, lambda j, s: j < s)
        rt_wide = [jnp.broadcast_to(rt_ref[h], (tq, tk)) for h in range(hpb)]

        def step(j, diagonal):
            base = _key_base(j, tq)
            for h in range(hpb):
                first = []
                for u in range(nsub):
                    ks = base + u * tk
                    r0 = u * tk if diagonal else 0
                    kv = k_ref[pl.ds(ks, tk), :]
                    z = _dot(qh[h][r0:], kv, "nt")
                    mask = (col[r0:] + u * tk) < row[r0:] if diagonal else None
                    ln, lb = _log_not_and_beta(z, mask)
                    dw = _dot(doh[h][r0:], v_ref[pl.ds(ks, tk), :], "nt")
                    first.append((r0, ks, kv, mask, lb, jnp.exp(lb), _dot_split(ln, upto, parts=1), dw))
                rt, pre, cpre = rt_wide[h], p_ref[h], c_ref[h]
                dq = None
                for r0, ks, kv, mask, lb, sig, local, dw in first:
                    if diagonal and r0:
                        pre, cpre = pre[tk:], cpre[tk:]
                    prefix = local + pre
                    w = jnp.exp(lb + (rt[r0:] - prefix))
                    if diagonal:
                        w = jnp.where(mask, w, 0.0)
                    g = dw * w
                    c = _dot_split(g, before, parts=1) + cpre
                    dz = g * (1.0 - sig) - c * sig
                    if diagonal:
                        dz = jnp.where(mask, dz, 0.0)
                    term = _dot(dz, kv)
                    if diagonal:
                        dq_acc[h, r0:, :] += term
                    else:
                        dq = term if dq is None else dq + term
                    dk_acc[pl.ds(ks, tk), :] += _dot(dz, qh[h][r0:], "tn")
                    dv_acc[pl.ds(ks, tk), :] += _dot(w, doh[h][r0:], "tn")
                    pre = prefix[:, tk - 1:tk]
                    cpre = c[:, tk - 1:tk] + g[:, tk - 1:tk]
                if not diagonal:
                    dq_acc[h] += dq
                    p_ref[h] = pre
                    c_ref[h] = cpre

        def below(j, carry):
            step(j, False)
            return carry

        lax.fori_loop(0, qi, below, 0)
        step(qi, True)
        dq_ref[...] = (_join_heads([dq_acc[h] for h in range(hpb)], d) * scale).astype(dq_ref.dtype)

        @pl.when(qi == pl.num_programs(1) - 1)
        def _():
            dk_ref[...] = dk_acc[...].astype(dk_ref.dtype)
            dv_ref[...] = dv_acc[...].astype(dv_ref.dtype)

    blk = pl.BlockSpec((tq, LANE), lambda g, i: (i, g))
    full = pl.BlockSpec((T, LANE), lambda g, i: (0, g))
    col = pl.BlockSpec((hpb, tq, 1), lambda g, i: (g, i, 0))
    wide = jax.ShapeDtypeStruct((T, n_tiles * LANE), BF16)
    return pl.pallas_call(
        body, name="sb_bwd", grid=(n_tiles, T // tq),
        in_specs=[_lane_tile(tq, q[1], False), _lane_tile(T, k[1], True), _lane_tile(T, v[1], True), blk, col],
        out_specs=[blk, full, full], out_shape=[wide, wide, wide],
        scratch_shapes=[pltpu.VMEM((T, LANE), F32), pltpu.VMEM((T, LANE), F32), pltpu.VMEM((hpb, tq, LANE), F32),
                        pltpu.VMEM((hpb, tq, 1), F32), pltpu.VMEM((hpb, tq, 1), F32)],
        compiler_params=_cparams(("parallel", "arbitrary")),
    )(q[0], k[0], v[0], do, rtot)


def _attn_fwd(name, q, k, v, n_tiles, d, scale, c=None):
    T, Tk = q[0].shape[0], k[0].shape[0]
    hpb = LANE // d
    H = n_tiles * hpb
    causal = c is not None
    tq, tk, nsub = _att_tiles(T, Tk, causal)
    fold = _is_pow2(scale)

    def body(*refs):
        q_ref, k_ref, v_ref = refs[:3]
        cc_ref, cr_ref = refs[3:5] if causal else (None, None)
        o_ref, ob_ref, lse_ref, m_ref, l_ref, acc_ref = refs[-6:]
        qi = pl.program_id(1)
        qh = _per_head(q_ref[...] * scale if fold else q_ref[...], hpb, d)
        bias = [jnp.broadcast_to(cc_ref[h], (tq, tk)) for h in range(hpb)] if causal else None
        ones = jnp.ones((tk, LANE), BF16)
        m_ref[...] = jnp.full_like(m_ref, NEG)
        l_ref[...] = jnp.zeros_like(l_ref)
        acc_ref[...] = jnp.zeros_like(acc_ref)
        row = lax.broadcasted_iota(jnp.int32, (tq, tk), 0)
        col = lax.broadcasted_iota(jnp.int32, (tq, tk), 1)

        def logits(h, j, base, u, r0, diagonal):
            z = _dot(qh[h][r0:], k_ref[pl.ds(base + u * tk, tk), :], "nt")
            if not fold:
                z = z * scale
            if causal:
                z = z + bias[h][r0:] - cr_ref[h, j * nsub + u]
            if diagonal:
                z = jnp.where((col[r0:] + u * tk) <= row[r0:], z, NEG)
            return z

        def absorb(h, r0, zs, vs):
            m_prev = m_ref[h, r0:, :]
            top = zs[0]
            for z in zs[1:]:
                top = jnp.maximum(top, z)
            m_new = jnp.maximum(m_prev, jnp.max(top, axis=1, keepdims=True))
            alpha = jnp.exp(m_prev - m_new)
            l_new = alpha * l_ref[h, r0:, :]
            out = alpha * acc_ref[h, r0:, :]
            m_wide = jnp.broadcast_to(m_new, top.shape)
            for z, vv in zip(zs, vs):
                p = jnp.exp(z - m_wide).astype(BF16)
                l_new = l_new + jnp.dot(p, ones, preferred_element_type=F32)[:, 0:1]
                out = out + _dot(p, vv)
            l_ref[h, r0:, :] = l_new
            acc_ref[h, r0:, :] = out
            m_ref[h, r0:, :] = m_new

        def step(j, diagonal):
            base = _key_base(j, tq)
            vs = [v_ref[pl.ds(base + u * tk, tk), :] for u in range(nsub)]
            for h in range(hpb):
                if diagonal:
                    for u in range(nsub):
                        absorb(h, u * tk, [logits(h, j, base, u, u * tk, True)], [vs[u]])
                else:
                    absorb(h, 0, [logits(h, j, base, u, 0, False) for u in range(nsub)], vs)

        def below(j, carry):
            step(j, False)
            return carry

        if causal:
            lax.fori_loop(0, qi, below, 0)
            step(qi, True)
        else:
            step(0, False)
        o = _join_heads([acc_ref[h] / l_ref[h] for h in range(hpb)], d)
        o_ref[...] = o
        ob_ref[...] = o.astype(ob_ref.dtype)
        lse_ref[...] = m_ref[...] + jnp.log(l_ref[...])

    out = pl.BlockSpec((tq, LANE), lambda g, i: (i, g))
    col = pl.BlockSpec((hpb, tq, 1), lambda g, i: (g, i, 0))
    in_specs = [_lane_tile(tq, q[1], False), _lane_tile(Tk, k[1], True), _lane_tile(Tk, v[1], True)]
    args = [q[0], k[0], v[0]]
    if causal:
        in_specs += [col, pl.BlockSpec((hpb, T // tk, 1, tk), lambda g, i: (g, 0, 0, 0))]
        args += [c.reshape(H, T, 1), c.reshape(H, T // tk, 1, tk)]
    return pl.pallas_call(
        body, name=name, grid=(n_tiles, T // tq),
        in_specs=in_specs, out_specs=[out, out, col],
        out_shape=[jax.ShapeDtypeStruct((T, n_tiles * LANE), F32), jax.ShapeDtypeStruct((T, n_tiles * LANE), BF16),
                   jax.ShapeDtypeStruct((H, T, 1), F32)],
        scratch_shapes=[pltpu.VMEM((hpb, tq, 1), F32), pltpu.VMEM((hpb, tq, 1), F32),
                        pltpu.VMEM((hpb, tq, LANE), F32)],
        compiler_params=_cparams(("parallel", "arbitrary")),
    )(*args)


def _attn_bwd(name, q, k, v, o, do, lse, n_tiles, d, scale, c=None):
    T, Tk = q[0].shape[0], k[0].shape[0]
    hpb = LANE // d
    H = n_tiles * hpb
    causal = c is not None
    tq, tk, nsub = _att_tiles(T, Tk, causal)
    fold = _is_pow2(scale)

    def body(*refs):
        q_ref, k_ref, v_ref, o_ref, do_ref, lse_ref = refs[:6]
        cc_ref, cr_ref = refs[6:8] if causal else (None, None)
        n_out = 5 if causal else 3
        outs = refs[-(n_out + 3):-3]
        dq_ref, dk_ref, dv_ref = outs[:3]
        dc_ref, drow_ref = outs[3:5] if causal else (None, None)
        dk_acc, dv_acc, dq_acc = refs[-3:]
        qi = pl.program_id(1)

        @pl.when(qi == 0)
        def _():
            dk_acc[...] = jnp.zeros_like(dk_acc)
            dv_acc[...] = jnp.zeros_like(dv_acc)
            if causal:
                dc_ref[...] = jnp.zeros_like(dc_ref)

        qh = _per_head(q_ref[...] * scale if fold else q_ref[...], hpb, d)
        doh = _per_head(do_ref[...], hpb, d)
        delta_wide = [jnp.broadcast_to(jnp.sum(t.astype(F32) * o_ref[...], axis=1, keepdims=True), (tq, tk))
                      for t in doh]
        shift = [jnp.broadcast_to((cc_ref[h] - lse_ref[h]) if causal else -lse_ref[h], (tq, tk)) for h in range(hpb)]
        dq_acc[...] = jnp.zeros_like(dq_acc)
        if causal:
            drow_ref[...] = jnp.zeros_like(drow_ref)
        row = lax.broadcasted_iota(jnp.int32, (tq, tk), 0)
        col = lax.broadcasted_iota(jnp.int32, (tq, tk), 1)

        def step(j, diagonal):
            base = _key_base(j, tq)
            for h in range(hpb):
                dq, dsum = None, None
                for u in range(nsub):
                    ks = base + u * tk
                    r0 = u * tk if diagonal else 0
                    kv = k_ref[pl.ds(ks, tk), :]
                    z = _dot(qh[h][r0:], kv, "nt")
                    if not fold:
                        z = z * scale
                    z = z + shift[h][r0:]
                    if causal:
                        z = z - cr_ref[h, j * nsub + u]
                    if diagonal:
                        z = jnp.where((col[r0:] + u * tk) <= row[r0:], z, NEG)
                    p = jnp.exp(z)
                    ds = p * (_dot(doh[h][r0:], v_ref[pl.ds(ks, tk), :], "nt") - delta_wide[h][r0:])
                    term = _dot(ds, kv)
                    dk = _dot(ds, qh[h][r0:], "tn")
                    dk_acc[pl.ds(ks, tk), :] += dk if fold else dk * scale
                    dv_acc[pl.ds(ks, tk), :] += _dot(p, doh[h][r0:], "tn")
                    if causal:
                        dc_ref[h, j * nsub + u] -= jnp.sum(ds, axis=0, keepdims=True)
                    if diagonal:
                        dq_acc[h, r0:, :] += term
                        drow_ref[h, r0:, :] += jnp.sum(ds, axis=1, keepdims=True)
                    else:
                        dq = term if dq is None else dq + term
                        if causal:
                            dsum = ds if dsum is None else dsum + ds
                if not diagonal:
                    dq_acc[h] += dq
                    if causal:
                        drow_ref[h] += jnp.sum(dsum, axis=1, keepdims=True)

        def below(j, carry):
            step(j, False)
            return carry

        if causal:
            lax.fori_loop(0, qi, below, 0)
            step(qi, True)
        else:
            step(0, False)
        dq_ref[...] = (_join_heads([dq_acc[h] for h in range(hpb)], d) * scale).astype(dq_ref.dtype)

        @pl.when(qi == pl.num_programs(1) - 1)
        def _():
            dk_ref[...] = dk_acc[...].astype(dk_ref.dtype)
            dv_ref[...] = dv_acc[...].astype(dv_ref.dtype)

    blk = pl.BlockSpec((tq, LANE), lambda g, i: (i, g))
    full = pl.BlockSpec((Tk, LANE), lambda g, i: (0, g))
    col = pl.BlockSpec((hpb, tq, 1), lambda g, i: (g, i, 0))
    crow = pl.BlockSpec((hpb, T // tk, 1, tk), lambda g, i: (g, 0, 0, 0))
    in_specs = [_lane_tile(tq, q[1], False), _lane_tile(Tk, k[1], True), _lane_tile(Tk, v[1], True), blk, blk, col]
    args = [q[0], k[0], v[0], o, do, lse]
    out_specs = [blk, full, full]
    out_shape = [jax.ShapeDtypeStruct((T, n_tiles * LANE), BF16), jax.ShapeDtypeStruct((Tk, n_tiles * LANE), BF16),
                 jax.ShapeDtypeStruct((Tk, n_tiles * LANE), BF16)]
    if causal:
        in_specs += [col, crow]
        args += [c.reshape(H, T, 1), c.reshape(H, T // tk, 1, tk)]
        out_specs += [crow, col]
        out_shape += [jax.ShapeDtypeStruct((H, T // tk, 1, tk), F32), jax.ShapeDtypeStruct((H, T, 1), F32)]
    outs = pl.pallas_call(
        body, name=name, grid=(n_tiles, T // tq),
        in_specs=in_specs, out_specs=out_specs, out_shape=out_shape,
        scratch_shapes=[pltpu.VMEM((Tk, LANE), F32), pltpu.VMEM((Tk, LANE), F32), pltpu.VMEM((hpb, tq, LANE), F32)],
        compiler_params=_cparams(("parallel", "arbitrary")),
    )(*args)
    if causal:
        return outs[0], outs[1], outs[2], outs[3].reshape(H, T), outs[4].reshape(H, T)
    return outs


def _decay_fwd(fl, b):
    H, T = fl.shape
    tk = DECAY_TK

    def body(x_ref, b_ref, c_ref):
        upto = _tri(tk, lambda j, s: j <= s)
        carry = jnp.zeros((H, 1), F32)
        for i in range(T // tk):
            xv = x_ref[:, i * tk:(i + 1) * tk] + b_ref[...]
            lf = jnp.minimum(xv, 0.0) - jnp.log(1.0 + jnp.exp(-jnp.abs(xv)))
            pref = _dot_split(lf, upto, parts=3) + carry
            c_ref[:, i * tk:(i + 1) * tk] = pref
            carry = pref[:, tk - 1:tk]

    vm = pl.BlockSpec(memory_space=pltpu.VMEM)
    return pl.pallas_call(
        body, name="decay_fwd", in_specs=[vm, vm], out_specs=vm,
        out_shape=jax.ShapeDtypeStruct((H, T), F32),
    )(fl, b)


def _decay_bwd(dc_cols, dc_rows, fl, b):
    H, T = fl.shape
    tk = DECAY_TK

    def body(dc_ref, dr_ref, x_ref, b_ref, dx_ref, db_ref):
        from_ = _tri(tk, lambda j, s: j >= s)
        carry = jnp.zeros((H, 1), F32)
        total = jnp.zeros((H, 1), F32)
        for i in reversed(range(T // tk)):
            sl = slice(i * tk, (i + 1) * tk)
            suffix = _dot_split(dc_ref[:, sl] + dr_ref[:, sl], from_, parts=3) + carry
            xv = x_ref[:, sl] + b_ref[...]
            dx = suffix / (1.0 + jnp.exp(xv))
            dx_ref[:, sl] = dx
            total = total + jnp.sum(dx, axis=1, keepdims=True)
            carry = suffix[:, 0:1]
        db_ref[...] = jnp.broadcast_to(total, db_ref.shape)

    vm = pl.BlockSpec(memory_space=pltpu.VMEM)
    dx, db = pl.pallas_call(
        body, name="decay_bwd", in_specs=[vm, vm, vm, vm], out_specs=[vm, vm],
        out_shape=[jax.ShapeDtypeStruct((H, T), F32), jax.ShapeDtypeStruct((H, LANE), F32)],
    )(dc_cols, dc_rows, fl, b)
    return dx, db[:, 0]


def _place():
    x, y, c = lax.axis_index("x"), lax.axis_index("y"), lax.axis_index("c")
    return x, y, c, [(1 - x, y), (x, 1 - y), (1 - x, 1 - y)]


def _all_gather(name, block):
    R, C = block.shape

    def body(x_ref, out_ref, send_sems, recv_sems, local_sem):
        x, y, c, chips = _place()
        me, sibling = (x, y, c), (x, y, 1 - c)

        def rows(px, py, pc):
            return out_ref.at[4 * px + 2 * py + pc]

        def copy(k, blk, to, src=None):
            return pltpu.make_async_remote_copy(
                src_ref=rows(*blk) if src is None else src, dst_ref=rows(*blk),
                send_sem=send_sems.at[k], recv_sem=recv_sems.at[k], device_id=to, device_id_type=MESH)

        mine = pltpu.make_async_copy(x_ref, rows(*me), local_sem)
        mine.start()
        first = [copy(0, me, sibling, src=x_ref)]
        first += [copy(1 + j, me, (*chip, c), src=x_ref) for j, chip in enumerate(chips)]
        for cp in first:
            cp.start()
        passed = [copy(4 + j, (*chip, c), sibling) for j, chip in enumerate(chips)]
        for j, chip in enumerate(chips):
            copy(1 + j, (*chip, c), me).wait_recv()
            passed[j].start()
        copy(0, sibling, me).wait_recv()
        for j, chip in enumerate(chips):
            copy(4 + j, (*chip, 1 - c), me).wait_recv()
        for cp in first + passed:
            cp.wait_send()
        mine.wait()

    return pl.pallas_call(
        body, name=name, in_specs=[ANY], out_specs=ANY,
        out_shape=jax.ShapeDtypeStruct((N_DEV, R, C), block.dtype),
        scratch_shapes=[pltpu.SemaphoreType.DMA((7,)), pltpu.SemaphoreType.DMA((7,)), pltpu.SemaphoreType.DMA(())],
    )(block)


def _swap_with_sibling(name, parts):
    _, R, C = parts.shape

    def body(p_ref, out_ref, send_sems, recv_sems):
        x, y, c, _ = _place()
        copies = [pltpu.make_async_remote_copy(
            src_ref=p_ref.at[2 * q + (1 - c)], dst_ref=out_ref.at[q],
            send_sem=send_sems.at[q], recv_sem=recv_sems.at[q], device_id=(x, y, 1 - c), device_id_type=MESH)
            for q in range(4)]
        for cp in copies:
            cp.start()
        for cp in copies:
            cp.wait_recv()
        for cp in copies:
            cp.wait_send()

    return pl.pallas_call(
        body, name=name, in_specs=[ANY], out_specs=ANY,
        out_shape=jax.ShapeDtypeStruct((4, R, C), parts.dtype),
        scratch_shapes=[pltpu.SemaphoreType.DMA((4,)), pltpu.SemaphoreType.DMA((4,))],
    )(parts)


def _add_own(name, parts, got, tr=512):
    _, R, C = parts.shape
    tr = _tile(R, tr, SUBLANE_BF16)

    def body(c_ref, p_ref, g_ref, o_ref):
        o_ref[...] = (p_ref[...].astype(F32) + g_ref[...].astype(F32)).astype(o_ref.dtype)

    return pl.pallas_call(
        body, name=name,
        grid_spec=pltpu.PrefetchScalarGridSpec(
            num_scalar_prefetch=1, grid=(4, R // tr),
            in_specs=[pl.BlockSpec((1, tr, C), lambda q, i, c: (2 * q + c[0], i, 0)),
                      pl.BlockSpec((1, tr, C), lambda q, i, c: (q, i, 0))],
            out_specs=pl.BlockSpec((1, tr, C), lambda q, i, c: (q, i, 0))),
        out_shape=jax.ShapeDtypeStruct((4, R, C), parts.dtype),
        compiler_params=_cparams(("parallel", "parallel")),
    )(lax.axis_index("c").astype(jnp.int32).reshape(1), parts, got)


def _swap_with_chips(name, parts):
    _, R, C = parts.shape

    def body(p_ref, out_ref, send_sems, recv_sems, local_sem):
        x, y, c, chips = _place()
        my_chip = 2 * x + y
        mine = pltpu.make_async_copy(p_ref.at[my_chip], out_ref.at[my_chip], local_sem)
        mine.start()
        sends = [pltpu.make_async_remote_copy(
            src_ref=p_ref.at[2 * cx + cy], dst_ref=out_ref.at[my_chip],
            send_sem=send_sems.at[j], recv_sem=recv_sems.at[j], device_id=(cx, cy, c), device_id_type=MESH)
            for j, (cx, cy) in enumerate(chips)]
        for cp in sends:
            cp.start()
        for j, (cx, cy) in enumerate(chips):
            pltpu.make_async_remote_copy(
                src_ref=p_ref.at[my_chip], dst_ref=out_ref.at[2 * cx + cy],
                send_sem=send_sems.at[j], recv_sem=recv_sems.at[j], device_id=(cx, cy, c), device_id_type=MESH,
            ).wait_recv()
        for cp in sends:
            cp.wait_send()
        mine.wait()

    return pl.pallas_call(
        body, name=name, in_specs=[ANY], out_specs=ANY,
        out_shape=jax.ShapeDtypeStruct((4, R, C), parts.dtype),
        scratch_shapes=[pltpu.SemaphoreType.DMA((3,)), pltpu.SemaphoreType.DMA((3,)), pltpu.SemaphoreType.DMA(())],
    )(parts)


def _sum_parts(name, parts, tr=512):
    P, R, C = parts.shape
    tr = _tile(R, tr, SUBLANE_BF16)

    def body(p_ref, o_ref):
        total = p_ref[0].astype(F32)
        for p in range(1, P):
            total = total + p_ref[p].astype(F32)
        o_ref[...] = total

    return pl.pallas_call(
        body, name=name, grid=(R // tr,),
        in_specs=[pl.BlockSpec((P, tr, C), lambda i: (0, i, 0))], out_specs=pl.BlockSpec((tr, C), lambda i: (i, 0)),
        out_shape=jax.ShapeDtypeStruct((R, C), F32),
        compiler_params=_cparams(("parallel",)),
    )(parts)


_HBM = pl.BlockSpec(memory_space=pltpu.HBM)
_SEM = pl.BlockSpec(memory_space=pltpu.SEMAPHORE)
_EFFECT = pltpu.SideEffectType.DATAFLOW_SIDE_EFFECTING


def _flipped(x, y, c, k):
    px, py, pc = (1 - x if k & 4 else x), (1 - y if k & 2 else y), (1 - c if k & 1 else c)
    return (px, py, pc), 4 * px + 2 * py + pc


def _exchange_start(name, src, per_peer):
    R, C = src.shape[-2:]

    def body(v_ref, land_ref, send_sem, recv_sem, v_thru, land_thru, token):
        x, y, c = lax.axis_index("x"), lax.axis_index("y"), lax.axis_index("c")
        me = 4 * x + 2 * y + c
        for k in range(1, N_DEV):
            peer, idx = _flipped(x, y, c, k)
            pltpu.make_async_remote_copy(
                src_ref=v_ref.at[idx] if per_peer else v_ref, dst_ref=land_ref.at[me],
                send_sem=send_sem, recv_sem=recv_sem, device_id=peer, device_id_type=MESH).start()
        token[...] = jnp.zeros_like(token)

    return pl.pallas_call(
        body, name=name,
        out_shape=(pltpu.SemaphoreType.DMA(()), pltpu.SemaphoreType.DMA(()), pltpu.HBM(src.shape, src.dtype),
                   pltpu.HBM((N_DEV, R, C), src.dtype), jax.ShapeDtypeStruct((8, LANE), F32)),
        in_specs=(_HBM, _HBM), out_specs=(_SEM, _SEM, _HBM, _HBM, pl.BlockSpec(memory_space=pltpu.VMEM)),
        input_output_aliases={0: 2, 1: 3},
        compiler_params=pltpu.CompilerParams(has_side_effects=_EFFECT),
    )(pltpu.with_memory_space_constraint(src, pltpu.HBM),
      pltpu.with_memory_space_constraint(lax.empty((N_DEV, R, C), src.dtype), pltpu.HBM))


def _exchange_wait(name, started, after):
    send_sem, recv_sem, v_thru, land_thru, _ = started

    def body(v_ref, land_ref, send_sem, recv_sem, after_ref, v_dead, got_ref):
        x, y, c = lax.axis_index("x"), lax.axis_index("y"), lax.axis_index("c")
        seven = land_ref.at[pl.ds(0, N_DEV - 1)]
        drain = pltpu.make_async_remote_copy(
            src_ref=seven, dst_ref=seven, send_sem=send_sem, recv_sem=recv_sem,
            device_id=(x, y, c), device_id_type=MESH)
        drain.wait_send()
        drain.wait_recv()

    return pl.pallas_call(
        body, name=name,
        out_shape=(pltpu.HBM(v_thru.shape, v_thru.dtype), pltpu.HBM(land_thru.shape, land_thru.dtype)),
        in_specs=(_HBM, _HBM, _SEM, _SEM, ANY), out_specs=(_HBM, _HBM), input_output_aliases={0: 0, 1: 1},
        compiler_params=pltpu.CompilerParams(has_side_effects=_EFFECT),
    )(v_thru, land_thru, send_sem, recv_sem, after)


def _my_index():
    return 4 * lax.axis_index("x") + 2 * lax.axis_index("y") + lax.axis_index("c")


def _sum_landed(name, landed, parts, tr=512):
    P, R, C = landed.shape
    tr = _tile(R, tr, SUBLANE_BF16)

    def body(me_ref, l_ref, own_ref, o_ref):
        total = None
        for s in range(P):
            part = jnp.where(me_ref[0] == s, own_ref[0], l_ref[s]).astype(F32)
            total = part if total is None else total + part
        o_ref[...] = total

    return pl.pallas_call(
        body, name=name,
        grid_spec=pltpu.PrefetchScalarGridSpec(
            num_scalar_prefetch=1, grid=(R // tr,),
            in_specs=[pl.BlockSpec((P, tr, C), lambda i, me: (0, i, 0)),
                      pl.BlockSpec((1, tr, C), lambda i, me: (me[0], i, 0))],
            out_specs=pl.BlockSpec((tr, C), lambda i, me: (i, 0))),
        out_shape=jax.ShapeDtypeStruct((R, C), F32),
        compiler_params=_cparams(("parallel",)),
    )(_my_index().astype(jnp.int32).reshape(1), landed, parts)


def _after(params, name, token):
    return {**params, name: params[name] + token[0, 0]}


def _reduce_scatter(tag, parts):
    got = _swap_with_sibling("rs_pair_" + tag, parts)
    pair = _add_own("rs_add_" + tag, parts, got)
    quad = _swap_with_chips("rs_chips_" + tag, pair)
    return _sum_parts("rs_sum_" + tag, quad)


def _adamw(name, g_parts, w, m, v, tr=512):
    P, R, C = g_parts.shape
    tr = _tile(R, tr, 8)

    def body(g_ref, w_ref, m_ref, v_ref, go_ref, d_ref, mo_ref, vo_ref):
        g = g_ref[0]
        for p in range(1, P):
            g = g + g_ref[p]
        mn = ADAM_B1 * m_ref[...] + (1.0 - ADAM_B1) * g
        vn = ADAM_B2 * v_ref[...] + (1.0 - ADAM_B2) * (g * g)
        m_hat = mn / (1.0 - ADAM_B1 ** ADAM_STEP)
        v_hat = vn / (1.0 - ADAM_B2 ** ADAM_STEP)
        go_ref[...] = g
        d_ref[...] = -ADAM_LR * (m_hat / (jnp.sqrt(v_hat) + ADAM_EPS) + ADAM_WD * w_ref[...])
        mo_ref[...] = mn
        vo_ref[...] = vn

    row = pl.BlockSpec((tr, C), lambda i: (i, 0))
    return pl.pallas_call(
        body, name=name, grid=(R // tr,),
        in_specs=[pl.BlockSpec((P, tr, C), lambda i: (0, i, 0)), row, row, row], out_specs=[row] * 4,
        out_shape=[jax.ShapeDtypeStruct((R, C), F32)] * 4,
        compiler_params=_cparams(("parallel",)),
    )(g_parts, w, m, v)


def _pad_rows(t, rows):
    return jnp.pad(t, ((0, rows - t.shape[0]), (0, 0)))


class _Layout:
    def __init__(self, D, ff_shard, in_shard, kv_shard, gate_shard, br_in, br_shard, out_shard):
        self.D = D
        self.in_shard = in_shard
        self.in_pad = -(-in_shard // LANE) * LANE
        self.in_cols = -(-N_DEV * in_shard // IN_TILE) * IN_TILE
        self.br_in, self.br_shard = br_in, br_shard
        br_rows = br_shard * br_in // D
        sizes = [("g1", ff_shard), ("u1", ff_shard), ("d1", ff_shard), ("win", self.in_pad), ("kv", kv_shard),
                 ("gate", gate_shard), ("br", br_rows), ("out", out_shard),
                 ("g2", ff_shard), ("u2", ff_shard), ("d2", ff_shard)]
        self.seg, off = {}, 0
        for key, n in sizes:
            assert n % SUBLANE_BF16 == 0, (key, n)
            self.seg[key] = (off, n)
            off += n
        self.rows = off

    def pack(self, parts):
        return jnp.concatenate([parts[key] for key in self.seg], axis=0)

    def take(self, gathered, key, own=None):
        off, n = self.seg[key]
        seg = gathered[:, off:off + n, :]
        if own is not None:
            seg = lax.dynamic_update_slice(seg, own[0][off:off + n][None], (own[1], 0, 0))
        return seg.reshape(N_DEV * n, self.D)

    def spread(self, full, key):
        _, n = self.seg[key]
        return full.reshape(N_DEV, n, self.D)


def _pack_layer(lay, l, p):
    D = lay.D
    br = jnp.concatenate([p["w_br_sb"][l], p["w_br_fox"][l], p["w_br_mem"][l]], axis=0)
    parts = {
        "g1": p["ffn1_w_gate"][l].T, "u1": p["ffn1_w_up"][l].T, "d1": p["ffn1_w_down"][l],
        "win": _pad_rows(p["w_in"][l].T, lay.in_pad), "kv": p["w_mem_kv"][l], "gate": p["w_gate"][l].T,
        "br": br.T.reshape(-1, D), "out": p["w_out"][l],
        "g2": p["ffn2_w_gate"][l].T, "u2": p["ffn2_w_up"][l].T, "d2": p["ffn2_w_down"][l],
    }
    return lay.pack({k: t.astype(BF16) for k, t in parts.items()})


def _align_win(lay, packed):
    D = lay.D
    real = packed.reshape(N_DEV, lay.in_pad, D)[:, :lay.in_shard].reshape(N_DEV * lay.in_shard, D)
    rows = jnp.concatenate([real[:_QKV_W], real[_QKV_W + N_FOX_HEADS:], real[_QKV_W:_QKV_W + N_FOX_HEADS]], axis=0)
    return _pad_rows(rows, lay.in_cols)


def _unalign_win(lay, aligned):
    D = lay.D
    n_real = N_DEV * lay.in_shard
    mem_w = n_real - _QKV_W - N_FOX_HEADS
    real = jnp.concatenate([aligned[:_QKV_W], aligned[_QKV_W + mem_w:n_real], aligned[_QKV_W:_QKV_W + mem_w]], axis=0)
    real = real.reshape(N_DEV, lay.in_shard, D)
    return jnp.pad(real, ((0, 0), (0, lay.in_pad - lay.in_shard), (0, 0))).reshape(N_DEV * lay.in_pad, D)


def _unpack_layer(lay, gathered, own=None):
    D = lay.D
    w = {k: lay.take(gathered, k, own) for k in ("g1", "u1", "d1", "kv", "out", "g2", "u2", "d2")}
    w["win"] = _align_win(lay, lay.take(gathered, "win", own))
    fl0 = N_DEV * lay.in_shard - N_FOX_HEADS
    w["wfl"] = w["win"][fl0:fl0 + LANE]
    gate = lay.take(gathered, "gate", own)
    w["gate"] = gate
    w["gate3"] = [gate[i * D:(i + 1) * D] for i in range(3)]
    br = lay.take(gathered, "br", own).reshape(N_DEV * lay.br_shard, lay.br_in)
    third = lay.br_in // 3
    w["br3"] = [br[:, i * third:(i + 1) * third] for i in range(3)]
    return w


def _silu_mul(accs, _):
    a, b = accs
    return [a, b, a * jax.nn.sigmoid(a) * b]


def _act_bwd(accs, extras):
    ds, (a, b) = accs[0], extras
    sig = jax.nn.sigmoid(a)
    return [ds * b * (sig * (1.0 + a * (1.0 - sig))), ds * (a * sig)]


def _res_norm(scale):
    def epilogue(accs, extras):
        f, (res, g) = accs[0], extras
        return [f, res + scale * ((f * _rstd(f)) * g)]
    return epilogue


def _norm_bwd(accs, extras):
    dy, (x, res, g) = _sum_accs(accs, None)[0], extras
    r = _rstd(x)
    xhat = x * r
    gy = dy * g
    dx = res + r * (gy - xhat * jnp.mean(gy * xhat, axis=-1, keepdims=True))
    part = jnp.sum(dy * xhat, axis=0, keepdims=True)
    first = lax.broadcasted_iota(jnp.int32, (8, part.shape[1]), 0) == 0
    return [dx, jnp.where(first, part, 0.0)]


def _ffn_fwd(tag, h, pre_g, post_g, wg, wu, wd):
    D = h.shape[1]
    n = _rms_fwd("ffn_norm_" + tag, h, pre_g, BF16)
    a, b, s = _mm("ffn_up_" + tag, [(n, wg), (n, wu)], "nt", [F32, F32, BF16], _silu_mul, tn=1408)
    f, out = _mm("ffn_down_" + tag, [(s, wd)], "nn", [F32, F32], _res_norm(0.5),
                 [(h, 0), (post_g.reshape(1, D), 0)], tn=D)
    return out, (h, n, a, b, s, f)


def _ffn_bwd(tag, dh, saved, pre_g, post_g, wg, wu, wd):
    h, n, a, b, s, f = saved
    D = h.shape[1]
    df, d_post = _rms_bwd("ffn_dout_" + tag, f, post_g, dh, BF16, scale=0.5)
    da, db = _mm("ffn_dact_" + tag, [(df, wd)], "nt", [BF16, BF16], _act_bwd, [(a, 0), (b, 0)], tn=1408)
    d_wd = _mm("ffn_dwd_" + tag, [(s, df)], "tn", [BF16], tm=256)
    dh_in, d_pre_rows = _mm("ffn_dn_" + tag, [(da, wg), (db, wu)], "nn", [F32, F32], _norm_bwd,
                            [(h, 0), (dh, 0), (pre_g.reshape(1, D), 0)], tm=256, tn=D, out_rows=[None, 8])
    d_pre = _colsum("ffn_dpre_" + tag, d_pre_rows)
    d_wg = _mm("ffn_dwg_" + tag, [(da, n)], "tn", [BF16], tm=256)
    d_wu = _mm("ffn_dwu_" + tag, [(db, n)], "tn", [BF16], tm=256)
    return dh_in, d_pre, d_post, d_wg, d_wu, d_wd


_SB_W = N_SB_HEADS * HEAD_DIM
_FOX_W = N_FOX_HEADS * HEAD_DIM
_QKV_W = 3 * _SB_W + 3 * _FOX_W


def _gate_act(accs, extras):
    return [jax.nn.sigmoid(accs[0] + extras[0])]


def _merge(accs, extras):
    return [extras[0] * accs[0] + extras[1] * accs[1] + extras[2] * accs[2]]


def _merge_bwd(accs, extras):
    dm = accs[0]
    d_branch = [dm * gi for gi in extras]
    d_gate = [dm * bi * gi * (1.0 - gi) for bi, gi in zip(accs[1:], extras)]
    return d_branch + d_gate


def _mix_tiles(lay):
    sb, fx = _SB_W // LANE, _FOX_W // LANE
    mem_w = N_DEV * lay.in_shard - _QKV_W - N_FOX_HEADS
    return (0, sb, 2 * sb, sb), (3 * sb, 3 * sb + fx, 3 * sb + 2 * fx, fx), (_QKV_W // LANE, mem_w // LANE)


def _mix_fwd(lay, h, w, pre_g, post_g, b_forget, b_gate, mem_n):
    D = lay.D
    (sq, sk, sv, sn), (fq, fk, fv, fn), (mq, mn) = _mix_tiles(lay)
    mem_d = mn * LANE // N_MEM_HEADS
    u = _rms_fwd("mix_norm", h, pre_g, BF16)
    proj = _mm("mix_in", [(u, w["win"])], "nt", [BF16], tm=1024, tn=IN_TILE)
    fl = _mm("mix_fl", [(u, w["wfl"])], "nt", [F32])[:, :N_FOX_HEADS].T
    c = _decay_fwd(fl, b_forget.reshape(-1, 1))
    o_sb, rtot = _sb_fwd((proj, sq), (proj, sk), (proj, sv), sn, HEAD_DIM, HEAD_DIM ** -0.5)
    o_fx32, o_fx, lse_fx = _attn_fwd("fox_fwd", (proj, fq), (proj, fk), (proj, fv), fn, HEAD_DIM,
                                     HEAD_DIM ** -0.5, c)
    kvm = _mm("mem_kv", [(mem_n, w["kv"])], "nn", [BF16])
    o_mem32, o_mem, lse_mem = _attn_fwd("mem_fwd", (proj, mq), (kvm, 0), (kvm, mn), mn, mem_d, mem_d ** -0.5)
    gates = _mm("mix_gate", [(u, w["gate"])], "nt", [F32], _gate_act, [(b_gate.reshape(1, -1), 0)], tm=1024)
    flat = [o_sb, o_fx, o_mem]
    merged = _mm("mix_merge", list(zip(flat, w["br3"])), "nt", [BF16], _merge,
                 [(gates, 0), (gates, D), (gates, 2 * D)])
    z, out = _mm("mix_out", [(merged, w["out"])], "nn", [F32, F32], _res_norm(1.0),
                 [(h, 0), (post_g.reshape(1, D), 0)], tn=D)
    saved = (h, u, proj, fl, c, rtot, o_fx32, lse_fx, kvm, o_mem32, lse_mem, gates, flat, merged, z)
    return out, saved


def _mix_bwd(lay, dh, saved, w, pre_g, post_g, b_forget, mem_n, dmem_n):
    D = lay.D
    (sq, sk, sv, sn), (fq, fk, fv, fn), (mq, mn) = _mix_tiles(lay)
    mem_d = mn * LANE // N_MEM_HEADS
    h, u, proj, fl, c, rtot, o_fx32, lse_fx, kvm, o_mem32, lse_mem, gates, flat, merged, z = saved
    dz, d_post = _rms_bwd("mix_dres", z, post_g, dh, BF16)
    outs = _mm("mix_dmerge", [(dz, w["out"])] + list(zip(flat, w["br3"])), "nt", [BF16] * 6, _merge_bwd,
               [(gates, 0), (gates, D), (gates, 2 * D)], tn=512)
    d_branch, d_gate = outs[:3], outs[3:]
    d_wout = _mm("mix_dwout", [(merged, dz)], "tn", [BF16])
    d_o = [_mm("mix_dbr%d" % i, [(d_branch[i], w["br3"][i])], "nn", [BF16]) for i in range(3)]
    d_wbr = [_mm("mix_dwbr%d" % i, [(d_branch[i], flat[i])], "tn", [BF16]) for i in range(3)]
    d_bgate = jnp.concatenate([_colsum("mix_dbgate%d" % i, d_gate[i]) for i in range(3)])
    d_wgate = [_mm("mix_dwgate%d" % i, [(d_gate[i], u)], "tn", [BF16]) for i in range(3)]

    d_sb = _sb_bwd((proj, sq), (proj, sk), (proj, sv), d_o[0], rtot, sn, HEAD_DIM, HEAD_DIM ** -0.5)
    *d_fx, dc, dc_rows = _attn_bwd("fox_bwd", (proj, fq), (proj, fk), (proj, fv), o_fx32, d_o[1], lse_fx, fn,
                                   HEAD_DIM, HEAD_DIM ** -0.5, c)
    dq_m, dk_m, dv_m = _attn_bwd("mem_bwd", (proj, mq), (kvm, 0), (kvm, mn), o_mem32, d_o[2], lse_mem, mn,
                                 mem_d, mem_d ** -0.5)
    dfl, d_bforget = _decay_bwd(dc, dc_rows, fl, b_forget.reshape(-1, 1))
    pieces = list(d_sb) + list(d_fx) + [dq_m]
    dflp = jnp.pad(dfl.T.astype(BF16), ((0, 0), (0, LANE - dfl.shape[0])))
    offs = [sum(t.shape[1] for t in pieces[:i]) for i in range(len(pieces) + 1)]
    win_rows = [w["win"][offs[i]:offs[i + 1]] for i in range(len(pieces))]
    du = _mm("mix_du", list(zip(d_gate, w["gate3"])) + list(zip(pieces, win_rows)) + [(dflp, w["wfl"])], "nn",
             [F32], _sum_accs, tm=256, tn=512)
    d_rows = [_mm("mix_dwin%d" % i, [(t, u)], "tn", [BF16]) for i, t in enumerate(pieces)]
    d_wfl = _mm("mix_dwfl", [(dflp, u)], "tn", [BF16])
    d_win = _unalign_win(lay, _pad_rows(jnp.concatenate(list(d_rows) + [d_wfl], axis=0), lay.in_cols))
    dh_in, d_pre = _rms_bwd("mix_dnorm", h, pre_g, du, F32, res=dh)

    dkvm = jnp.concatenate([dk_m, dv_m], axis=1)
    d_wkv = _mm("mem_dwkv", [(mem_n, dkvm)], "tn", [BF16])
    dmem_n = _mm("mem_dn", [(dkvm, w["kv"])], "nt", [F32], lambda accs, ex: [accs[0] + ex[0]], [(dmem_n, 0)])
    grads = {"win": d_win, "kv": d_wkv, "gate": jnp.concatenate(d_wgate, axis=0),
             "br": jnp.concatenate(d_wbr, axis=1), "out": d_wout}
    return dh_in, d_pre, d_post, d_bforget, d_bgate, grads, dmem_n


def _layer_fwd(lay, h, w, sp, mem_n):
    h1, s1 = _ffn_fwd("1", h, sp["ffn1_pre_g"], sp["ffn1_post_g"], w["g1"], w["u1"], w["d1"])
    h2, s2 = _mix_fwd(lay, h1, w, sp["mix_pre_g"], sp["mix_post_g"], sp["b_forget"], sp["b_gate"], mem_n)
    h3, s3 = _ffn_fwd("2", h2, sp["ffn2_pre_g"], sp["ffn2_post_g"], w["g2"], w["u2"], w["d2"])
    return h3, (s1, s2, s3)


def _layer_bwd(lay, dh, saved, w, sp, mem_n, dmem_n):
    s1, s2, s3 = saved
    dh, d_pre2, d_post2, d_g2, d_u2, d_d2 = _ffn_bwd("2", dh, s3, sp["ffn2_pre_g"], sp["ffn2_post_g"],
                                                     w["g2"], w["u2"], w["d2"])
    dh, d_mpre, d_mpost, d_bforget, d_bgate, g, dmem_n = _mix_bwd(
        lay, dh, s2, w, sp["mix_pre_g"], sp["mix_post_g"], sp["b_forget"], mem_n, dmem_n)
    dh, d_pre1, d_post1, d_g1, d_u1, d_d1 = _ffn_bwd("1", dh, s1, sp["ffn1_pre_g"], sp["ffn1_post_g"],
                                                     w["g1"], w["u1"], w["d1"])
    g.update({"g1": d_g1, "u1": d_u1, "d1": d_d1, "g2": d_g2, "u2": d_u2, "d2": d_d2})
    g["br"] = g["br"].reshape(N_DEV, lay.br_shard, lay.br_in).reshape(-1, lay.D)
    packed = jnp.concatenate([lay.spread(g[key], key) for key in lay.seg], axis=1)
    small = {"ffn1_pre_g": d_pre1, "ffn1_post_g": d_post1, "mix_pre_g": d_mpre, "mix_post_g": d_mpost,
             "ffn2_pre_g": d_pre2, "ffn2_post_g": d_post2, "b_gate": d_bgate, "b_forget": d_bforget}
    return dh, packed, small, dmem_n


_SHARDED = ["ffn1_w_gate", "ffn1_w_up", "ffn1_w_down", "w_in", "w_mem_kv", "w_gate", "w_br_sb", "w_br_fox",
            "w_br_mem", "w_out", "ffn2_w_gate", "ffn2_w_up", "ffn2_w_down"]
_SMALL_LAYER = ["ffn1_pre_g", "ffn1_post_g", "mix_pre_g", "mix_post_g", "ffn2_pre_g", "ffn2_post_g", "b_gate",
                "b_forget"]
_WEIGHTS = ["ffn1_pre_g", "ffn1_post_g", "ffn1_w_gate", "ffn1_w_up", "ffn1_w_down", "mix_pre_g", "mix_post_g",
            "w_in", "b_forget", "mem_norm_g", "w_mem_kv", "w_gate", "b_gate", "w_br_sb", "w_br_fox", "w_br_mem",
            "w_out", "ffn2_pre_g", "ffn2_post_g", "ffn2_w_gate", "ffn2_w_up", "ffn2_w_down"]


def _pack_small(vals, L, D):
    rows = []
    for l in range(L):
        for name in _SMALL_LAYER:
            t = vals[name][l]
            rows.append(jnp.pad(t, (0, -t.shape[0] % D)).reshape(-1, D))
    rows.append(vals["mem_norm_g"].reshape(1, D))
    packed = jnp.concatenate(rows, axis=0)
    return _pad_rows(packed, -(-packed.shape[0] // 8) * 8)


def _unpack_small(packed, shapes, L, D):
    out = {name: [] for name in _SMALL_LAYER}
    r = 0
    for l in range(L):
        for name in _SMALL_LAYER:
            n = shapes[name][1]
            nr = -(-n // D)
            out[name].append(packed[r:r + nr].reshape(-1)[:n])
            r += nr
    res = {name: jnp.stack(v) for name, v in out.items()}
    res["mem_norm_g"] = packed[r]
    return res


def _unpack_grads(lay, g, l_shapes):
    def seg(key):
        off, n = lay.seg[key]
        return g[off:off + n]
    br = seg("br").reshape(lay.br_shard, lay.br_in).T
    third = lay.br_in // 3
    return {
        "ffn1_w_gate": seg("g1").T, "ffn1_w_up": seg("u1").T, "ffn1_w_down": seg("d1"),
        "w_in": seg("win")[:lay.in_shard].T, "w_mem_kv": seg("kv"), "w_gate": seg("gate").T,
        "w_br_sb": br[:third], "w_br_fox": br[third:2 * third], "w_br_mem": br[2 * third:],
        "w_out": seg("out"), "ffn2_w_gate": seg("g2").T, "ffn2_w_up": seg("u2").T, "ffn2_w_down": seg("d2"),
    }


class _Exchanges:
    def gather(self, name, block):
        return _all_gather(name, block)

    def gather_start(self, block):
        return _exchange_start("ag_start", block, per_peer=False)

    def gather_wait(self, started, after):
        block, landed = _exchange_wait("ag_wait", started, after)
        return landed, (block, _my_index())

    def scatter(self, parts):
        return _reduce_scatter("w", parts)

    def scatter_start(self, parts):
        return _exchange_start("rs_start", parts, per_peer=True)

    def scatter_wait(self, started, after):
        parts, landed = _exchange_wait("rs_wait", started, after)
        return _sum_landed("rs_sum8", landed, parts)

    def token(self, started):
        return started[4]

    def loss_sum(self, part):
        return lax.psum(part, ("x", "y", "c"))


def _step(p, m, v, x, mem, tgt, ex):
    L, D = p["ffn1_pre_g"].shape
    lay = _Layout(D, p["ffn1_w_gate"].shape[2], p["w_in"].shape[2], p["w_mem_kv"].shape[1], p["w_gate"].shape[2],
                  3 * p["w_br_sb"].shape[1], p["w_br_sb"].shape[2], p["w_out"].shape[1])
    blocks = [_pack_layer(lay, l, p) for l in range(L)]
    sps = [{name: p[name][l] for name in _SMALL_LAYER} for l in range(L)]

    mem_n = _rms_fwd("mem_norm", mem, p["mem_norm_g"], BF16)
    gathered, own = ex.gather("ag_weights", blocks[0]), None
    h, saved, ws = x, [], []
    for l in range(L):
        if l + 1 < L:
            nxt, gathered = lax.optimization_barrier((blocks[l + 1], gathered))
            started = ex.gather_start(nxt)
            sp = _after(sps[l], "ffn1_pre_g", ex.token(started))
        else:
            sp = sps[l]
        ws.append(_unpack_layer(lay, gathered, own))
        h, s = _layer_fwd(lay, h, ws[l], sp, mem_n)
        saved.append(s)
        if l + 1 < L:
            gathered, own = ex.gather_wait(started, h)
    loss_part, dh = _loss_grad(h, tgt)
    loss = ex.loss_sum(loss_part)

    dmem_n = jnp.zeros(mem.shape, F32)
    big, small = [None] * L, {name: [None] * L for name in _SMALL_LAYER}
    flying, token = {}, None
    for l in reversed(range(L)):
        sp = sps[l] if token is None else _after(sps[l], "ffn2_post_g", token)
        dh, packed, sm, dmem_n = _layer_bwd(lay, dh, saved[l], ws[l], sp, mem_n, dmem_n)
        if l > 0:
            flying[l] = ex.scatter_start(packed)
            token = ex.token(flying[l])
        else:
            big[l] = _unpack_grads(lay, ex.scatter(packed), None)
        for name in _SMALL_LAYER:
            small[name][l] = sm[name]
    for l, started in flying.items():
        big[l] = _unpack_grads(lay, ex.scatter_wait(started, dh), None)
    _, d_memg = _rms_bwd("mem_dnorm", mem, p["mem_norm_g"], dmem_n, F32)

    small_g = {name: jnp.stack(vs) for name, vs in small.items()}
    small_g["mem_norm_g"] = d_memg
    small_names = _SMALL_LAYER + ["mem_norm_g"]
    shapes = {name: p[name].shape for name in small_names}
    g_all = ex.gather("ag_small", _pack_small(small_g, L, D))
    packs = [_pack_small({name: t[name] for name in small_names}, L, D) for t in (p, m, v)]
    res = [_unpack_small(t, shapes, L, D) for t in _adamw("adamw_small", g_all, *packs)]

    out = {kind: {} for kind in ("grad", "delta", "new_m", "new_v")}
    for name in small_names:
        for kind, r in zip(("grad", "delta", "new_m", "new_v"), res):
            out[kind][name] = r[name].reshape(p[name].shape)
    for name in _SHARDED:
        g = jnp.stack([big[l][name] for l in range(L)])
        shp = g.shape
        flat = lambda t: t.reshape(-1, shp[-1])
        r = _adamw("adamw_" + name, flat(g)[None], flat(p[name]), flat(m[name]), flat(v[name]))
        for kind, t in zip(("grad", "delta", "new_m", "new_v"), r):
            out[kind][name] = t.reshape(shp)
    return loss, dh, out


def kernel(x, mem, ffn1_pre_g, ffn1_post_g, ffn1_w_gate, ffn1_w_up, ffn1_w_down, mix_pre_g, mix_post_g, w_in, b_forget, mem_norm_g, w_mem_kv, w_gate, b_gate, w_br_sb, w_br_fox, w_br_mem, w_out, ffn2_pre_g, ffn2_post_g, ffn2_w_gate, ffn2_w_up, ffn2_w_down, loss_target, m_ffn1_pre_g, m_ffn1_post_g, m_ffn1_w_gate, m_ffn1_w_up, m_ffn1_w_down, m_mix_pre_g, m_mix_post_g, m_w_in, m_b_forget, m_mem_norm_g, m_w_mem_kv, m_w_gate, m_b_gate, m_w_br_sb, m_w_br_fox, m_w_br_mem, m_w_out, m_ffn2_pre_g, m_ffn2_post_g, m_ffn2_w_gate, m_ffn2_w_up, m_ffn2_w_down, v_ffn1_pre_g, v_ffn1_post_g, v_ffn1_w_gate, v_ffn1_w_up, v_ffn1_w_down, v_mix_pre_g, v_mix_post_g, v_w_in, v_b_forget, v_mem_norm_g, v_w_mem_kv, v_w_gate, v_b_gate, v_w_br_sb, v_w_br_fox, v_w_br_mem, v_w_out, v_ffn2_pre_g, v_ffn2_post_g, v_ffn2_w_gate, v_ffn2_w_up, v_ffn2_w_down):
    p = dict(zip(_WEIGHTS, (ffn1_pre_g, ffn1_post_g, ffn1_w_gate, ffn1_w_up, ffn1_w_down, mix_pre_g, mix_post_g, w_in, b_forget, mem_norm_g, w_mem_kv, w_gate, b_gate, w_br_sb, w_br_fox, w_br_mem, w_out, ffn2_pre_g, ffn2_post_g, ffn2_w_gate, ffn2_w_up, ffn2_w_down)))
    m = dict(zip(_WEIGHTS, (m_ffn1_pre_g, m_ffn1_post_g, m_ffn1_w_gate, m_ffn1_w_up, m_ffn1_w_down, m_mix_pre_g, m_mix_post_g, m_w_in, m_b_forget, m_mem_norm_g, m_w_mem_kv, m_w_gate, m_b_gate, m_w_br_sb, m_w_br_fox, m_w_br_mem, m_w_out, m_ffn2_pre_g, m_ffn2_post_g, m_ffn2_w_gate, m_ffn2_w_up, m_ffn2_w_down)))
    v = dict(zip(_WEIGHTS, (v_ffn1_pre_g, v_ffn1_post_g, v_ffn1_w_gate, v_ffn1_w_up, v_ffn1_w_down, v_mix_pre_g, v_mix_post_g, v_w_in, v_b_forget, v_mem_norm_g, v_w_mem_kv, v_w_gate, v_b_gate, v_w_br_sb, v_w_br_fox, v_w_br_mem, v_w_out, v_ffn2_pre_g, v_ffn2_post_g, v_ffn2_w_gate, v_ffn2_w_up, v_ffn2_w_down)))
    loss, dx, out = _step(p, m, v, x[0], mem[0], loss_target[0], _Exchanges())
    return (loss, dx[None], *[out["grad"][n] for n in _WEIGHTS], *[out["delta"][n] for n in _WEIGHTS],
            *[out["new_m"][n] for n in _WEIGHTS], *[out["new_v"][n] for n in _WEIGHTS])
```

```python
import functools
import math

import jax
import jax.numpy as jnp
from jax import lax
from jax.experimental import pallas as pl
from jax.experimental.pallas import tpu as pltpu

F32 = jnp.float32
BF16 = jnp.bfloat16

LANE = 128
SUBLANE_BF16 = 16
VMEM_LIMIT = 56 * 1024 * 1024
N_DEV = 8
MESH = pl.DeviceIdType.MESH
ANY = pl.BlockSpec(memory_space=pl.ANY)

RMS_EPS = 1e-6
HEAD_DIM = 64
N_SB_HEADS = 8
N_FOX_HEADS = 8
N_MEM_HEADS = 4
NEG = -1e30
ATT_TQ = 1024
ATT_TK = 256
DECAY_TK = 128
IN_TILE = 1280

ADAM_LR = 0.001
ADAM_B1 = 0.9
ADAM_B2 = 0.999
ADAM_EPS = 1e-08
ADAM_WD = 0.01
ADAM_STEP = 10


def _tile(n, target, mult=LANE):
    best = None
    for t in range(mult, min(n, target) + 1, mult):
        if n % t == 0:
            best = t
    return best if best is not None else n


def _cparams(sem):
    return pltpu.CompilerParams(dimension_semantics=sem, vmem_limit_bytes=VMEM_LIMIT)


_DIMS = {"nn": (((1,), (0,)), ((), ())), "nt": (((1,), (1,)), ((), ())), "tn": (((0,), (0,)), ((), ()))}


def _dot(a, b, mode="nn"):
    return lax.dot_general(a.astype(BF16), b.astype(BF16), _DIMS[mode], preferred_element_type=F32)


def _mm(name, pairs, mode, out_dtypes, epilogue=None, extras=(), tm=512, tn=1024, out_rows=None):
    a0, b0 = pairs[0]
    M = a0.shape[1] if mode == "tn" else a0.shape[0]
    N = b0.shape[0] if mode == "nt" else b0.shape[1]
    tm = _tile(M, tm)
    tn = _tile(N, tn)
    np_, ne, no = len(pairs), len(extras), len(out_dtypes)

    def body(*refs):
        a_refs, b_refs = refs[:np_], refs[np_:2 * np_]
        e_refs = refs[2 * np_:2 * np_ + ne]
        o_refs = refs[2 * np_ + ne:]
        accs = [_dot(a[...], b[...], mode) for a, b in zip(a_refs, b_refs)]
        outs = epilogue(accs, [e[...] for e in e_refs]) if epilogue is not None else accs
        for o, val in zip(o_refs, outs):
            o[...] = val.astype(o.dtype)

    in_specs = []
    for a, _ in pairs:
        if mode == "tn":
            in_specs.append(pl.BlockSpec((a.shape[0], tm), lambda j, i: (0, i)))
        else:
            in_specs.append(pl.BlockSpec((tm, a.shape[1]), lambda j, i: (i, 0)))
    for _, b in pairs:
        if mode == "nt":
            in_specs.append(pl.BlockSpec((tn, b.shape[1]), lambda j, i: (j, 0)))
        else:
            in_specs.append(pl.BlockSpec((b.shape[0], tn), lambda j, i: (0, j)))
    for e, off in extras:
        if e.shape[0] == 1:
            in_specs.append(pl.BlockSpec((1, tn), functools.partial(lambda j, i, o: (0, j + o), o=off // tn)))
        else:
            in_specs.append(pl.BlockSpec((tm, tn), functools.partial(lambda j, i, o: (i, j + o), o=off // tn)))
    rows = [tm if r is None else r for r in (out_rows or [None] * no)]
    out_specs = [pl.BlockSpec((r, tn), lambda j, i: (i, j)) for r in rows]
    outs = pl.pallas_call(
        body, name=name, grid=(N // tn, M // tm),
        in_specs=in_specs, out_specs=out_specs,
        out_shape=[jax.ShapeDtypeStruct((M // tm * r, N), dt) for r, dt in zip(rows, out_dtypes)],
        compiler_params=_cparams(("parallel", "parallel")),
    )(*[a for a, _ in pairs], *[b for _, b in pairs], *[e for e, _ in extras])
    return outs[0] if no == 1 else outs


def _sum_accs(accs, _):
    total = accs[0]
    for acc in accs[1:]:
        total = total + acc
    return [total]


def _rstd(x):
    return lax.rsqrt(jnp.mean(x * x, axis=-1, keepdims=True) + RMS_EPS)


def _rms_fwd(name, x, g, out_dtype, res=None, scale=1.0, tr=512):
    R, D = x.shape
    tr = _tile(R, tr, 8)
    has_res = res is not None

    def body(*refs):
        x_ref, g_ref = refs[:2]
        o_ref = refs[-1]
        xv = x_ref[...]
        y = (xv * _rstd(xv)) * g_ref[...]
        if has_res:
            y = refs[2][...] + scale * y
        o_ref[...] = y.astype(o_ref.dtype)

    row = pl.BlockSpec((tr, D), lambda i: (i, 0))
    gain = pl.BlockSpec((1, D), lambda i: (0, 0))
    return pl.pallas_call(
        body, name=name, grid=(R // tr,),
        in_specs=[row, gain] + ([row] if has_res else []), out_specs=row,
        out_shape=jax.ShapeDtypeStruct((R, D), out_dtype),
        compiler_params=_cparams(("parallel",)),
    )(x, g.reshape(1, D), *([res] if has_res else []))


def _rms_bwd(name, x, g, dy, out_dtype, scale=1.0, res=None, tr=512):
    R, D = x.shape
    tr = _tile(R, tr, 8)
    has_res = res is not None

    def body(*refs):
        x_ref, g_ref, dy_ref = refs[:3]
        dx_ref, dg_ref = refs[-2:]
        i = pl.program_id(0)
        xv = x_ref[...]
        xhat = xv * _rstd(xv)
        dyv = dy_ref[...].astype(F32) * scale
        gy = dyv * g_ref[...]
        dx = _rstd(xv) * (gy - xhat * jnp.mean(gy * xhat, axis=-1, keepdims=True))
        if has_res:
            dx = refs[3][...] + dx
        dx_ref[...] = dx.astype(dx_ref.dtype)
        part = jnp.sum(dyv * xhat, axis=0, keepdims=True)

        @pl.when(i == 0)
        def _():
            dg_ref[...] = part

        @pl.when(i > 0)
        def _():
            dg_ref[...] += part

    row = pl.BlockSpec((tr, D), lambda i: (i, 0))
    gain = pl.BlockSpec((1, D), lambda i: (0, 0))
    dx, dg = pl.pallas_call(
        body, name=name, grid=(R // tr,),
        in_specs=[row, gain, row] + ([row] if has_res else []), out_specs=[row, gain],
        out_shape=[jax.ShapeDtypeStruct((R, D), out_dtype), jax.ShapeDtypeStruct((1, D), F32)],
        compiler_params=_cparams(("arbitrary",)),
    )(x, g.reshape(1, D), dy, *([res] if has_res else []))
    return dx, dg[0]


def _loss_grad(y, tgt, tr=512):
    R, D = y.shape
    tr = _tile(R, tr, 8)

    def body(y_ref, t_ref, dy_ref, loss_ref):
        i = pl.program_id(0)
        d = y_ref[...] - t_ref[...]
        dy_ref[...] = d / D
        part = 0.5 * jnp.sum(jnp.mean(d * d, axis=-1, keepdims=True), axis=0, keepdims=True)
        tile = jnp.broadcast_to(part, loss_ref.shape)

        @pl.when(i == 0)
        def _():
            loss_ref[...] = tile

        @pl.when(i > 0)
        def _():
            loss_ref[...] += tile

    row = pl.BlockSpec((tr, D), lambda i: (i, 0))
    dy, loss = pl.pallas_call(
        body, name="loss_grad", grid=(R // tr,),
        in_specs=[row, row], out_specs=[row, pl.BlockSpec((8, LANE), lambda i: (0, 0))],
        out_shape=[jax.ShapeDtypeStruct((R, D), F32), jax.ShapeDtypeStruct((8, LANE), F32)],
        compiler_params=_cparams(("arbitrary",)),
    )(y, tgt)
    return loss[0, 0], dy


def _colsum(name, x, tr=512, tn=1024):
    R, N = x.shape
    tr, tn = _tile(R, tr, 8), _tile(N, tn)

    def body(x_ref, o_ref):
        i = pl.program_id(1)
        part = jnp.sum(x_ref[...].astype(F32), axis=0, keepdims=True)

        @pl.when(i == 0)
        def _():
            o_ref[...] = part

        @pl.when(i > 0)
        def _():
            o_ref[...] += part

    out = pl.pallas_call(
        body, name=name, grid=(N // tn, R // tr),
        in_specs=[pl.BlockSpec((tr, tn), lambda j, i: (i, j))], out_specs=pl.BlockSpec((1, tn), lambda j, i: (0, j)),
        out_shape=jax.ShapeDtypeStruct((1, N), F32),
        compiler_params=_cparams(("parallel", "arbitrary")),
    )(x)
    return out[0]


def _tri(tk, rel):
    j = lax.broadcasted_iota(jnp.int32, (tk, tk), 0)
    s = lax.broadcasted_iota(jnp.int32, (tk, tk), 1)
    return rel(j, s).astype(BF16)


def _dot_split(x, m, parts=2):
    total = None
    rem = x
    for _ in range(parts):
        piece = rem.astype(BF16)
        rem = rem - piece.astype(F32)
        term = jnp.dot(piece, m, preferred_element_type=F32)
        total = term if total is None else total + term
    return total


def _log_not_and_beta(z, mask):
    ln = -(jnp.maximum(z, 0.0) + jnp.log(1.0 + jnp.exp(-jnp.abs(z))))
    return (ln if mask is None else jnp.where(mask, ln, 0.0)), ln + z


def _att_tiles(T, Tk, causal):
    tq = min(ATT_TQ, T)
    tk = min(ATT_TK, tq if causal else Tk)
    return tq, tk, (tq if causal else Tk) // tk


def _key_base(j, tq):
    return j * tq if isinstance(j, int) else pl.multiple_of(j * tq, tq)


def _is_pow2(scale):
    return math.log2(scale).is_integer()


def _per_head(x, hpb, d):
    if hpb == 1:
        return [x]
    lane = lax.broadcasted_iota(jnp.int32, x.shape, 1)
    return [jnp.where((lane >= h * d) & (lane < (h + 1) * d), x, jnp.zeros_like(x)) for h in range(hpb)]


def _join_heads(xs, d):
    out = xs[-1]
    if len(xs) > 1:
        lane = lax.broadcasted_iota(jnp.int32, out.shape, 1)
        for h in reversed(range(len(xs) - 1)):
            out = jnp.where(lane < (h + 1) * d, xs[h], out)
    return out


def _lane_tile(rows, off, whole):
    if whole:
        return pl.BlockSpec((rows, LANE), lambda g, i: (0, off + g))
    return pl.BlockSpec((rows, LANE), lambda g, i: (i, off + g))


def _sb_fwd(q, k, v, n_tiles, d, scale):
    T = q[0].shape[0]
    hpb = LANE // d
    tq, tk, nsub = _att_tiles(T, T, True)
    assert _is_pow2(scale)

    def body(q_ref, k_ref, v_ref, ob_ref, rt_ref, acc_ref, r_ref):
        qi = pl.program_id(1)
        qh = _per_head(q_ref[...] * scale, hpb, d)
        acc_ref[...] = jnp.zeros_like(acc_ref)
        r_ref[...] = jnp.zeros_like(r_ref)
        row = lax.broadcasted_iota(jnp.int32, (tq, tk), 0)
        col = lax.broadcasted_iota(jnp.int32, (tq, tk), 1)
        after = _tri(tk, lambda j, s: j > s)

        def walk(h, base, r0, r1, subs, diagonal):
            parts = []
            for u in subs:
                z = _dot(qh[h][r0:r1], k_ref[pl.ds(base + u * tk, tk), :], "nt")
                mask = (col[r0:r1] + u * tk) < row[r0:r1] if diagonal else None
                ln, lb = _log_not_and_beta(z, mask)
                between = _dot_split(ln, after, parts=1)
                first = ln[:, 0:1].astype(BF16).astype(F32)
                parts.append((u, lb, between, between[:, 0:1] + first, mask))
            r = r_ref[h, r0:r1, :]
            out = None
            for u, lb, between, total, mask in parts:
                w = jnp.exp(lb + between + r)
                if diagonal:
                    w = jnp.where(mask, w, 0.0)
                term = _dot(w, v_ref[pl.ds(base + u * tk, tk), :])
                out = term if out is None else out + term
                r = r + total
            acc_ref[h, r0:r1, :] += out
            r_ref[h, r0:r1, :] = r

        def step(j, diagonal):
            base = _key_base(j, tq)
            for h in range(hpb):
                if diagonal and nsub % 2 == 0:
                    walk(h, base, 0, tq // 2, range(nsub // 2 - 1, -1, -1), True)
                    walk(h, base, tq // 2, tq, range(nsub - 1, -1, -1), True)
                else:
                    walk(h, base, 0, tq, range(nsub - 1, -1, -1), diagonal)

        def below(i, carry):
            step(qi - 1 - i, False)
            return carry

        step(qi, True)
        lax.fori_loop(0, qi, below, 0)
        ob_ref[...] = _join_heads([acc_ref[h] for h in range(hpb)], d).astype(ob_ref.dtype)
        rt_ref[...] = r_ref[...]

    out = pl.BlockSpec((tq, LANE), lambda g, i: (i, g))
    col = pl.BlockSpec((hpb, tq, 1), lambda g, i: (g, i, 0))
    return pl.pallas_call(
        body, name="sb_fwd", grid=(n_tiles, T // tq),
        in_specs=[_lane_tile(tq, q[1], False), _lane_tile(T, k[1], True), _lane_tile(T, v[1], True)],
        out_specs=[out, col],
        out_shape=[jax.ShapeDtypeStruct((T, n_tiles * LANE), BF16), jax.ShapeDtypeStruct((n_tiles * hpb, T, 1), F32)],
        scratch_shapes=[pltpu.VMEM((hpb, tq, LANE), F32), pltpu.VMEM((hpb, tq, 1), F32)],
        compiler_params=_cparams(("parallel", "arbitrary")),
    )(q[0], k[0], v[0])


def _sb_bwd(q, k, v, do, rtot, n_tiles, d, scale):
    T = q[0].shape[0]
    hpb = LANE // d
    tq, tk, nsub = _att_tiles(T, T, True)
    assert _is_pow2(scale)

    def body(q_ref, k_ref, v_ref, do_ref, rt_ref, dq_ref, dk_ref, dv_ref, dk_acc, dv_acc, dq_acc, p_ref, c_ref):
        qi = pl.program_id(1)

        @pl.when(qi == 0)
        def _():
            dk_acc[...] = jnp.zeros_like(dk_acc)
            dv_acc[...] = jnp.zeros_like(dv_acc)

        qh = _per_head(q_ref[...] * scale, hpb, d)
        doh = _per_head(do_ref[...], hpb, d)
        dq_acc[...] = jnp.zeros_like(dq_acc)
        p_ref[...] = jnp.zeros_like(p_ref)
        c_ref[...] = jnp.zeros_like(c_ref)
        row = lax.broadcasted_iota(jnp.int32, (tq, tk), 0)
        col = lax.broadcasted_iota(jnp.int32, (tq, tk), 1)
        upto = _tri(tk, lambda j, s: j <= s)
        before = _tri(tk, lambda j, s: j < s)
        rt_wide = [jnp.broadcast_to(rt_ref[h], (tq, tk)) for h in range(hpb)]

        def step(j, diagonal):
            base = _key_base(j, tq)
            for h in range(hpb):
                first = []
                for u in range(nsub):
                    ks = base + u * tk
                    r0 = u * tk if diagonal else 0
                    kv = k_ref[pl.ds(ks, tk), :]
                    z = _dot(qh[h][r0:], kv, "nt")
                    mask = (col[r0:] + u * tk) < row[r0:] if diagonal else None
                    ln, lb = _log_not_and_beta(z, mask)
                    dw = _dot(doh[h][r0:], v_ref[pl.ds(ks, tk), :], "nt")
                    first.append((r0, ks, kv, mask, lb, jnp.exp(lb), _dot_split(ln, upto, parts=1), dw))
                rt, pre, cpre = rt_wide[h], p_ref[h], c_ref[h]
                dq = None
                for r0, ks, kv, mask, lb, sig, local, dw in first:
                    if diagonal and r0:
                        pre, cpre = pre[tk:], cpre[tk:]
                    prefix = local + pre
                    w = jnp.exp(lb + (rt[r0:] - prefix))
                    if diagonal:
                        w = jnp.where(mask, w, 0.0)
                    g = dw * w
                    c = _dot_split(g, before, parts=1) + cpre
                    dz = g * (1.0 - sig) - c * sig
                    if diagonal:
                        dz = jnp.where(mask, dz, 0.0)
                    term = _dot(dz, kv)
                    if diagonal:
                        dq_acc[h, r0:, :] += term
                    else:
                        dq = term if dq is None else dq + term
                    dk_acc[pl.ds(ks, tk), :] += _dot(dz, qh[h][r0:], "tn")
                    dv_acc[pl.ds(ks, tk), :] += _dot(w, doh[h][r0:], "tn")
                    pre = prefix[:, tk - 1:tk]
                    cpre = c[:, tk - 1:tk] + g[:, tk - 1:tk]
                if not diagonal:
                    dq_acc[h] += dq
                    p_ref[h] = pre
                    c_ref[h] = cpre

        def below(j, carry):
            step(j, False)
            return carry

        lax.fori_loop(0, qi, below, 0)
        step(qi, True)
        dq_ref[...] = (_join_heads([dq_acc[h] for h in range(hpb)], d) * scale).astype(dq_ref.dtype)

        @pl.when(qi == pl.num_programs(1) - 1)
        def _():
            dk_ref[...] = dk_acc[...].astype(dk_ref.dtype)
            dv_ref[...] = dv_acc[...].astype(dv_ref.dtype)

    blk = pl.BlockSpec((tq, LANE), lambda g, i: (i, g))
    full = pl.BlockSpec((T, LANE), lambda g, i: (0, g))
    col = pl.BlockSpec((hpb, tq, 1), lambda g, i: (g, i, 0))
    wide = jax.ShapeDtypeStruct((T, n_tiles * LANE), BF16)
    return pl.pallas_call(
        body, name="sb_bwd", grid=(n_tiles, T // tq),
        in_specs=[_lane_tile(tq, q[1], False), _lane_tile(T, k[1], True), _lane_tile(T, v[1], True), blk, col],
        out_specs=[blk, full, full], out_shape=[wide, wide, wide],
        scratch_shapes=[pltpu.VMEM((T, LANE), F32), pltpu.VMEM((T, LANE), F32), pltpu.VMEM((hpb, tq, LANE), F32),
                        pltpu.VMEM((hpb, tq, 1), F32), pltpu.VMEM((hpb, tq, 1), F32)],
        compiler_params=_cparams(("parallel", "arbitrary")),
    )(q[0], k[0], v[0], do, rtot)


def _attn_fwd(name, q, k, v, n_tiles, d, scale, c=None):
    T, Tk = q[0].shape[0], k[0].shape[0]
    hpb = LANE // d
    H = n_tiles * hpb
    causal = c is not None
    tq, tk, nsub = _att_tiles(T, Tk, causal)
    fold = _is_pow2(scale)

    def body(*refs):
        q_ref, k_ref, v_ref = refs[:3]
        cc_ref, cr_ref = refs[3:5] if causal else (None, None)
        o_ref, ob_ref, lse_ref, m_ref, l_ref, acc_ref = refs[-6:]
        qi = pl.program_id(1)
        qh = _per_head(q_ref[...] * scale if fold else q_ref[...], hpb, d)
        bias = [jnp.broadcast_to(cc_ref[h], (tq, tk)) for h in range(hpb)] if causal else None
        ones = jnp.ones((tk, LANE), BF16)
        m_ref[...] = jnp.full_like(m_ref, NEG)
        l_ref[...] = jnp.zeros_like(l_ref)
        acc_ref[...] = jnp.zeros_like(acc_ref)
        row = lax.broadcasted_iota(jnp.int32, (tq, tk), 0)
        col = lax.broadcasted_iota(jnp.int32, (tq, tk), 1)

        def absorb(h, j, base, r0, r1, subs, diagonal):
            zs = []
            for u in subs:
                z = _dot(qh[h][r0:r1], k_ref[pl.ds(base + u * tk, tk), :], "nt")
                if not fold:
                    z = z * scale
                if causal:
                    z = z + bias[h][r0:r1] - cr_ref[h, j * nsub + u]
                if diagonal:
                    z = jnp.where((col[r0:r1] + u * tk) <= row[r0:r1], z, NEG)
                zs.append(z)
            m_prev = m_ref[h, r0:r1, :]
            top = zs[0]
            for z in zs[1:]:
                top = jnp.maximum(top, z)
            m_new = jnp.maximum(m_prev, jnp.max(top, axis=1, keepdims=True))
            alpha = jnp.exp(m_prev - m_new)
            l_new = alpha * l_ref[h, r0:r1, :]
            out = alpha * acc_ref[h, r0:r1, :]
            m_wide = jnp.broadcast_to(m_new, top.shape)
            for u, z in zip(subs, zs):
                p = jnp.exp(z - m_wide).astype(BF16)
                l_new = l_new + jnp.dot(p, ones, preferred_element_type=F32)[:, 0:1]
                out = out + _dot(p, v_ref[pl.ds(base + u * tk, tk), :])
            l_ref[h, r0:r1, :] = l_new
            acc_ref[h, r0:r1, :] = out
            m_ref[h, r0:r1, :] = m_new

        def step(j, diagonal):
            base = _key_base(j, tq)
            for h in range(hpb):
                if diagonal and nsub % 2 == 0:
                    absorb(h, j, base, 0, tq // 2, range(nsub // 2), True)
                    absorb(h, j, base, tq // 2, tq, range(nsub), True)
                else:
                    absorb(h, j, base, 0, tq, range(nsub), diagonal)

        def below(j, carry):
            step(j, False)
            return carry

        if causal:
            lax.fori_loop(0, qi, below, 0)
            step(qi, True)
        else:
            step(0, False)
        o = _join_heads([acc_ref[h] / l_ref[h] for h in range(hpb)], d)
        o_ref[...] = o
        ob_ref[...] = o.astype(ob_ref.dtype)
        lse_ref[...] = m_ref[...] + jnp.log(l_ref[...])

    out = pl.BlockSpec((tq, LANE), lambda g, i: (i, g))
    col = pl.BlockSpec((hpb, tq, 1), lambda g, i: (g, i, 0))
    in_specs = [_lane_tile(tq, q[1], False), _lane_tile(Tk, k[1], True), _lane_tile(Tk, v[1], True)]
    args = [q[0], k[0], v[0]]
    if causal:
        in_specs += [col, pl.BlockSpec((hpb, T // tk, 1, tk), lambda g, i: (g, 0, 0, 0))]
        args += [c.reshape(H, T, 1), c.reshape(H, T // tk, 1, tk)]
    return pl.pallas_call(
        body, name=name, grid=(n_tiles, T // tq),
        in_specs=in_specs, out_specs=[out, out, col],
        out_shape=[jax.ShapeDtypeStruct((T, n_tiles * LANE), F32), jax.ShapeDtypeStruct((T, n_tiles * LANE), BF16),
                   jax.ShapeDtypeStruct((H, T, 1), F32)],
        scratch_shapes=[pltpu.VMEM((hpb, tq, 1), F32), pltpu.VMEM((hpb, tq, 1), F32),
                        pltpu.VMEM((hpb, tq, LANE), F32)],
        compiler_params=_cparams(("parallel", "arbitrary")),
    )(*args)


def _attn_bwd(name, q, k, v, o, do, lse, n_tiles, d, scale, c=None):
    T, Tk = q[0].shape[0], k[0].shape[0]
    hpb = LANE // d
    H = n_tiles * hpb
    causal = c is not None
    tq, tk, nsub = _att_tiles(T, Tk, causal)
    fold = _is_pow2(scale)

    def body(*refs):
        q_ref, k_ref, v_ref, o_ref, do_ref, lse_ref = refs[:6]
        cc_ref, cr_ref = refs[6:8] if causal else (None, None)
        n_out = 5 if causal else 3
        outs = refs[-(n_out + 3):-3]
        dq_ref, dk_ref, dv_ref = outs[:3]
        dc_ref, drow_ref = outs[3:5] if causal else (None, None)
        dk_acc, dv_acc, dq_acc = refs[-3:]
        qi = pl.program_id(1)

        @pl.when(qi == 0)
        def _():
            dk_acc[...] = jnp.zeros_like(dk_acc)
            dv_acc[...] = jnp.zeros_like(dv_acc)
            if causal:
                dc_ref[...] = jnp.zeros_like(dc_ref)

        qh = _per_head(q_ref[...] * scale if fold else q_ref[...], hpb, d)
        doh = _per_head(do_ref[...], hpb, d)
        delta_wide = [jnp.broadcast_to(jnp.sum(t.astype(F32) * o_ref[...], axis=1, keepdims=True), (tq, tk))
                      for t in doh]
        shift = [jnp.broadcast_to((cc_ref[h] - lse_ref[h]) if causal else -lse_ref[h], (tq, tk)) for h in range(hpb)]
        dq_acc[...] = jnp.zeros_like(dq_acc)
        if causal:
            drow_ref[...] = jnp.zeros_like(drow_ref)
        row = lax.broadcasted_iota(jnp.int32, (tq, tk), 0)
        col = lax.broadcasted_iota(jnp.int32, (tq, tk), 1)

        def step(j, diagonal):
            base = _key_base(j, tq)
            for h in range(hpb):
                dq, dsum = None, None
                for u in range(nsub):
                    ks = base + u * tk
                    r0 = u * tk if diagonal else 0
                    kv = k_ref[pl.ds(ks, tk), :]
                    z = _dot(qh[h][r0:], kv, "nt")
                    if not fold:
                        z = z * scale
                    z = z + shift[h][r0:]
                    if causal:
                        z = z - cr_ref[h, j * nsub + u]
                    if diagonal:
                        z = jnp.where((col[r0:] + u * tk) <= row[r0:], z, NEG)
                    p = jnp.exp(z)
                    ds = p * (_dot(doh[h][r0:], v_ref[pl.ds(ks, tk), :], "nt") - delta_wide[h][r0:])
                    term = _dot(ds, kv)
                    dk = _dot(ds, qh[h][r0:], "tn")
                    dk_acc[pl.ds(ks, tk), :] += dk if fold else dk * scale
                    dv_acc[pl.ds(ks, tk), :] += _dot(p, doh[h][r0:], "tn")
                    if causal:
                        dc_ref[h, j * nsub + u] -= jnp.sum(ds, axis=0, keepdims=True)
                    if diagonal:
                        dq_acc[h, r0:, :] += term
                        drow_ref[h, r0:, :] += jnp.sum(ds, axis=1, keepdims=True)
                    else:
                        dq = term if dq is None else dq + term
                        if causal:
                            dsum = ds if dsum is None else dsum + ds
                if not diagonal:
                    dq_acc[h] += dq
                    if causal:
                        drow_ref[h] += jnp.sum(dsum, axis=1, keepdims=True)

        def below(j, carry):
            step(j, False)
            return carry

        if causal:
            lax.fori_loop(0, qi, below, 0)
            step(qi, True)
        else:
            step(0, False)
        dq_ref[...] = (_join_heads([dq_acc[h] for h in range(hpb)], d) * scale).astype(dq_ref.dtype)

        @pl.when(qi == pl.num_programs(1) - 1)
        def _():
            dk_ref[...] = dk_acc[...].astype(dk_ref.dtype)
            dv_ref[...] = dv_acc[...].astype(dv_ref.dtype)

    blk = pl.BlockSpec((tq, LANE), lambda g, i: (i, g))
    full = pl.BlockSpec((Tk, LANE), lambda g, i: (0, g))
    col = pl.BlockSpec((hpb, tq, 1), lambda g, i: (g, i, 0))
    crow = pl.BlockSpec((hpb, T // tk, 1, tk), lambda g, i: (g, 0, 0, 0))
    in_specs = [_lane_tile(tq, q[1], False), _lane_tile(Tk, k[1], True), _lane_tile(Tk, v[1], True), blk, blk, col]
    args = [q[0], k[0], v[0], o, do, lse]
    out_specs = [blk, full, full]
    out_shape = [jax.ShapeDtypeStruct((T, n_tiles * LANE), BF16), jax.ShapeDtypeStruct((Tk, n_tiles * LANE), BF16),
                 jax.ShapeDtypeStruct((Tk, n_tiles * LANE), BF16)]
    if causal:
        in_specs += [col, crow]
        args += [c.reshape(H, T, 1), c.reshape(H, T // tk, 1, tk)]
        out_specs += [crow, col]
        out_shape += [jax.ShapeDtypeStruct((H, T // tk, 1, tk), F32), jax.ShapeDtypeStruct((H, T, 1), F32)]
    outs = pl.pallas_call(
        body, name=name, grid=(n_tiles, T // tq),
        in_specs=in_specs, out_specs=out_specs, out_shape=out_shape,
        scratch_shapes=[pltpu.VMEM((Tk, LANE), F32), pltpu.VMEM((Tk, LANE), F32), pltpu.VMEM((hpb, tq, LANE), F32)],
        compiler_params=_cparams(("parallel", "arbitrary")),
    )(*args)
    if causal:
        return outs[0], outs[1], outs[2], outs[3].reshape(H, T), outs[4].reshape(H, T)
    return outs


def _decay_fwd(fl, b):
    H, T = fl.shape
    tk = DECAY_TK

    def body(x_ref, b_ref, c_ref):
        upto = _tri(tk, lambda j, s: j <= s)
        carry = jnp.zeros((H, 1), F32)
        for i in range(T // tk):
            xv = x_ref[:, i * tk:(i + 1) * tk] + b_ref[...]
            lf = jnp.minimum(xv, 0.0) - jnp.log(1.0 + jnp.exp(-jnp.abs(xv)))
            pref = _dot_split(lf, upto, parts=3) + carry
            c_ref[:, i * tk:(i + 1) * tk] = pref
            carry = pref[:, tk - 1:tk]

    vm = pl.BlockSpec(memory_space=pltpu.VMEM)
    return pl.pallas_call(
        body, name="decay_fwd", in_specs=[vm, vm], out_specs=vm,
        out_shape=jax.ShapeDtypeStruct((H, T), F32),
    )(fl, b)


def _decay_bwd(dc_cols, dc_rows, fl, b):
    H, T = fl.shape
    tk = DECAY_TK

    def body(dc_ref, dr_ref, x_ref, b_ref, dx_ref, db_ref):
        from_ = _tri(tk, lambda j, s: j >= s)
        carry = jnp.zeros((H, 1), F32)
        total = jnp.zeros((H, 1), F32)
        for i in reversed(range(T // tk)):
            sl = slice(i * tk, (i + 1) * tk)
            suffix = _dot_split(dc_ref[:, sl] + dr_ref[:, sl], from_, parts=3) + carry
            xv = x_ref[:, sl] + b_ref[...]
            dx = suffix / (1.0 + jnp.exp(xv))
            dx_ref[:, sl] = dx
            total = total + jnp.sum(dx, axis=1, keepdims=True)
            carry = suffix[:, 0:1]
        db_ref[...] = jnp.broadcast_to(total, db_ref.shape)

    vm = pl.BlockSpec(memory_space=pltpu.VMEM)
    dx, db = pl.pallas_call(
        body, name="decay_bwd", in_specs=[vm, vm, vm, vm], out_specs=[vm, vm],
        out_shape=[jax.ShapeDtypeStruct((H, T), F32), jax.ShapeDtypeStruct((H, LANE), F32)],
    )(dc_cols, dc_rows, fl, b)
    return dx, db[:, 0]


def _place():
    x, y, c = lax.axis_index("x"), lax.axis_index("y"), lax.axis_index("c")
    return x, y, c, [(1 - x, y), (x, 1 - y), (1 - x, 1 - y)]


def _all_gather(name, block):
    R, C = block.shape

    def body(x_ref, out_ref, send_sems, recv_sems, local_sem):
        x, y, c, chips = _place()
        me, sibling = (x, y, c), (x, y, 1 - c)

        def rows(px, py, pc):
            return out_ref.at[4 * px + 2 * py + pc]

        def copy(k, blk, to, src=None):
            return pltpu.make_async_remote_copy(
                src_ref=rows(*blk) if src is None else src, dst_ref=rows(*blk),
                send_sem=send_sems.at[k], recv_sem=recv_sems.at[k], device_id=to, device_id_type=MESH)

        mine = pltpu.make_async_copy(x_ref, rows(*me), local_sem)
        mine.start()
        first = [copy(0, me, sibling, src=x_ref)]
        first += [copy(1 + j, me, (*chip, c), src=x_ref) for j, chip in enumerate(chips)]
        for cp in first:
            cp.start()
        passed = [copy(4 + j, (*chip, c), sibling) for j, chip in enumerate(chips)]
        for j, chip in enumerate(chips):
            copy(1 + j, (*chip, c), me).wait_recv()
            passed[j].start()
        copy(0, sibling, me).wait_recv()
        for j, chip in enumerate(chips):
            copy(4 + j, (*chip, 1 - c), me).wait_recv()
        for cp in first + passed:
            cp.wait_send()
        mine.wait()

    return pl.pallas_call(
        body, name=name, in_specs=[ANY], out_specs=ANY,
        out_shape=jax.ShapeDtypeStruct((N_DEV, R, C), block.dtype),
        scratch_shapes=[pltpu.SemaphoreType.DMA((7,)), pltpu.SemaphoreType.DMA((7,)), pltpu.SemaphoreType.DMA(())],
    )(block)


def _swap_with_sibling(name, parts):
    _, R, C = parts.shape

    def body(p_ref, out_ref, send_sems, recv_sems):
        x, y, c, _ = _place()
        copies = [pltpu.make_async_remote_copy(
            src_ref=p_ref.at[2 * q + (1 - c)], dst_ref=out_ref.at[q],
            send_sem=send_sems.at[q], recv_sem=recv_sems.at[q], device_id=(x, y, 1 - c), device_id_type=MESH)
            for q in range(4)]
        for cp in copies:
            cp.start()
        for cp in copies:
            cp.wait_recv()
        for cp in copies:
            cp.wait_send()

    return pl.pallas_call(
        body, name=name, in_specs=[ANY], out_specs=ANY,
        out_shape=jax.ShapeDtypeStruct((4, R, C), parts.dtype),
        scratch_shapes=[pltpu.SemaphoreType.DMA((4,)), pltpu.SemaphoreType.DMA((4,))],
    )(parts)


def _add_own(name, parts, got, tr=512):
    _, R, C = parts.shape
    tr = _tile(R, tr, SUBLANE_BF16)

    def body(c_ref, p_ref, g_ref, o_ref):
        o_ref[...] = (p_ref[...].astype(F32) + g_ref[...].astype(F32)).astype(o_ref.dtype)

    return pl.pallas_call(
        body, name=name,
        grid_spec=pltpu.PrefetchScalarGridSpec(
            num_scalar_prefetch=1, grid=(4, R // tr),
            in_specs=[pl.BlockSpec((1, tr, C), lambda q, i, c: (2 * q + c[0], i, 0)),
                      pl.BlockSpec((1, tr, C), lambda q, i, c: (q, i, 0))],
            out_specs=pl.BlockSpec((1, tr, C), lambda q, i, c: (q, i, 0))),
        out_shape=jax.ShapeDtypeStruct((4, R, C), parts.dtype),
        compiler_params=_cparams(("parallel", "parallel")),
    )(lax.axis_index("c").astype(jnp.int32).reshape(1), parts, got)


def _swap_with_chips(name, parts):
    _, R, C = parts.shape

    def body(p_ref, out_ref, send_sems, recv_sems, local_sem):
        x, y, c, chips = _place()
        my_chip = 2 * x + y
        mine = pltpu.make_async_copy(p_ref.at[my_chip], out_ref.at[my_chip], local_sem)
        mine.start()
        sends = [pltpu.make_async_remote_copy(
            src_ref=p_ref.at[2 * cx + cy], dst_ref=out_ref.at[my_chip],
            send_sem=send_sems.at[j], recv_sem=recv_sems.at[j], device_id=(cx, cy, c), device_id_type=MESH)
            for j, (cx, cy) in enumerate(chips)]
        for cp in sends:
            cp.start()
        for j, (cx, cy) in enumerate(chips):
            pltpu.make_async_remote_copy(
                src_ref=p_ref.at[my_chip], dst_ref=out_ref.at[2 * cx + cy],
                send_sem=send_sems.at[j], recv_sem=recv_sems.at[j], device_id=(cx, cy, c), device_id_type=MESH,
            ).wait_recv()
        for cp in sends:
            cp.wait_send()
        mine.wait()

    return pl.pallas_call(
        body, name=name, in_specs=[ANY], out_specs=ANY,
        out_shape=jax.ShapeDtypeStruct((4, R, C), parts.dtype),
        scratch_shapes=[pltpu.SemaphoreType.DMA((3,)), pltpu.SemaphoreType.DMA((3,)), pltpu.SemaphoreType.DMA(())],
    )(parts)


def _sum_parts(name, parts, tr=512):
    P, R, C = parts.shape
    tr = _tile(R, tr, SUBLANE_BF16)

    def body(p_ref, o_ref):
        total = p_ref[0].astype(F32)
        for p in range(1, P):
            total = total + p_ref[p].astype(F32)
        o_ref[...] = total

    return pl.pallas_call(
        body, name=name, grid=(R // tr,),
        in_specs=[pl.BlockSpec((P, tr, C), lambda i: (0, i, 0))], out_specs=pl.BlockSpec((tr, C), lambda i: (i, 0)),
        out_shape=jax.ShapeDtypeStruct((R, C), F32),
        compiler_params=_cparams(("parallel",)),
    )(parts)


_HBM = pl.BlockSpec(memory_space=pltpu.HBM)
_SEM = pl.BlockSpec(memory_space=pltpu.SEMAPHORE)
_EFFECT = pltpu.SideEffectType.DATAFLOW_SIDE_EFFECTING


def _flipped(x, y, c, k):
    px, py, pc = (1 - x if k & 4 else x), (1 - y if k & 2 else y), (1 - c if k & 1 else c)
    return (px, py, pc), 4 * px + 2 * py + pc


def _exchange_start(name, src, per_peer):
    R, C = src.shape[-2:]

    def body(v_ref, land_ref, send_sem, recv_sem, v_thru, land_thru, token):
        x, y, c = lax.axis_index("x"), lax.axis_index("y"), lax.axis_index("c")
        me = 4 * x + 2 * y + c
        for k in range(1, N_DEV):
            peer, idx = _flipped(x, y, c, k)
            pltpu.make_async_remote_copy(
                src_ref=v_ref.at[idx] if per_peer else v_ref, dst_ref=land_ref.at[me],
                send_sem=send_sem, recv_sem=recv_sem, device_id=peer, device_id_type=MESH).start()
        token[...] = jnp.zeros_like(token)

    return pl.pallas_call(
        body, name=name,
        out_shape=(pltpu.SemaphoreType.DMA(()), pltpu.SemaphoreType.DMA(()), pltpu.HBM(src.shape, src.dtype),
                   pltpu.HBM((N_DEV, R, C), src.dtype), jax.ShapeDtypeStruct((8, LANE), F32)),
        in_specs=(_HBM, _HBM), out_specs=(_SEM, _SEM, _HBM, _HBM, pl.BlockSpec(memory_space=pltpu.VMEM)),
        input_output_aliases={0: 2, 1: 3},
        compiler_params=pltpu.CompilerParams(has_side_effects=_EFFECT),
    )(pltpu.with_memory_space_constraint(src, pltpu.HBM),
      pltpu.with_memory_space_constraint(lax.empty((N_DEV, R, C), src.dtype), pltpu.HBM))


def _exchange_wait(name, started, after):
    send_sem, recv_sem, v_thru, land_thru, _ = started

    def body(v_ref, land_ref, send_sem, recv_sem, after_ref, v_dead, got_ref):
        x, y, c = lax.axis_index("x"), lax.axis_index("y"), lax.axis_index("c")
        seven = land_ref.at[pl.ds(0, N_DEV - 1)]
        drain = pltpu.make_async_remote_copy(
            src_ref=seven, dst_ref=seven, send_sem=send_sem, recv_sem=recv_sem,
            device_id=(x, y, c), device_id_type=MESH)
        drain.wait_send()
        drain.wait_recv()

    return pl.pallas_call(
        body, name=name,
        out_shape=(pltpu.HBM(v_thru.shape, v_thru.dtype), pltpu.HBM(land_thru.shape, land_thru.dtype)),
        in_specs=(_HBM, _HBM, _SEM, _SEM, ANY), out_specs=(_HBM, _HBM), input_output_aliases={0: 0, 1: 1},
        compiler_params=pltpu.CompilerParams(has_side_effects=_EFFECT),
    )(v_thru, land_thru, send_sem, recv_sem, after)


def _my_index():
    return 4 * lax.axis_index("x") + 2 * lax.axis_index("y") + lax.axis_index("c")


def _sum_landed(name, landed, parts, tr=512):
    P, R, C = landed.shape
    tr = _tile(R, tr, SUBLANE_BF16)

    def body(me_ref, l_ref, own_ref, o_ref):
        total = None
        for s in range(P):
            part = jnp.where(me_ref[0] == s, own_ref[0], l_ref[s]).astype(F32)
            total = part if total is None else total + part
        o_ref[...] = total

    return pl.pallas_call(
        body, name=name,
        grid_spec=pltpu.PrefetchScalarGridSpec(
            num_scalar_prefetch=1, grid=(R // tr,),
            in_specs=[pl.BlockSpec((P, tr, C), lambda i, me: (0, i, 0)),
                      pl.BlockSpec((1, tr, C), lambda i, me: (me[0], i, 0))],
            out_specs=pl.BlockSpec((tr, C), lambda i, me: (i, 0))),
        out_shape=jax.ShapeDtypeStruct((R, C), F32),
        compiler_params=_cparams(("parallel",)),
    )(_my_index().astype(jnp.int32).reshape(1), landed, parts)


def _after(params, name, token):
    return {**params, name: params[name] + token[0, 0]}


def _reduce_scatter(tag, parts):
    got = _swap_with_sibling("rs_pair_" + tag, parts)
    pair = _add_own("rs_add_" + tag, parts, got)
    quad = _swap_with_chips("rs_chips_" + tag, pair)
    return _sum_parts("rs_sum_" + tag, quad)


def _adamw(name, g_parts, w, m, v, tr=512):
    P, R, C = g_parts.shape
    tr = _tile(R, tr, 8)

    def body(g_ref, w_ref, m_ref, v_ref, go_ref, d_ref, mo_ref, vo_ref):
        g = g_ref[0]
        for p in range(1, P):
            g = g + g_ref[p]
        mn = ADAM_B1 * m_ref[...] + (1.0 - ADAM_B1) * g
        vn = ADAM_B2 * v_ref[...] + (1.0 - ADAM_B2) * (g * g)
        m_hat = mn / (1.0 - ADAM_B1 ** ADAM_STEP)
        v_hat = vn / (1.0 - ADAM_B2 ** ADAM_STEP)
        go_ref[...] = g
        d_ref[...] = -ADAM_LR * (m_hat / (jnp.sqrt(v_hat) + ADAM_EPS) + ADAM_WD * w_ref[...])
        mo_ref[...] = mn
        vo_ref[...] = vn

    row = pl.BlockSpec((tr, C), lambda i: (i, 0))
    return pl.pallas_call(
        body, name=name, grid=(R // tr,),
        in_specs=[pl.BlockSpec((P, tr, C), lambda i: (0, i, 0)), row, row, row], out_specs=[row] * 4,
        out_shape=[jax.ShapeDtypeStruct((R, C), F32)] * 4,
        compiler_params=_cparams(("parallel",)),
    )(g_parts, w, m, v)


def _pad_rows(t, rows):
    return jnp.pad(t, ((0, rows - t.shape[0]), (0, 0)))


class _Layout:
    def __init__(self, D, ff_shard, in_shard, kv_shard, gate_shard, br_in, br_shard, out_shard):
        self.D = D
        self.in_shard = in_shard
        self.in_pad = -(-in_shard // LANE) * LANE
        self.in_cols = -(-N_DEV * in_shard // IN_TILE) * IN_TILE
        self.br_in, self.br_shard = br_in, br_shard
        br_rows = br_shard * br_in // D
        sizes = [("g1", ff_shard), ("u1", ff_shard), ("d1", ff_shard), ("win", self.in_pad), ("kv", kv_shard),
                 ("gate", gate_shard), ("br", br_rows), ("out", out_shard),
                 ("g2", ff_shard), ("u2", ff_shard), ("d2", ff_shard)]
        self.seg, off = {}, 0
        for key, n in sizes:
            assert n % SUBLANE_BF16 == 0, (key, n)
            self.seg[key] = (off, n)
            off += n
        self.rows = off

    def pack(self, parts):
        return jnp.concatenate([parts[key] for key in self.seg], axis=0)

    def take(self, gathered, key, own=None):
        off, n = self.seg[key]
        seg = gathered[:, off:off + n, :]
        if own is not None:
            seg = lax.dynamic_update_slice(seg, own[0][off:off + n][None], (own[1], 0, 0))
        return seg.reshape(N_DEV * n, self.D)

    def spread(self, full, key):
        _, n = self.seg[key]
        return full.reshape(N_DEV, n, self.D)


def _pack_layer(lay, l, p):
    D = lay.D
    br = jnp.concatenate([p["w_br_sb"][l], p["w_br_fox"][l], p["w_br_mem"][l]], axis=0)
    parts = {
        "g1": p["ffn1_w_gate"][l].T, "u1": p["ffn1_w_up"][l].T, "d1": p["ffn1_w_down"][l],
        "win": _pad_rows(p["w_in"][l].T, lay.in_pad), "kv": p["w_mem_kv"][l], "gate": p["w_gate"][l].T,
        "br": br.T.reshape(-1, D), "out": p["w_out"][l],
        "g2": p["ffn2_w_gate"][l].T, "u2": p["ffn2_w_up"][l].T, "d2": p["ffn2_w_down"][l],
    }
    return lay.pack({k: t.astype(BF16) for k, t in parts.items()})


def _align_win(lay, packed):
    D = lay.D
    real = packed.reshape(N_DEV, lay.in_pad, D)[:, :lay.in_shard].reshape(N_DEV * lay.in_shard, D)
    rows = jnp.concatenate([real[:_QKV_W], real[_QKV_W + N_FOX_HEADS:], real[_QKV_W:_QKV_W + N_FOX_HEADS]], axis=0)
    return _pad_rows(rows, lay.in_cols)


def _unalign_win(lay, aligned):
    D = lay.D
    n_real = N_DEV * lay.in_shard
    mem_w = n_real - _QKV_W - N_FOX_HEADS
    real = jnp.concatenate([aligned[:_QKV_W], aligned[_QKV_W + mem_w:n_real], aligned[_QKV_W:_QKV_W + mem_w]], axis=0)
    real = real.reshape(N_DEV, lay.in_shard, D)
    return jnp.pad(real, ((0, 0), (0, lay.in_pad - lay.in_shard), (0, 0))).reshape(N_DEV * lay.in_pad, D)


def _unpack_layer(lay, gathered, own=None):
    D = lay.D
    w = {k: lay.take(gathered, k, own) for k in ("g1", "u1", "d1", "kv", "out", "g2", "u2", "d2")}
    w["win"] = _align_win(lay, lay.take(gathered, "win", own))
    fl0 = N_DEV * lay.in_shard - N_FOX_HEADS
    w["wfl"] = w["win"][fl0:fl0 + LANE]
    gate = lay.take(gathered, "gate", own)
    w["gate"] = gate
    w["gate3"] = [gate[i * D:(i + 1) * D] for i in range(3)]
    br = lay.take(gathered, "br", own).reshape(N_DEV * lay.br_shard, lay.br_in)
    third = lay.br_in // 3
    w["br3"] = [br[:, i * third:(i + 1) * third] for i in range(3)]
    return w


def _silu_mul(accs, _):
    a, b = accs
    return [a, b, a * jax.nn.sigmoid(a) * b]


def _act_bwd(accs, extras):
    ds, (a, b) = accs[0], extras
    sig = jax.nn.sigmoid(a)
    return [ds * b * (sig * (1.0 + a * (1.0 - sig))), ds * (a * sig)]


def _res_norm(scale):
    def epilogue(accs, extras):
        f, (res, g) = accs[0], extras
        return [f, res + scale * ((f * _rstd(f)) * g)]
    return epilogue


def _norm_bwd(accs, extras):
    dy, (x, res, g) = _sum_accs(accs, None)[0], extras
    r = _rstd(x)
    xhat = x * r
    gy = dy * g
    dx = res + r * (gy - xhat * jnp.mean(gy * xhat, axis=-1, keepdims=True))
    part = jnp.sum(dy * xhat, axis=0, keepdims=True)
    first = lax.broadcasted_iota(jnp.int32, (8, part.shape[1]), 0) == 0
    return [dx, jnp.where(first, part, 0.0)]


def _ffn_fwd(tag, h, pre_g, post_g, wg, wu, wd):
    D = h.shape[1]
    n = _rms_fwd("ffn_norm_" + tag, h, pre_g, BF16)
    a, b, s = _mm("ffn_up_" + tag, [(n, wg), (n, wu)], "nt", [F32, F32, BF16], _silu_mul, tn=1408)
    f, out = _mm("ffn_down_" + tag, [(s, wd)], "nn", [F32, F32], _res_norm(0.5),
                 [(h, 0), (post_g.reshape(1, D), 0)], tn=D)
    return out, (h, n, a, b, s, f)


def _ffn_bwd(tag, dh, saved, pre_g, post_g, wg, wu, wd):
    h, n, a, b, s, f = saved
    D = h.shape[1]
    df, d_post = _rms_bwd("ffn_dout_" + tag, f, post_g, dh, BF16, scale=0.5)
    da, db = _mm("ffn_dact_" + tag, [(df, wd)], "nt", [BF16, BF16], _act_bwd, [(a, 0), (b, 0)], tn=1408)
    d_wd = _mm("ffn_dwd_" + tag, [(s, df)], "tn", [BF16], tm=256)
    dh_in, d_pre_rows = _mm("ffn_dn_" + tag, [(da, wg), (db, wu)], "nn", [F32, F32], _norm_bwd,
                            [(h, 0), (dh, 0), (pre_g.reshape(1, D), 0)], tm=256, tn=D, out_rows=[None, 8])
    d_pre = _colsum("ffn_dpre_" + tag, d_pre_rows)
    d_wg = _mm("ffn_dwg_" + tag, [(da, n)], "tn", [BF16], tm=256)
    d_wu = _mm("ffn_dwu_" + tag, [(db, n)], "tn", [BF16], tm=256)
    return dh_in, d_pre, d_post, d_wg, d_wu, d_wd


_SB_W = N_SB_HEADS * HEAD_DIM
_FOX_W = N_FOX_HEADS * HEAD_DIM
_QKV_W = 3 * _SB_W + 3 * _FOX_W


def _gate_act(accs, extras):
    return [jax.nn.sigmoid(accs[0] + extras[0])]


def _merge(accs, extras):
    return [extras[0] * accs[0] + extras[1] * accs[1] + extras[2] * accs[2]]


def _merge_bwd(accs, extras):
    dm = accs[0]
    d_branch = [dm * gi for gi in extras]
    d_gate = [dm * bi * gi * (1.0 - gi) for bi, gi in zip(accs[1:], extras)]
    return d_branch + d_gate


def _mix_tiles(lay):
    sb, fx = _SB_W // LANE, _FOX_W // LANE
    mem_w = N_DEV * lay.in_shard - _QKV_W - N_FOX_HEADS
    return (0, sb, 2 * sb, sb), (3 * sb, 3 * sb + fx, 3 * sb + 2 * fx, fx), (_QKV_W // LANE, mem_w // LANE)


def _mix_fwd(lay, h, w, pre_g, post_g, b_forget, b_gate, mem_n):
    D = lay.D
    (sq, sk, sv, sn), (fq, fk, fv, fn), (mq, mn) = _mix_tiles(lay)
    mem_d = mn * LANE // N_MEM_HEADS
    u = _rms_fwd("mix_norm", h, pre_g, BF16)
    proj = _mm("mix_in", [(u, w["win"])], "nt", [BF16], tm=1024, tn=IN_TILE)
    fl = _mm("mix_fl", [(u, w["wfl"])], "nt", [F32])[:, :N_FOX_HEADS].T
    c = _decay_fwd(fl, b_forget.reshape(-1, 1))
    o_sb, rtot = _sb_fwd((proj, sq), (proj, sk), (proj, sv), sn, HEAD_DIM, HEAD_DIM ** -0.5)
    o_fx32, o_fx, lse_fx = _attn_fwd("fox_fwd", (proj, fq), (proj, fk), (proj, fv), fn, HEAD_DIM,
                                     HEAD_DIM ** -0.5, c)
    kvm = _mm("mem_kv", [(mem_n, w["kv"])], "nn", [BF16])
    o_mem32, o_mem, lse_mem = _attn_fwd("mem_fwd", (proj, mq), (kvm, 0), (kvm, mn), mn, mem_d, mem_d ** -0.5)
    gates = _mm("mix_gate", [(u, w["gate"])], "nt", [F32], _gate_act, [(b_gate.reshape(1, -1), 0)], tm=1024)
    flat = [o_sb, o_fx, o_mem]
    merged = _mm("mix_merge", list(zip(flat, w["br3"])), "nt", [BF16], _merge,
                 [(gates, 0), (gates, D), (gates, 2 * D)])
    z, out = _mm("mix_out", [(merged, w["out"])], "nn", [F32, F32], _res_norm(1.0),
                 [(h, 0), (post_g.reshape(1, D), 0)], tn=D)
    saved = (h, u, proj, fl, c, rtot, o_fx32, lse_fx, kvm, o_mem32, lse_mem, gates, flat, merged, z)
    return out, saved


def _mix_bwd(lay, dh, saved, w, pre_g, post_g, b_forget, mem_n, dmem_n):
    D = lay.D
    (sq, sk, sv, sn), (fq, fk, fv, fn), (mq, mn) = _mix_tiles(lay)
    mem_d = mn * LANE // N_MEM_HEADS
    h, u, proj, fl, c, rtot, o_fx32, lse_fx, kvm, o_mem32, lse_mem, gates, flat, merged, z = saved
    dz, d_post = _rms_bwd("mix_dres", z, post_g, dh, BF16)
    outs = _mm("mix_dmerge", [(dz, w["out"])] + list(zip(flat, w["br3"])), "nt", [BF16] * 6, _merge_bwd,
               [(gates, 0), (gates, D), (gates, 2 * D)], tn=512)
    d_branch, d_gate = outs[:3], outs[3:]
    d_wout = _mm("mix_dwout", [(merged, dz)], "tn", [BF16])
    d_o = [_mm("mix_dbr%d" % i, [(d_branch[i], w["br3"][i])], "nn", [BF16]) for i in range(3)]
    d_wbr = [_mm("mix_dwbr%d" % i, [(d_branch[i], flat[i])], "tn", [BF16]) for i in range(3)]
    d_bgate = jnp.concatenate([_colsum("mix_dbgate%d" % i, d_gate[i]) for i in range(3)])
    d_wgate = [_mm("mix_dwgate%d" % i, [(d_gate[i], u)], "tn", [BF16]) for i in range(3)]

    d_sb = _sb_bwd((proj, sq), (proj, sk), (proj, sv), d_o[0], rtot, sn, HEAD_DIM, HEAD_DIM ** -0.5)
    *d_fx, dc, dc_rows = _attn_bwd("fox_bwd", (proj, fq), (proj, fk), (proj, fv), o_fx32, d_o[1], lse_fx, fn,
                                   HEAD_DIM, HEAD_DIM ** -0.5, c)
    dq_m, dk_m, dv_m = _attn_bwd("mem_bwd", (proj, mq), (kvm, 0), (kvm, mn), o_mem32, d_o[2], lse_mem, mn,
                                 mem_d, mem_d ** -0.5)
    dfl, d_bforget = _decay_bwd(dc, dc_rows, fl, b_forget.reshape(-1, 1))
    pieces = list(d_sb) + list(d_fx) + [dq_m]
    dflp = jnp.pad(dfl.T.astype(BF16), ((0, 0), (0, LANE - dfl.shape[0])))
    offs = [sum(t.shape[1] for t in pieces[:i]) for i in range(len(pieces) + 1)]
    win_rows = [w["win"][offs[i]:offs[i + 1]] for i in range(len(pieces))]
    du = _mm("mix_du", list(zip(d_gate, w["gate3"])) + list(zip(pieces, win_rows)) + [(dflp, w["wfl"])], "nn",
             [F32], _sum_accs, tm=256, tn=512)
    d_rows = [_mm("mix_dwin%d" % i, [(t, u)], "tn", [BF16]) for i, t in enumerate(pieces)]
    d_wfl = _mm("mix_dwfl", [(dflp, u)], "tn", [BF16])
    d_win = _unalign_win(lay, _pad_rows(jnp.concatenate(list(d_rows) + [d_wfl], axis=0), lay.in_cols))
    dh_in, d_pre = _rms_bwd("mix_dnorm", h, pre_g, du, F32, res=dh)

    dkvm = jnp.concatenate([dk_m, dv_m], axis=1)
    d_wkv = _mm("mem_dwkv", [(mem_n, dkvm)], "tn", [BF16])
    dmem_n = _mm("mem_dn", [(dkvm, w["kv"])], "nt", [F32], lambda accs, ex: [accs[0] + ex[0]], [(dmem_n, 0)])
    grads = {"win": d_win, "kv": d_wkv, "gate": jnp.concatenate(d_wgate, axis=0),
             "br": jnp.concatenate(d_wbr, axis=1), "out": d_wout}
    return dh_in, d_pre, d_post, d_bforget, d_bgate, grads, dmem_n


def _layer_fwd(lay, h, w, sp, mem_n):
    h1, s1 = _ffn_fwd("1", h, sp["ffn1_pre_g"], sp["ffn1_post_g"], w["g1"], w["u1"], w["d1"])
    h2, s2 = _mix_fwd(lay, h1, w, sp["mix_pre_g"], sp["mix_post_g"], sp["b_forget"], sp["b_gate"], mem_n)
    h3, s3 = _ffn_fwd("2", h2, sp["ffn2_pre_g"], sp["ffn2_post_g"], w["g2"], w["u2"], w["d2"])
    return h3, (s1, s2, s3)


def _layer_bwd(lay, dh, saved, w, sp, mem_n, dmem_n):
    s1, s2, s3 = saved
    dh, d_pre2, d_post2, d_g2, d_u2, d_d2 = _ffn_bwd("2", dh, s3, sp["ffn2_pre_g"], sp["ffn2_post_g"],
                                                     w["g2"], w["u2"], w["d2"])
    dh, d_mpre, d_mpost, d_bforget, d_bgate, g, dmem_n = _mix_bwd(
        lay, dh, s2, w, sp["mix_pre_g"], sp["mix_post_g"], sp["b_forget"], mem_n, dmem_n)
    dh, d_pre1, d_post1, d_g1, d_u1, d_d1 = _ffn_bwd("1", dh, s1, sp["ffn1_pre_g"], sp["ffn1_post_g"],
                                                     w["g1"], w["u1"], w["d1"])
    g.update({"g1": d_g1, "u1": d_u1, "d1": d_d1, "g2": d_g2, "u2": d_u2, "d2": d_d2})
    g["br"] = g["br"].reshape(N_DEV, lay.br_shard, lay.br_in).reshape(-1, lay.D)
    packed = jnp.concatenate([lay.spread(g[key], key) for key in lay.seg], axis=1)
    small = {"ffn1_pre_g": d_pre1, "ffn1_post_g": d_post1, "mix_pre_g": d_mpre, "mix_post_g": d_mpost,
             "ffn2_pre_g": d_pre2, "ffn2_post_g": d_post2, "b_gate": d_bgate, "b_forget": d_bforget}
    return dh, packed, small, dmem_n


_SHARDED = ["ffn1_w_gate", "ffn1_w_up", "ffn1_w_down", "w_in", "w_mem_kv", "w_gate", "w_br_sb", "w_br_fox",
            "w_br_mem", "w_out", "ffn2_w_gate", "ffn2_w_up", "ffn2_w_down"]
_SMALL_LAYER = ["ffn1_pre_g", "ffn1_post_g", "mix_pre_g", "mix_post_g", "ffn2_pre_g", "ffn2_post_g", "b_gate",
                "b_forget"]
_WEIGHTS = ["ffn1_pre_g", "ffn1_post_g", "ffn1_w_gate", "ffn1_w_up", "ffn1_w_down", "mix_pre_g", "mix_post_g",
            "w_in", "b_forget", "mem_norm_g", "w_mem_kv", "w_gate", "b_gate", "w_br_sb", "w_br_fox", "w_br_mem",
            "w_out", "ffn2_pre_g", "ffn2_post_g", "ffn2_w_gate", "ffn2_w_up", "ffn2_w_down"]


def _pack_small(vals, L, D):
    rows = []
    for l in range(L):
        for name in _SMALL_LAYER:
            t = vals[name][l]
            rows.append(jnp.pad(t, (0, -t.shape[0] % D)).reshape(-1, D))
    rows.append(vals["mem_norm_g"].reshape(1, D))
    packed = jnp.concatenate(rows, axis=0)
    return _pad_rows(packed, -(-packed.shape[0] // 8) * 8)


def _unpack_small(packed, shapes, L, D):
    out = {name: [] for name in _SMALL_LAYER}
    r = 0
    for l in range(L):
        for name in _SMALL_LAYER:
            n = shapes[name][1]
            nr = -(-n // D)
            out[name].append(packed[r:r + nr].reshape(-1)[:n])
            r += nr
    res = {name: jnp.stack(v) for name, v in out.items()}
    res["mem_norm_g"] = packed[r]
    return res


def _unpack_grads(lay, g, l_shapes):
    def seg(key):
        off, n = lay.seg[key]
        return g[off:off + n]
    br = seg("br").reshape(lay.br_shard, lay.br_in).T
    third = lay.br_in // 3
    return {
        "ffn1_w_gate": seg("g1").T, "ffn1_w_up": seg("u1").T, "ffn1_w_down": seg("d1"),
        "w_in": seg("win")[:lay.in_shard].T, "w_mem_kv": seg("kv"), "w_gate": seg("gate").T,
        "w_br_sb": br[:third], "w_br_fox": br[third:2 * third], "w_br_mem": br[2 * third:],
        "w_out": seg("out"), "ffn2_w_gate": seg("g2").T, "ffn2_w_up": seg("u2").T, "ffn2_w_down": seg("d2"),
    }


class _Exchanges:
    def gather(self, name, block):
        return _all_gather(name, block)

    def gather_start(self, block):
        return _exchange_start("ag_start", block, per_peer=False)

    def gather_wait(self, started, after):
        block, landed = _exchange_wait("ag_wait", started, after)
        return landed, (block, _my_index())

    def scatter(self, parts):
        return _reduce_scatter("w", parts)

    def scatter_start(self, parts):
        return _exchange_start("rs_start", parts, per_peer=True)

    def scatter_wait(self, started, after):
        parts, landed = _exchange_wait("rs_wait", started, after)
        return _sum_landed("rs_sum8", landed, parts)

    def token(self, started):
        return started[4]

    def loss_sum(self, part):
        return lax.psum(part, ("x", "y", "c"))


def _step(p, m, v, x, mem, tgt, ex):
    L, D = p["ffn1_pre_g"].shape
    lay = _Layout(D, p["ffn1_w_gate"].shape[2], p["w_in"].shape[2], p["w_mem_kv"].shape[1], p["w_gate"].shape[2],
                  3 * p["w_br_sb"].shape[1], p["w_br_sb"].shape[2], p["w_out"].shape[1])
    blocks = [_pack_layer(lay, l, p) for l in range(L)]
    sps = [{name: p[name][l] for name in _SMALL_LAYER} for l in range(L)]

    mem_n = _rms_fwd("mem_norm", mem, p["mem_norm_g"], BF16)
    gathered, own = ex.gather("ag_weights", blocks[0]), None
    h, saved, ws = x, [], []
    for l in range(L):
        if l + 1 < L:
            nxt, gathered = lax.optimization_barrier((blocks[l + 1], gathered))
            started = ex.gather_start(nxt)
            sp = _after(sps[l], "ffn1_pre_g", ex.token(started))
        else:
            sp = sps[l]
        ws.append(_unpack_layer(lay, gathered, own))
        h, s = _layer_fwd(lay, h, ws[l], sp, mem_n)
        saved.append(s)
        if l + 1 < L:
            gathered, own = ex.gather_wait(started, h)
    loss_part, dh = _loss_grad(h, tgt)
    loss = ex.loss_sum(loss_part)

    dmem_n = jnp.zeros(mem.shape, F32)
    big, small = [None] * L, {name: [None] * L for name in _SMALL_LAYER}
    flying, token = {}, None
    for l in reversed(range(L)):
        sp = sps[l] if token is None else _after(sps[l], "ffn2_post_g", token)
        dh, packed, sm, dmem_n = _layer_bwd(lay, dh, saved[l], ws[l], sp, mem_n, dmem_n)
        if l > 0:
            flying[l] = ex.scatter_start(packed)
            token = ex.token(flying[l])
        else:
            big[l] = _unpack_grads(lay, ex.scatter(packed), None)
        for name in _SMALL_LAYER:
            small[name][l] = sm[name]
    for l, started in flying.items():
        big[l] = _unpack_grads(lay, ex.scatter_wait(started, dh), None)
    _, d_memg = _rms_bwd("mem_dnorm", mem, p["mem_norm_g"], dmem_n, F32)

    small_g = {name: jnp.stack(vs) for name, vs in small.items()}
    small_g["mem_norm_g"] = d_memg
    small_names = _SMALL_LAYER + ["mem_norm_g"]
    shapes = {name: p[name].shape for name in small_names}
    g_all = ex.gather("ag_small", _pack_small(small_g, L, D))
    packs = [_pack_small({name: t[name] for name in small_names}, L, D) for t in (p, m, v)]
    res = [_unpack_small(t, shapes, L, D) for t in _adamw("adamw_small", g_all, *packs)]

    out = {kind: {} for kind in ("grad", "delta", "new_m", "new_v")}
    for name in small_names:
        for kind, r in zip(("grad", "delta", "new_m", "new_v"), res):
            out[kind][name] = r[name].reshape(p[name].shape)
    for name in _SHARDED:
        g = jnp.stack([big[l][name] for l in range(L)])
        shp = g.shape
        flat = lambda t: t.reshape(-1, shp[-1])
        r = _adamw("adamw_" + name, flat(g)[None], flat(p[name]), flat(m[name]), flat(v[name]))
        for kind, t in zip(("grad", "delta", "new_m", "new_v"), r):
            out[kind][name] = t.reshape(shp)
    return loss, dh, out


def kernel(x, mem, ffn1_pre_g, ffn1_post_g, ffn1_w_gate, ffn1_w_up, ffn1_w_down, mix_pre_g, mix_post_g, w_in, b_forget, mem_norm_g, w_mem_kv, w_gate, b_gate, w_br_sb, w_br_fox, w_br_mem, w_out, ffn2_pre_g, ffn2_post_g, ffn2_w_gate, ffn2_w_up, ffn2_w_down, loss_target, m_ffn1_pre_g, m_ffn1_post_g, m_ffn1_w_gate, m_ffn1_w_up, m_ffn1_w_down, m_mix_pre_g, m_mix_post_g, m_w_in, m_b_forget, m_mem_norm_g, m_w_mem_kv, m_w_gate, m_b_gate, m_w_br_sb, m_w_br_fox, m_w_br_mem, m_w_out, m_ffn2_pre_g, m_ffn2_post_g, m_ffn2_w_gate, m_ffn2_w_up, m_ffn2_w_down, v_ffn1_pre_g, v_ffn1_post_g, v_ffn1_w_gate, v_ffn1_w_up, v_ffn1_w_down, v_mix_pre_g, v_mix_post_g, v_w_in, v_b_forget, v_mem_norm_g, v_w_mem_kv, v_w_gate, v_b_gate, v_w_br_sb, v_w_br_fox, v_w_br_mem, v_w_out, v_ffn2_pre_g, v_ffn2_post_g, v_ffn2_w_gate, v_ffn2_w_up, v_ffn2_w_down):
    p = dict(zip(_WEIGHTS, (ffn1_pre_g, ffn1_post_g, ffn1_w_gate, ffn1_w_up, ffn1_w_down, mix_pre_g, mix_post_g, w_in, b_forget, mem_norm_g, w_mem_kv, w_gate, b_gate, w_br_sb, w_br_fox, w_br_mem, w_out, ffn2_pre_g, ffn2_post_g, ffn2_w_gate, ffn2_w_up, ffn2_w_down)))
    m = dict(zip(_WEIGHTS, (m_ffn1_pre_g, m_ffn1_post_g, m_ffn1_w_gate, m_ffn1_w_up, m_ffn1_w_down, m_mix_pre_g, m_mix_post_g, m_w_in, m_b_forget, m_mem_norm_g, m_w_mem_kv, m_w_gate, m_b_gate, m_w_br_sb, m_w_br_fox, m_w_br_mem, m_w_out, m_ffn2_pre_g, m_ffn2_post_g, m_ffn2_w_gate, m_ffn2_w_up, m_ffn2_w_down)))
    v = dict(zip(_WEIGHTS, (v_ffn1_pre_g, v_ffn1_post_g, v_ffn1_w_gate, v_ffn1_w_up, v_ffn1_w_down, v_mix_pre_g, v_mix_post_g, v_w_in, v_b_forget, v_mem_norm_g, v_w_mem_kv, v_w_gate, v_b_gate, v_w_br_sb, v_w_br_fox, v_w_br_mem, v_w_out, v_ffn2_pre_g, v_ffn2_post_g, v_ffn2_w_gate, v_ffn2_w_up, v_ffn2_w_down)))
    loss, dx, out = _step(p, m, v, x[0], mem[0], loss_target[0], _Exchanges())
    return (loss, dx[None], *[out["grad"][n] for n in _WEIGHTS], *[out["delta"][n] for n in _WEIGHTS],
            *[out["new_m"][n] for n in _WEIGHTS], *[out["new_v"][n] for n in _WEIGHTS])
```

```python
import functools
import math

import jax
import jax.numpy as jnp
from jax import lax
from jax.experimental import pallas as pl
from jax.experimental.pallas import tpu as pltpu

F32 = jnp.float32
BF16 = jnp.bfloat16

LANE = 128
SUBLANE_BF16 = 16
VMEM_LIMIT = 56 * 1024 * 1024
N_DEV = 8
MESH = pl.DeviceIdType.MESH
ANY = pl.BlockSpec(memory_space=pl.ANY)

RMS_EPS = 1e-6
HEAD_DIM = 64
N_SB_HEADS = 8
N_FOX_HEADS = 8
N_MEM_HEADS = 4
NEG = -1e30
ATT_TQ = 1024
ATT_TK = 256
DECAY_TK = 128
IN_TILE = 1280

ADAM_LR = 0.001
ADAM_B1 = 0.9
ADAM_B2 = 0.999
ADAM_EPS = 1e-08
ADAM_WD = 0.01
ADAM_STEP = 10


def _tile(n, target, mult=LANE):
    best = None
    for t in range(mult, min(n, target) + 1, mult):
        if n % t == 0:
            best = t
    return best if best is not None else n


def _cparams(sem):
    return pltpu.CompilerParams(dimension_semantics=sem, vmem_limit_bytes=VMEM_LIMIT)


_DIMS = {"nn": (((1,), (0,)), ((), ())), "nt": (((1,), (1,)), ((), ())), "tn": (((0,), (0,)), ((), ()))}


def _dot(a, b, mode="nn"):
    return lax.dot_general(a.astype(BF16), b.astype(BF16), _DIMS[mode], preferred_element_type=F32)


def _mm(name, pairs, mode, out_dtypes, epilogue=None, extras=(), tm=512, tn=1024, out_rows=None):
    a0, b0 = pairs[0]
    M = a0.shape[1] if mode == "tn" else a0.shape[0]
    N = b0.shape[0] if mode == "nt" else b0.shape[1]
    tm = _tile(M, tm)
    tn = _tile(N, tn)
    np_, ne, no = len(pairs), len(extras), len(out_dtypes)

    def body(*refs):
        a_refs, b_refs = refs[:np_], refs[np_:2 * np_]
        e_refs = refs[2 * np_:2 * np_ + ne]
        o_refs = refs[2 * np_ + ne:]
        accs = [_dot(a[...], b[...], mode) for a, b in zip(a_refs, b_refs)]
        outs = epilogue(accs, [e[...] for e in e_refs]) if epilogue is not None else accs
        for o, val in zip(o_refs, outs):
            o[...] = val.astype(o.dtype)

    in_specs = []
    for a, _ in pairs:
        if mode == "tn":
            in_specs.append(pl.BlockSpec((a.shape[0], tm), lambda j, i: (0, i)))
        else:
            in_specs.append(pl.BlockSpec((tm, a.shape[1]), lambda j, i: (i, 0)))
    for _, b in pairs:
        if mode == "nt":
            in_specs.append(pl.BlockSpec((tn, b.shape[1]), lambda j, i: (j, 0)))
        else:
            in_specs.append(pl.BlockSpec((b.shape[0], tn), lambda j, i: (0, j)))
    for e, off in extras:
        if e.shape[0] == 1:
            in_specs.append(pl.BlockSpec((1, tn), functools.partial(lambda j, i, o: (0, j + o), o=off // tn)))
        else:
            in_specs.append(pl.BlockSpec((tm, tn), functools.partial(lambda j, i, o: (i, j + o), o=off // tn)))
    rows = [tm if r is None else r for r in (out_rows or [None] * no)]
    out_specs = [pl.BlockSpec((r, tn), lambda j, i: (i, j)) for r in rows]
    outs = pl.pallas_call(
        body, name=name, grid=(N // tn, M // tm),
        in_specs=in_specs, out_specs=out_specs,
        out_shape=[jax.ShapeDtypeStruct((M // tm * r, N), dt) for r, dt in zip(rows, out_dtypes)],
        compiler_params=_cparams(("parallel", "parallel")),
    )(*[a for a, _ in pairs], *[b for _, b in pairs], *[e for e, _ in extras])
    return outs[0] if no == 1 else outs


def _sum_accs(accs, _):
    total = accs[0]
    for acc in accs[1:]:
        total = total + acc
    return [total]


def _rstd(x):
    return lax.rsqrt(jnp.mean(x * x, axis=-1, keepdims=True) + RMS_EPS)


def _rms_fwd(name, x, g, out_dtype, res=None, scale=1.0, tr=512):
    R, D = x.shape
    tr = _tile(R, tr, 8)
    has_res = res is not None

    def body(*refs):
        x_ref, g_ref = refs[:2]
        o_ref = refs[-1]
        xv = x_ref[...]
        y = (xv * _rstd(xv)) * g_ref[...]
        if has_res:
            y = refs[2][...] + scale * y
        o_ref[...] = y.astype(o_ref.dtype)

    row = pl.BlockSpec((tr, D), lambda i: (i, 0))
    gain = pl.BlockSpec((1, D), lambda i: (0, 0))
    return pl.pallas_call(
        body, name=name, grid=(R // tr,),
        in_specs=[row, gain] + ([row] if has_res else []), out_specs=row,
        out_shape=jax.ShapeDtypeStruct((R, D), out_dtype),
        compiler_params=_cparams(("parallel",)),
    )(x, g.reshape(1, D), *([res] if has_res else []))


def _rms_bwd(name, x, g, dy, out_dtype, scale=1.0, res=None, tr=512):
    R, D = x.shape
    tr = _tile(R, tr, 8)
    has_res = res is not None

    def body(*refs):
        x_ref, g_ref, dy_ref = refs[:3]
        dx_ref, dg_ref = refs[-2:]
        i = pl.program_id(0)
        xv = x_ref[...]
        xhat = xv * _rstd(xv)
        dyv = dy_ref[...].astype(F32) * scale
        gy = dyv * g_ref[...]
        dx = _rstd(xv) * (gy - xhat * jnp.mean(gy * xhat, axis=-1, keepdims=True))
        if has_res:
            dx = refs[3][...] + dx
        dx_ref[...] = dx.astype(dx_ref.dtype)
        part = jnp.sum(dyv * xhat, axis=0, keepdims=True)

        @pl.when(i == 0)
        def _():
            dg_ref[...] = part

        @pl.when(i > 0)
        def _():
            dg_ref[...] += part

    row = pl.BlockSpec((tr, D), lambda i: (i, 0))
    gain = pl.BlockSpec((1, D), lambda i: (0, 0))
    dx, dg = pl.pallas_call(
        body, name=name, grid=(R // tr,),
        in_specs=[row, gain, row] + ([row] if has_res else []), out_specs=[row, gain],
        out_shape=[jax.ShapeDtypeStruct((R, D), out_dtype), jax.ShapeDtypeStruct((1, D), F32)],
        compiler_params=_cparams(("arbitrary",)),
    )(x, g.reshape(1, D), dy, *([res] if has_res else []))
    return dx, dg[0]


def _loss_grad(y, tgt, tr=512):
    R, D = y.shape
    tr = _tile(R, tr, 8)

    def body(y_ref, t_ref, dy_ref, loss_ref):
        i = pl.program_id(0)
        d = y_ref[...] - t_ref[...]
        dy_ref[...] = d / D
        part = 0.5 * jnp.sum(jnp.mean(d * d, axis=-1, keepdims=True), axis=0, keepdims=True)
        tile = jnp.broadcast_to(part, loss_ref.shape)

        @pl.when(i == 0)
        def _():
            loss_ref[...] = tile

        @pl.when(i > 0)
        def _():
            loss_ref[...] += tile

    row = pl.BlockSpec((tr, D), lambda i: (i, 0))
    dy, loss = pl.pallas_call(
        body, name="loss_grad", grid=(R // tr,),
        in_specs=[row, row], out_specs=[row, pl.BlockSpec((8, LANE), lambda i: (0, 0))],
        out_shape=[jax.ShapeDtypeStruct((R, D), F32), jax.ShapeDtypeStruct((8, LANE), F32)],
        compiler_params=_cparams(("arbitrary",)),
    )(y, tgt)
    return loss[0, 0], dy


def _colsum(name, x, tr=512, tn=1024):
    R, N = x.shape
    tr, tn = _tile(R, tr, 8), _tile(N, tn)

    def body(x_ref, o_ref):
        i = pl.program_id(1)
        part = jnp.sum(x_ref[...].astype(F32), axis=0, keepdims=True)

        @pl.when(i == 0)
        def _():
            o_ref[...] = part

        @pl.when(i > 0)
        def _():
            o_ref[...] += part

    out = pl.pallas_call(
        body, name=name, grid=(N // tn, R // tr),
        in_specs=[pl.BlockSpec((tr, tn), lambda j, i: (i, j))], out_specs=pl.BlockSpec((1, tn), lambda j, i: (0, j)),
        out_shape=jax.ShapeDtypeStruct((1, N), F32),
        compiler_params=_cparams(("parallel", "arbitrary")),
    )(x)
    return out[0]


def _tri(tk, rel):
    j = lax.broadcasted_iota(jnp.int32, (tk, tk), 0)
    s = lax.broadcasted_iota(jnp.int32, (tk, tk), 1)
    return rel(j, s).astype(BF16)


def _dot_split(x, m, parts=2):
    total = None
    rem = x
    for _ in range(parts):
        piece = rem.astype(BF16)
        rem = rem - piece.astype(F32)
        term = jnp.dot(piece, m, preferred_element_type=F32)
        total = term if total is None else total + term
    return total


def _log_not_and_beta(z, mask):
    ln = -(jnp.maximum(z, 0.0) + jnp.log(1.0 + jnp.exp(-jnp.abs(z))))
    return (ln if mask is None else jnp.where(mask, ln, 0.0)), ln + z


def _att_tiles(T, Tk, causal):
    tq = min(ATT_TQ, T)
    tk = min(ATT_TK, tq if causal else Tk)
    return tq, tk, (tq if causal else Tk) // tk


def _key_base(j, tq):
    return j * tq if isinstance(j, int) else pl.multiple_of(j * tq, tq)


def _is_pow2(scale):
    return math.log2(scale).is_integer()


def _per_head(x, hpb, d):
    if hpb == 1:
        return [x]
    lane = lax.broadcasted_iota(jnp.int32, x.shape, 1)
    return [jnp.where((lane >= h * d) & (lane < (h + 1) * d), x, jnp.zeros_like(x)) for h in range(hpb)]


def _join_heads(xs, d):
    out = xs[-1]
    if len(xs) > 1:
        lane = lax.broadcasted_iota(jnp.int32, out.shape, 1)
        for h in reversed(range(len(xs) - 1)):
            out = jnp.where(lane < (h + 1) * d, xs[h], out)
    return out


def _lane_tile(rows, off, whole):
    if whole:
        return pl.BlockSpec((rows, LANE), lambda g, i: (0, off + g))
    return pl.BlockSpec((rows, LANE), lambda g, i: (i, off + g))


def _sb_fwd(q, k, v, n_tiles, d, scale):
    T = q[0].shape[0]
    hpb = LANE // d
    tq, tk, nsub = _att_tiles(T, T, True)
    assert _is_pow2(scale)

    def body(q_ref, k_ref, v_ref, ob_ref, rt_ref, acc_ref, r_ref):
        qi = pl.program_id(1)
        qh = _per_head(q_ref[...] * scale, hpb, d)
        acc_ref[...] = jnp.zeros_like(acc_ref)
        r_ref[...] = jnp.zeros_like(r_ref)
        row = lax.broadcasted_iota(jnp.int32, (tq, tk), 0)
        col = lax.broadcasted_iota(jnp.int32, (tq, tk), 1)
        after = _tri(tk, lambda j, s: j > s)

        def walk(h, base, r0, r1, subs, diagonal):
            parts = []
            for u in subs:
                z = _dot(qh[h][r0:r1], k_ref[pl.ds(base + u * tk, tk), :], "nt")
                mask = (col[r0:r1] + u * tk) < row[r0:r1] if diagonal else None
                ln, lb = _log_not_and_beta(z, mask)
                between = _dot_split(ln, after, parts=1)
                first = ln[:, 0:1].astype(BF16).astype(F32)
                parts.append((u, lb, between, between[:, 0:1] + first, mask))
            r = r_ref[h, r0:r1, :]
            out = None
            for u, lb, between, total, mask in parts:
                w = jnp.exp(lb + between + r)
                if diagonal:
                    w = jnp.where(mask, w, 0.0)
                term = _dot(w, v_ref[pl.ds(base + u * tk, tk), :])
                out = term if out is None else out + term
                r = r + total
            acc_ref[h, r0:r1, :] += out
            r_ref[h, r0:r1, :] = r

        def step(j, diagonal):
            base = _key_base(j, tq)
            for h in range(hpb):
                if diagonal and nsub % 2 == 0:
                    walk(h, base, 0, tq // 2, range(nsub // 2 - 1, -1, -1), True)
                    walk(h, base, tq // 2, tq, range(nsub - 1, -1, -1), True)
                else:
                    walk(h, base, 0, tq, range(nsub - 1, -1, -1), diagonal)

        def below(i, carry):
            step(qi - 1 - i, False)
            return carry

        step(qi, True)
        lax.fori_loop(0, qi, below, 0)
        ob_ref[...] = _join_heads([acc_ref[h] for h in range(hpb)], d).astype(ob_ref.dtype)
        rt_ref[...] = r_ref[...]

    out = pl.BlockSpec((tq, LANE), lambda g, i: (i, g))
    col = pl.BlockSpec((hpb, tq, 1), lambda g, i: (g, i, 0))
    return pl.pallas_call(
        body, name="sb_fwd", grid=(n_tiles, T // tq),
        in_specs=[_lane_tile(tq, q[1], False), _lane_tile(T, k[1], True), _lane_tile(T, v[1], True)],
        out_specs=[out, col],
        out_shape=[jax.ShapeDtypeStruct((T, n_tiles * LANE), BF16), jax.ShapeDtypeStruct((n_tiles * hpb, T, 1), F32)],
        scratch_shapes=[pltpu.VMEM((hpb, tq, LANE), F32), pltpu.VMEM((hpb, tq, 1), F32)],
        compiler_params=_cparams(("parallel", "arbitrary")),
    )(q[0], k[0], v[0])


def _sb_bwd(q, k, v, do, rtot, n_tiles, d, scale):
    T = q[0].shape[0]
    hpb = LANE // d
    tq, tk, nsub = _att_tiles(T, T, True)
    assert _is_pow2(scale)

    def body(q_ref, k_ref, v_ref, do_ref, rt_ref, dq_ref, dk_ref, dv_ref, dk_acc, dv_acc, dq_acc, p_ref, c_ref):
        qi = pl.program_id(1)

        @pl.when(qi == 0)
        def _():
            dk_acc[...] = jnp.zeros_like(dk_acc)
            dv_acc[...] = jnp.zeros_like(dv_acc)

        qh = _per_head(q_ref[...] * scale, hpb, d)
        doh = _per_head(do_ref[...], hpb, d)
        dq_acc[...] = jnp.zeros_like(dq_acc)
        p_ref[...] = jnp.zeros_like(p_ref)
        c_ref[...] = jnp.zeros_like(c_ref)
        row = lax.broadcasted_iota(jnp.int32, (tq, tk), 0)
        col = lax.broadcasted_iota(jnp.int32, (tq, tk), 1)
        upto = _tri(tk, lambda j, s: j <= s)
        before = _tri(tk, lambda j, s: j < s)
        rt_wide = [jnp.broadcast_to(rt_ref[h], (tq, tk)) for h in range(hpb)]

        def step(j, diagonal):
            base = _key_base(j, tq)
            for h in range(hpb):
                first = []
                for u in range(nsub):
                    ks = base + u * tk
                    r0 = u * tk if diagonal else 0
                    kv = k_ref[pl.ds(ks, tk), :]
                    z = _dot(qh[h][r0:], kv, "nt")
                    mask = (col[r0:] + u * tk) < row[r0:] if diagonal else None
                    ln, lb = _log_not_and_beta(z, mask)
                    dw = _dot(doh[h][r0:], v_ref[pl.ds(ks, tk), :], "nt")
                    first.append((r0, ks, kv, mask, lb, jnp.exp(lb), _dot_split(ln, upto, parts=1), dw))
                rt, pre, cpre = rt_wide[h], p_ref[h], c_ref[h]
                dq = None
                for r0, ks, kv, mask, lb, sig, local, dw in first:
                    if diagonal and r0:
                        pre, cpre = pre[tk:], cpre[tk:]
                    prefix = local + pre
                    w = jnp.exp(lb + (rt[r0:] - prefix))
                    if diagonal:
                        w = jnp.where(mask, w, 0.0)
                    g = dw * w
                    c = _dot_split(g, before, parts=1) + cpre
                    dz = g * (1.0 - sig) - c * sig
                    if diagonal:
                        dz = jnp.where(mask, dz, 0.0)
                    term = _dot(dz, kv)
                    if diagonal:
                        dq_acc[h, r0:, :] += term
                    else:
                        dq = term if dq is None else dq + term
                    dk_acc[pl.ds(ks, tk), :] += _dot(dz, qh[h][r0:], "tn")
                    dv_acc[pl.ds(ks, tk), :] += _dot(w, doh[h][r0:], "tn")
                    pre = prefix[:, tk - 1:tk]
                    cpre = c[:, tk - 1:tk] + g[:, tk - 1:tk]
                if not diagonal:
                    dq_acc[h] += dq
                    p_ref[h] = pre
                    c_ref[h] = cpre

        def below(j, carry):
            step(j, False)
            return carry

        lax.fori_loop(0, qi, below, 0)
        step(qi, True)
        dq_ref[...] = (_join_heads([dq_acc[h] for h in range(hpb)], d) * scale).astype(dq_ref.dtype)

        @pl.when(qi == pl.num_programs(1) - 1)
        def _():
            dk_ref[...] = dk_acc[...].astype(dk_ref.dtype)
            dv_ref[...] = dv_acc[...].astype(dv_ref.dtype)

    blk = pl.BlockSpec((tq, LANE), lambda g, i: (i, g))
    full = pl.BlockSpec((T, LANE), lambda g, i: (0, g))
    col = pl.BlockSpec((hpb, tq, 1), lambda g, i: (g, i, 0))
    wide = jax.ShapeDtypeStruct((T, n_tiles * LANE), BF16)
    return pl.pallas_call(
        body, name="sb_bwd", grid=(n_tiles, T // tq),
        in_specs=[_lane_tile(tq, q[1], False), _lane_tile(T, k[1], True), _lane_tile(T, v[1], True), blk, col],
        out_specs=[blk, full, full], out_shape=[wide, wide, wide],
        scratch_shapes=[pltpu.VMEM((T, LANE), F32), pltpu.VMEM((T, LANE), F32), pltpu.VMEM((hpb, tq, LANE), F32),
                        pltpu.VMEM((hpb, tq, 1), F32), pltpu.VMEM((hpb, tq, 1), F32)],
        compiler_params=_cparams(("parallel", "arbitrary")),
    )(q[0], k[0], v[0], do, rtot)


def _attn_fwd(name, q, k, v, n_tiles, d, scale, c=None):
    T, Tk = q[0].shape[0], k[0].shape[0]
    hpb = LANE // d
    H = n_tiles * hpb
    causal = c is not None
    tq, tk, nsub = _att_tiles(T, Tk, causal)
    fold = _is_pow2(scale)

    def body(*refs):
        q_ref, k_ref, v_ref = refs[:3]
        cc_ref, cr_ref = refs[3:5] if causal else (None, None)
        o_ref, ob_ref, lse_ref, m_ref, l_ref, acc_ref = refs[-6:]
        qi = pl.program_id(1)
        qh = _per_head(q_ref[...] * scale if fold else q_ref[...], hpb, d)
        bias = [jnp.broadcast_to(cc_ref[h], (tq, tk)) for h in range(hpb)] if causal else None
        ones = jnp.ones((tk, LANE), BF16)
        m_ref[...] = jnp.full_like(m_ref, NEG)
        l_ref[...] = jnp.zeros_like(l_ref)
        acc_ref[...] = jnp.zeros_like(acc_ref)
        row = lax.broadcasted_iota(jnp.int32, (tq, tk), 0)
        col = lax.broadcasted_iota(jnp.int32, (tq, tk), 1)

        def absorb(h, j, base, r0, r1, subs, diagonal):
            zs = []
            for u in subs:
                z = _dot(qh[h][r0:r1], k_ref[pl.ds(base + u * tk, tk), :], "nt")
                if not fold:
                    z = z * scale
                if causal:
                    z = z + bias[h][r0:r1] - cr_ref[h, j * nsub + u]
                if diagonal:
                    z = jnp.where((col[r0:r1] + u * tk) <= row[r0:r1], z, NEG)
                zs.append(z)
            m_prev = m_ref[h, r0:r1, :]
            top = zs[0]
            for z in zs[1:]:
                top = jnp.maximum(top, z)
            m_new = jnp.maximum(m_prev, jnp.max(top, axis=1, keepdims=True))
            alpha = jnp.exp(m_prev - m_new)
            l_new = alpha * l_ref[h, r0:r1, :]
            out = alpha * acc_ref[h, r0:r1, :]
            m_wide = jnp.broadcast_to(m_new, top.shape)
            for u, z in zip(subs, zs):
                p = jnp.exp(z - m_wide).astype(BF16)
                l_new = l_new + jnp.dot(p, ones, preferred_element_type=F32)[:, 0:1]
                out = out + _dot(p, v_ref[pl.ds(base + u * tk, tk), :])
            l_ref[h, r0:r1, :] = l_new
            acc_ref[h, r0:r1, :] = out
            m_ref[h, r0:r1, :] = m_new

        def step(j, diagonal):
            base = _key_base(j, tq)
            for h in range(hpb):
                if diagonal and nsub % 2 == 0:
                    absorb(h, j, base, 0, tq // 2, range(nsub // 2), True)
                    absorb(h, j, base, tq // 2, tq, range(nsub), True)
                else:
                    absorb(h, j, base, 0, tq, range(nsub), diagonal)

        def below(j, carry):
            step(j, False)
            return carry

        if causal:
            lax.fori_loop(0, qi, below, 0)
            step(qi, True)
        else:
            step(0, False)
        o = _join_heads([acc_ref[h] / l_ref[h] for h in range(hpb)], d)
        o_ref[...] = o
        ob_ref[...] = o.astype(ob_ref.dtype)
        lse_ref[...] = m_ref[...] + jnp.log(l_ref[...])

    out = pl.BlockSpec((tq, LANE), lambda g, i: (i, g))
    col = pl.BlockSpec((hpb, tq, 1), lambda g, i: (g, i, 0))
    in_specs = [_lane_tile(tq, q[1], False), _lane_tile(Tk, k[1], True), _lane_tile(Tk, v[1], True)]
    args = [q[0], k[0], v[0]]
    if causal:
        in_specs += [col, pl.BlockSpec((hpb, T // tk, 1, tk), lambda g, i: (g, 0, 0, 0))]
        args += [c.reshape(H, T, 1), c.reshape(H, T // tk, 1, tk)]
    return pl.pallas_call(
        body, name=name, grid=(n_tiles, T // tq),
        in_specs=in_specs, out_specs=[out, out, col],
        out_shape=[jax.ShapeDtypeStruct((T, n_tiles * LANE), F32), jax.ShapeDtypeStruct((T, n_tiles * LANE), BF16),
                   jax.ShapeDtypeStruct((H, T, 1), F32)],
        scratch_shapes=[pltpu.VMEM((hpb, tq, 1), F32), pltpu.VMEM((hpb, tq, 1), F32),
                        pltpu.VMEM((hpb, tq, LANE), F32)],
        compiler_params=_cparams(("parallel", "arbitrary")),
    )(*args)


def _attn_bwd(name, q, k, v, o, do, lse, n_tiles, d, scale, c=None):
    T, Tk = q[0].shape[0], k[0].shape[0]
    hpb = LANE // d
    H = n_tiles * hpb
    causal = c is not None
    tq, tk, nsub = _att_tiles(T, Tk, causal)
    fold = _is_pow2(scale)

    def body(*refs):
        q_ref, k_ref, v_ref, o_ref, do_ref, lse_ref = refs[:6]
        cc_ref, cr_ref = refs[6:8] if causal else (None, None)
        n_out = 5 if causal else 3
        outs = refs[-(n_out + 3):-3]
        dq_ref, dk_ref, dv_ref = outs[:3]
        dc_ref, drow_ref = outs[3:5] if causal else (None, None)
        dk_acc, dv_acc, dq_acc = refs[-3:]
        qi = pl.program_id(1)

        @pl.when(qi == 0)
        def _():
            dk_acc[...] = jnp.zeros_like(dk_acc)
            dv_acc[...] = jnp.zeros_like(dv_acc)
            if causal:
                dc_ref[...] = jnp.zeros_like(dc_ref)

        qh = _per_head(q_ref[...] * scale if fold else q_ref[...], hpb, d)
        doh = _per_head(do_ref[...], hpb, d)
        delta_wide = [jnp.broadcast_to(jnp.sum(t.astype(F32) * o_ref[...], axis=1, keepdims=True), (tq, tk))
                      for t in doh]
        shift = [jnp.broadcast_to((cc_ref[h] - lse_ref[h]) if causal else -lse_ref[h], (tq, tk)) for h in range(hpb)]
        dq_acc[...] = jnp.zeros_like(dq_acc)
        if causal:
            drow_ref[...] = jnp.zeros_like(drow_ref)
        row = lax.broadcasted_iota(jnp.int32, (tq, tk), 0)
        col = lax.broadcasted_iota(jnp.int32, (tq, tk), 1)

        def step(j, diagonal):
            base = _key_base(j, tq)
            for h in range(hpb):
                dq, dsum = None, None
                for u in range(nsub):
                    ks = base + u * tk
                    r0 = u * tk if diagonal else 0
                    kv = k_ref[pl.ds(ks, tk), :]
                    z = _dot(qh[h][r0:], kv, "nt")
                    if not fold:
                        z = z * scale
                    z = z + shift[h][r0:]
                    if causal:
                        z = z - cr_ref[h, j * nsub + u]
                    if diagonal:
                        z = jnp.where((col[r0:] + u * tk) <= row[r0:], z, NEG)
                    p = jnp.exp(z)
                    ds = p * (_dot(doh[h][r0:], v_ref[pl.ds(ks, tk), :], "nt") - delta_wide[h][r0:])
                    term = _dot(ds, kv)
                    dk = _dot(ds, qh[h][r0:], "tn")
                    dk_acc[pl.ds(ks, tk), :] += dk if fold else dk * scale
                    dv_acc[pl.ds(ks, tk), :] += _dot(p, doh[h][r0:], "tn")
                    if causal:
                        dc_ref[h, j * nsub + u] -= jnp.sum(ds, axis=0, keepdims=True)
                    if diagonal:
                        dq_acc[h, r0:, :] += term
                        drow_ref[h, r0:, :] += jnp.sum(ds, axis=1, keepdims=True)
                    else:
                        dq = term if dq is None else dq + term
                        if causal:
                            dsum = ds if dsum is None else dsum + ds
                if not diagonal:
                    dq_acc[h] += dq
                    if causal:
                        drow_ref[h] += jnp.sum(dsum, axis=1, keepdims=True)

        def below(j, carry):
            step(j, False)
            return carry

        if causal:
            lax.fori_loop(0, qi, below, 0)
            step(qi, True)
        else:
            step(0, False)
        dq_ref[...] = (_join_heads([dq_acc[h] for h in range(hpb)], d) * scale).astype(dq_ref.dtype)

        @pl.when(qi == pl.num_programs(1) - 1)
        def _():
            dk_ref[...] = dk_acc[...].astype(dk_ref.dtype)
            dv_ref[...] = dv_acc[...].astype(dv_ref.dtype)

    blk = pl.BlockSpec((tq, LANE), lambda g, i: (i, g))
    full = pl.BlockSpec((Tk, LANE), lambda g, i: (0, g))
    col = pl.BlockSpec((hpb, tq, 1), lambda g, i: (g, i, 0))
    crow = pl.BlockSpec((hpb, T // tk, 1, tk), lambda g, i: (g, 0, 0, 0))
    in_specs = [_lane_tile(tq, q[1], False), _lane_tile(Tk, k[1], True), _lane_tile(Tk, v[1], True), blk, blk, col]
    args = [q[0], k[0], v[0], o, do, lse]
    out_specs = [blk, full, full]
    out_shape = [jax.ShapeDtypeStruct((T, n_tiles * LANE), BF16), jax.ShapeDtypeStruct((Tk, n_tiles * LANE), BF16),
                 jax.ShapeDtypeStruct((Tk, n_tiles * LANE), BF16)]
    if causal:
        in_specs += [col, crow]
        args += [c.reshape(H, T, 1), c.reshape(H, T // tk, 1, tk)]
        out_specs += [crow, col]
        out_shape += [jax.ShapeDtypeStruct((H, T // tk, 1, tk), F32), jax.ShapeDtypeStruct((H, T, 1), F32)]
    outs = pl.pallas_call(
        body, name=name, grid=(n_tiles, T // tq),
        in_specs=in_specs, out_specs=out_specs, out_shape=out_shape,
        scratch_shapes=[pltpu.VMEM((Tk, LANE), F32), pltpu.VMEM((Tk, LANE), F32), pltpu.VMEM((hpb, tq, LANE), F32)],
        compiler_params=_cparams(("parallel", "arbitrary")),
    )(*args)
    if causal:
        return outs[0], outs[1], outs[2], outs[3].reshape(H, T), outs[4].reshape(H, T)
    return outs


def _decay_fwd(fl, b):
    H, T = fl.shape
    tk = DECAY_TK

    def body(x_ref, b_ref, c_ref):
        upto = _tri(tk, lambda j, s: j <= s)
        carry = jnp.zeros((H, 1), F32)
        for i in range(T // tk):
            xv = x_ref[:, i * tk:(i + 1) * tk] + b_ref[...]
            lf = jnp.minimum(xv, 0.0) - jnp.log(1.0 + jnp.exp(-jnp.abs(xv)))
            pref = _dot_split(lf, upto, parts=3) + carry
            c_ref[:, i * tk:(i + 1) * tk] = pref
            carry = pref[:, tk - 1:tk]

    vm = pl.BlockSpec(memory_space=pltpu.VMEM)
    return pl.pallas_call(
        body, name="decay_fwd", in_specs=[vm, vm], out_specs=vm,
        out_shape=jax.ShapeDtypeStruct((H, T), F32),
    )(fl, b)


def _decay_bwd(dc_cols, dc_rows, fl, b):
    H, T = fl.shape
    tk = DECAY_TK

    def body(dc_ref, dr_ref, x_ref, b_ref, dx_ref, db_ref):
        from_ = _tri(tk, lambda j, s: j >= s)
        carry = jnp.zeros((H, 1), F32)
        total = jnp.zeros((H, 1), F32)
        for i in reversed(range(T // tk)):
            sl = slice(i * tk, (i + 1) * tk)
            suffix = _dot_split(dc_ref[:, sl] + dr_ref[:, sl], from_, parts=3) + carry
            xv = x_ref[:, sl] + b_ref[...]
            dx = suffix / (1.0 + jnp.exp(xv))
            dx_ref[:, sl] = dx
            total = total + jnp.sum(dx, axis=1, keepdims=True)
            carry = suffix[:, 0:1]
        db_ref[...] = jnp.broadcast_to(total, db_ref.shape)

    vm = pl.BlockSpec(memory_space=pltpu.VMEM)
    dx, db = pl.pallas_call(
        body, name="decay_bwd", in_specs=[vm, vm, vm, vm], out_specs=[vm, vm],
        out_shape=[jax.ShapeDtypeStruct((H, T), F32), jax.ShapeDtypeStruct((H, LANE), F32)],
    )(dc_cols, dc_rows, fl, b)
    return dx, db[:, 0]


def _place():
    x, y, c = lax.axis_index("x"), lax.axis_index("y"), lax.axis_index("c")
    return x, y, c, [(1 - x, y), (x, 1 - y), (1 - x, 1 - y)]


def _all_gather(name, block):
    R, C = block.shape

    def body(x_ref, out_ref, send_sems, recv_sems, local_sem):
        x, y, c, chips = _place()
        me, sibling = (x, y, c), (x, y, 1 - c)

        def rows(px, py, pc):
            return out_ref.at[4 * px + 2 * py + pc]

        def copy(k, blk, to, src=None):
            return pltpu.make_async_remote_copy(
                src_ref=rows(*blk) if src is None else src, dst_ref=rows(*blk),
                send_sem=send_sems.at[k], recv_sem=recv_sems.at[k], device_id=to, device_id_type=MESH)

        mine = pltpu.make_async_copy(x_ref, rows(*me), local_sem)
        mine.start()
        first = [copy(0, me, sibling, src=x_ref)]
        first += [copy(1 + j, me, (*chip, c), src=x_ref) for j, chip in enumerate(chips)]
        for cp in first:
            cp.start()
        passed = [copy(4 + j, (*chip, c), sibling) for j, chip in enumerate(chips)]
        for j, chip in enumerate(chips):
            copy(1 + j, (*chip, c), me).wait_recv()
            passed[j].start()
        copy(0, sibling, me).wait_recv()
        for j, chip in enumerate(chips):
            copy(4 + j, (*chip, 1 - c), me).wait_recv()
        for cp in first + passed:
            cp.wait_send()
        mine.wait()

    return pl.pallas_call(
        body, name=name, in_specs=[ANY], out_specs=ANY,
        out_shape=jax.ShapeDtypeStruct((N_DEV, R, C), block.dtype),
        scratch_shapes=[pltpu.SemaphoreType.DMA((7,)), pltpu.SemaphoreType.DMA((7,)), pltpu.SemaphoreType.DMA(())],
    )(block)


def _swap_with_sibling(name, parts):
    _, R, C = parts.shape

    def body(p_ref, out_ref, send_sems, recv_sems):
        x, y, c, _ = _place()
        copies = [pltpu.make_async_remote_copy(
            src_ref=p_ref.at[2 * q + (1 - c)], dst_ref=out_ref.at[q],
            send_sem=send_sems.at[q], recv_sem=recv_sems.at[q], device_id=(x, y, 1 - c), device_id_type=MESH)
            for q in range(4)]
        for cp in copies:
            cp.start()
        for cp in copies:
            cp.wait_recv()
        for cp in copies:
            cp.wait_send()

    return pl.pallas_call(
        body, name=name, in_specs=[ANY], out_specs=ANY,
        out_shape=jax.ShapeDtypeStruct((4, R, C), parts.dtype),
        scratch_shapes=[pltpu.SemaphoreType.DMA((4,)), pltpu.SemaphoreType.DMA((4,))],
    )(parts)


def _add_own(name, parts, got, tr=512):
    _, R, C = parts.shape
    tr = _tile(R, tr, SUBLANE_BF16)

    def body(c_ref, p_ref, g_ref, o_ref):
        o_ref[...] = (p_ref[...].astype(F32) + g_ref[...].astype(F32)).astype(o_ref.dtype)

    return pl.pallas_call(
        body, name=name,
        grid_spec=pltpu.PrefetchScalarGridSpec(
            num_scalar_prefetch=1, grid=(4, R // tr),
            in_specs=[pl.BlockSpec((1, tr, C), lambda q, i, c: (2 * q + c[0], i, 0)),
                      pl.BlockSpec((1, tr, C), lambda q, i, c: (q, i, 0))],
            out_specs=pl.BlockSpec((1, tr, C), lambda q, i, c: (q, i, 0))),
        out_shape=jax.ShapeDtypeStruct((4, R, C), parts.dtype),
        compiler_params=_cparams(("parallel", "parallel")),
    )(lax.axis_index("c").astype(jnp.int32).reshape(1), parts, got)


def _swap_with_chips(name, parts):
    _, R, C = parts.shape

    def body(p_ref, out_ref, send_sems, recv_sems, local_sem):
        x, y, c, chips = _place()
        my_chip = 2 * x + y
        mine = pltpu.make_async_copy(p_ref.at[my_chip], out_ref.at[my_chip], local_sem)
        mine.start()
        sends = [pltpu.make_async_remote_copy(
            src_ref=p_ref.at[2 * cx + cy], dst_ref=out_ref.at[my_chip],
            send_sem=send_sems.at[j], recv_sem=recv_sems.at[j], device_id=(cx, cy, c), device_id_type=MESH)
            for j, (cx, cy) in enumerate(chips)]
        for cp in sends:
            cp.start()
        for j, (cx, cy) in enumerate(chips):
            pltpu.make_async_remote_copy(
                src_ref=p_ref.at[my_chip], dst_ref=out_ref.at[2 * cx + cy],
                send_sem=send_sems.at[j], recv_sem=recv_sems.at[j], device_id=(cx, cy, c), device_id_type=MESH,
            ).wait_recv()
        for cp in sends:
            cp.wait_send()
        mine.wait()

    return pl.pallas_call(
        body, name=name, in_specs=[ANY], out_specs=ANY,
        out_shape=jax.ShapeDtypeStruct((4, R, C), parts.dtype),
        scratch_shapes=[pltpu.SemaphoreType.DMA((3,)), pltpu.SemaphoreType.DMA((3,)), pltpu.SemaphoreType.DMA(())],
    )(parts)


def _sum_parts(name, parts, tr=512):
    P, R, C = parts.shape
    tr = _tile(R, tr, SUBLANE_BF16)

    def body(p_ref, o_ref):
        total = p_ref[0].astype(F32)
        for p in range(1, P):
            total = total + p_ref[p].astype(F32)
        o_ref[...] = total

    return pl.pallas_call(
        body, name=name, grid=(R // tr,),
        in_specs=[pl.BlockSpec((P, tr, C), lambda i: (0, i, 0))], out_specs=pl.BlockSpec((tr, C), lambda i: (i, 0)),
        out_shape=jax.ShapeDtypeStruct((R, C), F32),
        compiler_params=_cparams(("parallel",)),
    )(parts)


_HBM = pl.BlockSpec(memory_space=pltpu.HBM)
_SEM = pl.BlockSpec(memory_space=pltpu.SEMAPHORE)
_EFFECT = pltpu.SideEffectType.DATAFLOW_SIDE_EFFECTING


def _flipped(x, y, c, k):
    px, py, pc = (1 - x if k & 4 else x), (1 - y if k & 2 else y), (1 - c if k & 1 else c)
    return (px, py, pc), 4 * px + 2 * py + pc


def _exchange_start(name, src, per_peer):
    R, C = src.shape[-2:]

    def body(v_ref, land_ref, send_sem, recv_sem, v_thru, land_thru, token):
        x, y, c = lax.axis_index("x"), lax.axis_index("y"), lax.axis_index("c")
        me = 4 * x + 2 * y + c
        for k in range(1, N_DEV):
            peer, idx = _flipped(x, y, c, k)
            pltpu.make_async_remote_copy(
                src_ref=v_ref.at[idx] if per_peer else v_ref, dst_ref=land_ref.at[me],
                send_sem=send_sem, recv_sem=recv_sem, device_id=peer, device_id_type=MESH).start()
        token[...] = jnp.zeros_like(token)

    return pl.pallas_call(
        body, name=name,
        out_shape=(pltpu.SemaphoreType.DMA(()), pltpu.SemaphoreType.DMA(()), pltpu.HBM(src.shape, src.dtype),
                   pltpu.HBM((N_DEV, R, C), src.dtype), jax.ShapeDtypeStruct((8, LANE), F32)),
        in_specs=(_HBM, _HBM), out_specs=(_SEM, _SEM, _HBM, _HBM, pl.BlockSpec(memory_space=pltpu.VMEM)),
        input_output_aliases={0: 2, 1: 3},
        compiler_params=pltpu.CompilerParams(has_side_effects=_EFFECT),
    )(pltpu.with_memory_space_constraint(src, pltpu.HBM),
      pltpu.with_memory_space_constraint(lax.empty((N_DEV, R, C), src.dtype), pltpu.HBM))


def _exchange_wait(name, started, after):
    send_sem, recv_sem, v_thru, land_thru, _ = started

    def body(v_ref, land_ref, send_sem, recv_sem, after_ref, v_dead, got_ref):
        x, y, c = lax.axis_index("x"), lax.axis_index("y"), lax.axis_index("c")
        seven = land_ref.at[pl.ds(0, N_DEV - 1)]
        drain = pltpu.make_async_remote_copy(
            src_ref=seven, dst_ref=seven, send_sem=send_sem, recv_sem=recv_sem,
            device_id=(x, y, c), device_id_type=MESH)
        drain.wait_send()
        drain.wait_recv()

    return pl.pallas_call(
        body, name=name,
        out_shape=(pltpu.HBM(v_thru.shape, v_thru.dtype), pltpu.HBM(land_thru.shape, land_thru.dtype)),
        in_specs=(_HBM, _HBM, _SEM, _SEM, ANY), out_specs=(_HBM, _HBM), input_output_aliases={0: 0, 1: 1},
        compiler_params=pltpu.CompilerParams(has_side_effects=_EFFECT),
    )(v_thru, land_thru, send_sem, recv_sem, after)


def _my_index():
    return 4 * lax.axis_index("x") + 2 * lax.axis_index("y") + lax.axis_index("c")


def _sum_landed(name, landed, parts, tr=512):
    P, R, C = landed.shape
    tr = _tile(R, tr, SUBLANE_BF16)

    def body(me_ref, l_ref, own_ref, o_ref):
        total = None
        for s in range(P):
            part = jnp.where(me_ref[0] == s, own_ref[0], l_ref[s]).astype(F32)
            total = part if total is None else total + part
        o_ref[...] = total

    return pl.pallas_call(
        body, name=name,
        grid_spec=pltpu.PrefetchScalarGridSpec(
            num_scalar_prefetch=1, grid=(R // tr,),
            in_specs=[pl.BlockSpec((P, tr, C), lambda i, me: (0, i, 0)),
                      pl.BlockSpec((1, tr, C), lambda i, me: (me[0], i, 0))],
            out_specs=pl.BlockSpec((tr, C), lambda i, me: (i, 0))),
        out_shape=jax.ShapeDtypeStruct((R, C), F32),
        compiler_params=_cparams(("parallel",)),
    )(_my_index().astype(jnp.int32).reshape(1), landed, parts)


def _after(params, name, token):
    return {**params, name: params[name] + token[0, 0]}


def _reduce_scatter(tag, parts):
    got = _swap_with_sibling("rs_pair_" + tag, parts)
    pair = _add_own("rs_add_" + tag, parts, got)
    quad = _swap_with_chips("rs_chips_" + tag, pair)
    return _sum_parts("rs_sum_" + tag, quad)


def _adamw(name, g_parts, w, m, v, tr=512):
    P, R, C = g_parts.shape
    tr = _tile(R, tr, 8)

    def body(g_ref, w_ref, m_ref, v_ref, go_ref, d_ref, mo_ref, vo_ref):
        g = g_ref[0]
        for p in range(1, P):
            g = g + g_ref[p]
        mn = ADAM_B1 * m_ref[...] + (1.0 - ADAM_B1) * g
        vn = ADAM_B2 * v_ref[...] + (1.0 - ADAM_B2) * (g * g)
        m_hat = mn / (1.0 - ADAM_B1 ** ADAM_STEP)
        v_hat = vn / (1.0 - ADAM_B2 ** ADAM_STEP)
        go_ref[...] = g
        d_ref[...] = -ADAM_LR * (m_hat / (jnp.sqrt(v_hat) + ADAM_EPS) + ADAM_WD * w_ref[...])
        mo_ref[...] = mn
        vo_ref[...] = vn

    row = pl.BlockSpec((tr, C), lambda i: (i, 0))
    return pl.pallas_call(
        body, name=name, grid=(R // tr,),
        in_specs=[pl.BlockSpec((P, tr, C), lambda i: (0, i, 0)), row, row, row], out_specs=[row] * 4,
        out_shape=[jax.ShapeDtypeStruct((R, C), F32)] * 4,
        compiler_params=_cparams(("parallel",)),
    )(g_parts, w, m, v)


def _pad_rows(t, rows):
    return jnp.pad(t, ((0, rows - t.shape[0]), (0, 0)))


class _Layout:
    def __init__(self, D, ff_shard, in_shard, kv_shard, gate_shard, br_in, br_shard, out_shard):
        self.D = D
        self.in_shard = in_shard
        self.in_pad = -(-in_shard // LANE) * LANE
        self.in_cols = -(-N_DEV * in_shard // IN_TILE) * IN_TILE
        self.br_in, self.br_shard = br_in, br_shard
        br_rows = br_shard * br_in // D
        sizes = [("g1", ff_shard), ("u1", ff_shard), ("d1", ff_shard), ("win", self.in_pad), ("kv", kv_shard),
                 ("gate", gate_shard), ("br", br_rows), ("out", out_shard),
                 ("g2", ff_shard), ("u2", ff_shard), ("d2", ff_shard)]
        self.seg, off = {}, 0
        for key, n in sizes:
            assert n % SUBLANE_BF16 == 0, (key, n)
            self.seg[key] = (off, n)
            off += n
        self.rows = off

    def pack(self, parts):
        return jnp.concatenate([parts[key] for key in self.seg], axis=0)

    def take(self, gathered, key, own=None):
        off, n = self.seg[key]
        seg = gathered[:, off:off + n, :]
        if own is not None:
            seg = lax.dynamic_update_slice(seg, own[0][off:off + n][None], (own[1], 0, 0))
        return seg.reshape(N_DEV * n, self.D)

    def spread(self, full, key):
        _, n = self.seg[key]
        return full.reshape(N_DEV, n, self.D)


def _pack_layer(lay, l, p):
    D = lay.D
    br = jnp.concatenate([p["w_br_sb"][l], p["w_br_fox"][l], p["w_br_mem"][l]], axis=0)
    parts = {
        "g1": p["ffn1_w_gate"][l].T, "u1": p["ffn1_w_up"][l].T, "d1": p["ffn1_w_down"][l],
        "win": _pad_rows(p["w_in"][l].T, lay.in_pad), "kv": p["w_mem_kv"][l], "gate": p["w_gate"][l].T,
        "br": br.T.reshape(-1, D), "out": p["w_out"][l],
        "g2": p["ffn2_w_gate"][l].T, "u2": p["ffn2_w_up"][l].T, "d2": p["ffn2_w_down"][l],
    }
    return lay.pack({k: t.astype(BF16) for k, t in parts.items()})


def _align_win(lay, packed):
    D = lay.D
    real = packed.reshape(N_DEV, lay.in_pad, D)[:, :lay.in_shard].reshape(N_DEV * lay.in_shard, D)
    rows = jnp.concatenate([real[:_QKV_W], real[_QKV_W + N_FOX_HEADS:], real[_QKV_W:_QKV_W + N_FOX_HEADS]], axis=0)
    return _pad_rows(rows, lay.in_cols)


def _unalign_win(lay, aligned):
    D = lay.D
    n_real = N_DEV * lay.in_shard
    mem_w = n_real - _QKV_W - N_FOX_HEADS
    real = jnp.concatenate([aligned[:_QKV_W], aligned[_QKV_W + mem_w:n_real], aligned[_QKV_W:_QKV_W + mem_w]], axis=0)
    real = real.reshape(N_DEV, lay.in_shard, D)
    return jnp.pad(real, ((0, 0), (0, lay.in_pad - lay.in_shard), (0, 0))).reshape(N_DEV * lay.in_pad, D)


def _unpack_layer(lay, gathered, own=None):
    D = lay.D
    w = {k: lay.take(gathered, k, own) for k in ("g1", "u1", "d1", "kv", "out", "g2", "u2", "d2")}
    w["win"] = _align_win(lay, lay.take(gathered, "win", own))
    fl0 = N_DEV * lay.in_shard - N_FOX_HEADS
    w["wfl"] = w["win"][fl0:fl0 + LANE]
    gate = lay.take(gathered, "gate", own)
    w["gate"] = gate
    w["gate3"] = [gate[i * D:(i + 1) * D] for i in range(3)]
    br = lay.take(gathered, "br", own).reshape(N_DEV * lay.br_shard, lay.br_in)
    third = lay.br_in // 3
    w["br3"] = [br[:, i * third:(i + 1) * third] for i in range(3)]
    return w


def _silu_mul(accs, _):
    a, b = accs
    return [a, b, a * jax.nn.sigmoid(a) * b]


def _act_bwd(accs, extras):
    ds, (a, b) = accs[0], [e.astype(F32) for e in extras]
    sig = jax.nn.sigmoid(a)
    return [ds * b * (sig * (1.0 + a * (1.0 - sig))), ds * (a * sig)]


def _res_norm(scale):
    def epilogue(accs, extras):
        f, (res, g) = accs[0], extras
        return [f, res + scale * ((f * _rstd(f)) * g)]
    return epilogue


def _norm_bwd(accs, extras):
    dy, (x, res, g) = _sum_accs(accs, None)[0], extras
    r = _rstd(x)
    xhat = x * r
    gy = dy * g
    dx = res + r * (gy - xhat * jnp.mean(gy * xhat, axis=-1, keepdims=True))
    part = jnp.sum(dy * xhat, axis=0, keepdims=True)
    first = lax.broadcasted_iota(jnp.int32, (8, part.shape[1]), 0) == 0
    return [dx, jnp.where(first, part, 0.0)]


def _ffn_fwd(tag, h, pre_g, post_g, wg, wu, wd):
    D = h.shape[1]
    n = _rms_fwd("ffn_norm_" + tag, h, pre_g, BF16)
    a, b, s = _mm("ffn_up_" + tag, [(n, wg), (n, wu)], "nt", [BF16, BF16, BF16], _silu_mul, tn=1408)
    f, out = _mm("ffn_down_" + tag, [(s, wd)], "nn", [F32, F32], _res_norm(0.5),
                 [(h, 0), (post_g.reshape(1, D), 0)], tn=D)
    return out, (h, n, a, b, s, f)


def _ffn_bwd(tag, dh, saved, pre_g, post_g, wg, wu, wd):
    h, n, a, b, s, f = saved
    D = h.shape[1]
    df, d_post = _rms_bwd("ffn_dout_" + tag, f, post_g, dh, BF16, scale=0.5)
    da, db = _mm("ffn_dact_" + tag, [(df, wd)], "nt", [BF16, BF16], _act_bwd, [(a, 0), (b, 0)], tn=1408)
    d_wd = _mm("ffn_dwd_" + tag, [(s, df)], "tn", [BF16], tm=256)
    dh_in, d_pre_rows = _mm("ffn_dn_" + tag, [(da, wg), (db, wu)], "nn", [F32, F32], _norm_bwd,
                            [(h, 0), (dh, 0), (pre_g.reshape(1, D), 0)], tm=256, tn=D, out_rows=[None, 8])
    d_pre = _colsum("ffn_dpre_" + tag, d_pre_rows)
    d_wg = _mm("ffn_dwg_" + tag, [(da, n)], "tn", [BF16], tm=256)
    d_wu = _mm("ffn_dwu_" + tag, [(db, n)], "tn", [BF16], tm=256)
    return dh_in, d_pre, d_post, d_wg, d_wu, d_wd


_SB_W = N_SB_HEADS * HEAD_DIM
_FOX_W = N_FOX_HEADS * HEAD_DIM
_QKV_W = 3 * _SB_W + 3 * _FOX_W


def _gate_act(accs, extras):
    return [jax.nn.sigmoid(accs[0] + extras[0])]


def _merge(accs, extras):
    g = [e.astype(F32) for e in extras]
    return [g[0] * accs[0] + g[1] * accs[1] + g[2] * accs[2]]


def _merge_bwd(accs, extras):
    dm = accs[0]
    g = [e.astype(F32) for e in extras]
    d_branch = [dm * gi for gi in g]
    d_gate = [dm * bi * gi * (1.0 - gi) for bi, gi in zip(accs[1:], g)]
    return d_branch + d_gate


def _mix_tiles(lay):
    sb, fx = _SB_W // LANE, _FOX_W // LANE
    mem_w = N_DEV * lay.in_shard - _QKV_W - N_FOX_HEADS
    return (0, sb, 2 * sb, sb), (3 * sb, 3 * sb + fx, 3 * sb + 2 * fx, fx), (_QKV_W // LANE, mem_w // LANE)


def _mix_fwd(lay, h, w, pre_g, post_g, b_forget, b_gate, mem_n):
    D = lay.D
    (sq, sk, sv, sn), (fq, fk, fv, fn), (mq, mn) = _mix_tiles(lay)
    mem_d = mn * LANE // N_MEM_HEADS
    u = _rms_fwd("mix_norm", h, pre_g, BF16)
    proj = _mm("mix_in", [(u, w["win"])], "nt", [BF16], tm=1024, tn=IN_TILE)
    fl = _mm("mix_fl", [(u, w["wfl"])], "nt", [F32])[:, :N_FOX_HEADS].T
    c = _decay_fwd(fl, b_forget.reshape(-1, 1))
    o_sb, rtot = _sb_fwd((proj, sq), (proj, sk), (proj, sv), sn, HEAD_DIM, HEAD_DIM ** -0.5)
    o_fx32, o_fx, lse_fx = _attn_fwd("fox_fwd", (proj, fq), (proj, fk), (proj, fv), fn, HEAD_DIM,
                                     HEAD_DIM ** -0.5, c)
    kvm = _mm("mem_kv", [(mem_n, w["kv"])], "nn", [BF16])
    o_mem32, o_mem, lse_mem = _attn_fwd("mem_fwd", (proj, mq), (kvm, 0), (kvm, mn), mn, mem_d, mem_d ** -0.5)
    gates = _mm("mix_gate", [(u, w["gate"])], "nt", [BF16], _gate_act, [(b_gate.reshape(1, -1), 0)], tm=1024)
    flat = [o_sb, o_fx, o_mem]
    merged = _mm("mix_merge", list(zip(flat, w["br3"])), "nt", [BF16], _merge,
                 [(gates, 0), (gates, D), (gates, 2 * D)])
    z, out = _mm("mix_out", [(merged, w["out"])], "nn", [F32, F32], _res_norm(1.0),
                 [(h, 0), (post_g.reshape(1, D), 0)], tn=D)
    saved = (h, u, proj, fl, c, rtot, o_fx32, lse_fx, kvm, o_mem32, lse_mem, gates, flat, merged, z)
    return out, saved


def _mix_bwd(lay, dh, saved, w, pre_g, post_g, b_forget, mem_n, dmem_n):
    D = lay.D
    (sq, sk, sv, sn), (fq, fk, fv, fn), (mq, mn) = _mix_tiles(lay)
    mem_d = mn * LANE // N_MEM_HEADS
    h, u, proj, fl, c, rtot, o_fx32, lse_fx, kvm, o_mem32, lse_mem, gates, flat, merged, z = saved
    dz, d_post = _rms_bwd("mix_dres", z, post_g, dh, BF16)
    outs = _mm("mix_dmerge", [(dz, w["out"])] + list(zip(flat, w["br3"])), "nt", [BF16] * 6, _merge_bwd,
               [(gates, 0), (gates, D), (gates, 2 * D)], tn=512)
    d_branch, d_gate = outs[:3], outs[3:]
    d_wout = _mm("mix_dwout", [(merged, dz)], "tn", [BF16])
    d_o = [_mm("mix_dbr%d" % i, [(d_branch[i], w["br3"][i])], "nn", [BF16]) for i in range(3)]
    d_wbr = [_mm("mix_dwbr%d" % i, [(d_branch[i], flat[i])], "tn", [BF16]) for i in range(3)]
    d_bgate = jnp.concatenate([_colsum("mix_dbgate%d" % i, d_gate[i]) for i in range(3)])
    d_wgate = [_mm("mix_dwgate%d" % i, [(d_gate[i], u)], "tn", [BF16]) for i in range(3)]

    d_sb = _sb_bwd((proj, sq), (proj, sk), (proj, sv), d_o[0], rtot, sn, HEAD_DIM, HEAD_DIM ** -0.5)
    *d_fx, dc, dc_rows = _attn_bwd("fox_bwd", (proj, fq), (proj, fk), (proj, fv), o_fx32, d_o[1], lse_fx, fn,
                                   HEAD_DIM, HEAD_DIM ** -0.5, c)
    dq_m, dk_m, dv_m = _attn_bwd("mem_bwd", (proj, mq), (kvm, 0), (kvm, mn), o_mem32, d_o[2], lse_mem, mn,
                                 mem_d, mem_d ** -0.5)
    dfl, d_bforget = _decay_bwd(dc, dc_rows, fl, b_forget.reshape(-1, 1))
    pieces = list(d_sb) + list(d_fx) + [dq_m]
    dflp = jnp.pad(dfl.T.astype(BF16), ((0, 0), (0, LANE - dfl.shape[0])))
    offs = [sum(t.shape[1] for t in pieces[:i]) for i in range(len(pieces) + 1)]
    win_rows = [w["win"][offs[i]:offs[i + 1]] for i in range(len(pieces))]
    du = _mm("mix_du", list(zip(d_gate, w["gate3"])) + list(zip(pieces, win_rows)) + [(dflp, w["wfl"])], "nn",
             [F32], _sum_accs, tm=256, tn=512)
    d_rows = [_mm("mix_dwin%d" % i, [(t, u)], "tn", [BF16]) for i, t in enumerate(pieces)]
    d_wfl = _mm("mix_dwfl", [(dflp, u)], "tn", [BF16])
    d_win = _unalign_win(lay, _pad_rows(jnp.concatenate(list(d_rows) + [d_wfl], axis=0), lay.in_cols))
    dh_in, d_pre = _rms_bwd("mix_dnorm", h, pre_g, du, F32, res=dh)

    dkvm = jnp.concatenate([dk_m, dv_m], axis=1)
    d_wkv = _mm("mem_dwkv", [(mem_n, dkvm)], "tn", [BF16])
    dmem_n = _mm("mem_dn", [(dkvm, w["kv"])], "nt", [F32], lambda accs, ex: [accs[0] + ex[0]], [(dmem_n, 0)])
    grads = {"win": d_win, "kv": d_wkv, "gate": jnp.concatenate(d_wgate, axis=0),
             "br": jnp.concatenate(d_wbr, axis=1), "out": d_wout}
    return dh_in, d_pre, d_post, d_bforget, d_bgate, grads, dmem_n


def _layer_fwd(lay, h, w, sp, mem_n):
    h1, s1 = _ffn_fwd("1", h, sp["ffn1_pre_g"], sp["ffn1_post_g"], w["g1"], w["u1"], w["d1"])
    h2, s2 = _mix_fwd(lay, h1, w, sp["mix_pre_g"], sp["mix_post_g"], sp["b_forget"], sp["b_gate"], mem_n)
    h3, s3 = _ffn_fwd("2", h2, sp["ffn2_pre_g"], sp["ffn2_post_g"], w["g2"], w["u2"], w["d2"])
    return h3, (s1, s2, s3)


def _layer_bwd(lay, dh, saved, w, sp, mem_n, dmem_n):
    s1, s2, s3 = saved
    dh, d_pre2, d_post2, d_g2, d_u2, d_d2 = _ffn_bwd("2", dh, s3, sp["ffn2_pre_g"], sp["ffn2_post_g"],
                                                     w["g2"], w["u2"], w["d2"])
    dh, d_mpre, d_mpost, d_bforget, d_bgate, g, dmem_n = _mix_bwd(
        lay, dh, s2, w, sp["mix_pre_g"], sp["mix_post_g"], sp["b_forget"], mem_n, dmem_n)
    dh, d_pre1, d_post1, d_g1, d_u1, d_d1 = _ffn_bwd("1", dh, s1, sp["ffn1_pre_g"], sp["ffn1_post_g"],
                                                     w["g1"], w["u1"], w["d1"])
    g.update({"g1": d_g1, "u1": d_u1, "d1": d_d1, "g2": d_g2, "u2": d_u2, "d2": d_d2})
    g["br"] = g["br"].reshape(N_DEV, lay.br_shard, lay.br_in).reshape(-1, lay.D)
    packed = jnp.concatenate([lay.spread(g[key], key) for key in lay.seg], axis=1)
    small = {"ffn1_pre_g": d_pre1, "ffn1_post_g": d_post1, "mix_pre_g": d_mpre, "mix_post_g": d_mpost,
             "ffn2_pre_g": d_pre2, "ffn2_post_g": d_post2, "b_gate": d_bgate, "b_forget": d_bforget}
    return dh, packed, small, dmem_n


_SHARDED = ["ffn1_w_gate", "ffn1_w_up", "ffn1_w_down", "w_in", "w_mem_kv", "w_gate", "w_br_sb", "w_br_fox",
            "w_br_mem", "w_out", "ffn2_w_gate", "ffn2_w_up", "ffn2_w_down"]
_SMALL_LAYER = ["ffn1_pre_g", "ffn1_post_g", "mix_pre_g", "mix_post_g", "ffn2_pre_g", "ffn2_post_g", "b_gate",
                "b_forget"]
_WEIGHTS = ["ffn1_pre_g", "ffn1_post_g", "ffn1_w_gate", "ffn1_w_up", "ffn1_w_down", "mix_pre_g", "mix_post_g",
            "w_in", "b_forget", "mem_norm_g", "w_mem_kv", "w_gate", "b_gate", "w_br_sb", "w_br_fox", "w_br_mem",
            "w_out", "ffn2_pre_g", "ffn2_post_g", "ffn2_w_gate", "ffn2_w_up", "ffn2_w_down"]


def _pack_small(vals, L, D):
    rows = []
    for l in range(L):
        for name in _SMALL_LAYER:
            t = vals[name][l]
            rows.append(jnp.pad(t, (0, -t.shape[0] % D)).reshape(-1, D))
    rows.append(vals["mem_norm_g"].reshape(1, D))
    packed = jnp.concatenate(rows, axis=0)
    return _pad_rows(packed, -(-packed.shape[0] // 8) * 8)


def _unpack_small(packed, shapes, L, D):
    out = {name: [] for name in _SMALL_LAYER}
    r = 0
    for l in range(L):
        for name in _SMALL_LAYER:
            n = shapes[name][1]
            nr = -(-n // D)
            out[name].append(packed[r:r + nr].reshape(-1)[:n])
            r += nr
    res = {name: jnp.stack(v) for name, v in out.items()}
    res["mem_norm_g"] = packed[r]
    return res


def _unpack_grads(lay, g, l_shapes):
    def seg(key):
        off, n = lay.seg[key]
        return g[off:off + n]
    br = seg("br").reshape(lay.br_shard, lay.br_in).T
    third = lay.br_in // 3
    return {
        "ffn1_w_gate": seg("g1").T, "ffn1_w_up": seg("u1").T, "ffn1_w_down": seg("d1"),
        "w_in": seg("win")[:lay.in_shard].T, "w_mem_kv": seg("kv"), "w_gate": seg("gate").T,
        "w_br_sb": br[:third], "w_br_fox": br[third:2 * third], "w_br_mem": br[2 * third:],
        "w_out": seg("out"), "ffn2_w_gate": seg("g2").T, "ffn2_w_up": seg("u2").T, "ffn2_w_down": seg("d2"),
    }


class _Exchanges:
    def gather(self, name, block):
        return _all_gather(name, block)

    def gather_start(self, block):
        return _exchange_start("ag_start", block, per_peer=False)

    def gather_wait(self, started, after):
        block, landed = _exchange_wait("ag_wait", started, after)
        return landed, (block, _my_index())

    def scatter(self, parts):
        return _reduce_scatter("w", parts)

    def scatter_start(self, parts):
        return _exchange_start("rs_start", parts, per_peer=True)

    def scatter_wait(self, started, after):
        parts, landed = _exchange_wait("rs_wait", started, after)
        return _sum_landed("rs_sum8", landed, parts)

    def token(self, started):
        return started[4]

    def loss_sum(self, part):
        return lax.psum(part, ("x", "y", "c"))


def _step(p, m, v, x, mem, tgt, ex):
    L, D = p["ffn1_pre_g"].shape
    lay = _Layout(D, p["ffn1_w_gate"].shape[2], p["w_in"].shape[2], p["w_mem_kv"].shape[1], p["w_gate"].shape[2],
                  3 * p["w_br_sb"].shape[1], p["w_br_sb"].shape[2], p["w_out"].shape[1])
    blocks = [_pack_layer(lay, l, p) for l in range(L)]
    sps = [{name: p[name][l] for name in _SMALL_LAYER} for l in range(L)]

    mem_n = _rms_fwd("mem_norm", mem, p["mem_norm_g"], BF16)
    gathered, own = ex.gather("ag_weights", blocks[0]), None
    h, saved, ws = x, [], []
    for l in range(L):
        if l + 1 < L:
            nxt, gathered = lax.optimization_barrier((blocks[l + 1], gathered))
            started = ex.gather_start(nxt)
            sp = _after(sps[l], "ffn1_pre_g", ex.token(started))
        else:
            sp = sps[l]
        ws.append(_unpack_layer(lay, gathered, own))
        h, s = _layer_fwd(lay, h, ws[l], sp, mem_n)
        saved.append(s)
        if l + 1 < L:
            gathered, own = ex.gather_wait(started, h)
    loss_part, dh = _loss_grad(h, tgt)
    loss = ex.loss_sum(loss_part)

    dmem_n = jnp.zeros(mem.shape, F32)
    big, small = [None] * L, {name: [None] * L for name in _SMALL_LAYER}
    flying, token = {}, None
    for l in reversed(range(L)):
        sp = sps[l] if token is None else _after(sps[l], "ffn2_post_g", token)
        dh, packed, sm, dmem_n = _layer_bwd(lay, dh, saved[l], ws[l], sp, mem_n, dmem_n)
        if l > 0:
            flying[l] = ex.scatter_start(packed)
            token = ex.token(flying[l])
        else:
            big[l] = _unpack_grads(lay, ex.scatter(packed), None)
        for name in _SMALL_LAYER:
            small[name][l] = sm[name]
    for l, started in flying.items():
        big[l] = _unpack_grads(lay, ex.scatter_wait(started, dh), None)
    _, d_memg = _rms_bwd("mem_dnorm", mem, p["mem_norm_g"], dmem_n, F32)

    small_g = {name: jnp.stack(vs) for name, vs in small.items()}
    small_g["mem_norm_g"] = d_memg
    small_names = _SMALL_LAYER + ["mem_norm_g"]
    shapes = {name: p[name].shape for name in small_names}
    g_all = ex.gather("ag_small", _pack_small(small_g, L, D))
    packs = [_pack_small({name: t[name] for name in small_names}, L, D) for t in (p, m, v)]
    res = [_unpack_small(t, shapes, L, D) for t in _adamw("adamw_small", g_all, *packs)]

    out = {kind: {} for kind in ("grad", "delta", "new_m", "new_v")}
    for name in small_names:
        for kind, r in zip(("grad", "delta", "new_m", "new_v"), res):
            out[kind][name] = r[name].reshape(p[name].shape)
    for name in _SHARDED:
        g = jnp.stack([big[l][name] for l in range(L)])
        shp = g.shape
        flat = lambda t: t.reshape(-1, shp[-1])
        r = _adamw("adamw_" + name, flat(g)[None], flat(p[name]), flat(m[name]), flat(v[name]))
        for kind, t in zip(("grad", "delta", "new_m", "new_v"), r):
            out[kind][name] = t.reshape(shp)
    return loss, dh, out


def kernel(x, mem, ffn1_pre_g, ffn1_post_g, ffn1_w_gate, ffn1_w_up, ffn1_w_down, mix_pre_g, mix_post_g, w_in, b_forget, mem_norm_g, w_mem_kv, w_gate, b_gate, w_br_sb, w_br_fox, w_br_mem, w_out, ffn2_pre_g, ffn2_post_g, ffn2_w_gate, ffn2_w_up, ffn2_w_down, loss_target, m_ffn1_pre_g, m_ffn1_post_g, m_ffn1_w_gate, m_ffn1_w_up, m_ffn1_w_down, m_mix_pre_g, m_mix_post_g, m_w_in, m_b_forget, m_mem_norm_g, m_w_mem_kv, m_w_gate, m_b_gate, m_w_br_sb, m_w_br_fox, m_w_br_mem, m_w_out, m_ffn2_pre_g, m_ffn2_post_g, m_ffn2_w_gate, m_ffn2_w_up, m_ffn2_w_down, v_ffn1_pre_g, v_ffn1_post_g, v_ffn1_w_gate, v_ffn1_w_up, v_ffn1_w_down, v_mix_pre_g, v_mix_post_g, v_w_in, v_b_forget, v_mem_norm_g, v_w_mem_kv, v_w_gate, v_b_gate, v_w_br_sb, v_w_br_fox, v_w_br_mem, v_w_out, v_ffn2_pre_g, v_ffn2_post_g, v_ffn2_w_gate, v_ffn2_w_up, v_ffn2_w_down):
    p = dict(zip(_WEIGHTS, (ffn1_pre_g, ffn1_post_g, ffn1_w_gate, ffn1_w_up, ffn1_w_down, mix_pre_g, mix_post_g, w_in, b_forget, mem_norm_g, w_mem_kv, w_gate, b_gate, w_br_sb, w_br_fox, w_br_mem, w_out, ffn2_pre_g, ffn2_post_g, ffn2_w_gate, ffn2_w_up, ffn2_w_down)))
    m = dict(zip(_WEIGHTS, (m_ffn1_pre_g, m_ffn1_post_g, m_ffn1_w_gate, m_ffn1_w_up, m_ffn1_w_down, m_mix_pre_g, m_mix_post_g, m_w_in, m_b_forget, m_mem_norm_g, m_w_mem_kv, m_w_gate, m_b_gate, m_w_br_sb, m_w_br_fox, m_w_br_mem, m_w_out, m_ffn2_pre_g, m_ffn2_post_g, m_ffn2_w_gate, m_ffn2_w_up, m_ffn2_w_down)))
    v = dict(zip(_WEIGHTS, (v_ffn1_pre_g, v_ffn1_post_g, v_ffn1_w_gate, v_ffn1_w_up, v_ffn1_w_down, v_mix_pre_g, v_mix_post_g, v_w_in, v_b_forget, v_mem_norm_g, v_w_mem_kv, v_w_gate, v_b_gate, v_w_br_sb, v_w_br_fox, v_w_br_mem, v_w_out, v_ffn2_pre_g, v_ffn2_post_g, v_ffn2_w_gate, v_ffn2_w_up, v_ffn2_w_down)))
    loss, dx, out = _step(p, m, v, x[0], mem[0], loss_target[0], _Exchanges())
    return (loss, dx[None], *[out["grad"][n] for n in _WEIGHTS], *[out["delta"][n] for n in _WEIGHTS],
            *[out["new_m"][n] for n in _WEIGHTS], *[out["new_v"][n] for n in _WEIGHTS])
```

```python
import functools
import math

import jax
import jax.numpy as jnp
from jax import lax
from jax.experimental import pallas as pl
from jax.experimental.pallas import tpu as pltpu

F32 = jnp.float32
BF16 = jnp.bfloat16

LANE = 128
SUBLANE_BF16 = 16
VMEM_LIMIT = 56 * 1024 * 1024
N_DEV = 8
MESH = pl.DeviceIdType.MESH
ANY = pl.BlockSpec(memory_space=pl.ANY)

RMS_EPS = 1e-6
HEAD_DIM = 64
N_SB_HEADS = 8
N_FOX_HEADS = 8
N_MEM_HEADS = 4
NEG = -1e30
ATT_TQ = 1024
ATT_TK = 256
DECAY_TK = 128
IN_TILE = 1280

ADAM_LR = 0.001
ADAM_B1 = 0.9
ADAM_B2 = 0.999
ADAM_EPS = 1e-08
ADAM_WD = 0.01
ADAM_STEP = 10


def _tile(n, target, mult=LANE):
    best = None
    for t in range(mult, min(n, target) + 1, mult):
        if n % t == 0:
            best = t
    return best if best is not None else n


def _cparams(sem):
    return pltpu.CompilerParams(dimension_semantics=sem, vmem_limit_bytes=VMEM_LIMIT)


_DIMS = {"nn": (((1,), (0,)), ((), ())), "nt": (((1,), (1,)), ((), ())), "tn": (((0,), (0,)), ((), ()))}


def _dot(a, b, mode="nn"):
    return lax.dot_general(a.astype(BF16), b.astype(BF16), _DIMS[mode], preferred_element_type=F32)


def _mm(name, pairs, mode, out_dtypes, epilogue=None, extras=(), tm=512, tn=1024, out_rows=None):
    a0, b0 = pairs[0]
    M = a0.shape[1] if mode == "tn" else a0.shape[0]
    N = b0.shape[0] if mode == "nt" else b0.shape[1]
    tm = _tile(M, tm)
    tn = _tile(N, tn)
    np_, ne, no = len(pairs), len(extras), len(out_dtypes)

    def body(*refs):
        a_refs, b_refs = refs[:np_], refs[np_:2 * np_]
        e_refs = refs[2 * np_:2 * np_ + ne]
        o_refs = refs[2 * np_ + ne:]
        accs = [_dot(a[...], b[...], mode) for a, b in zip(a_refs, b_refs)]
        outs = epilogue(accs, [e[...] for e in e_refs]) if epilogue is not None else accs
        for o, val in zip(o_refs, outs):
            o[...] = val.astype(o.dtype)

    in_specs = []
    for a, _ in pairs:
        if mode == "tn":
            in_specs.append(pl.BlockSpec((a.shape[0], tm), lambda j, i: (0, i)))
        else:
            in_specs.append(pl.BlockSpec((tm, a.shape[1]), lambda j, i: (i, 0)))
    for _, b in pairs:
        if mode == "nt":
            in_specs.append(pl.BlockSpec((tn, b.shape[1]), lambda j, i: (j, 0)))
        else:
            in_specs.append(pl.BlockSpec((b.shape[0], tn), lambda j, i: (0, j)))
    for e, off in extras:
        if e.shape[0] == 1:
            in_specs.append(pl.BlockSpec((1, tn), functools.partial(lambda j, i, o: (0, j + o), o=off // tn)))
        else:
            in_specs.append(pl.BlockSpec((tm, tn), functools.partial(lambda j, i, o: (i, j + o), o=off // tn)))
    rows = [tm if r is None else r for r in (out_rows or [None] * no)]
    out_specs = [pl.BlockSpec((r, tn), lambda j, i: (i, j)) for r in rows]
    outs = pl.pallas_call(
        body, name=name, grid=(N // tn, M // tm),
        in_specs=in_specs, out_specs=out_specs,
        out_shape=[jax.ShapeDtypeStruct((M // tm * r, N), dt) for r, dt in zip(rows, out_dtypes)],
        compiler_params=_cparams(("parallel", "parallel")),
    )(*[a for a, _ in pairs], *[b for _, b in pairs], *[e for e, _ in extras])
    return outs[0] if no == 1 else outs


def _sum_accs(accs, _):
    total = accs[0]
    for acc in accs[1:]:
        total = total + acc
    return [total]


def _rstd(x):
    return lax.rsqrt(jnp.mean(x * x, axis=-1, keepdims=True) + RMS_EPS)


def _rms_fwd(name, x, g, out_dtype, tr=512):
    R, D = x.shape
    tr = _tile(R, tr, 8)

    def body(x_ref, g_ref, o_ref):
        xv = x_ref[...]
        o_ref[...] = ((xv * _rstd(xv)) * g_ref[...]).astype(o_ref.dtype)

    row = pl.BlockSpec((tr, D), lambda i: (i, 0))
    return pl.pallas_call(
        body, name=name, grid=(R // tr,),
        in_specs=[row, pl.BlockSpec((1, D), lambda i: (0, 0))], out_specs=row,
        out_shape=jax.ShapeDtypeStruct((R, D), out_dtype),
        compiler_params=_cparams(("parallel",)),
    )(x, g.reshape(1, D))


def _rms_bwd(name, x, g, dy, out_dtype, scale=1.0, res=None, tr=512):
    R, D = x.shape
    tr = _tile(R, tr, 8)
    has_res = res is not None

    def body(*refs):
        x_ref, g_ref, dy_ref = refs[:3]
        dx_ref, dg_ref = refs[-2:]
        i = pl.program_id(0)
        xv = x_ref[...]
        xhat = xv * _rstd(xv)
        dyv = dy_ref[...].astype(F32) * scale
        gy = dyv * g_ref[...]
        dx = _rstd(xv) * (gy - xhat * jnp.mean(gy * xhat, axis=-1, keepdims=True))
        if has_res:
            dx = refs[3][...] + dx
        dx_ref[...] = dx.astype(dx_ref.dtype)
        part = jnp.sum(dyv * xhat, axis=0, keepdims=True)

        @pl.when(i == 0)
        def _():
            dg_ref[...] = part

        @pl.when(i > 0)
        def _():
            dg_ref[...] += part

    row = pl.BlockSpec((tr, D), lambda i: (i, 0))
    gain = pl.BlockSpec((1, D), lambda i: (0, 0))
    dx, dg = pl.pallas_call(
        body, name=name, grid=(R // tr,),
        in_specs=[row, gain, row] + ([row] if has_res else []), out_specs=[row, gain],
        out_shape=[jax.ShapeDtypeStruct((R, D), out_dtype), jax.ShapeDtypeStruct((1, D), F32)],
        compiler_params=_cparams(("arbitrary",)),
    )(x, g.reshape(1, D), dy, *([res] if has_res else []))
    return dx, dg[0]


def _loss_grad(y, tgt, tr=512):
    R, D = y.shape
    tr = _tile(R, tr, 8)

    def body(y_ref, t_ref, dy_ref, loss_ref):
        i = pl.program_id(0)
        d = y_ref[...] - t_ref[...]
        dy_ref[...] = d / D
        part = 0.5 * jnp.sum(jnp.mean(d * d, axis=-1, keepdims=True), axis=0, keepdims=True)
        tile = jnp.broadcast_to(part, loss_ref.shape)

        @pl.when(i == 0)
        def _():
            loss_ref[...] = tile

        @pl.when(i > 0)
        def _():
            loss_ref[...] += tile

    row = pl.BlockSpec((tr, D), lambda i: (i, 0))
    dy, loss = pl.pallas_call(
        body, name="loss_grad", grid=(R // tr,),
        in_specs=[row, row], out_specs=[row, pl.BlockSpec((8, LANE), lambda i: (0, 0))],
        out_shape=[jax.ShapeDtypeStruct((R, D), F32), jax.ShapeDtypeStruct((8, LANE), F32)],
        compiler_params=_cparams(("arbitrary",)),
    )(y, tgt)
    return loss[0, 0], dy


def _colsum(name, x, tr=512, tn=1024):
    R, N = x.shape
    tr, tn = _tile(R, tr, 8), _tile(N, tn)

    def body(x_ref, o_ref):
        i = pl.program_id(1)
        part = jnp.sum(x_ref[...].astype(F32), axis=0, keepdims=True)

        @pl.when(i == 0)
        def _():
            o_ref[...] = part

        @pl.when(i > 0)
        def _():
            o_ref[...] += part

    out = pl.pallas_call(
        body, name=name, grid=(N // tn, R // tr),
        in_specs=[pl.BlockSpec((tr, tn), lambda j, i: (i, j))], out_specs=pl.BlockSpec((1, tn), lambda j, i: (0, j)),
        out_shape=jax.ShapeDtypeStruct((1, N), F32),
        compiler_params=_cparams(("parallel", "arbitrary")),
    )(x)
    return out[0]


def _tri(tk, rel):
    j = lax.broadcasted_iota(jnp.int32, (tk, tk), 0)
    s = lax.broadcasted_iota(jnp.int32, (tk, tk), 1)
    return rel(j, s).astype(BF16)


def _dot_split(x, m, parts=2):
    total = None
    rem = x
    for _ in range(parts):
        piece = rem.astype(BF16)
        rem = rem - piece.astype(F32)
        term = jnp.dot(piece, m, preferred_element_type=F32)
        total = term if total is None else total + term
    return total


def _log_not_and_beta(z, mask):
    ln = -(jnp.maximum(z, 0.0) + jnp.log(1.0 + jnp.exp(-jnp.abs(z))))
    return (ln if mask is None else jnp.where(mask, ln, 0.0)), ln + z


def _att_tiles(T, Tk, causal):
    tq = min(ATT_TQ, T)
    tk = min(ATT_TK, tq if causal else Tk)
    return tq, tk, (tq if causal else Tk) // tk


def _key_base(j, tq):
    return j * tq if isinstance(j, int) else pl.multiple_of(j * tq, tq)


def _is_pow2(scale):
    return math.log2(scale).is_integer()


def _per_head(x, hpb, d):
    if hpb == 1:
        return [x]
    lane = lax.broadcasted_iota(jnp.int32, x.shape, 1)
    return [jnp.where((lane >= h * d) & (lane < (h + 1) * d), x, jnp.zeros_like(x)) for h in range(hpb)]


def _join_heads(xs, d):
    out = xs[-1]
    if len(xs) > 1:
        lane = lax.broadcasted_iota(jnp.int32, out.shape, 1)
        for h in reversed(range(len(xs) - 1)):
            out = jnp.where(lane < (h + 1) * d, xs[h], out)
    return out


def _lane_tile(rows, off, whole):
    if whole:
        return pl.BlockSpec((rows, LANE), lambda g, i: (0, off + g))
    return pl.BlockSpec((rows, LANE), lambda g, i: (i, off + g))


def _sb_fwd(q, k, v, n_tiles, d, scale):
    T = q[0].shape[0]
    hpb = LANE // d
    tq, tk, nsub = _att_tiles(T, T, True)
    assert _is_pow2(scale)

    def body(q_ref, k_ref, v_ref, ob_ref, rt_ref, acc_ref, r_ref):
        qi = pl.program_id(1)
        qh = _per_head(q_ref[...] * scale, hpb, d)
        acc_ref[...] = jnp.zeros_like(acc_ref)
        r_ref[...] = jnp.zeros_like(r_ref)
        row = lax.broadcasted_iota(jnp.int32, (tq, tk), 0)
        col = lax.broadcasted_iota(jnp.int32, (tq, tk), 1)
        after = _tri(tk, lambda j, s: j > s)

        def walk(h, base, r0, r1, subs, diagonal):
            parts = []
            for u in subs:
                z = _dot(qh[h][r0:r1], k_ref[pl.ds(base + u * tk, tk), :], "nt")
                mask = (col[r0:r1] + u * tk) < row[r0:r1] if diagonal else None
                ln, lb = _log_not_and_beta(z, mask)
                between = _dot_split(ln, after, parts=1)
                first = ln[:, 0:1].astype(BF16).astype(F32)
                parts.append((u, lb, between, between[:, 0:1] + first, mask))
            r = r_ref[h, r0:r1, :]
            out = None
            for u, lb, between, total, mask in parts:
                w = jnp.exp(lb + between + r)
                if diagonal:
                    w = jnp.where(mask, w, 0.0)
                term = _dot(w, v_ref[pl.ds(base + u * tk, tk), :])
                out = term if out is None else out + term
                r = r + total
            acc_ref[h, r0:r1, :] += out
            r_ref[h, r0:r1, :] = r

        def step(j, diagonal):
            base = _key_base(j, tq)
            for h in range(hpb):
                if diagonal and nsub % 2 == 0:
                    walk(h, base, 0, tq // 2, range(nsub // 2 - 1, -1, -1), True)
                    walk(h, base, tq // 2, tq, range(nsub - 1, -1, -1), True)
                else:
                    walk(h, base, 0, tq, range(nsub - 1, -1, -1), diagonal)

        def below(i, carry):
            step(qi - 1 - i, False)
            return carry

        step(qi, True)
        lax.fori_loop(0, qi, below, 0)
        ob_ref[...] = _join_heads([acc_ref[h] for h in range(hpb)], d).astype(ob_ref.dtype)
        rt_ref[...] = r_ref[...]

    out = pl.BlockSpec((tq, LANE), lambda g, i: (i, g))
    col = pl.BlockSpec((hpb, tq, 1), lambda g, i: (g, i, 0))
    return pl.pallas_call(
        body, name="sb_fwd", grid=(n_tiles, T // tq),
        in_specs=[_lane_tile(tq, q[1], False), _lane_tile(T, k[1], True), _lane_tile(T, v[1], True)],
        out_specs=[out, col],
        out_shape=[jax.ShapeDtypeStruct((T, n_tiles * LANE), BF16), jax.ShapeDtypeStruct((n_tiles * hpb, T, 1), F32)],
        scratch_shapes=[pltpu.VMEM((hpb, tq, LANE), F32), pltpu.VMEM((hpb, tq, 1), F32)],
        compiler_params=_cparams(("parallel", "arbitrary")),
    )(q[0], k[0], v[0])


def _sb_bwd(q, k, v, do, rtot, n_tiles, d, scale):
    T = q[0].shape[0]
    hpb = LANE // d
    tq, tk, nsub = _att_tiles(T, T, True)
    assert _is_pow2(scale)

    def body(q_ref, k_ref, v_ref, do_ref, rt_ref, dq_ref, dk_ref, dv_ref, dk_acc, dv_acc, dq_acc, p_ref, c_ref):
        qi = pl.program_id(1)

        @pl.when(qi == 0)
        def _():
            dk_acc[...] = jnp.zeros_like(dk_acc)
            dv_acc[...] = jnp.zeros_like(dv_acc)

        qh = _per_head(q_ref[...] * scale, hpb, d)
        doh = _per_head(do_ref[...], hpb, d)
        dq_acc[...] = jnp.zeros_like(dq_acc)
        p_ref[...] = jnp.zeros_like(p_ref)
        c_ref[...] = jnp.zeros_like(c_ref)
        row = lax.broadcasted_iota(jnp.int32, (tq, tk), 0)
        col = lax.broadcasted_iota(jnp.int32, (tq, tk), 1)
        upto = _tri(tk, lambda j, s: j <= s)
        before = _tri(tk, lambda j, s: j < s)
        rt_wide = [jnp.broadcast_to(rt_ref[h], (tq, tk)) for h in range(hpb)]

        def step(j, diagonal):
            base = _key_base(j, tq)
            for h in range(hpb):
                first = []
                for u in range(nsub):
                    ks = base + u * tk
                    r0 = u * tk if diagonal else 0
                    kv = k_ref[pl.ds(ks, tk), :]
                    z = _dot(qh[h][r0:], kv, "nt")
                    mask = (col[r0:] + u * tk) < row[r0:] if diagonal else None
                    ln, lb = _log_not_and_beta(z, mask)
                    dw = _dot(doh[h][r0:], v_ref[pl.ds(ks, tk), :], "nt")
                    first.append((r0, ks, kv, mask, lb, jnp.exp(lb), _dot_split(ln, upto, parts=1), dw))
                rt, pre, cpre = rt_wide[h], p_ref[h], c_ref[h]
                dq = None
                for r0, ks, kv, mask, lb, sig, local, dw in first:
                    if diagonal and r0:
                        pre, cpre = pre[tk:], cpre[tk:]
                    prefix = local + pre
                    w = jnp.exp(lb + (rt[r0:] - prefix))
                    if diagonal:
                        w = jnp.where(mask, w, 0.0)
                    g = dw * w
                    c = _dot_split(g, before, parts=1) + cpre
                    dz = g * (1.0 - sig) - c * sig
                    if diagonal:
                        dz = jnp.where(mask, dz, 0.0)
                    term = _dot(dz, kv)
                    if diagonal:
                        dq_acc[h, r0:, :] += term
                    else:
                        dq = term if dq is None else dq + term
                    dk_acc[pl.ds(ks, tk), :] += _dot(dz, qh[h][r0:], "tn")
                    dv_acc[pl.ds(ks, tk), :] += _dot(w, doh[h][r0:], "tn")
                    pre = prefix[:, tk - 1:tk]
                    cpre = c[:, tk - 1:tk] + g[:, tk - 1:tk]
                if not diagonal:
                    dq_acc[h] += dq
                    p_ref[h] = pre
                    c_ref[h] = cpre

        def below(j, carry):
            step(j, False)
            return carry

        lax.fori_loop(0, qi, below, 0)
        step(qi, True)
        dq_ref[...] = (_join_heads([dq_acc[h] for h in range(hpb)], d) * scale).astype(dq_ref.dtype)

        @pl.when(qi == pl.num_programs(1) - 1)
        def _():
            dk_ref[...] = dk_acc[...].astype(dk_ref.dtype)
            dv_ref[...] = dv_acc[...].astype(dv_ref.dtype)

    blk = pl.BlockSpec((tq, LANE), lambda g, i: (i, g))
    full = pl.BlockSpec((T, LANE), lambda g, i: (0, g))
    col = pl.BlockSpec((hpb, tq, 1), lambda g, i: (g, i, 0))
    wide = jax.ShapeDtypeStruct((T, n_tiles * LANE), BF16)
    return pl.pallas_call(
        body, name="sb_bwd", grid=(n_tiles, T // tq),
        in_specs=[_lane_tile(tq, q[1], False), _lane_tile(T, k[1], True), _lane_tile(T, v[1], True), blk, col],
        out_specs=[blk, full, full], out_shape=[wide, wide, wide],
        scratch_shapes=[pltpu.VMEM((T, LANE), F32), pltpu.VMEM((T, LANE), F32), pltpu.VMEM((hpb, tq, LANE), F32),
                        pltpu.VMEM((hpb, tq, 1), F32), pltpu.VMEM((hpb, tq, 1), F32)],
        compiler_params=_cparams(("parallel", "arbitrary")),
    )(q[0], k[0], v[0], do, rtot)


def _attn_fwd(name, q, k, v, n_tiles, d, scale, c=None):
    T, Tk = q[0].shape[0], k[0].shape[0]
    hpb = LANE // d
    H = n_tiles * hpb
    causal = c is not None
    tq, tk, nsub = _att_tiles(T, Tk, causal)
    fold = _is_pow2(scale)

    def body(*refs):
        q_ref, k_ref, v_ref = refs[:3]
        cc_ref, cr_ref = refs[3:5] if causal else (None, None)
        o_ref, ob_ref, lse_ref, m_ref, l_ref, acc_ref = refs[-6:]
        qi = pl.program_id(1)
        qh = _per_head(q_ref[...] * scale if fold else q_ref[...], hpb, d)
        bias = [jnp.broadcast_to(cc_ref[h], (tq, tk)) for h in range(hpb)] if causal else None
        ones = jnp.ones((tk, LANE), BF16)
        m_ref[...] = jnp.full_like(m_ref, NEG)
        l_ref[...] = jnp.zeros_like(l_ref)
        acc_ref[...] = jnp.zeros_like(acc_ref)
        row = lax.broadcasted_iota(jnp.int32, (tq, tk), 0)
        col = lax.broadcasted_iota(jnp.int32, (tq, tk), 1)

        def absorb(h, j, base, r0, r1, subs, diagonal):
            zs = []
            for u in subs:
                z = _dot(qh[h][r0:r1], k_ref[pl.ds(base + u * tk, tk), :], "nt")
                if not fold:
                    z = z * scale
                if causal:
                    z = z + bias[h][r0:r1] - cr_ref[h, j * nsub + u]
                if diagonal:
                    z = jnp.where((col[r0:r1] + u * tk) <= row[r0:r1], z, NEG)
                zs.append(z)
            m_prev = m_ref[h, r0:r1, :]
            top = zs[0]
            for z in zs[1:]:
                top = jnp.maximum(top, z)
            m_new = jnp.maximum(m_prev, jnp.max(top, axis=1, keepdims=True))
            alpha = jnp.exp(m_prev - m_new)
            l_new = alpha * l_ref[h, r0:r1, :]
            out = alpha * acc_ref[h, r0:r1, :]
            m_wide = jnp.broadcast_to(m_new, top.shape)
            for u, z in zip(subs, zs):
                p = jnp.exp(z - m_wide).astype(BF16)
                l_new = l_new + jnp.dot(p, ones, preferred_element_type=F32)[:, 0:1]
                out = out + _dot(p, v_ref[pl.ds(base + u * tk, tk), :])
            l_ref[h, r0:r1, :] = l_new
            acc_ref[h, r0:r1, :] = out
            m_ref[h, r0:r1, :] = m_new

        def step(j, diagonal):
            base = _key_base(j, tq)
            for h in range(hpb):
                if diagonal and nsub % 2 == 0:
                    absorb(h, j, base, 0, tq // 2, range(nsub // 2), True)
                    absorb(h, j, base, tq // 2, tq, range(nsub), True)
                else:
                    absorb(h, j, base, 0, tq, range(nsub), diagonal)

        def below(j, carry):
            step(j, False)
            return carry

        if causal:
            lax.fori_loop(0, qi, below, 0)
            step(qi, True)
        else:
            step(0, False)
        o = _join_heads([acc_ref[h] / l_ref[h] for h in range(hpb)], d)
        o_ref[...] = o
        ob_ref[...] = o.astype(ob_ref.dtype)
        lse_ref[...] = m_ref[...] + jnp.log(l_ref[...])

    out = pl.BlockSpec((tq, LANE), lambda g, i: (i, g))
    col = pl.BlockSpec((hpb, tq, 1), lambda g, i: (g, i, 0))
    in_specs = [_lane_tile(tq, q[1], False), _lane_tile(Tk, k[1], True), _lane_tile(Tk, v[1], True)]
    args = [q[0], k[0], v[0]]
    if causal:
        in_specs += [col, pl.BlockSpec((hpb, T // tk, 1, tk), lambda g, i: (g, 0, 0, 0))]
        args += [c.reshape(H, T, 1), c.reshape(H, T // tk, 1, tk)]
    return pl.pallas_call(
        body, name=name, grid=(n_tiles, T // tq),
        in_specs=in_specs, out_specs=[out, out, col],
        out_shape=[jax.ShapeDtypeStruct((T, n_tiles * LANE), F32), jax.ShapeDtypeStruct((T, n_tiles * LANE), BF16),
                   jax.ShapeDtypeStruct((H, T, 1), F32)],
        scratch_shapes=[pltpu.VMEM((hpb, tq, 1), F32), pltpu.VMEM((hpb, tq, 1), F32),
                        pltpu.VMEM((hpb, tq, LANE), F32)],
        compiler_params=_cparams(("parallel", "arbitrary")),
    )(*args)


def _attn_bwd(name, q, k, v, o, do, lse, n_tiles, d, scale, c=None):
    T, Tk = q[0].shape[0], k[0].shape[0]
    hpb = LANE // d
    H = n_tiles * hpb
    causal = c is not None
    tq, tk, nsub = _att_tiles(T, Tk, causal)
    fold = _is_pow2(scale)

    def body(*refs):
        q_ref, k_ref, v_ref, o_ref, do_ref, lse_ref = refs[:6]
        cc_ref, cr_ref = refs[6:8] if causal else (None, None)
        n_out = 5 if causal else 3
        outs = refs[-(n_out + 3):-3]
        dq_ref, dk_ref, dv_ref = outs[:3]
        dc_ref, drow_ref = outs[3:5] if causal else (None, None)
        dk_acc, dv_acc, dq_acc = refs[-3:]
        qi = pl.program_id(1)

        @pl.when(qi == 0)
        def _():
            dk_acc[...] = jnp.zeros_like(dk_acc)
            dv_acc[...] = jnp.zeros_like(dv_acc)
            if causal:
                dc_ref[...] = jnp.zeros_like(dc_ref)

        qh = _per_head(q_ref[...] * scale if fold else q_ref[...], hpb, d)
        doh = _per_head(do_ref[...], hpb, d)
        delta_wide = [jnp.broadcast_to(jnp.sum(t.astype(F32) * o_ref[...], axis=1, keepdims=True), (tq, tk))
                      for t in doh]
        shift = [jnp.broadcast_to((cc_ref[h] - lse_ref[h]) if causal else -lse_ref[h], (tq, tk)) for h in range(hpb)]
        dq_acc[...] = jnp.zeros_like(dq_acc)
        if causal:
            drow_ref[...] = jnp.zeros_like(drow_ref)
        row = lax.broadcasted_iota(jnp.int32, (tq, tk), 0)
        col = lax.broadcasted_iota(jnp.int32, (tq, tk), 1)

        def step(j, diagonal):
            base = _key_base(j, tq)
            for h in range(hpb):
                dq, dsum = None, None
                for u in range(nsub):
                    ks = base + u * tk
                    r0 = u * tk if diagonal else 0
                    kv = k_ref[pl.ds(ks, tk), :]
                    z = _dot(qh[h][r0:], kv, "nt")
                    if not fold:
                        z = z * scale
                    z = z + shift[h][r0:]
                    if causal:
                        z = z - cr_ref[h, j * nsub + u]
                    if diagonal:
                        z = jnp.where((col[r0:] + u * tk) <= row[r0:], z, NEG)
                    p = jnp.exp(z)
                    ds = p * (_dot(doh[h][r0:], v_ref[pl.ds(ks, tk), :], "nt") - delta_wide[h][r0:])
                    term = _dot(ds, kv)
                    dk = _dot(ds, qh[h][r0:], "tn")
                    dk_acc[pl.ds(ks, tk), :] += dk if fold else dk * scale
                    dv_acc[pl.ds(ks, tk), :] += _dot(p, doh[h][r0:], "tn")
                    if causal:
                        dc_ref[h, j * nsub + u] -= jnp.sum(ds, axis=0, keepdims=True)
                    if diagonal:
                        dq_acc[h, r0:, :] += term
                        drow_ref[h, r0:, :] += jnp.sum(ds, axis=1, keepdims=True)
                    else:
                        dq = term if dq is None else dq + term
                        if causal:
                            dsum = ds if dsum is None else dsum + ds
                if not diagonal:
                    dq_acc[h] += dq
                    if causal:
                        drow_ref[h] += jnp.sum(dsum, axis=1, keepdims=True)

        def below(j, carry):
            step(j, False)
            return carry

        if causal:
            lax.fori_loop(0, qi, below, 0)
            step(qi, True)
        else:
            step(0, False)
        dq_ref[...] = (_join_heads([dq_acc[h] for h in range(hpb)], d) * scale).astype(dq_ref.dtype)

        @pl.when(qi == pl.num_programs(1) - 1)
        def _():
            dk_ref[...] = dk_acc[...].astype(dk_ref.dtype)
            dv_ref[...] = dv_acc[...].astype(dv_ref.dtype)

    blk = pl.BlockSpec((tq, LANE), lambda g, i: (i, g))
    full = pl.BlockSpec((Tk, LANE), lambda g, i: (0, g))
    col = pl.BlockSpec((hpb, tq, 1), lambda g, i: (g, i, 0))
    crow = pl.BlockSpec((hpb, T // tk, 1, tk), lambda g, i: (g, 0, 0, 0))
    in_specs = [_lane_tile(tq, q[1], False), _lane_tile(Tk, k[1], True), _lane_tile(Tk, v[1], True), blk, blk, col]
    args = [q[0], k[0], v[0], o, do, lse]
    out_specs = [blk, full, full]
    out_shape = [jax.ShapeDtypeStruct((T, n_tiles * LANE), BF16), jax.ShapeDtypeStruct((Tk, n_tiles * LANE), BF16),
                 jax.ShapeDtypeStruct((Tk, n_tiles * LANE), BF16)]
    if causal:
        in_specs += [col, crow]
        args += [c.reshape(H, T, 1), c.reshape(H, T // tk, 1, tk)]
        out_specs += [crow, col]
        out_shape += [jax.ShapeDtypeStruct((H, T // tk, 1, tk), F32), jax.ShapeDtypeStruct((H, T, 1), F32)]
    outs = pl.pallas_call(
        body, name=name, grid=(n_tiles, T // tq),
        in_specs=in_specs, out_specs=out_specs, out_shape=out_shape,
        scratch_shapes=[pltpu.VMEM((Tk, LANE), F32), pltpu.VMEM((Tk, LANE), F32), pltpu.VMEM((hpb, tq, LANE), F32)],
        compiler_params=_cparams(("parallel", "arbitrary")),
    )(*args)
    if causal:
        return outs[0], outs[1], outs[2], outs[3].reshape(H, T), outs[4].reshape(H, T)
    return outs


def _decay_fwd(fl, b):
    H, T = fl.shape
    tk = DECAY_TK

    def body(x_ref, b_ref, c_ref):
        upto = _tri(tk, lambda j, s: j <= s)
        carry = jnp.zeros((H, 1), F32)
        for i in range(T // tk):
            xv = x_ref[:, i * tk:(i + 1) * tk] + b_ref[...]
            lf = jnp.minimum(xv, 0.0) - jnp.log(1.0 + jnp.exp(-jnp.abs(xv)))
            pref = _dot_split(lf, upto, parts=3) + carry
            c_ref[:, i * tk:(i + 1) * tk] = pref
            carry = pref[:, tk - 1:tk]

    vm = pl.BlockSpec(memory_space=pltpu.VMEM)
    return pl.pallas_call(
        body, name="decay_fwd", in_specs=[vm, vm], out_specs=vm,
        out_shape=jax.ShapeDtypeStruct((H, T), F32),
    )(fl, b)


def _decay_bwd(dc_cols, dc_rows, fl, b):
    H, T = fl.shape
    tk = DECAY_TK

    def body(dc_ref, dr_ref, x_ref, b_ref, dx_ref, db_ref):
        from_ = _tri(tk, lambda j, s: j >= s)
        carry = jnp.zeros((H, 1), F32)
        total = jnp.zeros((H, 1), F32)
        for i in reversed(range(T // tk)):
            sl = slice(i * tk, (i + 1) * tk)
            suffix = _dot_split(dc_ref[:, sl] + dr_ref[:, sl], from_, parts=3) + carry
            xv = x_ref[:, sl] + b_ref[...]
            dx = suffix / (1.0 + jnp.exp(xv))
            dx_ref[:, sl] = dx
            total = total + jnp.sum(dx, axis=1, keepdims=True)
            carry = suffix[:, 0:1]
        db_ref[...] = jnp.broadcast_to(total, db_ref.shape)

    vm = pl.BlockSpec(memory_space=pltpu.VMEM)
    dx, db = pl.pallas_call(
        body, name="decay_bwd", in_specs=[vm, vm, vm, vm], out_specs=[vm, vm],
        out_shape=[jax.ShapeDtypeStruct((H, T), F32), jax.ShapeDtypeStruct((H, LANE), F32)],
    )(dc_cols, dc_rows, fl, b)
    return dx, db[:, 0]


def _place():
    x, y, c = lax.axis_index("x"), lax.axis_index("y"), lax.axis_index("c")
    return x, y, c, [(1 - x, y), (x, 1 - y), (1 - x, 1 - y)]


def _all_gather(name, block):
    R, C = block.shape

    def body(x_ref, out_ref, send_sems, recv_sems, local_sem):
        x, y, c, chips = _place()
        me, sibling = (x, y, c), (x, y, 1 - c)

        def rows(px, py, pc):
            return out_ref.at[4 * px + 2 * py + pc]

        def copy(k, blk, to, src=None):
            return pltpu.make_async_remote_copy(
                src_ref=rows(*blk) if src is None else src, dst_ref=rows(*blk),
                send_sem=send_sems.at[k], recv_sem=recv_sems.at[k], device_id=to, device_id_type=MESH)

        mine = pltpu.make_async_copy(x_ref, rows(*me), local_sem)
        mine.start()
        first = [copy(0, me, sibling, src=x_ref)]
        first += [copy(1 + j, me, (*chip, c), src=x_ref) for j, chip in enumerate(chips)]
        for cp in first:
            cp.start()
        passed = [copy(4 + j, (*chip, c), sibling) for j, chip in enumerate(chips)]
        for j, chip in enumerate(chips):
            copy(1 + j, (*chip, c), me).wait_recv()
            passed[j].start()
        copy(0, sibling, me).wait_recv()
        for j, chip in enumerate(chips):
            copy(4 + j, (*chip, 1 - c), me).wait_recv()
        for cp in first + passed:
            cp.wait_send()
        mine.wait()

    return pl.pallas_call(
        body, name=name, in_specs=[ANY], out_specs=ANY,
        out_shape=jax.ShapeDtypeStruct((N_DEV, R, C), block.dtype),
        scratch_shapes=[pltpu.SemaphoreType.DMA((7,)), pltpu.SemaphoreType.DMA((7,)), pltpu.SemaphoreType.DMA(())],
    )(block)


def _swap_with_sibling(name, parts):
    _, R, C = parts.shape

    def body(p_ref, out_ref, send_sems, recv_sems):
        x, y, c, _ = _place()
        copies = [pltpu.make_async_remote_copy(
            src_ref=p_ref.at[2 * q + (1 - c)], dst_ref=out_ref.at[q],
            send_sem=send_sems.at[q], recv_sem=recv_sems.at[q], device_id=(x, y, 1 - c), device_id_type=MESH)
            for q in range(4)]
        for cp in copies:
            cp.start()
        for cp in copies:
            cp.wait_recv()
        for cp in copies:
            cp.wait_send()

    return pl.pallas_call(
        body, name=name, in_specs=[ANY], out_specs=ANY,
        out_shape=jax.ShapeDtypeStruct((4, R, C), parts.dtype),
        scratch_shapes=[pltpu.SemaphoreType.DMA((4,)), pltpu.SemaphoreType.DMA((4,))],
    )(parts)


def _add_own(name, parts, got, tr=512):
    _, R, C = parts.shape
    tr = _tile(R, tr, SUBLANE_BF16)

    def body(c_ref, p_ref, g_ref, o_ref):
        o_ref[...] = (p_ref[...].astype(F32) + g_ref[...].astype(F32)).astype(o_ref.dtype)

    return pl.pallas_call(
        body, name=name,
        grid_spec=pltpu.PrefetchScalarGridSpec(
            num_scalar_prefetch=1, grid=(4, R // tr),
            in_specs=[pl.BlockSpec((1, tr, C), lambda q, i, c: (2 * q + c[0], i, 0)),
                      pl.BlockSpec((1, tr, C), lambda q, i, c: (q, i, 0))],
            out_specs=pl.BlockSpec((1, tr, C), lambda q, i, c: (q, i, 0))),
        out_shape=jax.ShapeDtypeStruct((4, R, C), parts.dtype),
        compiler_params=_cparams(("parallel", "parallel")),
    )(lax.axis_index("c").astype(jnp.int32).reshape(1), parts, got)


def _swap_with_chips(name, parts):
    _, R, C = parts.shape

    def body(p_ref, out_ref, send_sems, recv_sems, local_sem):
        x, y, c, chips = _place()
        my_chip = 2 * x + y
        mine = pltpu.make_async_copy(p_ref.at[my_chip], out_ref.at[my_chip], local_sem)
        mine.start()
        sends = [pltpu.make_async_remote_copy(
            src_ref=p_ref.at[2 * cx + cy], dst_ref=out_ref.at[my_chip],
            send_sem=send_sems.at[j], recv_sem=recv_sems.at[j], device_id=(cx, cy, c), device_id_type=MESH)
            for j, (cx, cy) in enumerate(chips)]
        for cp in sends:
            cp.start()
        for j, (cx, cy) in enumerate(chips):
            pltpu.make_async_remote_copy(
                src_ref=p_ref.at[my_chip], dst_ref=out_ref.at[2 * cx + cy],
                send_sem=send_sems.at[j], recv_sem=recv_sems.at[j], device_id=(cx, cy, c), device_id_type=MESH,
            ).wait_recv()
        for cp in sends:
            cp.wait_send()
        mine.wait()

    return pl.pallas_call(
        body, name=name, in_specs=[ANY], out_specs=ANY,
        out_shape=jax.ShapeDtypeStruct((4, R, C), parts.dtype),
        scratch_shapes=[pltpu.SemaphoreType.DMA((3,)), pltpu.SemaphoreType.DMA((3,)), pltpu.SemaphoreType.DMA(())],
    )(parts)


def _sum_parts(name, parts, tr=512):
    P, R, C = parts.shape
    tr = _tile(R, tr, SUBLANE_BF16)

    def body(p_ref, o_ref):
        total = p_ref[0].astype(F32)
        for p in range(1, P):
            total = total + p_ref[p].astype(F32)
        o_ref[...] = total

    return pl.pallas_call(
        body, name=name, grid=(R // tr,),
        in_specs=[pl.BlockSpec((P, tr, C), lambda i: (0, i, 0))], out_specs=pl.BlockSpec((tr, C), lambda i: (i, 0)),
        out_shape=jax.ShapeDtypeStruct((R, C), F32),
        compiler_params=_cparams(("parallel",)),
    )(parts)


_HBM = pl.BlockSpec(memory_space=pltpu.HBM)
_SEM = pl.BlockSpec(memory_space=pltpu.SEMAPHORE)
_EFFECT = pltpu.SideEffectType.DATAFLOW_SIDE_EFFECTING


def _flipped(x, y, c, k):
    px, py, pc = (1 - x if k & 4 else x), (1 - y if k & 2 else y), (1 - c if k & 1 else c)
    return (px, py, pc), 4 * px + 2 * py + pc


def _exchange_start(name, src, per_peer):
    R, C = src.shape[-2:]

    def body(v_ref, land_ref, send_sem, recv_sem, v_thru, land_thru, token):
        x, y, c = lax.axis_index("x"), lax.axis_index("y"), lax.axis_index("c")
        me = 4 * x + 2 * y + c
        for k in range(1, N_DEV):
            peer, idx = _flipped(x, y, c, k)
            pltpu.make_async_remote_copy(
                src_ref=v_ref.at[idx] if per_peer else v_ref, dst_ref=land_ref.at[me],
                send_sem=send_sem, recv_sem=recv_sem, device_id=peer, device_id_type=MESH).start()
        token[...] = jnp.zeros_like(token)

    return pl.pallas_call(
        body, name=name,
        out_shape=(pltpu.SemaphoreType.DMA(()), pltpu.SemaphoreType.DMA(()), pltpu.HBM(src.shape, src.dtype),
                   pltpu.HBM((N_DEV, R, C), src.dtype), jax.ShapeDtypeStruct((8, LANE), F32)),
        in_specs=(_HBM, _HBM), out_specs=(_SEM, _SEM, _HBM, _HBM, pl.BlockSpec(memory_space=pltpu.VMEM)),
        input_output_aliases={0: 2, 1: 3},
        compiler_params=pltpu.CompilerParams(has_side_effects=_EFFECT),
    )(pltpu.with_memory_space_constraint(src, pltpu.HBM),
      pltpu.with_memory_space_constraint(lax.empty((N_DEV, R, C), src.dtype), pltpu.HBM))


def _exchange_wait(name, started, after):
    send_sem, recv_sem, v_thru, land_thru, _ = started

    def body(v_ref, land_ref, send_sem, recv_sem, after_ref, v_dead, got_ref):
        x, y, c = lax.axis_index("x"), lax.axis_index("y"), lax.axis_index("c")
        seven = land_ref.at[pl.ds(0, N_DEV - 1)]
        drain = pltpu.make_async_remote_copy(
            src_ref=seven, dst_ref=seven, send_sem=send_sem, recv_sem=recv_sem,
            device_id=(x, y, c), device_id_type=MESH)
        drain.wait_send()
        drain.wait_recv()

    return pl.pallas_call(
        body, name=name,
        out_shape=(pltpu.HBM(v_thru.shape, v_thru.dtype), pltpu.HBM(land_thru.shape, land_thru.dtype)),
        in_specs=(_HBM, _HBM, _SEM, _SEM, ANY), out_specs=(_HBM, _HBM), input_output_aliases={0: 0, 1: 1},
        compiler_params=pltpu.CompilerParams(has_side_effects=_EFFECT),
    )(v_thru, land_thru, send_sem, recv_sem, after)


def _my_index():
    return 4 * lax.axis_index("x") + 2 * lax.axis_index("y") + lax.axis_index("c")


def _sum_landed(name, landed, parts, tr=512):
    P, R, C = landed.shape
    tr = _tile(R, tr, SUBLANE_BF16)

    def body(me_ref, l_ref, own_ref, o_ref):
        total = None
        for s in range(P):
            part = jnp.where(me_ref[0] == s, own_ref[0], l_ref[s]).astype(F32)
            total = part if total is None else total + part
        o_ref[...] = total

    return pl.pallas_call(
        body, name=name,
        grid_spec=pltpu.PrefetchScalarGridSpec(
            num_scalar_prefetch=1, grid=(R // tr,),
            in_specs=[pl.BlockSpec((P, tr, C), lambda i, me: (0, i, 0)),
                      pl.BlockSpec((1, tr, C), lambda i, me: (me[0], i, 0))],
            out_specs=pl.BlockSpec((tr, C), lambda i, me: (i, 0))),
        out_shape=jax.ShapeDtypeStruct((R, C), F32),
        compiler_params=_cparams(("parallel",)),
    )(_my_index().astype(jnp.int32).reshape(1), landed, parts)


def _after(params, name, token):
    return {**params, name: params[name] + token[0, 0]}


def _reduce_scatter(tag, parts):
    got = _swap_with_sibling("rs_pair_" + tag, parts)
    pair = _add_own("rs_add_" + tag, parts, got)
    quad = _swap_with_chips("rs_chips_" + tag, pair)
    return _sum_parts("rs_sum_" + tag, quad)


def _adamw(name, g_parts, w, m, v, tr=512):
    P, R, C = g_parts.shape
    tr = _tile(R, tr, 8)

    def body(g_ref, w_ref, m_ref, v_ref, go_ref, d_ref, mo_ref, vo_ref):
        g = g_ref[0]
        for p in range(1, P):
            g = g + g_ref[p]
        mn = ADAM_B1 * m_ref[...] + (1.0 - ADAM_B1) * g
        vn = ADAM_B2 * v_ref[...] + (1.0 - ADAM_B2) * (g * g)
        m_hat = mn / (1.0 - ADAM_B1 ** ADAM_STEP)
        v_hat = vn / (1.0 - ADAM_B2 ** ADAM_STEP)
        go_ref[...] = g
        d_ref[...] = -ADAM_LR * (m_hat / (jnp.sqrt(v_hat) + ADAM_EPS) + ADAM_WD * w_ref[...])
        mo_ref[...] = mn
        vo_ref[...] = vn

    row = pl.BlockSpec((tr, C), lambda i: (i, 0))
    return pl.pallas_call(
        body, name=name, grid=(R // tr,),
        in_specs=[pl.BlockSpec((P, tr, C), lambda i: (0, i, 0)), row, row, row], out_specs=[row] * 4,
        out_shape=[jax.ShapeDtypeStruct((R, C), F32)] * 4,
        compiler_params=_cparams(("parallel",)),
    )(g_parts, w, m, v)


def _pad_rows(t, rows):
    return jnp.pad(t, ((0, rows - t.shape[0]), (0, 0)))


class _Layout:
    def __init__(self, D, ff_shard, in_shard, kv_shard, gate_shard, br_in, br_shard, out_shard):
        self.D = D
        self.in_shard = in_shard
        self.in_pad = -(-in_shard // LANE) * LANE
        self.in_cols = -(-N_DEV * in_shard // IN_TILE) * IN_TILE
        self.br_in, self.br_shard = br_in, br_shard
        br_rows = br_shard * br_in // D
        sizes = [("g1", ff_shard), ("u1", ff_shard), ("d1", ff_shard), ("win", self.in_pad), ("kv", kv_shard),
                 ("gate", gate_shard), ("br", br_rows), ("out", out_shard),
                 ("g2", ff_shard), ("u2", ff_shard), ("d2", ff_shard)]
        self.seg, off = {}, 0
        for key, n in sizes:
            assert n % SUBLANE_BF16 == 0, (key, n)
            self.seg[key] = (off, n)
            off += n
        self.rows = off

    def pack(self, parts):
        return jnp.concatenate([parts[key] for key in self.seg], axis=0)

    def take(self, gathered, key, own=None):
        off, n = self.seg[key]
        seg = gathered[:, off:off + n, :]
        if own is not None:
            seg = lax.dynamic_update_slice(seg, own[0][off:off + n][None], (own[1], 0, 0))
        return seg.reshape(N_DEV * n, self.D)

    def spread(self, full, key):
        _, n = self.seg[key]
        return full.reshape(N_DEV, n, self.D)


def _pack_layer(lay, l, p):
    D = lay.D
    br = jnp.concatenate([p["w_br_sb"][l], p["w_br_fox"][l], p["w_br_mem"][l]], axis=0)
    parts = {
        "g1": p["ffn1_w_gate"][l].T, "u1": p["ffn1_w_up"][l].T, "d1": p["ffn1_w_down"][l],
        "win": _pad_rows(p["w_in"][l].T, lay.in_pad), "kv": p["w_mem_kv"][l], "gate": p["w_gate"][l].T,
        "br": br.T.reshape(-1, D), "out": p["w_out"][l],
        "g2": p["ffn2_w_gate"][l].T, "u2": p["ffn2_w_up"][l].T, "d2": p["ffn2_w_down"][l],
    }
    return lay.pack({k: t.astype(BF16) for k, t in parts.items()})


def _align_win(lay, packed):
    D = lay.D
    real = packed.reshape(N_DEV, lay.in_pad, D)[:, :lay.in_shard].reshape(N_DEV * lay.in_shard, D)
    rows = jnp.concatenate([real[:_QKV_W], real[_QKV_W + N_FOX_HEADS:], real[_QKV_W:_QKV_W + N_FOX_HEADS]], axis=0)
    return _pad_rows(rows, lay.in_cols)


def _unalign_win(lay, aligned):
    D = lay.D
    n_real = N_DEV * lay.in_shard
    mem_w = n_real - _QKV_W - N_FOX_HEADS
    real = jnp.concatenate([aligned[:_QKV_W], aligned[_QKV_W + mem_w:n_real], aligned[_QKV_W:_QKV_W + mem_w]], axis=0)
    real = real.reshape(N_DEV, lay.in_shard, D)
    return jnp.pad(real, ((0, 0), (0, lay.in_pad - lay.in_shard), (0, 0))).reshape(N_DEV * lay.in_pad, D)


def _unpack_layer(lay, gathered, own=None):
    D = lay.D
    w = {k: lay.take(gathered, k, own) for k in ("g1", "u1", "d1", "kv", "out", "g2", "u2", "d2")}
    w["win"] = _align_win(lay, lay.take(gathered, "win", own))
    fl0 = N_DEV * lay.in_shard - N_FOX_HEADS
    w["wfl"] = w["win"][fl0:fl0 + LANE]
    gate = lay.take(gathered, "gate", own)
    w["gate"] = gate
    w["gate3"] = [gate[i * D:(i + 1) * D] for i in range(3)]
    br = lay.take(gathered, "br", own).reshape(N_DEV * lay.br_shard, lay.br_in)
    third = lay.br_in // 3
    w["br3"] = [br[:, i * third:(i + 1) * third] for i in range(3)]
    return w


def _silu_mul(accs, _):
    a, b = accs
    return [a, b, a * jax.nn.sigmoid(a) * b]


def _act_bwd(accs, extras):
    ds, (a, b) = accs[0], [e.astype(F32) for e in extras]
    sig = jax.nn.sigmoid(a)
    return [ds * b * (sig * (1.0 + a * (1.0 - sig))), ds * (a * sig)]


def _res_norm(scale):
    def epilogue(accs, extras):
        f, res, g = accs[0], extras[0], extras[1]
        out = res + scale * ((f * _rstd(f)) * g)
        return [f, out] + [(out * _rstd(out)) * g_next for g_next in extras[2:]]
    return epilogue


def _down_proj(name, x, w, res, g, scale, next_g):
    D = res.shape[1]
    extras = [(res, 0), (g.reshape(1, D), 0)] + ([(next_g.reshape(1, D), 0)] if next_g is not None else [])
    outs = _mm(name, [(x, w)], "nn", [F32, F32] + [BF16] * (len(extras) - 2), _res_norm(scale), extras, tn=D)
    return outs[0], outs[1], (outs[2] if next_g is not None else None)


def _norm_bwd(accs, extras):
    dy, (x, res, g) = _sum_accs(accs, None)[0], extras
    r = _rstd(x)
    xhat = x * r
    gy = dy * g
    dx = res + r * (gy - xhat * jnp.mean(gy * xhat, axis=-1, keepdims=True))
    part = jnp.sum(dy * xhat, axis=0, keepdims=True)
    first = lax.broadcasted_iota(jnp.int32, (8, part.shape[1]), 0) == 0
    return [dx, jnp.where(first, part, 0.0)]


def _ffn_fwd(tag, h, n, post_g, wg, wu, wd, next_g):
    a, b, s = _mm("ffn_up_" + tag, [(n, wg), (n, wu)], "nt", [BF16, BF16, BF16], _silu_mul, tn=1408)
    f, out, n_next = _down_proj("ffn_down_" + tag, s, wd, h, post_g, 0.5, next_g)
    return out, n_next, (h, n, a, b, s, f)


def _ffn_bwd(tag, dh, saved, pre_g, post_g, wg, wu, wd):
    h, n, a, b, s, f = saved
    D = h.shape[1]
    df, d_post = _rms_bwd("ffn_dout_" + tag, f, post_g, dh, BF16, scale=0.5)
    da, db = _mm("ffn_dact_" + tag, [(df, wd)], "nt", [BF16, BF16], _act_bwd, [(a, 0), (b, 0)], tn=1408)
    d_wd = _mm("ffn_dwd_" + tag, [(s, df)], "tn", [BF16], tm=256)
    dh_in, d_pre_rows = _mm("ffn_dn_" + tag, [(da, wg), (db, wu)], "nn", [F32, F32], _norm_bwd,
                            [(h, 0), (dh, 0), (pre_g.reshape(1, D), 0)], tm=256, tn=D, out_rows=[None, 8])
    d_pre = _colsum("ffn_dpre_" + tag, d_pre_rows)
    d_wg = _mm("ffn_dwg_" + tag, [(da, n)], "tn", [BF16], tm=256)
    d_wu = _mm("ffn_dwu_" + tag, [(db, n)], "tn", [BF16], tm=256)
    return dh_in, d_pre, d_post, d_wg, d_wu, d_wd


_SB_W = N_SB_HEADS * HEAD_DIM
_FOX_W = N_FOX_HEADS * HEAD_DIM
_QKV_W = 3 * _SB_W + 3 * _FOX_W


def _gate_act(accs, extras):
    return [jax.nn.sigmoid(accs[0] + extras[0])]


def _merge(accs, extras):
    g = [e.astype(F32) for e in extras]
    return [g[0] * accs[0] + g[1] * accs[1] + g[2] * accs[2]]


def _merge_bwd(accs, extras):
    dm = accs[0]
    g = [e.astype(F32) for e in extras]
    d_branch = [dm * gi for gi in g]
    d_gate = [dm * bi * gi * (1.0 - gi) for bi, gi in zip(accs[1:], g)]
    return d_branch + d_gate


def _mix_tiles(lay):
    sb, fx = _SB_W // LANE, _FOX_W // LANE
    mem_w = N_DEV * lay.in_shard - _QKV_W - N_FOX_HEADS
    return (0, sb, 2 * sb, sb), (3 * sb, 3 * sb + fx, 3 * sb + 2 * fx, fx), (_QKV_W // LANE, mem_w // LANE)


def _mix_fwd(lay, h, u, w, post_g, b_forget, b_gate, mem_n, next_g):
    D = lay.D
    (sq, sk, sv, sn), (fq, fk, fv, fn), (mq, mn) = _mix_tiles(lay)
    mem_d = mn * LANE // N_MEM_HEADS
    proj = _mm("mix_in", [(u, w["win"])], "nt", [BF16], tm=1024, tn=IN_TILE)
    fl = _mm("mix_fl", [(u, w["wfl"])], "nt", [F32])[:, :N_FOX_HEADS].T
    c = _decay_fwd(fl, b_forget.reshape(-1, 1))
    o_sb, rtot = _sb_fwd((proj, sq), (proj, sk), (proj, sv), sn, HEAD_DIM, HEAD_DIM ** -0.5)
    o_fx32, o_fx, lse_fx = _attn_fwd("fox_fwd", (proj, fq), (proj, fk), (proj, fv), fn, HEAD_DIM,
                                     HEAD_DIM ** -0.5, c)
    kvm = _mm("mem_kv", [(mem_n, w["kv"])], "nn", [BF16])
    o_mem32, o_mem, lse_mem = _attn_fwd("mem_fwd", (proj, mq), (kvm, 0), (kvm, mn), mn, mem_d, mem_d ** -0.5)
    gates = _mm("mix_gate", [(u, w["gate"])], "nt", [BF16], _gate_act, [(b_gate.reshape(1, -1), 0)], tm=1024)
    flat = [o_sb, o_fx, o_mem]
    merged = _mm("mix_merge", list(zip(flat, w["br3"])), "nt", [BF16], _merge,
                 [(gates, 0), (gates, D), (gates, 2 * D)])
    z, out, n_next = _down_proj("mix_out", merged, w["out"], h, post_g, 1.0, next_g)
    saved = (h, u, proj, fl, c, rtot, o_fx32, lse_fx, kvm, o_mem32, lse_mem, gates, flat, merged, z)
    return out, n_next, saved


def _mix_bwd(lay, dh, saved, w, pre_g, post_g, b_forget, mem_n, dmem_n):
    D = lay.D
    (sq, sk, sv, sn), (fq, fk, fv, fn), (mq, mn) = _mix_tiles(lay)
    mem_d = mn * LANE // N_MEM_HEADS
    h, u, proj, fl, c, rtot, o_fx32, lse_fx, kvm, o_mem32, lse_mem, gates, flat, merged, z = saved
    dz, d_post = _rms_bwd("mix_dres", z, post_g, dh, BF16)
    outs = _mm("mix_dmerge", [(dz, w["out"])] + list(zip(flat, w["br3"])), "nt", [BF16] * 6, _merge_bwd,
               [(gates, 0), (gates, D), (gates, 2 * D)], tn=512)
    d_branch, d_gate = outs[:3], outs[3:]
    d_wout = _mm("mix_dwout", [(merged, dz)], "tn", [BF16])
    d_o = [_mm("mix_dbr%d" % i, [(d_branch[i], w["br3"][i])], "nn", [BF16]) for i in range(3)]
    d_wbr = [_mm("mix_dwbr%d" % i, [(d_branch[i], flat[i])], "tn", [BF16]) for i in range(3)]
    d_bgate = jnp.concatenate([_colsum("mix_dbgate%d" % i, d_gate[i]) for i in range(3)])
    d_wgate = [_mm("mix_dwgate%d" % i, [(d_gate[i], u)], "tn", [BF16]) for i in range(3)]

    d_sb = _sb_bwd((proj, sq), (proj, sk), (proj, sv), d_o[0], rtot, sn, HEAD_DIM, HEAD_DIM ** -0.5)
    *d_fx, dc, dc_rows = _attn_bwd("fox_bwd", (proj, fq), (proj, fk), (proj, fv), o_fx32, d_o[1], lse_fx, fn,
                                   HEAD_DIM, HEAD_DIM ** -0.5, c)
    dq_m, dk_m, dv_m = _attn_bwd("mem_bwd", (proj, mq), (kvm, 0), (kvm, mn), o_mem32, d_o[2], lse_mem, mn,
                                 mem_d, mem_d ** -0.5)
    dfl, d_bforget = _decay_bwd(dc, dc_rows, fl, b_forget.reshape(-1, 1))
    pieces = list(d_sb) + list(d_fx) + [dq_m]
    dflp = jnp.pad(dfl.T.astype(BF16), ((0, 0), (0, LANE - dfl.shape[0])))
    offs = [sum(t.shape[1] for t in pieces[:i]) for i in range(len(pieces) + 1)]
    win_rows = [w["win"][offs[i]:offs[i + 1]] for i in range(len(pieces))]
    du = _mm("mix_du", list(zip(d_gate, w["gate3"])) + list(zip(pieces, win_rows)) + [(dflp, w["wfl"])], "nn",
             [F32], _sum_accs, tm=256, tn=512)
    d_rows = [_mm("mix_dwin%d" % i, [(t, u)], "tn", [BF16]) for i, t in enumerate(pieces)]
    d_wfl = _mm("mix_dwfl", [(dflp, u)], "tn", [BF16])
    d_win = _unalign_win(lay, _pad_rows(jnp.concatenate(list(d_rows) + [d_wfl], axis=0), lay.in_cols))
    dh_in, d_pre = _rms_bwd("mix_dnorm", h, pre_g, du, F32, res=dh)

    dkvm = jnp.concatenate([dk_m, dv_m], axis=1)
    d_wkv = _mm("mem_dwkv", [(mem_n, dkvm)], "tn", [BF16])
    dmem_n = _mm("mem_dn", [(dkvm, w["kv"])], "nt", [F32], lambda accs, ex: [accs[0] + ex[0]], [(dmem_n, 0)])
    grads = {"win": d_win, "kv": d_wkv, "gate": jnp.concatenate(d_wgate, axis=0),
             "br": jnp.concatenate(d_wbr, axis=1), "out": d_wout}
    return dh_in, d_pre, d_post, d_bforget, d_bgate, grads, dmem_n


def _layer_fwd(lay, h, n, w, sp, mem_n, next_g):
    h1, u, s1 = _ffn_fwd("1", h, n, sp["ffn1_post_g"], w["g1"], w["u1"], w["d1"], sp["mix_pre_g"])
    h2, n2, s2 = _mix_fwd(lay, h1, u, w, sp["mix_post_g"], sp["b_forget"], sp["b_gate"], mem_n, sp["ffn2_pre_g"])
    h3, n_next, s3 = _ffn_fwd("2", h2, n2, sp["ffn2_post_g"], w["g2"], w["u2"], w["d2"], next_g)
    return h3, n_next, (s1, s2, s3)


def _layer_bwd(lay, dh, saved, w, sp, mem_n, dmem_n):
    s1, s2, s3 = saved
    dh, d_pre2, d_post2, d_g2, d_u2, d_d2 = _ffn_bwd("2", dh, s3, sp["ffn2_pre_g"], sp["ffn2_post_g"],
                                                     w["g2"], w["u2"], w["d2"])
    dh, d_mpre, d_mpost, d_bforget, d_bgate, g, dmem_n = _mix_bwd(
        lay, dh, s2, w, sp["mix_pre_g"], sp["mix_post_g"], sp["b_forget"], mem_n, dmem_n)
    dh, d_pre1, d_post1, d_g1, d_u1, d_d1 = _ffn_bwd("1", dh, s1, sp["ffn1_pre_g"], sp["ffn1_post_g"],
                                                     w["g1"], w["u1"], w["d1"])
    g.update({"g1": d_g1, "u1": d_u1, "d1": d_d1, "g2": d_g2, "u2": d_u2, "d2": d_d2})
    g["br"] = g["br"].reshape(N_DEV, lay.br_shard, lay.br_in).reshape(-1, lay.D)
    packed = jnp.concatenate([lay.spread(g[key], key) for key in lay.seg], axis=1)
    small = {"ffn1_pre_g": d_pre1, "ffn1_post_g": d_post1, "mix_pre_g": d_mpre, "mix_post_g": d_mpost,
             "ffn2_pre_g": d_pre2, "ffn2_post_g": d_post2, "b_gate": d_bgate, "b_forget": d_bforget}
    return dh, packed, small, dmem_n


_SHARDED = ["ffn1_w_gate", "ffn1_w_up", "ffn1_w_down", "w_in", "w_mem_kv", "w_gate", "w_br_sb", "w_br_fox",
            "w_br_mem", "w_out", "ffn2_w_gate", "ffn2_w_up", "ffn2_w_down"]
_SMALL_LAYER = ["ffn1_pre_g", "ffn1_post_g", "mix_pre_g", "mix_post_g", "ffn2_pre_g", "ffn2_post_g", "b_gate",
                "b_forget"]
_WEIGHTS = ["ffn1_pre_g", "ffn1_post_g", "ffn1_w_gate", "ffn1_w_up", "ffn1_w_down", "mix_pre_g", "mix_post_g",
            "w_in", "b_forget", "mem_norm_g", "w_mem_kv", "w_gate", "b_gate", "w_br_sb", "w_br_fox", "w_br_mem",
            "w_out", "ffn2_pre_g", "ffn2_post_g", "ffn2_w_gate", "ffn2_w_up", "ffn2_w_down"]


def _pack_small(vals, L, D):
    rows = []
    for l in range(L):
        for name in _SMALL_LAYER:
            t = vals[name][l]
            rows.append(jnp.pad(t, (0, -t.shape[0] % D)).reshape(-1, D))
    rows.append(vals["mem_norm_g"].reshape(1, D))
    packed = jnp.concatenate(rows, axis=0)
    return _pad_rows(packed, -(-packed.shape[0] // 8) * 8)


def _unpack_small(packed, shapes, L, D):
    out = {name: [] for name in _SMALL_LAYER}
    r = 0
    for l in range(L):
        for name in _SMALL_LAYER:
            n = shapes[name][1]
            nr = -(-n // D)
            out[name].append(packed[r:r + nr].reshape(-1)[:n])
            r += nr
    res = {name: jnp.stack(v) for name, v in out.items()}
    res["mem_norm_g"] = packed[r]
    return res


def _unpack_grads(lay, g):
    def seg(key):
        off, n = lay.seg[key]
        return g[off:off + n]
    br = seg("br").reshape(lay.br_shard, lay.br_in).T
    third = lay.br_in // 3
    return {
        "ffn1_w_gate": seg("g1").T, "ffn1_w_up": seg("u1").T, "ffn1_w_down": seg("d1"),
        "w_in": seg("win")[:lay.in_shard].T, "w_mem_kv": seg("kv"), "w_gate": seg("gate").T,
        "w_br_sb": br[:third], "w_br_fox": br[third:2 * third], "w_br_mem": br[2 * third:],
        "w_out": seg("out"), "ffn2_w_gate": seg("g2").T, "ffn2_w_up": seg("u2").T, "ffn2_w_down": seg("d2"),
    }


class _Exchanges:
    def gather(self, name, block):
        return _all_gather(name, block)

    def gather_start(self, block):
        return _exchange_start("ag_start", block, per_peer=False)

    def gather_wait(self, started, after):
        block, landed = _exchange_wait("ag_wait", started, after)
        return landed, (block, _my_index())

    def scatter(self, parts):
        return _reduce_scatter("w", parts)

    def scatter_start(self, parts):
        return _exchange_start("rs_start", parts, per_peer=True)

    def scatter_wait(self, started, after):
        parts, landed = _exchange_wait("rs_wait", started, after)
        return _sum_landed("rs_sum8", landed, parts)

    def token(self, started):
        return started[4]

    def loss_sum(self, part):
        return lax.psum(part, ("x", "y", "c"))


def _step(p, m, v, x, mem, tgt, ex):
    L, D = p["ffn1_pre_g"].shape
    lay = _Layout(D, p["ffn1_w_gate"].shape[2], p["w_in"].shape[2], p["w_mem_kv"].shape[1], p["w_gate"].shape[2],
                  3 * p["w_br_sb"].shape[1], p["w_br_sb"].shape[2], p["w_out"].shape[1])
    blocks = [_pack_layer(lay, l, p) for l in range(L)]
    sps = [{name: p[name][l] for name in _SMALL_LAYER} for l in range(L)]

    mem_n = _rms_fwd("mem_norm", mem, p["mem_norm_g"], BF16)
    gathered, own = ex.gather("ag_weights", blocks[0]), None
    h, saved, ws = x, [], []
    n = _rms_fwd("first_norm", x, sps[0]["ffn1_pre_g"], BF16)
    for l in range(L):
        if l + 1 < L:
            nxt, gathered = lax.optimization_barrier((blocks[l + 1], gathered))
            started = ex.gather_start(nxt)
            sp = _after(sps[l], "ffn1_post_g", ex.token(started))
        else:
            sp = sps[l]
        ws.append(_unpack_layer(lay, gathered, own))
        h, n, s = _layer_fwd(lay, h, n, ws[l], sp, mem_n, sps[l + 1]["ffn1_pre_g"] if l + 1 < L else None)
        saved.append(s)
        if l + 1 < L:
            gathered, own = ex.gather_wait(started, h)
    loss_part, dh = _loss_grad(h, tgt)
    loss = ex.loss_sum(loss_part)

    dmem_n = jnp.zeros(mem.shape, F32)
    big, small = [None] * L, {name: [None] * L for name in _SMALL_LAYER}
    flying, token = {}, None
    for l in reversed(range(L)):
        sp = sps[l] if token is None else _after(sps[l], "ffn2_post_g", token)
        dh, packed, sm, dmem_n = _layer_bwd(lay, dh, saved[l], ws[l], sp, mem_n, dmem_n)
        if l > 0:
            flying[l] = ex.scatter_start(packed)
            token = ex.token(flying[l])
        else:
            big[l] = _unpack_grads(lay, ex.scatter(packed))
        for name in _SMALL_LAYER:
            small[name][l] = sm[name]
    for l, started in flying.items():
        big[l] = _unpack_grads(lay, ex.scatter_wait(started, dh))
    _, d_memg = _rms_bwd("mem_dnorm", mem, p["mem_norm_g"], dmem_n, F32)

    small_g = {name: jnp.stack(vs) for name, vs in small.items()}
    small_g["mem_norm_g"] = d_memg
    small_names = _SMALL_LAYER + ["mem_norm_g"]
    shapes = {name: p[name].shape for name in small_names}
    g_all = ex.gather("ag_small", _pack_small(small_g, L, D))
    packs = [_pack_small({name: t[name] for name in small_names}, L, D) for t in (p, m, v)]
    res = [_unpack_small(t, shapes, L, D) for t in _adamw("adamw_small", g_all, *packs)]

    out = {kind: {} for kind in ("grad", "delta", "new_m", "new_v")}
    for name in small_names:
        for kind, r in zip(("grad", "delta", "new_m", "new_v"), res):
            out[kind][name] = r[name].reshape(p[name].shape)
    for name in _SHARDED:
        g = jnp.stack([big[l][name] for l in range(L)])
        shp = g.shape
        flat = lambda t: t.reshape(-1, shp[-1])
        r = _adamw("adamw_" + name, flat(g)[None], flat(p[name]), flat(m[name]), flat(v[name]))
        for kind, t in zip(("grad", "delta", "new_m", "new_v"), r):
            out[kind][name] = t.reshape(shp)
    return loss, dh, out


def kernel(x, mem, ffn1_pre_g, ffn1_post_g, ffn1_w_gate, ffn1_w_up, ffn1_w_down, mix_pre_g, mix_post_g, w_in, b_forget, mem_norm_g, w_mem_kv, w_gate, b_gate, w_br_sb, w_br_fox, w_br_mem, w_out, ffn2_pre_g, ffn2_post_g, ffn2_w_gate, ffn2_w_up, ffn2_w_down, loss_target, m_ffn1_pre_g, m_ffn1_post_g, m_ffn1_w_gate, m_ffn1_w_up, m_ffn1_w_down, m_mix_pre_g, m_mix_post_g, m_w_in, m_b_forget, m_mem_norm_g, m_w_mem_kv, m_w_gate, m_b_gate, m_w_br_sb, m_w_br_fox, m_w_br_mem, m_w_out, m_ffn2_pre_g, m_ffn2_post_g, m_ffn2_w_gate, m_ffn2_w_up, m_ffn2_w_down, v_ffn1_pre_g, v_ffn1_post_g, v_ffn1_w_gate, v_ffn1_w_up, v_ffn1_w_down, v_mix_pre_g, v_mix_post_g, v_w_in, v_b_forget, v_mem_norm_g, v_w_mem_kv, v_w_gate, v_b_gate, v_w_br_sb, v_w_br_fox, v_w_br_mem, v_w_out, v_ffn2_pre_g, v_ffn2_post_g, v_ffn2_w_gate, v_ffn2_w_up, v_ffn2_w_down):
    p = dict(zip(_WEIGHTS, (ffn1_pre_g, ffn1_post_g, ffn1_w_gate, ffn1_w_up, ffn1_w_down, mix_pre_g, mix_post_g, w_in, b_forget, mem_norm_g, w_mem_kv, w_gate, b_gate, w_br_sb, w_br_fox, w_br_mem, w_out, ffn2_pre_g, ffn2_post_g, ffn2_w_gate, ffn2_w_up, ffn2_w_down)))
    m = dict(zip(_WEIGHTS, (m_ffn1_pre_g, m_ffn1_post_g, m_ffn1_w_gate, m_ffn1_w_up, m_ffn1_w_down, m_mix_pre_g, m_mix_post_g, m_w_in, m_b_forget, m_mem_norm_g, m_w_mem_kv, m_w_gate, m_b_gate, m_w_br_sb, m_w_br_fox, m_w_br_mem, m_w_out, m_ffn2_pre_g, m_ffn2_post_g, m_ffn2_w_gate, m_ffn2_w_up, m_ffn2_w_down)))
    v = dict(zip(_WEIGHTS, (v_ffn1_pre_g, v_ffn1_post_g, v_ffn1_w_gate, v_ffn1_w_up, v_ffn1_w_down, v_mix_pre_g, v_mix_post_g, v_w_in, v_b_forget, v_mem_norm_g, v_w_mem_kv, v_w_gate, v_b_gate, v_w_br_sb, v_w_br_fox, v_w_br_mem, v_w_out, v_ffn2_pre_g, v_ffn2_post_g, v_ffn2_w_gate, v_ffn2_w_up, v_ffn2_w_down)))
    loss, dx, out = _step(p, m, v, x[0], mem[0], loss_target[0], _Exchanges())
    return (loss, dx[None], *[out["grad"][n] for n in _WEIGHTS], *[out["delta"][n] for n in _WEIGHTS],
            *[out["new_m"][n] for n in _WEIGHTS], *[out["new_v"][n] for n in _WEIGHTS])
```

```python
import functools
import math

import jax
import jax.numpy as jnp
from jax import lax
from jax.experimental import pallas as pl
from jax.experimental.pallas import tpu as pltpu

F32 = jnp.float32
BF16 = jnp.bfloat16

LANE = 128
SUBLANE_BF16 = 16
VMEM_LIMIT = 56 * 1024 * 1024
N_DEV = 8
MESH = pl.DeviceIdType.MESH
ANY = pl.BlockSpec(memory_space=pl.ANY)

RMS_EPS = 1e-6
HEAD_DIM = 64
N_SB_HEADS = 8
N_FOX_HEADS = 8
N_MEM_HEADS = 4
NEG = -1e30
ATT_TQ = 1024
ATT_TK = 256
DECAY_TK = 128
IN_TILE = 1280

ADAM_LR = 0.001
ADAM_B1 = 0.9
ADAM_B2 = 0.999
ADAM_EPS = 1e-08
ADAM_WD = 0.01
ADAM_STEP = 10


def _tile(n, target, mult=LANE):
    best = None
    for t in range(mult, min(n, target) + 1, mult):
        if n % t == 0:
            best = t
    return best if best is not None else n


def _cparams(sem):
    return pltpu.CompilerParams(dimension_semantics=sem, vmem_limit_bytes=VMEM_LIMIT)


_DIMS = {"nn": (((1,), (0,)), ((), ())), "nt": (((1,), (1,)), ((), ())), "tn": (((0,), (0,)), ((), ()))}


def _dot(a, b, mode="nn"):
    return lax.dot_general(a.astype(BF16), b.astype(BF16), _DIMS[mode], preferred_element_type=F32)


def _mm(name, pairs, mode, out_dtypes, epilogue=None, extras=(), tm=512, tn=1024, out_rows=None):
    a0, b0 = pairs[0]
    M = a0.shape[1] if mode == "tn" else a0.shape[0]
    N = b0.shape[0] if mode == "nt" else b0.shape[1]
    tm = _tile(M, tm)
    tn = _tile(N, tn)
    np_, ne, no = len(pairs), len(extras), len(out_dtypes)

    def body(*refs):
        a_refs, b_refs = refs[:np_], refs[np_:2 * np_]
        e_refs = refs[2 * np_:2 * np_ + ne]
        o_refs = refs[2 * np_ + ne:]
        accs = [_dot(a[...], b[...], mode) for a, b in zip(a_refs, b_refs)]
        outs = epilogue(accs, [e[...] for e in e_refs]) if epilogue is not None else accs
        for o, val in zip(o_refs, outs):
            o[...] = val.astype(o.dtype)

    in_specs = []
    for a, _ in pairs:
        if mode == "tn":
            in_specs.append(pl.BlockSpec((a.shape[0], tm), lambda j, i: (0, i)))
        else:
            in_specs.append(pl.BlockSpec((tm, a.shape[1]), lambda j, i: (i, 0)))
    for _, b in pairs:
        if mode == "nt":
            in_specs.append(pl.BlockSpec((tn, b.shape[1]), lambda j, i: (j, 0)))
        else:
            in_specs.append(pl.BlockSpec((b.shape[0], tn), lambda j, i: (0, j)))
    for e, off in extras:
        if e.shape[0] == 1:
            in_specs.append(pl.BlockSpec((1, tn), functools.partial(lambda j, i, o: (0, j + o), o=off // tn)))
        else:
            in_specs.append(pl.BlockSpec((tm, tn), functools.partial(lambda j, i, o: (i, j + o), o=off // tn)))
    rows = [tm if r is None else r for r in (out_rows or [None] * no)]
    out_specs = [pl.BlockSpec((r, tn), lambda j, i: (i, j)) for r in rows]
    outs = pl.pallas_call(
        body, name=name, grid=(N // tn, M // tm),
        in_specs=in_specs, out_specs=out_specs,
        out_shape=[jax.ShapeDtypeStruct((M // tm * r, N), dt) for r, dt in zip(rows, out_dtypes)],
        compiler_params=_cparams(("parallel", "parallel")),
    )(*[a for a, _ in pairs], *[b for _, b in pairs], *[e for e, _ in extras])
    return outs[0] if no == 1 else outs


def _sum_accs(accs, _):
    total = accs[0]
    for acc in accs[1:]:
        total = total + acc
    return [total]


def _rstd(x):
    return lax.rsqrt(jnp.mean(x * x, axis=-1, keepdims=True) + RMS_EPS)


def _rms_fwd(name, x, g, out_dtype, tr=512):
    R, D = x.shape
    tr = _tile(R, tr, 8)

    def body(x_ref, g_ref, o_ref):
        xv = x_ref[...]
        o_ref[...] = ((xv * _rstd(xv)) * g_ref[...]).astype(o_ref.dtype)

    row = pl.BlockSpec((tr, D), lambda i: (i, 0))
    return pl.pallas_call(
        body, name=name, grid=(R // tr,),
        in_specs=[row, pl.BlockSpec((1, D), lambda i: (0, 0))], out_specs=row,
        out_shape=jax.ShapeDtypeStruct((R, D), out_dtype),
        compiler_params=_cparams(("parallel",)),
    )(x, g.reshape(1, D))


def _rms_bwd(name, x, g, dy, out_dtype, scale=1.0, res=None, tr=512):
    R, D = x.shape
    tr = _tile(R, tr, 8)
    has_res = res is not None

    def body(*refs):
        x_ref, g_ref, dy_ref = refs[:3]
        dx_ref, dg_ref = refs[-2:]
        i = pl.program_id(0)
        xv = x_ref[...]
        xhat = xv * _rstd(xv)
        dyv = dy_ref[...].astype(F32) * scale
        gy = dyv * g_ref[...]
        dx = _rstd(xv) * (gy - xhat * jnp.mean(gy * xhat, axis=-1, keepdims=True))
        if has_res:
            dx = refs[3][...] + dx
        dx_ref[...] = dx.astype(dx_ref.dtype)
        part = jnp.sum(dyv * xhat, axis=0, keepdims=True)

        @pl.when(i == 0)
        def _():
            dg_ref[...] = part

        @pl.when(i > 0)
        def _():
            dg_ref[...] += part

    row = pl.BlockSpec((tr, D), lambda i: (i, 0))
    gain = pl.BlockSpec((1, D), lambda i: (0, 0))
    dx, dg = pl.pallas_call(
        body, name=name, grid=(R // tr,),
        in_specs=[row, gain, row] + ([row] if has_res else []), out_specs=[row, gain],
        out_shape=[jax.ShapeDtypeStruct((R, D), out_dtype), jax.ShapeDtypeStruct((1, D), F32)],
        compiler_params=_cparams(("arbitrary",)),
    )(x, g.reshape(1, D), dy, *([res] if has_res else []))
    return dx, dg[0]


def _loss_grad(y, tgt, tr=512):
    R, D = y.shape
    tr = _tile(R, tr, 8)

    def body(y_ref, t_ref, dy_ref, loss_ref):
        i = pl.program_id(0)
        d = y_ref[...] - t_ref[...]
        dy_ref[...] = d / D
        part = 0.5 * jnp.sum(jnp.mean(d * d, axis=-1, keepdims=True), axis=0, keepdims=True)
        tile = jnp.broadcast_to(part, loss_ref.shape)

        @pl.when(i == 0)
        def _():
            loss_ref[...] = tile

        @pl.when(i > 0)
        def _():
            loss_ref[...] += tile

    row = pl.BlockSpec((tr, D), lambda i: (i, 0))
    dy, loss = pl.pallas_call(
        body, name="loss_grad", grid=(R // tr,),
        in_specs=[row, row], out_specs=[row, pl.BlockSpec((8, LANE), lambda i: (0, 0))],
        out_shape=[jax.ShapeDtypeStruct((R, D), F32), jax.ShapeDtypeStruct((8, LANE), F32)],
        compiler_params=_cparams(("arbitrary",)),
    )(y, tgt)
    return loss[0, 0], dy


def _colsum(name, x, tr=512, tn=1024):
    R, N = x.shape
    tr, tn = _tile(R, tr, 8), _tile(N, tn)

    def body(x_ref, o_ref):
        i = pl.program_id(1)
        part = jnp.sum(x_ref[...].astype(F32), axis=0, keepdims=True)

        @pl.when(i == 0)
        def _():
            o_ref[...] = part

        @pl.when(i > 0)
        def _():
            o_ref[...] += part

    out = pl.pallas_call(
        body, name=name, grid=(N // tn, R // tr),
        in_specs=[pl.BlockSpec((tr, tn), lambda j, i: (i, j))], out_specs=pl.BlockSpec((1, tn), lambda j, i: (0, j)),
        out_shape=jax.ShapeDtypeStruct((1, N), F32),
        compiler_params=_cparams(("parallel", "arbitrary")),
    )(x)
    return out[0]


def _tri(tk, rel):
    j = lax.broadcasted_iota(jnp.int32, (tk, tk), 0)
    s = lax.broadcasted_iota(jnp.int32, (tk, tk), 1)
    return rel(j, s).astype(BF16)


def _dot_split(x, m, parts=2):
    total = None
    rem = x
    for _ in range(parts):
        piece = rem.astype(BF16)
        rem = rem - piece.astype(F32)
        term = jnp.dot(piece, m, preferred_element_type=F32)
        total = term if total is None else total + term
    return total


def _log_not_and_beta(z, mask):
    ln = -(jnp.maximum(z, 0.0) + jnp.log(1.0 + jnp.exp(-jnp.abs(z))))
    return (ln if mask is None else jnp.where(mask, ln, 0.0)), ln + z


def _att_tiles(T, Tk, causal):
    tq = min(ATT_TQ, T)
    tk = min(ATT_TK, tq if causal else Tk)
    return tq, tk, (tq if causal else Tk) // tk


def _key_base(j, tq):
    return j * tq if isinstance(j, int) else pl.multiple_of(j * tq, tq)


def _is_pow2(scale):
    return math.log2(scale).is_integer()


def _per_head(x, hpb, d):
    if hpb == 1:
        return [x]
    lane = lax.broadcasted_iota(jnp.int32, x.shape, 1)
    return [jnp.where((lane >= h * d) & (lane < (h + 1) * d), x, jnp.zeros_like(x)) for h in range(hpb)]


def _join_heads(xs, d):
    out = xs[-1]
    if len(xs) > 1:
        lane = lax.broadcasted_iota(jnp.int32, out.shape, 1)
        for h in reversed(range(len(xs) - 1)):
            out = jnp.where(lane < (h + 1) * d, xs[h], out)
    return out


def _lane_tile(rows, off, whole):
    if whole:
        return pl.BlockSpec((rows, LANE), lambda g, i: (0, off + g))
    return pl.BlockSpec((rows, LANE), lambda g, i: (i, off + g))


def _sb_fwd(q, k, v, n_tiles, d, scale):
    T = q[0].shape[0]
    hpb = LANE // d
    tq, tk, nsub = _att_tiles(T, T, True)
    assert _is_pow2(scale)

    def body(q_ref, k_ref, v_ref, ob_ref, rt_ref, acc_ref, r_ref):
        qi = pl.program_id(1)
        qh = _per_head(q_ref[...] * scale, hpb, d)
        acc_ref[...] = jnp.zeros_like(acc_ref)
        r_ref[...] = jnp.zeros_like(r_ref)
        row = lax.broadcasted_iota(jnp.int32, (tq, tk), 0)
        col = lax.broadcasted_iota(jnp.int32, (tq, tk), 1)
        after = _tri(tk, lambda j, s: j > s)

        def walk(h, base, r0, r1, subs, diagonal):
            parts = []
            for u in subs:
                z = _dot(qh[h][r0:r1], k_ref[pl.ds(base + u * tk, tk), :], "nt")
                mask = (col[r0:r1] + u * tk) < row[r0:r1] if diagonal else None
                ln, lb = _log_not_and_beta(z, mask)
                between = _dot_split(ln, after, parts=1)
                first = ln[:, 0:1].astype(BF16).astype(F32)
                parts.append((u, lb, between, between[:, 0:1] + first, mask))
            r = r_ref[h, r0:r1, :]
            out = None
            for u, lb, between, total, mask in parts:
                w = jnp.exp(lb + between + r)
                if diagonal:
                    w = jnp.where(mask, w, 0.0)
                term = _dot(w, v_ref[pl.ds(base + u * tk, tk), :])
                out = term if out is None else out + term
                r = r + total
            acc_ref[h, r0:r1, :] += out
            r_ref[h, r0:r1, :] = r

        def step(j, diagonal):
            base = _key_base(j, tq)
            for h in range(hpb):
                if diagonal and nsub % 2 == 0:
                    walk(h, base, 0, tq // 2, range(nsub // 2 - 1, -1, -1), True)
                    walk(h, base, tq // 2, tq, range(nsub - 1, -1, -1), True)
                else:
                    walk(h, base, 0, tq, range(nsub - 1, -1, -1), diagonal)

        def below(i, carry):
            step(qi - 1 - i, False)
            return carry

        step(qi, True)
        lax.fori_loop(0, qi, below, 0)
        ob_ref[...] = _join_heads([acc_ref[h] for h in range(hpb)], d).astype(ob_ref.dtype)
        rt_ref[...] = r_ref[...]

    out = pl.BlockSpec((tq, LANE), lambda g, i: (i, g))
    col = pl.BlockSpec((hpb, tq, 1), lambda g, i: (g, i, 0))
    return pl.pallas_call(
        body, name="sb_fwd", grid=(n_tiles, T // tq),
        in_specs=[_lane_tile(tq, q[1], False), _lane_tile(T, k[1], True), _lane_tile(T, v[1], True)],
        out_specs=[out, col],
        out_shape=[jax.ShapeDtypeStruct((T, n_tiles * LANE), BF16), jax.ShapeDtypeStruct((n_tiles * hpb, T, 1), F32)],
        scratch_shapes=[pltpu.VMEM((hpb, tq, LANE), F32), pltpu.VMEM((hpb, tq, 1), F32)],
        compiler_params=_cparams(("parallel", "arbitrary")),
    )(q[0], k[0], v[0])


def _sb_bwd(q, k, v, do, rtot, n_tiles, d, scale):
    T = q[0].shape[0]
    hpb = LANE // d
    tq, tk, nsub = _att_tiles(T, T, True)
    assert _is_pow2(scale)

    def body(q_ref, k_ref, v_ref, do_ref, rt_ref, dq_ref, dk_ref, dv_ref, dk_acc, dv_acc, dq_acc, p_ref, c_ref):
        qi = pl.program_id(1)

        @pl.when(qi == 0)
        def _():
            dk_acc[...] = jnp.zeros_like(dk_acc)
            dv_acc[...] = jnp.zeros_like(dv_acc)

        qh = _per_head(q_ref[...] * scale, hpb, d)
        doh = _per_head(do_ref[...], hpb, d)
        dq_acc[...] = jnp.zeros_like(dq_acc)
        p_ref[...] = jnp.zeros_like(p_ref)
        c_ref[...] = jnp.zeros_like(c_ref)
        row = lax.broadcasted_iota(jnp.int32, (tq, tk), 0)
        col = lax.broadcasted_iota(jnp.int32, (tq, tk), 1)
        upto = _tri(tk, lambda j, s: j <= s)
        before = _tri(tk, lambda j, s: j < s)
        rt_wide = [jnp.broadcast_to(rt_ref[h], (tq, tk)) for h in range(hpb)]

        def step(j, diagonal):
            base = _key_base(j, tq)
            for h in range(hpb):
                first = []
                for u in range(nsub):
                    ks = base + u * tk
                    r0 = u * tk if diagonal else 0
                    kv = k_ref[pl.ds(ks, tk), :]
                    z = _dot(qh[h][r0:], kv, "nt")
                    mask = (col[r0:] + u * tk) < row[r0:] if diagonal else None
                    ln, lb = _log_not_and_beta(z, mask)
                    dw = _dot(doh[h][r0:], v_ref[pl.ds(ks, tk), :], "nt")
                    first.append((r0, ks, kv, mask, lb, jnp.exp(lb), _dot_split(ln, upto, parts=1), dw))
                rt, pre, cpre = rt_wide[h], p_ref[h], c_ref[h]
                dq = None
                for r0, ks, kv, mask, lb, sig, local, dw in first:
                    if diagonal and r0:
                        pre, cpre = pre[tk:], cpre[tk:]
                    prefix = local + pre
                    w = jnp.exp(lb + (rt[r0:] - prefix))
                    if diagonal:
                        w = jnp.where(mask, w, 0.0)
                    g = dw * w
                    c = _dot_split(g, before, parts=1) + cpre
                    dz = g * (1.0 - sig) - c * sig
                    if diagonal:
                        dz = jnp.where(mask, dz, 0.0)
                    term = _dot(dz, kv)
                    if diagonal:
                        dq_acc[h, r0:, :] += term
                    else:
                        dq = term if dq is None else dq + term
                    dk_acc[pl.ds(ks, tk), :] += _dot(dz, qh[h][r0:], "tn")
                    dv_acc[pl.ds(ks, tk), :] += _dot(w, doh[h][r0:], "tn")
                    pre = prefix[:, tk - 1:tk]
                    cpre = c[:, tk - 1:tk] + g[:, tk - 1:tk]
                if not diagonal:
                    dq_acc[h] += dq
                    p_ref[h] = pre
                    c_ref[h] = cpre

        def below(j, carry):
            step(j, False)
            return carry

        lax.fori_loop(0, qi, below, 0)
        step(qi, True)
        dq_ref[...] = (_join_heads([dq_acc[h] for h in range(hpb)], d) * scale).astype(dq_ref.dtype)

        @pl.when(qi == pl.num_programs(1) - 1)
        def _():
            dk_ref[...] = dk_acc[...].astype(dk_ref.dtype)
            dv_ref[...] = dv_acc[...].astype(dv_ref.dtype)

    blk = pl.BlockSpec((tq, LANE), lambda g, i: (i, g))
    full = pl.BlockSpec((T, LANE), lambda g, i: (0, g))
    col = pl.BlockSpec((hpb, tq, 1), lambda g, i: (g, i, 0))
    wide = jax.ShapeDtypeStruct((T, n_tiles * LANE), BF16)
    return pl.pallas_call(
        body, name="sb_bwd", grid=(n_tiles, T // tq),
        in_specs=[_lane_tile(tq, q[1], False), _lane_tile(T, k[1], True), _lane_tile(T, v[1], True), blk, col],
        out_specs=[blk, full, full], out_shape=[wide, wide, wide],
        scratch_shapes=[pltpu.VMEM((T, LANE), F32), pltpu.VMEM((T, LANE), F32), pltpu.VMEM((hpb, tq, LANE), F32),
                        pltpu.VMEM((hpb, tq, 1), F32), pltpu.VMEM((hpb, tq, 1), F32)],
        compiler_params=_cparams(("parallel", "arbitrary")),
    )(q[0], k[0], v[0], do, rtot)


def _attn_fwd(name, q, k, v, n_tiles, d, scale, c=None):
    T, Tk = q[0].shape[0], k[0].shape[0]
    hpb = LANE // d
    H = n_tiles * hpb
    causal = c is not None
    tq, tk, nsub = _att_tiles(T, Tk, causal)
    fold = _is_pow2(scale)

    def body(*refs):
        q_ref, k_ref, v_ref = refs[:3]
        cc_ref, cr_ref = refs[3:5] if causal else (None, None)
        o_ref, ob_ref, lse_ref, m_ref, l_ref, acc_ref = refs[-6:]
        qi = pl.program_id(1)
        qh = _per_head(q_ref[...] * scale if fold else q_ref[...], hpb, d)
        bias = [jnp.broadcast_to(cc_ref[h], (tq, tk)) for h in range(hpb)] if causal else None
        ones = jnp.ones((tk, LANE), BF16)
        m_ref[...] = jnp.full_like(m_ref, NEG)
        l_ref[...] = jnp.zeros_like(l_ref)
        acc_ref[...] = jnp.zeros_like(acc_ref)
        row = lax.broadcasted_iota(jnp.int32, (tq, tk), 0)
        col = lax.broadcasted_iota(jnp.int32, (tq, tk), 1)

        def absorb(h, j, base, r0, r1, subs, diagonal):
            zs = []
            for u in subs:
                z = _dot(qh[h][r0:r1], k_ref[pl.ds(base + u * tk, tk), :], "nt")
                if not fold:
                    z = z * scale
                if causal:
                    z = z + bias[h][r0:r1] - cr_ref[h, j * nsub + u]
                if diagonal:
                    z = jnp.where((col[r0:r1] + u * tk) <= row[r0:r1], z, NEG)
                zs.append(z)
            m_prev = m_ref[h, r0:r1, :]
            top = zs[0]
            for z in zs[1:]:
                top = jnp.maximum(top, z)
            m_new = jnp.maximum(m_prev, jnp.max(top, axis=1, keepdims=True))
            alpha = jnp.exp(m_prev - m_new)
            l_new = alpha * l_ref[h, r0:r1, :]
            out = alpha * acc_ref[h, r0:r1, :]
            m_wide = jnp.broadcast_to(m_new, top.shape)
            for u, z in zip(subs, zs):
                p = jnp.exp(z - m_wide).astype(BF16)
                l_new = l_new + jnp.dot(p, ones, preferred_element_type=F32)[:, 0:1]
                out = out + _dot(p, v_ref[pl.ds(base + u * tk, tk), :])
            l_ref[h, r0:r1, :] = l_new
            acc_ref[h, r0:r1, :] = out
            m_ref[h, r0:r1, :] = m_new

        def step(j, diagonal):
            base = _key_base(j, tq)
            for h in range(hpb):
                if diagonal and nsub % 2 == 0:
                    absorb(h, j, base, 0, tq // 2, range(nsub // 2), True)
                    absorb(h, j, base, tq // 2, tq, range(nsub), True)
                else:
                    absorb(h, j, base, 0, tq, range(nsub), diagonal)

        def below(j, carry):
            step(j, False)
            return carry

        if causal:
            lax.fori_loop(0, qi, below, 0)
            step(qi, True)
        else:
            step(0, False)
        o = _join_heads([acc_ref[h] / l_ref[h] for h in range(hpb)], d)
        o_ref[...] = o
        ob_ref[...] = o.astype(ob_ref.dtype)
        lse_ref[...] = m_ref[...] + jnp.log(l_ref[...])

    out = pl.BlockSpec((tq, LANE), lambda g, i: (i, g))
    col = pl.BlockSpec((hpb, tq, 1), lambda g, i: (g, i, 0))
    in_specs = [_lane_tile(tq, q[1], False), _lane_tile(Tk, k[1], True), _lane_tile(Tk, v[1], True)]
    args = [q[0], k[0], v[0]]
    if causal:
        in_specs += [col, pl.BlockSpec((hpb, T // tk, 1, tk), lambda g, i: (g, 0, 0, 0))]
        args += [c.reshape(H, T, 1), c.reshape(H, T // tk, 1, tk)]
    return pl.pallas_call(
        body, name=name, grid=(n_tiles, T // tq),
        in_specs=in_specs, out_specs=[out, out, col],
        out_shape=[jax.ShapeDtypeStruct((T, n_tiles * LANE), F32), jax.ShapeDtypeStruct((T, n_tiles * LANE), BF16),
                   jax.ShapeDtypeStruct((H, T, 1), F32)],
        scratch_shapes=[pltpu.VMEM((hpb, tq, 1), F32), pltpu.VMEM((hpb, tq, 1), F32),
                        pltpu.VMEM((hpb, tq, LANE), F32)],
        compiler_params=_cparams(("parallel", "arbitrary")),
    )(*args)


def _attn_bwd(name, q, k, v, o, do, lse, n_tiles, d, scale, c=None):
    T, Tk = q[0].shape[0], k[0].shape[0]
    hpb = LANE // d
    H = n_tiles * hpb
    causal = c is not None
    tq, tk, nsub = _att_tiles(T, Tk, causal)
    fold = _is_pow2(scale)

    def body(*refs):
        q_ref, k_ref, v_ref, o_ref, do_ref, lse_ref = refs[:6]
        cc_ref, cr_ref = refs[6:8] if causal else (None, None)
        n_out = 5 if causal else 3
        outs = refs[-(n_out + 3):-3]
        dq_ref, dk_ref, dv_ref = outs[:3]
        dc_ref, drow_ref = outs[3:5] if causal else (None, None)
        dk_acc, dv_acc, dq_acc = refs[-3:]
        qi = pl.program_id(1)

        @pl.when(qi == 0)
        def _():
            dk_acc[...] = jnp.zeros_like(dk_acc)
            dv_acc[...] = jnp.zeros_like(dv_acc)
            if causal:
                dc_ref[...] = jnp.zeros_like(dc_ref)

        qh = _per_head(q_ref[...] * scale if fold else q_ref[...], hpb, d)
        doh = _per_head(do_ref[...], hpb, d)
        delta_wide = [jnp.broadcast_to(jnp.sum(t.astype(F32) * o_ref[...], axis=1, keepdims=True), (tq, tk))
                      for t in doh]
        shift = [jnp.broadcast_to((cc_ref[h] - lse_ref[h]) if causal else -lse_ref[h], (tq, tk)) for h in range(hpb)]
        dq_acc[...] = jnp.zeros_like(dq_acc)
        if causal:
            drow_ref[...] = jnp.zeros_like(drow_ref)
        row = lax.broadcasted_iota(jnp.int32, (tq, tk), 0)
        col = lax.broadcasted_iota(jnp.int32, (tq, tk), 1)

        def step(j, diagonal):
            base = _key_base(j, tq)
            for h in range(hpb):
                dq, dsum = None, None
                for u in range(nsub):
                    ks = base + u * tk
                    r0 = u * tk if diagonal else 0
                    kv = k_ref[pl.ds(ks, tk), :]
                    z = _dot(qh[h][r0:], kv, "nt")
                    if not fold:
                        z = z * scale
                    z = z + shift[h][r0:]
                    if causal:
                        z = z - cr_ref[h, j * nsub + u]
                    if diagonal:
                        z = jnp.where((col[r0:] + u * tk) <= row[r0:], z, NEG)
                    p = jnp.exp(z)
                    ds = p * (_dot(doh[h][r0:], v_ref[pl.ds(ks, tk), :], "nt") - delta_wide[h][r0:])
                    term = _dot(ds, kv)
                    dk = _dot(ds, qh[h][r0:], "tn")
                    dk_acc[pl.ds(ks, tk), :] += dk if fold else dk * scale
                    dv_acc[pl.ds(ks, tk), :] += _dot(p, doh[h][r0:], "tn")
                    if causal:
                        dc_ref[h, j * nsub + u] -= jnp.sum(ds, axis=0, keepdims=True)
                    if diagonal:
                        dq_acc[h, r0:, :] += term
                        drow_ref[h, r0:, :] += jnp.sum(ds, axis=1, keepdims=True)
                    else:
                        dq = term if dq is None else dq + term
                        if causal:
                            dsum = ds if dsum is None else dsum + ds
                if not diagonal:
                    dq_acc[h] += dq
                    if causal:
                        drow_ref[h] += jnp.sum(dsum, axis=1, keepdims=True)

        def below(j, carry):
            step(j, False)
            return carry

        if causal:
            lax.fori_loop(0, qi, below, 0)
            step(qi, True)
        else:
            step(0, False)
        dq_ref[...] = (_join_heads([dq_acc[h] for h in range(hpb)], d) * scale).astype(dq_ref.dtype)

        @pl.when(qi == pl.num_programs(1) - 1)
        def _():
            dk_ref[...] = dk_acc[...].astype(dk_ref.dtype)
            dv_ref[...] = dv_acc[...].astype(dv_ref.dtype)

    blk = pl.BlockSpec((tq, LANE), lambda g, i: (i, g))
    full = pl.BlockSpec((Tk, LANE), lambda g, i: (0, g))
    col = pl.BlockSpec((hpb, tq, 1), lambda g, i: (g, i, 0))
    crow = pl.BlockSpec((hpb, T // tk, 1, tk), lambda g, i: (g, 0, 0, 0))
    in_specs = [_lane_tile(tq, q[1], False), _lane_tile(Tk, k[1], True), _lane_tile(Tk, v[1], True), blk, blk, col]
    args = [q[0], k[0], v[0], o, do, lse]
    out_specs = [blk, full, full]
    out_shape = [jax.ShapeDtypeStruct((T, n_tiles * LANE), BF16), jax.ShapeDtypeStruct((Tk, n_tiles * LANE), BF16),
                 jax.ShapeDtypeStruct((Tk, n_tiles * LANE), BF16)]
    if causal:
        in_specs += [col, crow]
        args += [c.reshape(H, T, 1), c.reshape(H, T // tk, 1, tk)]
        out_specs += [crow, col]
        out_shape += [jax.ShapeDtypeStruct((H, T // tk, 1, tk), F32), jax.ShapeDtypeStruct((H, T, 1), F32)]
    outs = pl.pallas_call(
        body, name=name, grid=(n_tiles, T // tq),
        in_specs=in_specs, out_specs=out_specs, out_shape=out_shape,
        scratch_shapes=[pltpu.VMEM((Tk, LANE), F32), pltpu.VMEM((Tk, LANE), F32), pltpu.VMEM((hpb, tq, LANE), F32)],
        compiler_params=_cparams(("parallel", "arbitrary")),
    )(*args)
    if causal:
        return outs[0], outs[1], outs[2], outs[3].reshape(H, T), outs[4].reshape(H, T)
    return outs


def _decay_fwd(fl, b):
    H, T = fl.shape
    tk = DECAY_TK

    def body(x_ref, b_ref, c_ref):
        upto = _tri(tk, lambda j, s: j <= s)
        carry = jnp.zeros((H, 1), F32)
        for i in range(T // tk):
            xv = x_ref[:, i * tk:(i + 1) * tk] + b_ref[...]
            lf = jnp.minimum(xv, 0.0) - jnp.log(1.0 + jnp.exp(-jnp.abs(xv)))
            pref = _dot_split(lf, upto, parts=3) + carry
            c_ref[:, i * tk:(i + 1) * tk] = pref
            carry = pref[:, tk - 1:tk]

    vm = pl.BlockSpec(memory_space=pltpu.VMEM)
    return pl.pallas_call(
        body, name="decay_fwd", in_specs=[vm, vm], out_specs=vm,
        out_shape=jax.ShapeDtypeStruct((H, T), F32),
    )(fl, b)


def _decay_bwd(dc_cols, dc_rows, fl, b):
    H, T = fl.shape
    tk = DECAY_TK

    def body(dc_ref, dr_ref, x_ref, b_ref, dx_ref, db_ref):
        from_ = _tri(tk, lambda j, s: j >= s)
        carry = jnp.zeros((H, 1), F32)
        total = jnp.zeros((H, 1), F32)
        for i in reversed(range(T // tk)):
            sl = slice(i * tk, (i + 1) * tk)
            suffix = _dot_split(dc_ref[:, sl] + dr_ref[:, sl], from_, parts=3) + carry
            xv = x_ref[:, sl] + b_ref[...]
            dx = suffix / (1.0 + jnp.exp(xv))
            dx_ref[:, sl] = dx
            total = total + jnp.sum(dx, axis=1, keepdims=True)
            carry = suffix[:, 0:1]
        db_ref[...] = jnp.broadcast_to(total, db_ref.shape)

    vm = pl.BlockSpec(memory_space=pltpu.VMEM)
    dx, db = pl.pallas_call(
        body, name="decay_bwd", in_specs=[vm, vm, vm, vm], out_specs=[vm, vm],
        out_shape=[jax.ShapeDtypeStruct((H, T), F32), jax.ShapeDtypeStruct((H, LANE), F32)],
    )(dc_cols, dc_rows, fl, b)
    return dx, db[:, 0]


def _place():
    x, y, c = lax.axis_index("x"), lax.axis_index("y"), lax.axis_index("c")
    return x, y, c, [(1 - x, y), (x, 1 - y), (1 - x, 1 - y)]


def _all_gather(name, block):
    R, C = block.shape

    def body(x_ref, out_ref, send_sems, recv_sems, local_sem):
        x, y, c, chips = _place()
        me, sibling = (x, y, c), (x, y, 1 - c)

        def rows(px, py, pc):
            return out_ref.at[4 * px + 2 * py + pc]

        def copy(k, blk, to, src=None):
            return pltpu.make_async_remote_copy(
                src_ref=rows(*blk) if src is None else src, dst_ref=rows(*blk),
                send_sem=send_sems.at[k], recv_sem=recv_sems.at[k], device_id=to, device_id_type=MESH)

        mine = pltpu.make_async_copy(x_ref, rows(*me), local_sem)
        mine.start()
        first = [copy(0, me, sibling, src=x_ref)]
        first += [copy(1 + j, me, (*chip, c), src=x_ref) for j, chip in enumerate(chips)]
        for cp in first:
            cp.start()
        passed = [copy(4 + j, (*chip, c), sibling) for j, chip in enumerate(chips)]
        for j, chip in enumerate(chips):
            copy(1 + j, (*chip, c), me).wait_recv()
            passed[j].start()
        copy(0, sibling, me).wait_recv()
        for j, chip in enumerate(chips):
            copy(4 + j, (*chip, 1 - c), me).wait_recv()
        for cp in first + passed:
            cp.wait_send()
        mine.wait()

    return pl.pallas_call(
        body, name=name, in_specs=[ANY], out_specs=ANY,
        out_shape=jax.ShapeDtypeStruct((N_DEV, R, C), block.dtype),
        scratch_shapes=[pltpu.SemaphoreType.DMA((7,)), pltpu.SemaphoreType.DMA((7,)), pltpu.SemaphoreType.DMA(())],
    )(block)


def _swap_with_sibling(name, parts):
    _, R, C = parts.shape

    def body(p_ref, out_ref, send_sems, recv_sems):
        x, y, c, _ = _place()
        copies = [pltpu.make_async_remote_copy(
            src_ref=p_ref.at[2 * q + (1 - c)], dst_ref=out_ref.at[q],
            send_sem=send_sems.at[q], recv_sem=recv_sems.at[q], device_id=(x, y, 1 - c), device_id_type=MESH)
            for q in range(4)]
        for cp in copies:
            cp.start()
        for cp in copies:
            cp.wait_recv()
        for cp in copies:
            cp.wait_send()

    return pl.pallas_call(
        body, name=name, in_specs=[ANY], out_specs=ANY,
        out_shape=jax.ShapeDtypeStruct((4, R, C), parts.dtype),
        scratch_shapes=[pltpu.SemaphoreType.DMA((4,)), pltpu.SemaphoreType.DMA((4,))],
    )(parts)


def _add_own(name, parts, got, tr=512):
    _, R, C = parts.shape
    tr = _tile(R, tr, SUBLANE_BF16)

    def body(c_ref, p_ref, g_ref, o_ref):
        o_ref[...] = (p_ref[...].astype(F32) + g_ref[...].astype(F32)).astype(o_ref.dtype)

    return pl.pallas_call(
        body, name=name,
        grid_spec=pltpu.PrefetchScalarGridSpec(
            num_scalar_prefetch=1, grid=(4, R // tr),
            in_specs=[pl.BlockSpec((1, tr, C), lambda q, i, c: (2 * q + c[0], i, 0)),
                      pl.BlockSpec((1, tr, C), lambda q, i, c: (q, i, 0))],
            out_specs=pl.BlockSpec((1, tr, C), lambda q, i, c: (q, i, 0))),
        out_shape=jax.ShapeDtypeStruct((4, R, C), parts.dtype),
        compiler_params=_cparams(("parallel", "parallel")),
    )(lax.axis_index("c").astype(jnp.int32).reshape(1), parts, got)


def _swap_with_chips(name, parts):
    _, R, C = parts.shape

    def body(p_ref, out_ref, send_sems, recv_sems, local_sem):
        x, y, c, chips = _place()
        my_chip = 2 * x + y
        mine = pltpu.make_async_copy(p_ref.at[my_chip], out_ref.at[my_chip], local_sem)
        mine.start()
        sends = [pltpu.make_async_remote_copy(
            src_ref=p_ref.at[2 * cx + cy], dst_ref=out_ref.at[my_chip],
            send_sem=send_sems.at[j], recv_sem=recv_sems.at[j], device_id=(cx, cy, c), device_id_type=MESH)
            for j, (cx, cy) in enumerate(chips)]
        for cp in sends:
            cp.start()
        for j, (cx, cy) in enumerate(chips):
            pltpu.make_async_remote_copy(
                src_ref=p_ref.at[my_chip], dst_ref=out_ref.at[2 * cx + cy],
                send_sem=send_sems.at[j], recv_sem=recv_sems.at[j], device_id=(cx, cy, c), device_id_type=MESH,
            ).wait_recv()
        for cp in sends:
            cp.wait_send()
        mine.wait()

    return pl.pallas_call(
        body, name=name, in_specs=[ANY], out_specs=ANY,
        out_shape=jax.ShapeDtypeStruct((4, R, C), parts.dtype),
        scratch_shapes=[pltpu.SemaphoreType.DMA((3,)), pltpu.SemaphoreType.DMA((3,)), pltpu.SemaphoreType.DMA(())],
    )(parts)


def _sum_parts(name, parts, tr=512):
    P, R, C = parts.shape
    tr = _tile(R, tr, SUBLANE_BF16)

    def body(p_ref, o_ref):
        total = p_ref[0].astype(F32)
        for p in range(1, P):
            total = total + p_ref[p].astype(F32)
        o_ref[...] = total

    return pl.pallas_call(
        body, name=name, grid=(R // tr,),
        in_specs=[pl.BlockSpec((P, tr, C), lambda i: (0, i, 0))], out_specs=pl.BlockSpec((tr, C), lambda i: (i, 0)),
        out_shape=jax.ShapeDtypeStruct((R, C), F32),
        compiler_params=_cparams(("parallel",)),
    )(parts)


_HBM = pl.BlockSpec(memory_space=pltpu.HBM)
_SEM = pl.BlockSpec(memory_space=pltpu.SEMAPHORE)
_EFFECT = pltpu.SideEffectType.DATAFLOW_SIDE_EFFECTING


def _flipped(x, y, c, k):
    px, py, pc = (1 - x if k & 4 else x), (1 - y if k & 2 else y), (1 - c if k & 1 else c)
    return (px, py, pc), 4 * px + 2 * py + pc


def _exchange_start(name, src, per_peer):
    R, C = src.shape[-2:]

    def body(v_ref, land_ref, send_sem, recv_sem, v_thru, land_thru, token):
        x, y, c = lax.axis_index("x"), lax.axis_index("y"), lax.axis_index("c")
        me = 4 * x + 2 * y + c
        for k in range(1, N_DEV):
            peer, idx = _flipped(x, y, c, k)
            pltpu.make_async_remote_copy(
                src_ref=v_ref.at[idx] if per_peer else v_ref, dst_ref=land_ref.at[me],
                send_sem=send_sem, recv_sem=recv_sem, device_id=peer, device_id_type=MESH).start()
        token[...] = jnp.zeros_like(token)

    return pl.pallas_call(
        body, name=name,
        out_shape=(pltpu.SemaphoreType.DMA(()), pltpu.SemaphoreType.DMA(()), pltpu.HBM(src.shape, src.dtype),
                   pltpu.HBM((N_DEV, R, C), src.dtype), jax.ShapeDtypeStruct((8, LANE), F32)),
        in_specs=(_HBM, _HBM), out_specs=(_SEM, _SEM, _HBM, _HBM, pl.BlockSpec(memory_space=pltpu.VMEM)),
        input_output_aliases={0: 2, 1: 3},
        compiler_params=pltpu.CompilerParams(has_side_effects=_EFFECT),
    )(pltpu.with_memory_space_constraint(src, pltpu.HBM),
      pltpu.with_memory_space_constraint(lax.empty((N_DEV, R, C), src.dtype), pltpu.HBM))


def _exchange_wait(name, started, after):
    send_sem, recv_sem, v_thru, land_thru, _ = started

    def body(v_ref, land_ref, send_sem, recv_sem, after_ref, v_dead, got_ref):
        x, y, c = lax.axis_index("x"), lax.axis_index("y"), lax.axis_index("c")
        seven = land_ref.at[pl.ds(0, N_DEV - 1)]
        drain = pltpu.make_async_remote_copy(
            src_ref=seven, dst_ref=seven, send_sem=send_sem, recv_sem=recv_sem,
            device_id=(x, y, c), device_id_type=MESH)
        drain.wait_send()
        drain.wait_recv()

    return pl.pallas_call(
        body, name=name,
        out_shape=(pltpu.HBM(v_thru.shape, v_thru.dtype), pltpu.HBM(land_thru.shape, land_thru.dtype)),
        in_specs=(_HBM, _HBM, _SEM, _SEM, ANY), out_specs=(_HBM, _HBM), input_output_aliases={0: 0, 1: 1},
        compiler_params=pltpu.CompilerParams(has_side_effects=_EFFECT),
    )(v_thru, land_thru, send_sem, recv_sem, after)


def _my_index():
    return 4 * lax.axis_index("x") + 2 * lax.axis_index("y") + lax.axis_index("c")


def _sum_landed(name, landed, parts, tr=512):
    P, R, C = landed.shape
    tr = _tile(R, tr, SUBLANE_BF16)

    def body(me_ref, l_ref, own_ref, o_ref):
        total = None
        for s in range(P):
            part = jnp.where(me_ref[0] == s, own_ref[0], l_ref[s]).astype(F32)
            total = part if total is None else total + part
        o_ref[...] = total

    return pl.pallas_call(
        body, name=name,
        grid_spec=pltpu.PrefetchScalarGridSpec(
            num_scalar_prefetch=1, grid=(R // tr,),
            in_specs=[pl.BlockSpec((P, tr, C), lambda i, me: (0, i, 0)),
                      pl.BlockSpec((1, tr, C), lambda i, me: (me[0], i, 0))],
            out_specs=pl.BlockSpec((tr, C), lambda i, me: (i, 0))),
        out_shape=jax.ShapeDtypeStruct((R, C), F32),
        compiler_params=_cparams(("parallel",)),
    )(_my_index().astype(jnp.int32).reshape(1), landed, parts)


def _after(params, name, token):
    return {**params, name: params[name] + token[0, 0]}


def _reduce_scatter(tag, parts):
    got = _swap_with_sibling("rs_pair_" + tag, parts)
    pair = _add_own("rs_add_" + tag, parts, got)
    quad = _swap_with_chips("rs_chips_" + tag, pair)
    return _sum_parts("rs_sum_" + tag, quad)


def _adamw(name, g_parts, w, m, v, tr=512):
    P, R, C = g_parts.shape
    tr = _tile(R, tr, 8)

    def body(g_ref, w_ref, m_ref, v_ref, go_ref, d_ref, mo_ref, vo_ref):
        g = g_ref[0]
        for p in range(1, P):
            g = g + g_ref[p]
        mn = ADAM_B1 * m_ref[...] + (1.0 - ADAM_B1) * g
        vn = ADAM_B2 * v_ref[...] + (1.0 - ADAM_B2) * (g * g)
        m_hat = mn / (1.0 - ADAM_B1 ** ADAM_STEP)
        v_hat = vn / (1.0 - ADAM_B2 ** ADAM_STEP)
        go_ref[...] = g
        d_ref[...] = -ADAM_LR * (m_hat / (jnp.sqrt(v_hat) + ADAM_EPS) + ADAM_WD * w_ref[...])
        mo_ref[...] = mn
        vo_ref[...] = vn

    row = pl.BlockSpec((tr, C), lambda i: (i, 0))
    return pl.pallas_call(
        body, name=name, grid=(R // tr,),
        in_specs=[pl.BlockSpec((P, tr, C), lambda i: (0, i, 0)), row, row, row], out_specs=[row] * 4,
        out_shape=[jax.ShapeDtypeStruct((R, C), F32)] * 4,
        compiler_params=_cparams(("parallel",)),
    )(g_parts, w, m, v)


def _pad_rows(t, rows):
    return jnp.pad(t, ((0, rows - t.shape[0]), (0, 0)))


class _Layout:
    def __init__(self, D, ff_shard, in_shard, kv_shard, gate_shard, br_in, br_shard, out_shard):
        self.D = D
        self.in_shard = in_shard
        self.in_pad = -(-in_shard // LANE) * LANE
        self.in_cols = -(-N_DEV * in_shard // IN_TILE) * IN_TILE
        self.br_in, self.br_shard = br_in, br_shard
        br_rows = br_shard * br_in // D
        sizes = [("g1", ff_shard), ("u1", ff_shard), ("d1", ff_shard), ("win", self.in_pad), ("kv", kv_shard),
                 ("gate", gate_shard), ("br", br_rows), ("out", out_shard),
                 ("g2", ff_shard), ("u2", ff_shard), ("d2", ff_shard)]
        self.seg, off = {}, 0
        for key, n in sizes:
            assert n % SUBLANE_BF16 == 0, (key, n)
            self.seg[key] = (off, n)
            off += n
        self.rows = off

    def pack(self, parts):
        return jnp.concatenate([parts[key] for key in self.seg], axis=0)

    def take(self, gathered, key, own=None):
        off, n = self.seg[key]
        seg = gathered[:, off:off + n, :]
        if own is not None:
            seg = lax.dynamic_update_slice(seg, own[0][off:off + n][None], (own[1], 0, 0))
        return seg.reshape(N_DEV * n, self.D)

    def spread(self, full, key):
        _, n = self.seg[key]
        return full.reshape(N_DEV, n, self.D)


def _pack_layer(lay, l, p):
    D = lay.D
    br = jnp.concatenate([p["w_br_sb"][l], p["w_br_fox"][l], p["w_br_mem"][l]], axis=0)
    parts = {
        "g1": p["ffn1_w_gate"][l].T, "u1": p["ffn1_w_up"][l].T, "d1": p["ffn1_w_down"][l],
        "win": _pad_rows(p["w_in"][l].T, lay.in_pad), "kv": p["w_mem_kv"][l], "gate": p["w_gate"][l].T,
        "br": br.T.reshape(-1, D), "out": p["w_out"][l],
        "g2": p["ffn2_w_gate"][l].T, "u2": p["ffn2_w_up"][l].T, "d2": p["ffn2_w_down"][l],
    }
    return lay.pack({k: t.astype(BF16) for k, t in parts.items()})


def _align_win(lay, packed):
    D = lay.D
    real = packed.reshape(N_DEV, lay.in_pad, D)[:, :lay.in_shard].reshape(N_DEV * lay.in_shard, D)
    rows = jnp.concatenate([real[:_QKV_W], real[_QKV_W + N_FOX_HEADS:], real[_QKV_W:_QKV_W + N_FOX_HEADS]], axis=0)
    return _pad_rows(rows, lay.in_cols)


def _unalign_win(lay, aligned):
    D = lay.D
    n_real = N_DEV * lay.in_shard
    mem_w = n_real - _QKV_W - N_FOX_HEADS
    real = jnp.concatenate([aligned[:_QKV_W], aligned[_QKV_W + mem_w:n_real], aligned[_QKV_W:_QKV_W + mem_w]], axis=0)
    real = real.reshape(N_DEV, lay.in_shard, D)
    return jnp.pad(real, ((0, 0), (0, lay.in_pad - lay.in_shard), (0, 0))).reshape(N_DEV * lay.in_pad, D)


def _unpack_layer(lay, gathered, own=None):
    D = lay.D
    w = {k: lay.take(gathered, k, own) for k in ("g1", "u1", "d1", "kv", "out", "g2", "u2", "d2")}
    w["win"] = _align_win(lay, lay.take(gathered, "win", own))
    fl0 = N_DEV * lay.in_shard - N_FOX_HEADS
    w["wfl"] = w["win"][fl0:fl0 + LANE]
    gate = lay.take(gathered, "gate", own)
    w["gate"] = gate
    w["gate3"] = [gate[i * D:(i + 1) * D] for i in range(3)]
    br = lay.take(gathered, "br", own).reshape(N_DEV * lay.br_shard, lay.br_in)
    third = lay.br_in // 3
    w["br3"] = [br[:, i * third:(i + 1) * third] for i in range(3)]
    return w


def _silu_mul(accs, _):
    a, b = accs
    return [a, b, a * jax.nn.sigmoid(a) * b]


def _act_bwd(accs, extras):
    ds, (a, b) = accs[0], [e.astype(F32) for e in extras]
    sig = jax.nn.sigmoid(a)
    return [ds * b * (sig * (1.0 + a * (1.0 - sig))), ds * (a * sig)]


def _res_norm(scale):
    def epilogue(accs, extras):
        f, res, g = accs[0], extras[0], extras[1]
        out = res + scale * ((f * _rstd(f)) * g)
        return [f, out] + [(out * _rstd(out)) * g_next for g_next in extras[2:]]
    return epilogue


def _down_proj(name, x, w, res, g, scale, next_g):
    D = res.shape[1]
    extras = [(res, 0), (g.reshape(1, D), 0)] + ([(next_g.reshape(1, D), 0)] if next_g is not None else [])
    outs = _mm(name, [(x, w)], "nn", [F32, F32] + [BF16] * (len(extras) - 2), _res_norm(scale), extras, tn=D)
    return outs[0], outs[1], (outs[2] if next_g is not None else None)


def _norm_bwd(accs, extras):
    dy, (x, res, g) = _sum_accs(accs, None)[0], extras
    r = _rstd(x)
    xhat = x * r
    gy = dy * g
    dx = res + r * (gy - xhat * jnp.mean(gy * xhat, axis=-1, keepdims=True))
    part = jnp.sum(dy * xhat, axis=0, keepdims=True)
    first = lax.broadcasted_iota(jnp.int32, (8, part.shape[1]), 0) == 0
    return [dx, jnp.where(first, part, 0.0)]


def _ffn_fwd(tag, h, n, post_g, wg, wu, wd, next_g):
    a, b, s = _mm("ffn_up_" + tag, [(n, wg), (n, wu)], "nt", [BF16, BF16, BF16], _silu_mul, tn=1408)
    f, out, n_next = _down_proj("ffn_down_" + tag, s, wd, h, post_g, 0.5, next_g)
    return out, n_next, (h, n, a, b, s, f)


def _ffn_bwd(tag, dh, saved, pre_g, post_g, wg, wu, wd):
    h, n, a, b, s, f = saved
    D = h.shape[1]
    df, d_post = _rms_bwd("ffn_dout_" + tag, f, post_g, dh, BF16, scale=0.5)
    da, db = _mm("ffn_dact_" + tag, [(df, wd)], "nt", [BF16, BF16], _act_bwd, [(a, 0), (b, 0)], tn=1408)
    d_wd = _mm("ffn_dwd_" + tag, [(s, df)], "tn", [BF16], tm=256)
    dh_in, d_pre_rows = _mm("ffn_dn_" + tag, [(da, wg), (db, wu)], "nn", [F32, F32], _norm_bwd,
                            [(h, 0), (dh, 0), (pre_g.reshape(1, D), 0)], tm=256, tn=D, out_rows=[None, 8])
    d_pre = _colsum("ffn_dpre_" + tag, d_pre_rows)
    d_wg = _mm("ffn_dwg_" + tag, [(da, n)], "tn", [BF16], tm=256)
    d_wu = _mm("ffn_dwu_" + tag, [(db, n)], "tn", [BF16], tm=256)
    return dh_in, d_pre, d_post, d_wg, d_wu, d_wd


_SB_W = N_SB_HEADS * HEAD_DIM
_FOX_W = N_FOX_HEADS * HEAD_DIM
_QKV_W = 3 * _SB_W + 3 * _FOX_W


def _gate_act(accs, extras):
    return [jax.nn.sigmoid(accs[0] + extras[0])]


def _merge(accs, extras):
    g = [e.astype(F32) for e in extras]
    return [g[0] * accs[0] + g[1] * accs[1] + g[2] * accs[2]]


def _merge_bwd(accs, extras):
    dm = accs[0]
    g = [e.astype(F32) for e in extras]
    d_branch = [dm * gi for gi in g]
    d_gate = [dm * bi * gi * (1.0 - gi) for bi, gi in zip(accs[1:], g)]
    return d_branch + d_gate


def _mix_tiles(lay):
    sb, fx = _SB_W // LANE, _FOX_W // LANE
    mem_w = N_DEV * lay.in_shard - _QKV_W - N_FOX_HEADS
    return (0, sb, 2 * sb, sb), (3 * sb, 3 * sb + fx, 3 * sb + 2 * fx, fx), (_QKV_W // LANE, mem_w // LANE)


def _mix_fwd(lay, h, u, w, post_g, b_forget, b_gate, mem_n, next_g):
    D = lay.D
    (sq, sk, sv, sn), (fq, fk, fv, fn), (mq, mn) = _mix_tiles(lay)
    mem_d = mn * LANE // N_MEM_HEADS
    proj = _mm("mix_in", [(u, w["win"])], "nt", [BF16], tm=1024, tn=IN_TILE)
    fl = _mm("mix_fl", [(u, w["wfl"])], "nt", [F32])[:, :N_FOX_HEADS].T
    c = _decay_fwd(fl, b_forget.reshape(-1, 1))
    o_sb, rtot = _sb_fwd((proj, sq), (proj, sk), (proj, sv), sn, HEAD_DIM, HEAD_DIM ** -0.5)
    o_fx32, o_fx, lse_fx = _attn_fwd("fox_fwd", (proj, fq), (proj, fk), (proj, fv), fn, HEAD_DIM,
                                     HEAD_DIM ** -0.5, c)
    kvm = _mm("mem_kv", [(mem_n, w["kv"])], "nn", [BF16])
    o_mem32, o_mem, lse_mem = _attn_fwd("mem_fwd", (proj, mq), (kvm, 0), (kvm, mn), mn, mem_d, mem_d ** -0.5)
    gates = _mm("mix_gate", [(u, w["gate"])], "nt", [BF16], _gate_act, [(b_gate.reshape(1, -1), 0)], tm=1024)
    flat = [o_sb, o_fx, o_mem]
    merged = _mm("mix_merge", list(zip(flat, w["br3"])), "nt", [BF16], _merge,
                 [(gates, 0), (gates, D), (gates, 2 * D)])
    z, out, n_next = _down_proj("mix_out", merged, w["out"], h, post_g, 1.0, next_g)
    saved = (h, u, proj, fl, c, rtot, o_fx32, lse_fx, kvm, o_mem32, lse_mem, gates, flat, merged, z)
    return out, n_next, saved


def _mix_bwd(lay, dh, saved, w, pre_g, post_g, b_forget, mem_n, dmem_n):
    D = lay.D
    (sq, sk, sv, sn), (fq, fk, fv, fn), (mq, mn) = _mix_tiles(lay)
    mem_d = mn * LANE // N_MEM_HEADS
    h, u, proj, fl, c, rtot, o_fx32, lse_fx, kvm, o_mem32, lse_mem, gates, flat, merged, z = saved
    dz, d_post = _rms_bwd("mix_dres", z, post_g, dh, BF16)
    outs = _mm("mix_dmerge", [(dz, w["out"])] + list(zip(flat, w["br3"])), "nt", [BF16] * 6, _merge_bwd,
               [(gates, 0), (gates, D), (gates, 2 * D)], tn=512)
    d_branch, d_gate = outs[:3], outs[3:]
    d_wout = _mm("mix_dwout", [(merged, dz)], "tn", [BF16])
    d_o = [_mm("mix_dbr%d" % i, [(d_branch[i], w["br3"][i])], "nn", [BF16]) for i in range(3)]
    d_wbr = [_mm("mix_dwbr%d" % i, [(d_branch[i], flat[i])], "tn", [BF16]) for i in range(3)]
    d_bgate = jnp.concatenate([_colsum("mix_dbgate%d" % i, d_gate[i]) for i in range(3)])
    d_wgate = [_mm("mix_dwgate%d" % i, [(d_gate[i], u)], "tn", [BF16]) for i in range(3)]

    d_sb = _sb_bwd((proj, sq), (proj, sk), (proj, sv), d_o[0], rtot, sn, HEAD_DIM, HEAD_DIM ** -0.5)
    *d_fx, dc, dc_rows = _attn_bwd("fox_bwd", (proj, fq), (proj, fk), (proj, fv), o_fx32, d_o[1], lse_fx, fn,
                                   HEAD_DIM, HEAD_DIM ** -0.5, c)
    dq_m, dk_m, dv_m = _attn_bwd("mem_bwd", (proj, mq), (kvm, 0), (kvm, mn), o_mem32, d_o[2], lse_mem, mn,
                                 mem_d, mem_d ** -0.5)
    dfl, d_bforget = _decay_bwd(dc, dc_rows, fl, b_forget.reshape(-1, 1))
    pieces = list(d_sb) + list(d_fx) + [dq_m]
    dflp = jnp.pad(dfl.T.astype(BF16), ((0, 0), (0, LANE - dfl.shape[0])))
    offs = [sum(t.shape[1] for t in pieces[:i]) for i in range(len(pieces) + 1)]
    win_rows = [w["win"][offs[i]:offs[i + 1]] for i in range(len(pieces))]
    dh_in, d_pre_rows = _mm(
        "mix_du", list(zip(d_gate, w["gate3"])) + list(zip(pieces, win_rows)) + [(dflp, w["wfl"])], "nn",
        [F32, F32], _norm_bwd, [(h, 0), (dh, 0), (pre_g.reshape(1, D), 0)], tm=256, tn=D, out_rows=[None, 8])
    d_pre = _colsum("mix_dpre", d_pre_rows)
    d_rows = [_mm("mix_dwin%d" % i, [(t, u)], "tn", [BF16]) for i, t in enumerate(pieces)]
    d_wfl = _mm("mix_dwfl", [(dflp, u)], "tn", [BF16])
    d_win = _unalign_win(lay, _pad_rows(jnp.concatenate(list(d_rows) + [d_wfl], axis=0), lay.in_cols))

    dkvm = jnp.concatenate([dk_m, dv_m], axis=1)
    d_wkv = _mm("mem_dwkv", [(mem_n, dkvm)], "tn", [BF16])
    dmem_n = _mm("mem_dn", [(dkvm, w["kv"])], "nt", [F32], lambda accs, ex: [accs[0] + ex[0]], [(dmem_n, 0)])
    grads = {"win": d_win, "kv": d_wkv, "gate": jnp.concatenate(d_wgate, axis=0),
             "br": jnp.concatenate(d_wbr, axis=1), "out": d_wout}
    return dh_in, d_pre, d_post, d_bforget, d_bgate, grads, dmem_n


def _layer_fwd(lay, h, n, w, sp, mem_n, next_g):
    h1, u, s1 = _ffn_fwd("1", h, n, sp["ffn1_post_g"], w["g1"], w["u1"], w["d1"], sp["mix_pre_g"])
    h2, n2, s2 = _mix_fwd(lay, h1, u, w, sp["mix_post_g"], sp["b_forget"], sp["b_gate"], mem_n, sp["ffn2_pre_g"])
    h3, n_next, s3 = _ffn_fwd("2", h2, n2, sp["ffn2_post_g"], w["g2"], w["u2"], w["d2"], next_g)
    return h3, n_next, (s1, s2, s3)


def _layer_bwd(lay, dh, saved, w, sp, mem_n, dmem_n):
    s1, s2, s3 = saved
    dh, d_pre2, d_post2, d_g2, d_u2, d_d2 = _ffn_bwd("2", dh, s3, sp["ffn2_pre_g"], sp["ffn2_post_g"],
                                                     w["g2"], w["u2"], w["d2"])
    dh, d_mpre, d_mpost, d_bforget, d_bgate, g, dmem_n = _mix_bwd(
        lay, dh, s2, w, sp["mix_pre_g"], sp["mix_post_g"], sp["b_forget"], mem_n, dmem_n)
    dh, d_pre1, d_post1, d_g1, d_u1, d_d1 = _ffn_bwd("1", dh, s1, sp["ffn1_pre_g"], sp["ffn1_post_g"],
                                                     w["g1"], w["u1"], w["d1"])
    g.update({"g1": d_g1, "u1": d_u1, "d1": d_d1, "g2": d_g2, "u2": d_u2, "d2": d_d2})
    g["br"] = g["br"].reshape(N_DEV, lay.br_shard, lay.br_in).reshape(-1, lay.D)
    packed = jnp.concatenate([lay.spread(g[key], key) for key in lay.seg], axis=1)
    small = {"ffn1_pre_g": d_pre1, "ffn1_post_g": d_post1, "mix_pre_g": d_mpre, "mix_post_g": d_mpost,
             "ffn2_pre_g": d_pre2, "ffn2_post_g": d_post2, "b_gate": d_bgate, "b_forget": d_bforget}
    return dh, packed, small, dmem_n


_SHARDED = ["ffn1_w_gate", "ffn1_w_up", "ffn1_w_down", "w_in", "w_mem_kv", "w_gate", "w_br_sb", "w_br_fox",
            "w_br_mem", "w_out", "ffn2_w_gate", "ffn2_w_up", "ffn2_w_down"]
_SMALL_LAYER = ["ffn1_pre_g", "ffn1_post_g", "mix_pre_g", "mix_post_g", "ffn2_pre_g", "ffn2_post_g", "b_gate",
                "b_forget"]
_WEIGHTS = ["ffn1_pre_g", "ffn1_post_g", "ffn1_w_gate", "ffn1_w_up", "ffn1_w_down", "mix_pre_g", "mix_post_g",
            "w_in", "b_forget", "mem_norm_g", "w_mem_kv", "w_gate", "b_gate", "w_br_sb", "w_br_fox", "w_br_mem",
            "w_out", "ffn2_pre_g", "ffn2_post_g", "ffn2_w_gate", "ffn2_w_up", "ffn2_w_down"]


def _pack_small(vals, L, D):
    rows = []
    for l in range(L):
        for name in _SMALL_LAYER:
            t = vals[name][l]
            rows.append(jnp.pad(t, (0, -t.shape[0] % D)).reshape(-1, D))
    rows.append(vals["mem_norm_g"].reshape(1, D))
    packed = jnp.concatenate(rows, axis=0)
    return _pad_rows(packed, -(-packed.shape[0] // 8) * 8)


def _unpack_small(packed, shapes, L, D):
    out = {name: [] for name in _SMALL_LAYER}
    r = 0
    for l in range(L):
        for name in _SMALL_LAYER:
            n = shapes[name][1]
            nr = -(-n // D)
            out[name].append(packed[r:r + nr].reshape(-1)[:n])
            r += nr
    res = {name: jnp.stack(v) for name, v in out.items()}
    res["mem_norm_g"] = packed[r]
    return res


def _unpack_grads(lay, g):
    def seg(key):
        off, n = lay.seg[key]
        return g[off:off + n]
    br = seg("br").reshape(lay.br_shard, lay.br_in).T
    third = lay.br_in // 3
    return {
        "ffn1_w_gate": seg("g1").T, "ffn1_w_up": seg("u1").T, "ffn1_w_down": seg("d1"),
        "w_in": seg("win")[:lay.in_shard].T, "w_mem_kv": seg("kv"), "w_gate": seg("gate").T,
        "w_br_sb": br[:third], "w_br_fox": br[third:2 * third], "w_br_mem": br[2 * third:],
        "w_out": seg("out"), "ffn2_w_gate": seg("g2").T, "ffn2_w_up": seg("u2").T, "ffn2_w_down": seg("d2"),
    }


class _Exchanges:
    def gather(self, name, block):
        return _all_gather(name, block)

    def gather_start(self, block):
        return _exchange_start("ag_start", block, per_peer=False)

    def gather_wait(self, started, after):
        block, landed = _exchange_wait("ag_wait", started, after)
        return landed, (block, _my_index())

    def scatter(self, parts):
        return _reduce_scatter("w", parts)

    def scatter_start(self, parts):
        return _exchange_start("rs_start", parts, per_peer=True)

    def scatter_wait(self, started, after):
        parts, landed = _exchange_wait("rs_wait", started, after)
        return _sum_landed("rs_sum8", landed, parts)

    def token(self, started):
        return started[4]

    def loss_sum(self, part):
        return lax.psum(part, ("x", "y", "c"))


def _step(p, m, v, x, mem, tgt, ex):
    L, D = p["ffn1_pre_g"].shape
    lay = _Layout(D, p["ffn1_w_gate"].shape[2], p["w_in"].shape[2], p["w_mem_kv"].shape[1], p["w_gate"].shape[2],
                  3 * p["w_br_sb"].shape[1], p["w_br_sb"].shape[2], p["w_out"].shape[1])
    blocks = [_pack_layer(lay, l, p) for l in range(L)]
    sps = [{name: p[name][l] for name in _SMALL_LAYER} for l in range(L)]

    mem_n = _rms_fwd("mem_norm", mem, p["mem_norm_g"], BF16)
    gathered, own = ex.gather("ag_weights", blocks[0]), None
    h, saved, ws = x, [], []
    n = _rms_fwd("first_norm", x, sps[0]["ffn1_pre_g"], BF16)
    for l in range(L):
        if l + 1 < L:
            nxt, gathered = lax.optimization_barrier((blocks[l + 1], gathered))
            started = ex.gather_start(nxt)
            sp = _after(sps[l], "ffn1_post_g", ex.token(started))
        else:
            sp = sps[l]
        ws.append(_unpack_layer(lay, gathered, own))
        h, n, s = _layer_fwd(lay, h, n, ws[l], sp, mem_n, sps[l + 1]["ffn1_pre_g"] if l + 1 < L else None)
        saved.append(s)
        if l + 1 < L:
            gathered, own = ex.gather_wait(started, h)
    loss_part, dh = _loss_grad(h, tgt)
    loss = ex.loss_sum(loss_part)

    dmem_n = jnp.zeros(mem.shape, F32)
    big, small = [None] * L, {name: [None] * L for name in _SMALL_LAYER}
    flying, token = {}, None
    for l in reversed(range(L)):
        sp = sps[l] if token is None else _after(sps[l], "ffn2_post_g", token)
        dh, packed, sm, dmem_n = _layer_bwd(lay, dh, saved[l], ws[l], sp, mem_n, dmem_n)
        if l > 0:
            flying[l] = ex.scatter_start(packed)
            token = ex.token(flying[l])
        else:
            big[l] = _unpack_grads(lay, ex.scatter(packed))
        for name in _SMALL_LAYER:
            small[name][l] = sm[name]
    for l, started in flying.items():
        big[l] = _unpack_grads(lay, ex.scatter_wait(started, dh))
    _, d_memg = _rms_bwd("mem_dnorm", mem, p["mem_norm_g"], dmem_n, F32)

    small_g = {name: jnp.stack(vs) for name, vs in small.items()}
    small_g["mem_norm_g"] = d_memg
    small_names = _SMALL_LAYER + ["mem_norm_g"]
    shapes = {name: p[name].shape for name in small_names}
    g_all = ex.gather("ag_small", _pack_small(small_g, L, D))
    packs = [_pack_small({name: t[name] for name in small_names}, L, D) for t in (p, m, v)]
    res = [_unpack_small(t, shapes, L, D) for t in _adamw("adamw_small", g_all, *packs)]

    out = {kind: {} for kind in ("grad", "delta", "new_m", "new_v")}
    for name in small_names:
        for kind, r in zip(("grad", "delta", "new_m", "new_v"), res):
            out[kind][name] = r[name].reshape(p[name].shape)
    for name in _SHARDED:
        g = jnp.stack([big[l][name] for l in range(L)])
        shp = g.shape
        flat = lambda t: t.reshape(-1, shp[-1])
        r = _adamw("adamw_" + name, flat(g)[None], flat(p[name]), flat(m[name]), flat(v[name]))
        for kind, t in zip(("grad", "delta", "new_m", "new_v"), r):
            out[kind][name] = t.reshape(shp)
    return loss, dh, out


def kernel(x, mem, ffn1_pre_g, ffn1_post_g, ffn1_w_gate, ffn1_w_up, ffn1_w_down, mix_pre_g, mix_post_g, w_in, b_forget, mem_norm_g, w_mem_kv, w_gate, b_gate, w_br_sb, w_br_fox, w_br_mem, w_out, ffn2_pre_g, ffn2_post_g, ffn2_w_gate, ffn2_w_up, ffn2_w_down, loss_target, m_ffn1_pre_g, m_ffn1_post_g, m_ffn1_w_gate, m_ffn1_w_up, m_ffn1_w_down, m_mix_pre_g, m_mix_post_g, m_w_in, m_b_forget, m_mem_norm_g, m_w_mem_kv, m_w_gate, m_b_gate, m_w_br_sb, m_w_br_fox, m_w_br_mem, m_w_out, m_ffn2_pre_g, m_ffn2_post_g, m_ffn2_w_gate, m_ffn2_w_up, m_ffn2_w_down, v_ffn1_pre_g, v_ffn1_post_g, v_ffn1_w_gate, v_ffn1_w_up, v_ffn1_w_down, v_mix_pre_g, v_mix_post_g, v_w_in, v_b_forget, v_mem_norm_g, v_w_mem_kv, v_w_gate, v_b_gate, v_w_br_sb, v_w_br_fox, v_w_br_mem, v_w_out, v_ffn2_pre_g, v_ffn2_post_g, v_ffn2_w_gate, v_ffn2_w_up, v_ffn2_w_down):
    p = dict(zip(_WEIGHTS, (ffn1_pre_g, ffn1_post_g, ffn1_w_gate, ffn1_w_up, ffn1_w_down, mix_pre_g, mix_post_g, w_in, b_forget, mem_norm_g, w_mem_kv, w_gate, b_gate, w_br_sb, w_br_fox, w_br_mem, w_out, ffn2_pre_g, ffn2_post_g, ffn2_w_gate, ffn2_w_up, ffn2_w_down)))
    m = dict(zip(_WEIGHTS, (m_ffn1_pre_g, m_ffn1_post_g, m_ffn1_w_gate, m_ffn1_w_up, m_ffn1_w_down, m_mix_pre_g, m_mix_post_g, m_w_in, m_b_forget, m_mem_norm_g, m_w_mem_kv, m_w_gate, m_b_gate, m_w_br_sb, m_w_br_fox, m_w_br_mem, m_w_out, m_ffn2_pre_g, m_ffn2_post_g, m_ffn2_w_gate, m_ffn2_w_up, m_ffn2_w_down)))
    v = dict(zip(_WEIGHTS, (v_ffn1_pre_g, v_ffn1_post_g, v_ffn1_w_gate, v_ffn1_w_up, v_ffn1_w_down, v_mix_pre_g, v_mix_post_g, v_w_in, v_b_forget, v_mem_norm_g, v_w_mem_kv, v_w_gate, v_b_gate, v_w_br_sb, v_w_br_fox, v_w_br_mem, v_w_out, v_ffn2_pre_g, v_ffn2_post_g, v_ffn2_w_gate, v_ffn2_w_up, v_ffn2_w_down)))
    loss, dx, out = _step(p, m, v, x[0], mem[0], loss_target[0], _Exchanges())
    return (loss, dx[None], *[out["grad"][n] for n in _WEIGHTS], *[out["delta"][n] for n in _WEIGHTS],
            *[out["new_m"][n] for n in _WEIGHTS], *[out["new_v"][n] for n in _WEIGHTS])
```

```python
import functools
import math

import jax
import jax.numpy as jnp
from jax import lax
from jax.experimental import pallas as pl
from jax.experimental.pallas import tpu as pltpu

F32 = jnp.float32
BF16 = jnp.bfloat16

LANE = 128
SUBLANE_BF16 = 16
VMEM_LIMIT = 56 * 1024 * 1024
N_DEV = 8
MESH = pl.DeviceIdType.MESH
ANY = pl.BlockSpec(memory_space=pl.ANY)

RMS_EPS = 1e-6
HEAD_DIM = 64
N_SB_HEADS = 8
N_FOX_HEADS = 8
N_MEM_HEADS = 4
NEG = -1e30
ATT_TQ = 1024
ATT_TK = 256
DECAY_TK = 128
IN_TILE = 1280

ADAM_LR = 0.001
ADAM_B1 = 0.9
ADAM_B2 = 0.999
ADAM_EPS = 1e-08
ADAM_WD = 0.01
ADAM_STEP = 10


def _tile(n, target, mult=LANE):
    best = None
    for t in range(mult, min(n, target) + 1, mult):
        if n % t == 0:
            best = t
    return best if best is not None else n


def _cparams(sem):
    return pltpu.CompilerParams(dimension_semantics=sem, vmem_limit_bytes=VMEM_LIMIT)


_DIMS = {"nn": (((1,), (0,)), ((), ())), "nt": (((1,), (1,)), ((), ())), "tn": (((0,), (0,)), ((), ()))}


def _dot(a, b, mode="nn"):
    return lax.dot_general(a.astype(BF16), b.astype(BF16), _DIMS[mode], preferred_element_type=F32)


def _mm(name, pairs, mode, out_dtypes, epilogue=None, extras=(), tm=512, tn=1024, out_rows=None):
    a0, b0 = pairs[0]
    M = a0.shape[1] if mode == "tn" else a0.shape[0]
    N = b0.shape[0] if mode == "nt" else b0.shape[1]
    tm = _tile(M, tm)
    tn = _tile(N, tn)
    np_, ne, no = len(pairs), len(extras), len(out_dtypes)

    def body(*refs):
        a_refs, b_refs = refs[:np_], refs[np_:2 * np_]
        e_refs = refs[2 * np_:2 * np_ + ne]
        o_refs = refs[2 * np_ + ne:]
        accs = [_dot(a[...], b[...], mode) for a, b in zip(a_refs, b_refs)]
        outs = epilogue(accs, [e[...] for e in e_refs]) if epilogue is not None else accs
        for o, val in zip(o_refs, outs):
            o[...] = val.astype(o.dtype)

    in_specs = []
    for a, _ in pairs:
        if mode == "tn":
            in_specs.append(pl.BlockSpec((a.shape[0], tm), lambda j, i: (0, i)))
        else:
            in_specs.append(pl.BlockSpec((tm, a.shape[1]), lambda j, i: (i, 0)))
    for _, b in pairs:
        if mode == "nt":
            in_specs.append(pl.BlockSpec((tn, b.shape[1]), lambda j, i: (j, 0)))
        else:
            in_specs.append(pl.BlockSpec((b.shape[0], tn), lambda j, i: (0, j)))
    for e, off in extras:
        if e.shape[0] == 1:
            in_specs.append(pl.BlockSpec((1, tn), functools.partial(lambda j, i, o: (0, j + o), o=off // tn)))
        else:
            in_specs.append(pl.BlockSpec((tm, tn), functools.partial(lambda j, i, o: (i, j + o), o=off // tn)))
    rows = [tm if r is None else r for r in (out_rows or [None] * no)]
    out_specs = [pl.BlockSpec((r, tn), lambda j, i: (i, j)) for r in rows]
    outs = pl.pallas_call(
        body, name=name, grid=(N // tn, M // tm),
        in_specs=in_specs, out_specs=out_specs,
        out_shape=[jax.ShapeDtypeStruct((M // tm * r, N), dt) for r, dt in zip(rows, out_dtypes)],
        compiler_params=_cparams(("parallel", "parallel")),
    )(*[a for a, _ in pairs], *[b for _, b in pairs], *[e for e, _ in extras])
    return outs[0] if no == 1 else outs


def _sum_accs(accs, _):
    total = accs[0]
    for acc in accs[1:]:
        total = total + acc
    return [total]


def _rstd(x):
    return lax.rsqrt(jnp.mean(x * x, axis=-1, keepdims=True) + RMS_EPS)


def _rms_fwd(name, x, g, out_dtype, tr=512):
    R, D = x.shape
    tr = _tile(R, tr, 8)

    def body(x_ref, g_ref, o_ref):
        xv = x_ref[...]
        o_ref[...] = ((xv * _rstd(xv)) * g_ref[...]).astype(o_ref.dtype)

    row = pl.BlockSpec((tr, D), lambda i: (i, 0))
    return pl.pallas_call(
        body, name=name, grid=(R // tr,),
        in_specs=[row, pl.BlockSpec((1, D), lambda i: (0, 0))], out_specs=row,
        out_shape=jax.ShapeDtypeStruct((R, D), out_dtype),
        compiler_params=_cparams(("parallel",)),
    )(x, g.reshape(1, D))


def _rms_bwd(name, x, g, dy, out_dtype, scale=1.0, res=None, tr=512):
    R, D = x.shape
    tr = _tile(R, tr, 8)
    has_res = res is not None

    def body(*refs):
        x_ref, g_ref, dy_ref = refs[:3]
        dx_ref, dg_ref = refs[-2:]
        i = pl.program_id(0)
        xv = x_ref[...]
        xhat = xv * _rstd(xv)
        dyv = dy_ref[...].astype(F32) * scale
        gy = dyv * g_ref[...]
        dx = _rstd(xv) * (gy - xhat * jnp.mean(gy * xhat, axis=-1, keepdims=True))
        if has_res:
            dx = refs[3][...] + dx
        dx_ref[...] = dx.astype(dx_ref.dtype)
        part = jnp.sum(dyv * xhat, axis=0, keepdims=True)

        @pl.when(i == 0)
        def _():
            dg_ref[...] = part

        @pl.when(i > 0)
        def _():
            dg_ref[...] += part

    row = pl.BlockSpec((tr, D), lambda i: (i, 0))
    gain = pl.BlockSpec((1, D), lambda i: (0, 0))
    dx, dg = pl.pallas_call(
        body, name=name, grid=(R // tr,),
        in_specs=[row, gain, row] + ([row] if has_res else []), out_specs=[row, gain],
        out_shape=[jax.ShapeDtypeStruct((R, D), out_dtype), jax.ShapeDtypeStruct((1, D), F32)],
        compiler_params=_cparams(("arbitrary",)),
    )(x, g.reshape(1, D), dy, *([res] if has_res else []))
    return dx, dg[0]


def _loss_grad(y, tgt, tr=512):
    R, D = y.shape
    tr = _tile(R, tr, 8)

    def body(y_ref, t_ref, dy_ref, loss_ref):
        i = pl.program_id(0)
        d = y_ref[...] - t_ref[...]
        dy_ref[...] = d / D
        part = 0.5 * jnp.sum(jnp.mean(d * d, axis=-1, keepdims=True), axis=0, keepdims=True)
        tile = jnp.broadcast_to(part, loss_ref.shape)

        @pl.when(i == 0)
        def _():
            loss_ref[...] = tile

        @pl.when(i > 0)
        def _():
            loss_ref[...] += tile

    row = pl.BlockSpec((tr, D), lambda i: (i, 0))
    dy, loss = pl.pallas_call(
        body, name="loss_grad", grid=(R // tr,),
        in_specs=[row, row], out_specs=[row, pl.BlockSpec((8, LANE), lambda i: (0, 0))],
        out_shape=[jax.ShapeDtypeStruct((R, D), F32), jax.ShapeDtypeStruct((8, LANE), F32)],
        compiler_params=_cparams(("arbitrary",)),
    )(y, tgt)
    return loss[0, 0], dy


def _colsum(name, x, tr=512, tn=1024):
    R, N = x.shape
    tr, tn = _tile(R, tr, 8), _tile(N, tn)

    def body(x_ref, o_ref):
        i = pl.program_id(1)
        part = jnp.sum(x_ref[...].astype(F32), axis=0, keepdims=True)

        @pl.when(i == 0)
        def _():
            o_ref[...] = part

        @pl.when(i > 0)
        def _():
            o_ref[...] += part

    out = pl.pallas_call(
        body, name=name, grid=(N // tn, R // tr),
        in_specs=[pl.BlockSpec((tr, tn), lambda j, i: (i, j))], out_specs=pl.BlockSpec((1, tn), lambda j, i: (0, j)),
        out_shape=jax.ShapeDtypeStruct((1, N), F32),
        compiler_params=_cparams(("parallel", "arbitrary")),
    )(x)
    return out[0]


def _tri(tk, rel):
    j = lax.broadcasted_iota(jnp.int32, (tk, tk), 0)
    s = lax.broadcasted_iota(jnp.int32, (tk, tk), 1)
    return rel(j, s).astype(BF16)


def _dot_split(x, m, parts=2):
    total = None
    rem = x
    for _ in range(parts):
        piece = rem.astype(BF16)
        rem = rem - piece.astype(F32)
        term = jnp.dot(piece, m, preferred_element_type=F32)
        total = term if total is None else total + term
    return total


def _log_not_and_beta(z, mask):
    ln = -(jnp.maximum(z, 0.0) + jnp.log(1.0 + jnp.exp(-jnp.abs(z))))
    return (ln if mask is None else jnp.where(mask, ln, 0.0)), ln + z


def _att_tiles(T, Tk, causal):
    tq = min(ATT_TQ, T)
    tk = min(ATT_TK, tq if causal else Tk)
    return tq, tk, (tq if causal else Tk) // tk


def _key_base(j, tq):
    return j * tq if isinstance(j, int) else pl.multiple_of(j * tq, tq)


def _is_pow2(scale):
    return math.log2(scale).is_integer()


def _per_head(x, hpb, d):
    if hpb == 1:
        return [x]
    lane = lax.broadcasted_iota(jnp.int32, x.shape, 1)
    return [jnp.where((lane >= h * d) & (lane < (h + 1) * d), x, jnp.zeros_like(x)) for h in range(hpb)]


def _join_heads(xs, d):
    out = xs[-1]
    if len(xs) > 1:
        lane = lax.broadcasted_iota(jnp.int32, out.shape, 1)
        for h in reversed(range(len(xs) - 1)):
            out = jnp.where(lane < (h + 1) * d, xs[h], out)
    return out


def _lane_tile(rows, off, whole):
    if whole:
        return pl.BlockSpec((rows, LANE), lambda g, i: (0, off + g))
    return pl.BlockSpec((rows, LANE), lambda g, i: (i, off + g))


def _sb_fwd(q, k, v, n_tiles, d, scale):
    T = q[0].shape[0]
    hpb = LANE // d
    tq, tk, nsub = _att_tiles(T, T, True)
    assert _is_pow2(scale)

    def body(q_ref, k_ref, v_ref, ob_ref, rt_ref, acc_ref, r_ref):
        qi = pl.program_id(1)
        qh = _per_head(q_ref[...] * scale, hpb, d)
        acc_ref[...] = jnp.zeros_like(acc_ref)
        r_ref[...] = jnp.zeros_like(r_ref)
        row = lax.broadcasted_iota(jnp.int32, (tq, tk), 0)
        col = lax.broadcasted_iota(jnp.int32, (tq, tk), 1)
        after = _tri(tk, lambda j, s: j > s)

        def walk(h, base, r0, r1, subs, diagonal):
            parts = []
            for u in subs:
                z = _dot(qh[h][r0:r1], k_ref[pl.ds(base + u * tk, tk), :], "nt")
                mask = (col[r0:r1] + u * tk) < row[r0:r1] if diagonal else None
                ln, lb = _log_not_and_beta(z, mask)
                between = _dot_split(ln, after, parts=1)
                first = ln[:, 0:1].astype(BF16).astype(F32)
                parts.append((u, lb, between, between[:, 0:1] + first, mask))
            r = r_ref[h, r0:r1, :]
            out = None
            for u, lb, between, total, mask in parts:
                w = jnp.exp(lb + between + r)
                if diagonal:
                    w = jnp.where(mask, w, 0.0)
                term = _dot(w, v_ref[pl.ds(base + u * tk, tk), :])
                out = term if out is None else out + term
                r = r + total
            acc_ref[h, r0:r1, :] += out
            r_ref[h, r0:r1, :] = r

        def step(j, diagonal):
            base = _key_base(j, tq)
            for h in range(hpb):
                if diagonal and nsub % 2 == 0:
                    walk(h, base, 0, tq // 2, range(nsub // 2 - 1, -1, -1), True)
                    walk(h, base, tq // 2, tq, range(nsub - 1, -1, -1), True)
                else:
                    walk(h, base, 0, tq, range(nsub - 1, -1, -1), diagonal)

        def below(i, carry):
            step(qi - 1 - i, False)
            return carry

        step(qi, True)
        lax.fori_loop(0, qi, below, 0)
        ob_ref[...] = _join_heads([acc_ref[h] for h in range(hpb)], d).astype(ob_ref.dtype)
        rt_ref[...] = r_ref[...]

    out = pl.BlockSpec((tq, LANE), lambda g, i: (i, g))
    col = pl.BlockSpec((hpb, tq, 1), lambda g, i: (g, i, 0))
    return pl.pallas_call(
        body, name="sb_fwd", grid=(n_tiles, T // tq),
        in_specs=[_lane_tile(tq, q[1], False), _lane_tile(T, k[1], True), _lane_tile(T, v[1], True)],
        out_specs=[out, col],
        out_shape=[jax.ShapeDtypeStruct((T, n_tiles * LANE), BF16), jax.ShapeDtypeStruct((n_tiles * hpb, T, 1), F32)],
        scratch_shapes=[pltpu.VMEM((hpb, tq, LANE), F32), pltpu.VMEM((hpb, tq, 1), F32)],
        compiler_params=_cparams(("parallel", "arbitrary")),
    )(q[0], k[0], v[0])


def _sb_bwd(q, k, v, do, rtot, n_tiles, d, scale):
    T = q[0].shape[0]
    hpb = LANE // d
    tq, tk, nsub = _att_tiles(T, T, True)
    assert _is_pow2(scale)

    def body(q_ref, k_ref, v_ref, do_ref, rt_ref, dq_ref, dk_ref, dv_ref, dk_acc, dv_acc, dq_acc, p_ref, c_ref):
        qi = pl.program_id(1)

        @pl.when(qi == 0)
        def _():
            dk_acc[...] = jnp.zeros_like(dk_acc)
            dv_acc[...] = jnp.zeros_like(dv_acc)

        qh = _per_head(q_ref[...] * scale, hpb, d)
        doh = _per_head(do_ref[...], hpb, d)
        dq_acc[...] = jnp.zeros_like(dq_acc)
        p_ref[...] = jnp.zeros_like(p_ref)
        c_ref[...] = jnp.zeros_like(c_ref)
        row = lax.broadcasted_iota(jnp.int32, (tq, tk), 0)
        col = lax.broadcasted_iota(jnp.int32, (tq, tk), 1)
        upto = _tri(tk, lambda j, s: j <= s)
        before = _tri(tk, lambda j, s: j < s)
        rt_wide = [jnp.broadcast_to(rt_ref[h], (tq, tk)) for h in range(hpb)]

        def step(j, diagonal):
            base = _key_base(j, tq)
            for h in range(hpb):
                first = []
                for u in range(nsub):
                    ks = base + u * tk
                    r0 = u * tk if diagonal else 0
                    kv = k_ref[pl.ds(ks, tk), :]
                    z = _dot(qh[h][r0:], kv, "nt")
                    mask = (col[r0:] + u * tk) < row[r0:] if diagonal else None
                    ln, lb = _log_not_and_beta(z, mask)
                    dw = _dot(doh[h][r0:], v_ref[pl.ds(ks, tk), :], "nt")
                    first.append((r0, ks, kv, mask, lb, jnp.exp(lb), _dot_split(ln, upto, parts=1), dw))
                rt, pre, cpre = rt_wide[h], p_ref[h], c_ref[h]
                dq = None
                for r0, ks, kv, mask, lb, sig, local, dw in first:
                    if diagonal and r0:
                        pre, cpre = pre[tk:], cpre[tk:]
                    prefix = local + pre
                    w = jnp.exp(lb + (rt[r0:] - prefix))
                    if diagonal:
                        w = jnp.where(mask, w, 0.0)
                    g = dw * w
                    c = _dot_split(g, before, parts=1) + cpre
                    dz = g * (1.0 - sig) - c * sig
                    if diagonal:
                        dz = jnp.where(mask, dz, 0.0)
                    term = _dot(dz, kv)
                    if diagonal:
                        dq_acc[h, r0:, :] += term
                    else:
                        dq = term if dq is None else dq + term
                    dk_acc[pl.ds(ks, tk), :] += _dot(dz, qh[h][r0:], "tn")
                    dv_acc[pl.ds(ks, tk), :] += _dot(w, doh[h][r0:], "tn")
                    pre = prefix[:, tk - 1:tk]
                    cpre = c[:, tk - 1:tk] + g[:, tk - 1:tk]
                if not diagonal:
                    dq_acc[h] += dq
                    p_ref[h] = pre
                    c_ref[h] = cpre

        def below(j, carry):
            step(j, False)
            return carry

        lax.fori_loop(0, qi, below, 0)
        step(qi, True)
        dq_ref[...] = (_join_heads([dq_acc[h] for h in range(hpb)], d) * scale).astype(dq_ref.dtype)

        @pl.when(qi == pl.num_programs(1) - 1)
        def _():
            dk_ref[...] = dk_acc[...].astype(dk_ref.dtype)
            dv_ref[...] = dv_acc[...].astype(dv_ref.dtype)

    blk = pl.BlockSpec((tq, LANE), lambda g, i: (i, g))
    full = pl.BlockSpec((T, LANE), lambda g, i: (0, g))
    col = pl.BlockSpec((hpb, tq, 1), lambda g, i: (g, i, 0))
    wide = jax.ShapeDtypeStruct((T, n_tiles * LANE), BF16)
    return pl.pallas_call(
        body, name="sb_bwd", grid=(n_tiles, T // tq),
        in_specs=[_lane_tile(tq, q[1], False), _lane_tile(T, k[1], True), _lane_tile(T, v[1], True), blk, col],
        out_specs=[blk, full, full], out_shape=[wide, wide, wide],
        scratch_shapes=[pltpu.VMEM((T, LANE), F32), pltpu.VMEM((T, LANE), F32), pltpu.VMEM((hpb, tq, LANE), F32),
                        pltpu.VMEM((hpb, tq, 1), F32), pltpu.VMEM((hpb, tq, 1), F32)],
        compiler_params=_cparams(("parallel", "arbitrary")),
    )(q[0], k[0], v[0], do, rtot)


def _attn_fwd(name, q, k, v, n_tiles, d, scale, c=None):
    T, Tk = q[0].shape[0], k[0].shape[0]
    hpb = LANE // d
    H = n_tiles * hpb
    causal = c is not None
    tq, tk, nsub = _att_tiles(T, Tk, causal)
    fold = _is_pow2(scale)

    def body(*refs):
        q_ref, k_ref, v_ref = refs[:3]
        cc_ref, cr_ref = refs[3:5] if causal else (None, None)
        o_ref, ob_ref, lse_ref, m_ref, l_ref, acc_ref = refs[-6:]
        qi = pl.program_id(1)
        qh = _per_head(q_ref[...] * scale if fold else q_ref[...], hpb, d)
        bias = [jnp.broadcast_to(cc_ref[h], (tq, tk)) for h in range(hpb)] if causal else None
        ones = jnp.ones((tk, LANE), BF16)
        m_ref[...] = jnp.full_like(m_ref, NEG)
        l_ref[...] = jnp.zeros_like(l_ref)
        acc_ref[...] = jnp.zeros_like(acc_ref)
        row = lax.broadcasted_iota(jnp.int32, (tq, tk), 0)
        col = lax.broadcasted_iota(jnp.int32, (tq, tk), 1)

        def absorb(h, j, base, r0, r1, subs, diagonal):
            zs = []
            for u in subs:
                z = _dot(qh[h][r0:r1], k_ref[pl.ds(base + u * tk, tk), :], "nt")
                if not fold:
                    z = z * scale
                if causal:
                    z = z + bias[h][r0:r1] - cr_ref[h, j * nsub + u]
                if diagonal:
                    z = jnp.where((col[r0:r1] + u * tk) <= row[r0:r1], z, NEG)
                zs.append(z)
            m_prev = m_ref[h, r0:r1, :]
            top = zs[0]
            for z in zs[1:]:
                top = jnp.maximum(top, z)
            m_new = jnp.maximum(m_prev, jnp.max(top, axis=1, keepdims=True))
            alpha = jnp.exp(m_prev - m_new)
            l_new = alpha * l_ref[h, r0:r1, :]
            out = alpha * acc_ref[h, r0:r1, :]
            m_wide = jnp.broadcast_to(m_new, top.shape)
            for u, z in zip(subs, zs):
                p = jnp.exp(z - m_wide).astype(BF16)
                l_new = l_new + jnp.dot(p, ones, preferred_element_type=F32)[:, 0:1]
                out = out + _dot(p, v_ref[pl.ds(base + u * tk, tk), :])
            l_ref[h, r0:r1, :] = l_new
            acc_ref[h, r0:r1, :] = out
            m_ref[h, r0:r1, :] = m_new

        def step(j, diagonal):
            base = _key_base(j, tq)
            for h in range(hpb):
                if diagonal and nsub % 2 == 0:
                    absorb(h, j, base, 0, tq // 2, range(nsub // 2), True)
                    absorb(h, j, base, tq // 2, tq, range(nsub), True)
                else:
                    absorb(h, j, base, 0, tq, range(nsub), diagonal)

        def below(j, carry):
            step(j, False)
            return carry

        if causal:
            lax.fori_loop(0, qi, below, 0)
            step(qi, True)
        else:
            step(0, False)
        o = _join_heads([acc_ref[h] / l_ref[h] for h in range(hpb)], d)
        o_ref[...] = o
        ob_ref[...] = o.astype(ob_ref.dtype)
        lse_ref[...] = m_ref[...] + jnp.log(l_ref[...])

    out = pl.BlockSpec((tq, LANE), lambda g, i: (i, g))
    col = pl.BlockSpec((hpb, tq, 1), lambda g, i: (g, i, 0))
    in_specs = [_lane_tile(tq, q[1], False), _lane_tile(Tk, k[1], True), _lane_tile(Tk, v[1], True)]
    args = [q[0], k[0], v[0]]
    if causal:
        in_specs += [col, pl.BlockSpec((hpb, T // tk, 1, tk), lambda g, i: (g, 0, 0, 0))]
        args += [c.reshape(H, T, 1), c.reshape(H, T // tk, 1, tk)]
    return pl.pallas_call(
        body, name=name, grid=(n_tiles, T // tq),
        in_specs=in_specs, out_specs=[out, out, col],
        out_shape=[jax.ShapeDtypeStruct((T, n_tiles * LANE), F32), jax.ShapeDtypeStruct((T, n_tiles * LANE), BF16),
                   jax.ShapeDtypeStruct((H, T, 1), F32)],
        scratch_shapes=[pltpu.VMEM((hpb, tq, 1), F32), pltpu.VMEM((hpb, tq, 1), F32),
                        pltpu.VMEM((hpb, tq, LANE), F32)],
        compiler_params=_cparams(("parallel", "arbitrary")),
    )(*args)


def _attn_bwd(name, q, k, v, o, do, lse, n_tiles, d, scale, c=None):
    T, Tk = q[0].shape[0], k[0].shape[0]
    hpb = LANE // d
    H = n_tiles * hpb
    causal = c is not None
    tq, tk, nsub = _att_tiles(T, Tk, causal)
    fold = _is_pow2(scale)

    def body(*refs):
        q_ref, k_ref, v_ref, o_ref, do_ref, lse_ref = refs[:6]
        cc_ref, cr_ref = refs[6:8] if causal else (None, None)
        n_out = 5 if causal else 3
        outs = refs[-(n_out + 3):-3]
        dq_ref, dk_ref, dv_ref = outs[:3]
        dc_ref, drow_ref = outs[3:5] if causal else (None, None)
        dk_acc, dv_acc, dq_acc = refs[-3:]
        qi = pl.program_id(1)

        @pl.when(qi == 0)
        def _():
            dk_acc[...] = jnp.zeros_like(dk_acc)
            dv_acc[...] = jnp.zeros_like(dv_acc)
            if causal:
                dc_ref[...] = jnp.zeros_like(dc_ref)

        qh = _per_head(q_ref[...] * scale if fold else q_ref[...], hpb, d)
        doh = _per_head(do_ref[...], hpb, d)
        delta_wide = [jnp.broadcast_to(jnp.sum(t.astype(F32) * o_ref[...], axis=1, keepdims=True), (tq, tk))
                      for t in doh]
        shift = [jnp.broadcast_to((cc_ref[h] - lse_ref[h]) if causal else -lse_ref[h], (tq, tk)) for h in range(hpb)]
        dq_acc[...] = jnp.zeros_like(dq_acc)
        if causal:
            drow_ref[...] = jnp.zeros_like(drow_ref)
        row = lax.broadcasted_iota(jnp.int32, (tq, tk), 0)
        col = lax.broadcasted_iota(jnp.int32, (tq, tk), 1)

        def step(j, diagonal):
            base = _key_base(j, tq)
            for h in range(hpb):
                dq, dsum = None, None
                for u in range(nsub):
                    ks = base + u * tk
                    r0 = u * tk if diagonal else 0
                    kv = k_ref[pl.ds(ks, tk), :]
                    z = _dot(qh[h][r0:], kv, "nt")
                    if not fold:
                        z = z * scale
                    z = z + shift[h][r0:]
                    if causal:
                        z = z - cr_ref[h, j * nsub + u]
                    if diagonal:
                        z = jnp.where((col[r0:] + u * tk) <= row[r0:], z, NEG)
                    p = jnp.exp(z)
                    ds = p * (_dot(doh[h][r0:], v_ref[pl.ds(ks, tk), :], "nt") - delta_wide[h][r0:])
                    term = _dot(ds, kv)
                    dk = _dot(ds, qh[h][r0:], "tn")
                    dk_acc[pl.ds(ks, tk), :] += dk if fold else dk * scale
                    dv_acc[pl.ds(ks, tk), :] += _dot(p, doh[h][r0:], "tn")
                    if causal:
                        dc_ref[h, j * nsub + u] -= jnp.sum(ds, axis=0, keepdims=True)
                    if diagonal:
                        dq_acc[h, r0:, :] += term
                        drow_ref[h, r0:, :] += jnp.sum(ds, axis=1, keepdims=True)
                    else:
                        dq = term if dq is None else dq + term
                        if causal:
                            dsum = ds if dsum is None else dsum + ds
                if not diagonal:
                    dq_acc[h] += dq
                    if causal:
                        drow_ref[h] += jnp.sum(dsum, axis=1, keepdims=True)

        def below(j, carry):
            step(j, False)
            return carry

        if causal:
            lax.fori_loop(0, qi, below, 0)
            step(qi, True)
        else:
            step(0, False)
        dq_ref[...] = (_join_heads([dq_acc[h] for h in range(hpb)], d) * scale).astype(dq_ref.dtype)

        @pl.when(qi == pl.num_programs(1) - 1)
        def _():
            dk_ref[...] = dk_acc[...].astype(dk_ref.dtype)
            dv_ref[...] = dv_acc[...].astype(dv_ref.dtype)

    blk = pl.BlockSpec((tq, LANE), lambda g, i: (i, g))
    full = pl.BlockSpec((Tk, LANE), lambda g, i: (0, g))
    col = pl.BlockSpec((hpb, tq, 1), lambda g, i: (g, i, 0))
    crow = pl.BlockSpec((hpb, T // tk, 1, tk), lambda g, i: (g, 0, 0, 0))
    in_specs = [_lane_tile(tq, q[1], False), _lane_tile(Tk, k[1], True), _lane_tile(Tk, v[1], True), blk, blk, col]
    args = [q[0], k[0], v[0], o, do, lse]
    out_specs = [blk, full, full]
    out_shape = [jax.ShapeDtypeStruct((T, n_tiles * LANE), BF16), jax.ShapeDtypeStruct((Tk, n_tiles * LANE), BF16),
                 jax.ShapeDtypeStruct((Tk, n_tiles * LANE), BF16)]
    if causal:
        in_specs += [col, crow]
        args += [c.reshape(H, T, 1), c.reshape(H, T // tk, 1, tk)]
        out_specs += [crow, col]
        out_shape += [jax.ShapeDtypeStruct((H, T // tk, 1, tk), F32), jax.ShapeDtypeStruct((H, T, 1), F32)]
    outs = pl.pallas_call(
        body, name=name, grid=(n_tiles, T // tq),
        in_specs=in_specs, out_specs=out_specs, out_shape=out_shape,
        scratch_shapes=[pltpu.VMEM((Tk, LANE), F32), pltpu.VMEM((Tk, LANE), F32), pltpu.VMEM((hpb, tq, LANE), F32)],
        compiler_params=_cparams(("parallel", "arbitrary")),
    )(*args)
    if causal:
        return outs[0], outs[1], outs[2], outs[3].reshape(H, T), outs[4].reshape(H, T)
    return outs


def _decay_fwd(fl, b):
    H, T = fl.shape
    tk = DECAY_TK

    def body(x_ref, b_ref, c_ref):
        upto = _tri(tk, lambda j, s: j <= s)
        carry = jnp.zeros((H, 1), F32)
        for i in range(T // tk):
            xv = x_ref[:, i * tk:(i + 1) * tk] + b_ref[...]
            lf = jnp.minimum(xv, 0.0) - jnp.log(1.0 + jnp.exp(-jnp.abs(xv)))
            pref = _dot_split(lf, upto, parts=3) + carry
            c_ref[:, i * tk:(i + 1) * tk] = pref
            carry = pref[:, tk - 1:tk]

    vm = pl.BlockSpec(memory_space=pltpu.VMEM)
    return pl.pallas_call(
        body, name="decay_fwd", in_specs=[vm, vm], out_specs=vm,
        out_shape=jax.ShapeDtypeStruct((H, T), F32),
    )(fl, b)


def _decay_bwd(dc_cols, dc_rows, fl, b):
    H, T = fl.shape
    tk = DECAY_TK

    def body(dc_ref, dr_ref, x_ref, b_ref, dx_ref, db_ref):
        from_ = _tri(tk, lambda j, s: j >= s)
        carry = jnp.zeros((H, 1), F32)
        total = jnp.zeros((H, 1), F32)
        for i in reversed(range(T // tk)):
            sl = slice(i * tk, (i + 1) * tk)
            suffix = _dot_split(dc_ref[:, sl] + dr_ref[:, sl], from_, parts=3) + carry
            xv = x_ref[:, sl] + b_ref[...]
            dx = suffix / (1.0 + jnp.exp(xv))
            dx_ref[:, sl] = dx
            total = total + jnp.sum(dx, axis=1, keepdims=True)
            carry = suffix[:, 0:1]
        db_ref[...] = jnp.broadcast_to(total, db_ref.shape)

    vm = pl.BlockSpec(memory_space=pltpu.VMEM)
    dx, db = pl.pallas_call(
        body, name="decay_bwd", in_specs=[vm, vm, vm, vm], out_specs=[vm, vm],
        out_shape=[jax.ShapeDtypeStruct((H, T), F32), jax.ShapeDtypeStruct((H, LANE), F32)],
    )(dc_cols, dc_rows, fl, b)
    return dx, db[:, 0]


def _place():
    x, y, c = lax.axis_index("x"), lax.axis_index("y"), lax.axis_index("c")
    return x, y, c, [(1 - x, y), (x, 1 - y), (1 - x, 1 - y)]


def _all_gather(name, block):
    R, C = block.shape

    def body(x_ref, out_ref, send_sems, recv_sems, local_sem):
        x, y, c, chips = _place()
        me, sibling = (x, y, c), (x, y, 1 - c)

        def rows(px, py, pc):
            return out_ref.at[4 * px + 2 * py + pc]

        def copy(k, blk, to, src=None):
            return pltpu.make_async_remote_copy(
                src_ref=rows(*blk) if src is None else src, dst_ref=rows(*blk),
                send_sem=send_sems.at[k], recv_sem=recv_sems.at[k], device_id=to, device_id_type=MESH)

        mine = pltpu.make_async_copy(x_ref, rows(*me), local_sem)
        mine.start()
        first = [copy(0, me, sibling, src=x_ref)]
        first += [copy(1 + j, me, (*chip, c), src=x_ref) for j, chip in enumerate(chips)]
        for cp in first:
            cp.start()
        passed = [copy(4 + j, (*chip, c), sibling) for j, chip in enumerate(chips)]
        for j, chip in enumerate(chips):
            copy(1 + j, (*chip, c), me).wait_recv()
            passed[j].start()
        copy(0, sibling, me).wait_recv()
        for j, chip in enumerate(chips):
            copy(4 + j, (*chip, 1 - c), me).wait_recv()
        for cp in first + passed:
            cp.wait_send()
        mine.wait()

    return pl.pallas_call(
        body, name=name, in_specs=[ANY], out_specs=ANY,
        out_shape=jax.ShapeDtypeStruct((N_DEV, R, C), block.dtype),
        scratch_shapes=[pltpu.SemaphoreType.DMA((7,)), pltpu.SemaphoreType.DMA((7,)), pltpu.SemaphoreType.DMA(())],
    )(block)


def _swap_with_sibling(name, parts):
    _, R, C = parts.shape

    def body(p_ref, out_ref, send_sems, recv_sems):
        x, y, c, _ = _place()
        copies = [pltpu.make_async_remote_copy(
            src_ref=p_ref.at[2 * q + (1 - c)], dst_ref=out_ref.at[q],
            send_sem=send_sems.at[q], recv_sem=recv_sems.at[q], device_id=(x, y, 1 - c), device_id_type=MESH)
            for q in range(4)]
        for cp in copies:
            cp.start()
        for cp in copies:
            cp.wait_recv()
        for cp in copies:
            cp.wait_send()

    return pl.pallas_call(
        body, name=name, in_specs=[ANY], out_specs=ANY,
        out_shape=jax.ShapeDtypeStruct((4, R, C), parts.dtype),
        scratch_shapes=[pltpu.SemaphoreType.DMA((4,)), pltpu.SemaphoreType.DMA((4,))],
    )(parts)


def _add_own(name, parts, got, tr=512):
    _, R, C = parts.shape
    tr = _tile(R, tr, SUBLANE_BF16)

    def body(c_ref, p_ref, g_ref, o_ref):
        o_ref[...] = (p_ref[...].astype(F32) + g_ref[...].astype(F32)).astype(o_ref.dtype)

    return pl.pallas_call(
        body, name=name,
        grid_spec=pltpu.PrefetchScalarGridSpec(
            num_scalar_prefetch=1, grid=(4, R // tr),
            in_specs=[pl.BlockSpec((1, tr, C), lambda q, i, c: (2 * q + c[0], i, 0)),
                      pl.BlockSpec((1, tr, C), lambda q, i, c: (q, i, 0))],
            out_specs=pl.BlockSpec((1, tr, C), lambda q, i, c: (q, i, 0))),
        out_shape=jax.ShapeDtypeStruct((4, R, C), parts.dtype),
        compiler_params=_cparams(("parallel", "parallel")),
    )(lax.axis_index("c").astype(jnp.int32).reshape(1), parts, got)


def _swap_with_chips(name, parts):
    _, R, C = parts.shape

    def body(p_ref, out_ref, send_sems, recv_sems, local_sem):
        x, y, c, chips = _place()
        my_chip = 2 * x + y
        mine = pltpu.make_async_copy(p_ref.at[my_chip], out_ref.at[my_chip], local_sem)
        mine.start()
        sends = [pltpu.make_async_remote_copy(
            src_ref=p_ref.at[2 * cx + cy], dst_ref=out_ref.at[my_chip],
            send_sem=send_sems.at[j], recv_sem=recv_sems.at[j], device_id=(cx, cy, c), device_id_type=MESH)
            for j, (cx, cy) in enumerate(chips)]
        for cp in sends:
            cp.start()
        for j, (cx, cy) in enumerate(chips):
            pltpu.make_async_remote_copy(
                src_ref=p_ref.at[my_chip], dst_ref=out_ref.at[2 * cx + cy],
                send_sem=send_sems.at[j], recv_sem=recv_sems.at[j], device_id=(cx, cy, c), device_id_type=MESH,
            ).wait_recv()
        for cp in sends:
            cp.wait_send()
        mine.wait()

    return pl.pallas_call(
        body, name=name, in_specs=[ANY], out_specs=ANY,
        out_shape=jax.ShapeDtypeStruct((4, R, C), parts.dtype),
        scratch_shapes=[pltpu.SemaphoreType.DMA((3,)), pltpu.SemaphoreType.DMA((3,)), pltpu.SemaphoreType.DMA(())],
    )(parts)


def _sum_parts(name, parts, tr=512):
    P, R, C = parts.shape
    tr = _tile(R, tr, SUBLANE_BF16)

    def body(p_ref, o_ref):
        total = p_ref[0].astype(F32)
        for p in range(1, P):
            total = total + p_ref[p].astype(F32)
        o_ref[...] = total

    return pl.pallas_call(
        body, name=name, grid=(R // tr,),
        in_specs=[pl.BlockSpec((P, tr, C), lambda i: (0, i, 0))], out_specs=pl.BlockSpec((tr, C), lambda i: (i, 0)),
        out_shape=jax.ShapeDtypeStruct((R, C), F32),
        compiler_params=_cparams(("parallel",)),
    )(parts)


_HBM = pl.BlockSpec(memory_space=pltpu.HBM)
_SEM = pl.BlockSpec(memory_space=pltpu.SEMAPHORE)
_EFFECT = pltpu.SideEffectType.DATAFLOW_SIDE_EFFECTING


def _flipped(x, y, c, k):
    px, py, pc = (1 - x if k & 4 else x), (1 - y if k & 2 else y), (1 - c if k & 1 else c)
    return (px, py, pc), 4 * px + 2 * py + pc


def _exchange_start(name, src, per_peer):
    R, C = src.shape[-2:]

    def body(v_ref, land_ref, send_sem, recv_sem, v_thru, land_thru, token):
        x, y, c = lax.axis_index("x"), lax.axis_index("y"), lax.axis_index("c")
        me = 4 * x + 2 * y + c
        for k in range(1, N_DEV):
            peer, idx = _flipped(x, y, c, k)
            pltpu.make_async_remote_copy(
                src_ref=v_ref.at[idx] if per_peer else v_ref, dst_ref=land_ref.at[me],
                send_sem=send_sem, recv_sem=recv_sem, device_id=peer, device_id_type=MESH).start()
        token[...] = jnp.zeros_like(token)

    return pl.pallas_call(
        body, name=name,
        out_shape=(pltpu.SemaphoreType.DMA(()), pltpu.SemaphoreType.DMA(()), pltpu.HBM(src.shape, src.dtype),
                   pltpu.HBM((N_DEV, R, C), src.dtype), jax.ShapeDtypeStruct((8, LANE), F32)),
        in_specs=(_HBM, _HBM), out_specs=(_SEM, _SEM, _HBM, _HBM, pl.BlockSpec(memory_space=pltpu.VMEM)),
        input_output_aliases={0: 2, 1: 3},
        compiler_params=pltpu.CompilerParams(has_side_effects=_EFFECT),
    )(pltpu.with_memory_space_constraint(src, pltpu.HBM),
      pltpu.with_memory_space_constraint(lax.empty((N_DEV, R, C), src.dtype), pltpu.HBM))


def _exchange_wait(name, started, after):
    send_sem, recv_sem, v_thru, land_thru, _ = started

    def body(v_ref, land_ref, send_sem, recv_sem, after_ref, v_dead, got_ref):
        x, y, c = lax.axis_index("x"), lax.axis_index("y"), lax.axis_index("c")
        seven = land_ref.at[pl.ds(0, N_DEV - 1)]
        drain = pltpu.make_async_remote_copy(
            src_ref=seven, dst_ref=seven, send_sem=send_sem, recv_sem=recv_sem,
            device_id=(x, y, c), device_id_type=MESH)
        drain.wait_send()
        drain.wait_recv()

    return pl.pallas_call(
        body, name=name,
        out_shape=(pltpu.HBM(v_thru.shape, v_thru.dtype), pltpu.HBM(land_thru.shape, land_thru.dtype)),
        in_specs=(_HBM, _HBM, _SEM, _SEM, ANY), out_specs=(_HBM, _HBM), input_output_aliases={0: 0, 1: 1},
        compiler_params=pltpu.CompilerParams(has_side_effects=_EFFECT),
    )(v_thru, land_thru, send_sem, recv_sem, after)


def _my_index():
    return 4 * lax.axis_index("x") + 2 * lax.axis_index("y") + lax.axis_index("c")


def _sum_landed(name, landed, parts, tr=512):
    P, R, C = landed.shape
    tr = _tile(R, tr, SUBLANE_BF16)

    def body(me_ref, l_ref, own_ref, o_ref):
        total = None
        for s in range(P):
            part = jnp.where(me_ref[0] == s, own_ref[0], l_ref[s]).astype(F32)
            total = part if total is None else total + part
        o_ref[...] = total

    return pl.pallas_call(
        body, name=name,
        grid_spec=pltpu.PrefetchScalarGridSpec(
            num_scalar_prefetch=1, grid=(R // tr,),
            in_specs=[pl.BlockSpec((P, tr, C), lambda i, me: (0, i, 0)),
                      pl.BlockSpec((1, tr, C), lambda i, me: (me[0], i, 0))],
            out_specs=pl.BlockSpec((tr, C), lambda i, me: (i, 0))),
        out_shape=jax.ShapeDtypeStruct((R, C), F32),
        compiler_params=_cparams(("parallel",)),
    )(_my_index().astype(jnp.int32).reshape(1), landed, parts)


def _after(params, name, token):
    return {**params, name: params[name] + token[0, 0]}


def _reduce_scatter(tag, parts):
    got = _swap_with_sibling("rs_pair_" + tag, parts)
    pair = _add_own("rs_add_" + tag, parts, got)
    quad = _swap_with_chips("rs_chips_" + tag, pair)
    return _sum_parts("rs_sum_" + tag, quad)


def _adamw(name, g_parts, w, m, v, tr=512):
    P, R, C = g_parts.shape
    tr = _tile(R, tr, 8)

    def body(g_ref, w_ref, m_ref, v_ref, go_ref, d_ref, mo_ref, vo_ref):
        g = g_ref[0]
        for p in range(1, P):
            g = g + g_ref[p]
        mn = ADAM_B1 * m_ref[...] + (1.0 - ADAM_B1) * g
        vn = ADAM_B2 * v_ref[...] + (1.0 - ADAM_B2) * (g * g)
        m_hat = mn / (1.0 - ADAM_B1 ** ADAM_STEP)
        v_hat = vn / (1.0 - ADAM_B2 ** ADAM_STEP)
        go_ref[...] = g
        d_ref[...] = -ADAM_LR * (m_hat / (jnp.sqrt(v_hat) + ADAM_EPS) + ADAM_WD * w_ref[...])
        mo_ref[...] = mn
        vo_ref[...] = vn

    row = pl.BlockSpec((tr, C), lambda i: (i, 0))
    return pl.pallas_call(
        body, name=name, grid=(R // tr,),
        in_specs=[pl.BlockSpec((P, tr, C), lambda i: (0, i, 0)), row, row, row], out_specs=[row] * 4,
        out_shape=[jax.ShapeDtypeStruct((R, C), F32)] * 4,
        compiler_params=_cparams(("parallel",)),
    )(g_parts, w, m, v)


def _pad_rows(t, rows):
    return jnp.pad(t, ((0, rows - t.shape[0]), (0, 0)))


class _Layout:
    def __init__(self, D, ff_shard, in_shard, kv_shard, gate_shard, br_in, br_shard, out_shard):
        self.D = D
        self.in_shard = in_shard
        self.in_pad = -(-in_shard // LANE) * LANE
        self.in_cols = -(-N_DEV * in_shard // IN_TILE) * IN_TILE
        self.br_in, self.br_shard = br_in, br_shard
        br_rows = br_shard * br_in // D
        sizes = [("g1", ff_shard), ("u1", ff_shard), ("d1", ff_shard), ("win", self.in_pad), ("kv", kv_shard),
                 ("gate", gate_shard), ("br", br_rows), ("out", out_shard),
                 ("g2", ff_shard), ("u2", ff_shard), ("d2", ff_shard)]
        self.seg, off = {}, 0
        for key, n in sizes:
            assert n % SUBLANE_BF16 == 0, (key, n)
            self.seg[key] = (off, n)
            off += n
        self.rows = off

    def pack(self, parts):
        return jnp.concatenate([parts[key] for key in self.seg], axis=0)

    def take(self, gathered, key, own=None):
        off, n = self.seg[key]
        seg = gathered[:, off:off + n, :]
        if own is not None:
            seg = lax.dynamic_update_slice(seg, own[0][off:off + n][None], (own[1], 0, 0))
        return seg.reshape(N_DEV * n, self.D)

    def spread(self, full, key):
        _, n = self.seg[key]
        return full.reshape(N_DEV, n, self.D)


def _pack_layer(lay, l, p):
    D = lay.D
    br = jnp.concatenate([p["w_br_sb"][l], p["w_br_fox"][l], p["w_br_mem"][l]], axis=0)
    parts = {
        "g1": p["ffn1_w_gate"][l].T, "u1": p["ffn1_w_up"][l].T, "d1": p["ffn1_w_down"][l],
        "win": _pad_rows(p["w_in"][l].T, lay.in_pad), "kv": p["w_mem_kv"][l], "gate": p["w_gate"][l].T,
        "br": br.T.reshape(-1, D), "out": p["w_out"][l],
        "g2": p["ffn2_w_gate"][l].T, "u2": p["ffn2_w_up"][l].T, "d2": p["ffn2_w_down"][l],
    }
    return lay.pack({k: t.astype(BF16) for k, t in parts.items()})


def _align_win(lay, packed):
    D = lay.D
    real = packed.reshape(N_DEV, lay.in_pad, D)[:, :lay.in_shard].reshape(N_DEV * lay.in_shard, D)
    rows = jnp.concatenate([real[:_QKV_W], real[_QKV_W + N_FOX_HEADS:], real[_QKV_W:_QKV_W + N_FOX_HEADS]], axis=0)
    return _pad_rows(rows, lay.in_cols)


def _unalign_win(lay, aligned):
    D = lay.D
    n_real = N_DEV * lay.in_shard
    mem_w = n_real - _QKV_W - N_FOX_HEADS
    real = jnp.concatenate([aligned[:_QKV_W], aligned[_QKV_W + mem_w:n_real], aligned[_QKV_W:_QKV_W + mem_w]], axis=0)
    real = real.reshape(N_DEV, lay.in_shard, D)
    return jnp.pad(real, ((0, 0), (0, lay.in_pad - lay.in_shard), (0, 0))).reshape(N_DEV * lay.in_pad, D)


def _unpack_layer(lay, gathered, own=None):
    D = lay.D
    w = {k: lay.take(gathered, k, own) for k in ("g1", "u1", "d1", "kv", "out", "g2", "u2", "d2")}
    w["win"] = _align_win(lay, lay.take(gathered, "win", own))
    fl0 = N_DEV * lay.in_shard - N_FOX_HEADS
    w["wfl"] = w["win"][fl0:fl0 + LANE]
    gate = lay.take(gathered, "gate", own)
    w["gate"] = gate
    w["gate3"] = [gate[i * D:(i + 1) * D] for i in range(3)]
    br = lay.take(gathered, "br", own).reshape(N_DEV * lay.br_shard, lay.br_in)
    third = lay.br_in // 3
    w["br3"] = [br[:, i * third:(i + 1) * third] for i in range(3)]
    return w


def _silu_mul(accs, _):
    a, b = accs
    return [a, b, a * jax.nn.sigmoid(a) * b]


def _act_bwd(accs, extras):
    ds, (a, b) = accs[0], [e.astype(F32) for e in extras]
    sig = jax.nn.sigmoid(a)
    return [ds * b * (sig * (1.0 + a * (1.0 - sig))), ds * (a * sig)]


def _res_norm(scale):
    def epilogue(accs, extras):
        f, res, g = accs[0], extras[0], extras[1]
        out = res + scale * ((f * _rstd(f)) * g)
        return [f, out] + [(out * _rstd(out)) * g_next for g_next in extras[2:]]
    return epilogue


def _down_proj(name, x, w, res, g, scale, next_g):
    D = res.shape[1]
    extras = [(res, 0), (g.reshape(1, D), 0)] + ([(next_g.reshape(1, D), 0)] if next_g is not None else [])
    outs = _mm(name, [(x, w)], "nn", [F32, F32] + [BF16] * (len(extras) - 2), _res_norm(scale), extras, tn=D)
    return outs[0], outs[1], (outs[2] if next_g is not None else None)


def _rms_grad(x, g, dy):
    r = _rstd(x)
    xhat = x * r
    gy = dy * g
    dx = r * (gy - xhat * jnp.mean(gy * xhat, axis=-1, keepdims=True))
    part = jnp.sum(dy * xhat, axis=0, keepdims=True)
    first = lax.broadcasted_iota(jnp.int32, (8, part.shape[1]), 0) == 0
    return dx, jnp.where(first, part, 0.0)


def _norm_bwd(next_scale):
    def epilogue(accs, extras):
        dx, part = _rms_grad(extras[0], extras[2], _sum_accs(accs, None)[0])
        dh = extras[1] + dx
        if len(extras) == 3:
            return [dh, part]
        return [dh, part, *_rms_grad(extras[3], extras[4], next_scale * dh)]
    return epilogue


def _up_grad(name, pairs, x, res, g, nxt, tm=256):
    D = x.shape[1]
    extras = [(x, 0), (res, 0), (g.reshape(1, D), 0)]
    dtypes, rows = [F32, F32], [None, 8]
    if nxt is not None:
        extras += [(nxt[0], 0), (nxt[1].reshape(1, D), 0)]
        dtypes, rows = dtypes + [BF16, F32], rows + [None, 8]
    outs = _mm(name, pairs, "nn", dtypes, _norm_bwd(nxt[2] if nxt is not None else None), extras, tm=tm, tn=D,
               out_rows=rows)
    dg = _colsum(name + "_dg", outs[1])
    if nxt is None:
        return outs[0], dg, None, None
    return outs[0], dg, outs[2], _colsum(name + "_dg2", outs[3])


def _ffn_fwd(tag, h, n, post_g, wg, wu, wd, next_g):
    a, b, s = _mm("ffn_up_" + tag, [(n, wg), (n, wu)], "nt", [BF16, BF16, BF16], _silu_mul, tn=1408)
    f, out, n_next = _down_proj("ffn_down_" + tag, s, wd, h, post_g, 0.5, next_g)
    return out, n_next, (h, n, a, b, s, f)


def _ffn_bwd(tag, dh, df, saved, pre_g, wg, wu, wd, nxt):
    h, n, a, b, s, f = saved
    da, db = _mm("ffn_dact_" + tag, [(df, wd)], "nt", [BF16, BF16], _act_bwd, [(a, 0), (b, 0)], tn=1408)
    d_wd = _mm("ffn_dwd_" + tag, [(s, df)], "tn", [BF16], tm=256)
    dh_in, d_pre, d_next, dg_next = _up_grad("ffn_dn_" + tag, [(da, wg), (db, wu)], h, dh, pre_g, nxt)
    d_wg = _mm("ffn_dwg_" + tag, [(da, n)], "tn", [BF16], tm=256)
    d_wu = _mm("ffn_dwu_" + tag, [(db, n)], "tn", [BF16], tm=256)
    return dh_in, d_pre, d_next, dg_next, d_wg, d_wu, d_wd


_SB_W = N_SB_HEADS * HEAD_DIM
_FOX_W = N_FOX_HEADS * HEAD_DIM
_QKV_W = 3 * _SB_W + 3 * _FOX_W


def _gate_act(accs, extras):
    return [jax.nn.sigmoid(accs[0] + extras[0])]


def _merge(accs, extras):
    g = [e.astype(F32) for e in extras]
    return [g[0] * accs[0] + g[1] * accs[1] + g[2] * accs[2]]


def _merge_bwd(accs, extras):
    dm = accs[0]
    g = [e.astype(F32) for e in extras]
    d_branch = [dm * gi for gi in g]
    d_gate = [dm * bi * gi * (1.0 - gi) for bi, gi in zip(accs[1:], g)]
    return d_branch + d_gate


def _mix_tiles(lay):
    sb, fx = _SB_W // LANE, _FOX_W // LANE
    mem_w = N_DEV * lay.in_shard - _QKV_W - N_FOX_HEADS
    return (0, sb, 2 * sb, sb), (3 * sb, 3 * sb + fx, 3 * sb + 2 * fx, fx), (_QKV_W // LANE, mem_w // LANE)


def _mix_fwd(lay, h, u, w, post_g, b_forget, b_gate, mem_n, next_g):
    D = lay.D
    (sq, sk, sv, sn), (fq, fk, fv, fn), (mq, mn) = _mix_tiles(lay)
    mem_d = mn * LANE // N_MEM_HEADS
    proj = _mm("mix_in", [(u, w["win"])], "nt", [BF16], tm=1024, tn=IN_TILE)
    fl = _mm("mix_fl", [(u, w["wfl"])], "nt", [F32])[:, :N_FOX_HEADS].T
    c = _decay_fwd(fl, b_forget.reshape(-1, 1))
    o_sb, rtot = _sb_fwd((proj, sq), (proj, sk), (proj, sv), sn, HEAD_DIM, HEAD_DIM ** -0.5)
    o_fx32, o_fx, lse_fx = _attn_fwd("fox_fwd", (proj, fq), (proj, fk), (proj, fv), fn, HEAD_DIM,
                                     HEAD_DIM ** -0.5, c)
    kvm = _mm("mem_kv", [(mem_n, w["kv"])], "nn", [BF16])
    o_mem32, o_mem, lse_mem = _attn_fwd("mem_fwd", (proj, mq), (kvm, 0), (kvm, mn), mn, mem_d, mem_d ** -0.5)
    gates = _mm("mix_gate", [(u, w["gate"])], "nt", [BF16], _gate_act, [(b_gate.reshape(1, -1), 0)], tm=1024)
    flat = [o_sb, o_fx, o_mem]
    merged = _mm("mix_merge", list(zip(flat, w["br3"])), "nt", [BF16], _merge,
                 [(gates, 0), (gates, D), (gates, 2 * D)])
    z, out, n_next = _down_proj("mix_out", merged, w["out"], h, post_g, 1.0, next_g)
    saved = (h, u, proj, fl, c, rtot, o_fx32, lse_fx, kvm, o_mem32, lse_mem, gates, flat, merged, z)
    return out, n_next, saved


def _mix_bwd(lay, dh, dz, saved, w, pre_g, b_forget, mem_n, dmem_n, nxt):
    D = lay.D
    (sq, sk, sv, sn), (fq, fk, fv, fn), (mq, mn) = _mix_tiles(lay)
    mem_d = mn * LANE // N_MEM_HEADS
    h, u, proj, fl, c, rtot, o_fx32, lse_fx, kvm, o_mem32, lse_mem, gates, flat, merged, z = saved
    outs = _mm("mix_dmerge", [(dz, w["out"])] + list(zip(flat, w["br3"])), "nt", [BF16] * 6, _merge_bwd,
               [(gates, 0), (gates, D), (gates, 2 * D)], tn=512)
    d_branch, d_gate = outs[:3], outs[3:]
    d_wout = _mm("mix_dwout", [(merged, dz)], "tn", [BF16])
    d_o = [_mm("mix_dbr%d" % i, [(d_branch[i], w["br3"][i])], "nn", [BF16]) for i in range(3)]
    d_wbr = [_mm("mix_dwbr%d" % i, [(d_branch[i], flat[i])], "tn", [BF16]) for i in range(3)]
    d_bgate = jnp.concatenate([_colsum("mix_dbgate%d" % i, d_gate[i]) for i in range(3)])
    d_wgate = [_mm("mix_dwgate%d" % i, [(d_gate[i], u)], "tn", [BF16]) for i in range(3)]

    d_sb = _sb_bwd((proj, sq), (proj, sk), (proj, sv), d_o[0], rtot, sn, HEAD_DIM, HEAD_DIM ** -0.5)
    *d_fx, dc, dc_rows = _attn_bwd("fox_bwd", (proj, fq), (proj, fk), (proj, fv), o_fx32, d_o[1], lse_fx, fn,
                                   HEAD_DIM, HEAD_DIM ** -0.5, c)
    dq_m, dk_m, dv_m = _attn_bwd("mem_bwd", (proj, mq), (kvm, 0), (kvm, mn), o_mem32, d_o[2], lse_mem, mn,
                                 mem_d, mem_d ** -0.5)
    dfl, d_bforget = _decay_bwd(dc, dc_rows, fl, b_forget.reshape(-1, 1))
    pieces = list(d_sb) + list(d_fx) + [dq_m]
    dflp = jnp.pad(dfl.T.astype(BF16), ((0, 0), (0, LANE - dfl.shape[0])))
    offs = [sum(t.shape[1] for t in pieces[:i]) for i in range(len(pieces) + 1)]
    win_rows = [w["win"][offs[i]:offs[i + 1]] for i in range(len(pieces))]
    dh_in, d_pre, d_next, dg_next = _up_grad(
        "mix_du", list(zip(d_gate, w["gate3"])) + list(zip(pieces, win_rows)) + [(dflp, w["wfl"])], h, dh, pre_g, nxt)
    d_rows = [_mm("mix_dwin%d" % i, [(t, u)], "tn", [BF16]) for i, t in enumerate(pieces)]
    d_wfl = _mm("mix_dwfl", [(dflp, u)], "tn", [BF16])
    d_win = _unalign_win(lay, _pad_rows(jnp.concatenate(list(d_rows) + [d_wfl], axis=0), lay.in_cols))

    dkvm = jnp.concatenate([dk_m, dv_m], axis=1)
    d_wkv = _mm("mem_dwkv", [(mem_n, dkvm)], "tn", [BF16])
    dmem_n = _mm("mem_dn", [(dkvm, w["kv"])], "nt", [F32], lambda accs, ex: [accs[0] + ex[0]], [(dmem_n, 0)])
    grads = {"win": d_win, "kv": d_wkv, "gate": jnp.concatenate(d_wgate, axis=0),
             "br": jnp.concatenate(d_wbr, axis=1), "out": d_wout}
    return dh_in, d_pre, d_next, dg_next, d_bforget, d_bgate, grads, dmem_n


def _layer_fwd(lay, h, n, w, sp, mem_n, next_g):
    h1, u, s1 = _ffn_fwd("1", h, n, sp["ffn1_post_g"], w["g1"], w["u1"], w["d1"], sp["mix_pre_g"])
    h2, n2, s2 = _mix_fwd(lay, h1, u, w, sp["mix_post_g"], sp["b_forget"], sp["b_gate"], mem_n, sp["ffn2_pre_g"])
    h3, n_next, s3 = _ffn_fwd("2", h2, n2, sp["ffn2_post_g"], w["g2"], w["u2"], w["d2"], next_g)
    return h3, n_next, (s1, s2, s3)


def _layer_bwd(lay, dh, df2, d_post2, saved, w, sp, mem_n, dmem_n, below):
    s1, s2, s3 = saved
    dh, d_pre2, dz, d_mpost, d_g2, d_u2, d_d2 = _ffn_bwd(
        "2", dh, df2, s3, sp["ffn2_pre_g"], w["g2"], w["u2"], w["d2"], (s2[-1], sp["mix_post_g"], 1.0))
    dh, d_mpre, df1, d_post1, d_bforget, d_bgate, g, dmem_n = _mix_bwd(
        lay, dh, dz, s2, w, sp["mix_pre_g"], sp["b_forget"], mem_n, dmem_n, (s1[5], sp["ffn1_post_g"], 0.5))
    dh, d_pre1, df_below, d_post_below, d_g1, d_u1, d_d1 = _ffn_bwd(
        "1", dh, df1, s1, sp["ffn1_pre_g"], w["g1"], w["u1"], w["d1"], below)
    g.update({"g1": d_g1, "u1": d_u1, "d1": d_d1, "g2": d_g2, "u2": d_u2, "d2": d_d2})
    g["br"] = g["br"].reshape(N_DEV, lay.br_shard, lay.br_in).reshape(-1, lay.D)
    packed = jnp.concatenate([lay.spread(g[key], key) for key in lay.seg], axis=1)
    small = {"ffn1_pre_g": d_pre1, "ffn1_post_g": d_post1, "mix_pre_g": d_mpre, "mix_post_g": d_mpost,
             "ffn2_pre_g": d_pre2, "ffn2_post_g": d_post2, "b_gate": d_bgate, "b_forget": d_bforget}
    return dh, df_below, d_post_below, packed, small, dmem_n


_SHARDED = ["ffn1_w_gate", "ffn1_w_up", "ffn1_w_down", "w_in", "w_mem_kv", "w_gate", "w_br_sb", "w_br_fox",
            "w_br_mem", "w_out", "ffn2_w_gate", "ffn2_w_up", "ffn2_w_down"]
_SMALL_LAYER = ["ffn1_pre_g", "ffn1_post_g", "mix_pre_g", "mix_post_g", "ffn2_pre_g", "ffn2_post_g", "b_gate",
                "b_forget"]
_WEIGHTS = ["ffn1_pre_g", "ffn1_post_g", "ffn1_w_gate", "ffn1_w_up", "ffn1_w_down", "mix_pre_g", "mix_post_g",
            "w_in", "b_forget", "mem_norm_g", "w_mem_kv", "w_gate", "b_gate", "w_br_sb", "w_br_fox", "w_br_mem",
            "w_out", "ffn2_pre_g", "ffn2_post_g", "ffn2_w_gate", "ffn2_w_up", "ffn2_w_down"]


def _pack_small(vals, L, D):
    rows = []
    for l in range(L):
        for name in _SMALL_LAYER:
            t = vals[name][l]
            rows.append(jnp.pad(t, (0, -t.shape[0] % D)).reshape(-1, D))
    rows.append(vals["mem_norm_g"].reshape(1, D))
    packed = jnp.concatenate(rows, axis=0)
    return _pad_rows(packed, -(-packed.shape[0] // 8) * 8)


def _unpack_small(packed, shapes, L, D):
    out = {name: [] for name in _SMALL_LAYER}
    r = 0
    for l in range(L):
        for name in _SMALL_LAYER:
            n = shapes[name][1]
            nr = -(-n // D)
            out[name].append(packed[r:r + nr].reshape(-1)[:n])
            r += nr
    res = {name: jnp.stack(v) for name, v in out.items()}
    res["mem_norm_g"] = packed[r]
    return res


def _unpack_grads(lay, g):
    def seg(key):
        off, n = lay.seg[key]
        return g[off:off + n]
    br = seg("br").reshape(lay.br_shard, lay.br_in).T
    third = lay.br_in // 3
    return {
        "ffn1_w_gate": seg("g1").T, "ffn1_w_up": seg("u1").T, "ffn1_w_down": seg("d1"),
        "w_in": seg("win")[:lay.in_shard].T, "w_mem_kv": seg("kv"), "w_gate": seg("gate").T,
        "w_br_sb": br[:third], "w_br_fox": br[third:2 * third], "w_br_mem": br[2 * third:],
        "w_out": seg("out"), "ffn2_w_gate": seg("g2").T, "ffn2_w_up": seg("u2").T, "ffn2_w_down": seg("d2"),
    }


class _Exchanges:
    def gather(self, name, block):
        return _all_gather(name, block)

    def gather_start(self, block):
        return _exchange_start("ag_start", block, per_peer=False)

    def gather_wait(self, started, after):
        block, landed = _exchange_wait("ag_wait", started, after)
        return landed, (block, _my_index())

    def scatter(self, parts):
        return _reduce_scatter("w", parts)

    def scatter_start(self, parts):
        return _exchange_start("rs_start", parts, per_peer=True)

    def scatter_wait(self, started, after):
        parts, landed = _exchange_wait("rs_wait", started, after)
        return _sum_landed("rs_sum8", landed, parts)

    def token(self, started):
        return started[4]

    def loss_sum(self, part):
        return lax.psum(part, ("x", "y", "c"))


def _step(p, m, v, x, mem, tgt, ex):
    L, D = p["ffn1_pre_g"].shape
    lay = _Layout(D, p["ffn1_w_gate"].shape[2], p["w_in"].shape[2], p["w_mem_kv"].shape[1], p["w_gate"].shape[2],
                  3 * p["w_br_sb"].shape[1], p["w_br_sb"].shape[2], p["w_out"].shape[1])
    blocks = [_pack_layer(lay, l, p) for l in range(L)]
    sps = [{name: p[name][l] for name in _SMALL_LAYER} for l in range(L)]

    mem_n = _rms_fwd("mem_norm", mem, p["mem_norm_g"], BF16)
    gathered, own = ex.gather("ag_weights", blocks[0]), None
    h, saved, ws = x, [], []
    n = _rms_fwd("first_norm", x, sps[0]["ffn1_pre_g"], BF16)
    for l in range(L):
        if l + 1 < L:
            nxt, gathered = lax.optimization_barrier((blocks[l + 1], gathered))
            started = ex.gather_start(nxt)
            sp = _after(sps[l], "ffn1_post_g", ex.token(started))
        else:
            sp = sps[l]
        ws.append(_unpack_layer(lay, gathered, own))
        h, n, s = _layer_fwd(lay, h, n, ws[l], sp, mem_n, sps[l + 1]["ffn1_pre_g"] if l + 1 < L else None)
        saved.append(s)
        if l + 1 < L:
            gathered, own = ex.gather_wait(started, h)
    loss_part, dh = _loss_grad(h, tgt)
    loss = ex.loss_sum(loss_part)

    dmem_n = jnp.zeros(mem.shape, F32)
    big, small = [None] * L, {name: [None] * L for name in _SMALL_LAYER}
    flying, token = {}, None
    df, d_post = _rms_bwd("last_dout", saved[L - 1][2][5], sps[L - 1]["ffn2_post_g"], dh, BF16, scale=0.5)
    for l in reversed(range(L)):
        sp = sps[l] if token is None else _after(sps[l], "ffn2_pre_g", token)
        below = (saved[l - 1][2][5], sps[l - 1]["ffn2_post_g"], 0.5) if l > 0 else None
        dh, df, d_post, packed, sm, dmem_n = _layer_bwd(lay, dh, df, d_post, saved[l], ws[l], sp, mem_n, dmem_n,
                                                         below)
        if l > 0:
            flying[l] = ex.scatter_start(packed)
            token = ex.token(flying[l])
        else:
            big[l] = _unpack_grads(lay, ex.scatter(packed))
        for name in _SMALL_LAYER:
            small[name][l] = sm[name]
    for l, started in flying.items():
        big[l] = _unpack_grads(lay, ex.scatter_wait(started, dh))
    _, d_memg = _rms_bwd("mem_dnorm", mem, p["mem_norm_g"], dmem_n, F32)

    small_g = {name: jnp.stack(vs) for name, vs in small.items()}
    small_g["mem_norm_g"] = d_memg
    small_names = _SMALL_LAYER + ["mem_norm_g"]
    shapes = {name: p[name].shape for name in small_names}
    g_all = ex.gather("ag_small", _pack_small(small_g, L, D))
    packs = [_pack_small({name: t[name] for name in small_names}, L, D) for t in (p, m, v)]
    res = [_unpack_small(t, shapes, L, D) for t in _adamw("adamw_small", g_all, *packs)]

    out = {kind: {} for kind in ("grad", "delta", "new_m", "new_v")}
    for name in small_names:
        for kind, r in zip(("grad", "delta", "new_m", "new_v"), res):
            out[kind][name] = r[name].reshape(p[name].shape)
    for name in _SHARDED:
        g = jnp.stack([big[l][name] for l in range(L)])
        shp = g.shape
        flat = lambda t: t.reshape(-1, shp[-1])
        r = _adamw("adamw_" + name, flat(g)[None], flat(p[name]), flat(m[name]), flat(v[name]))
        for kind, t in zip(("grad", "delta", "new_m", "new_v"), r):
            out[kind][name] = t.reshape(shp)
    return loss, dh, out


def kernel(x, mem, ffn1_pre_g, ffn1_post_g, ffn1_w_gate, ffn1_w_up, ffn1_w_down, mix_pre_g, mix_post_g, w_in, b_forget, mem_norm_g, w_mem_kv, w_gate, b_gate, w_br_sb, w_br_fox, w_br_mem, w_out, ffn2_pre_g, ffn2_post_g, ffn2_w_gate, ffn2_w_up, ffn2_w_down, loss_target, m_ffn1_pre_g, m_ffn1_post_g, m_ffn1_w_gate, m_ffn1_w_up, m_ffn1_w_down, m_mix_pre_g, m_mix_post_g, m_w_in, m_b_forget, m_mem_norm_g, m_w_mem_kv, m_w_gate, m_b_gate, m_w_br_sb, m_w_br_fox, m_w_br_mem, m_w_out, m_ffn2_pre_g, m_ffn2_post_g, m_ffn2_w_gate, m_ffn2_w_up, m_ffn2_w_down, v_ffn1_pre_g, v_ffn1_post_g, v_ffn1_w_gate, v_ffn1_w_up, v_ffn1_w_down, v_mix_pre_g, v_mix_post_g, v_w_in, v_b_forget, v_mem_norm_g, v_w_mem_kv, v_w_gate, v_b_gate, v_w_br_sb, v_w_br_fox, v_w_br_mem, v_w_out, v_ffn2_pre_g, v_ffn2_post_g, v_ffn2_w_gate, v_ffn2_w_up, v_ffn2_w_down):
    p = dict(zip(_WEIGHTS, (ffn1_pre_g, ffn1_post_g, ffn1_w_gate, ffn1_w_up, ffn1_w_down, mix_pre_g, mix_post_g, w_in, b_forget, mem_norm_g, w_mem_kv, w_gate, b_gate, w_br_sb, w_br_fox, w_br_mem, w_out, ffn2_pre_g, ffn2_post_g, ffn2_w_gate, ffn2_w_up, ffn2_w_down)))
    m = dict(zip(_WEIGHTS, (m_ffn1_pre_g, m_ffn1_post_g, m_ffn1_w_gate, m_ffn1_w_up, m_ffn1_w_down, m_mix_pre_g, m_mix_post_g, m_w_in, m_b_forget, m_mem_norm_g, m_w_mem_kv, m_w_gate, m_b_gate, m_w_br_sb, m_w_br_fox, m_w_br_mem, m_w_out, m_ffn2_pre_g, m_ffn2_post_g, m_ffn2_w_gate, m_ffn2_w_up, m_ffn2_w_down)))
    v = dict(zip(_WEIGHTS, (v_ffn1_pre_g, v_ffn1_post_g, v_ffn1_w_gate, v_ffn1_w_up, v_ffn1_w_down, v_mix_pre_g, v_mix_post_g, v_w_in, v_b_forget, v_mem_norm_g, v_w_mem_kv, v_w_gate, v_b_gate, v_w_br_sb, v_w_br_fox, v_w_br_mem, v_w_out, v_ffn2_pre_g, v_ffn2_post_g, v_ffn2_w_gate, v_ffn2_w_up, v_ffn2_w_down)))
    loss, dx, out = _step(p, m, v, x[0], mem[0], loss_target[0], _Exchanges())
    return (loss, dx[None], *[out["grad"][n] for n in _WEIGHTS], *[out["delta"][n] for n in _WEIGHTS],
            *[out["new_m"][n] for n in _WEIGHTS], *[out["new_v"][n] for n in _WEIGHTS])
```

```python
import functools
import math

import jax
import jax.numpy as jnp
from jax import lax
from jax.experimental import pallas as pl
from jax.experimental.pallas import tpu as pltpu

F32 = jnp.float32
BF16 = jnp.bfloat16

LANE = 128
SUBLANE_BF16 = 16
VMEM_LIMIT = 56 * 1024 * 1024
N_DEV = 8
MESH = pl.DeviceIdType.MESH
ANY = pl.BlockSpec(memory_space=pl.ANY)

RMS_EPS = 1e-6
HEAD_DIM = 64
N_SB_HEADS = 8
N_FOX_HEADS = 8
N_MEM_HEADS = 4
NEG = -1e30
ATT_TQ = 1024
ATT_TK = 256
DECAY_TK = 128
IN_TILE = 1280

ADAM_LR = 0.001
ADAM_B1 = 0.9
ADAM_B2 = 0.999
ADAM_EPS = 1e-08
ADAM_WD = 0.01
ADAM_STEP = 10


def _tile(n, target, mult=LANE):
    best = None
    for t in range(mult, min(n, target) + 1, mult):
        if n % t == 0:
            best = t
    return best if best is not None else n


def _cparams(sem):
    return pltpu.CompilerParams(dimension_semantics=sem, vmem_limit_bytes=VMEM_LIMIT)


_DIMS = {"nn": (((1,), (0,)), ((), ())), "nt": (((1,), (1,)), ((), ())), "tn": (((0,), (0,)), ((), ()))}


def _dot(a, b, mode="nn"):
    return lax.dot_general(a.astype(BF16), b.astype(BF16), _DIMS[mode], preferred_element_type=F32)


def _mm(name, pairs, mode, out_dtypes, epilogue=None, extras=(), tm=512, tn=1024, out_rows=None):
    a0, b0 = pairs[0]
    M = a0.shape[1] if mode == "tn" else a0.shape[0]
    N = b0.shape[0] if mode == "nt" else b0.shape[1]
    tm = _tile(M, tm)
    tn = _tile(N, tn)
    np_, ne, no = len(pairs), len(extras), len(out_dtypes)

    def body(*refs):
        a_refs, b_refs = refs[:np_], refs[np_:2 * np_]
        e_refs = refs[2 * np_:2 * np_ + ne]
        o_refs = refs[2 * np_ + ne:]
        accs = [_dot(a[...], b[...], mode) for a, b in zip(a_refs, b_refs)]
        outs = epilogue(accs, [e[...] for e in e_refs]) if epilogue is not None else accs
        for o, val in zip(o_refs, outs):
            o[...] = val.astype(o.dtype)

    in_specs = []
    for a, _ in pairs:
        if mode == "tn":
            in_specs.append(pl.BlockSpec((a.shape[0], tm), lambda j, i: (0, i)))
        else:
            in_specs.append(pl.BlockSpec((tm, a.shape[1]), lambda j, i: (i, 0)))
    for _, b in pairs:
        if mode == "nt":
            in_specs.append(pl.BlockSpec((tn, b.shape[1]), lambda j, i: (j, 0)))
        else:
            in_specs.append(pl.BlockSpec((b.shape[0], tn), lambda j, i: (0, j)))
    for e, off in extras:
        if e.shape[0] == 1:
            in_specs.append(pl.BlockSpec((1, tn), functools.partial(lambda j, i, o: (0, j + o), o=off // tn)))
        else:
            in_specs.append(pl.BlockSpec((tm, tn), functools.partial(lambda j, i, o: (i, j + o), o=off // tn)))
    rows = [tm if r is None else r for r in (out_rows or [None] * no)]
    out_specs = [pl.BlockSpec((r, tn), lambda j, i: (i, j)) for r in rows]
    outs = pl.pallas_call(
        body, name=name, grid=(N // tn, M // tm),
        in_specs=in_specs, out_specs=out_specs,
        out_shape=[jax.ShapeDtypeStruct((M // tm * r, N), dt) for r, dt in zip(rows, out_dtypes)],
        compiler_params=_cparams(("parallel", "parallel")),
    )(*[a for a, _ in pairs], *[b for _, b in pairs], *[e for e, _ in extras])
    return outs[0] if no == 1 else outs


def _sum_accs(accs, _):
    total = accs[0]
    for acc in accs[1:]:
        total = total + acc
    return [total]


def _rstd(x):
    return lax.rsqrt(jnp.mean(x * x, axis=-1, keepdims=True) + RMS_EPS)


def _rms_fwd(name, x, g, out_dtype, tr=512):
    R, D = x.shape
    tr = _tile(R, tr, 8)

    def body(x_ref, g_ref, o_ref):
        xv = x_ref[...]
        o_ref[...] = ((xv * _rstd(xv)) * g_ref[...]).astype(o_ref.dtype)

    row = pl.BlockSpec((tr, D), lambda i: (i, 0))
    return pl.pallas_call(
        body, name=name, grid=(R // tr,),
        in_specs=[row, pl.BlockSpec((1, D), lambda i: (0, 0))], out_specs=row,
        out_shape=jax.ShapeDtypeStruct((R, D), out_dtype),
        compiler_params=_cparams(("parallel",)),
    )(x, g.reshape(1, D))


def _rms_bwd(name, x, g, dy, out_dtype, scale=1.0, res=None, tr=512):
    R, D = x.shape
    tr = _tile(R, tr, 8)
    has_res = res is not None

    def body(*refs):
        x_ref, g_ref, dy_ref = refs[:3]
        dx_ref, dg_ref = refs[-2:]
        i = pl.program_id(0)
        xv = x_ref[...]
        xhat = xv * _rstd(xv)
        dyv = dy_ref[...].astype(F32) * scale
        gy = dyv * g_ref[...]
        dx = _rstd(xv) * (gy - xhat * jnp.mean(gy * xhat, axis=-1, keepdims=True))
        if has_res:
            dx = refs[3][...] + dx
        dx_ref[...] = dx.astype(dx_ref.dtype)
        part = jnp.sum(dyv * xhat, axis=0, keepdims=True)

        @pl.when(i == 0)
        def _():
            dg_ref[...] = part

        @pl.when(i > 0)
        def _():
            dg_ref[...] += part

    row = pl.BlockSpec((tr, D), lambda i: (i, 0))
    gain = pl.BlockSpec((1, D), lambda i: (0, 0))
    dx, dg = pl.pallas_call(
        body, name=name, grid=(R // tr,),
        in_specs=[row, gain, row] + ([row] if has_res else []), out_specs=[row, gain],
        out_shape=[jax.ShapeDtypeStruct((R, D), out_dtype), jax.ShapeDtypeStruct((1, D), F32)],
        compiler_params=_cparams(("arbitrary",)),
    )(x, g.reshape(1, D), dy, *([res] if has_res else []))
    return dx, dg[0]


def _loss_grad(y, tgt, tr=512):
    R, D = y.shape
    tr = _tile(R, tr, 8)

    def body(y_ref, t_ref, dy_ref, loss_ref):
        i = pl.program_id(0)
        d = y_ref[...] - t_ref[...]
        dy_ref[...] = d / D
        part = 0.5 * jnp.sum(jnp.mean(d * d, axis=-1, keepdims=True), axis=0, keepdims=True)
        tile = jnp.broadcast_to(part, loss_ref.shape)

        @pl.when(i == 0)
        def _():
            loss_ref[...] = tile

        @pl.when(i > 0)
        def _():
            loss_ref[...] += tile

    row = pl.BlockSpec((tr, D), lambda i: (i, 0))
    dy, loss = pl.pallas_call(
        body, name="loss_grad", grid=(R // tr,),
        in_specs=[row, row], out_specs=[row, pl.BlockSpec((8, LANE), lambda i: (0, 0))],
        out_shape=[jax.ShapeDtypeStruct((R, D), F32), jax.ShapeDtypeStruct((8, LANE), F32)],
        compiler_params=_cparams(("arbitrary",)),
    )(y, tgt)
    return loss[0, 0], dy


def _colsum(name, x, tr=512, tn=1024):
    R, N = x.shape
    tr, tn = _tile(R, tr, 8), _tile(N, tn)

    def body(x_ref, o_ref):
        i = pl.program_id(1)
        part = jnp.sum(x_ref[...].astype(F32), axis=0, keepdims=True)

        @pl.when(i == 0)
        def _():
            o_ref[...] = part

        @pl.when(i > 0)
        def _():
            o_ref[...] += part

    out = pl.pallas_call(
        body, name=name, grid=(N // tn, R // tr),
        in_specs=[pl.BlockSpec((tr, tn), lambda j, i: (i, j))], out_specs=pl.BlockSpec((1, tn), lambda j, i: (0, j)),
        out_shape=jax.ShapeDtypeStruct((1, N), F32),
        compiler_params=_cparams(("parallel", "arbitrary")),
    )(x)
    return out[0]


def _tri(tk, rel):
    j = lax.broadcasted_iota(jnp.int32, (tk, tk), 0)
    s = lax.broadcasted_iota(jnp.int32, (tk, tk), 1)
    return rel(j, s).astype(BF16)


def _dot_split(x, m, parts=2):
    total = None
    rem = x
    for _ in range(parts):
        piece = rem.astype(BF16)
        rem = rem - piece.astype(F32)
        term = jnp.dot(piece, m, preferred_element_type=F32)
        total = term if total is None else total + term
    return total


def _log_not_and_beta(z, mask):
    ln = -(jnp.maximum(z, 0.0) + jnp.log(1.0 + jnp.exp(-jnp.abs(z))))
    return (ln if mask is None else jnp.where(mask, ln, 0.0)), ln + z


def _att_tiles(T, Tk, causal):
    tq = min(ATT_TQ, T)
    tk = min(ATT_TK, tq if causal else Tk)
    return tq, tk, (tq if causal else Tk) // tk


def _key_base(j, tq):
    return j * tq if isinstance(j, int) else pl.multiple_of(j * tq, tq)


def _is_pow2(scale):
    return math.log2(scale).is_integer()


def _per_head(x, hpb, d):
    if hpb == 1:
        return [x]
    lane = lax.broadcasted_iota(jnp.int32, x.shape, 1)
    return [jnp.where((lane >= h * d) & (lane < (h + 1) * d), x, jnp.zeros_like(x)) for h in range(hpb)]


def _join_heads(xs, d):
    out = xs[-1]
    if len(xs) > 1:
        lane = lax.broadcasted_iota(jnp.int32, out.shape, 1)
        for h in reversed(range(len(xs) - 1)):
            out = jnp.where(lane < (h + 1) * d, xs[h], out)
    return out


def _lane_tile(rows, off, whole):
    if whole:
        return pl.BlockSpec((rows, LANE), lambda g, i: (0, off + g))
    return pl.BlockSpec((rows, LANE), lambda g, i: (i, off + g))


def _sb_fwd(q, k, v, n_tiles, d, scale):
    T = q[0].shape[0]
    hpb = LANE // d
    tq, tk, nsub = _att_tiles(T, T, True)
    assert _is_pow2(scale)

    def body(q_ref, k_ref, v_ref, ob_ref, rt_ref, acc_ref, r_ref):
        qi = pl.program_id(1)
        qh = _per_head(q_ref[...] * scale, hpb, d)
        acc_ref[...] = jnp.zeros_like(acc_ref)
        r_ref[...] = jnp.zeros_like(r_ref)
        row = lax.broadcasted_iota(jnp.int32, (tq, tk), 0)
        col = lax.broadcasted_iota(jnp.int32, (tq, tk), 1)
        after = _tri(tk, lambda j, s: j > s)

        def walk(h, base, r0, r1, subs, diagonal):
            parts = []
            for u in subs:
                z = _dot(qh[h][r0:r1], k_ref[pl.ds(base + u * tk, tk), :], "nt")
                mask = (col[r0:r1] + u * tk) < row[r0:r1] if diagonal else None
                ln, lb = _log_not_and_beta(z, mask)
                between = _dot_split(ln, after, parts=1)
                first = ln[:, 0:1].astype(BF16).astype(F32)
                parts.append((u, lb, between, between[:, 0:1] + first, mask))
            r = r_ref[h, r0:r1, :]
            out = None
            for u, lb, between, total, mask in parts:
                w = jnp.exp(lb + between + r)
                if diagonal:
                    w = jnp.where(mask, w, 0.0)
                term = _dot(w, v_ref[pl.ds(base + u * tk, tk), :])
                out = term if out is None else out + term
                r = r + total
            acc_ref[h, r0:r1, :] += out
            r_ref[h, r0:r1, :] = r

        def step(j, diagonal):
            base = _key_base(j, tq)
            for h in range(hpb):
                if diagonal and nsub % 2 == 0:
                    walk(h, base, 0, tq // 2, range(nsub // 2 - 1, -1, -1), True)
                    walk(h, base, tq // 2, tq, range(nsub - 1, -1, -1), True)
                else:
                    walk(h, base, 0, tq, range(nsub - 1, -1, -1), diagonal)

        def below(i, carry):
            step(qi - 1 - i, False)
            return carry

        step(qi, True)
        lax.fori_loop(0, qi, below, 0)
        ob_ref[...] = _join_heads([acc_ref[h] for h in range(hpb)], d).astype(ob_ref.dtype)
        rt_ref[...] = r_ref[...]

    out = pl.BlockSpec((tq, LANE), lambda g, i: (i, g))
    col = pl.BlockSpec((hpb, tq, 1), lambda g, i: (g, i, 0))
    return pl.pallas_call(
        body, name="sb_fwd", grid=(n_tiles, T // tq),
        in_specs=[_lane_tile(tq, q[1], False), _lane_tile(T, k[1], True), _lane_tile(T, v[1], True)],
        out_specs=[out, col],
        out_shape=[jax.ShapeDtypeStruct((T, n_tiles * LANE), BF16), jax.ShapeDtypeStruct((n_tiles * hpb, T, 1), F32)],
        scratch_shapes=[pltpu.VMEM((hpb, tq, LANE), F32), pltpu.VMEM((hpb, tq, 1), F32)],
        compiler_params=_cparams(("parallel", "arbitrary")),
    )(q[0], k[0], v[0])


def _sb_bwd(q, k, v, do, rtot, n_tiles, d, scale):
    T = q[0].shape[0]
    hpb = LANE // d
    tq, tk, nsub = _att_tiles(T, T, True)
    assert _is_pow2(scale)

    def body(q_ref, k_ref, v_ref, do_ref, rt_ref, dq_ref, dk_ref, dv_ref, dk_acc, dv_acc, dq_acc, p_ref, c_ref):
        qi = pl.program_id(1)

        @pl.when(qi == 0)
        def _():
            dk_acc[...] = jnp.zeros_like(dk_acc)
            dv_acc[...] = jnp.zeros_like(dv_acc)

        qh = _per_head(q_ref[...] * scale, hpb, d)
        doh = _per_head(do_ref[...], hpb, d)
        dq_acc[...] = jnp.zeros_like(dq_acc)
        p_ref[...] = jnp.zeros_like(p_ref)
        c_ref[...] = jnp.zeros_like(c_ref)
        row = lax.broadcasted_iota(jnp.int32, (tq, tk), 0)
        col = lax.broadcasted_iota(jnp.int32, (tq, tk), 1)
        upto = _tri(tk, lambda j, s: j <= s)
        before = _tri(tk, lambda j, s: j < s)
        rt_wide = [jnp.broadcast_to(rt_ref[h], (tq, tk)) for h in range(hpb)]

        def step(j, diagonal):
            base = _key_base(j, tq)
            for h in range(hpb):
                first = []
                for u in range(nsub):
                    ks = base + u * tk
                    r0 = u * tk if diagonal else 0
                    kv = k_ref[pl.ds(ks, tk), :]
                    z = _dot(qh[h][r0:], kv, "nt")
                    mask = (col[r0:] + u * tk) < row[r0:] if diagonal else None
                    ln, lb = _log_not_and_beta(z, mask)
                    dw = _dot(doh[h][r0:], v_ref[pl.ds(ks, tk), :], "nt")
                    first.append((r0, ks, kv, mask, lb, jnp.exp(lb), _dot_split(ln, upto, parts=1), dw))
                rt, pre, cpre = rt_wide[h], p_ref[h], c_ref[h]
                dq = None
                for r0, ks, kv, mask, lb, sig, local, dw in first:
                    if diagonal and r0:
                        pre, cpre = pre[tk:], cpre[tk:]
                    prefix = local + pre
                    w = jnp.exp(lb + (rt[r0:] - prefix))
                    if diagonal:
                        w = jnp.where(mask, w, 0.0)
                    g = dw * w
                    c = _dot_split(g, before, parts=1) + cpre
                    dz = g * (1.0 - sig) - c * sig
                    if diagonal:
                        dz = jnp.where(mask, dz, 0.0)
                    term = _dot(dz, kv)
                    if diagonal:
                        dq_acc[h, r0:, :] += term
                    else:
                        dq = term if dq is None else dq + term
                    dk_acc[pl.ds(ks, tk), :] += _dot(dz, qh[h][r0:], "tn")
                    dv_acc[pl.ds(ks, tk), :] += _dot(w, doh[h][r0:], "tn")
                    pre = prefix[:, tk - 1:tk]
                    cpre = c[:, tk - 1:tk] + g[:, tk - 1:tk]
                if not diagonal:
                    dq_acc[h] += dq
                    p_ref[h] = pre
                    c_ref[h] = cpre

        def below(j, carry):
            step(j, False)
            return carry

        lax.fori_loop(0, qi, below, 0)
        step(qi, True)
        dq_ref[...] = (_join_heads([dq_acc[h] for h in range(hpb)], d) * scale).astype(dq_ref.dtype)

        @pl.when(qi == pl.num_programs(1) - 1)
        def _():
            dk_ref[...] = dk_acc[...].astype(dk_ref.dtype)
            dv_ref[...] = dv_acc[...].astype(dv_ref.dtype)

    blk = pl.BlockSpec((tq, LANE), lambda g, i: (i, g))
    full = pl.BlockSpec((T, LANE), lambda g, i: (0, g))
    col = pl.BlockSpec((hpb, tq, 1), lambda g, i: (g, i, 0))
    wide = jax.ShapeDtypeStruct((T, n_tiles * LANE), BF16)
    return pl.pallas_call(
        body, name="sb_bwd", grid=(n_tiles, T // tq),
        in_specs=[_lane_tile(tq, q[1], False), _lane_tile(T, k[1], True), _lane_tile(T, v[1], True), blk, col],
        out_specs=[blk, full, full], out_shape=[wide, wide, wide],
        scratch_shapes=[pltpu.VMEM((T, LANE), F32), pltpu.VMEM((T, LANE), F32), pltpu.VMEM((hpb, tq, LANE), F32),
                        pltpu.VMEM((hpb, tq, 1), F32), pltpu.VMEM((hpb, tq, 1), F32)],
        compiler_params=_cparams(("parallel", "arbitrary")),
    )(q[0], k[0], v[0], do, rtot)


def _attn_fwd(name, q, k, v, n_tiles, d, scale, c=None):
    T, Tk = q[0].shape[0], k[0].shape[0]
    hpb = LANE // d
    H = n_tiles * hpb
    causal = c is not None
    tq, tk, nsub = _att_tiles(T, Tk, causal)
    fold = _is_pow2(scale)

    def body(*refs):
        q_ref, k_ref, v_ref = refs[:3]
        cc_ref, cr_ref = refs[3:5] if causal else (None, None)
        o_ref, ob_ref, lse_ref, m_ref, l_ref, acc_ref = refs[-6:]
        qi = pl.program_id(1)
        qh = _per_head(q_ref[...] * scale if fold else q_ref[...], hpb, d)
        bias = [jnp.broadcast_to(cc_ref[h], (tq, tk)) for h in range(hpb)] if causal else None
        ones = jnp.ones((tk, LANE), BF16)
        m_ref[...] = jnp.full_like(m_ref, NEG)
        l_ref[...] = jnp.zeros_like(l_ref)
        acc_ref[...] = jnp.zeros_like(acc_ref)
        row = lax.broadcasted_iota(jnp.int32, (tq, tk), 0)
        col = lax.broadcasted_iota(jnp.int32, (tq, tk), 1)

        def absorb(h, j, base, r0, r1, subs, diagonal):
            zs = []
            for u in subs:
                z = _dot(qh[h][r0:r1], k_ref[pl.ds(base + u * tk, tk), :], "nt")
                if not fold:
                    z = z * scale
                if causal:
                    z = z + bias[h][r0:r1] - cr_ref[h, j * nsub + u]
                if diagonal:
                    z = jnp.where((col[r0:r1] + u * tk) <= row[r0:r1], z, NEG)
                zs.append(z)
            m_prev = m_ref[h, r0:r1, :]
            top = zs[0]
            for z in zs[1:]:
                top = jnp.maximum(top, z)
            m_new = jnp.maximum(m_prev, jnp.max(top, axis=1, keepdims=True))
            alpha = jnp.exp(m_prev - m_new)
            l_new = alpha * l_ref[h, r0:r1, :]
            out = alpha * acc_ref[h, r0:r1, :]
            m_wide = jnp.broadcast_to(m_new, top.shape)
            for u, z in zip(subs, zs):
                p = jnp.exp(z - m_wide).astype(BF16)
                l_new = l_new + jnp.dot(p, ones, preferred_element_type=F32)[:, 0:1]
                out = out + _dot(p, v_ref[pl.ds(base + u * tk, tk), :])
            l_ref[h, r0:r1, :] = l_new
            acc_ref[h, r0:r1, :] = out
            m_ref[h, r0:r1, :] = m_new

        def step(j, diagonal):
            base = _key_base(j, tq)
            for h in range(hpb):
                if diagonal and nsub % 2 == 0:
                    absorb(h, j, base, 0, tq // 2, range(nsub // 2), True)
                    absorb(h, j, base, tq // 2, tq, range(nsub), True)
                else:
                    absorb(h, j, base, 0, tq, range(nsub), diagonal)

        def below(j, carry):
            step(j, False)
            return carry

        if causal:
            lax.fori_loop(0, qi, below, 0)
            step(qi, True)
        else:
            step(0, False)
        o = _join_heads([acc_ref[h] / l_ref[h] for h in range(hpb)], d)
        o_ref[...] = o
        ob_ref[...] = o.astype(ob_ref.dtype)
        lse_ref[...] = m_ref[...] + jnp.log(l_ref[...])

    out = pl.BlockSpec((tq, LANE), lambda g, i: (i, g))
    col = pl.BlockSpec((hpb, tq, 1), lambda g, i: (g, i, 0))
    in_specs = [_lane_tile(tq, q[1], False), _lane_tile(Tk, k[1], True), _lane_tile(Tk, v[1], True)]
    args = [q[0], k[0], v[0]]
    if causal:
        in_specs += [col, pl.BlockSpec((hpb, T // tk, 1, tk), lambda g, i: (g, 0, 0, 0))]
        args += [c.reshape(H, T, 1), c.reshape(H, T // tk, 1, tk)]
    return pl.pallas_call(
        body, name=name, grid=(n_tiles, T // tq),
        in_specs=in_specs, out_specs=[out, out, col],
        out_shape=[jax.ShapeDtypeStruct((T, n_tiles * LANE), F32), jax.ShapeDtypeStruct((T, n_tiles * LANE), BF16),
                   jax.ShapeDtypeStruct((H, T, 1), F32)],
        scratch_shapes=[pltpu.VMEM((hpb, tq, 1), F32), pltpu.VMEM((hpb, tq, 1), F32),
                        pltpu.VMEM((hpb, tq, LANE), F32)],
        compiler_params=_cparams(("parallel", "arbitrary")),
    )(*args)


def _attn_bwd(name, q, k, v, o, do, lse, n_tiles, d, scale, c=None):
    T, Tk = q[0].shape[0], k[0].shape[0]
    hpb = LANE // d
    H = n_tiles * hpb
    causal = c is not None
    tq, tk, nsub = _att_tiles(T, Tk, causal)
    fold = _is_pow2(scale)

    def body(*refs):
        q_ref, k_ref, v_ref, o_ref, do_ref, lse_ref = refs[:6]
        cc_ref, cr_ref = refs[6:8] if causal else (None, None)
        n_out = 5 if causal else 3
        outs = refs[-(n_out + 3):-3]
        dq_ref, dk_ref, dv_ref = outs[:3]
        dc_ref, drow_ref = outs[3:5] if causal else (None, None)
        dk_acc, dv_acc, dq_acc = refs[-3:]
        qi = pl.program_id(1)

        @pl.when(qi == 0)
        def _():
            dk_acc[...] = jnp.zeros_like(dk_acc)
            dv_acc[...] = jnp.zeros_like(dv_acc)
            if causal:
                dc_ref[...] = jnp.zeros_like(dc_ref)

        qh = _per_head(q_ref[...] * scale if fold else q_ref[...], hpb, d)
        doh = _per_head(do_ref[...], hpb, d)
        delta_wide = [jnp.broadcast_to(jnp.sum(t.astype(F32) * o_ref[...], axis=1, keepdims=True), (tq, tk))
                      for t in doh]
        shift = [jnp.broadcast_to((cc_ref[h] - lse_ref[h]) if causal else -lse_ref[h], (tq, tk)) for h in range(hpb)]
        dq_acc[...] = jnp.zeros_like(dq_acc)
        if causal:
            drow_ref[...] = jnp.zeros_like(drow_ref)
        row = lax.broadcasted_iota(jnp.int32, (tq, tk), 0)
        col = lax.broadcasted_iota(jnp.int32, (tq, tk), 1)

        def step(j, diagonal):
            base = _key_base(j, tq)
            for h in range(hpb):
                dq, dsum = None, None
                for u in range(nsub):
                    ks = base + u * tk
                    r0 = u * tk if diagonal else 0
                    kv = k_ref[pl.ds(ks, tk), :]
                    z = _dot(qh[h][r0:], kv, "nt")
                    if not fold:
                        z = z * scale
                    z = z + shift[h][r0:]
                    if causal:
                        z = z - cr_ref[h, j * nsub + u]
                    if diagonal:
                        z = jnp.where((col[r0:] + u * tk) <= row[r0:], z, NEG)
                    p = jnp.exp(z)
                    ds = p * (_dot(doh[h][r0:], v_ref[pl.ds(ks, tk), :], "nt") - delta_wide[h][r0:])
                    term = _dot(ds, kv)
                    dk = _dot(ds, qh[h][r0:], "tn")
                    dk_acc[pl.ds(ks, tk), :] += dk if fold else dk * scale
                    dv_acc[pl.ds(ks, tk), :] += _dot(p, doh[h][r0:], "tn")
                    if causal:
                        dc_ref[h, j * nsub + u] -= jnp.sum(ds, axis=0, keepdims=True)
                    if diagonal:
                        dq_acc[h, r0:, :] += term
                        drow_ref[h, r0:, :] += jnp.sum(ds, axis=1, keepdims=True)
                    else:
                        dq = term if dq is None else dq + term
                        if causal:
                            dsum = ds if dsum is None else dsum + ds
                if not diagonal:
                    dq_acc[h] += dq
                    if causal:
                        drow_ref[h] += jnp.sum(dsum, axis=1, keepdims=True)

        def below(j, carry):
            step(j, False)
            return carry

        if causal:
            lax.fori_loop(0, qi, below, 0)
            step(qi, True)
        else:
            step(0, False)
        dq_ref[...] = (_join_heads([dq_acc[h] for h in range(hpb)], d) * scale).astype(dq_ref.dtype)

        @pl.when(qi == pl.num_programs(1) - 1)
        def _():
            dk_ref[...] = dk_acc[...].astype(dk_ref.dtype)
            dv_ref[...] = dv_acc[...].astype(dv_ref.dtype)

    blk = pl.BlockSpec((tq, LANE), lambda g, i: (i, g))
    full = pl.BlockSpec((Tk, LANE), lambda g, i: (0, g))
    col = pl.BlockSpec((hpb, tq, 1), lambda g, i: (g, i, 0))
    crow = pl.BlockSpec((hpb, T // tk, 1, tk), lambda g, i: (g, 0, 0, 0))
    in_specs = [_lane_tile(tq, q[1], False), _lane_tile(Tk, k[1], True), _lane_tile(Tk, v[1], True), blk, blk, col]
    args = [q[0], k[0], v[0], o, do, lse]
    out_specs = [blk, full, full]
    out_shape = [jax.ShapeDtypeStruct((T, n_tiles * LANE), BF16), jax.ShapeDtypeStruct((Tk, n_tiles * LANE), BF16),
                 jax.ShapeDtypeStruct((Tk, n_tiles * LANE), BF16)]
    if causal:
        in_specs += [col, crow]
        args += [c.reshape(H, T, 1), c.reshape(H, T // tk, 1, tk)]
        out_specs += [crow, col]
        out_shape += [jax.ShapeDtypeStruct((H, T // tk, 1, tk), F32), jax.ShapeDtypeStruct((H, T, 1), F32)]
    outs = pl.pallas_call(
        body, name=name, grid=(n_tiles, T // tq),
        in_specs=in_specs, out_specs=out_specs, out_shape=out_shape,
        scratch_shapes=[pltpu.VMEM((Tk, LANE), F32), pltpu.VMEM((Tk, LANE), F32), pltpu.VMEM((hpb, tq, LANE), F32)],
        compiler_params=_cparams(("parallel", "arbitrary")),
    )(*args)
    if causal:
        return outs[0], outs[1], outs[2], outs[3].reshape(H, T), outs[4].reshape(H, T)
    return outs


def _decay_fwd(fl, b):
    H, T = fl.shape
    tk = DECAY_TK

    def body(x_ref, b_ref, c_ref):
        upto = _tri(tk, lambda j, s: j <= s)
        carry = jnp.zeros((H, 1), F32)
        for i in range(T // tk):
            xv = x_ref[:, i * tk:(i + 1) * tk] + b_ref[...]
            lf = jnp.minimum(xv, 0.0) - jnp.log(1.0 + jnp.exp(-jnp.abs(xv)))
            pref = _dot_split(lf, upto, parts=3) + carry
            c_ref[:, i * tk:(i + 1) * tk] = pref
            carry = pref[:, tk - 1:tk]

    vm = pl.BlockSpec(memory_space=pltpu.VMEM)
    return pl.pallas_call(
        body, name="decay_fwd", in_specs=[vm, vm], out_specs=vm,
        out_shape=jax.ShapeDtypeStruct((H, T), F32),
    )(fl, b)


def _decay_bwd(dc_cols, dc_rows, fl, b):
    H, T = fl.shape
    tk = DECAY_TK

    def body(dc_ref, dr_ref, x_ref, b_ref, dx_ref, db_ref):
        from_ = _tri(tk, lambda j, s: j >= s)
        carry = jnp.zeros((H, 1), F32)
        total = jnp.zeros((H, 1), F32)
        for i in reversed(range(T // tk)):
            sl = slice(i * tk, (i + 1) * tk)
            suffix = _dot_split(dc_ref[:, sl] + dr_ref[:, sl], from_, parts=3) + carry
            xv = x_ref[:, sl] + b_ref[...]
            dx = suffix / (1.0 + jnp.exp(xv))
            dx_ref[:, sl] = dx
            total = total + jnp.sum(dx, axis=1, keepdims=True)
            carry = suffix[:, 0:1]
        db_ref[...] = jnp.broadcast_to(total, db_ref.shape)

    vm = pl.BlockSpec(memory_space=pltpu.VMEM)
    dx, db = pl.pallas_call(
        body, name="decay_bwd", in_specs=[vm, vm, vm, vm], out_specs=[vm, vm],
        out_shape=[jax.ShapeDtypeStruct((H, T), F32), jax.ShapeDtypeStruct((H, LANE), F32)],
    )(dc_cols, dc_rows, fl, b)
    return dx, db[:, 0]


def _place():
    x, y, c = lax.axis_index("x"), lax.axis_index("y"), lax.axis_index("c")
    return x, y, c, [(1 - x, y), (x, 1 - y), (1 - x, 1 - y)]


def _all_gather(name, block):
    R, C = block.shape

    def body(x_ref, out_ref, send_sems, recv_sems, local_sem):
        x, y, c, chips = _place()
        me, sibling = (x, y, c), (x, y, 1 - c)

        def rows(px, py, pc):
            return out_ref.at[4 * px + 2 * py + pc]

        def copy(k, blk, to, src=None):
            return pltpu.make_async_remote_copy(
                src_ref=rows(*blk) if src is None else src, dst_ref=rows(*blk),
                send_sem=send_sems.at[k], recv_sem=recv_sems.at[k], device_id=to, device_id_type=MESH)

        mine = pltpu.make_async_copy(x_ref, rows(*me), local_sem)
        mine.start()
        first = [copy(0, me, sibling, src=x_ref)]
        first += [copy(1 + j, me, (*chip, c), src=x_ref) for j, chip in enumerate(chips)]
        for cp in first:
            cp.start()
        passed = [copy(4 + j, (*chip, c), sibling) for j, chip in enumerate(chips)]
        for j, chip in enumerate(chips):
            copy(1 + j, (*chip, c), me).wait_recv()
            passed[j].start()
        copy(0, sibling, me).wait_recv()
        for j, chip in enumerate(chips):
            copy(4 + j, (*chip, 1 - c), me).wait_recv()
        for cp in first + passed:
            cp.wait_send()
        mine.wait()

    return pl.pallas_call(
        body, name=name, in_specs=[ANY], out_specs=ANY,
        out_shape=jax.ShapeDtypeStruct((N_DEV, R, C), block.dtype),
        scratch_shapes=[pltpu.SemaphoreType.DMA((7,)), pltpu.SemaphoreType.DMA((7,)), pltpu.SemaphoreType.DMA(())],
    )(block)


def _swap_with_sibling(name, parts):
    _, R, C = parts.shape

    def body(p_ref, out_ref, send_sems, recv_sems):
        x, y, c, _ = _place()
        copies = [pltpu.make_async_remote_copy(
            src_ref=p_ref.at[2 * q + (1 - c)], dst_ref=out_ref.at[q],
            send_sem=send_sems.at[q], recv_sem=recv_sems.at[q], device_id=(x, y, 1 - c), device_id_type=MESH)
            for q in range(4)]
        for cp in copies:
            cp.start()
        for cp in copies:
            cp.wait_recv()
        for cp in copies:
            cp.wait_send()

    return pl.pallas_call(
        body, name=name, in_specs=[ANY], out_specs=ANY,
        out_shape=jax.ShapeDtypeStruct((4, R, C), parts.dtype),
        scratch_shapes=[pltpu.SemaphoreType.DMA((4,)), pltpu.SemaphoreType.DMA((4,))],
    )(parts)


def _add_own(name, parts, got, tr=512):
    _, R, C = parts.shape
    tr = _tile(R, tr, SUBLANE_BF16)

    def body(c_ref, p_ref, g_ref, o_ref):
        o_ref[...] = (p_ref[...].astype(F32) + g_ref[...].astype(F32)).astype(o_ref.dtype)

    return pl.pallas_call(
        body, name=name,
        grid_spec=pltpu.PrefetchScalarGridSpec(
            num_scalar_prefetch=1, grid=(4, R // tr),
            in_specs=[pl.BlockSpec((1, tr, C), lambda q, i, c: (2 * q + c[0], i, 0)),
                      pl.BlockSpec((1, tr, C), lambda q, i, c: (q, i, 0))],
            out_specs=pl.BlockSpec((1, tr, C), lambda q, i, c: (q, i, 0))),
        out_shape=jax.ShapeDtypeStruct((4, R, C), parts.dtype),
        compiler_params=_cparams(("parallel", "parallel")),
    )(lax.axis_index("c").astype(jnp.int32).reshape(1), parts, got)


def _swap_with_chips(name, parts):
    _, R, C = parts.shape

    def body(p_ref, out_ref, send_sems, recv_sems, local_sem):
        x, y, c, chips = _place()
        my_chip = 2 * x + y
        mine = pltpu.make_async_copy(p_ref.at[my_chip], out_ref.at[my_chip], local_sem)
        mine.start()
        sends = [pltpu.make_async_remote_copy(
            src_ref=p_ref.at[2 * cx + cy], dst_ref=out_ref.at[my_chip],
            send_sem=send_sems.at[j], recv_sem=recv_sems.at[j], device_id=(cx, cy, c), device_id_type=MESH)
            for j, (cx, cy) in enumerate(chips)]
        for cp in sends:
            cp.start()
        for j, (cx, cy) in enumerate(chips):
            pltpu.make_async_remote_copy(
                src_ref=p_ref.at[my_chip], dst_ref=out_ref.at[2 * cx + cy],
                send_sem=send_sems.at[j], recv_sem=recv_sems.at[j], device_id=(cx, cy, c), device_id_type=MESH,
            ).wait_recv()
        for cp in sends:
            cp.wait_send()
        mine.wait()

    return pl.pallas_call(
        body, name=name, in_specs=[ANY], out_specs=ANY,
        out_shape=jax.ShapeDtypeStruct((4, R, C), parts.dtype),
        scratch_shapes=[pltpu.SemaphoreType.DMA((3,)), pltpu.SemaphoreType.DMA((3,)), pltpu.SemaphoreType.DMA(())],
    )(parts)


def _sum_parts(name, parts, tr=512):
    P, R, C = parts.shape
    tr = _tile(R, tr, SUBLANE_BF16)

    def body(p_ref, o_ref):
        total = p_ref[0].astype(F32)
        for p in range(1, P):
            total = total + p_ref[p].astype(F32)
        o_ref[...] = total

    return pl.pallas_call(
        body, name=name, grid=(R // tr,),
        in_specs=[pl.BlockSpec((P, tr, C), lambda i: (0, i, 0))], out_specs=pl.BlockSpec((tr, C), lambda i: (i, 0)),
        out_shape=jax.ShapeDtypeStruct((R, C), F32),
        compiler_params=_cparams(("parallel",)),
    )(parts)


_HBM = pl.BlockSpec(memory_space=pltpu.HBM)
_SEM = pl.BlockSpec(memory_space=pltpu.SEMAPHORE)
_EFFECT = pltpu.SideEffectType.DATAFLOW_SIDE_EFFECTING


def _flipped(x, y, c, k):
    px, py, pc = (1 - x if k & 4 else x), (1 - y if k & 2 else y), (1 - c if k & 1 else c)
    return (px, py, pc), 4 * px + 2 * py + pc


def _exchange_start(name, src, per_peer):
    R, C = src.shape[-2:]

    def body(v_ref, land_ref, send_sem, recv_sem, v_thru, land_thru, token):
        x, y, c = lax.axis_index("x"), lax.axis_index("y"), lax.axis_index("c")
        me = 4 * x + 2 * y + c
        for k in range(1, N_DEV):
            peer, idx = _flipped(x, y, c, k)
            pltpu.make_async_remote_copy(
                src_ref=v_ref.at[idx] if per_peer else v_ref, dst_ref=land_ref.at[me],
                send_sem=send_sem, recv_sem=recv_sem, device_id=peer, device_id_type=MESH).start()
        token[...] = jnp.zeros_like(token)

    return pl.pallas_call(
        body, name=name,
        out_shape=(pltpu.SemaphoreType.DMA(()), pltpu.SemaphoreType.DMA(()), pltpu.HBM(src.shape, src.dtype),
                   pltpu.HBM((N_DEV, R, C), src.dtype), jax.ShapeDtypeStruct((8, LANE), F32)),
        in_specs=(_HBM, _HBM), out_specs=(_SEM, _SEM, _HBM, _HBM, pl.BlockSpec(memory_space=pltpu.VMEM)),
        input_output_aliases={0: 2, 1: 3},
        compiler_params=pltpu.CompilerParams(has_side_effects=_EFFECT),
    )(pltpu.with_memory_space_constraint(src, pltpu.HBM),
      pltpu.with_memory_space_constraint(lax.empty((N_DEV, R, C), src.dtype), pltpu.HBM))


def _exchange_wait(name, started, after):
    send_sem, recv_sem, v_thru, land_thru, _ = started

    def body(v_ref, land_ref, send_sem, recv_sem, after_ref, v_dead, got_ref):
        x, y, c = lax.axis_index("x"), lax.axis_index("y"), lax.axis_index("c")
        seven = land_ref.at[pl.ds(0, N_DEV - 1)]
        drain = pltpu.make_async_remote_copy(
            src_ref=seven, dst_ref=seven, send_sem=send_sem, recv_sem=recv_sem,
            device_id=(x, y, c), device_id_type=MESH)
        drain.wait_send()
        drain.wait_recv()

    return pl.pallas_call(
        body, name=name,
        out_shape=(pltpu.HBM(v_thru.shape, v_thru.dtype), pltpu.HBM(land_thru.shape, land_thru.dtype)),
        in_specs=(_HBM, _HBM, _SEM, _SEM, ANY), out_specs=(_HBM, _HBM), input_output_aliases={0: 0, 1: 1},
        compiler_params=pltpu.CompilerParams(has_side_effects=_EFFECT),
    )(v_thru, land_thru, send_sem, recv_sem, after)


def _my_index():
    return 4 * lax.axis_index("x") + 2 * lax.axis_index("y") + lax.axis_index("c")


def _sum_landed(name, landed, parts, tr=512):
    P, R, C = landed.shape
    tr = _tile(R, tr, SUBLANE_BF16)

    def body(me_ref, l_ref, own_ref, o_ref):
        total = None
        for s in range(P):
            part = jnp.where(me_ref[0] == s, own_ref[0], l_ref[s]).astype(F32)
            total = part if total is None else total + part
        o_ref[...] = total

    return pl.pallas_call(
        body, name=name,
        grid_spec=pltpu.PrefetchScalarGridSpec(
            num_scalar_prefetch=1, grid=(R // tr,),
            in_specs=[pl.BlockSpec((P, tr, C), lambda i, me: (0, i, 0)),
                      pl.BlockSpec((1, tr, C), lambda i, me: (me[0], i, 0))],
            out_specs=pl.BlockSpec((tr, C), lambda i, me: (i, 0))),
        out_shape=jax.ShapeDtypeStruct((R, C), F32),
        compiler_params=_cparams(("parallel",)),
    )(_my_index().astype(jnp.int32).reshape(1), landed, parts)


def _after(params, name, token):
    return {**params, name: params[name] + token[0, 0]}


def _reduce_scatter(tag, parts):
    got = _swap_with_sibling("rs_pair_" + tag, parts)
    pair = _add_own("rs_add_" + tag, parts, got)
    quad = _swap_with_chips("rs_chips_" + tag, pair)
    return _sum_parts("rs_sum_" + tag, quad)


def _adamw(name, g_parts, w, m, v, tr=512):
    P, R, C = g_parts.shape
    tr = _tile(R, tr, 8)

    def body(g_ref, w_ref, m_ref, v_ref, go_ref, d_ref, mo_ref, vo_ref):
        g = g_ref[0]
        for p in range(1, P):
            g = g + g_ref[p]
        mn = ADAM_B1 * m_ref[...] + (1.0 - ADAM_B1) * g
        vn = ADAM_B2 * v_ref[...] + (1.0 - ADAM_B2) * (g * g)
        m_hat = mn / (1.0 - ADAM_B1 ** ADAM_STEP)
        v_hat = vn / (1.0 - ADAM_B2 ** ADAM_STEP)
        go_ref[...] = g
        d_ref[...] = -ADAM_LR * (m_hat / (jnp.sqrt(v_hat) + ADAM_EPS) + ADAM_WD * w_ref[...])
        mo_ref[...] = mn
        vo_ref[...] = vn

    row = pl.BlockSpec((tr, C), lambda i: (i, 0))
    return pl.pallas_call(
        body, name=name, grid=(R // tr,),
        in_specs=[pl.BlockSpec((P, tr, C), lambda i: (0, i, 0)), row, row, row], out_specs=[row] * 4,
        out_shape=[jax.ShapeDtypeStruct((R, C), F32)] * 4,
        compiler_params=_cparams(("parallel",)),
    )(g_parts, w, m, v)


def _pad_rows(t, rows):
    return jnp.pad(t, ((0, rows - t.shape[0]), (0, 0)))


class _Layout:
    def __init__(self, D, ff_shard, in_shard, kv_shard, gate_shard, br_in, br_shard, out_shard):
        self.D = D
        self.in_shard = in_shard
        self.in_pad = -(-in_shard // LANE) * LANE
        self.in_cols = -(-N_DEV * in_shard // IN_TILE) * IN_TILE
        self.br_in, self.br_shard = br_in, br_shard
        br_rows = br_shard * br_in // D
        sizes = [("g1", ff_shard), ("u1", ff_shard), ("d1", ff_shard), ("win", self.in_pad), ("kv", kv_shard),
                 ("gate", gate_shard), ("br", br_rows), ("out", out_shard),
                 ("g2", ff_shard), ("u2", ff_shard), ("d2", ff_shard)]
        self.seg, off = {}, 0
        for key, n in sizes:
            assert n % SUBLANE_BF16 == 0, (key, n)
            self.seg[key] = (off, n)
            off += n
        self.rows = off

    def pack(self, parts):
        return jnp.concatenate([parts[key] for key in self.seg], axis=0)

    def take(self, gathered, key, own=None):
        off, n = self.seg[key]
        seg = gathered[:, off:off + n, :]
        if own is not None:
            seg = lax.dynamic_update_slice(seg, own[0][off:off + n][None], (own[1], 0, 0))
        return seg.reshape(N_DEV * n, self.D)

    def spread(self, full, key):
        _, n = self.seg[key]
        return full.reshape(N_DEV, n, self.D)


def _pack_layer(lay, l, p):
    D = lay.D
    br = jnp.concatenate([p["w_br_sb"][l], p["w_br_fox"][l], p["w_br_mem"][l]], axis=0)
    parts = {
        "g1": p["ffn1_w_gate"][l].T, "u1": p["ffn1_w_up"][l].T, "d1": p["ffn1_w_down"][l],
        "win": _pad_rows(p["w_in"][l].T, lay.in_pad), "kv": p["w_mem_kv"][l], "gate": p["w_gate"][l].T,
        "br": br.T.reshape(-1, D), "out": p["w_out"][l],
        "g2": p["ffn2_w_gate"][l].T, "u2": p["ffn2_w_up"][l].T, "d2": p["ffn2_w_down"][l],
    }
    return lay.pack({k: t.astype(BF16) for k, t in parts.items()})


def _align_win(lay, packed):
    D = lay.D
    real = packed.reshape(N_DEV, lay.in_pad, D)[:, :lay.in_shard].reshape(N_DEV * lay.in_shard, D)
    rows = jnp.concatenate([real[:_QKV_W], real[_QKV_W + N_FOX_HEADS:], real[_QKV_W:_QKV_W + N_FOX_HEADS]], axis=0)
    return _pad_rows(rows, lay.in_cols)


def _unalign_win(lay, aligned):
    D = lay.D
    n_real = N_DEV * lay.in_shard
    mem_w = n_real - _QKV_W - N_FOX_HEADS
    real = jnp.concatenate([aligned[:_QKV_W], aligned[_QKV_W + mem_w:n_real], aligned[_QKV_W:_QKV_W + mem_w]], axis=0)
    real = real.reshape(N_DEV, lay.in_shard, D)
    return jnp.pad(real, ((0, 0), (0, lay.in_pad - lay.in_shard), (0, 0))).reshape(N_DEV * lay.in_pad, D)


def _unpack_layer(lay, gathered, own=None):
    D = lay.D
    w = {k: lay.take(gathered, k, own) for k in ("g1", "u1", "d1", "kv", "out", "g2", "u2", "d2")}
    w["win"] = _align_win(lay, lay.take(gathered, "win", own))
    fl0 = N_DEV * lay.in_shard - N_FOX_HEADS
    w["wfl"] = w["win"][fl0:fl0 + LANE]
    gate = lay.take(gathered, "gate", own)
    w["gate"] = gate
    w["gate3"] = [gate[i * D:(i + 1) * D] for i in range(3)]
    br = lay.take(gathered, "br", own).reshape(N_DEV * lay.br_shard, lay.br_in)
    third = lay.br_in // 3
    w["br3"] = [br[:, i * third:(i + 1) * third] for i in range(3)]
    return w


def _silu_mul(accs, _):
    a, b = accs
    return [a, b, a * jax.nn.sigmoid(a) * b]


def _act_bwd(accs, extras):
    ds, (a, b) = accs[0], [e.astype(F32) for e in extras]
    sig = jax.nn.sigmoid(a)
    return [ds * b * (sig * (1.0 + a * (1.0 - sig))), ds * (a * sig)]


def _res_norm(scale):
    def epilogue(accs, extras):
        f, res, g = accs[0], extras[0], extras[1]
        out = res + scale * ((f * _rstd(f)) * g)
        return [f, out] + [(out * _rstd(out)) * g_next for g_next in extras[2:]]
    return epilogue


def _down_proj(name, x, w, res, g, scale, next_g):
    D = res.shape[1]
    extras = [(res, 0), (g.reshape(1, D), 0)] + ([(next_g.reshape(1, D), 0)] if next_g is not None else [])
    outs = _mm(name, [(x, w)], "nn", [F32, F32] + [BF16] * (len(extras) - 2), _res_norm(scale), extras, tn=D)
    return outs[0], outs[1], (outs[2] if next_g is not None else None)


def _rms_grad(x, g, dy):
    r = _rstd(x)
    xhat = x * r
    gy = dy * g
    dx = r * (gy - xhat * jnp.mean(gy * xhat, axis=-1, keepdims=True))
    part = jnp.sum(dy * xhat, axis=0, keepdims=True)
    first = lax.broadcasted_iota(jnp.int32, (8, part.shape[1]), 0) == 0
    return dx, jnp.where(first, part, 0.0)


def _norm_bwd(next_scale):
    def epilogue(accs, extras):
        dx, part = _rms_grad(extras[0], extras[2], _sum_accs(accs, None)[0])
        dh = extras[1] + dx
        if len(extras) == 3:
            return [dh, part]
        return [dh, part, *_rms_grad(extras[3], extras[4], next_scale * dh)]
    return epilogue


def _up_grad(name, pairs, x, res, g, nxt, tm=256):
    D = x.shape[1]
    extras = [(x, 0), (res, 0), (g.reshape(1, D), 0)]
    dtypes, rows = [F32, F32], [None, 8]
    if nxt is not None:
        extras += [(nxt[0], 0), (nxt[1].reshape(1, D), 0)]
        dtypes, rows = dtypes + [BF16, F32], rows + [None, 8]
    outs = _mm(name, pairs, "nn", dtypes, _norm_bwd(nxt[2] if nxt is not None else None), extras, tm=tm, tn=D,
               out_rows=rows)
    dg = _colsum(name + "_dg", outs[1])
    if nxt is None:
        return outs[0], dg, None, None
    return outs[0], dg, outs[2], _colsum(name + "_dg2", outs[3])


def _ffn_fwd(tag, h, n, post_g, wg, wu, wd, next_g):
    a, b, s = _mm("ffn_up_" + tag, [(n, wg), (n, wu)], "nt", [BF16, BF16, BF16], _silu_mul, tn=1408)
    f, out, n_next = _down_proj("ffn_down_" + tag, s, wd, h, post_g, 0.5, next_g)
    return out, n_next, (h, n, a, b, s, f)


def _ffn_bwd(tag, dh, df, saved, pre_g, wg, wu, wd, nxt):
    h, n, a, b, s, f = saved
    da, db = _mm("ffn_dact_" + tag, [(df, wd)], "nt", [BF16, BF16], _act_bwd, [(a, 0), (b, 0)], tn=1408)
    d_wd = _mm("ffn_dwd_" + tag, [(s, df)], "tn", [BF16], tm=256)
    dh_in, d_pre, d_next, dg_next = _up_grad("ffn_dn_" + tag, [(da, wg), (db, wu)], h, dh, pre_g, nxt)
    d_wg = _mm("ffn_dwg_" + tag, [(da, n)], "tn", [BF16], tm=256)
    d_wu = _mm("ffn_dwu_" + tag, [(db, n)], "tn", [BF16], tm=256)
    return dh_in, d_pre, d_next, dg_next, d_wg, d_wu, d_wd


_SB_W = N_SB_HEADS * HEAD_DIM
_FOX_W = N_FOX_HEADS * HEAD_DIM
_QKV_W = 3 * _SB_W + 3 * _FOX_W


def _gate_act(accs, extras):
    return [jax.nn.sigmoid(accs[0] + extras[0])]


def _merge(accs, extras):
    g = [e.astype(F32) for e in extras]
    return [g[0] * accs[0] + g[1] * accs[1] + g[2] * accs[2]]


def _merge_bwd(accs, extras):
    dm = accs[0]
    g = [e.astype(F32) for e in extras]
    d_branch = [dm * gi for gi in g]
    d_gate = [dm * bi * gi * (1.0 - gi) for bi, gi in zip(accs[1:], g)]
    return d_branch + d_gate


def _mix_tiles(lay):
    sb, fx = _SB_W // LANE, _FOX_W // LANE
    mem_w = N_DEV * lay.in_shard - _QKV_W - N_FOX_HEADS
    return (0, sb, 2 * sb, sb), (3 * sb, 3 * sb + fx, 3 * sb + 2 * fx, fx), (_QKV_W // LANE, mem_w // LANE)


def _mix_fwd(lay, h, u, w, post_g, b_forget, b_gate, mem_n, next_g):
    D = lay.D
    (sq, sk, sv, sn), (fq, fk, fv, fn), (mq, mn) = _mix_tiles(lay)
    mem_d = mn * LANE // N_MEM_HEADS
    proj = _mm("mix_in", [(u, w["win"])], "nt", [BF16], tm=1024, tn=IN_TILE)
    fl = _mm("mix_fl", [(u, w["wfl"])], "nt", [F32])[:, :N_FOX_HEADS].T
    c = _decay_fwd(fl, b_forget.reshape(-1, 1))
    o_sb, rtot = _sb_fwd((proj, sq), (proj, sk), (proj, sv), sn, HEAD_DIM, HEAD_DIM ** -0.5)
    o_fx32, o_fx, lse_fx = _attn_fwd("fox_fwd", (proj, fq), (proj, fk), (proj, fv), fn, HEAD_DIM,
                                     HEAD_DIM ** -0.5, c)
    kvm = _mm("mem_kv", [(mem_n, w["kv"])], "nn", [BF16])
    o_mem32, o_mem, lse_mem = _attn_fwd("mem_fwd", (proj, mq), (kvm, 0), (kvm, mn), mn, mem_d, mem_d ** -0.5)
    gates = _mm("mix_gate", [(u, w["gate"])], "nt", [BF16], _gate_act, [(b_gate.reshape(1, -1), 0)], tm=1024)
    flat = [o_sb, o_fx, o_mem]
    merged = _mm("mix_merge", list(zip(flat, w["br3"])), "nt", [BF16], _merge,
                 [(gates, 0), (gates, D), (gates, 2 * D)])
    z, out, n_next = _down_proj("mix_out", merged, w["out"], h, post_g, 1.0, next_g)
    saved = (h, u, proj, fl, c, rtot, o_fx32, lse_fx, kvm, o_mem32, lse_mem, gates, flat, merged, z)
    return out, n_next, saved


def _mix_bwd(lay, dh, dz, saved, w, pre_g, b_forget, mem_n, dmem_n, nxt):
    D = lay.D
    (sq, sk, sv, sn), (fq, fk, fv, fn), (mq, mn) = _mix_tiles(lay)
    mem_d = mn * LANE // N_MEM_HEADS
    h, u, proj, fl, c, rtot, o_fx32, lse_fx, kvm, o_mem32, lse_mem, gates, flat, merged, z = saved
    outs = _mm("mix_dmerge", [(dz, w["out"])] + list(zip(flat, w["br3"])), "nt", [BF16] * 6, _merge_bwd,
               [(gates, 0), (gates, D), (gates, 2 * D)], tn=512)
    d_branch, d_gate = outs[:3], outs[3:]
    d_wout = _mm("mix_dwout", [(merged, dz)], "tn", [BF16])
    d_o = [_mm("mix_dbr%d" % i, [(d_branch[i], w["br3"][i])], "nn", [BF16]) for i in range(3)]
    d_wbr = [_mm("mix_dwbr%d" % i, [(d_branch[i], flat[i])], "tn", [BF16]) for i in range(3)]
    d_bgate = jnp.concatenate([_colsum("mix_dbgate%d" % i, d_gate[i]) for i in range(3)])
    d_wgate = [_mm("mix_dwgate%d" % i, [(d_gate[i], u)], "tn", [BF16]) for i in range(3)]

    d_sb = _sb_bwd((proj, sq), (proj, sk), (proj, sv), d_o[0], rtot, sn, HEAD_DIM, HEAD_DIM ** -0.5)
    *d_fx, dc, dc_rows = _attn_bwd("fox_bwd", (proj, fq), (proj, fk), (proj, fv), o_fx32, d_o[1], lse_fx, fn,
                                   HEAD_DIM, HEAD_DIM ** -0.5, c)
    dq_m, dk_m, dv_m = _attn_bwd("mem_bwd", (proj, mq), (kvm, 0), (kvm, mn), o_mem32, d_o[2], lse_mem, mn,
                                 mem_d, mem_d ** -0.5)
    dfl, d_bforget = _decay_bwd(dc, dc_rows, fl, b_forget.reshape(-1, 1))
    pieces = list(d_sb) + list(d_fx) + [dq_m]
    dflp = jnp.pad(dfl.T.astype(BF16), ((0, 0), (0, LANE - dfl.shape[0])))
    offs = [sum(t.shape[1] for t in pieces[:i]) for i in range(len(pieces) + 1)]
    win_rows = [w["win"][offs[i]:offs[i + 1]] for i in range(len(pieces))]
    dh_in, d_pre, d_next, dg_next = _up_grad(
        "mix_du", list(zip(d_gate, w["gate3"])) + list(zip(pieces, win_rows)) + [(dflp, w["wfl"])], h, dh, pre_g, nxt)
    d_rows = [_mm("mix_dwin%d" % i, [(t, u)], "tn", [BF16]) for i, t in enumerate(pieces)]
    d_wfl = _mm("mix_dwfl", [(dflp, u)], "tn", [BF16])
    d_win = _unalign_win(lay, _pad_rows(jnp.concatenate(list(d_rows) + [d_wfl], axis=0), lay.in_cols))

    dkvm = jnp.concatenate([dk_m, dv_m], axis=1)
    d_wkv = _mm("mem_dwkv", [(mem_n, dkvm)], "tn", [BF16])
    dmem_n = _mm("mem_dn", [(dkvm, w["kv"])], "nt", [F32], lambda accs, ex: [accs[0] + ex[0]], [(dmem_n, 0)])
    grads = {"win": d_win, "kv": d_wkv, "gate": jnp.concatenate(d_wgate, axis=0),
             "br": jnp.concatenate(d_wbr, axis=1), "out": d_wout}
    return dh_in, d_pre, d_next, dg_next, d_bforget, d_bgate, grads, dmem_n


def _layer_fwd(lay, h, n, w, sp, mem_n, next_g):
    h1, u, s1 = _ffn_fwd("1", h, n, sp["ffn1_post_g"], w["g1"], w["u1"], w["d1"], sp["mix_pre_g"])
    h2, n2, s2 = _mix_fwd(lay, h1, u, w, sp["mix_post_g"], sp["b_forget"], sp["b_gate"], mem_n, sp["ffn2_pre_g"])
    h3, n_next, s3 = _ffn_fwd("2", h2, n2, sp["ffn2_post_g"], w["g2"], w["u2"], w["d2"], next_g)
    return h3, n_next, (s1, s2, s3)


_FFN1_SEGS = 3


def _layer_bwd(lay, dh, df2, d_post2, saved, w, sp, mem_n, dmem_n, below, early=None):
    s1, s2, s3 = saved
    dh, d_pre2, dz, d_mpost, d_g2, d_u2, d_d2 = _ffn_bwd(
        "2", dh, df2, s3, sp["ffn2_pre_g"], w["g2"], w["u2"], w["d2"], (s2[-1], sp["mix_post_g"], 1.0))
    dh, d_mpre, df1, d_post1, d_bforget, d_bgate, g, dmem_n = _mix_bwd(
        lay, dh, dz, s2, w, sp["mix_pre_g"], sp["b_forget"], mem_n, dmem_n, (s1[5], sp["ffn1_post_g"], 0.5))
    g.update({"g2": d_g2, "u2": d_u2, "d2": d_d2})
    g["br"] = g["br"].reshape(N_DEV, lay.br_shard, lay.br_in).reshape(-1, lay.D)
    keys = list(lay.seg)
    if early is not None:
        token = early(jnp.concatenate([lay.spread(g[key], key) for key in keys[_FFN1_SEGS:]], axis=1))
        sp, keys = _after(sp, "ffn1_pre_g", token), keys[:_FFN1_SEGS]
    dh, d_pre1, df_below, d_post_below, d_g1, d_u1, d_d1 = _ffn_bwd(
        "1", dh, df1, s1, sp["ffn1_pre_g"], w["g1"], w["u1"], w["d1"], below)
    g.update({"g1": d_g1, "u1": d_u1, "d1": d_d1})
    packed = jnp.concatenate([lay.spread(g[key], key) for key in keys], axis=1)
    small = {"ffn1_pre_g": d_pre1, "ffn1_post_g": d_post1, "mix_pre_g": d_mpre, "mix_post_g": d_mpost,
             "ffn2_pre_g": d_pre2, "ffn2_post_g": d_post2, "b_gate": d_bgate, "b_forget": d_bforget}
    return dh, df_below, d_post_below, packed, small, dmem_n


_SHARDED = ["ffn1_w_gate", "ffn1_w_up", "ffn1_w_down", "w_in", "w_mem_kv", "w_gate", "w_br_sb", "w_br_fox",
            "w_br_mem", "w_out", "ffn2_w_gate", "ffn2_w_up", "ffn2_w_down"]
_SMALL_LAYER = ["ffn1_pre_g", "ffn1_post_g", "mix_pre_g", "mix_post_g", "ffn2_pre_g", "ffn2_post_g", "b_gate",
                "b_forget"]
_WEIGHTS = ["ffn1_pre_g", "ffn1_post_g", "ffn1_w_gate", "ffn1_w_up", "ffn1_w_down", "mix_pre_g", "mix_post_g",
            "w_in", "b_forget", "mem_norm_g", "w_mem_kv", "w_gate", "b_gate", "w_br_sb", "w_br_fox", "w_br_mem",
            "w_out", "ffn2_pre_g", "ffn2_post_g", "ffn2_w_gate", "ffn2_w_up", "ffn2_w_down"]


def _pack_small(vals, L, D):
    rows = []
    for l in range(L):
        for name in _SMALL_LAYER:
            t = vals[name][l]
            rows.append(jnp.pad(t, (0, -t.shape[0] % D)).reshape(-1, D))
    rows.append(vals["mem_norm_g"].reshape(1, D))
    packed = jnp.concatenate(rows, axis=0)
    return _pad_rows(packed, -(-packed.shape[0] // 8) * 8)


def _unpack_small(packed, shapes, L, D):
    out = {name: [] for name in _SMALL_LAYER}
    r = 0
    for l in range(L):
        for name in _SMALL_LAYER:
            n = shapes[name][1]
            nr = -(-n // D)
            out[name].append(packed[r:r + nr].reshape(-1)[:n])
            r += nr
    res = {name: jnp.stack(v) for name, v in out.items()}
    res["mem_norm_g"] = packed[r]
    return res


def _unpack_grads(lay, g):
    def seg(key):
        off, n = lay.seg[key]
        return g[off:off + n]
    br = seg("br").reshape(lay.br_shard, lay.br_in).T
    third = lay.br_in // 3
    return {
        "ffn1_w_gate": seg("g1").T, "ffn1_w_up": seg("u1").T, "ffn1_w_down": seg("d1"),
        "w_in": seg("win")[:lay.in_shard].T, "w_mem_kv": seg("kv"), "w_gate": seg("gate").T,
        "w_br_sb": br[:third], "w_br_fox": br[third:2 * third], "w_br_mem": br[2 * third:],
        "w_out": seg("out"), "ffn2_w_gate": seg("g2").T, "ffn2_w_up": seg("u2").T, "ffn2_w_down": seg("d2"),
    }


class _Exchanges:
    def gather(self, name, block):
        return _all_gather(name, block)

    def gather_start(self, block):
        return _exchange_start("ag_start", block, per_peer=False)

    def gather_wait(self, started, after):
        block, landed = _exchange_wait("ag_wait", started, after)
        return landed, (block, _my_index())

    def scatter(self, parts):
        return _reduce_scatter("w", parts)

    def scatter_start(self, parts):
        return _exchange_start("rs_start", parts, per_peer=True)

    def scatter_wait(self, started, after):
        parts, landed = _exchange_wait("rs_wait", started, after)
        return _sum_landed("rs_sum8", landed, parts)

    def token(self, started):
        return started[4]

    def loss_sum(self, part):
        return lax.psum(part, ("x", "y", "c"))


def _step(p, m, v, x, mem, tgt, ex):
    L, D = p["ffn1_pre_g"].shape
    lay = _Layout(D, p["ffn1_w_gate"].shape[2], p["w_in"].shape[2], p["w_mem_kv"].shape[1], p["w_gate"].shape[2],
                  3 * p["w_br_sb"].shape[1], p["w_br_sb"].shape[2], p["w_out"].shape[1])
    blocks = [_pack_layer(lay, l, p) for l in range(L)]
    sps = [{name: p[name][l] for name in _SMALL_LAYER} for l in range(L)]

    mem_n = _rms_fwd("mem_norm", mem, p["mem_norm_g"], BF16)
    gathered, own = ex.gather("ag_weights", blocks[0]), None
    h, saved, ws = x, [], []
    n = _rms_fwd("first_norm", x, sps[0]["ffn1_pre_g"], BF16)
    for l in range(L):
        if l + 1 < L:
            nxt, gathered = lax.optimization_barrier((blocks[l + 1], gathered))
            started = ex.gather_start(nxt)
            sp = _after(sps[l], "ffn1_post_g", ex.token(started))
        else:
            sp = sps[l]
        ws.append(_unpack_layer(lay, gathered, own))
        h, n, s = _layer_fwd(lay, h, n, ws[l], sp, mem_n, sps[l + 1]["ffn1_pre_g"] if l + 1 < L else None)
        saved.append(s)
        if l + 1 < L:
            gathered, own = ex.gather_wait(started, h)
    loss_part, dh = _loss_grad(h, tgt)
    loss = ex.loss_sum(loss_part)

    dmem_n = jnp.zeros(mem.shape, F32)
    big, small = [None] * L, {name: [None] * L for name in _SMALL_LAYER}
    flying, token = {}, None
    df, d_post = _rms_bwd("last_dout", saved[L - 1][2][5], sps[L - 1]["ffn2_post_g"], dh, BF16, scale=0.5)
    for l in reversed(range(L)):
        sp = sps[l] if token is None else _after(sps[l], "ffn2_pre_g", token)
        below = (saved[l - 1][2][5], sps[l - 1]["ffn2_post_g"], 0.5) if l > 0 else None
        first = []

        def early(part):
            first.append(ex.scatter_start(part))
            return ex.token(first[0])

        dh, df, d_post, packed, sm, dmem_n = _layer_bwd(lay, dh, df, d_post, saved[l], ws[l], sp, mem_n, dmem_n,
                                                         below, early if l == 0 else None)
        if l > 0:
            flying[l] = ex.scatter_start(packed)
            token = ex.token(flying[l])
        else:
            late = ex.scatter(packed)
            big[l] = _unpack_grads(lay, jnp.concatenate([late, ex.scatter_wait(first[0], late)], axis=0))
        for name in _SMALL_LAYER:
            small[name][l] = sm[name]
    for l, started in flying.items():
        big[l] = _unpack_grads(lay, ex.scatter_wait(started, dh))
    _, d_memg = _rms_bwd("mem_dnorm", mem, p["mem_norm_g"], dmem_n, F32)

    small_g = {name: jnp.stack(vs) for name, vs in small.items()}
    small_g["mem_norm_g"] = d_memg
    small_names = _SMALL_LAYER + ["mem_norm_g"]
    shapes = {name: p[name].shape for name in small_names}
    g_all = ex.gather("ag_small", _pack_small(small_g, L, D))
    packs = [_pack_small({name: t[name] for name in small_names}, L, D) for t in (p, m, v)]
    res = [_unpack_small(t, shapes, L, D) for t in _adamw("adamw_small", g_all, *packs)]

    out = {kind: {} for kind in ("grad", "delta", "new_m", "new_v")}
    for name in small_names:
        for kind, r in zip(("grad", "delta", "new_m", "new_v"), res):
            out[kind][name] = r[name].reshape(p[name].shape)
    for name in _SHARDED:
        g = jnp.stack([big[l][name] for l in range(L)])
        shp = g.shape
        flat = lambda t: t.reshape(-1, shp[-1])
        r = _adamw("adamw_" + name, flat(g)[None], flat(p[name]), flat(m[name]), flat(v[name]))
        for kind, t in zip(("grad", "delta", "new_m", "new_v"), r):
            out[kind][name] = t.reshape(shp)
    return loss, dh, out


def kernel(x, mem, ffn1_pre_g, ffn1_post_g, ffn1_w_gate, ffn1_w_up, ffn1_w_down, mix_pre_g, mix_post_g, w_in, b_forget, mem_norm_g, w_mem_kv, w_gate, b_gate, w_br_sb, w_br_fox, w_br_mem, w_out, ffn2_pre_g, ffn2_post_g, ffn2_w_gate, ffn2_w_up, ffn2_w_down, loss_target, m_ffn1_pre_g, m_ffn1_post_g, m_ffn1_w_gate, m_ffn1_w_up, m_ffn1_w_down, m_mix_pre_g, m_mix_post_g, m_w_in, m_b_forget, m_mem_norm_g, m_w_mem_kv, m_w_gate, m_b_gate, m_w_br_sb, m_w_br_fox, m_w_br_mem, m_w_out, m_ffn2_pre_g, m_ffn2_post_g, m_ffn2_w_gate, m_ffn2_w_up, m_ffn2_w_down, v_ffn1_pre_g, v_ffn1_post_g, v_ffn1_w_gate, v_ffn1_w_up, v_ffn1_w_down, v_mix_pre_g, v_mix_post_g, v_w_in, v_b_forget, v_mem_norm_g, v_w_mem_kv, v_w_gate, v_b_gate, v_w_br_sb, v_w_br_fox, v_w_br_mem, v_w_out, v_ffn2_pre_g, v_ffn2_post_g, v_ffn2_w_gate, v_ffn2_w_up, v_ffn2_w_down):
    p = dict(zip(_WEIGHTS, (ffn1_pre_g, ffn1_post_g, ffn1_w_gate, ffn1_w_up, ffn1_w_down, mix_pre_g, mix_post_g, w_in, b_forget, mem_norm_g, w_mem_kv, w_gate, b_gate, w_br_sb, w_br_fox, w_br_mem, w_out, ffn2_pre_g, ffn2_post_g, ffn2_w_gate, ffn2_w_up, ffn2_w_down)))
    m = dict(zip(_WEIGHTS, (m_ffn1_pre_g, m_ffn1_post_g, m_ffn1_w_gate, m_ffn1_w_up, m_ffn1_w_down, m_mix_pre_g, m_mix_post_g, m_w_in, m_b_forget, m_mem_norm_g, m_w_mem_kv, m_w_gate, m_b_gate, m_w_br_sb, m_w_br_fox, m_w_br_mem, m_w_out, m_ffn2_pre_g, m_ffn2_post_g, m_ffn2_w_gate, m_ffn2_w_up, m_ffn2_w_down)))
    v = dict(zip(_WEIGHTS, (v_ffn1_pre_g, v_ffn1_post_g, v_ffn1_w_gate, v_ffn1_w_up, v_ffn1_w_down, v_mix_pre_g, v_mix_post_g, v_w_in, v_b_forget, v_mem_norm_g, v_w_mem_kv, v_w_gate, v_b_gate, v_w_br_sb, v_w_br_fox, v_w_br_mem, v_w_out, v_ffn2_pre_g, v_ffn2_post_g, v_ffn2_w_gate, v_ffn2_w_up, v_ffn2_w_down)))
    loss, dx, out = _step(p, m, v, x[0], mem[0], loss_target[0], _Exchanges())
    return (loss, dx[None], *[out["grad"][n] for n in _WEIGHTS], *[out["delta"][n] for n in _WEIGHTS],
            *[out["new_m"][n] for n in _WEIGHTS], *[out["new_v"][n] for n in _WEIGHTS])
```

```python
import functools
import math

import jax
import jax.numpy as jnp
from jax import lax
from jax.experimental import pallas as pl
from jax.experimental.pallas import tpu as pltpu

F32 = jnp.float32
BF16 = jnp.bfloat16

LANE = 128
SUBLANE_BF16 = 16
VMEM_LIMIT = 56 * 1024 * 1024
N_DEV = 8
MESH = pl.DeviceIdType.MESH
ANY = pl.BlockSpec(memory_space=pl.ANY)

RMS_EPS = 1e-6
HEAD_DIM = 64
N_SB_HEADS = 8
N_FOX_HEADS = 8
N_MEM_HEADS = 4
NEG = -1e30
ATT_TQ = 1024
ATT_TK = 256
DECAY_TK = 128
IN_TILE = 1280

ADAM_LR = 0.001
ADAM_B1 = 0.9
ADAM_B2 = 0.999
ADAM_EPS = 1e-08
ADAM_WD = 0.01
ADAM_STEP = 10


def _tile(n, target, mult=LANE):
    best = None
    for t in range(mult, min(n, target) + 1, mult):
        if n % t == 0:
            best = t
    return best if best is not None else n


def _cparams(sem):
    return pltpu.CompilerParams(dimension_semantics=sem, vmem_limit_bytes=VMEM_LIMIT)


_DIMS = {"nn": (((1,), (0,)), ((), ())), "nt": (((1,), (1,)), ((), ())), "tn": (((0,), (0,)), ((), ()))}


def _dot(a, b, mode="nn"):
    return lax.dot_general(a.astype(BF16), b.astype(BF16), _DIMS[mode], preferred_element_type=F32)


def _mm(name, pairs, mode, out_dtypes, epilogue=None, extras=(), tm=512, tn=1024, out_rows=None):
    a0, b0 = pairs[0]
    M = a0.shape[1] if mode == "tn" else a0.shape[0]
    N = b0.shape[0] if mode == "nt" else b0.shape[1]
    tm = _tile(M, tm)
    tn = _tile(N, tn)
    np_, ne, no = len(pairs), len(extras), len(out_dtypes)

    def body(*refs):
        a_refs, b_refs = refs[:np_], refs[np_:2 * np_]
        e_refs = refs[2 * np_:2 * np_ + ne]
        o_refs = refs[2 * np_ + ne:]
        accs = [_dot(a[...], b[...], mode) for a, b in zip(a_refs, b_refs)]
        outs = epilogue(accs, [e[...] for e in e_refs]) if epilogue is not None else accs
        for o, val in zip(o_refs, outs):
            o[...] = val.astype(o.dtype)

    in_specs = []
    for a, _ in pairs:
        if mode == "tn":
            in_specs.append(pl.BlockSpec((a.shape[0], tm), lambda j, i: (0, i)))
        else:
            in_specs.append(pl.BlockSpec((tm, a.shape[1]), lambda j, i: (i, 0)))
    for _, b in pairs:
        if mode == "nt":
            in_specs.append(pl.BlockSpec((tn, b.shape[1]), lambda j, i: (j, 0)))
        else:
            in_specs.append(pl.BlockSpec((b.shape[0], tn), lambda j, i: (0, j)))
    for e, off in extras:
        if e.shape[0] == 1:
            in_specs.append(pl.BlockSpec((1, tn), functools.partial(lambda j, i, o: (0, j + o), o=off // tn)))
        else:
            in_specs.append(pl.BlockSpec((tm, tn), functools.partial(lambda j, i, o: (i, j + o), o=off // tn)))
    rows = [tm if r is None else r for r in (out_rows or [None] * no)]
    out_specs = [pl.BlockSpec((r, tn), lambda j, i: (i, j)) for r in rows]
    outs = pl.pallas_call(
        body, name=name, grid=(N // tn, M // tm),
        in_specs=in_specs, out_specs=out_specs,
        out_shape=[jax.ShapeDtypeStruct((M // tm * r, N), dt) for r, dt in zip(rows, out_dtypes)],
        compiler_params=_cparams(("parallel", "parallel")),
    )(*[a for a, _ in pairs], *[b for _, b in pairs], *[e for e, _ in extras])
    return outs[0] if no == 1 else outs


def _sum_accs(accs, _):
    total = accs[0]
    for acc in accs[1:]:
        total = total + acc
    return [total]


def _rstd(x):
    return lax.rsqrt(jnp.mean(x * x, axis=-1, keepdims=True) + RMS_EPS)


def _rms_fwd(name, x, g, out_dtype, tr=512):
    R, D = x.shape
    tr = _tile(R, tr, 8)

    def body(x_ref, g_ref, o_ref):
        xv = x_ref[...]
        o_ref[...] = ((xv * _rstd(xv)) * g_ref[...]).astype(o_ref.dtype)

    row = pl.BlockSpec((tr, D), lambda i: (i, 0))
    return pl.pallas_call(
        body, name=name, grid=(R // tr,),
        in_specs=[row, pl.BlockSpec((1, D), lambda i: (0, 0))], out_specs=row,
        out_shape=jax.ShapeDtypeStruct((R, D), out_dtype),
        compiler_params=_cparams(("parallel",)),
    )(x, g.reshape(1, D))


def _rms_bwd(name, x, g, dy, out_dtype, scale=1.0, res=None, tr=512):
    R, D = x.shape
    tr = _tile(R, tr, 8)
    has_res = res is not None

    def body(*refs):
        x_ref, g_ref, dy_ref = refs[:3]
        dx_ref, dg_ref = refs[-2:]
        i = pl.program_id(0)
        xv = x_ref[...]
        xhat = xv * _rstd(xv)
        dyv = dy_ref[...].astype(F32) * scale
        gy = dyv * g_ref[...]
        dx = _rstd(xv) * (gy - xhat * jnp.mean(gy * xhat, axis=-1, keepdims=True))
        if has_res:
            dx = refs[3][...] + dx
        dx_ref[...] = dx.astype(dx_ref.dtype)
        part = jnp.sum(dyv * xhat, axis=0, keepdims=True)

        @pl.when(i == 0)
        def _():
            dg_ref[...] = part

        @pl.when(i > 0)
        def _():
            dg_ref[...] += part

    row = pl.BlockSpec((tr, D), lambda i: (i, 0))
    gain = pl.BlockSpec((1, D), lambda i: (0, 0))
    dx, dg = pl.pallas_call(
        body, name=name, grid=(R // tr,),
        in_specs=[row, gain, row] + ([row] if has_res else []), out_specs=[row, gain],
        out_shape=[jax.ShapeDtypeStruct((R, D), out_dtype), jax.ShapeDtypeStruct((1, D), F32)],
        compiler_params=_cparams(("arbitrary",)),
    )(x, g.reshape(1, D), dy, *([res] if has_res else []))
    return dx, dg[0]


def _loss_grad(y, tgt, tr=512):
    R, D = y.shape
    tr = _tile(R, tr, 8)

    def body(y_ref, t_ref, dy_ref, loss_ref):
        i = pl.program_id(0)
        d = y_ref[...] - t_ref[...]
        dy_ref[...] = d / D
        part = 0.5 * jnp.sum(jnp.mean(d * d, axis=-1, keepdims=True), axis=0, keepdims=True)
        tile = jnp.broadcast_to(part, loss_ref.shape)

        @pl.when(i == 0)
        def _():
            loss_ref[...] = tile

        @pl.when(i > 0)
        def _():
            loss_ref[...] += tile

    row = pl.BlockSpec((tr, D), lambda i: (i, 0))
    dy, loss = pl.pallas_call(
        body, name="loss_grad", grid=(R // tr,),
        in_specs=[row, row], out_specs=[row, pl.BlockSpec((8, LANE), lambda i: (0, 0))],
        out_shape=[jax.ShapeDtypeStruct((R, D), F32), jax.ShapeDtypeStruct((8, LANE), F32)],
        compiler_params=_cparams(("arbitrary",)),
    )(y, tgt)
    return loss[0, 0], dy


def _colsum(name, x, tr=512, tn=1024):
    R, N = x.shape
    tr, tn = _tile(R, tr, 8), _tile(N, tn)

    def body(x_ref, o_ref):
        i = pl.program_id(1)
        part = jnp.sum(x_ref[...].astype(F32), axis=0, keepdims=True)

        @pl.when(i == 0)
        def _():
            o_ref[...] = part

        @pl.when(i > 0)
        def _():
            o_ref[...] += part

    out = pl.pallas_call(
        body, name=name, grid=(N // tn, R // tr),
        in_specs=[pl.BlockSpec((tr, tn), lambda j, i: (i, j))], out_specs=pl.BlockSpec((1, tn), lambda j, i: (0, j)),
        out_shape=jax.ShapeDtypeStruct((1, N), F32),
        compiler_params=_cparams(("parallel", "arbitrary")),
    )(x)
    return out[0]


def _tri(tk, rel):
    j = lax.broadcasted_iota(jnp.int32, (tk, tk), 0)
    s = lax.broadcasted_iota(jnp.int32, (tk, tk), 1)
    return rel(j, s).astype(BF16)


def _dot_split(x, m, parts=2):
    total = None
    rem = x
    for _ in range(parts):
        piece = rem.astype(BF16)
        rem = rem - piece.astype(F32)
        term = jnp.dot(piece, m, preferred_element_type=F32)
        total = term if total is None else total + term
    return total


def _log_not_and_beta(z, mask):
    ln = -(jnp.maximum(z, 0.0) + jnp.log(1.0 + jnp.exp(-jnp.abs(z))))
    return (ln if mask is None else jnp.where(mask, ln, 0.0)), ln + z


def _att_tiles(T, Tk, causal):
    tq = min(ATT_TQ, T)
    tk = min(ATT_TK, tq if causal else Tk)
    return tq, tk, (tq if causal else Tk) // tk


def _key_base(j, tq):
    return j * tq if isinstance(j, int) else pl.multiple_of(j * tq, tq)


def _is_pow2(scale):
    return math.log2(scale).is_integer()


def _per_head(x, hpb, d):
    if hpb == 1:
        return [x]
    lane = lax.broadcasted_iota(jnp.int32, x.shape, 1)
    return [jnp.where((lane >= h * d) & (lane < (h + 1) * d), x, jnp.zeros_like(x)) for h in range(hpb)]


def _join_heads(xs, d):
    out = xs[-1]
    if len(xs) > 1:
        lane = lax.broadcasted_iota(jnp.int32, out.shape, 1)
        for h in reversed(range(len(xs) - 1)):
            out = jnp.where(lane < (h + 1) * d, xs[h], out)
    return out


def _lane_tile(rows, off, whole):
    if whole:
        return pl.BlockSpec((rows, LANE), lambda g, i: (0, off + g))
    return pl.BlockSpec((rows, LANE), lambda g, i: (i, off + g))


def _sb_fwd(q, k, v, n_tiles, d, scale):
    T = q[0].shape[0]
    hpb = LANE // d
    tq, tk, nsub = _att_tiles(T, T, True)
    assert _is_pow2(scale)

    def body(q_ref, k_ref, v_ref, ob_ref, rt_ref, acc_ref, r_ref):
        qi = pl.program_id(1)
        qh = _per_head(q_ref[...] * scale, hpb, d)
        acc_ref[...] = jnp.zeros_like(acc_ref)
        r_ref[...] = jnp.zeros_like(r_ref)
        row = lax.broadcasted_iota(jnp.int32, (tq, tk), 0)
        col = lax.broadcasted_iota(jnp.int32, (tq, tk), 1)
        after = _tri(tk, lambda j, s: j > s)

        def walk(h, base, r0, r1, subs, diagonal):
            parts = []
            for u in subs:
                z = _dot(qh[h][r0:r1], k_ref[pl.ds(base + u * tk, tk), :], "nt")
                mask = (col[r0:r1] + u * tk) < row[r0:r1] if diagonal else None
                ln, lb = _log_not_and_beta(z, mask)
                between = _dot_split(ln, after, parts=1)
                first = ln[:, 0:1].astype(BF16).astype(F32)
                parts.append((u, lb, between, between[:, 0:1] + first, mask))
            r = r_ref[h, r0:r1, :]
            out = None
            for u, lb, between, total, mask in parts:
                w = jnp.exp(lb + between + r)
                if diagonal:
                    w = jnp.where(mask, w, 0.0)
                term = _dot(w, v_ref[pl.ds(base + u * tk, tk), :])
                out = term if out is None else out + term
                r = r + total
            acc_ref[h, r0:r1, :] += out
            r_ref[h, r0:r1, :] = r

        def step(j, diagonal):
            base = _key_base(j, tq)
            for h in range(hpb):
                if diagonal and nsub % 2 == 0:
                    walk(h, base, 0, tq // 2, range(nsub // 2 - 1, -1, -1), True)
                    walk(h, base, tq // 2, tq, range(nsub - 1, -1, -1), True)
                else:
                    walk(h, base, 0, tq, range(nsub - 1, -1, -1), diagonal)

        def below(i, carry):
            step(qi - 1 - i, False)
            return carry

        step(qi, True)
        lax.fori_loop(0, qi, below, 0)
        ob_ref[...] = _join_heads([acc_ref[h] for h in range(hpb)], d).astype(ob_ref.dtype)
        rt_ref[...] = r_ref[...]

    out = pl.BlockSpec((tq, LANE), lambda g, i: (i, g))
    col = pl.BlockSpec((hpb, tq, 1), lambda g, i: (g, i, 0))
    return pl.pallas_call(
        body, name="sb_fwd", grid=(n_tiles, T // tq),
        in_specs=[_lane_tile(tq, q[1], False), _lane_tile(T, k[1], True), _lane_tile(T, v[1], True)],
        out_specs=[out, col],
        out_shape=[jax.ShapeDtypeStruct((T, n_tiles * LANE), BF16), jax.ShapeDtypeStruct((n_tiles * hpb, T, 1), F32)],
        scratch_shapes=[pltpu.VMEM((hpb, tq, LANE), F32), pltpu.VMEM((hpb, tq, 1), F32)],
        compiler_params=_cparams(("parallel", "arbitrary")),
    )(q[0], k[0], v[0])


def _sb_bwd(q, k, v, do, rtot, n_tiles, d, scale):
    T = q[0].shape[0]
    hpb = LANE // d
    tq, tk, nsub = _att_tiles(T, T, True)
    assert _is_pow2(scale)

    def body(q_ref, k_ref, v_ref, do_ref, rt_ref, dq_ref, dk_ref, dv_ref, dk_acc, dv_acc, dq_acc, p_ref, c_ref):
        qi = pl.program_id(1)

        @pl.when(qi == 0)
        def _():
            dk_acc[...] = jnp.zeros_like(dk_acc)
            dv_acc[...] = jnp.zeros_like(dv_acc)

        qh = _per_head(q_ref[...] * scale, hpb, d)
        doh = _per_head(do_ref[...], hpb, d)
        dq_acc[...] = jnp.zeros_like(dq_acc)
        p_ref[...] = jnp.zeros_like(p_ref)
        c_ref[...] = jnp.zeros_like(c_ref)
        row = lax.broadcasted_iota(jnp.int32, (tq, tk), 0)
        col = lax.broadcasted_iota(jnp.int32, (tq, tk), 1)
        upto = _tri(tk, lambda j, s: j <= s)
        before = _tri(tk, lambda j, s: j < s)
        rt_wide = [jnp.broadcast_to(rt_ref[h], (tq, tk)) for h in range(hpb)]

        def step(j, diagonal):
            base = _key_base(j, tq)
            for h in range(hpb):
                first = []
                for u in range(nsub):
                    ks = base + u * tk
                    r0 = u * tk if diagonal else 0
                    kv = k_ref[pl.ds(ks, tk), :]
                    z = _dot(qh[h][r0:], kv, "nt")
                    mask = (col[r0:] + u * tk) < row[r0:] if diagonal else None
                    ln, lb = _log_not_and_beta(z, mask)
                    dw = _dot(doh[h][r0:], v_ref[pl.ds(ks, tk), :], "nt")
                    first.append((r0, ks, kv, mask, lb, jnp.exp(lb), _dot_split(ln, upto, parts=1), dw))
                rt, pre, cpre = rt_wide[h], p_ref[h], c_ref[h]
                dq = None
                for r0, ks, kv, mask, lb, sig, local, dw in first:
                    if diagonal and r0:
                        pre, cpre = pre[tk:], cpre[tk:]
                    prefix = local + pre
                    w = jnp.exp(lb + (rt[r0:] - prefix))
                    if diagonal:
                        w = jnp.where(mask, w, 0.0)
                    g = dw * w
                    c = _dot_split(g, before, parts=1) + cpre
                    dz = g * (1.0 - sig) - c * sig
                    if diagonal:
                        dz = jnp.where(mask, dz, 0.0)
                    term = _dot(dz, kv)
                    if diagonal:
                        dq_acc[h, r0:, :] += term
                    else:
                        dq = term if dq is None else dq + term
                    dk_acc[pl.ds(ks, tk), :] += _dot(dz, qh[h][r0:], "tn")
                    dv_acc[pl.ds(ks, tk), :] += _dot(w, doh[h][r0:], "tn")
                    pre = prefix[:, tk - 1:tk]
                    cpre = c[:, tk - 1:tk] + g[:, tk - 1:tk]
                if not diagonal:
                    dq_acc[h] += dq
                    p_ref[h] = pre
                    c_ref[h] = cpre

        def below(j, carry):
            step(j, False)
            return carry

        lax.fori_loop(0, qi, below, 0)
        step(qi, True)
        dq_ref[...] = (_join_heads([dq_acc[h] for h in range(hpb)], d) * scale).astype(dq_ref.dtype)

        @pl.when(qi == pl.num_programs(1) - 1)
        def _():
            dk_ref[...] = dk_acc[...].astype(dk_ref.dtype)
            dv_ref[...] = dv_acc[...].astype(dv_ref.dtype)

    blk = pl.BlockSpec((tq, LANE), lambda g, i: (i, g))
    full = pl.BlockSpec((T, LANE), lambda g, i: (0, g))
    col = pl.BlockSpec((hpb, tq, 1), lambda g, i: (g, i, 0))
    wide = jax.ShapeDtypeStruct((T, n_tiles * LANE), BF16)
    return pl.pallas_call(
        body, name="sb_bwd", grid=(n_tiles, T // tq),
        in_specs=[_lane_tile(tq, q[1], False), _lane_tile(T, k[1], True), _lane_tile(T, v[1], True), blk, col],
        out_specs=[blk, full, full], out_shape=[wide, wide, wide],
        scratch_shapes=[pltpu.VMEM((T, LANE), F32), pltpu.VMEM((T, LANE), F32), pltpu.VMEM((hpb, tq, LANE), F32),
                        pltpu.VMEM((hpb, tq, 1), F32), pltpu.VMEM((hpb, tq, 1), F32)],
        compiler_params=_cparams(("parallel", "arbitrary")),
    )(q[0], k[0], v[0], do, rtot)


def _attn_fwd(name, q, k, v, n_tiles, d, scale, c=None):
    T, Tk = q[0].shape[0], k[0].shape[0]
    hpb = LANE // d
    H = n_tiles * hpb
    causal = c is not None
    tq, tk, nsub = _att_tiles(T, Tk, causal)
    fold = _is_pow2(scale)

    def body(*refs):
        q_ref, k_ref, v_ref = refs[:3]
        cc_ref, cr_ref = refs[3:5] if causal else (None, None)
        o_ref, ob_ref, lse_ref, m_ref, l_ref, acc_ref = refs[-6:]
        qi = pl.program_id(1)
        qh = _per_head(q_ref[...] * scale if fold else q_ref[...], hpb, d)
        bias = [jnp.broadcast_to(cc_ref[h], (tq, tk)) for h in range(hpb)] if causal else None
        ones = jnp.ones((tk, LANE), BF16)
        m_ref[...] = jnp.full_like(m_ref, NEG)
        l_ref[...] = jnp.zeros_like(l_ref)
        acc_ref[...] = jnp.zeros_like(acc_ref)
        row = lax.broadcasted_iota(jnp.int32, (tq, tk), 0)
        col = lax.broadcasted_iota(jnp.int32, (tq, tk), 1)

        def absorb(h, j, base, r0, r1, subs, diagonal):
            zs = []
            for u in subs:
                z = _dot(qh[h][r0:r1], k_ref[pl.ds(base + u * tk, tk), :], "nt")
                if not fold:
                    z = z * scale
                if causal:
                    z = z + bias[h][r0:r1] - cr_ref[h, j * nsub + u]
                if diagonal:
                    z = jnp.where((col[r0:r1] + u * tk) <= row[r0:r1], z, NEG)
                zs.append(z)
            m_prev = m_ref[h, r0:r1, :]
            top = zs[0]
            for z in zs[1:]:
                top = jnp.maximum(top, z)
            m_new = jnp.maximum(m_prev, jnp.max(top, axis=1, keepdims=True))
            alpha = jnp.exp(m_prev - m_new)
            l_new = alpha * l_ref[h, r0:r1, :]
            out = alpha * acc_ref[h, r0:r1, :]
            m_wide = jnp.broadcast_to(m_new, top.shape)
            for u, z in zip(subs, zs):
                p = jnp.exp(z - m_wide).astype(BF16)
                l_new = l_new + jnp.dot(p, ones, preferred_element_type=F32)[:, 0:1]
                out = out + _dot(p, v_ref[pl.ds(base + u * tk, tk), :])
            l_ref[h, r0:r1, :] = l_new
            acc_ref[h, r0:r1, :] = out
            m_ref[h, r0:r1, :] = m_new

        def step(j, diagonal):
            base = _key_base(j, tq)
            for h in range(hpb):
                if diagonal and nsub % 2 == 0:
                    absorb(h, j, base, 0, tq // 2, range(nsub // 2), True)
                    absorb(h, j, base, tq // 2, tq, range(nsub), True)
                else:
                    absorb(h, j, base, 0, tq, range(nsub), diagonal)

        def below(j, carry):
            step(j, False)
            return carry

        if causal:
            lax.fori_loop(0, qi, below, 0)
            step(qi, True)
        else:
            step(0, False)
        o = _join_heads([acc_ref[h] / l_ref[h] for h in range(hpb)], d)
        o_ref[...] = o
        ob_ref[...] = o.astype(ob_ref.dtype)
        lse_ref[...] = m_ref[...] + jnp.log(l_ref[...])

    out = pl.BlockSpec((tq, LANE), lambda g, i: (i, g))
    col = pl.BlockSpec((hpb, tq, 1), lambda g, i: (g, i, 0))
    in_specs = [_lane_tile(tq, q[1], False), _lane_tile(Tk, k[1], True), _lane_tile(Tk, v[1], True)]
    args = [q[0], k[0], v[0]]
    if causal:
        in_specs += [col, pl.BlockSpec((hpb, T // tk, 1, tk), lambda g, i: (g, 0, 0, 0))]
        args += [c.reshape(H, T, 1), c.reshape(H, T // tk, 1, tk)]
    return pl.pallas_call(
        body, name=name, grid=(n_tiles, T // tq),
        in_specs=in_specs, out_specs=[out, out, col],
        out_shape=[jax.ShapeDtypeStruct((T, n_tiles * LANE), F32), jax.ShapeDtypeStruct((T, n_tiles * LANE), BF16),
                   jax.ShapeDtypeStruct((H, T, 1), F32)],
        scratch_shapes=[pltpu.VMEM((hpb, tq, 1), F32), pltpu.VMEM((hpb, tq, 1), F32),
                        pltpu.VMEM((hpb, tq, LANE), F32)],
        compiler_params=_cparams(("parallel", "arbitrary")),
    )(*args)


def _attn_bwd(name, q, k, v, o, do, lse, n_tiles, d, scale, c=None):
    T, Tk = q[0].shape[0], k[0].shape[0]
    hpb = LANE // d
    H = n_tiles * hpb
    causal = c is not None
    tq, tk, nsub = _att_tiles(T, Tk, causal)
    fold = _is_pow2(scale)

    def body(*refs):
        q_ref, k_ref, v_ref, o_ref, do_ref, lse_ref = refs[:6]
        cc_ref, cr_ref = refs[6:8] if causal else (None, None)
        n_out = 5 if causal else 3
        outs = refs[-(n_out + 3):-3]
        dq_ref, dk_ref, dv_ref = outs[:3]
        dc_ref, drow_ref = outs[3:5] if causal else (None, None)
        dk_acc, dv_acc, dq_acc = refs[-3:]
        qi = pl.program_id(1)

        @pl.when(qi == 0)
        def _():
            dk_acc[...] = jnp.zeros_like(dk_acc)
            dv_acc[...] = jnp.zeros_like(dv_acc)
            if causal:
                dc_ref[...] = jnp.zeros_like(dc_ref)

        qh = _per_head(q_ref[...] * scale if fold else q_ref[...], hpb, d)
        doh = _per_head(do_ref[...], hpb, d)
        delta_wide = [jnp.broadcast_to(jnp.sum(t.astype(F32) * o_ref[...], axis=1, keepdims=True), (tq, tk))
                      for t in doh]
        shift = [jnp.broadcast_to((cc_ref[h] - lse_ref[h]) if causal else -lse_ref[h], (tq, tk)) for h in range(hpb)]
        dq_acc[...] = jnp.zeros_like(dq_acc)
        if causal:
            drow_ref[...] = jnp.zeros_like(drow_ref)
        row = lax.broadcasted_iota(jnp.int32, (tq, tk), 0)
        col = lax.broadcasted_iota(jnp.int32, (tq, tk), 1)

        def step(j, diagonal):
            base = _key_base(j, tq)
            for h in range(hpb):
                dq, dsum = None, None
                for u in range(nsub):
                    ks = base + u * tk
                    r0 = u * tk if diagonal else 0
                    kv = k_ref[pl.ds(ks, tk), :]
                    z = _dot(qh[h][r0:], kv, "nt")
                    if not fold:
                        z = z * scale
                    z = z + shift[h][r0:]
                    if causal:
                        z = z - cr_ref[h, j * nsub + u]
                    if diagonal:
                        z = jnp.where((col[r0:] + u * tk) <= row[r0:], z, NEG)
                    p = jnp.exp(z)
                    ds = p * (_dot(doh[h][r0:], v_ref[pl.ds(ks, tk), :], "nt") - delta_wide[h][r0:])
                    term = _dot(ds, kv)
                    dk = _dot(ds, qh[h][r0:], "tn")
                    dk_acc[pl.ds(ks, tk), :] += dk if fold else dk * scale
                    dv_acc[pl.ds(ks, tk), :] += _dot(p, doh[h][r0:], "tn")
                    if causal:
                        dc_ref[h, j * nsub + u] -= jnp.sum(ds, axis=0, keepdims=True)
                    if diagonal:
                        dq_acc[h, r0:, :] += term
                        drow_ref[h, r0:, :] += jnp.sum(ds, axis=1, keepdims=True)
                    else:
                        dq = term if dq is None else dq + term
                        if causal:
                            dsum = ds if dsum is None else dsum + ds
                if not diagonal:
                    dq_acc[h] += dq
                    if causal:
                        drow_ref[h] += jnp.sum(dsum, axis=1, keepdims=True)

        def below(j, carry):
            step(j, False)
            return carry

        if causal:
            lax.fori_loop(0, qi, below, 0)
            step(qi, True)
        else:
            step(0, False)
        dq_ref[...] = (_join_heads([dq_acc[h] for h in range(hpb)], d) * scale).astype(dq_ref.dtype)

        @pl.when(qi == pl.num_programs(1) - 1)
        def _():
            dk_ref[...] = dk_acc[...].astype(dk_ref.dtype)
            dv_ref[...] = dv_acc[...].astype(dv_ref.dtype)

    blk = pl.BlockSpec((tq, LANE), lambda g, i: (i, g))
    full = pl.BlockSpec((Tk, LANE), lambda g, i: (0, g))
    col = pl.BlockSpec((hpb, tq, 1), lambda g, i: (g, i, 0))
    crow = pl.BlockSpec((hpb, T // tk, 1, tk), lambda g, i: (g, 0, 0, 0))
    in_specs = [_lane_tile(tq, q[1], False), _lane_tile(Tk, k[1], True), _lane_tile(Tk, v[1], True), blk, blk, col]
    args = [q[0], k[0], v[0], o, do, lse]
    out_specs = [blk, full, full]
    out_shape = [jax.ShapeDtypeStruct((T, n_tiles * LANE), BF16), jax.ShapeDtypeStruct((Tk, n_tiles * LANE), BF16),
                 jax.ShapeDtypeStruct((Tk, n_tiles * LANE), BF16)]
    if causal:
        in_specs += [col, crow]
        args += [c.reshape(H, T, 1), c.reshape(H, T // tk, 1, tk)]
        out_specs += [crow, col]
        out_shape += [jax.ShapeDtypeStruct((H, T // tk, 1, tk), F32), jax.ShapeDtypeStruct((H, T, 1), F32)]
    outs = pl.pallas_call(
        body, name=name, grid=(n_tiles, T // tq),
        in_specs=in_specs, out_specs=out_specs, out_shape=out_shape,
        scratch_shapes=[pltpu.VMEM((Tk, LANE), F32), pltpu.VMEM((Tk, LANE), F32), pltpu.VMEM((hpb, tq, LANE), F32)],
        compiler_params=_cparams(("parallel", "arbitrary")),
    )(*args)
    if causal:
        return outs[0], outs[1], outs[2], outs[3].reshape(H, T), outs[4].reshape(H, T)
    return outs


def _decay_fwd(fl, b):
    H, T = fl.shape
    tk = DECAY_TK

    def body(x_ref, b_ref, c_ref):
        upto = _tri(tk, lambda j, s: j <= s)
        carry = jnp.zeros((H, 1), F32)
        for i in range(T // tk):
            xv = x_ref[:, i * tk:(i + 1) * tk] + b_ref[...]
            lf = jnp.minimum(xv, 0.0) - jnp.log(1.0 + jnp.exp(-jnp.abs(xv)))
            pref = _dot_split(lf, upto, parts=3) + carry
            c_ref[:, i * tk:(i + 1) * tk] = pref
            carry = pref[:, tk - 1:tk]

    vm = pl.BlockSpec(memory_space=pltpu.VMEM)
    return pl.pallas_call(
        body, name="decay_fwd", in_specs=[vm, vm], out_specs=vm,
        out_shape=jax.ShapeDtypeStruct((H, T), F32),
    )(fl, b)


def _decay_bwd(dc_cols, dc_rows, fl, b):
    H, T = fl.shape
    tk = DECAY_TK

    def body(dc_ref, dr_ref, x_ref, b_ref, dx_ref, db_ref):
        from_ = _tri(tk, lambda j, s: j >= s)
        carry = jnp.zeros((H, 1), F32)
        total = jnp.zeros((H, 1), F32)
        for i in reversed(range(T // tk)):
            sl = slice(i * tk, (i + 1) * tk)
            suffix = _dot_split(dc_ref[:, sl] + dr_ref[:, sl], from_, parts=3) + carry
            xv = x_ref[:, sl] + b_ref[...]
            dx = suffix / (1.0 + jnp.exp(xv))
            dx_ref[:, sl] = dx
            total = total + jnp.sum(dx, axis=1, keepdims=True)
            carry = suffix[:, 0:1]
        db_ref[...] = jnp.broadcast_to(total, db_ref.shape)

    vm = pl.BlockSpec(memory_space=pltpu.VMEM)
    dx, db = pl.pallas_call(
        body, name="decay_bwd", in_specs=[vm, vm, vm, vm], out_specs=[vm, vm],
        out_shape=[jax.ShapeDtypeStruct((H, T), F32), jax.ShapeDtypeStruct((H, LANE), F32)],
    )(dc_cols, dc_rows, fl, b)
    return dx, db[:, 0]


def _place():
    x, y, c = lax.axis_index("x"), lax.axis_index("y"), lax.axis_index("c")
    return x, y, c, [(1 - x, y), (x, 1 - y), (1 - x, 1 - y)]


def _all_gather(name, block):
    R, C = block.shape

    def body(x_ref, out_ref, send_sems, recv_sems, local_sem):
        x, y, c, chips = _place()
        me, sibling = (x, y, c), (x, y, 1 - c)

        def rows(px, py, pc):
            return out_ref.at[4 * px + 2 * py + pc]

        def copy(k, blk, to, src=None):
            return pltpu.make_async_remote_copy(
                src_ref=rows(*blk) if src is None else src, dst_ref=rows(*blk),
                send_sem=send_sems.at[k], recv_sem=recv_sems.at[k], device_id=to, device_id_type=MESH)

        mine = pltpu.make_async_copy(x_ref, rows(*me), local_sem)
        mine.start()
        first = [copy(0, me, sibling, src=x_ref)]
        first += [copy(1 + j, me, (*chip, c), src=x_ref) for j, chip in enumerate(chips)]
        for cp in first:
            cp.start()
        passed = [copy(4 + j, (*chip, c), sibling) for j, chip in enumerate(chips)]
        for j, chip in enumerate(chips):
            copy(1 + j, (*chip, c), me).wait_recv()
            passed[j].start()
        copy(0, sibling, me).wait_recv()
        for j, chip in enumerate(chips):
            copy(4 + j, (*chip, 1 - c), me).wait_recv()
        for cp in first + passed:
            cp.wait_send()
        mine.wait()

    return pl.pallas_call(
        body, name=name, in_specs=[ANY], out_specs=ANY,
        out_shape=jax.ShapeDtypeStruct((N_DEV, R, C), block.dtype),
        scratch_shapes=[pltpu.SemaphoreType.DMA((7,)), pltpu.SemaphoreType.DMA((7,)), pltpu.SemaphoreType.DMA(())],
    )(block)


def _swap_with_sibling(name, parts):
    _, R, C = parts.shape

    def body(p_ref, out_ref, send_sems, recv_sems):
        x, y, c, _ = _place()
        copies = [pltpu.make_async_remote_copy(
            src_ref=p_ref.at[2 * q + (1 - c)], dst_ref=out_ref.at[q],
            send_sem=send_sems.at[q], recv_sem=recv_sems.at[q], device_id=(x, y, 1 - c), device_id_type=MESH)
            for q in range(4)]
        for cp in copies:
            cp.start()
        for cp in copies:
            cp.wait_recv()
        for cp in copies:
            cp.wait_send()

    return pl.pallas_call(
        body, name=name, in_specs=[ANY], out_specs=ANY,
        out_shape=jax.ShapeDtypeStruct((4, R, C), parts.dtype),
        scratch_shapes=[pltpu.SemaphoreType.DMA((4,)), pltpu.SemaphoreType.DMA((4,))],
    )(parts)


def _add_own(name, parts, got, tr=512):
    _, R, C = parts.shape
    tr = _tile(R, tr, SUBLANE_BF16)

    def body(c_ref, p_ref, g_ref, o_ref):
        o_ref[...] = (p_ref[...].astype(F32) + g_ref[...].astype(F32)).astype(o_ref.dtype)

    return pl.pallas_call(
        body, name=name,
        grid_spec=pltpu.PrefetchScalarGridSpec(
            num_scalar_prefetch=1, grid=(4, R // tr),
            in_specs=[pl.BlockSpec((1, tr, C), lambda q, i, c: (2 * q + c[0], i, 0)),
                      pl.BlockSpec((1, tr, C), lambda q, i, c: (q, i, 0))],
            out_specs=pl.BlockSpec((1, tr, C), lambda q, i, c: (q, i, 0))),
        out_shape=jax.ShapeDtypeStruct((4, R, C), parts.dtype),
        compiler_params=_cparams(("parallel", "parallel")),
    )(lax.axis_index("c").astype(jnp.int32).reshape(1), parts, got)


def _swap_with_chips(name, parts):
    _, R, C = parts.shape

    def body(p_ref, out_ref, send_sems, recv_sems, local_sem):
        x, y, c, chips = _place()
        my_chip = 2 * x + y
        mine = pltpu.make_async_copy(p_ref.at[my_chip], out_ref.at[my_chip], local_sem)
        mine.start()
        sends = [pltpu.make_async_remote_copy(
            src_ref=p_ref.at[2 * cx + cy], dst_ref=out_ref.at[my_chip],
            send_sem=send_sems.at[j], recv_sem=recv_sems.at[j], device_id=(cx, cy, c), device_id_type=MESH)
            for j, (cx, cy) in enumerate(chips)]
        for cp in sends:
            cp.start()
        for j, (cx, cy) in enumerate(chips):
            pltpu.make_async_remote_copy(
                src_ref=p_ref.at[my_chip], dst_ref=out_ref.at[2 * cx + cy],
                send_sem=send_sems.at[j], recv_sem=recv_sems.at[j], device_id=(cx, cy, c), device_id_type=MESH,
            ).wait_recv()
        for cp in sends:
            cp.wait_send()
        mine.wait()

    return pl.pallas_call(
        body, name=name, in_specs=[ANY], out_specs=ANY,
        out_shape=jax.ShapeDtypeStruct((4, R, C), parts.dtype),
        scratch_shapes=[pltpu.SemaphoreType.DMA((3,)), pltpu.SemaphoreType.DMA((3,)), pltpu.SemaphoreType.DMA(())],
    )(parts)


def _sum_parts(name, parts, tr=512):
    P, R, C = parts.shape
    tr = _tile(R, tr, SUBLANE_BF16)

    def body(p_ref, o_ref):
        total = p_ref[0].astype(F32)
        for p in range(1, P):
            total = total + p_ref[p].astype(F32)
        o_ref[...] = total

    return pl.pallas_call(
        body, name=name, grid=(R // tr,),
        in_specs=[pl.BlockSpec((P, tr, C), lambda i: (0, i, 0))], out_specs=pl.BlockSpec((tr, C), lambda i: (i, 0)),
        out_shape=jax.ShapeDtypeStruct((R, C), F32),
        compiler_params=_cparams(("parallel",)),
    )(parts)


_HBM = pl.BlockSpec(memory_space=pltpu.HBM)
_SEM = pl.BlockSpec(memory_space=pltpu.SEMAPHORE)
_EFFECT = pltpu.SideEffectType.DATAFLOW_SIDE_EFFECTING


def _flipped(x, y, c, k):
    px, py, pc = (1 - x if k & 4 else x), (1 - y if k & 2 else y), (1 - c if k & 1 else c)
    return (px, py, pc), 4 * px + 2 * py + pc


def _exchange_start(name, src, per_peer):
    R, C = src.shape[-2:]

    def body(v_ref, land_ref, send_sem, recv_sem, v_thru, land_thru, token):
        x, y, c = lax.axis_index("x"), lax.axis_index("y"), lax.axis_index("c")
        me = 4 * x + 2 * y + c
        for k in range(1, N_DEV):
            peer, idx = _flipped(x, y, c, k)
            pltpu.make_async_remote_copy(
                src_ref=v_ref.at[idx] if per_peer else v_ref, dst_ref=land_ref.at[me],
                send_sem=send_sem, recv_sem=recv_sem, device_id=peer, device_id_type=MESH).start()
        token[...] = jnp.zeros_like(token)

    return pl.pallas_call(
        body, name=name,
        out_shape=(pltpu.SemaphoreType.DMA(()), pltpu.SemaphoreType.DMA(()), pltpu.HBM(src.shape, src.dtype),
                   pltpu.HBM((N_DEV, R, C), src.dtype), jax.ShapeDtypeStruct((8, LANE), F32)),
        in_specs=(_HBM, _HBM), out_specs=(_SEM, _SEM, _HBM, _HBM, pl.BlockSpec(memory_space=pltpu.VMEM)),
        input_output_aliases={0: 2, 1: 3},
        compiler_params=pltpu.CompilerParams(has_side_effects=_EFFECT),
    )(pltpu.with_memory_space_constraint(src, pltpu.HBM),
      pltpu.with_memory_space_constraint(lax.empty((N_DEV, R, C), src.dtype), pltpu.HBM))


def _exchange_wait(name, started, after):
    send_sem, recv_sem, v_thru, land_thru, _ = started

    def body(v_ref, land_ref, send_sem, recv_sem, after_ref, v_dead, got_ref):
        x, y, c = lax.axis_index("x"), lax.axis_index("y"), lax.axis_index("c")
        seven = land_ref.at[pl.ds(0, N_DEV - 1)]
        drain = pltpu.make_async_remote_copy(
            src_ref=seven, dst_ref=seven, send_sem=send_sem, recv_sem=recv_sem,
            device_id=(x, y, c), device_id_type=MESH)
        drain.wait_send()
        drain.wait_recv()

    return pl.pallas_call(
        body, name=name,
        out_shape=(pltpu.HBM(v_thru.shape, v_thru.dtype), pltpu.HBM(land_thru.shape, land_thru.dtype)),
        in_specs=(_HBM, _HBM, _SEM, _SEM, ANY), out_specs=(_HBM, _HBM), input_output_aliases={0: 0, 1: 1},
        compiler_params=pltpu.CompilerParams(has_side_effects=_EFFECT),
    )(v_thru, land_thru, send_sem, recv_sem, after)


def _my_index():
    return 4 * lax.axis_index("x") + 2 * lax.axis_index("y") + lax.axis_index("c")


def _sum_landed(name, landed, parts, tr=512):
    P, R, C = landed.shape
    tr = _tile(R, tr, SUBLANE_BF16)

    def body(me_ref, l_ref, own_ref, o_ref):
        total = None
        for s in range(P):
            part = jnp.where(me_ref[0] == s, own_ref[0], l_ref[s]).astype(F32)
            total = part if total is None else total + part
        o_ref[...] = total

    return pl.pallas_call(
        body, name=name,
        grid_spec=pltpu.PrefetchScalarGridSpec(
            num_scalar_prefetch=1, grid=(R // tr,),
            in_specs=[pl.BlockSpec((P, tr, C), lambda i, me: (0, i, 0)),
                      pl.BlockSpec((1, tr, C), lambda i, me: (me[0], i, 0))],
            out_specs=pl.BlockSpec((tr, C), lambda i, me: (i, 0))),
        out_shape=jax.ShapeDtypeStruct((R, C), F32),
        compiler_params=_cparams(("parallel",)),
    )(_my_index().astype(jnp.int32).reshape(1), landed, parts)


def _after(params, name, token):
    return {**params, name: params[name] + token[0, 0]}


def _reduce_scatter(tag, parts):
    got = _swap_with_sibling("rs_pair_" + tag, parts)
    pair = _add_own("rs_add_" + tag, parts, got)
    quad = _swap_with_chips("rs_chips_" + tag, pair)
    return _sum_parts("rs_sum_" + tag, quad)


def _adamw(name, g_parts, w, m, v, tr=512):
    P, R, C = g_parts.shape
    tr = _tile(R, tr, 8)

    def body(g_ref, w_ref, m_ref, v_ref, go_ref, d_ref, mo_ref, vo_ref):
        g = g_ref[0]
        for p in range(1, P):
            g = g + g_ref[p]
        mn = ADAM_B1 * m_ref[...] + (1.0 - ADAM_B1) * g
        vn = ADAM_B2 * v_ref[...] + (1.0 - ADAM_B2) * (g * g)
        m_hat = mn / (1.0 - ADAM_B1 ** ADAM_STEP)
        v_hat = vn / (1.0 - ADAM_B2 ** ADAM_STEP)
        go_ref[...] = g
        d_ref[...] = -ADAM_LR * (m_hat / (jnp.sqrt(v_hat) + ADAM_EPS) + ADAM_WD * w_ref[...])
        mo_ref[...] = mn
        vo_ref[...] = vn

    row = pl.BlockSpec((tr, C), lambda i: (i, 0))
    return pl.pallas_call(
        body, name=name, grid=(R // tr,),
        in_specs=[pl.BlockSpec((P, tr, C), lambda i: (0, i, 0)), row, row, row], out_specs=[row] * 4,
        out_shape=[jax.ShapeDtypeStruct((R, C), F32)] * 4,
        compiler_params=_cparams(("parallel",)),
    )(g_parts, w, m, v)


def _pad_rows(t, rows):
    return jnp.pad(t, ((0, rows - t.shape[0]), (0, 0)))


class _Layout:
    def __init__(self, D, ff_shard, in_shard, kv_shard, gate_shard, br_in, br_shard, out_shard):
        self.D = D
        self.in_shard = in_shard
        self.in_pad = -(-in_shard // LANE) * LANE
        self.in_cols = -(-N_DEV * in_shard // IN_TILE) * IN_TILE
        self.br_in, self.br_shard = br_in, br_shard
        br_rows = br_shard * br_in // D
        sizes = [("g1", ff_shard), ("u1", ff_shard), ("d1", ff_shard), ("win", self.in_pad), ("kv", kv_shard),
                 ("gate", gate_shard), ("br", br_rows), ("out", out_shard),
                 ("g2", ff_shard), ("u2", ff_shard), ("d2", ff_shard)]
        self.seg, off = {}, 0
        for key, n in sizes:
            assert n % SUBLANE_BF16 == 0, (key, n)
            self.seg[key] = (off, n)
            off += n
        self.rows = off

    def pack(self, parts):
        return jnp.concatenate([parts[key] for key in self.seg], axis=0)

    def take(self, gathered, key, own=None):
        off, n = self.seg[key]
        seg = gathered[:, off:off + n, :]
        if own is not None:
            seg = lax.dynamic_update_slice(seg, own[0][off:off + n][None], (own[1], 0, 0))
        return seg.reshape(N_DEV * n, self.D)

    def spread(self, full, key):
        _, n = self.seg[key]
        return full.reshape(N_DEV, n, self.D)


def _pack_layer(lay, l, p):
    D = lay.D
    br = jnp.concatenate([p["w_br_sb"][l], p["w_br_fox"][l], p["w_br_mem"][l]], axis=0)
    parts = {
        "g1": p["ffn1_w_gate"][l].T, "u1": p["ffn1_w_up"][l].T, "d1": p["ffn1_w_down"][l],
        "win": _pad_rows(p["w_in"][l].T, lay.in_pad), "kv": p["w_mem_kv"][l], "gate": p["w_gate"][l].T,
        "br": br.T.reshape(-1, D), "out": p["w_out"][l],
        "g2": p["ffn2_w_gate"][l].T, "u2": p["ffn2_w_up"][l].T, "d2": p["ffn2_w_down"][l],
    }
    return lay.pack({k: t.astype(BF16) for k, t in parts.items()})


def _align_win(lay, packed):
    D = lay.D
    real = packed.reshape(N_DEV, lay.in_pad, D)[:, :lay.in_shard].reshape(N_DEV * lay.in_shard, D)
    rows = jnp.concatenate([real[:_QKV_W], real[_QKV_W + N_FOX_HEADS:], real[_QKV_W:_QKV_W + N_FOX_HEADS]], axis=0)
    return _pad_rows(rows, lay.in_cols)


def _unalign_win(lay, aligned):
    D = lay.D
    n_real = N_DEV * lay.in_shard
    mem_w = n_real - _QKV_W - N_FOX_HEADS
    real = jnp.concatenate([aligned[:_QKV_W], aligned[_QKV_W + mem_w:n_real], aligned[_QKV_W:_QKV_W + mem_w]], axis=0)
    real = real.reshape(N_DEV, lay.in_shard, D)
    return jnp.pad(real, ((0, 0), (0, lay.in_pad - lay.in_shard), (0, 0))).reshape(N_DEV * lay.in_pad, D)


def _unpack_layer(lay, gathered, own=None):
    D = lay.D
    w = {k: lay.take(gathered, k, own) for k in ("g1", "u1", "d1", "kv", "out", "g2", "u2", "d2")}
    w["win"] = _align_win(lay, lay.take(gathered, "win", own))
    fl0 = N_DEV * lay.in_shard - N_FOX_HEADS
    w["wfl"] = w["win"][fl0:fl0 + LANE]
    gate = lay.take(gathered, "gate", own)
    w["gate"] = gate
    w["gate3"] = [gate[i * D:(i + 1) * D] for i in range(3)]
    br = lay.take(gathered, "br", own).reshape(N_DEV * lay.br_shard, lay.br_in)
    third = lay.br_in // 3
    w["br3"] = [br[:, i * third:(i + 1) * third] for i in range(3)]
    return w


def _silu_mul(accs, _):
    a, b = accs
    return [a, b, a * jax.nn.sigmoid(a) * b]


def _act_bwd(accs, extras):
    ds, (a, b) = accs[0], [e.astype(F32) for e in extras]
    sig = jax.nn.sigmoid(a)
    return [ds * b * (sig * (1.0 + a * (1.0 - sig))), ds * (a * sig)]


def _res_norm(scale):
    def epilogue(accs, extras):
        f, res, g = accs[0], extras[0], extras[1]
        out = res + scale * ((f * _rstd(f)) * g)
        return [f, out] + [(out * _rstd(out)) * g_next for g_next in extras[2:]]
    return epilogue


def _down_proj(name, x, w, res, g, scale, next_g):
    D = res.shape[1]
    extras = [(res, 0), (g.reshape(1, D), 0)] + ([(next_g.reshape(1, D), 0)] if next_g is not None else [])
    outs = _mm(name, [(x, w)], "nn", [F32, F32] + [BF16] * (len(extras) - 2), _res_norm(scale), extras, tn=D)
    return outs[0], outs[1], (outs[2] if next_g is not None else None)


def _rms_grad(x, g, dy):
    r = _rstd(x)
    xhat = x * r
    gy = dy * g
    dx = r * (gy - xhat * jnp.mean(gy * xhat, axis=-1, keepdims=True))
    part = jnp.sum(dy * xhat, axis=0, keepdims=True)
    first = lax.broadcasted_iota(jnp.int32, (8, part.shape[1]), 0) == 0
    return dx, jnp.where(first, part, 0.0)


def _norm_bwd(next_scale):
    def epilogue(accs, extras):
        dx, part = _rms_grad(extras[0], extras[2], _sum_accs(accs, None)[0])
        dh = extras[1] + dx
        if len(extras) == 3:
            return [dh, part]
        return [dh, part, *_rms_grad(extras[3], extras[4], next_scale * dh)]
    return epilogue


def _up_grad(name, pairs, x, res, g, nxt, tm=256):
    D = x.shape[1]
    extras = [(x, 0), (res, 0), (g.reshape(1, D), 0)]
    dtypes, rows = [F32, F32], [None, 8]
    if nxt is not None:
        extras += [(nxt[0], 0), (nxt[1].reshape(1, D), 0)]
        dtypes, rows = dtypes + [BF16, F32], rows + [None, 8]
    outs = _mm(name, pairs, "nn", dtypes, _norm_bwd(nxt[2] if nxt is not None else None), extras, tm=tm, tn=D,
               out_rows=rows)
    dg = _colsum(name + "_dg", outs[1])
    if nxt is None:
        return outs[0], dg, None, None
    return outs[0], dg, outs[2], _colsum(name + "_dg2", outs[3])


def _ffn_fwd(tag, h, n, post_g, wg, wu, wd, next_g):
    a, b, s = _mm("ffn_up_" + tag, [(n, wg), (n, wu)], "nt", [BF16, BF16, BF16], _silu_mul, tn=1408)
    f, out, n_next = _down_proj("ffn_down_" + tag, s, wd, h, post_g, 0.5, next_g)
    return out, n_next, (h, n, a, b, s, f)


def _ffn_bwd(tag, dh, df, saved, pre_g, wg, wu, wd, nxt):
    h, n, a, b, s, f = saved
    da, db = _mm("ffn_dact_" + tag, [(df, wd)], "nt", [BF16, BF16], _act_bwd, [(a, 0), (b, 0)], tn=1408)
    d_wd = _mm("ffn_dwd_" + tag, [(s, df)], "tn", [BF16], tm=256)
    dh_in, d_pre, d_next, dg_next = _up_grad("ffn_dn_" + tag, [(da, wg), (db, wu)], h, dh, pre_g, nxt)
    d_wg = _mm("ffn_dwg_" + tag, [(da, n)], "tn", [BF16], tm=256)
    d_wu = _mm("ffn_dwu_" + tag, [(db, n)], "tn", [BF16], tm=256)
    return dh_in, d_pre, d_next, dg_next, d_wg, d_wu, d_wd


_SB_W = N_SB_HEADS * HEAD_DIM
_FOX_W = N_FOX_HEADS * HEAD_DIM
_QKV_W = 3 * _SB_W + 3 * _FOX_W


def _gate_act(accs, extras):
    return [jax.nn.sigmoid(accs[0] + extras[0])]


def _merge(accs, extras):
    g = [e.astype(F32) for e in extras]
    return [g[0] * accs[0] + g[1] * accs[1] + g[2] * accs[2]]


def _merge_bwd(accs, extras):
    dm = accs[0]
    g = [e.astype(F32) for e in extras]
    d_branch = [dm * gi for gi in g]
    d_gate = [dm * bi * gi * (1.0 - gi) for bi, gi in zip(accs[1:], g)]
    return d_branch + d_gate


def _mix_tiles(lay):
    sb, fx = _SB_W // LANE, _FOX_W // LANE
    mem_w = N_DEV * lay.in_shard - _QKV_W - N_FOX_HEADS
    return (0, sb, 2 * sb, sb), (3 * sb, 3 * sb + fx, 3 * sb + 2 * fx, fx), (_QKV_W // LANE, mem_w // LANE)


def _mix_fwd(lay, h, u, w, post_g, b_forget, b_gate, mem_n, next_g):
    D = lay.D
    (sq, sk, sv, sn), (fq, fk, fv, fn), (mq, mn) = _mix_tiles(lay)
    mem_d = mn * LANE // N_MEM_HEADS
    proj = _mm("mix_in", [(u, w["win"])], "nt", [BF16], tm=1024, tn=IN_TILE)
    fl = _mm("mix_fl", [(u, w["wfl"])], "nt", [F32])[:, :N_FOX_HEADS].T
    c = _decay_fwd(fl, b_forget.reshape(-1, 1))
    o_sb, rtot = _sb_fwd((proj, sq), (proj, sk), (proj, sv), sn, HEAD_DIM, HEAD_DIM ** -0.5)
    o_fx32, o_fx, lse_fx = _attn_fwd("fox_fwd", (proj, fq), (proj, fk), (proj, fv), fn, HEAD_DIM,
                                     HEAD_DIM ** -0.5, c)
    kvm = _mm("mem_kv", [(mem_n, w["kv"])], "nn", [BF16])
    o_mem32, o_mem, lse_mem = _attn_fwd("mem_fwd", (proj, mq), (kvm, 0), (kvm, mn), mn, mem_d, mem_d ** -0.5)
    gates = _mm("mix_gate", [(u, w["gate"])], "nt", [BF16], _gate_act, [(b_gate.reshape(1, -1), 0)], tm=1024)
    flat = [o_sb, o_fx, o_mem]
    merged = _mm("mix_merge", list(zip(flat, w["br3"])), "nt", [BF16], _merge,
                 [(gates, 0), (gates, D), (gates, 2 * D)])
    z, out, n_next = _down_proj("mix_out", merged, w["out"], h, post_g, 1.0, next_g)
    saved = (h, u, proj, fl, c, rtot, o_fx32, lse_fx, kvm, o_mem32, lse_mem, gates, flat, merged, z)
    return out, n_next, saved


def _mix_bwd(lay, dh, dz, saved, w, pre_g, b_forget, mem_n, dmem_n, nxt):
    D = lay.D
    (sq, sk, sv, sn), (fq, fk, fv, fn), (mq, mn) = _mix_tiles(lay)
    mem_d = mn * LANE // N_MEM_HEADS
    h, u, proj, fl, c, rtot, o_fx32, lse_fx, kvm, o_mem32, lse_mem, gates, flat, merged, z = saved
    outs = _mm("mix_dmerge", [(dz, w["out"])] + list(zip(flat, w["br3"])), "nt", [BF16] * 6, _merge_bwd,
               [(gates, 0), (gates, D), (gates, 2 * D)], tn=512)
    d_branch, d_gate = outs[:3], outs[3:]
    d_wout = _mm("mix_dwout", [(merged, dz)], "tn", [BF16])
    d_o = [_mm("mix_dbr%d" % i, [(d_branch[i], w["br3"][i])], "nn", [BF16]) for i in range(3)]
    d_wbr = [_mm("mix_dwbr%d" % i, [(d_branch[i], flat[i])], "tn", [BF16]) for i in range(3)]
    d_bgate = jnp.concatenate([_colsum("mix_dbgate%d" % i, d_gate[i]) for i in range(3)])
    d_wgate = [_mm("mix_dwgate%d" % i, [(d_gate[i], u)], "tn", [BF16]) for i in range(3)]

    d_sb = _sb_bwd((proj, sq), (proj, sk), (proj, sv), d_o[0], rtot, sn, HEAD_DIM, HEAD_DIM ** -0.5)
    *d_fx, dc, dc_rows = _attn_bwd("fox_bwd", (proj, fq), (proj, fk), (proj, fv), o_fx32, d_o[1], lse_fx, fn,
                                   HEAD_DIM, HEAD_DIM ** -0.5, c)
    dq_m, dk_m, dv_m = _attn_bwd("mem_bwd", (proj, mq), (kvm, 0), (kvm, mn), o_mem32, d_o[2], lse_mem, mn,
                                 mem_d, mem_d ** -0.5)
    dfl, d_bforget = _decay_bwd(dc, dc_rows, fl, b_forget.reshape(-1, 1))
    pieces = list(d_sb) + list(d_fx) + [dq_m]
    dflp = jnp.pad(dfl.T.astype(BF16), ((0, 0), (0, LANE - dfl.shape[0])))
    offs = [sum(t.shape[1] for t in pieces[:i]) for i in range(len(pieces) + 1)]
    win_rows = [w["win"][offs[i]:offs[i + 1]] for i in range(len(pieces))]
    dh_in, d_pre, d_next, dg_next = _up_grad(
        "mix_du", list(zip(d_gate, w["gate3"])) + list(zip(pieces, win_rows)) + [(dflp, w["wfl"])], h, dh, pre_g, nxt)
    d_rows = [_mm("mix_dwin%d" % i, [(t, u)], "tn", [BF16]) for i, t in enumerate(pieces)]
    d_wfl = _mm("mix_dwfl", [(dflp, u)], "tn", [BF16])
    d_win = _unalign_win(lay, _pad_rows(jnp.concatenate(list(d_rows) + [d_wfl], axis=0), lay.in_cols))

    dkvm = jnp.concatenate([dk_m, dv_m], axis=1)
    d_wkv = _mm("mem_dwkv", [(mem_n, dkvm)], "tn", [BF16])
    dmem_n = _mm("mem_dn", [(dkvm, w["kv"])], "nt", [F32], lambda accs, ex: [accs[0] + ex[0]], [(dmem_n, 0)])
    grads = {"win": d_win, "kv": d_wkv, "gate": jnp.concatenate(d_wgate, axis=0),
             "br": jnp.concatenate(d_wbr, axis=1), "out": d_wout}
    return dh_in, d_pre, d_next, dg_next, d_bforget, d_bgate, grads, dmem_n


def _layer_fwd(lay, h, n, w, sp, mem_n, next_g):
    h1, u, s1 = _ffn_fwd("1", h, n, sp["ffn1_post_g"], w["g1"], w["u1"], w["d1"], sp["mix_pre_g"])
    h2, n2, s2 = _mix_fwd(lay, h1, u, w, sp["mix_post_g"], sp["b_forget"], sp["b_gate"], mem_n, sp["ffn2_pre_g"])
    h3, n_next, s3 = _ffn_fwd("2", h2, n2, sp["ffn2_post_g"], w["g2"], w["u2"], w["d2"], next_g)
    return h3, n_next, (s1, s2, s3)


_FFN1_SEGS = 3


def _layer_bwd(lay, dh, df2, d_post2, saved, w, sp, mem_n, dmem_n, below, early=None):
    s1, s2, s3 = saved
    dh, d_pre2, dz, d_mpost, d_g2, d_u2, d_d2 = _ffn_bwd(
        "2", dh, df2, s3, sp["ffn2_pre_g"], w["g2"], w["u2"], w["d2"], (s2[-1], sp["mix_post_g"], 1.0))
    dh, d_mpre, df1, d_post1, d_bforget, d_bgate, g, dmem_n = _mix_bwd(
        lay, dh, dz, s2, w, sp["mix_pre_g"], sp["b_forget"], mem_n, dmem_n, (s1[5], sp["ffn1_post_g"], 0.5))
    g.update({"g2": d_g2, "u2": d_u2, "d2": d_d2})
    g["br"] = g["br"].reshape(N_DEV, lay.br_shard, lay.br_in).reshape(-1, lay.D)
    keys = list(lay.seg)
    if early is not None:
        token = early(jnp.concatenate([lay.spread(g[key], key) for key in keys[_FFN1_SEGS:]], axis=1))
        sp, keys = _after(sp, "ffn1_pre_g", token), keys[:_FFN1_SEGS]
    dh, d_pre1, df_below, d_post_below, d_g1, d_u1, d_d1 = _ffn_bwd(
        "1", dh, df1, s1, sp["ffn1_pre_g"], w["g1"], w["u1"], w["d1"], below)
    g.update({"g1": d_g1, "u1": d_u1, "d1": d_d1})
    packed = jnp.concatenate([lay.spread(g[key], key) for key in keys], axis=1)
    small = {"ffn1_pre_g": d_pre1, "ffn1_post_g": d_post1, "mix_pre_g": d_mpre, "mix_post_g": d_mpost,
             "ffn2_pre_g": d_pre2, "ffn2_post_g": d_post2, "b_gate": d_bgate, "b_forget": d_bforget}
    return dh, df_below, d_post_below, packed, small, dmem_n


_SHARDED = ["ffn1_w_gate", "ffn1_w_up", "ffn1_w_down", "w_in", "w_mem_kv", "w_gate", "w_br_sb", "w_br_fox",
            "w_br_mem", "w_out", "ffn2_w_gate", "ffn2_w_up", "ffn2_w_down"]
_SMALL_LAYER = ["ffn1_pre_g", "ffn1_post_g", "mix_pre_g", "mix_post_g", "ffn2_pre_g", "ffn2_post_g", "b_gate",
                "b_forget"]
_WEIGHTS = ["ffn1_pre_g", "ffn1_post_g", "ffn1_w_gate", "ffn1_w_up", "ffn1_w_down", "mix_pre_g", "mix_post_g",
            "w_in", "b_forget", "mem_norm_g", "w_mem_kv", "w_gate", "b_gate", "w_br_sb", "w_br_fox", "w_br_mem",
            "w_out", "ffn2_pre_g", "ffn2_post_g", "ffn2_w_gate", "ffn2_w_up", "ffn2_w_down"]


def _pack_small(vals, L, D):
    rows = []
    for l in range(L):
        for name in _SMALL_LAYER:
            t = vals[name][l]
            rows.append(jnp.pad(t, (0, -t.shape[0] % D)).reshape(-1, D))
    rows.append(vals["mem_norm_g"].reshape(1, D))
    packed = jnp.concatenate(rows, axis=0)
    return _pad_rows(packed, -(-packed.shape[0] // 8) * 8)


def _unpack_small(packed, shapes, L, D):
    out = {name: [] for name in _SMALL_LAYER}
    r = 0
    for l in range(L):
        for name in _SMALL_LAYER:
            n = shapes[name][1]
            nr = -(-n // D)
            out[name].append(packed[r:r + nr].reshape(-1)[:n])
            r += nr
    res = {name: jnp.stack(v) for name, v in out.items()}
    res["mem_norm_g"] = packed[r]
    return res


def _unpack_grads(lay, g):
    def seg(key):
        off, n = lay.seg[key]
        return g[off:off + n]
    br = seg("br").reshape(lay.br_shard, lay.br_in).T
    third = lay.br_in // 3
    return {
        "ffn1_w_gate": seg("g1").T, "ffn1_w_up": seg("u1").T, "ffn1_w_down": seg("d1"),
        "w_in": seg("win")[:lay.in_shard].T, "w_mem_kv": seg("kv"), "w_gate": seg("gate").T,
        "w_br_sb": br[:third], "w_br_fox": br[third:2 * third], "w_br_mem": br[2 * third:],
        "w_out": seg("out"), "ffn2_w_gate": seg("g2").T, "ffn2_w_up": seg("u2").T, "ffn2_w_down": seg("d2"),
    }


class _Exchanges:
    def gather(self, name, block):
        return _all_gather(name, block)

    def gather_start(self, block):
        return _exchange_start("ag_start", block, per_peer=False)

    def gather_wait(self, started, after):
        block, landed = _exchange_wait("ag_wait", started, after)
        return landed, (block, _my_index())

    def scatter(self, parts):
        return _reduce_scatter("w", parts)

    def scatter_start(self, parts):
        return _exchange_start("rs_start", parts, per_peer=True)

    def scatter_wait(self, started, after):
        parts, landed = _exchange_wait("rs_wait", started, after)
        return _sum_landed("rs_sum8", landed, parts)

    def token(self, started):
        return started[4]

    def loss_sum(self, part):
        return lax.psum(part, ("x", "y", "c"))


def _step(p, m, v, x, mem, tgt, ex):
    L, D = p["ffn1_pre_g"].shape
    lay = _Layout(D, p["ffn1_w_gate"].shape[2], p["w_in"].shape[2], p["w_mem_kv"].shape[1], p["w_gate"].shape[2],
                  3 * p["w_br_sb"].shape[1], p["w_br_sb"].shape[2], p["w_out"].shape[1])
    blocks = [_pack_layer(lay, l, p) for l in range(L)]
    sps = [{name: p[name][l] for name in _SMALL_LAYER} for l in range(L)]

    mem_n = _rms_fwd("mem_norm", mem, p["mem_norm_g"], BF16)
    gathered, own = ex.gather("ag_weights", blocks[0]), None
    h, saved, ws = x, [], []
    n = _rms_fwd("first_norm", x, sps[0]["ffn1_pre_g"], BF16)
    for l in range(L):
        if l + 1 < L:
            nxt, gathered = lax.optimization_barrier((blocks[l + 1], gathered))
            started = ex.gather_start(nxt)
            sp = _after(sps[l], "ffn1_post_g", ex.token(started))
        else:
            sp = sps[l]
        ws.append(_unpack_layer(lay, gathered, own))
        h, n, s = _layer_fwd(lay, h, n, ws[l], sp, mem_n, sps[l + 1]["ffn1_pre_g"] if l + 1 < L else None)
        saved.append(s)
        if l + 1 < L:
            gathered, own = ex.gather_wait(started, h)
    loss_part, dh = _loss_grad(h, tgt)
    loss = ex.loss_sum(loss_part)

    dmem_n = jnp.zeros(mem.shape, F32)
    big, small = [None] * L, {name: [None] * L for name in _SMALL_LAYER}
    flying, token = {}, None
    df, d_post = _rms_bwd("last_dout", saved[L - 1][2][5], sps[L - 1]["ffn2_post_g"], dh, BF16, scale=0.5)
    for l in reversed(range(L)):
        sp = sps[l] if token is None else _after(sps[l], "ffn2_pre_g", token)
        below = (saved[l - 1][2][5], sps[l - 1]["ffn2_post_g"], 0.5) if l > 0 else None
        first = []

        def early(part):
            first.append(ex.scatter_start(part))
            return ex.token(first[0])

        dh, df, d_post, packed, sm, dmem_n = _layer_bwd(lay, dh, df, d_post, saved[l], ws[l], sp, mem_n, dmem_n,
                                                         below, early if l == 0 else None)
        if l > 0:
            flying[l] = ex.scatter_start(packed)
            token = ex.token(flying[l])
        else:
            last = ex.scatter_start(packed)
        for name in _SMALL_LAYER:
            small[name][l] = sm[name]
    for l, started in flying.items():
        big[l] = _unpack_grads(lay, ex.scatter_wait(started, ex.token(last)))
    rest = ex.scatter_wait(first[0], ex.token(last))
    big[0] = _unpack_grads(lay, jnp.concatenate([ex.scatter_wait(last, rest), rest], axis=0))
    _, d_memg = _rms_bwd("mem_dnorm", mem, p["mem_norm_g"], dmem_n, F32)

    small_g = {name: jnp.stack(vs) for name, vs in small.items()}
    small_g["mem_norm_g"] = d_memg
    small_names = _SMALL_LAYER + ["mem_norm_g"]
    shapes = {name: p[name].shape for name in small_names}
    g_all = ex.gather("ag_small", _pack_small(small_g, L, D))
    packs = [_pack_small({name: t[name] for name in small_names}, L, D) for t in (p, m, v)]
    res = [_unpack_small(t, shapes, L, D) for t in _adamw("adamw_small", g_all, *packs)]

    out = {kind: {} for kind in ("grad", "delta", "new_m", "new_v")}
    for name in small_names:
        for kind, r in zip(("grad", "delta", "new_m", "new_v"), res):
            out[kind][name] = r[name].reshape(p[name].shape)
    for name in _SHARDED:
        g = jnp.stack([big[l][name] for l in range(L)])
        shp = g.shape
        flat = lambda t: t.reshape(-1, shp[-1])
        r = _adamw("adamw_" + name, flat(g)[None], flat(p[name]), flat(m[name]), flat(v[name]))
        for kind, t in zip(("grad", "delta", "new_m", "new_v"), r):
            out[kind][name] = t.reshape(shp)
    return loss, dh, out


def kernel(x, mem, ffn1_pre_g, ffn1_post_g, ffn1_w_gate, ffn1_w_up, ffn1_w_down, mix_pre_g, mix_post_g, w_in, b_forget, mem_norm_g, w_mem_kv, w_gate, b_gate, w_br_sb, w_br_fox, w_br_mem, w_out, ffn2_pre_g, ffn2_post_g, ffn2_w_gate, ffn2_w_up, ffn2_w_down, loss_target, m_ffn1_pre_g, m_ffn1_post_g, m_ffn1_w_gate, m_ffn1_w_up, m_ffn1_w_down, m_mix_pre_g, m_mix_post_g, m_w_in, m_b_forget, m_mem_norm_g, m_w_mem_kv, m_w_gate, m_b_gate, m_w_br_sb, m_w_br_fox, m_w_br_mem, m_w_out, m_ffn2_pre_g, m_ffn2_post_g, m_ffn2_w_gate, m_ffn2_w_up, m_ffn2_w_down, v_ffn1_pre_g, v_ffn1_post_g, v_ffn1_w_gate, v_ffn1_w_up, v_ffn1_w_down, v_mix_pre_g, v_mix_post_g, v_w_in, v_b_forget, v_mem_norm_g, v_w_mem_kv, v_w_gate, v_b_gate, v_w_br_sb, v_w_br_fox, v_w_br_mem, v_w_out, v_ffn2_pre_g, v_ffn2_post_g, v_ffn2_w_gate, v_ffn2_w_up, v_ffn2_w_down):
    p = dict(zip(_WEIGHTS, (ffn1_pre_g, ffn1_post_g, ffn1_w_gate, ffn1_w_up, ffn1_w_down, mix_pre_g, mix_post_g, w_in, b_forget, mem_norm_g, w_mem_kv, w_gate, b_gate, w_br_sb, w_br_fox, w_br_mem, w_out, ffn2_pre_g, ffn2_post_g, ffn2_w_gate, ffn2_w_up, ffn2_w_down)))
    m = dict(zip(_WEIGHTS, (m_ffn1_pre_g, m_ffn1_post_g, m_ffn1_w_gate, m_ffn1_w_up, m_ffn1_w_down, m_mix_pre_g, m_mix_post_g, m_w_in, m_b_forget, m_mem_norm_g, m_w_mem_kv, m_w_gate, m_b_gate, m_w_br_sb, m_w_br_fox, m_w_br_mem, m_w_out, m_ffn2_pre_g, m_ffn2_post_g, m_ffn2_w_gate, m_ffn2_w_up, m_ffn2_w_down)))
    v = dict(zip(_WEIGHTS, (v_ffn1_pre_g, v_ffn1_post_g, v_ffn1_w_gate, v_ffn1_w_up, v_ffn1_w_down, v_mix_pre_g, v_mix_post_g, v_w_in, v_b_forget, v_mem_norm_g, v_w_mem_kv, v_w_gate, v_b_gate, v_w_br_sb, v_w_br_fox, v_w_br_mem, v_w_out, v_ffn2_pre_g, v_ffn2_post_g, v_ffn2_w_gate, v_ffn2_w_up, v_ffn2_w_down)))
    loss, dx, out = _step(p, m, v, x[0], mem[0], loss_target[0], _Exchanges())
    return (loss, dx[None], *[out["grad"][n] for n in _WEIGHTS], *[out["delta"][n] for n in _WEIGHTS],
            *[out["new_m"][n] for n in _WEIGHTS], *[out["new_v"][n] for n in _WEIGHTS])
```
